```python
import jax, jax.numpy as jnp
from jax import lax
import numpy as np

D_MODEL = 2048
BATCH = 8
SEQ = 4096
DEPTH = 2

GRID_W = 64
CTX_LEN = 256
D_MIX = D_MODEL
RET_HEADS = 4
RET_DK = 128
RET_DV = 256
RET_QK_W = RET_HEADS * RET_DK
RET_W = RET_HEADS * RET_DV
RET_CHUNK = 128
CONV_W = 512
CONV_K = 31
NA_HEADS = 4
NA_DH = 128
NA_W = NA_HEADS * NA_DH
NA_ROWS = 8
NA_COLS = 16
D_FF = 5632
FFN_K = 3
D_IN = 2 * RET_QK_W + 2 * RET_W + 2 * CONV_W + 3 * NA_W
ROPE_BASE = 10000.0
EPS = 1e-6

kernel_name = "hybrid_retention_conformer_natten_dit"


def _rmsnorm(x, g):
    xf = x.astype(jnp.float32)
    y = xf * lax.rsqrt(jnp.mean(xf * xf, axis=-1, keepdims=True) + EPS)
    return (y * g.astype(jnp.float32)).astype(x.dtype)


def _layernorm(x, g, b):
    xf = x.astype(jnp.float32)
    mu = jnp.mean(xf, axis=-1, keepdims=True)
    var = jnp.mean(jnp.square(xf - mu), axis=-1, keepdims=True)
    y = (xf - mu) * lax.rsqrt(var + EPS)
    return (y * g.astype(jnp.float32) + b.astype(jnp.float32)).astype(x.dtype)


def _adaln(cond, w, b):
    m = jax.nn.silu(cond) @ w + b
    return jnp.split(m, 6, axis=-1)


def _modulate(h, shift, scale):
    return h * (1 + scale) + shift


def _depthwise_conv(x, w, b):
    y = lax.conv_general_dilated(x, w[:, None, :].astype(x.dtype), window_strides=(1,), padding="SAME",
                                 dimension_numbers=("NWC", "WIO", "NWC"), feature_group_count=x.shape[-1])
    return y + b.astype(x.dtype)


def _axial_rope(t):
    L, dh = t.shape[1], t.shape[-1]
    half = dh // 2
    nf = half // 2
    pos = jnp.arange(L)
    row = (pos // GRID_W).astype(jnp.float32)
    col = (pos % GRID_W).astype(jnp.float32)
    inv = ROPE_BASE ** (-jnp.arange(nf, dtype=jnp.float32) / nf)

    def rot(xa, p):
        ang = p[:, None] * inv[None, :]
        cos = jnp.cos(ang)[None, :, None, :]
        sin = jnp.sin(ang)[None, :, None, :]
        x1, x2 = xa[..., :nf], xa[..., nf:]
        return jnp.concatenate([x1 * cos - x2 * sin, x2 * cos + x1 * sin], axis=-1)

    return jnp.concatenate([rot(t[..., :half], row), rot(t[..., half:], col)], axis=-1)


def _retention_chunkwise(q, k, v, log_gamma, state0):
    b, L, h, _ = q.shape
    dv = v.shape[-1]
    n = L // RET_CHUNK

    def chunks(t):
        return t.reshape(b, n, RET_CHUNK, h, t.shape[-1]).transpose(1, 0, 3, 2, 4)

    idx = jnp.arange(RET_CHUNK, dtype=jnp.float32)
    diff = idx[:, None] - idx[None, :]
    lower = diff >= 0
    decay_in = jnp.where(lower, jnp.exp(jnp.where(lower, diff, 0.0) * log_gamma[:, None, None]), 0.0)
    xi = jnp.exp((idx + 1.0) * log_gamma[:, None])[None, :, :, None]
    zeta = jnp.exp((RET_CHUNK - 1.0 - idx) * log_gamma[:, None])[None, :, :, None]
    g_chunk = jnp.exp(RET_CHUNK * log_gamma)[None, :, None, None]

    def step(state, blk):
        qc, kc, vc = blk
        inner = jnp.einsum("bhid,bhjd->bhij", qc, kc) * decay_in[None]
        out = jnp.einsum("bhij,bhjv->bhiv", inner, vc) + jnp.einsum("bhid,bhdv->bhiv", qc, state) * xi
        state = state * g_chunk + jnp.einsum("bhjd,bhjv->bhdv", kc * zeta, vc)
        return state, out

    state, out = lax.scan(step, state0, (chunks(q), chunks(k), chunks(v)))
    return out.transpose(1, 0, 3, 2, 4).reshape(b, L, h, dv), state


def _bidirectional_retention(q_l, k_l, v_l, q_c, k_c, v_c, log_gamma):
    b = q_l.shape[0]
    zeros = jnp.zeros((b, RET_HEADS, RET_DK, RET_DV), jnp.float32)
    flip = lambda t: jnp.flip(t, axis=1)
    oc_f, s_f = _retention_chunkwise(q_c, k_c, v_c, log_gamma[0], zeros)
    ol_f, _ = _retention_chunkwise(q_l, k_l, v_l, log_gamma[0], s_f)
    oc_b, s_b = _retention_chunkwise(flip(q_c), flip(k_c), flip(v_c), log_gamma[1], zeros)
    ol_b, _ = _retention_chunkwise(flip(q_l), flip(k_l), flip(v_l), log_gamma[1], s_b)
    return ol_f + flip(ol_b), oc_f + flip(oc_b)


def _gated_group_norm(o, gate, g):
    b, L, h, dv = o.shape
    mu = jnp.mean(o, axis=-1, keepdims=True)
    var = jnp.mean(jnp.square(o - mu), axis=-1, keepdims=True)
    y = ((o - mu) * lax.rsqrt(var + EPS)).reshape(b, L, h * dv) * g.astype(jnp.float32)
    return y.astype(gate.dtype) * jax.nn.silu(gate)


def _retention_group(lq, lk, lv, lg, cq, ck, cv, cg, decay_logits, gn_g, with_ctx):
    def heads(t, d):
        return t.astype(jnp.float32).reshape(t.shape[0], t.shape[1], RET_HEADS, d)

    scale = RET_DK ** -0.5
    q_l = _axial_rope(heads(lq, RET_DK)) * scale
    k_l = _axial_rope(heads(lk, RET_DK))
    q_c = heads(cq, RET_DK) * scale
    k_c = heads(ck, RET_DK)
    log_gamma = jax.nn.log_sigmoid(decay_logits.astype(jnp.float32))
    o_l, o_c = _bidirectional_retention(q_l, k_l, heads(lv, RET_DV), q_c, k_c, heads(cv, RET_DV), log_gamma)
    out_l = _gated_group_norm(o_l, lg, gn_g)
    out_c = _gated_group_norm(o_c, cg, gn_g) if with_ctx else None
    return out_l, out_c


def _conv_group(a, b, dw_w, dw_b, ln_g, ln_b, pw):
    u = a * jax.nn.sigmoid(b)
    u = _depthwise_conv(u, dw_w, dw_b)
    u = jax.nn.silu(_layernorm(u, ln_g, ln_b))
    return u @ pw


def _na_latent(q, k, v, k_c, v_c, rpb):
    b, L, h, d = q.shape
    rows_n = L // GRID_W
    kh = min(NA_ROWS, rows_n)
    rows = jnp.arange(rows_n)
    cols = jnp.arange(GRID_W)
    key_rows = jnp.clip(rows - kh // 2, 0, rows_n - kh)[:, None] + jnp.arange(kh)[None, :]
    col_start = jnp.clip(cols - NA_COLS // 2, 0, GRID_W - NA_COLS)
    col_in = (cols[None, :] >= col_start[:, None]) & (cols[None, :] < col_start[:, None] + NA_COLS)
    row_off = key_rows - rows[:, None] + NA_ROWS - 1
    col_off = jnp.clip(cols[None, :] - cols[:, None] + NA_COLS - 1, 0, 2 * NA_COLS - 2)
    bias = rpb.astype(jnp.float32)[:, row_off[:, None, :, None], col_off[None, :, None, :]]
    qg = q.reshape(b, rows_n, GRID_W, h, d) * (NA_DH ** -0.5)
    kg = k.reshape(b, rows_n, GRID_W, h, d)[:, key_rows]
    vg = v.reshape(b, rows_n, GRID_W, h, d)[:, key_rows]
    s_lat = jnp.einsum("brqhd,brkwhd->bhrqkw", qg, kg).astype(jnp.float32) + bias[None]
    s_lat = jnp.where(col_in[:, None, :], s_lat, -jnp.inf)
    s_ctx = jnp.einsum("brqhd,bchd->bhrqc", qg, k_c).astype(jnp.float32)
    n_lat = kh * GRID_W
    p = jax.nn.softmax(jnp.concatenate([s_lat.reshape(b, h, rows_n, GRID_W, n_lat), s_ctx], axis=-1), axis=-1)
    p = p.astype(v.dtype)
    p_lat = p[..., :n_lat].reshape(b, h, rows_n, GRID_W, kh, GRID_W)
    out = jnp.einsum("bhrqkw,brkwhd->brqhd", p_lat, vg) + jnp.einsum("bhrqc,bchd->brqhd", p[..., n_lat:], v_c)
    return out.reshape(b, L, h * d)


def _na_context(q_c, k_c, v_c):
    b, lc, h, d = q_c.shape
    s = jnp.einsum("bqhd,bkhd->bhqk", q_c * (NA_DH ** -0.5), k_c).astype(jnp.float32)
    p = jax.nn.softmax(s, axis=-1).astype(v_c.dtype)
    return jnp.einsum("bhqk,bkhd->bqhd", p, v_c).reshape(b, lc, h * d)


def _token_mixers(p_l, p_c, decay_logits, gn_g, dw_w, dw_b, ln_g, ln_b, pw, rpb, with_ctx):
    sizes = [RET_QK_W, RET_QK_W, RET_W, RET_W, CONV_W, CONV_W, NA_W, NA_W, NA_W]
    cuts = [int(s) for s in np.cumsum(sizes)[:-1]]
    lq, lk, lv, lg, la, lb, nq, nk, nv = jnp.split(p_l, cuts, axis=-1)
    cq, ck, cv, cg, ca, cb, cnq, cnk, cnv = jnp.split(p_c, cuts, axis=-1)

    def na_heads(t):
        return t.reshape(t.shape[0], t.shape[1], NA_HEADS, NA_DH)

    ret_l, ret_c = _retention_group(lq, lk, lv, lg, cq, ck, cv, cg, decay_logits, gn_g, with_ctx)
    conv_l = _conv_group(la, lb, dw_w, dw_b, ln_g, ln_b, pw)
    k_c, v_c = na_heads(cnk), na_heads(cnv)
    na_l = _na_latent(na_heads(nq), na_heads(nk), na_heads(nv), k_c, v_c, rpb)
    out_l = jnp.concatenate([ret_l, conv_l, na_l], axis=-1)
    if not with_ctx:
        return out_l, None
    conv_c = _conv_group(ca, cb, dw_w, dw_b, ln_g, ln_b, pw)
    na_c = _na_context(na_heads(cnq), k_c, v_c)
    out_c = jnp.concatenate([ret_c, conv_c, na_c], axis=-1)
    return out_l, out_c


def _conv_ffn(h, up, dw_w, dw_b, down):
    u = _depthwise_conv(h @ up, dw_w, dw_b)
    val, gate = jnp.split(u, 2, axis=-1)
    return (jax.nn.silu(gate) * val) @ down


def _fwd_setup_inputs(seed: int = 0) -> dict:
    key = jax.random.key(seed)
    ks = jax.random.split(key, 24)
    f32 = jnp.float32

    def nrm(k, shape, scale):
        return jax.random.normal(k, shape, f32) * scale

    base_decay = np.log(2.0 ** (5 + np.arange(RET_HEADS)) - 1.0)
    return {
        "x": nrm(ks[0], (BATCH, SEQ, D_MODEL), 1.0),
        "c": nrm(ks[1], (BATCH, D_MODEL), 1.0),
        "ctx": nrm(ks[2], (BATCH, CTX_LEN, D_MODEL), 1.0),
        "c_ctx": nrm(ks[3], (D_MODEL,), 1.0),
        "w_ada": nrm(ks[4], (DEPTH, D_MODEL, 6 * D_MODEL), 0.5 * D_MODEL ** -0.5),
        "b_ada": nrm(ks[5], (DEPTH, 6 * D_MODEL), 0.02),
        "norm1_g": 1.0 + nrm(ks[6], (DEPTH, D_MODEL), 0.02),
        "w_in": nrm(ks[7], (DEPTH, D_MODEL, D_IN), D_MODEL ** -0.5),
        "ret_decay": jnp.asarray(base_decay, f32)[None, None, :] + nrm(ks[8], (DEPTH, 2, RET_HEADS), 0.05),
        "ret_gn_g": 1.0 + nrm(ks[9], (DEPTH, RET_W), 0.02),
        "conv_dw_w": nrm(ks[10], (DEPTH, CONV_K, CONV_W), CONV_K ** -0.5),
        "conv_dw_b": nrm(ks[11], (DEPTH, CONV_W), 0.02),
        "conv_ln_g": 1.0 + nrm(ks[12], (DEPTH, CONV_W), 0.02),
        "conv_ln_b": nrm(ks[13], (DEPTH, CONV_W), 0.02),
        "conv_pw": nrm(ks[14], (DEPTH, CONV_W, CONV_W), CONV_W ** -0.5),
        "na_rpb": nrm(ks[15], (DEPTH, NA_HEADS, 2 * NA_ROWS - 1, 2 * NA_COLS - 1), 0.05),
        "w_out": nrm(ks[16], (DEPTH, D_MIX, D_MODEL), D_MIX ** -0.5),
        "norm2_g": 1.0 + nrm(ks[17], (DEPTH, D_MODEL), 0.02),
        "ffn_up": nrm(ks[18], (DEPTH, D_MODEL, 2 * D_FF), D_MODEL ** -0.5),
        "ffn_dw_w": nrm(ks[19], (DEPTH, FFN_K, 2 * D_FF), FFN_K ** -0.5),
        "ffn_dw_b": nrm(ks[20], (DEPTH, 2 * D_FF), 0.02),
        "ffn_down": nrm(ks[21], (DEPTH, D_FF, D_MODEL), D_FF ** -0.5),
        "final_g": 1.0 + nrm(ks[22], (D_MODEL,), 0.02),
    }


def _fwd_reference(x, c, ctx, c_ctx, w_ada, b_ada, norm1_g, w_in, ret_decay, ret_gn_g, conv_dw_w, conv_dw_b,
              conv_ln_g, conv_ln_b, conv_pw, na_rpb, w_out, norm2_g, ffn_up, ffn_dw_w, ffn_dw_b, ffn_down, final_g):
    h_ctx = ctx
    for l in range(DEPTH):
        last = l == DEPTH - 1
        sh1, sc1, g1, sh2, sc2, g2 = [t[:, None, :] for t in _adaln(c, w_ada[l], b_ada[l])]
        csh1, csc1, cg1, csh2, csc2, cg2 = _adaln(c_ctx, w_ada[l], b_ada[l])
        hl = _modulate(_rmsnorm(x, norm1_g[l]), sh1, sc1)
        hc = _modulate(_rmsnorm(h_ctx, norm1_g[l]), csh1, csc1)
        mix_l, mix_c = _token_mixers(hl @ w_in[l], hc @ w_in[l], ret_decay[l], ret_gn_g[l], conv_dw_w[l],
                                     conv_dw_b[l], conv_ln_g[l], conv_ln_b[l], conv_pw[l], na_rpb[l],
                                     with_ctx=not last)
        x = x + g1 * (mix_l @ w_out[l])
        hl2 = _modulate(_rmsnorm(x, norm2_g[l]), sh2, sc2)
        x = x + g2 * _conv_ffn(hl2, ffn_up[l], ffn_dw_w[l], ffn_dw_b[l], ffn_down[l])
        if not last:
            h_ctx = h_ctx + cg1 * (mix_c @ w_out[l])
            hc2 = _modulate(_rmsnorm(h_ctx, norm2_g[l]), csh2, csc2)
            h_ctx = h_ctx + cg2 * _conv_ffn(hc2, ffn_up[l], ffn_dw_w[l], ffn_dw_b[l], ffn_down[l])
    return _rmsnorm(x, final_g)


import jax as _jax
import jax.numpy as _jnp

TWIN_FORMAT = 'train_step'
FWD_PARAMS = ['x', 'c', 'ctx', 'c_ctx', 'w_ada', 'b_ada', 'norm1_g', 'w_in', 'ret_decay', 'ret_gn_g', 'conv_dw_w', 'conv_dw_b', 'conv_ln_g', 'conv_ln_b', 'conv_pw', 'na_rpb', 'w_out', 'norm2_g', 'ffn_up', 'ffn_dw_w', 'ffn_dw_b', 'ffn_down', 'final_g']
TWIN_WEIGHTS = ['c_ctx', 'w_ada', 'b_ada', 'norm1_g', 'w_in', 'ret_decay', 'ret_gn_g', 'conv_dw_w', 'conv_dw_b', 'conv_ln_g', 'conv_ln_b', 'conv_pw', 'na_rpb', 'w_out', 'norm2_g', 'ffn_up', 'ffn_dw_w', 'ffn_dw_b', 'ffn_down', 'final_g']
TWIN_DIFF_INPUT = 'x'
TWIN_INPUTS = ['x', 'c', 'ctx', 'c_ctx', 'w_ada', 'b_ada', 'norm1_g', 'w_in', 'ret_decay', 'ret_gn_g', 'conv_dw_w', 'conv_dw_b', 'conv_ln_g', 'conv_ln_b', 'conv_pw', 'na_rpb', 'w_out', 'norm2_g', 'ffn_up', 'ffn_dw_w', 'ffn_dw_b', 'ffn_down', 'final_g', 'loss_target', 'm_c_ctx', 'm_w_ada', 'm_b_ada', 'm_norm1_g', 'm_w_in', 'm_ret_decay', 'm_ret_gn_g', 'm_conv_dw_w', 'm_conv_dw_b', 'm_conv_ln_g', 'm_conv_ln_b', 'm_conv_pw', 'm_na_rpb', 'm_w_out', 'm_norm2_g', 'm_ffn_up', 'm_ffn_dw_w', 'm_ffn_dw_b', 'm_ffn_down', 'm_final_g', 'v_c_ctx', 'v_w_ada', 'v_b_ada', 'v_norm1_g', 'v_w_in', 'v_ret_decay', 'v_ret_gn_g', 'v_conv_dw_w', 'v_conv_dw_b', 'v_conv_ln_g', 'v_conv_ln_b', 'v_conv_pw', 'v_na_rpb', 'v_w_out', 'v_norm2_g', 'v_ffn_up', 'v_ffn_dw_w', 'v_ffn_dw_b', 'v_ffn_down', 'v_final_g']
TWIN_OUTPUTS = ['loss', 'grad_x', 'grad_c_ctx', 'grad_w_ada', 'grad_b_ada', 'grad_norm1_g', 'grad_w_in', 'grad_ret_decay', 'grad_ret_gn_g', 'grad_conv_dw_w', 'grad_conv_dw_b', 'grad_conv_ln_g', 'grad_conv_ln_b', 'grad_conv_pw', 'grad_na_rpb', 'grad_w_out', 'grad_norm2_g', 'grad_ffn_up', 'grad_ffn_dw_w', 'grad_ffn_dw_b', 'grad_ffn_down', 'grad_final_g', 'delta_c_ctx', 'delta_w_ada', 'delta_b_ada', 'delta_norm1_g', 'delta_w_in', 'delta_ret_decay', 'delta_ret_gn_g', 'delta_conv_dw_w', 'delta_conv_dw_b', 'delta_conv_ln_g', 'delta_conv_ln_b', 'delta_conv_pw', 'delta_na_rpb', 'delta_w_out', 'delta_norm2_g', 'delta_ffn_up', 'delta_ffn_dw_w', 'delta_ffn_dw_b', 'delta_ffn_down', 'delta_final_g', 'new_m_c_ctx', 'new_m_w_ada', 'new_m_b_ada', 'new_m_norm1_g', 'new_m_w_in', 'new_m_ret_decay', 'new_m_ret_gn_g', 'new_m_conv_dw_w', 'new_m_conv_dw_b', 'new_m_conv_ln_g', 'new_m_conv_ln_b', 'new_m_conv_pw', 'new_m_na_rpb', 'new_m_w_out', 'new_m_norm2_g', 'new_m_ffn_up', 'new_m_ffn_dw_w', 'new_m_ffn_dw_b', 'new_m_ffn_down', 'new_m_final_g', 'new_v_c_ctx', 'new_v_w_ada', 'new_v_b_ada', 'new_v_norm1_g', 'new_v_w_in', 'new_v_ret_decay', 'new_v_ret_gn_g', 'new_v_conv_dw_w', 'new_v_conv_dw_b', 'new_v_conv_ln_g', 'new_v_conv_ln_b', 'new_v_conv_pw', 'new_v_na_rpb', 'new_v_w_out', 'new_v_norm2_g', 'new_v_ffn_up', 'new_v_ffn_dw_w', 'new_v_ffn_dw_b', 'new_v_ffn_down', 'new_v_final_g']
TWIN_LEAF_KINDS = {'loss': 'loss', 'grad_x': 'grad_x', 'grad_c_ctx': 'grad_w', 'grad_w_ada': 'grad_w', 'grad_b_ada': 'grad_w', 'grad_norm1_g': 'grad_w', 'grad_w_in': 'grad_w', 'grad_ret_decay': 'grad_w', 'grad_ret_gn_g': 'grad_w', 'grad_conv_dw_w': 'grad_w', 'grad_conv_dw_b': 'grad_w', 'grad_conv_ln_g': 'grad_w', 'grad_conv_ln_b': 'grad_w', 'grad_conv_pw': 'grad_w', 'grad_na_rpb': 'grad_w', 'grad_w_out': 'grad_w', 'grad_norm2_g': 'grad_w', 'grad_ffn_up': 'grad_w', 'grad_ffn_dw_w': 'grad_w', 'grad_ffn_dw_b': 'grad_w', 'grad_ffn_down': 'grad_w', 'grad_final_g': 'grad_w', 'delta_c_ctx': 'delta_w', 'delta_w_ada': 'delta_w', 'delta_b_ada': 'delta_w', 'delta_norm1_g': 'delta_w', 'delta_w_in': 'delta_w', 'delta_ret_decay': 'delta_w', 'delta_ret_gn_g': 'delta_w', 'delta_conv_dw_w': 'delta_w', 'delta_conv_dw_b': 'delta_w', 'delta_conv_ln_g': 'delta_w', 'delta_conv_ln_b': 'delta_w', 'delta_conv_pw': 'delta_w', 'delta_na_rpb': 'delta_w', 'delta_w_out': 'delta_w', 'delta_norm2_g': 'delta_w', 'delta_ffn_up': 'delta_w', 'delta_ffn_dw_w': 'delta_w', 'delta_ffn_dw_b': 'delta_w', 'delta_ffn_down': 'delta_w', 'delta_final_g': 'delta_w', 'new_m_c_ctx': 'new_m', 'new_m_w_ada': 'new_m', 'new_m_b_ada': 'new_m', 'new_m_norm1_g': 'new_m', 'new_m_w_in': 'new_m', 'new_m_ret_decay': 'new_m', 'new_m_ret_gn_g': 'new_m', 'new_m_conv_dw_w': 'new_m', 'new_m_conv_dw_b': 'new_m', 'new_m_conv_ln_g': 'new_m', 'new_m_conv_ln_b': 'new_m', 'new_m_conv_pw': 'new_m', 'new_m_na_rpb': 'new_m', 'new_m_w_out': 'new_m', 'new_m_norm2_g': 'new_m', 'new_m_ffn_up': 'new_m', 'new_m_ffn_dw_w': 'new_m', 'new_m_ffn_dw_b': 'new_m', 'new_m_ffn_down': 'new_m', 'new_m_final_g': 'new_m', 'new_v_c_ctx': 'new_v', 'new_v_w_ada': 'new_v', 'new_v_b_ada': 'new_v', 'new_v_norm1_g': 'new_v', 'new_v_w_in': 'new_v', 'new_v_ret_decay': 'new_v', 'new_v_ret_gn_g': 'new_v', 'new_v_conv_dw_w': 'new_v', 'new_v_conv_dw_b': 'new_v', 'new_v_conv_ln_g': 'new_v', 'new_v_conv_ln_b': 'new_v', 'new_v_conv_pw': 'new_v', 'new_v_na_rpb': 'new_v', 'new_v_w_out': 'new_v', 'new_v_norm2_g': 'new_v', 'new_v_ffn_up': 'new_v', 'new_v_ffn_dw_w': 'new_v', 'new_v_ffn_dw_b': 'new_v', 'new_v_ffn_down': 'new_v', 'new_v_final_g': 'new_v'}


def _forward(args):
    return _fwd_reference(*[args[k] for k in FWD_PARAMS])


def _output_shape():
    def fwd():
        inp = _fwd_setup_inputs(0)
        return _fwd_reference(*[inp[k] for k in FWD_PARAMS])
    out = _jax.eval_shape(fwd)
    return out.shape, out.dtype

N_MICROBATCH = 1
ADAM_LR = 0.001
ADAM_B1 = 0.9
ADAM_B2 = 0.999
ADAM_EPS = 1e-08
ADAM_WD = 0.01
ADAM_STEP = 10
PER_EXAMPLE_BATCH_AXIS = {'x': 0, 'c': 0, 'ctx': 0, 'loss_target': 0}
SHARED_INPUTS = []
_WEIGHT_DTYPES = {'c_ctx': _jnp.float32, 'w_ada': _jnp.float32, 'b_ada': _jnp.float32, 'norm1_g': _jnp.float32, 'w_in': _jnp.float32, 'ret_decay': _jnp.float32, 'ret_gn_g': _jnp.float32, 'conv_dw_w': _jnp.float32, 'conv_dw_b': _jnp.float32, 'conv_ln_g': _jnp.float32, 'conv_ln_b': _jnp.float32, 'conv_pw': _jnp.float32, 'na_rpb': _jnp.float32, 'w_out': _jnp.float32, 'norm2_g': _jnp.float32, 'ffn_up': _jnp.float32, 'ffn_dw_w': _jnp.float32, 'ffn_dw_b': _jnp.float32, 'ffn_down': _jnp.float32, 'final_g': _jnp.float32}
MOMENT_SCALE = {'c_ctx': 1.059353e-02, 'w_ada': 2.601774e-02, 'b_ada': 4.377345e-02, 'norm1_g': 2.637771e-02, 'w_in': 1.744138e-02, 'ret_decay': 1.052664e-01, 'ret_gn_g': 1.807775e-02, 'conv_dw_w': 1.675731e-02, 'conv_dw_b': 2.924684e-02, 'conv_ln_g': 1.976321e-02, 'conv_ln_b': 1.707260e-02, 'conv_pw': 1.628098e-02, 'na_rpb': 9.416517e-04, 'w_out': 1.523327e-02, 'norm2_g': 2.634869e-02, 'ffn_up': 1.153026e-02, 'ffn_dw_w': 1.151620e-02, 'ffn_dw_b': 1.070492e-02, 'ffn_down': 1.887091e-02, 'final_g': 1.598932e+01}


def _to_microbatches(a, axis):
    t = _jnp.moveaxis(a, axis, 0)
    t = t.reshape((N_MICROBATCH, t.shape[0] // N_MICROBATCH) + t.shape[1:])
    return _jnp.moveaxis(t, 1, axis + 1)


def setup_inputs(seed: int = 0) -> dict:
    inp = _fwd_setup_inputs(seed)
    key = _jax.random.fold_in(_jax.random.key(seed), 7919)
    shape, _ = _output_shape()
    out = dict(inp)
    out["loss_target"] = _jax.random.normal(_jax.random.fold_in(key, 0), shape, _jnp.float32)
    for i, name in enumerate(TWIN_WEIGHTS):
        w = inp[name].astype(_jnp.float32)
        if MOMENT_SCALE is None:
            s = _jnp.sqrt(_jnp.mean(_jnp.square(w)) + 1e-30)
        else:
            s = MOMENT_SCALE[name]
        km, kv = _jax.random.split(_jax.random.fold_in(key, i + 1))
        out[name] = w
        out["m_" + name] = s * _jax.random.normal(km, w.shape, _jnp.float32)
        out["v_" + name] = (s * s) * _jax.random.uniform(kv, w.shape, _jnp.float32, 0.5, 1.5)
    if N_MICROBATCH > 1:
        for name, axis in PER_EXAMPLE_BATCH_AXIS.items():
            out[name] = _to_microbatches(out[name], axis)
    return {'x': out['x'], 'c': out['c'], 'ctx': out['ctx'], 'c_ctx': out['c_ctx'], 'w_ada': out['w_ada'], 'b_ada': out['b_ada'], 'norm1_g': out['norm1_g'], 'w_in': out['w_in'], 'ret_decay': out['ret_decay'], 'ret_gn_g': out['ret_gn_g'], 'conv_dw_w': out['conv_dw_w'], 'conv_dw_b': out['conv_dw_b'], 'conv_ln_g': out['conv_ln_g'], 'conv_ln_b': out['conv_ln_b'], 'conv_pw': out['conv_pw'], 'na_rpb': out['na_rpb'], 'w_out': out['w_out'], 'norm2_g': out['norm2_g'], 'ffn_up': out['ffn_up'], 'ffn_dw_w': out['ffn_dw_w'], 'ffn_dw_b': out['ffn_dw_b'], 'ffn_down': out['ffn_down'], 'final_g': out['final_g'], 'loss_target': out['loss_target'], 'm_c_ctx': out['m_c_ctx'], 'm_w_ada': out['m_w_ada'], 'm_b_ada': out['m_b_ada'], 'm_norm1_g': out['m_norm1_g'], 'm_w_in': out['m_w_in'], 'm_ret_decay': out['m_ret_decay'], 'm_ret_gn_g': out['m_ret_gn_g'], 'm_conv_dw_w': out['m_conv_dw_w'], 'm_conv_dw_b': out['m_conv_dw_b'], 'm_conv_ln_g': out['m_conv_ln_g'], 'm_conv_ln_b': out['m_conv_ln_b'], 'm_conv_pw': out['m_conv_pw'], 'm_na_rpb': out['m_na_rpb'], 'm_w_out': out['m_w_out'], 'm_norm2_g': out['m_norm2_g'], 'm_ffn_up': out['m_ffn_up'], 'm_ffn_dw_w': out['m_ffn_dw_w'], 'm_ffn_dw_b': out['m_ffn_dw_b'], 'm_ffn_down': out['m_ffn_down'], 'm_final_g': out['m_final_g'], 'v_c_ctx': out['v_c_ctx'], 'v_w_ada': out['v_w_ada'], 'v_b_ada': out['v_b_ada'], 'v_norm1_g': out['v_norm1_g'], 'v_w_in': out['v_w_in'], 'v_ret_decay': out['v_ret_decay'], 'v_ret_gn_g': out['v_ret_gn_g'], 'v_conv_dw_w': out['v_conv_dw_w'], 'v_conv_dw_b': out['v_conv_dw_b'], 'v_conv_ln_g': out['v_conv_ln_g'], 'v_conv_ln_b': out['v_conv_ln_b'], 'v_conv_pw': out['v_conv_pw'], 'v_na_rpb': out['v_na_rpb'], 'v_w_out': out['v_w_out'], 'v_norm2_g': out['v_norm2_g'], 'v_ffn_up': out['v_ffn_up'], 'v_ffn_dw_w': out['v_ffn_dw_w'], 'v_ffn_dw_b': out['v_ffn_dw_b'], 'v_ffn_down': out['v_ffn_down'], 'v_final_g': out['v_final_g']}


def _loss(weights, diff, rest, loss_target):
    with _jax.named_scope("forward"):
        args = {**rest, TWIN_DIFF_INPUT: diff, **{k: w.astype(_WEIGHT_DTYPES[k]) for k, w in weights.items()}}
        y = _forward(args)
    with _jax.named_scope("loss_head"):
        err = _jnp.square(y.astype(_jnp.float32) - loss_target)
        return 0.5 * _jnp.sum(_jnp.mean(err, axis=-1)) if err.ndim else 0.5 * err


def _adamw(w, g, m, v):
    m = ADAM_B1 * m + (1.0 - ADAM_B1) * g
    v = ADAM_B2 * v + (1.0 - ADAM_B2) * _jnp.square(g)
    m_hat = m / (1.0 - ADAM_B1 ** ADAM_STEP)
    v_hat = v / (1.0 - ADAM_B2 ** ADAM_STEP)
    delta = -ADAM_LR * (m_hat / (_jnp.sqrt(v_hat) + ADAM_EPS) + ADAM_WD * w)
    return delta, m, v


def reference(x, c, ctx, c_ctx, w_ada, b_ada, norm1_g, w_in, ret_decay, ret_gn_g, conv_dw_w, conv_dw_b, conv_ln_g, conv_ln_b, conv_pw, na_rpb, w_out, norm2_g, ffn_up, ffn_dw_w, ffn_dw_b, ffn_down, final_g, loss_target, m_c_ctx, m_w_ada, m_b_ada, m_norm1_g, m_w_in, m_ret_decay, m_ret_gn_g, m_conv_dw_w, m_conv_dw_b, m_conv_ln_g, m_conv_ln_b, m_conv_pw, m_na_rpb, m_w_out, m_norm2_g, m_ffn_up, m_ffn_dw_w, m_ffn_dw_b, m_ffn_down, m_final_g, v_c_ctx, v_w_ada, v_b_ada, v_norm1_g, v_w_in, v_ret_decay, v_ret_gn_g, v_conv_dw_w, v_conv_dw_b, v_conv_ln_g, v_conv_ln_b, v_conv_pw, v_na_rpb, v_w_out, v_norm2_g, v_ffn_up, v_ffn_dw_w, v_ffn_dw_b, v_ffn_down, v_final_g):
    given = dict(x=x, c=c, ctx=ctx, c_ctx=c_ctx, w_ada=w_ada, b_ada=b_ada, norm1_g=norm1_g, w_in=w_in, ret_decay=ret_decay, ret_gn_g=ret_gn_g, conv_dw_w=conv_dw_w, conv_dw_b=conv_dw_b, conv_ln_g=conv_ln_g, conv_ln_b=conv_ln_b, conv_pw=conv_pw, na_rpb=na_rpb, w_out=w_out, norm2_g=norm2_g, ffn_up=ffn_up, ffn_dw_w=ffn_dw_w, ffn_dw_b=ffn_dw_b, ffn_down=ffn_down, final_g=final_g, loss_target=loss_target, m_c_ctx=m_c_ctx, m_w_ada=m_w_ada, m_b_ada=m_b_ada, m_norm1_g=m_norm1_g, m_w_in=m_w_in, m_ret_decay=m_ret_decay, m_ret_gn_g=m_ret_gn_g, m_conv_dw_w=m_conv_dw_w, m_conv_dw_b=m_conv_dw_b, m_conv_ln_g=m_conv_ln_g, m_conv_ln_b=m_conv_ln_b, m_conv_pw=m_conv_pw, m_na_rpb=m_na_rpb, m_w_out=m_w_out, m_norm2_g=m_norm2_g, m_ffn_up=m_ffn_up, m_ffn_dw_w=m_ffn_dw_w, m_ffn_dw_b=m_ffn_dw_b, m_ffn_down=m_ffn_down, m_final_g=m_final_g, v_c_ctx=v_c_ctx, v_w_ada=v_w_ada, v_b_ada=v_b_ada, v_norm1_g=v_norm1_g, v_w_in=v_w_in, v_ret_decay=v_ret_decay, v_ret_gn_g=v_ret_gn_g, v_conv_dw_w=v_conv_dw_w, v_conv_dw_b=v_conv_dw_b, v_conv_ln_g=v_conv_ln_g, v_conv_ln_b=v_conv_ln_b, v_conv_pw=v_conv_pw, v_na_rpb=v_na_rpb, v_w_out=v_w_out, v_norm2_g=v_norm2_g, v_ffn_up=v_ffn_up, v_ffn_dw_w=v_ffn_dw_w, v_ffn_dw_b=v_ffn_dw_b, v_ffn_down=v_ffn_down, v_final_g=v_final_g)
    weights = {n: given[n] for n in TWIN_WEIGHTS}
    shared = {n: given[n] for n in SHARED_INPUTS}
    per_example = {n: given[n] for n in ['x', 'c', 'ctx']}
    grad_fn = _jax.value_and_grad(_loss, argnums=(0, 1))

    def one_microbatch(ex, loss_target):
        ex = dict(ex)
        diff = ex.pop(TWIN_DIFF_INPUT)
        return grad_fn(weights, diff, {**shared, **ex}, loss_target)

    if N_MICROBATCH == 1:
        loss, (grad_w, grad_x) = one_microbatch(per_example, given["loss_target"])
    else:
        def body(carry, xs):
            loss_sum, grad_sum = carry
            l_k, (gw_k, gx_k) = one_microbatch(xs[0], xs[1])
            with _jax.named_scope("update"):
                return (loss_sum + l_k, _jax.tree.map(_jnp.add, grad_sum, gw_k)), gx_k

        init = (_jnp.zeros((), _jnp.float32), _jax.tree.map(_jnp.zeros_like, weights))
        (loss, grad_w), grad_x = _jax.lax.scan(body, init, (per_example, given["loss_target"]))
    with _jax.named_scope("update"):
        delta_w, new_m, new_v = {}, {}, {}
        for n in TWIN_WEIGHTS:
            delta_w[n], new_m[n], new_v[n] = _adamw(weights[n], grad_w[n], given["m_" + n], given["v_" + n])
    return (loss, grad_x, *[grad_w[n] for n in TWIN_WEIGHTS], *[delta_w[n] for n in TWIN_WEIGHTS],
            *[new_m[n] for n in TWIN_WEIGHTS], *[new_v[n] for n in TWIN_WEIGHTS])
```

```python
import functools
import math

import numpy as np
import jax
import jax.numpy as jnp
from jax import lax
from jax.experimental import pallas as pl
from jax.experimental.pallas import tpu as pltpu

D_MODEL = 2048
SEQ = 4096
DEPTH = 2
GRID_W = 64
CTX_LEN = 256
RET_HEADS = 4
RET_DK = 128
RET_DV = 256
RET_CHUNK = 128
CONV_W = 512
CONV_K = 31
NA_HEADS = 4
NA_DH = 128
NA_ROWS = 8
NA_COLS = 16
D_FF = 5632
FFN_K = 3
ROPE_BASE = 10000.0
EPS = 1e-6
ADAM_LR = 0.001
ADAM_B1 = 0.9
ADAM_B2 = 0.999
ADAM_EPS = 1e-08
ADAM_WD = 0.01
ADAM_STEP = 10
N_DEV = 8

LANES = 128
SUBLANES = 8
VMEM_LIMIT = 56 * 1024 * 1024

F32 = jnp.float32
BF16 = jnp.bfloat16
MESH = pl.DeviceIdType.MESH
NEG = -1e30


def _ret_qk_w():
    return RET_HEADS * RET_DK


def _ret_w():
    return RET_HEADS * RET_DV


def _na_w():
    return NA_HEADS * NA_DH


def _d_in():
    return 2 * _ret_qk_w() + 2 * _ret_w() + 2 * CONV_W + 3 * _na_w()


def _offsets():
    sizes = [_ret_qk_w(), _ret_qk_w(), _ret_w(), _ret_w(), CONV_W, CONV_W, _na_w(), _na_w(), _na_w()]
    offs = [0]
    for s in sizes[:-1]:
        offs.append(offs[-1] + s)
    return dict(zip(["q", "k", "v", "g", "a", "b", "nq", "nk", "nv"], offs))


def _t_rows():
    return CTX_LEN + SEQ


def _tm():
    return CTX_LEN


def _params(sem=None):
    kw = dict(vmem_limit_bytes=VMEM_LIMIT)
    if sem is not None:
        kw["dimension_semantics"] = sem
    return pltpu.CompilerParams(**kw)


def _tile(n, pref, align):
    best = None
    for t in range(align, min(n, pref) + 1, align):
        if n % t == 0:
            best = t
    return best if best is not None else n


def _dg(a, b, ca, cb):
    return lax.dot_general(a.astype(BF16), b.astype(BF16), (((ca,), (cb,)), ((), ())), preferred_element_type=F32)


@jax.custom_vjp
def dot_nn(a, b):
    return _dg(a, b, 1, 0)


dot_nn.defvjp(lambda a, b: (_dg(a, b, 1, 0), (a, b)),
              lambda r, g: (_dg(g, r[1], 1, 1), _dg(r[0], g, 0, 0)))


@jax.custom_vjp
def dot_nt(a, b):
    return _dg(a, b, 1, 1)


dot_nt.defvjp(lambda a, b: (_dg(a, b, 1, 1), (a, b)),
              lambda r, g: (_dg(g, r[1], 1, 0), _dg(g, r[0], 0, 0)))


@jax.custom_vjp
def dot_tn(a, b):
    return _dg(a, b, 0, 0)


dot_tn.defvjp(lambda a, b: (_dg(a, b, 0, 0), (a, b)),
              lambda r, g: (_dg(r[1], g, 1, 1), _dg(r[0], g, 1, 0)))


def _sigmoid(x):
    return 1.0 / (1.0 + jnp.exp(-x))


def _silu(x):
    return x * _sigmoid(x)


def _my_pos():
    return lax.axis_index("x"), lax.axis_index("y"), lax.axis_index("c")


def _my_index():
    x, y, c = _my_pos()
    return 4 * x + 2 * y + c


_ANY = pl.BlockSpec(memory_space=pl.ANY)


def _all_gather(arrays, name):
    n = len(arrays)

    def body(*refs):
        xs, outs = refs[:n], refs[n:2 * n]
        send_sems, recv_sems, local_sems = refs[2 * n:]
        x, y, c = _my_pos()
        me, sibling = (x, y, c), (x, y, 1 - c)
        chips = [(1 - x, y), (x, 1 - y), (1 - x, 1 - y)]

        def slot(a, p):
            return outs[a].at[4 * p[0] + 2 * p[1] + p[2]]

        def copy(a, k, block, to, src=None):
            return pltpu.make_async_remote_copy(
                src_ref=slot(a, block) if src is None else src, dst_ref=slot(a, block),
                send_sem=send_sems.at[a, k], recv_sem=recv_sems.at[a, k], device_id=to, device_id_type=MESH)

        mine = [pltpu.make_async_copy(xs[a], slot(a, me), local_sems.at[a]) for a in range(n)]
        for m in mine:
            m.start()
        first = []
        for a in range(n):
            first.append(copy(a, 0, me, sibling, src=xs[a]))
            first += [copy(a, 1 + j, me, (*chip, c), src=xs[a]) for j, chip in enumerate(chips)]
        for cp in first:
            cp.start()
        passed = []
        for a in range(n):
            for j, chip in enumerate(chips):
                copy(a, 1 + j, (*chip, c), me).wait_recv()
                p = copy(a, 4 + j, (*chip, c), sibling)
                p.start()
                passed.append(p)
        for a in range(n):
            copy(a, 0, sibling, me).wait_recv()
            for j, chip in enumerate(chips):
                copy(a, 4 + j, (*chip, 1 - c), me).wait_recv()
        for cp in first + passed:
            cp.wait_send()
        for m in mine:
            m.wait()

    return pl.pallas_call(
        body, name=name,
        out_shape=[jax.ShapeDtypeStruct((N_DEV,) + a.shape, a.dtype) for a in arrays],
        in_specs=[_ANY] * n, out_specs=[_ANY] * n,
        scratch_shapes=[pltpu.SemaphoreType.DMA((n, 7)), pltpu.SemaphoreType.DMA((n, 7)),
                        pltpu.SemaphoreType.DMA((n,))],
    )(*arrays)


def _grad_exchange(groups, name):
    flat = [a for grp in groups for a in grp]
    n = len(flat)
    where = [(g, l) for g, grp in enumerate(groups) for l in range(len(grp))]

    def body(*refs):
        xs, outs = refs[:n], refs[n:n + len(groups)]
        send_sems, recv_sems, local_sems = refs[n + len(groups):]
        x, y, c = _my_pos()
        me = 4 * x + 2 * y + c
        mine = []
        for a, (g, l) in enumerate(where):
            m = pltpu.make_async_copy(xs[a].at[me], outs[g].at[l, me], local_sems.at[a])
            m.start()
            mine.append(m)
        sends, recvs = [], []
        for k in range(1, N_DEV):
            dx, dy, dc = (k >> 2) & 1, (k >> 1) & 1, k & 1
            px = 1 - x if dx else x
            py = 1 - y if dy else y
            pc = 1 - c if dc else c
            peer = 4 * px + 2 * py + pc
            for a, (g, l) in enumerate(where):
                s = pltpu.make_async_remote_copy(
                    src_ref=xs[a].at[peer], dst_ref=outs[g].at[l, me], send_sem=send_sems.at[a, k - 1],
                    recv_sem=recv_sems.at[a, k - 1], device_id=(px, py, pc), device_id_type=MESH)
                s.start()
                sends.append(s)
                recvs.append(pltpu.make_async_remote_copy(
                    src_ref=xs[a].at[me], dst_ref=outs[g].at[l, peer], send_sem=send_sems.at[a, k - 1],
                    recv_sem=recv_sems.at[a, k - 1], device_id=(px, py, pc), device_id_type=MESH))
        for r in recvs:
            r.wait_recv()
        for s in sends:
            s.wait_send()
        for m in mine:
            m.wait()

    return pl.pallas_call(
        body, name=name,
        out_shape=[jax.ShapeDtypeStruct((len(grp),) + grp[0].shape, grp[0].dtype) for grp in groups],
        in_specs=[_ANY] * n, out_specs=[_ANY] * len(groups),
        scratch_shapes=[pltpu.SemaphoreType.DMA((n, 7)), pltpu.SemaphoreType.DMA((n, 7)),
                        pltpu.SemaphoreType.DMA((n,))],
    )(*flat)


def _mm(a, b, name, ta=False, tb=False, out_dtype=F32, b3=False, o3=False, tm=1088, tn=1024, tk=512):
    m = a.shape[1] if ta else a.shape[0]
    k = a.shape[0] if ta else a.shape[1]
    if b3:
        cs = b.shape[2]
        n = b.shape[1] if tb else N_DEV * cs
        kb = N_DEV * cs if tb else b.shape[1]
    else:
        n = b.shape[0] if tb else b.shape[1]
        kb = b.shape[1] if tb else b.shape[0]
    assert k == kb, (a.shape, b.shape, ta, tb)
    tm = _tile(m, tm, LANES if ta else 2 * SUBLANES)
    if o3 or (b3 and not tb):
        tn = (n // N_DEV)
    else:
        tn = _tile(n, tn, LANES)
    if b3 and tb:
        tk = cs
    else:
        tk = _tile(k, tk, LANES)
    nk = k // tk
    ca, cb = (0 if ta else 1), (1 if tb else 0)

    def body(a_ref, b_ref, o_ref, acc_ref):
        kk = pl.program_id(2)

        @pl.when(kk == 0)
        def _():
            acc_ref[...] = jnp.zeros_like(acc_ref)

        acc_ref[...] += lax.dot_general(a_ref[...], b_ref[...], (((ca,), (cb,)), ((), ())),
                                        preferred_element_type=F32)

        @pl.when(kk == nk - 1)
        def _():
            o_ref[...] = acc_ref[...].astype(o_ref.dtype)

    a_spec = pl.BlockSpec((tk, tm), lambda i, j, kk: (kk, i)) if ta else pl.BlockSpec((tm, tk), lambda i, j, kk: (i, kk))
    if b3:
        if tb:
            b_spec = pl.BlockSpec((None, tn, cs), lambda i, j, kk: (kk, j, 0))
        else:
            b_spec = pl.BlockSpec((None, tk, cs), lambda i, j, kk: (j, kk, 0))
    else:
        b_spec = pl.BlockSpec((tn, tk), lambda i, j, kk: (j, kk)) if tb else pl.BlockSpec((tk, tn), lambda i, j, kk: (kk, j))
    if o3:
        o_spec = pl.BlockSpec((None, tm, tn), lambda i, j, kk: (j, i, 0))
        o_shape = jax.ShapeDtypeStruct((N_DEV, m, tn), out_dtype)
    else:
        o_spec = pl.BlockSpec((tm, tn), lambda i, j, kk: (i, j))
        o_shape = jax.ShapeDtypeStruct((m, n), out_dtype)
    return pl.pallas_call(
        body, name=name, grid=(m // tm, n // tn, nk), in_specs=[a_spec, b_spec], out_specs=o_spec, out_shape=o_shape,
        scratch_shapes=[pltpu.VMEM((tm, tn), F32)],
        compiler_params=_params(("parallel", "parallel", "arbitrary")),
    )(a, b)


def _cast_bf16(x, name):
    r, c = x.shape
    tr = _tile(r, 512, 2 * SUBLANES)

    def body(x_ref, o_ref):
        o_ref[...] = x_ref[...].astype(BF16)

    return pl.pallas_call(body, name=name, grid=(r // tr,), in_specs=[pl.BlockSpec((tr, c), lambda i: (i, 0))],
                          out_specs=pl.BlockSpec((tr, c), lambda i: (i, 0)),
                          out_shape=jax.ShapeDtypeStruct((r, c), BF16), compiler_params=_params(("parallel",)))(x)


def _cols_from_shards(wg, name):
    _, k, cs = wg.shape
    tk = _tile(k, 256, 2 * SUBLANES)

    def body(w_ref, o_ref):
        for j in range(N_DEV):
            o_ref[:, j * cs:(j + 1) * cs] = w_ref[j]

    return pl.pallas_call(body, name=name, grid=(k // tk,),
                          in_specs=[pl.BlockSpec((N_DEV, tk, cs), lambda i: (0, i, 0))],
                          out_specs=pl.BlockSpec((tk, N_DEV * cs), lambda i: (i, 0)),
                          out_shape=jax.ShapeDtypeStruct((k, N_DEV * cs), wg.dtype),
                          compiler_params=_params(("parallel",)))(wg)


def _cols_to_shards(w, name):
    k, n = w.shape
    cs = n // N_DEV
    tk = _tile(k, 256, 2 * SUBLANES)

    def body(w_ref, o_ref):
        for j in range(N_DEV):
            o_ref[j] = w_ref[:, j * cs:(j + 1) * cs].astype(BF16)

    return pl.pallas_call(body, name=name, grid=(k // tk,),
                          in_specs=[pl.BlockSpec((tk, n), lambda i: (i, 0))],
                          out_specs=pl.BlockSpec((N_DEV, tk, cs), lambda i: (0, i, 0)),
                          out_shape=jax.ShapeDtypeStruct((N_DEV, k, cs), BF16),
                          compiler_params=_params(("parallel",)))(w)


def _stream(i):
    return jnp.minimum(i, 1)


def _normmod(x, g, sh, sc):
    y = x * lax.rsqrt(jnp.mean(x * x, axis=-1, keepdims=True) + EPS)
    return (y * g) * (1.0 + sc) + sh


def _mod_spec(chunk, d):
    return pl.BlockSpec((None, None, 1, d), lambda i: (_stream(i), chunk, 0, 0))


def _normmod_fwd(x, g, mod4, which, name):
    t, d = x.shape
    tm = _tm()
    ish, isc = (0, 1) if which == 0 else (3, 4)

    def body(x_ref, g_ref, sh_ref, sc_ref, o_ref):
        o_ref[...] = _normmod(x_ref[...], g_ref[...], sh_ref[...], sc_ref[...]).astype(BF16)

    row = pl.BlockSpec((tm, d), lambda i: (i, 0))
    return pl.pallas_call(body, name=name, grid=(t // tm,),
                          in_specs=[row, pl.BlockSpec((1, d), lambda i: (0, 0)), _mod_spec(ish, d), _mod_spec(isc, d)],
                          out_specs=row, out_shape=jax.ShapeDtypeStruct((t, d), BF16),
                          compiler_params=_params(("parallel",)))(x, g, mod4, mod4)


def _normmod_bwd(x, g, mod4, which, dh, dres, name):
    t, d = x.shape
    tm = _tm()
    ish, isc = (0, 1) if which == 0 else (3, 4)

    def body(x_ref, g_ref, sh_ref, sc_ref, dh_ref, dres_ref, dx_ref, dg_ref, dsh_ref, dsc_ref):
        i = pl.program_id(0)
        _, vjp = jax.vjp(_normmod, x_ref[...], g_ref[...], sh_ref[...], sc_ref[...])
        dx, dg, dsh, dsc = vjp(dh_ref[...])
        dx_ref[...] = dres_ref[...] + dx

        @pl.when(i == 0)
        def _():
            dg_ref[...] = jnp.zeros_like(dg_ref)

        @pl.when(i <= 1)
        def _():
            dsh_ref[...] = jnp.zeros_like(dsh_ref)
            dsc_ref[...] = jnp.zeros_like(dsc_ref)

        dg_ref[...] += dg
        dsh_ref[...] += dsh
        dsc_ref[...] += dsc

    row = pl.BlockSpec((tm, d), lambda i: (i, 0))
    vec = pl.BlockSpec((1, d), lambda i: (0, 0))
    svec = pl.BlockSpec((None, 1, d), lambda i: (_stream(i), 0, 0))
    return pl.pallas_call(
        body, name=name, grid=(t // tm,),
        in_specs=[row, vec, _mod_spec(ish, d), _mod_spec(isc, d), row, row],
        out_specs=[row, vec, svec, svec],
        out_shape=[jax.ShapeDtypeStruct((t, d), F32), jax.ShapeDtypeStruct((1, d), F32),
                   jax.ShapeDtypeStruct((2, 1, d), F32), jax.ShapeDtypeStruct((2, 1, d), F32)],
        compiler_params=_params(("arbitrary",)))(x, g, mod4, mod4, dh, dres)


def _gate_res_fwd(x, f, mod4, chunk, name):
    t, d = x.shape
    tm = _tm()

    def body(x_ref, f_ref, g_ref, o_ref):
        o_ref[...] = x_ref[...] + g_ref[...] * f_ref[...]

    row = pl.BlockSpec((tm, d), lambda i: (i, 0))
    return pl.pallas_call(body, name=name, grid=(t // tm,), in_specs=[row, row, _mod_spec(chunk, d)], out_specs=row,
                          out_shape=jax.ShapeDtypeStruct((t, d), F32), compiler_params=_params(("parallel",)))(x, f, mod4)


def _gate_res_bwd(dx, f, mod4, chunk, name):
    t, d = dx.shape
    tm = _tm()

    def body(dx_ref, f_ref, g_ref, o_ref, dg_ref):
        i = pl.program_id(0)
        dxv = dx_ref[...]
        o_ref[...] = (dxv * g_ref[...]).astype(BF16)

        @pl.when(i <= 1)
        def _():
            dg_ref[...] = jnp.zeros_like(dg_ref)

        dg_ref[...] += jnp.sum(dxv * f_ref[...], axis=0, keepdims=True)

    row = pl.BlockSpec((tm, d), lambda i: (i, 0))
    return pl.pallas_call(
        body, name=name, grid=(t // tm,), in_specs=[row, row, _mod_spec(chunk, d)],
        out_specs=[row, pl.BlockSpec((None, 1, d), lambda i: (_stream(i), 0, 0))],
        out_shape=[jax.ShapeDtypeStruct((t, d), BF16), jax.ShapeDtypeStruct((2, 1, d), F32)],
        compiler_params=_params(("arbitrary",)))(dx, f, mod4)


def _loss_head(x, final_g, target, name):
    t, d = x.shape
    tm = _tm()

    def loss_fn(xv, g, tgt):
        y = (xv * lax.rsqrt(jnp.mean(xv * xv, axis=-1, keepdims=True) + EPS)) * g
        err = y - tgt
        return 0.5 * jnp.sum(jnp.mean(err * err, axis=-1, keepdims=True))

    def body(x_ref, g_ref, t_ref, l_ref, dx_ref, dg_ref):
        i = pl.program_id(0)

        @pl.when(i == 0)
        def _():
            l_ref[...] = jnp.zeros_like(l_ref)
            dg_ref[...] = jnp.zeros_like(dg_ref)
            dx_ref[...] = jnp.zeros_like(dx_ref)

        @pl.when(i > 0)
        def _():
            l, (dx, dg) = jax.value_and_grad(loss_fn, argnums=(0, 1))(x_ref[...], g_ref[...], t_ref[...])
            l_ref[...] += jnp.full(l_ref.shape, l, F32)
            dx_ref[...] = dx
            dg_ref[...] += dg

    row = pl.BlockSpec((tm, d), lambda i: (i, 0))
    vec = pl.BlockSpec((1, d), lambda i: (0, 0))
    return pl.pallas_call(
        body, name=name, grid=(t // tm,),
        in_specs=[row, vec, pl.BlockSpec((tm, d), lambda i: (jnp.maximum(i - 1, 0), 0))],
        out_specs=[pl.BlockSpec((SUBLANES, LANES), lambda i: (0, 0)), row, vec],
        out_shape=[jax.ShapeDtypeStruct((SUBLANES, LANES), F32), jax.ShapeDtypeStruct((t, d), F32),
                   jax.ShapeDtypeStruct((1, d), F32)],
        compiler_params=_params(("arbitrary",)))(x, final_g, target)


def _swap_quarters(x):
    half, nf = RET_DK // 2, RET_DK // 4
    lane = lax.broadcasted_iota(jnp.int32, x.shape, 1)
    return jnp.where((lane % half) < nf, pltpu.roll(x, RET_DK - nf, 1), pltpu.roll(x, nf, 1))


def _rope(x, cos, sin):
    return x * cos + _swap_quarters(x) * sin


def _rope_t(y, cos, sin):
    return y * cos + _swap_quarters(y * sin)


def _ret_consts(d):
    c = RET_CHUNK
    ii = lax.broadcasted_iota(jnp.int32, (c, 1), 0).astype(F32)
    jj = lax.broadcasted_iota(jnp.int32, (1, c), 1).astype(F32)
    fwd = d == 0
    sgn = jnp.where(fwd, 1.0, -1.0).astype(F32)
    pos = jnp.where(fwd, ii, c - 1.0 - ii)
    return sgn * (ii - jj), pos


def _ret_step(lgt, state, q, k, v, diff, pos):
    c = float(RET_CHUNK)
    lg = -(jnp.maximum(-lgt, 0.0) + jnp.log1p(jnp.exp(-jnp.abs(lgt))))
    lower = diff >= 0
    decay = jnp.where(lower, jnp.exp(jnp.where(lower, diff, 0.0) * lg), 0.0)
    xi = jnp.exp((pos + 1.0) * lg)
    zeta = jnp.exp((c - 1.0 - pos) * lg)
    gch = jnp.exp(c * lg)
    inner = dot_nt(q, k) * decay
    out = dot_nn(inner, v) + dot_nn(q, state) * xi
    new_state = state * gch + dot_tn(k * zeta, v)
    return out, new_state


def _chunk_order():
    nc, nch = CTX_LEN // RET_CHUNK, _t_rows() // RET_CHUNK
    fwd = list(range(nch))
    bwd = list(range(nc - 1, -1, -1)) + list(range(nch - 1, nc - 1, -1))
    return jnp.asarray(np.array([fwd, bwd], np.int32))


def _ret_fwd(p, cos, sin, decay, order, name):
    t = p.shape[0]
    c, dk, dv, nh = RET_CHUNK, RET_DK, RET_DV, RET_HEADS
    nch = t // c
    off = _offsets()
    qb, kb, vb = off["q"] // dk, off["k"] // dk, off["v"] // dv
    scale = RET_DK ** -0.5

    def body(ord_ref, dec_ref, q_ref, k_ref, v_ref, cos_ref, sin_ref, o_ref, st_ref, state):
        d, h, s = pl.program_id(0), pl.program_id(1), pl.program_id(2)

        @pl.when(s == 0)
        def _():
            state[...] = jnp.zeros_like(state)

        st_ref[...] = state[...]
        diff, pos = _ret_consts(d)
        lgt = jnp.full((1, 1), dec_ref[d, h], F32)
        q = _rope(q_ref[...], cos_ref[...], sin_ref[...]) * scale
        k = _rope(k_ref[...], cos_ref[...], sin_ref[...])
        out, ns = _ret_step(lgt, state[...], q, k, v_ref[...], diff, pos)
        o_ref[...] = out
        state[...] = ns

    grid_spec = pltpu.PrefetchScalarGridSpec(
        num_scalar_prefetch=1, grid=(2, nh, nch),
        in_specs=[pl.BlockSpec(memory_space=pltpu.SMEM),
                  pl.BlockSpec((c, dk), lambda d, h, s, o: (o[d, s], qb + h)),
                  pl.BlockSpec((c, dk), lambda d, h, s, o: (o[d, s], kb + h)),
                  pl.BlockSpec((c, dv), lambda d, h, s, o: (o[d, s], vb + h)),
                  pl.BlockSpec((c, dk), lambda d, h, s, o: (o[d, s], 0)),
                  pl.BlockSpec((c, dk), lambda d, h, s, o: (o[d, s], 0))],
        out_specs=[pl.BlockSpec((None, c, dv), lambda d, h, s, o: (d, o[d, s], h)),
                   pl.BlockSpec((None, None, None, dk, dv), lambda d, h, s, o: (d, h, s, 0, 0))],
        scratch_shapes=[pltpu.VMEM((dk, dv), F32)])
    return pl.pallas_call(
        body, name=name, grid_spec=grid_spec,
        out_shape=[jax.ShapeDtypeStruct((2, t, nh * dv), F32), jax.ShapeDtypeStruct((2, nh, nch, dk, dv), F32)],
        compiler_params=_params(("arbitrary", "arbitrary", "arbitrary")))(order, decay, p, p, p, cos, sin)


def _ret_bwd(p, cos, sin, decay, order, states, do, name):
    t = p.shape[0]
    c, dk, dv, nh = RET_CHUNK, RET_DK, RET_DV, RET_HEADS
    nch = t // c
    off = _offsets()
    qb, kb, vb = off["q"] // dk, off["k"] // dk, off["v"] // dv
    scale = RET_DK ** -0.5

    def body(ord_ref, dec_ref, q_ref, k_ref, v_ref, cos_ref, sin_ref, st_ref, do_ref,
             dq_ref, dk_ref, dv_ref, dd_ref, dstate):
        d, h, s = pl.program_id(0), pl.program_id(1), pl.program_id(2)

        @pl.when(s == 0)
        def _():
            dstate[...] = jnp.zeros_like(dstate)
            dd_ref[...] = jnp.zeros_like(dd_ref)

        diff, pos = _ret_consts(d)
        lgt = jnp.full((1, 1), dec_ref[d, h], F32)
        cosv, sinv = cos_ref[...], sin_ref[...]
        q = _rope(q_ref[...], cosv, sinv) * scale
        k = _rope(k_ref[...], cosv, sinv)
        _, vjp = jax.vjp(lambda a, b, cq, ck, cv: _ret_step(a, b, cq, ck, cv, diff, pos),
                         lgt, st_ref[...], q, k, v_ref[...])
        dlgt, dst, dq, dkk, dvv = vjp((do_ref[...], dstate[...]))
        dstate[...] = dst
        dq_ref[...] = _rope_t(dq * scale, cosv, sinv)
        dk_ref[...] = _rope_t(dkk, cosv, sinv)
        dv_ref[...] = dvv
        dd_ref[...] += jnp.broadcast_to(dlgt, dd_ref.shape)

    rev = lambda o, d, s: o[d, nch - 1 - s]
    grid_spec = pltpu.PrefetchScalarGridSpec(
        num_scalar_prefetch=1, grid=(2, nh, nch),
        in_specs=[pl.BlockSpec(memory_space=pltpu.SMEM),
                  pl.BlockSpec((c, dk), lambda d, h, s, o: (rev(o, d, s), qb + h)),
                  pl.BlockSpec((c, dk), lambda d, h, s, o: (rev(o, d, s), kb + h)),
                  pl.BlockSpec((c, dv), lambda d, h, s, o: (rev(o, d, s), vb + h)),
                  pl.BlockSpec((c, dk), lambda d, h, s, o: (rev(o, d, s), 0)),
                  pl.BlockSpec((c, dk), lambda d, h, s, o: (rev(o, d, s), 0)),
                  pl.BlockSpec((None, None, None, dk, dv), lambda d, h, s, o: (d, h, nch - 1 - s, 0, 0)),
                  pl.BlockSpec((c, dv), lambda d, h, s, o: (rev(o, d, s), h))],
        out_specs=[pl.BlockSpec((None, c, dk), lambda d, h, s, o: (d, rev(o, d, s), h)),
                   pl.BlockSpec((None, c, dk), lambda d, h, s, o: (d, rev(o, d, s), h)),
                   pl.BlockSpec((None, c, dv), lambda d, h, s, o: (d, rev(o, d, s), h)),
                   pl.BlockSpec((None, None, SUBLANES, LANES), lambda d, h, s, o: (d, h, 0, 0))],
        scratch_shapes=[pltpu.VMEM((dk, dv), F32)])
    return pl.pallas_call(
        body, name=name, grid_spec=grid_spec,
        out_shape=[jax.ShapeDtypeStruct((2, t, nh * dk), F32), jax.ShapeDtypeStruct((2, t, nh * dk), F32),
                   jax.ShapeDtypeStruct((2, t, nh * dv), F32), jax.ShapeDtypeStruct((2, nh, SUBLANES, LANES), F32)],
        compiler_params=_params(("arbitrary", "arbitrary", "arbitrary")))(order, decay, p, p, p, cos, sin, states, do)


def _ggn_head(of, ob, gate, g):
    o = of + ob
    mu = jnp.mean(o, axis=-1, keepdims=True)
    var = jnp.mean(jnp.square(o - mu), axis=-1, keepdims=True)
    return ((o - mu) * lax.rsqrt(var + EPS) * g) * _silu(gate)


def _ggn_fwd(o2, p, gn_g, name):
    t = p.shape[0]
    tm, w, dv = _tm(), _ret_w(), RET_DV
    gb = _offsets()["g"] // w

    def body(o_ref, gate_ref, g_ref, out_ref):
        for h in range(RET_HEADS):
            sl = slice(h * dv, (h + 1) * dv)
            out_ref[:, sl] = _ggn_head(o_ref[0, :, sl], o_ref[1, :, sl], gate_ref[:, sl], g_ref[:, sl]).astype(BF16)

    return pl.pallas_call(
        body, name=name, grid=(t // tm,),
        in_specs=[pl.BlockSpec((2, tm, w), lambda i: (0, i, 0)), pl.BlockSpec((tm, w), lambda i: (i, gb)),
                  pl.BlockSpec((1, w), lambda i: (0, 0))],
        out_specs=pl.BlockSpec((tm, w), lambda i: (i, 0)), out_shape=jax.ShapeDtypeStruct((t, w), BF16),
        compiler_params=_params(("parallel",)))(o2, p, gn_g)


def _ggn_bwd(o2, p, gn_g, dmix, name):
    t = p.shape[0]
    tm, w, dv = _tm(), _ret_w(), RET_DV
    gb = _offsets()["g"] // w

    def body(o_ref, gate_ref, g_ref, dy_ref, do_ref, dgate_ref, dg_ref):
        i = pl.program_id(0)

        @pl.when(i == 0)
        def _():
            dg_ref[...] = jnp.zeros_like(dg_ref)

        for h in range(RET_HEADS):
            sl = slice(h * dv, (h + 1) * dv)
            _, vjp = jax.vjp(_ggn_head, o_ref[0, :, sl], o_ref[1, :, sl], gate_ref[:, sl], g_ref[:, sl])
            do, _, dgate, dg = vjp(dy_ref[:, sl])
            do_ref[:, sl] = do
            dgate_ref[:, sl] = dgate
            dg_ref[:, sl] += dg

    row = pl.BlockSpec((tm, w), lambda i: (i, 0))
    return pl.pallas_call(
        body, name=name, grid=(t // tm,),
        in_specs=[pl.BlockSpec((2, tm, w), lambda i: (0, i, 0)), pl.BlockSpec((tm, w), lambda i: (i, gb)),
                  pl.BlockSpec((1, w), lambda i: (0, 0)), row],
        out_specs=[row, row, pl.BlockSpec((1, w), lambda i: (0, 0))],
        out_shape=[jax.ShapeDtypeStruct((t, w), F32), jax.ShapeDtypeStruct((t, w), F32),
                   jax.ShapeDtypeStruct((1, w), F32)],
        compiler_params=_params(("arbitrary",)))(o2, p, gn_g, dmix)


def _halo(k):
    return SUBLANES * ((k // 2 + SUBLANES - 1) // SUBLANES)


def _halo_specs(width, colblock, h, tm):
    r = tm // h
    return [pl.BlockSpec((h, width), lambda i, *_: (jnp.maximum(i * r - 1, 0), colblock(*_))),
            pl.BlockSpec((tm, width), lambda i, *_: (i, colblock(*_))),
            pl.BlockSpec((h, width), lambda i, *_: (jnp.minimum((i + 1) * r, (_t_rows() // h) - 1), colblock(*_)))]


def _fill_ext(ext_ref, prev, cur, nxt, i, h, tm):
    nt = _t_rows() // tm
    ext_ref[0:h, :] = jnp.where(i >= 2, prev, 0.0)
    ext_ref[h:h + tm, :] = cur
    ext_ref[h + tm:h + tm + h, :] = jnp.where((i >= 1) & (i <= nt - 2), nxt, 0.0)


def _corr(ext_ref, w_ref, k, h, tm, flip):
    pad = k // 2
    acc = None
    for kk in range(k):
        o = h + (pad - kk if flip else kk - pad)
        term = w_ref[kk:kk + 1, :] * ext_ref[o:o + tm, :]
        acc = term if acc is None else acc + term
    return acc


def _conv_post(u2, ln_g, ln_b, pw):
    mu = jnp.mean(u2, axis=-1, keepdims=True)
    var = jnp.mean(jnp.square(u2 - mu), axis=-1, keepdims=True)
    y = (u2 - mu) * lax.rsqrt(var + EPS) * ln_g + ln_b
    return dot_nn(_silu(y), pw)


def _conv_fwd(p, dw_w, dw_b, ln_g, ln_b, pw, name):
    t = p.shape[0]
    tm, w, k = _tm(), CONV_W, CONV_K
    h = _halo(k)
    off = _offsets()
    ab, bb = off["a"] // w, off["b"] // w

    def body(ap, ac, an, bp, bc, bn, w_ref, b_ref, g_ref, beta_ref, pw_ref, u2_ref, out_ref, ext):
        i = pl.program_id(0)
        glu = lambda a, b: a * _sigmoid(b)
        _fill_ext(ext, glu(ap[...], bp[...]), glu(ac[...], bc[...]), glu(an[...], bn[...]), i, h, tm)
        u2 = _corr(ext, w_ref, k, h, tm, False) + b_ref[...]
        u2_ref[...] = u2
        out_ref[...] = _conv_post(u2, g_ref[...], beta_ref[...], pw_ref[...]).astype(BF16)

    vec = pl.BlockSpec((1, w), lambda i: (0, 0))
    row = pl.BlockSpec((tm, w), lambda i: (i, 0))
    return pl.pallas_call(
        body, name=name, grid=(t // tm,),
        in_specs=_halo_specs(w, lambda: ab, h, tm) + _halo_specs(w, lambda: bb, h, tm)
        + [pl.BlockSpec((k, w), lambda i: (0, 0)), vec, vec, vec, pl.BlockSpec((w, w), lambda i: (0, 0))],
        out_specs=[row, row],
        out_shape=[jax.ShapeDtypeStruct((t, w), F32), jax.ShapeDtypeStruct((t, w), BF16)],
        scratch_shapes=[pltpu.VMEM((tm + 2 * h, w), F32)],
        compiler_params=_params(("parallel",)))(p, p, p, p, p, p, dw_w, dw_b, ln_g, ln_b, pw)


def _conv_bwd1(u2, dmix, ln_g, ln_b, pw, name):
    t = u2.shape[0]
    tm, w = _tm(), CONV_W
    cb = _ret_w() // w

    def body(u2_ref, dy_ref, g_ref, beta_ref, pw_ref, du2_ref, dg_ref, db_ref, dpw_ref):
        i = pl.program_id(0)

        @pl.when(i == 0)
        def _():
            dg_ref[...] = jnp.zeros_like(dg_ref)
            db_ref[...] = jnp.zeros_like(db_ref)
            dpw_ref[...] = jnp.zeros_like(dpw_ref)

        _, vjp = jax.vjp(_conv_post, u2_ref[...], g_ref[...], beta_ref[...], pw_ref[...])
        du2, dg, db, dpw = vjp(dy_ref[...])
        du2_ref[...] = du2
        dg_ref[...] += dg
        db_ref[...] += db
        dpw_ref[...] += dpw

    vec = pl.BlockSpec((1, w), lambda i: (0, 0))
    row = pl.BlockSpec((tm, w), lambda i: (i, 0))
    mat = pl.BlockSpec((w, w), lambda i: (0, 0))
    return pl.pallas_call(
        body, name=name, grid=(t // tm,),
        in_specs=[row, pl.BlockSpec((tm, w), lambda i: (i, cb)), vec, vec, mat],
        out_specs=[row, vec, vec, mat],
        out_shape=[jax.ShapeDtypeStruct((t, w), F32), jax.ShapeDtypeStruct((1, w), F32),
                   jax.ShapeDtypeStruct((1, w), F32), jax.ShapeDtypeStruct((w, w), F32)],
        compiler_params=_params(("arbitrary",)))(u2, dmix, ln_g, ln_b, pw)


def _conv_bwd2(du2, p, dw_w, name):
    t = p.shape[0]
    tm, w, k = _tm(), CONV_W, CONV_K
    h = _halo(k)
    pad = k // 2
    off = _offsets()
    ab, bb = off["a"] // w, off["b"] // w

    def body(dp, dc, dn, ap, ac, an, bp, bc, bn, w_ref, da_ref, db_ref, dw_ref, dbias_ref, ext_d, ext_u):
        i = pl.program_id(0)

        @pl.when(i == 0)
        def _():
            dw_ref[...] = jnp.zeros_like(dw_ref)
            dbias_ref[...] = jnp.zeros_like(dbias_ref)

        glu = lambda a, b: a * _sigmoid(b)
        a, b, d = ac[...], bc[...], dc[...]
        _fill_ext(ext_d, dp[...], d, dn[...], i, h, tm)
        _fill_ext(ext_u, glu(ap[...], bp[...]), glu(a, b), glu(an[...], bn[...]), i, h, tm)
        du = _corr(ext_d, w_ref, k, h, tm, True)
        sg = _sigmoid(b)
        da_ref[...] = du * sg
        db_ref[...] = du * a * sg * (1.0 - sg)
        dbias_ref[...] += jnp.sum(d, axis=0, keepdims=True)
        for kk in range(k):
            o = h + kk - pad
            dw_ref[kk:kk + 1, :] += jnp.sum(d * ext_u[o:o + tm, :], axis=0, keepdims=True)

    vec = pl.BlockSpec((1, w), lambda i: (0, 0))
    row = pl.BlockSpec((tm, w), lambda i: (i, 0))
    kw = pl.BlockSpec((k, w), lambda i: (0, 0))
    return pl.pallas_call(
        body, name=name, grid=(t // tm,),
        in_specs=_halo_specs(w, lambda: 0, h, tm) + _halo_specs(w, lambda: ab, h, tm)
        + _halo_specs(w, lambda: bb, h, tm) + [kw],
        out_specs=[row, row, kw, vec],
        out_shape=[jax.ShapeDtypeStruct((t, w), F32), jax.ShapeDtypeStruct((t, w), F32),
                   jax.ShapeDtypeStruct((k, w), F32), jax.ShapeDtypeStruct((1, w), F32)],
        scratch_shapes=[pltpu.VMEM((tm + 2 * h, w), F32), pltpu.VMEM((tm + 2 * h, w), F32)],
        compiler_params=_params(("arbitrary",)))(du2, du2, du2, p, p, p, p, p, p, dw_w)


def _ffn_tc():
    return _tile(D_FF, 512, LANES)


def _ffn_act_fwd(u, dw_w, dw_b, name):
    t = u.shape[0]
    tm, k, tc = _tm(), FFN_K, _ffn_tc()
    h = _halo(k)
    nj = D_FF // tc

    def body(vp, vc, vn, gp, gc, gn, wv, wg, bv, bg, out_ref, ext_v, ext_g):
        i = pl.program_id(0)
        _fill_ext(ext_v, vp[...], vc[...], vn[...], i, h, tm)
        _fill_ext(ext_g, gp[...], gc[...], gn[...], i, h, tm)
        val = _corr(ext_v, wv, k, h, tm, False) + bv[...]
        gate = _corr(ext_g, wg, k, h, tm, False) + bg[...]
        out_ref[...] = (_silu(gate) * val).astype(BF16)

    wspec = lambda s: pl.BlockSpec((k, tc), lambda i, j: (0, j + s))
    bspec = lambda s: pl.BlockSpec((1, tc), lambda i, j: (0, j + s))
    return pl.pallas_call(
        body, name=name, grid=(t // tm, nj),
        in_specs=_halo_specs(tc, lambda j: j, h, tm) + _halo_specs(tc, lambda j: j + nj, h, tm)
        + [wspec(0), wspec(nj), bspec(0), bspec(nj)],
        out_specs=pl.BlockSpec((tm, tc), lambda i, j: (i, j)),
        out_shape=jax.ShapeDtypeStruct((t, D_FF), BF16),
        scratch_shapes=[pltpu.VMEM((tm + 2 * h, tc), F32), pltpu.VMEM((tm + 2 * h, tc), F32)],
        compiler_params=_params(("parallel", "parallel")))(u, u, u, u, u, u, dw_w, dw_w, dw_b, dw_b)


def _ffn_act_bwd1(u, da, dw_w, dw_b, name):
    t = u.shape[0]
    tm, k, tc = _tm(), FFN_K, _ffn_tc()
    h = _halo(k)
    nj = D_FF // tc

    def body(vp, vc, vn, gp, gc, gn, wv, wg, bv, bg, da_ref, dv_ref, dg_ref, ext_v, ext_g):
        i = pl.program_id(0)
        _fill_ext(ext_v, vp[...], vc[...], vn[...], i, h, tm)
        _fill_ext(ext_g, gp[...], gc[...], gn[...], i, h, tm)
        val = _corr(ext_v, wv, k, h, tm, False) + bv[...]
        gate = _corr(ext_g, wg, k, h, tm, False) + bg[...]
        _, vjp = jax.vjp(lambda a, b: _silu(b) * a, val, gate)
        dval, dgate = vjp(da_ref[...])
        dv_ref[...] = dval
        dg_ref[...] = dgate

    wspec = lambda s: pl.BlockSpec((k, tc), lambda i, j: (0, j + s))
    bspec = lambda s: pl.BlockSpec((1, tc), lambda i, j: (0, j + s))
    dc = pl.pallas_call(
        body, name=name, grid=(t // tm, nj),
        in_specs=_halo_specs(tc, lambda j: j, h, tm) + _halo_specs(tc, lambda j: j + nj, h, tm)
        + [wspec(0), wspec(nj), bspec(0), bspec(nj), pl.BlockSpec((tm, tc), lambda i, j: (i, j))],
        out_specs=[pl.BlockSpec((tm, tc), lambda i, j: (i, j)), pl.BlockSpec((tm, tc), lambda i, j: (i, j))],
        out_shape=[jax.ShapeDtypeStruct((t, D_FF), F32), jax.ShapeDtypeStruct((t, D_FF), F32)],
        scratch_shapes=[pltpu.VMEM((tm + 2 * h, tc), F32), pltpu.VMEM((tm + 2 * h, tc), F32)],
        compiler_params=_params(("parallel", "parallel")))(u, u, u, u, u, u, dw_w, dw_w, dw_b, dw_b, da)
    return dc


def _dwconv_bwd(dc, u, dw_w, colblock, name):
    t = u.shape[0]
    tm, k, tc = _tm(), FFN_K, _ffn_tc()
    h = _halo(k)
    pad = k // 2
    nj = D_FF // tc

    def body(dp, dcur, dn, up, uc, un, w_ref, du_ref, dw_ref, dbias_ref, ext_d, ext_u):
        i = pl.program_id(1)

        @pl.when(i == 0)
        def _():
            dw_ref[...] = jnp.zeros_like(dw_ref)
            dbias_ref[...] = jnp.zeros_like(dbias_ref)

        d = dcur[...]
        _fill_ext(ext_d, dp[...], d, dn[...], i, h, tm)
        _fill_ext(ext_u, up[...], uc[...], un[...], i, h, tm)
        du_ref[...] = _corr(ext_d, w_ref, k, h, tm, True).astype(BF16)
        dbias_ref[...] += jnp.sum(d, axis=0, keepdims=True)
        for kk in range(k):
            o = h + kk - pad
            dw_ref[kk:kk + 1, :] += jnp.sum(d * ext_u[o:o + tm, :], axis=0, keepdims=True)

    def hs(cb):
        r = tm // h
        return [pl.BlockSpec((h, tc), lambda j, i: (jnp.maximum(i * r - 1, 0), cb(j))),
                pl.BlockSpec((tm, tc), lambda j, i: (i, cb(j))),
                pl.BlockSpec((h, tc), lambda j, i: (jnp.minimum((i + 1) * r, (_t_rows() // h) - 1), cb(j)))]

    return pl.pallas_call(
        body, name=name, grid=(nj, t // tm),
        in_specs=hs(lambda j: j) + hs(lambda j: j + colblock) + [pl.BlockSpec((k, tc), lambda j, i: (0, j + colblock))],
        out_specs=[pl.BlockSpec((tm, tc), lambda j, i: (i, j)), pl.BlockSpec((k, tc), lambda j, i: (0, j)),
                   pl.BlockSpec((1, tc), lambda j, i: (0, j))],
        out_shape=[jax.ShapeDtypeStruct((t, D_FF), BF16), jax.ShapeDtypeStruct((k, D_FF), F32),
                   jax.ShapeDtypeStruct((1, D_FF), F32)],
        scratch_shapes=[pltpu.VMEM((tm + 2 * h, tc), F32), pltpu.VMEM((tm + 2 * h, tc), F32)],
        compiler_params=_params(("parallel", "arbitrary")))(dc, dc, dc, u, u, u, dw_w)


def _na_consts():
    nc = 2 * NA_COLS - 1
    e = np.zeros((nc + 1, GRID_W, NA_ROWS * GRID_W), np.float32)
    for q in range(GRID_W):
        for w in range(GRID_W):
            c = w - q + NA_COLS - 1
            if 0 <= c < nc:
                e[c, q, w::GRID_W] = 1.0
    seg = np.zeros((NA_ROWS, NA_ROWS * GRID_W), np.float32)
    for kh in range(NA_ROWS):
        seg[kh, kh * GRID_W:(kh + 1) * GRID_W] = 1.0
    return jnp.asarray(e), jnp.asarray(seg)


def _na_geometry(rq):
    ncb = CTX_LEN // GRID_W
    rows_n = SEQ // GRID_W
    r = jnp.maximum(rq - ncb, 0)
    kstart = jnp.clip(r - NA_ROWS // 2, 0, rows_n - NA_ROWS)
    base = kstart - r + NA_ROWS - 1
    return rq >= ncb, kstart, base


def _na_core(q, kl, vl, kc, vc, bias, mask):
    qs = q * (NA_DH ** -0.5)
    s_l = jnp.where(mask, dot_nt(qs, kl) + bias, NEG)
    s_c = dot_nt(qs, kc)
    m = lax.stop_gradient(jnp.maximum(jnp.max(s_l, axis=1, keepdims=True), jnp.max(s_c, axis=1, keepdims=True)))
    e_l, e_c = jnp.exp(s_l - m), jnp.exp(s_c - m)
    inv = 1.0 / (jnp.sum(e_l, axis=1, keepdims=True) + jnp.sum(e_c, axis=1, keepdims=True))
    return dot_nn(e_l * inv, vl) + dot_nn(e_c * inv, vc)


def _na_mask(is_lat):
    nl = NA_ROWS * GRID_W
    q = lax.broadcasted_iota(jnp.int32, (GRID_W, nl), 0)
    w = lax.broadcasted_iota(jnp.int32, (GRID_W, nl), 1) % GRID_W
    cs = jnp.clip(q - NA_COLS // 2, 0, GRID_W - NA_COLS)
    return (w >= cs) & (w < cs + NA_COLS) & is_lat


def _na_bias(rb_ref, e_ref, seg_ref, rr_ref):
    rr_ref[...] = lax.dot_general(rb_ref[...], seg_ref[...], (((0,), (0,)), ((), ())),
                                  precision=lax.Precision.HIGHEST, preferred_element_type=F32)
    bias = None
    for c in range(2 * NA_COLS - 1):
        term = e_ref[c] * rr_ref[c:c + 1, :]
        bias = term if bias is None else bias + term
    return bias


def _na_specs(p_offsets, ncb):
    dh = NA_DH
    t = _t_rows()
    nl = NA_ROWS * GRID_W
    qb, kb, vb = (p_offsets[n] // dh for n in ("nq", "nk", "nv"))
    return [pl.BlockSpec((GRID_W, dh), lambda h, r: (r, qb + h)),
            pl.BlockSpec((t, dh), lambda h, r: (0, kb + h)),
            pl.BlockSpec((t, dh), lambda h, r: (0, vb + h)),
            pl.BlockSpec((None, None, NA_ROWS, LANES), lambda h, r: (h, _na_geometry(r)[2], 0, 0)),
            pl.BlockSpec((2 * NA_COLS, GRID_W, nl), lambda h, r: (0, 0, 0)),
            pl.BlockSpec((NA_ROWS, nl), lambda h, r: (0, 0))]


def _na_fwd(p, rb, e, seg, name):
    t = p.shape[0]
    dh, nl = NA_DH, NA_ROWS * GRID_W
    ncb = CTX_LEN // GRID_W

    def body(q_ref, k_ref, v_ref, rb_ref, e_ref, seg_ref, out_ref, rr):
        rq = pl.program_id(1)
        is_lat, kstart, _ = _na_geometry(rq)
        start = pl.multiple_of(CTX_LEN + kstart * GRID_W, GRID_W)
        bias = _na_bias(rb_ref, e_ref, seg_ref, rr)
        out = _na_core(q_ref[...], k_ref[pl.ds(start, nl), :], v_ref[pl.ds(start, nl), :],
                       k_ref[0:CTX_LEN, :], v_ref[0:CTX_LEN, :], bias, _na_mask(is_lat))
        out_ref[...] = out.astype(BF16)

    return pl.pallas_call(
        body, name=name, grid=(NA_HEADS, t // GRID_W), in_specs=_na_specs(_offsets(), ncb),
        out_specs=pl.BlockSpec((GRID_W, dh), lambda h, r: (r, h)),
        out_shape=jax.ShapeDtypeStruct((t, _na_w()), BF16),
        scratch_shapes=[pltpu.VMEM((LANES, nl), F32)],
        compiler_params=_params(("parallel", "arbitrary")))(p, p, p, rb, e, seg)


def _na_bwd(p, rb, e, seg, dmix, name):
    t = p.shape[0]
    dh, nl = NA_DH, NA_ROWS * GRID_W
    ncb = CTX_LEN // GRID_W
    ob = (_ret_w() + CONV_W) // dh

    def body(q_ref, k_ref, v_ref, rb_ref, e_ref, seg_ref, dy_ref, dq_ref, dk_ref, dv_ref, drb_ref, rr, drr):
        rq = pl.program_id(1)
        is_lat, kstart, base = _na_geometry(rq)
        _, _, prev_base = _na_geometry(rq - 1)
        start = pl.multiple_of(CTX_LEN + kstart * GRID_W, GRID_W)

        @pl.when(rq == 0)
        def _():
            dk_ref[...] = jnp.zeros_like(dk_ref)
            dv_ref[...] = jnp.zeros_like(dv_ref)

        @pl.when((rq == 0) | (base != prev_base))
        def _():
            drb_ref[...] = jnp.zeros_like(drb_ref)

        bias = _na_bias(rb_ref, e_ref, seg_ref, rr)
        mask = _na_mask(is_lat)
        _, vjp = jax.vjp(lambda *a: _na_core(*a, mask), q_ref[...], k_ref[pl.ds(start, nl), :],
                         v_ref[pl.ds(start, nl), :], k_ref[0:CTX_LEN, :], v_ref[0:CTX_LEN, :], bias)
        dq, dkl, dvl, dkc, dvc, dbias = vjp(dy_ref[...])
        dq_ref[...] = dq
        dk_ref[pl.ds(start, nl), :] += dkl
        dv_ref[pl.ds(start, nl), :] += dvl
        dk_ref[0:CTX_LEN, :] += dkc
        dv_ref[0:CTX_LEN, :] += dvc
        drr[...] = jnp.zeros_like(drr)
        for c in range(2 * NA_COLS - 1):
            drr[c:c + 1, :] = jnp.sum(dbias * e_ref[c], axis=0, keepdims=True)
        drb_ref[...] += lax.dot_general(seg_ref[...], drr[...], (((1,), (1,)), ((), ())),
                                        precision=lax.Precision.HIGHEST, preferred_element_type=F32)

    return pl.pallas_call(
        body, name=name, grid=(NA_HEADS, t // GRID_W),
        in_specs=_na_specs(_offsets(), ncb) + [pl.BlockSpec((GRID_W, dh), lambda h, r: (r, ob + h))],
        out_specs=[pl.BlockSpec((GRID_W, dh), lambda h, r: (r, h)), pl.BlockSpec((t, dh), lambda h, r: (0, h)),
                   pl.BlockSpec((t, dh), lambda h, r: (0, h)),
                   pl.BlockSpec((None, None, NA_ROWS, LANES), lambda h, r: (h, _na_geometry(r)[2], 0, 0))],
        out_shape=[jax.ShapeDtypeStruct((t, _na_w()), F32), jax.ShapeDtypeStruct((t, _na_w()), F32),
                   jax.ShapeDtypeStruct((t, _na_w()), F32),
                   jax.ShapeDtypeStruct((NA_HEADS, NA_ROWS, NA_ROWS, LANES), F32)],
        scratch_shapes=[pltpu.VMEM((LANES, nl), F32), pltpu.VMEM((LANES, nl), F32)],
        compiler_params=_params(("parallel", "arbitrary")))(p, p, p, rb, e, seg, dmix)


def _rpb_rows(rpb):
    nr, nc = 2 * NA_ROWS - 1, 2 * NA_COLS - 1
    pad = jnp.pad(rpb, ((0, 0), (0, 0), (0, LANES - nc)))
    return jnp.stack([pad[:, b:b + NA_ROWS] for b in range(NA_ROWS)], axis=1)


def _rpb_rows_t(drb):
    nr, nc = 2 * NA_ROWS - 1, 2 * NA_COLS - 1
    out = jnp.zeros((NA_HEADS, nr, LANES), F32)
    for b in range(NA_ROWS):
        out = out.at[:, b:b + NA_ROWS].add(drb[:, b])
    return out[:, :, :nc]


def _assemble_dp(dqr, dkr, dvr, dgate, da, db, dnq, dnk, dnv, name):
    t = dgate.shape[0]
    tm = _tm()
    off = _offsets()
    sizes = dict(q=_ret_qk_w(), k=_ret_qk_w(), v=_ret_w(), g=_ret_w(), a=CONV_W, b=CONV_W, nq=_na_w(), nk=_na_w(), nv=_na_w())

    def body(q_ref, k_ref, v_ref, g_ref, a_ref, b_ref, nq_ref, nk_ref, nv_ref, o_ref):
        def put(n, val):
            o_ref[:, off[n]:off[n] + sizes[n]] = val.astype(BF16)

        put("q", q_ref[0] + q_ref[1])
        put("k", k_ref[0] + k_ref[1])
        put("v", v_ref[0] + v_ref[1])
        put("g", g_ref[...])
        put("a", a_ref[...])
        put("b", b_ref[...])
        put("nq", nq_ref[...])
        put("nk", nk_ref[...])
        put("nv", nv_ref[...])

    two = lambda w: pl.BlockSpec((2, tm, w), lambda i: (0, i, 0))
    one = lambda w: pl.BlockSpec((tm, w), lambda i: (i, 0))
    return pl.pallas_call(
        body, name=name, grid=(t // tm,),
        in_specs=[two(sizes["q"]), two(sizes["k"]), two(sizes["v"]), one(sizes["g"]), one(CONV_W), one(CONV_W),
                  one(_na_w()), one(_na_w()), one(_na_w())],
        out_specs=one(_d_in()), out_shape=jax.ShapeDtypeStruct((t, _d_in()), BF16),
        compiler_params=_params(("parallel",)))(dqr, dkr, dvr, dgate, da, db, dnq, dnk, dnv)


def _adamw(w, m, v, gs, name):
    nl, r, c = w.shape
    g_n = gs.shape[1]
    tr = _tile(r, max(2 * SUBLANES, (2 * 1024 * 1024) // (4 * c) // (2 * SUBLANES) * (2 * SUBLANES)), 2 * SUBLANES)
    c1 = 1.0 - ADAM_B1 ** ADAM_STEP
    c2 = 1.0 - ADAM_B2 ** ADAM_STEP

    def body(w_ref, m_ref, v_ref, g_ref, go_ref, d_ref, mo_ref, vo_ref):
        g = g_ref[0].astype(F32)
        for j in range(1, g_n):
            g = g + g_ref[j].astype(F32)
        mn = ADAM_B1 * m_ref[...] + (1.0 - ADAM_B1) * g
        vn = ADAM_B2 * v_ref[...] + (1.0 - ADAM_B2) * (g * g)
        m_hat = mn / c1
        v_hat = vn / c2
        go_ref[...] = g
        d_ref[...] = -ADAM_LR * (m_hat / (jnp.sqrt(v_hat) + ADAM_EPS) + ADAM_WD * w_ref[...])
        mo_ref[...] = mn
        vo_ref[...] = vn

    blk = pl.BlockSpec((None, tr, c), lambda l, i: (l, i, 0))
    sds = jax.ShapeDtypeStruct((nl, r, c), F32)
    return pl.pallas_call(
        body, name=name, grid=(nl, r // tr),
        in_specs=[blk, blk, blk, pl.BlockSpec((None, g_n, tr, c), lambda l, i: (l, 0, i, 0))],
        out_specs=[blk, blk, blk, blk], out_shape=[sds, sds, sds, sds],
        compiler_params=_params(("parallel", "parallel")))(w, m, v, gs)


def _sum_devices(g, name):
    _, r, c = g.shape
    tr = _tile(r, 512, SUBLANES)

    def body(g_ref, o_ref):
        acc = g_ref[0]
        for j in range(1, N_DEV):
            acc = acc + g_ref[j]
        o_ref[...] = acc

    return pl.pallas_call(body, name=name, grid=(r // tr,), in_specs=[pl.BlockSpec((N_DEV, tr, c), lambda i: (0, i, 0))],
                          out_specs=pl.BlockSpec((tr, c), lambda i: (i, 0)), out_shape=jax.ShapeDtypeStruct((r, c), F32),
                          compiler_params=_params(("parallel",)))(g)


def _ada_fwd(c16, w_ada, b_shard, name):
    nl, d, cs = w_ada.shape
    tk = _tile(d, 512, LANES)
    nk = d // tk

    def body(c_ref, w_ref, b_ref, o_ref):
        kk = pl.program_id(1)

        @pl.when(kk == 0)
        def _():
            o_ref[...] = jnp.broadcast_to(b_ref[...], o_ref.shape)

        o_ref[...] += _dg(_silu(c_ref[...]), w_ref[...], 1, 0)

    return pl.pallas_call(
        body, name=name, grid=(nl, nk),
        in_specs=[pl.BlockSpec((16, tk), lambda l, kk: (0, kk)), pl.BlockSpec((None, tk, cs), lambda l, kk: (l, kk, 0)),
                  pl.BlockSpec((None, 1, cs), lambda l, kk: (l, 0, 0))],
        out_specs=pl.BlockSpec((None, 16, cs), lambda l, kk: (l, 0, 0)),
        out_shape=jax.ShapeDtypeStruct((nl, 16, cs), F32),
        compiler_params=_params(("parallel", "arbitrary")))(c16, w_ada, b_shard)


def _ada_bwd(c16, dm16, w_ada, name):
    nl, d, cs = w_ada.shape
    td = _tile(d, 512, LANES)

    def body(c_ref, dm_ref, w_ref, gw_ref, dc_ref):
        cv = c_ref[...]
        s, vjp = jax.vjp(_silu, cv)
        gw_ref[...] = _dg(s, dm_ref[...], 0, 0)
        ds = _dg(dm_ref[...], w_ref[...], 1, 1)
        dc_ref[...] = vjp(ds)[0]

    return pl.pallas_call(
        body, name=name, grid=(nl, d // td),
        in_specs=[pl.BlockSpec((16, td), lambda l, i: (0, i)), pl.BlockSpec((None, 16, cs), lambda l, i: (l, 0, 0)),
                  pl.BlockSpec((None, td, cs), lambda l, i: (l, i, 0))],
        out_specs=[pl.BlockSpec((None, td, cs), lambda l, i: (l, i, 0)), pl.BlockSpec((None, 16, td), lambda l, i: (l, 0, i))],
        out_shape=[jax.ShapeDtypeStruct((nl, d, cs), F32), jax.ShapeDtypeStruct((nl, 16, d), F32)],
        compiler_params=_params(("parallel", "parallel")))(c16, dm16, w_ada)


def _pack(arrays, row_align):
    flat = jnp.concatenate([a.reshape(-1).astype(F32) for a in arrays])
    n = flat.shape[0]
    per = LANES * row_align
    padded = ((n + per - 1) // per) * per
    return jnp.pad(flat, (0, padded - n)).reshape(padded // LANES, LANES)


def _unpack(packed, shapes):
    flat = packed.reshape(-1)
    out, o = [], 0
    for s in shapes:
        n = int(np.prod(s))
        out.append(flat[o:o + n].reshape(s))
        o += n
    return out


def _rope_tables():
    half, nf = RET_DK // 2, RET_DK // 4
    pos = jnp.arange(SEQ)
    row = (pos // GRID_W).astype(F32)
    col = (pos % GRID_W).astype(F32)
    inv = ROPE_BASE ** (-jnp.arange(nf, dtype=F32) / nf)
    ar, ac = row[:, None] * inv[None, :], col[:, None] * inv[None, :]
    cos = jnp.concatenate([jnp.cos(ar), jnp.cos(ar), jnp.cos(ac), jnp.cos(ac)], axis=-1)
    sin = jnp.concatenate([-jnp.sin(ar), jnp.sin(ar), -jnp.sin(ac), jnp.sin(ac)], axis=-1)
    cos = jnp.concatenate([jnp.ones((CTX_LEN, RET_DK), F32), cos], axis=0)
    sin = jnp.concatenate([jnp.zeros((CTX_LEN, RET_DK), F32), sin], axis=0)
    return cos, sin


def _layer_fwd(l, x, mod4, w, cst):
    n = lambda s: f"l{l}_{s}"
    h1 = _normmod_fwd(x, w["norm1_g"], mod4, 0, n("norm1"))
    p = _mm(h1, w["w_in"], n("proj_in"))
    o2, states = _ret_fwd(p, cst["cos"], cst["sin"], w["ret_decay"], cst["order"], n("ret_fwd"))
    ret_out = _ggn_fwd(o2, p, w["ret_gn_g"], n("ret_gn"))
    u2, conv_out = _conv_fwd(p, w["conv_dw_w"], w["conv_dw_b"], w["conv_ln_g"], w["conv_ln_b"], w["conv_pw"], n("conv_fwd"))
    na_out = _na_fwd(p, w["rb"], cst["e"], cst["seg"], n("na_fwd"))
    mix = jnp.concatenate([ret_out, conv_out, na_out], axis=1)
    g1 = _mm(mix, w["w_out"], n("proj_out"))
    x1 = _gate_res_fwd(x, g1, mod4, 2, n("res1"))
    h2 = _normmod_fwd(x1, w["norm2_g"], mod4, 1, n("norm2"))
    u = _mm(h2, w["ffn_up"], n("ffn_up"), b3=True)
    a = _ffn_act_fwd(u, w["ffn_dw_w"], w["ffn_dw_b"], n("ffn_act"))
    f = _mm(a, w["ffn_down"], n("ffn_down"))
    x2 = _gate_res_fwd(x1, f, mod4, 5, n("res2"))
    saved = dict(x=x, h1=h1, p=p, o2=o2, states=states, u2=u2, mix=mix, g1=g1, x1=x1, h2=h2, u=u, a=a, f=f)
    return x2, saved


def _layer_bwd(l, dx2, s, mod4, w, cst):
    n = lambda t: f"l{l}_{t}"
    d = D_MODEL
    nj = D_FF // _ffn_tc()
    dfg, dg2 = _gate_res_bwd(dx2, s["f"], mod4, 5, n("res2_bwd"))
    da = _mm(dfg, w["ffn_down"], n("ffn_down_dx"), tb=True)
    d_ffn_down = _mm(s["a"], dfg, n("ffn_down_dw"), ta=True, out_dtype=BF16)
    dcv, dcg = _ffn_act_bwd1(s["u"], da, w["ffn_dw_w"], w["ffn_dw_b"], n("ffn_act_bwd"))
    duv, dwv, dbv = _dwconv_bwd(dcv, s["u"], w["ffn_dw_w"], 0, n("ffn_dw_bwd_val"))
    dug, dwg, dbg = _dwconv_bwd(dcg, s["u"], w["ffn_dw_w"], nj, n("ffn_dw_bwd_gate"))
    du = jnp.concatenate([duv, dug], axis=1)
    d_ffn_dw_w = jnp.concatenate([dwv, dwg], axis=1)
    d_ffn_dw_b = jnp.concatenate([dbv, dbg], axis=1)[0]
    dh2 = _mm(du, w["ffn_up"], n("ffn_up_dx"), tb=True, b3=True)
    d_ffn_up = _mm(s["h2"], du, n("ffn_up_dw"), ta=True, o3=True, out_dtype=BF16)
    dx1, dn2, dsh2, dsc2 = _normmod_bwd(s["x1"], w["norm2_g"], mod4, 1, dh2, dx2, n("norm2_bwd"))
    dgg, dg1 = _gate_res_bwd(dx1, s["g1"], mod4, 2, n("res1_bwd"))
    dmix = _mm(dgg, w["w_out"], n("proj_out_dx"), tb=True)
    d_w_out = _mm(s["mix"], dgg, n("proj_out_dw"), ta=True, out_dtype=BF16)
    do, dgate, dgn = _ggn_bwd(s["o2"], s["p"], w["ret_gn_g"], dmix, n("ret_gn_bwd"))
    dqr, dkr, dvr, ddec = _ret_bwd(s["p"], cst["cos"], cst["sin"], w["ret_decay"], cst["order"], s["states"], do, n("ret_bwd"))
    du2, dlng, dlnb, dpw = _conv_bwd1(s["u2"], dmix, w["conv_ln_g"], w["conv_ln_b"], w["conv_pw"], n("conv_bwd1"))
    dca, dcb, ddww, ddwb = _conv_bwd2(du2, s["p"], w["conv_dw_w"], n("conv_bwd2"))
    dnq, dnk, dnv, drb = _na_bwd(s["p"], w["rb"], cst["e"], cst["seg"], dmix, n("na_bwd"))
    dp = _assemble_dp(dqr, dkr, dvr, dgate, dca, dcb, dnq, dnk, dnv, n("dproj"))
    dh1 = _mm(dp, w["w_in"], n("proj_in_dx"), tb=True)
    d_w_in = _cols_to_shards(_mm(s["h1"], dp, n("proj_in_dw"), ta=True), n("proj_in_dw_shards"))
    dx, dn1, dsh1, dsc1 = _normmod_bwd(s["x"], w["norm1_g"], mod4, 0, dh1, dx1, n("norm1_bwd"))
    dmod = jnp.concatenate([dsh1, dsc1, dg1, dsh2, dsc2, dg2], axis=1)
    big = dict(w_in=d_w_in, w_out=d_w_out.reshape(N_DEV, _d_mix() // N_DEV, d), ffn_up=d_ffn_up,
               ffn_down=d_ffn_down.reshape(N_DEV, D_FF // N_DEV, d))
    small = dict(norm1_g=dn1[0], ret_decay=ddec[:, :, 0, 0], ret_gn_g=dgn[0], conv_dw_w=ddww, conv_dw_b=ddwb[0],
                 conv_ln_g=dlng[0], conv_ln_b=dlnb[0], conv_pw=dpw, na_rpb=_rpb_rows_t(drb), norm2_g=dn2[0],
                 ffn_dw_w=d_ffn_dw_w, ffn_dw_b=d_ffn_dw_b)
    return dx, dmod, big, small


def _d_mix():
    return _ret_w() + CONV_W + _na_w()


_SMALL = ["c_ctx", "b_ada", "norm1_g", "ret_decay", "ret_gn_g", "conv_dw_w", "conv_dw_b", "conv_ln_g", "conv_ln_b",
          "conv_pw", "na_rpb", "norm2_g", "ffn_dw_w", "ffn_dw_b", "final_g"]
_SMALL_SHARD_AXIS = {"conv_dw_w": 2, "conv_pw": 1, "ffn_dw_w": 2}


def kernel(x, c, ctx, c_ctx, w_ada, b_ada, norm1_g, w_in, ret_decay, ret_gn_g, conv_dw_w, conv_dw_b, conv_ln_g, conv_ln_b, conv_pw, na_rpb, w_out, norm2_g, ffn_up, ffn_dw_w, ffn_dw_b, ffn_down, final_g, loss_target, m_c_ctx, m_w_ada, m_b_ada, m_norm1_g, m_w_in, m_ret_decay, m_ret_gn_g, m_conv_dw_w, m_conv_dw_b, m_conv_ln_g, m_conv_ln_b, m_conv_pw, m_na_rpb, m_w_out, m_norm2_g, m_ffn_up, m_ffn_dw_w, m_ffn_dw_b, m_ffn_down, m_final_g, v_c_ctx, v_w_ada, v_b_ada, v_norm1_g, v_w_in, v_ret_decay, v_ret_gn_g, v_conv_dw_w, v_conv_dw_b, v_conv_ln_g, v_conv_ln_b, v_conv_pw, v_na_rpb, v_w_out, v_norm2_g, v_ffn_up, v_ffn_dw_w, v_ffn_dw_b, v_ffn_down, v_final_g):
    d, nl = D_MODEL, DEPTH
    cs = 6 * d // N_DEV
    me = _my_index()
    weights = dict(c_ctx=c_ctx, w_ada=w_ada, b_ada=b_ada, norm1_g=norm1_g, w_in=w_in, ret_decay=ret_decay, ret_gn_g=ret_gn_g,
                   conv_dw_w=conv_dw_w, conv_dw_b=conv_dw_b, conv_ln_g=conv_ln_g, conv_ln_b=conv_ln_b, conv_pw=conv_pw,
                   na_rpb=na_rpb, w_out=w_out, norm2_g=norm2_g, ffn_up=ffn_up, ffn_dw_w=ffn_dw_w, ffn_dw_b=ffn_dw_b,
                   ffn_down=ffn_down, final_g=final_g)
    mom = dict(c_ctx=m_c_ctx, w_ada=m_w_ada, b_ada=m_b_ada, norm1_g=m_norm1_g, w_in=m_w_in, ret_decay=m_ret_decay,
               ret_gn_g=m_ret_gn_g, conv_dw_w=m_conv_dw_w, conv_dw_b=m_conv_dw_b, conv_ln_g=m_conv_ln_g,
               conv_ln_b=m_conv_ln_b, conv_pw=m_conv_pw, na_rpb=m_na_rpb, w_out=m_w_out, norm2_g=m_norm2_g,
               ffn_up=m_ffn_up, ffn_dw_w=m_ffn_dw_w, ffn_dw_b=m_ffn_dw_b, ffn_down=m_ffn_down, final_g=m_final_g)
    var = dict(c_ctx=v_c_ctx, w_ada=v_w_ada, b_ada=v_b_ada, norm1_g=v_norm1_g, w_in=v_w_in, ret_decay=v_ret_decay,
               ret_gn_g=v_ret_gn_g, conv_dw_w=v_conv_dw_w, conv_dw_b=v_conv_dw_b, conv_ln_g=v_conv_ln_g,
               conv_ln_b=v_conv_ln_b, conv_pw=v_conv_pw, na_rpb=v_na_rpb, w_out=v_w_out, norm2_g=v_norm2_g,
               ffn_up=v_ffn_up, ffn_dw_w=v_ffn_dw_w, ffn_dw_b=v_ffn_dw_b, ffn_down=v_ffn_down, final_g=v_final_g)

    big_names = ["w_in", "w_out", "ffn_up", "ffn_down"]
    shards_bf16 = []
    for l in range(nl):
        for nm in big_names:
            shards_bf16.append(_cast_bf16(weights[nm][l], f"cast_{nm}{l}"))
    small_sharded = _pack([conv_dw_w, conv_pw, ffn_dw_w], SUBLANES)
    c_rows = jnp.pad(c, ((0, SUBLANES - 1), (0, 0)))
    gathered = _all_gather([c_rows] + shards_bf16 + [small_sharded], "gather_weights")
    c_all = gathered[0][:, 0, :]
    big = {nm: [gathered[1 + l * len(big_names) + i] for l in range(nl)] for i, nm in enumerate(big_names)}
    sm = [_unpack(gathered[-1][j], [conv_dw_w.shape, conv_pw.shape, ffn_dw_w.shape]) for j in range(N_DEV)]
    full_conv_dw_w = jnp.concatenate([s[0] for s in sm], axis=2)
    full_conv_pw = jnp.concatenate([s[1] for s in sm], axis=1)
    full_ffn_dw_w = jnp.concatenate([s[2] for s in sm], axis=2)

    c16 = jnp.concatenate([c_all, jnp.broadcast_to(c_ctx[None, :], (N_DEV, d))], axis=0)
    b_shard = lax.dynamic_slice_in_dim(b_ada, me * cs, cs, axis=1)[:, None, :]
    m_shard = _ada_fwd(c16, w_ada, b_shard, "ada_fwd")
    m_all = _all_gather([m_shard.reshape(nl * 16, cs)], "gather_mod")[0]
    m_full = m_all.reshape(N_DEV, nl, 16, cs).transpose(1, 2, 0, 3).reshape(nl, 16, 6 * d)
    m_lat = lax.dynamic_index_in_dim(m_full, me, axis=1, keepdims=False)
    mod = jnp.stack([m_full[:, N_DEV], m_lat], axis=1).reshape(nl, 2, 6, 1, d)

    cos, sin = _rope_tables()
    e, seg = _na_consts()
    cst = dict(cos=cos, sin=sin, order=_chunk_order(), e=e, seg=seg)
    layer_w = []
    for l in range(nl):
        layer_w.append(dict(
            norm1_g=norm1_g[l][None], norm2_g=norm2_g[l][None], ret_decay=ret_decay[l], ret_gn_g=ret_gn_g[l][None],
            conv_dw_w=full_conv_dw_w[l], conv_dw_b=conv_dw_b[l][None], conv_ln_g=conv_ln_g[l][None],
            conv_ln_b=conv_ln_b[l][None], conv_pw=full_conv_pw[l], rb=_rpb_rows(na_rpb[l]),
            ffn_dw_w=full_ffn_dw_w[l], ffn_dw_b=ffn_dw_b[l][None],
            w_in=_cols_from_shards(big["w_in"][l], f"l{l}_w_in_cols"),
            w_out=big["w_out"][l].reshape(_d_mix(), d), ffn_up=big["ffn_up"][l],
            ffn_down=big["ffn_down"][l].reshape(D_FF, d)))

    xs = jnp.concatenate([ctx[0], x[0]], axis=0)
    saved = []
    for l in range(nl):
        xs, sv = _layer_fwd(l, xs, mod[l], layer_w[l], cst)
        saved.append(sv)
    loss_tile, dxs, dfinal = _loss_head(xs, final_g[None], loss_target[0], "loss_head")
    loss = lax.psum(loss_tile[0, 0], ("x", "y", "c"))

    dmods, bigs, smalls = [None] * nl, [None] * nl, [None] * nl
    for l in reversed(range(nl)):
        dxs, dmods[l], bigs[l], smalls[l] = _layer_bwd(l, dxs, saved[l], mod[l], layer_w[l], cst)
    grad_x = dxs[CTX_LEN:][None]

    dm_mine = jnp.stack(dmods).reshape(nl * 2, 6 * d)
    dm_rows = jnp.pad(dm_mine, ((0, SUBLANES - nl * 2), (0, 0)))
    dm_all = _all_gather([dm_rows], "gather_dmod")[0][:, :nl * 2].reshape(N_DEV, nl, 2, 6 * d)
    dm16_full = jnp.concatenate([dm_all[:, :, 1].transpose(1, 0, 2), dm_all[:, :, 0].transpose(1, 0, 2)], axis=1)
    dm16 = lax.dynamic_slice_in_dim(dm16_full, me * cs, cs, axis=2)
    g_w_ada, dc16 = _ada_bwd(c16, dm16, w_ada, "ada_bwd")

    small_grads = dict(
        c_ctx=jnp.sum(dc16[:, N_DEV:], axis=(0, 1)),
        b_ada=jnp.sum(jnp.stack(dmods).reshape(nl, 2, 6 * d), axis=1),
        final_g=dfinal[0])
    for nm in _SMALL:
        if nm not in small_grads:
            small_grads[nm] = jnp.stack([smalls[l][nm] for l in range(nl)])
    shapes_full = [small_grads[nm].shape for nm in _SMALL]
    packed = _pack([small_grads[nm] for nm in _SMALL], 512)
    summed = _sum_devices(_all_gather([packed], "gather_small_grads")[0], "sum_small_grads")
    g_small = dict(zip(_SMALL, _unpack(summed, shapes_full)))
    for nm, ax in _SMALL_SHARD_AXIS.items():
        n_sh = weights[nm].shape[ax]
        g_small[nm] = lax.dynamic_slice_in_dim(g_small[nm], me * n_sh, n_sh, axis=ax)
    shapes_own = [weights[nm].shape for nm in _SMALL]
    pk = lambda src: _pack([src[nm] for nm in _SMALL], 2 * SUBLANES)[None]
    res_small = _adamw(pk(weights), pk(mom), pk(var), pk(g_small)[:, None], "adamw_small")
    out_small = [dict(zip(_SMALL, _unpack(r[0], shapes_own))) for r in res_small]

    exchanged = _grad_exchange([[bigs[l][nm] for l in range(nl)] for nm in big_names], "exchange_grads")
    out_big = {}
    for nm, gx in zip(big_names, exchanged):
        wshape = weights[nm].shape
        r2 = (nl, wshape[1], wshape[2])
        out_big[nm] = _adamw(weights[nm], mom[nm], var[nm], gx.reshape((nl, N_DEV) + r2[1:]), f"adamw_{nm}")
    out_big["w_ada"] = _adamw(w_ada, m_w_ada, v_w_ada, g_w_ada[:, None], "adamw_w_ada")

    names = ["c_ctx", "w_ada", "b_ada", "norm1_g", "w_in", "ret_decay", "ret_gn_g", "conv_dw_w", "conv_dw_b", "conv_ln_g",
             "conv_ln_b", "conv_pw", "na_rpb", "w_out", "norm2_g", "ffn_up", "ffn_dw_w", "ffn_dw_b", "ffn_down", "final_g"]
    outs = [loss, grad_x]
    for kind in range(4):
        for nm in names:
            outs.append(out_big[nm][kind] if nm in out_big else out_small[kind][nm])
    return tuple(outs)
```

```python
import functools
import math

import numpy as np
import jax
import jax.numpy as jnp
from jax import lax
from jax.experimental import pallas as pl
from jax.experimental.pallas import tpu as pltpu

D_MODEL = 2048
SEQ = 4096
DEPTH = 2
GRID_W = 64
CTX_LEN = 256
RET_HEADS = 4
RET_DK = 128
RET_DV = 256
RET_CHUNK = 128
CONV_W = 512
CONV_K = 31
NA_HEADS = 4
NA_DH = 128
NA_ROWS = 8
NA_COLS = 16
D_FF = 5632
FFN_K = 3
ROPE_BASE = 10000.0
EPS = 1e-6
ADAM_LR = 0.001
ADAM_B1 = 0.9
ADAM_B2 = 0.999
ADAM_EPS = 1e-08
ADAM_WD = 0.01
ADAM_STEP = 10
N_DEV = 8

LANES = 128
SUBLANES = 8
VMEM_LIMIT = 56 * 1024 * 1024

F32 = jnp.float32
BF16 = jnp.bfloat16
MESH = pl.DeviceIdType.MESH
NEG = -1e30


def _ret_qk_w():
    return RET_HEADS * RET_DK


def _ret_w():
    return RET_HEADS * RET_DV


def _na_w():
    return NA_HEADS * NA_DH


def _d_in():
    return 2 * _ret_qk_w() + 2 * _ret_w() + 2 * CONV_W + 3 * _na_w()


def _offsets():
    sizes = [_ret_qk_w(), _ret_qk_w(), _ret_w(), _ret_w(), CONV_W, CONV_W, _na_w(), _na_w(), _na_w()]
    offs = [0]
    for s in sizes[:-1]:
        offs.append(offs[-1] + s)
    return dict(zip(["q", "k", "v", "g", "a", "b", "nq", "nk", "nv"], offs))


def _t_rows():
    return CTX_LEN + SEQ


def _tm():
    return CTX_LEN


def _params(sem=None):
    kw = dict(vmem_limit_bytes=VMEM_LIMIT)
    if sem is not None:
        kw["dimension_semantics"] = sem
    return pltpu.CompilerParams(**kw)


def _tile(n, pref, align):
    best = None
    for t in range(align, min(n, pref) + 1, align):
        if n % t == 0:
            best = t
    return best if best is not None else n


def _dg(a, b, ca, cb):
    return lax.dot_general(a.astype(BF16), b.astype(BF16), (((ca,), (cb,)), ((), ())), preferred_element_type=F32)


@jax.custom_vjp
def dot_nn(a, b):
    return _dg(a, b, 1, 0)


dot_nn.defvjp(lambda a, b: (_dg(a, b, 1, 0), (a, b)),
              lambda r, g: (_dg(g, r[1], 1, 1), _dg(r[0], g, 0, 0)))


@jax.custom_vjp
def dot_nt(a, b):
    return _dg(a, b, 1, 1)


dot_nt.defvjp(lambda a, b: (_dg(a, b, 1, 1), (a, b)),
              lambda r, g: (_dg(g, r[1], 1, 0), _dg(g, r[0], 0, 0)))


@jax.custom_vjp
def dot_tn(a, b):
    return _dg(a, b, 0, 0)


dot_tn.defvjp(lambda a, b: (_dg(a, b, 0, 0), (a, b)),
              lambda r, g: (_dg(r[1], g, 1, 1), _dg(r[0], g, 1, 0)))


def _sigmoid(x):
    return 1.0 / (1.0 + jnp.exp(-x))


def _silu(x):
    return x * _sigmoid(x)


def _my_pos():
    return lax.axis_index("x"), lax.axis_index("y"), lax.axis_index("c")


def _my_index():
    x, y, c = _my_pos()
    return 4 * x + 2 * y + c


_ANY = pl.BlockSpec(memory_space=pl.ANY)


class _Gather:
    def __init__(self, arrays):
        self.arrays = list(arrays)
        n = len(self.arrays)
        self.out_shape = [jax.ShapeDtypeStruct((N_DEV,) + a.shape, a.dtype) for a in self.arrays]
        self.scratch = [pltpu.SemaphoreType.DMA((n, 7)), pltpu.SemaphoreType.DMA((n, 7)), pltpu.SemaphoreType.DMA((n,))]

    def _plan(self, xs, outs, sems):
        send_sems, recv_sems, local_sems = sems
        n = len(self.arrays)
        x, y, c = _my_pos()
        me, sibling = (x, y, c), (x, y, 1 - c)
        chips = [(1 - x, y), (x, 1 - y), (1 - x, 1 - y)]

        def slot(a, p):
            return outs[a].at[4 * p[0] + 2 * p[1] + p[2]]

        def copy(a, k, block, to, src=None):
            return pltpu.make_async_remote_copy(
                src_ref=slot(a, block) if src is None else src, dst_ref=slot(a, block),
                send_sem=send_sems.at[a, k], recv_sem=recv_sems.at[a, k], device_id=to, device_id_type=MESH)

        mine = [pltpu.make_async_copy(xs[a], slot(a, me), local_sems.at[a]) for a in range(n)]
        first = []
        for a in range(n):
            first.append(copy(a, 0, me, sibling, src=xs[a]))
            first += [copy(a, 1 + j, me, (*chip, c), src=xs[a]) for j, chip in enumerate(chips)]
        return n, c, me, sibling, chips, copy, mine, first

    def start(self, xs, outs, sems):
        _, _, _, _, _, _, mine, first = self._plan(xs, outs, sems)
        for m in mine:
            m.start()
        for cp in first:
            cp.start()

    def finish(self, xs, outs, sems):
        n, c, me, sibling, chips, copy, mine, first = self._plan(xs, outs, sems)
        passed = []
        for a in range(n):
            for j, chip in enumerate(chips):
                copy(a, 1 + j, (*chip, c), me).wait_recv()
                p = copy(a, 4 + j, (*chip, c), sibling)
                p.start()
                passed.append(p)
        for a in range(n):
            copy(a, 0, sibling, me).wait_recv()
            for j, chip in enumerate(chips):
                copy(a, 4 + j, (*chip, 1 - c), me).wait_recv()
        for cp in first + passed:
            cp.wait_send()
        for m in mine:
            m.wait()


class _Exchange:
    def __init__(self, arrays):
        self.arrays = list(arrays)
        n = len(self.arrays)
        self.out_shape = [jax.ShapeDtypeStruct(a.shape, a.dtype) for a in self.arrays]
        self.scratch = [pltpu.SemaphoreType.DMA((n, 7)), pltpu.SemaphoreType.DMA((n, 7)), pltpu.SemaphoreType.DMA((n,))]

    def _plan(self, xs, outs, sems):
        send_sems, recv_sems, local_sems = sems
        x, y, c = _my_pos()
        me = 4 * x + 2 * y + c
        mine, sends, recvs = [], [], []
        for a in range(len(self.arrays)):
            mine.append(pltpu.make_async_copy(xs[a].at[me], outs[a].at[me], local_sems.at[a]))
        for k in range(1, N_DEV):
            px = 1 - x if (k >> 2) & 1 else x
            py = 1 - y if (k >> 1) & 1 else y
            pc = 1 - c if k & 1 else c
            peer = 4 * px + 2 * py + pc
            for a in range(len(self.arrays)):
                sends.append(pltpu.make_async_remote_copy(
                    src_ref=xs[a].at[peer], dst_ref=outs[a].at[me], send_sem=send_sems.at[a, k - 1],
                    recv_sem=recv_sems.at[a, k - 1], device_id=(px, py, pc), device_id_type=MESH))
                recvs.append(pltpu.make_async_remote_copy(
                    src_ref=xs[a].at[me], dst_ref=outs[a].at[peer], send_sem=send_sems.at[a, k - 1],
                    recv_sem=recv_sems.at[a, k - 1], device_id=(px, py, pc), device_id_type=MESH))
        return mine, sends, recvs

    def start(self, xs, outs, sems):
        mine, sends, _ = self._plan(xs, outs, sems)
        for m in mine:
            m.start()
        for s in sends:
            s.start()

    def finish(self, xs, outs, sems):
        mine, sends, recvs = self._plan(xs, outs, sems)
        for r in recvs:
            r.wait_recv()
        for s in sends:
            s.wait_send()
        for m in mine:
            m.wait()


def _run_comm(comm, name):
    n = len(comm.arrays)

    def body(*refs):
        xs, outs, sems = refs[:n], refs[n:2 * n], refs[2 * n:]
        comm.start(xs, outs, sems)
        comm.finish(xs, outs, sems)

    return pl.pallas_call(body, name=name, out_shape=comm.out_shape, in_specs=[_ANY] * n, out_specs=[_ANY] * n,
                          scratch_shapes=comm.scratch)(*comm.arrays)


def _call(body, *, name, grid, in_specs, out_specs, out_shape, args, scratch=(), sem=None, comm=None):
    if comm is None:
        res = pl.pallas_call(body, name=name, grid=grid, in_specs=list(in_specs), out_specs=list(out_specs),
                             out_shape=list(out_shape), scratch_shapes=list(scratch), compiler_params=_params(sem))(*args)
        return list(res), None
    n_in, n_out, n_scr = len(in_specs), len(out_specs), len(scratch)
    c_n = len(comm.arrays)

    def wrapped(*refs):
        ins, cin = refs[:n_in], refs[n_in:n_in + c_n]
        o0 = n_in + c_n
        outs, cout = refs[o0:o0 + n_out], refs[o0 + n_out:o0 + n_out + c_n]
        s0 = o0 + n_out + c_n
        scr, cscr = refs[s0:s0 + n_scr], refs[s0 + n_scr:]
        ids = [pl.program_id(ax) for ax in range(len(grid))]
        first = functools.reduce(jnp.logical_and, [i == 0 for i in ids])
        last = functools.reduce(jnp.logical_and, [i == g - 1 for i, g in zip(ids, grid)])

        @pl.when(first)
        def _():
            comm.start(cin, cout, cscr)

        body(*ins, *outs, *scr)

        @pl.when(last)
        def _():
            comm.finish(cin, cout, cscr)

    res = pl.pallas_call(
        wrapped, name=name, grid=grid, in_specs=list(in_specs) + [_ANY] * c_n, out_specs=list(out_specs) + [_ANY] * c_n,
        out_shape=list(out_shape) + list(comm.out_shape), scratch_shapes=list(scratch) + list(comm.scratch),
        compiler_params=_params(("arbitrary",) * len(grid)))(*args, *comm.arrays)
    return list(res[:n_out]), list(res[n_out:])


MM_B_BLOCK_BYTES = 6 * 1024 * 1024
MM_O_BLOCK_BYTES = 13 * 1024 * 1024 // 2


def _mm(a, b, name, tb=False, out_dtype=F32, b3=False, comm=None):
    m, k = a.shape
    if b3:
        cs = b.shape[2]
        n, kb = (b.shape[1], N_DEV * cs) if tb else (N_DEV * cs, b.shape[1])
    else:
        n, kb = (b.shape[0], b.shape[1]) if tb else (b.shape[1], b.shape[0])
    assert k == kb, (a.shape, b.shape, tb)
    tm = _tile(m, 1088, 2 * SUBLANES)
    tk = cs if (b3 and tb) else k
    nk = k // tk
    if b3 and not tb:
        tn = cs
    else:
        tn = _tile(n, min(MM_B_BLOCK_BYTES // (2 * tk), MM_O_BLOCK_BYTES // (4 * tm)), LANES)
    cb = 1 if tb else 0
    dn = (((1,), (cb,)), ((), ()))

    def body_one(a_ref, b_ref, o_ref):
        o_ref[...] = lax.dot_general(a_ref[...], b_ref[...], dn, preferred_element_type=F32).astype(o_ref.dtype)

    def body_acc(a_ref, b_ref, o_ref, acc_ref):
        kk = pl.program_id(2)
        prod = lax.dot_general(a_ref[...], b_ref[...], dn, preferred_element_type=F32)

        @pl.when(kk == 0)
        def _():
            acc_ref[...] = prod

        @pl.when(kk > 0)
        def _():
            acc_ref[...] += prod

        @pl.when(kk == nk - 1)
        def _():
            o_ref[...] = acc_ref[...].astype(o_ref.dtype)

    a_spec = pl.BlockSpec((tm, tk), lambda i, j, kk: (i, kk))
    if b3:
        b_spec = (pl.BlockSpec((None, tn, cs), lambda i, j, kk: (kk, j, 0)) if tb
                  else pl.BlockSpec((None, tk, cs), lambda i, j, kk: (j, kk, 0)))
    else:
        b_spec = pl.BlockSpec((tn, tk), lambda i, j, kk: (j, kk)) if tb else pl.BlockSpec((tk, tn), lambda i, j, kk: (kk, j))
    res, cres = _call(
        body_one if nk == 1 else body_acc, name=name, grid=(m // tm, n // tn, nk), in_specs=[a_spec, b_spec],
        out_specs=[pl.BlockSpec((tm, tn), lambda i, j, kk: (i, j))], out_shape=[jax.ShapeDtypeStruct((m, n), out_dtype)],
        scratch=[] if nk == 1 else [pltpu.VMEM((tm, tn), F32)], sem=("parallel", "parallel", "arbitrary"),
        args=(a, b), comm=comm)
    return res[0], cres


def _mm_tn(a, b, name, cs=None, comm=None):
    kk, m = a.shape
    n = b.shape[1]
    nc = _tile(n, 5632, cs if cs is not None else LANES)
    tm = _tile(m, 512, LANES)
    tk = _tile(kk, 272, 2 * SUBLANES)
    nk = kk // tk

    def body(a_ref, b_ref, o_ref, acc_ref):
        s = pl.program_id(2)
        prod = jnp.dot(a_ref[...].T, b_ref[...], preferred_element_type=F32)

        @pl.when(s == 0)
        def _():
            acc_ref[...] = prod

        @pl.when(s > 0)
        def _():
            acc_ref[...] += prod

        @pl.when(s == nk - 1)
        def _():
            if cs is None:
                o_ref[...] = acc_ref[...].astype(BF16)
            else:
                for j in range(nc // cs):
                    o_ref[j] = acc_ref[:, j * cs:(j + 1) * cs].astype(BF16)

    if cs is None:
        o_spec = pl.BlockSpec((tm, nc), lambda j, i, s: (i, j))
        o_shape = jax.ShapeDtypeStruct((m, n), BF16)
    else:
        o_spec = pl.BlockSpec((nc // cs, tm, cs), lambda j, i, s: (j, i, 0))
        o_shape = jax.ShapeDtypeStruct((n // cs, m, cs), BF16)
    res, cres = _call(
        body, name=name, grid=(n // nc, m // tm, nk),
        in_specs=[pl.BlockSpec((tk, tm), lambda j, i, s: (s, i)), pl.BlockSpec((tk, nc), lambda j, i, s: (s, j))],
        out_specs=[o_spec], out_shape=[o_shape], scratch=[pltpu.VMEM((tm, nc), F32)],
        sem=("parallel", "parallel", "arbitrary"), args=(a, b), comm=comm)
    return res[0], cres


def _cast_bf16(x, name):
    r, c = x.shape
    tr = _tile(r, 512, 2 * SUBLANES)

    def body(x_ref, o_ref):
        o_ref[...] = x_ref[...].astype(BF16)

    return pl.pallas_call(body, name=name, grid=(r // tr,), in_specs=[pl.BlockSpec((tr, c), lambda i: (i, 0))],
                          out_specs=pl.BlockSpec((tr, c), lambda i: (i, 0)),
                          out_shape=jax.ShapeDtypeStruct((r, c), BF16), compiler_params=_params(("parallel",)))(x)


def _cols_from_shards(wg, name):
    _, k, cs = wg.shape
    tk = _tile(k, 256, 2 * SUBLANES)

    def body(w_ref, o_ref):
        for j in range(N_DEV):
            o_ref[:, j * cs:(j + 1) * cs] = w_ref[j]

    return pl.pallas_call(body, name=name, grid=(k // tk,),
                          in_specs=[pl.BlockSpec((N_DEV, tk, cs), lambda i: (0, i, 0))],
                          out_specs=pl.BlockSpec((tk, N_DEV * cs), lambda i: (i, 0)),
                          out_shape=jax.ShapeDtypeStruct((k, N_DEV * cs), wg.dtype),
                          compiler_params=_params(("parallel",)))(wg)


def _stream(i):
    return jnp.minimum(i, 1)


def _normmod(x, g, sh, sc):
    y = x * lax.rsqrt(jnp.mean(x * x, axis=-1, keepdims=True) + EPS)
    return (y * g) * (1.0 + sc) + sh


def _mod_spec(chunk, d):
    return pl.BlockSpec((None, None, 1, d), lambda i: (_stream(i), chunk, 0, 0))


def _normmod_fwd(x, g, mod4, which, name):
    t, d = x.shape
    tm = _tm()
    ish, isc = (0, 1) if which == 0 else (3, 4)

    def body(x_ref, g_ref, sh_ref, sc_ref, o_ref):
        o_ref[...] = _normmod(x_ref[...], g_ref[...], sh_ref[...], sc_ref[...]).astype(BF16)

    row = pl.BlockSpec((tm, d), lambda i: (i, 0))
    return pl.pallas_call(body, name=name, grid=(t // tm,),
                          in_specs=[row, pl.BlockSpec((1, d), lambda i: (0, 0)), _mod_spec(ish, d), _mod_spec(isc, d)],
                          out_specs=row, out_shape=jax.ShapeDtypeStruct((t, d), BF16),
                          compiler_params=_params(("parallel",)))(x, g, mod4, mod4)


def _normmod_bwd(x, g, mod4, which, dh, dres, name):
    t, d = x.shape
    tm = _tm()
    ish, isc = (0, 1) if which == 0 else (3, 4)

    def body(x_ref, g_ref, sh_ref, sc_ref, dh_ref, dres_ref, dx_ref, dg_ref, dsh_ref, dsc_ref):
        i = pl.program_id(0)
        _, vjp = jax.vjp(_normmod, x_ref[...], g_ref[...], sh_ref[...], sc_ref[...])
        dx, dg, dsh, dsc = vjp(dh_ref[...])
        dx_ref[...] = dres_ref[...] + dx

        @pl.when(i == 0)
        def _():
            dg_ref[...] = jnp.zeros_like(dg_ref)

        @pl.when(i <= 1)
        def _():
            dsh_ref[...] = jnp.zeros_like(dsh_ref)
            dsc_ref[...] = jnp.zeros_like(dsc_ref)

        dg_ref[...] += dg
        dsh_ref[...] += dsh
        dsc_ref[...] += dsc

    row = pl.BlockSpec((tm, d), lambda i: (i, 0))
    vec = pl.BlockSpec((1, d), lambda i: (0, 0))
    svec = pl.BlockSpec((None, 1, d), lambda i: (_stream(i), 0, 0))
    return pl.pallas_call(
        body, name=name, grid=(t // tm,),
        in_specs=[row, vec, _mod_spec(ish, d), _mod_spec(isc, d), row, row],
        out_specs=[row, vec, svec, svec],
        out_shape=[jax.ShapeDtypeStruct((t, d), F32), jax.ShapeDtypeStruct((1, d), F32),
                   jax.ShapeDtypeStruct((2, 1, d), F32), jax.ShapeDtypeStruct((2, 1, d), F32)],
        compiler_params=_params(("arbitrary",)))(x, g, mod4, mod4, dh, dres)


def _gate_res_fwd(x, f, mod4, chunk, name):
    t, d = x.shape
    tm = _tm()

    def body(x_ref, f_ref, g_ref, o_ref):
        o_ref[...] = x_ref[...] + g_ref[...] * f_ref[...]

    row = pl.BlockSpec((tm, d), lambda i: (i, 0))
    return pl.pallas_call(body, name=name, grid=(t // tm,), in_specs=[row, row, _mod_spec(chunk, d)], out_specs=row,
                          out_shape=jax.ShapeDtypeStruct((t, d), F32), compiler_params=_params(("parallel",)))(x, f, mod4)


def _gate_res_bwd(dx, f, mod4, chunk, name):
    t, d = dx.shape
    tm = _tm()

    def body(dx_ref, f_ref, g_ref, o_ref, dg_ref):
        i = pl.program_id(0)
        dxv = dx_ref[...]
        o_ref[...] = (dxv * g_ref[...]).astype(BF16)

        @pl.when(i <= 1)
        def _():
            dg_ref[...] = jnp.zeros_like(dg_ref)

        dg_ref[...] += jnp.sum(dxv * f_ref[...], axis=0, keepdims=True)

    row = pl.BlockSpec((tm, d), lambda i: (i, 0))
    return pl.pallas_call(
        body, name=name, grid=(t // tm,), in_specs=[row, row, _mod_spec(chunk, d)],
        out_specs=[row, pl.BlockSpec((None, 1, d), lambda i: (_stream(i), 0, 0))],
        out_shape=[jax.ShapeDtypeStruct((t, d), BF16), jax.ShapeDtypeStruct((2, 1, d), F32)],
        compiler_params=_params(("arbitrary",)))(dx, f, mod4)


def _loss_head(x, final_g, target, name):
    t, d = x.shape
    tm = _tm()

    def loss_fn(xv, g, tgt):
        y = (xv * lax.rsqrt(jnp.mean(xv * xv, axis=-1, keepdims=True) + EPS)) * g
        err = y - tgt
        return 0.5 * jnp.sum(jnp.mean(err * err, axis=-1, keepdims=True))

    def body(x_ref, g_ref, t_ref, l_ref, dx_ref, dg_ref):
        i = pl.program_id(0)

        @pl.when(i == 0)
        def _():
            l_ref[...] = jnp.zeros_like(l_ref)
            dg_ref[...] = jnp.zeros_like(dg_ref)
            dx_ref[...] = jnp.zeros_like(dx_ref)

        @pl.when(i > 0)
        def _():
            l, (dx, dg) = jax.value_and_grad(loss_fn, argnums=(0, 1))(x_ref[...], g_ref[...], t_ref[...])
            l_ref[...] += jnp.full(l_ref.shape, l, F32)
            dx_ref[...] = dx
            dg_ref[...] += dg

    row = pl.BlockSpec((tm, d), lambda i: (i, 0))
    vec = pl.BlockSpec((1, d), lambda i: (0, 0))
    return pl.pallas_call(
        body, name=name, grid=(t // tm,),
        in_specs=[row, vec, pl.BlockSpec((tm, d), lambda i: (jnp.maximum(i - 1, 0), 0))],
        out_specs=[pl.BlockSpec((SUBLANES, LANES), lambda i: (0, 0)), row, vec],
        out_shape=[jax.ShapeDtypeStruct((SUBLANES, LANES), F32), jax.ShapeDtypeStruct((t, d), F32),
                   jax.ShapeDtypeStruct((1, d), F32)],
        compiler_params=_params(("arbitrary",)))(x, final_g, target)


def _swap_quarters(x):
    half, nf = RET_DK // 2, RET_DK // 4
    lane = lax.broadcasted_iota(jnp.int32, x.shape, 1)
    return jnp.where((lane % half) < nf, pltpu.roll(x, RET_DK - nf, 1), pltpu.roll(x, nf, 1))


def _rope(x, cos, sin):
    return x * cos + _swap_quarters(x) * sin


def _rope_t(y, cos, sin):
    return y * cos + _swap_quarters(y * sin)


def _ret_consts(d):
    c = RET_CHUNK
    ii = lax.broadcasted_iota(jnp.int32, (c, 1), 0).astype(F32)
    jj = lax.broadcasted_iota(jnp.int32, (1, c), 1).astype(F32)
    fwd = d == 0
    sgn = jnp.where(fwd, 1.0, -1.0).astype(F32)
    pos = jnp.where(fwd, ii, c - 1.0 - ii)
    return sgn * (ii - jj), pos


def _ret_step(lgt, state, q, k, v, diff, pos):
    c = float(RET_CHUNK)
    lg = -(jnp.maximum(-lgt, 0.0) + jnp.log1p(jnp.exp(-jnp.abs(lgt))))
    lower = diff >= 0
    decay = jnp.where(lower, jnp.exp(jnp.where(lower, diff, 0.0) * lg), 0.0)
    xi = jnp.exp((pos + 1.0) * lg)
    zeta = jnp.exp((c - 1.0 - pos) * lg)
    gch = jnp.exp(c * lg)
    inner = dot_nt(q, k) * decay
    out = dot_nn(inner, v) + dot_nn(q, state) * xi
    new_state = state * gch + dot_tn(k * zeta, v)
    return out, new_state


def _chunk_order():
    nc, nch = CTX_LEN // RET_CHUNK, _t_rows() // RET_CHUNK
    fwd = list(range(nch))
    bwd = list(range(nc - 1, -1, -1)) + list(range(nch - 1, nc - 1, -1))
    return jnp.asarray(np.array([fwd, bwd], np.int32))


def _ret_fwd(p, cos, sin, decay, order, name):
    t = p.shape[0]
    c, dk, dv, nh = RET_CHUNK, RET_DK, RET_DV, RET_HEADS
    nch = t // c
    off = _offsets()
    qb, kb, vb = off["q"] // dk, off["k"] // dk, off["v"] // dv
    scale = RET_DK ** -0.5

    def body(ord_ref, dec_ref, q_ref, k_ref, v_ref, cos_ref, sin_ref, o_ref, st_ref, state):
        d, h, s = pl.program_id(0), pl.program_id(1), pl.program_id(2)

        @pl.when(s == 0)
        def _():
            state[...] = jnp.zeros_like(state)

        st_ref[...] = state[...]
        diff, pos = _ret_consts(d)
        lgt = jnp.full((1, 1), dec_ref[d, h], F32)
        q = _rope(q_ref[...], cos_ref[...], sin_ref[...]) * scale
        k = _rope(k_ref[...], cos_ref[...], sin_ref[...])
        out, ns = _ret_step(lgt, state[...], q, k, v_ref[...], diff, pos)
        o_ref[...] = out
        state[...] = ns

    grid_spec = pltpu.PrefetchScalarGridSpec(
        num_scalar_prefetch=1, grid=(2, nh, nch),
        in_specs=[pl.BlockSpec(memory_space=pltpu.SMEM),
                  pl.BlockSpec((c, dk), lambda d, h, s, o: (o[d, s], qb + h)),
                  pl.BlockSpec((c, dk), lambda d, h, s, o: (o[d, s], kb + h)),
                  pl.BlockSpec((c, dv), lambda d, h, s, o: (o[d, s], vb + h)),
                  pl.BlockSpec((c, dk), lambda d, h, s, o: (o[d, s], 0)),
                  pl.BlockSpec((c, dk), lambda d, h, s, o: (o[d, s], 0))],
        out_specs=[pl.BlockSpec((None, c, dv), lambda d, h, s, o: (d, o[d, s], h)),
                   pl.BlockSpec((None, None, None, dk, dv), lambda d, h, s, o: (d, h, s, 0, 0))],
        scratch_shapes=[pltpu.VMEM((dk, dv), F32)])
    return pl.pallas_call(
        body, name=name, grid_spec=grid_spec,
        out_shape=[jax.ShapeDtypeStruct((2, t, nh * dv), F32), jax.ShapeDtypeStruct((2, nh, nch, dk, dv), F32)],
        compiler_params=_params(("arbitrary", "arbitrary", "arbitrary")))(order, decay, p, p, p, cos, sin)


def _ret_bwd(p, cos, sin, decay, order, states, do, name):
    t = p.shape[0]
    c, dk, dv, nh = RET_CHUNK, RET_DK, RET_DV, RET_HEADS
    nch = t // c
    off = _offsets()
    qb, kb, vb = off["q"] // dk, off["k"] // dk, off["v"] // dv
    scale = RET_DK ** -0.5

    def body(ord_ref, dec_ref, q_ref, k_ref, v_ref, cos_ref, sin_ref, st_ref, do_ref,
             dq_ref, dk_ref, dv_ref, dd_ref, dstate):
        d, h, s = pl.program_id(0), pl.program_id(1), pl.program_id(2)

        @pl.when(s == 0)
        def _():
            dstate[...] = jnp.zeros_like(dstate)
            dd_ref[...] = jnp.zeros_like(dd_ref)

        diff, pos = _ret_consts(d)
        lgt = jnp.full((1, 1), dec_ref[d, h], F32)
        cosv, sinv = cos_ref[...], sin_ref[...]
        q = _rope(q_ref[...], cosv, sinv) * scale
        k = _rope(k_ref[...], cosv, sinv)
        _, vjp = jax.vjp(lambda a, b, cq, ck, cv: _ret_step(a, b, cq, ck, cv, diff, pos),
                         lgt, st_ref[...], q, k, v_ref[...])
        dlgt, dst, dq, dkk, dvv = vjp((do_ref[...], dstate[...]))
        dstate[...] = dst
        dq_ref[...] = _rope_t(dq * scale, cosv, sinv)
        dk_ref[...] = _rope_t(dkk, cosv, sinv)
        dv_ref[...] = dvv
        dd_ref[...] += jnp.broadcast_to(dlgt, dd_ref.shape)

    rev = lambda o, d, s: o[d, nch - 1 - s]
    grid_spec = pltpu.PrefetchScalarGridSpec(
        num_scalar_prefetch=1, grid=(2, nh, nch),
        in_specs=[pl.BlockSpec(memory_space=pltpu.SMEM),
                  pl.BlockSpec((c, dk), lambda d, h, s, o: (rev(o, d, s), qb + h)),
                  pl.BlockSpec((c, dk), lambda d, h, s, o: (rev(o, d, s), kb + h)),
                  pl.BlockSpec((c, dv), lambda d, h, s, o: (rev(o, d, s), vb + h)),
                  pl.BlockSpec((c, dk), lambda d, h, s, o: (rev(o, d, s), 0)),
                  pl.BlockSpec((c, dk), lambda d, h, s, o: (rev(o, d, s), 0)),
                  pl.BlockSpec((None, None, None, dk, dv), lambda d, h, s, o: (d, h, nch - 1 - s, 0, 0)),
                  pl.BlockSpec((c, dv), lambda d, h, s, o: (rev(o, d, s), h))],
        out_specs=[pl.BlockSpec((None, c, dk), lambda d, h, s, o: (d, rev(o, d, s), h)),
                   pl.BlockSpec((None, c, dk), lambda d, h, s, o: (d, rev(o, d, s), h)),
                   pl.BlockSpec((None, c, dv), lambda d, h, s, o: (d, rev(o, d, s), h)),
                   pl.BlockSpec((None, None, SUBLANES, LANES), lambda d, h, s, o: (d, h, 0, 0))],
        scratch_shapes=[pltpu.VMEM((dk, dv), F32)])
    return pl.pallas_call(
        body, name=name, grid_spec=grid_spec,
        out_shape=[jax.ShapeDtypeStruct((2, t, nh * dk), F32), jax.ShapeDtypeStruct((2, t, nh * dk), F32),
                   jax.ShapeDtypeStruct((2, t, nh * dv), F32), jax.ShapeDtypeStruct((2, nh, SUBLANES, LANES), F32)],
        compiler_params=_params(("arbitrary", "arbitrary", "arbitrary")))(order, decay, p, p, p, cos, sin, states, do)


def _ggn_head(of, ob, gate, g):
    o = of + ob
    mu = jnp.mean(o, axis=-1, keepdims=True)
    var = jnp.mean(jnp.square(o - mu), axis=-1, keepdims=True)
    return ((o - mu) * lax.rsqrt(var + EPS) * g) * _silu(gate)


def _ggn_fwd(o2, p, gn_g, name):
    t = p.shape[0]
    tm, w, dv = _tm(), _ret_w(), RET_DV
    gb = _offsets()["g"] // w

    def body(o_ref, gate_ref, g_ref, out_ref):
        for h in range(RET_HEADS):
            sl = slice(h * dv, (h + 1) * dv)
            out_ref[:, sl] = _ggn_head(o_ref[0, :, sl], o_ref[1, :, sl], gate_ref[:, sl], g_ref[:, sl]).astype(BF16)

    return pl.pallas_call(
        body, name=name, grid=(t // tm,),
        in_specs=[pl.BlockSpec((2, tm, w), lambda i: (0, i, 0)), pl.BlockSpec((tm, w), lambda i: (i, gb)),
                  pl.BlockSpec((1, w), lambda i: (0, 0))],
        out_specs=pl.BlockSpec((tm, w), lambda i: (i, 0)), out_shape=jax.ShapeDtypeStruct((t, w), BF16),
        compiler_params=_params(("parallel",)))(o2, p, gn_g)


def _ggn_bwd(o2, p, gn_g, dmix, name):
    t = p.shape[0]
    tm, w, dv = _tm(), _ret_w(), RET_DV
    gb = _offsets()["g"] // w

    def body(o_ref, gate_ref, g_ref, dy_ref, do_ref, dgate_ref, dg_ref):
        i = pl.program_id(0)

        @pl.when(i == 0)
        def _():
            dg_ref[...] = jnp.zeros_like(dg_ref)

        for h in range(RET_HEADS):
            sl = slice(h * dv, (h + 1) * dv)
            _, vjp = jax.vjp(_ggn_head, o_ref[0, :, sl], o_ref[1, :, sl], gate_ref[:, sl], g_ref[:, sl])
            do, _, dgate, dg = vjp(dy_ref[:, sl])
            do_ref[:, sl] = do
            dgate_ref[:, sl] = dgate
            dg_ref[:, sl] += dg

    row = pl.BlockSpec((tm, w), lambda i: (i, 0))
    return pl.pallas_call(
        body, name=name, grid=(t // tm,),
        in_specs=[pl.BlockSpec((2, tm, w), lambda i: (0, i, 0)), pl.BlockSpec((tm, w), lambda i: (i, gb)),
                  pl.BlockSpec((1, w), lambda i: (0, 0)), row],
        out_specs=[row, row, pl.BlockSpec((1, w), lambda i: (0, 0))],
        out_shape=[jax.ShapeDtypeStruct((t, w), F32), jax.ShapeDtypeStruct((t, w), F32),
                   jax.ShapeDtypeStruct((1, w), F32)],
        compiler_params=_params(("arbitrary",)))(o2, p, gn_g, dmix)


def _halo(k):
    return SUBLANES * ((k // 2 + SUBLANES - 1) // SUBLANES)


def _halo_specs(width, colblock, h, tm):
    r = tm // h
    return [pl.BlockSpec((h, width), lambda i, *_: (jnp.maximum(i * r - 1, 0), colblock(*_))),
            pl.BlockSpec((tm, width), lambda i, *_: (i, colblock(*_))),
            pl.BlockSpec((h, width), lambda i, *_: (jnp.minimum((i + 1) * r, (_t_rows() // h) - 1), colblock(*_)))]


def _fill_ext(ext_ref, prev, cur, nxt, i, h, tm):
    nt = _t_rows() // tm
    ext_ref[0:h, :] = jnp.where(i >= 2, prev, 0.0)
    ext_ref[h:h + tm, :] = cur
    ext_ref[h + tm:h + tm + h, :] = jnp.where((i >= 1) & (i <= nt - 2), nxt, 0.0)


def _corr(ext_ref, w_ref, k, h, tm, flip):
    pad = k // 2
    acc = None
    for kk in range(k):
        o = h + (pad - kk if flip else kk - pad)
        term = w_ref[kk:kk + 1, :] * ext_ref[o:o + tm, :]
        acc = term if acc is None else acc + term
    return acc


def _conv_post(u2, ln_g, ln_b, pw):
    mu = jnp.mean(u2, axis=-1, keepdims=True)
    var = jnp.mean(jnp.square(u2 - mu), axis=-1, keepdims=True)
    y = (u2 - mu) * lax.rsqrt(var + EPS) * ln_g + ln_b
    return dot_nn(_silu(y), pw)


def _conv_fwd(p, dw_w, dw_b, ln_g, ln_b, pw, name):
    t = p.shape[0]
    tm, w, k = _tm(), CONV_W, CONV_K
    h = _halo(k)
    off = _offsets()
    ab, bb = off["a"] // w, off["b"] // w

    def body(ap, ac, an, bp, bc, bn, w_ref, b_ref, g_ref, beta_ref, pw_ref, u2_ref, out_ref, ext):
        i = pl.program_id(0)
        glu = lambda a, b: a * _sigmoid(b)
        _fill_ext(ext, glu(ap[...], bp[...]), glu(ac[...], bc[...]), glu(an[...], bn[...]), i, h, tm)
        u2 = _corr(ext, w_ref, k, h, tm, False) + b_ref[...]
        u2_ref[...] = u2
        out_ref[...] = _conv_post(u2, g_ref[...], beta_ref[...], pw_ref[...]).astype(BF16)

    vec = pl.BlockSpec((1, w), lambda i: (0, 0))
    row = pl.BlockSpec((tm, w), lambda i: (i, 0))
    return pl.pallas_call(
        body, name=name, grid=(t // tm,),
        in_specs=_halo_specs(w, lambda: ab, h, tm) + _halo_specs(w, lambda: bb, h, tm)
        + [pl.BlockSpec((k, w), lambda i: (0, 0)), vec, vec, vec, pl.BlockSpec((w, w), lambda i: (0, 0))],
        out_specs=[row, row],
        out_shape=[jax.ShapeDtypeStruct((t, w), F32), jax.ShapeDtypeStruct((t, w), BF16)],
        scratch_shapes=[pltpu.VMEM((tm + 2 * h, w), F32)],
        compiler_params=_params(("parallel",)))(p, p, p, p, p, p, dw_w, dw_b, ln_g, ln_b, pw)


def _conv_bwd1(u2, dmix, ln_g, ln_b, pw, name):
    t = u2.shape[0]
    tm, w = _tm(), CONV_W
    cb = _ret_w() // w

    def body(u2_ref, dy_ref, g_ref, beta_ref, pw_ref, du2_ref, dg_ref, db_ref, dpw_ref):
        i = pl.program_id(0)

        @pl.when(i == 0)
        def _():
            dg_ref[...] = jnp.zeros_like(dg_ref)
            db_ref[...] = jnp.zeros_like(db_ref)
            dpw_ref[...] = jnp.zeros_like(dpw_ref)

        _, vjp = jax.vjp(_conv_post, u2_ref[...], g_ref[...], beta_ref[...], pw_ref[...])
        du2, dg, db, dpw = vjp(dy_ref[...])
        du2_ref[...] = du2
        dg_ref[...] += dg
        db_ref[...] += db
        dpw_ref[...] += dpw

    vec = pl.BlockSpec((1, w), lambda i: (0, 0))
    row = pl.BlockSpec((tm, w), lambda i: (i, 0))
    mat = pl.BlockSpec((w, w), lambda i: (0, 0))
    return pl.pallas_call(
        body, name=name, grid=(t // tm,),
        in_specs=[row, pl.BlockSpec((tm, w), lambda i: (i, cb)), vec, vec, mat],
        out_specs=[row, vec, vec, mat],
        out_shape=[jax.ShapeDtypeStruct((t, w), F32), jax.ShapeDtypeStruct((1, w), F32),
                   jax.ShapeDtypeStruct((1, w), F32), jax.ShapeDtypeStruct((w, w), F32)],
        compiler_params=_params(("arbitrary",)))(u2, dmix, ln_g, ln_b, pw)


def _conv_bwd2(du2, p, dw_w, name):
    t = p.shape[0]
    tm, w, k = _tm(), CONV_W, CONV_K
    h = _halo(k)
    pad = k // 2
    off = _offsets()
    ab, bb = off["a"] // w, off["b"] // w

    def body(dp, dc, dn, ap, ac, an, bp, bc, bn, w_ref, da_ref, db_ref, dw_ref, dbias_ref, ext_d, ext_u):
        i = pl.program_id(0)

        @pl.when(i == 0)
        def _():
            dw_ref[...] = jnp.zeros_like(dw_ref)
            dbias_ref[...] = jnp.zeros_like(dbias_ref)

        glu = lambda a, b: a * _sigmoid(b)
        a, b, d = ac[...], bc[...], dc[...]
        _fill_ext(ext_d, dp[...], d, dn[...], i, h, tm)
        _fill_ext(ext_u, glu(ap[...], bp[...]), glu(a, b), glu(an[...], bn[...]), i, h, tm)
        du = _corr(ext_d, w_ref, k, h, tm, True)
        sg = _sigmoid(b)
        da_ref[...] = du * sg
        db_ref[...] = du * a * sg * (1.0 - sg)
        dbias_ref[...] += jnp.sum(d, axis=0, keepdims=True)
        for kk in range(k):
            o = h + kk - pad
            dw_ref[kk:kk + 1, :] += jnp.sum(d * ext_u[o:o + tm, :], axis=0, keepdims=True)

    vec = pl.BlockSpec((1, w), lambda i: (0, 0))
    row = pl.BlockSpec((tm, w), lambda i: (i, 0))
    kw = pl.BlockSpec((k, w), lambda i: (0, 0))
    return pl.pallas_call(
        body, name=name, grid=(t // tm,),
        in_specs=_halo_specs(w, lambda: 0, h, tm) + _halo_specs(w, lambda: ab, h, tm)
        + _halo_specs(w, lambda: bb, h, tm) + [kw],
        out_specs=[row, row, kw, vec],
        out_shape=[jax.ShapeDtypeStruct((t, w), F32), jax.ShapeDtypeStruct((t, w), F32),
                   jax.ShapeDtypeStruct((k, w), F32), jax.ShapeDtypeStruct((1, w), F32)],
        scratch_shapes=[pltpu.VMEM((tm + 2 * h, w), F32), pltpu.VMEM((tm + 2 * h, w), F32)],
        compiler_params=_params(("arbitrary",)))(du2, du2, du2, p, p, p, p, p, p, dw_w)


def _ffn_tc():
    return _tile(D_FF, 512, LANES)


def _ffn_act_fwd(u, dw_w, dw_b, name):
    t = u.shape[0]
    tm, k, tc = _tm(), FFN_K, _ffn_tc()
    h = _halo(k)
    nj = D_FF // tc

    def body(vp, vc, vn, gp, gc, gn, wv, wg, bv, bg, out_ref, ext_v, ext_g):
        i = pl.program_id(0)
        _fill_ext(ext_v, vp[...], vc[...], vn[...], i, h, tm)
        _fill_ext(ext_g, gp[...], gc[...], gn[...], i, h, tm)
        val = _corr(ext_v, wv, k, h, tm, False) + bv[...]
        gate = _corr(ext_g, wg, k, h, tm, False) + bg[...]
        out_ref[...] = (_silu(gate) * val).astype(BF16)

    wspec = lambda s: pl.BlockSpec((k, tc), lambda i, j: (0, j + s))
    bspec = lambda s: pl.BlockSpec((1, tc), lambda i, j: (0, j + s))
    return pl.pallas_call(
        body, name=name, grid=(t // tm, nj),
        in_specs=_halo_specs(tc, lambda j: j, h, tm) + _halo_specs(tc, lambda j: j + nj, h, tm)
        + [wspec(0), wspec(nj), bspec(0), bspec(nj)],
        out_specs=pl.BlockSpec((tm, tc), lambda i, j: (i, j)),
        out_shape=jax.ShapeDtypeStruct((t, D_FF), BF16),
        scratch_shapes=[pltpu.VMEM((tm + 2 * h, tc), F32), pltpu.VMEM((tm + 2 * h, tc), F32)],
        compiler_params=_params(("parallel", "parallel")))(u, u, u, u, u, u, dw_w, dw_w, dw_b, dw_b)


def _ffn_act_bwd1(u, da, dw_w, dw_b, name):
    t = u.shape[0]
    tm, k, tc = _tm(), FFN_K, _ffn_tc()
    h = _halo(k)
    nj = D_FF // tc

    def body(vp, vc, vn, gp, gc, gn, wv, wg, bv, bg, da_ref, dv_ref, dg_ref, ext_v, ext_g):
        i = pl.program_id(0)
        _fill_ext(ext_v, vp[...], vc[...], vn[...], i, h, tm)
        _fill_ext(ext_g, gp[...], gc[...], gn[...], i, h, tm)
        val = _corr(ext_v, wv, k, h, tm, False) + bv[...]
        gate = _corr(ext_g, wg, k, h, tm, False) + bg[...]
        _, vjp = jax.vjp(lambda a, b: _silu(b) * a, val, gate)
        dval, dgate = vjp(da_ref[...])
        dv_ref[...] = dval
        dg_ref[...] = dgate

    wspec = lambda s: pl.BlockSpec((k, tc), lambda i, j: (0, j + s))
    bspec = lambda s: pl.BlockSpec((1, tc), lambda i, j: (0, j + s))
    dc = pl.pallas_call(
        body, name=name, grid=(t // tm, nj),
        in_specs=_halo_specs(tc, lambda j: j, h, tm) + _halo_specs(tc, lambda j: j + nj, h, tm)
        + [wspec(0), wspec(nj), bspec(0), bspec(nj), pl.BlockSpec((tm, tc), lambda i, j: (i, j))],
        out_specs=[pl.BlockSpec((tm, tc), lambda i, j: (i, j)), pl.BlockSpec((tm, tc), lambda i, j: (i, j))],
        out_shape=[jax.ShapeDtypeStruct((t, D_FF), F32), jax.ShapeDtypeStruct((t, D_FF), F32)],
        scratch_shapes=[pltpu.VMEM((tm + 2 * h, tc), F32), pltpu.VMEM((tm + 2 * h, tc), F32)],
        compiler_params=_params(("parallel", "parallel")))(u, u, u, u, u, u, dw_w, dw_w, dw_b, dw_b, da)
    return dc


def _dwconv_bwd(dc, u, dw_w, colblock, name):
    t = u.shape[0]
    tm, k, tc = _tm(), FFN_K, _ffn_tc()
    h = _halo(k)
    pad = k // 2
    nj = D_FF // tc

    def body(dp, dcur, dn, up, uc, un, w_ref, du_ref, dw_ref, dbias_ref, ext_d, ext_u):
        i = pl.program_id(1)

        @pl.when(i == 0)
        def _():
            dw_ref[...] = jnp.zeros_like(dw_ref)
            dbias_ref[...] = jnp.zeros_like(dbias_ref)

        d = dcur[...]
        _fill_ext(ext_d, dp[...], d, dn[...], i, h, tm)
        _fill_ext(ext_u, up[...], uc[...], un[...], i, h, tm)
        du_ref[...] = _corr(ext_d, w_ref, k, h, tm, True).astype(BF16)
        dbias_ref[...] += jnp.sum(d, axis=0, keepdims=True)
        for kk in range(k):
            o = h + kk - pad
            dw_ref[kk:kk + 1, :] += jnp.sum(d * ext_u[o:o + tm, :], axis=0, keepdims=True)

    def hs(cb):
        r = tm // h
        return [pl.BlockSpec((h, tc), lambda j, i: (jnp.maximum(i * r - 1, 0), cb(j))),
                pl.BlockSpec((tm, tc), lambda j, i: (i, cb(j))),
                pl.BlockSpec((h, tc), lambda j, i: (jnp.minimum((i + 1) * r, (_t_rows() // h) - 1), cb(j)))]

    return pl.pallas_call(
        body, name=name, grid=(nj, t // tm),
        in_specs=hs(lambda j: j) + hs(lambda j: j + colblock) + [pl.BlockSpec((k, tc), lambda j, i: (0, j + colblock))],
        out_specs=[pl.BlockSpec((tm, tc), lambda j, i: (i, j)), pl.BlockSpec((k, tc), lambda j, i: (0, j)),
                   pl.BlockSpec((1, tc), lambda j, i: (0, j))],
        out_shape=[jax.ShapeDtypeStruct((t, D_FF), BF16), jax.ShapeDtypeStruct((k, D_FF), F32),
                   jax.ShapeDtypeStruct((1, D_FF), F32)],
        scratch_shapes=[pltpu.VMEM((tm + 2 * h, tc), F32), pltpu.VMEM((tm + 2 * h, tc), F32)],
        compiler_params=_params(("parallel", "arbitrary")))(dc, dc, dc, u, u, u, dw_w)


def _na_geometry(rq):
    ncb = CTX_LEN // GRID_W
    rows_n = SEQ // GRID_W
    r = jnp.maximum(rq - ncb, 0)
    kstart = jnp.clip(r - NA_ROWS // 2, 0, rows_n - NA_ROWS)
    base = kstart - r + NA_ROWS - 1
    return rq >= ncb, kstart, base


def _na_core(q, kl, vl, kc, vc, bias, mask):
    qs = q * (NA_DH ** -0.5)
    s_l = jnp.where(mask, dot_nt(qs, kl) + bias, NEG)
    s_c = dot_nt(qs, kc)
    m = lax.stop_gradient(jnp.maximum(jnp.max(s_l, axis=1, keepdims=True), jnp.max(s_c, axis=1, keepdims=True)))
    e_l, e_c = jnp.exp(s_l - m), jnp.exp(s_c - m)
    inv = 1.0 / (jnp.sum(e_l, axis=1, keepdims=True) + jnp.sum(e_c, axis=1, keepdims=True))
    return dot_nn(e_l * inv, vl) + dot_nn(e_c * inv, vc)


def _na_mask(is_lat):
    nl = NA_ROWS * GRID_W
    q = lax.broadcasted_iota(jnp.int32, (GRID_W, nl), 0)
    w = lax.broadcasted_iota(jnp.int32, (GRID_W, nl), 1) % GRID_W
    cs = jnp.clip(q - NA_COLS // 2, 0, GRID_W - NA_COLS)
    return (w >= cs) & (w < cs + NA_COLS) & is_lat


def _na_bias(rb_ref):
    assert 2 * GRID_W == LANES
    lane = lax.broadcasted_iota(jnp.int32, (GRID_W, LANES), 1)
    tiles = []
    for kp in range(NA_ROWS // 2):
        ev = jnp.broadcast_to(rb_ref[2 * kp:2 * kp + 1, :], (GRID_W, LANES))
        od = jnp.broadcast_to(rb_ref[2 * kp + 1:2 * kp + 2, :], (GRID_W, LANES))
        ev = pltpu.roll(ev, LANES - (NA_COLS - 1), 1, stride=1, stride_axis=0)
        od = pltpu.roll(od, LANES - (NA_COLS - 1) - GRID_W, 1, stride=1, stride_axis=0)
        tiles.append(jnp.where(lane < GRID_W, ev, od))
    return jnp.concatenate(tiles, axis=1)


def _na_dbias(dbias, drb_ref):
    qi = lax.broadcasted_iota(jnp.int32, (GRID_W, GRID_W), 0)
    qj = lax.broadcasted_iota(jnp.int32, (GRID_W, GRID_W), 1)
    flip = (qi + qj == GRID_W - 1).astype(F32)
    rev = lax.dot_general(flip, dbias, (((1,), (0,)), ((), ())), precision=lax.Precision.HIGHEST,
                          preferred_element_type=F32)
    lane = lax.broadcasted_iota(jnp.int32, (GRID_W, LANES), 1)
    s_ev = LANES - (GRID_W - NA_COLS)
    for kp in range(NA_ROWS // 2):
        tile = rev[:, kp * LANES:(kp + 1) * LANES]
        ev = pltpu.roll(jnp.where(lane < GRID_W, tile, 0.0), s_ev, 1, stride=1, stride_axis=0)
        od = pltpu.roll(jnp.where(lane >= GRID_W, tile, 0.0), s_ev - GRID_W, 1, stride=1, stride_axis=0)
        drb_ref[2 * kp:2 * kp + 1, :] += jnp.sum(ev, axis=0, keepdims=True)
        drb_ref[2 * kp + 1:2 * kp + 2, :] += jnp.sum(od, axis=0, keepdims=True)


def _na_specs(p_offsets):
    dh = NA_DH
    t = _t_rows()
    qb, kb, vb = (p_offsets[n] // dh for n in ("nq", "nk", "nv"))
    return [pl.BlockSpec((GRID_W, dh), lambda h, r: (r, qb + h)),
            pl.BlockSpec((t, dh), lambda h, r: (0, kb + h)),
            pl.BlockSpec((t, dh), lambda h, r: (0, vb + h)),
            pl.BlockSpec((None, None, NA_ROWS, LANES), lambda h, r: (h, _na_geometry(r)[2], 0, 0))]


def _na_fwd(p, rb, name):
    t = p.shape[0]
    dh, nl = NA_DH, NA_ROWS * GRID_W

    def body(q_ref, k_ref, v_ref, rb_ref, out_ref):
        rq = pl.program_id(1)
        is_lat, kstart, _ = _na_geometry(rq)
        start = pl.multiple_of(CTX_LEN + kstart * GRID_W, GRID_W)
        out = _na_core(q_ref[...], k_ref[pl.ds(start, nl), :], v_ref[pl.ds(start, nl), :],
                       k_ref[0:CTX_LEN, :], v_ref[0:CTX_LEN, :], _na_bias(rb_ref), _na_mask(is_lat))
        out_ref[...] = out.astype(BF16)

    return pl.pallas_call(
        body, name=name, grid=(NA_HEADS, t // GRID_W), in_specs=_na_specs(_offsets()),
        out_specs=pl.BlockSpec((GRID_W, dh), lambda h, r: (r, h)),
        out_shape=jax.ShapeDtypeStruct((t, _na_w()), BF16),
        compiler_params=_params(("parallel", "arbitrary")))(p, p, p, rb)


def _na_bwd(p, rb, dmix, name, comm=None):
    t = p.shape[0]
    dh, nl = NA_DH, NA_ROWS * GRID_W
    ob = (_ret_w() + CONV_W) // dh

    def body(q_ref, k_ref, v_ref, rb_ref, dy_ref, dq_ref, dk_ref, dv_ref, drb_ref):
        rq = pl.program_id(1)
        is_lat, kstart, base = _na_geometry(rq)
        _, _, prev_base = _na_geometry(rq - 1)
        start = pl.multiple_of(CTX_LEN + kstart * GRID_W, GRID_W)

        @pl.when(rq == 0)
        def _():
            dk_ref[...] = jnp.zeros_like(dk_ref)
            dv_ref[...] = jnp.zeros_like(dv_ref)

        @pl.when((rq == 0) | (base != prev_base))
        def _():
            drb_ref[...] = jnp.zeros_like(drb_ref)

        mask = _na_mask(is_lat)
        _, vjp = jax.vjp(lambda *a: _na_core(*a, mask), q_ref[...], k_ref[pl.ds(start, nl), :],
                         v_ref[pl.ds(start, nl), :], k_ref[0:CTX_LEN, :], v_ref[0:CTX_LEN, :], _na_bias(rb_ref))
        dq, dkl, dvl, dkc, dvc, dbias = vjp(dy_ref[...])
        dq_ref[...] = dq
        dk_ref[pl.ds(start, nl), :] += dkl
        dv_ref[pl.ds(start, nl), :] += dvl
        dk_ref[0:CTX_LEN, :] += dkc
        dv_ref[0:CTX_LEN, :] += dvc
        _na_dbias(dbias, drb_ref)

    return _call(
        body, name=name, grid=(NA_HEADS, t // GRID_W),
        in_specs=_na_specs(_offsets()) + [pl.BlockSpec((GRID_W, dh), lambda h, r: (r, ob + h))],
        out_specs=[pl.BlockSpec((GRID_W, dh), lambda h, r: (r, h)), pl.BlockSpec((t, dh), lambda h, r: (0, h)),
                   pl.BlockSpec((t, dh), lambda h, r: (0, h)),
                   pl.BlockSpec((None, None, NA_ROWS, LANES), lambda h, r: (h, _na_geometry(r)[2], 0, 0))],
        out_shape=[jax.ShapeDtypeStruct((t, _na_w()), F32), jax.ShapeDtypeStruct((t, _na_w()), F32),
                   jax.ShapeDtypeStruct((t, _na_w()), F32),
                   jax.ShapeDtypeStruct((NA_HEADS, NA_ROWS, NA_ROWS, LANES), F32)],
        sem=("parallel", "arbitrary"), args=(p, p, p, rb, dmix), comm=comm)


def _rpb_rows(rpb):
    nr, nc = 2 * NA_ROWS - 1, 2 * NA_COLS - 1
    pad = jnp.pad(rpb, ((0, 0), (0, 0), (0, LANES - nc)))
    return jnp.stack([pad[:, b:b + NA_ROWS] for b in range(NA_ROWS)], axis=1)


def _rpb_rows_t(drb):
    nr, nc = 2 * NA_ROWS - 1, 2 * NA_COLS - 1
    out = jnp.zeros((NA_HEADS, nr, LANES), F32)
    for b in range(NA_ROWS):
        out = out.at[:, b:b + NA_ROWS].add(drb[:, b])
    return out[:, :, :nc]


def _assemble_dp(dqr, dkr, dvr, dgate, da, db, dnq, dnk, dnv, name):
    t = dgate.shape[0]
    tm = _tm()
    off = _offsets()
    sizes = dict(q=_ret_qk_w(), k=_ret_qk_w(), v=_ret_w(), g=_ret_w(), a=CONV_W, b=CONV_W, nq=_na_w(), nk=_na_w(), nv=_na_w())

    def body(q_ref, k_ref, v_ref, g_ref, a_ref, b_ref, nq_ref, nk_ref, nv_ref, o_ref):
        def put(n, val):
            o_ref[:, off[n]:off[n] + sizes[n]] = val.astype(BF16)

        put("q", q_ref[0] + q_ref[1])
        put("k", k_ref[0] + k_ref[1])
        put("v", v_ref[0] + v_ref[1])
        put("g", g_ref[...])
        put("a", a_ref[...])
        put("b", b_ref[...])
        put("nq", nq_ref[...])
        put("nk", nk_ref[...])
        put("nv", nv_ref[...])

    two = lambda w: pl.BlockSpec((2, tm, w), lambda i: (0, i, 0))
    one = lambda w: pl.BlockSpec((tm, w), lambda i: (i, 0))
    return pl.pallas_call(
        body, name=name, grid=(t // tm,),
        in_specs=[two(sizes["q"]), two(sizes["k"]), two(sizes["v"]), one(sizes["g"]), one(CONV_W), one(CONV_W),
                  one(_na_w()), one(_na_w()), one(_na_w())],
        out_specs=one(_d_in()), out_shape=jax.ShapeDtypeStruct((t, _d_in()), BF16),
        compiler_params=_params(("parallel",)))(dqr, dkr, dvr, dgate, da, db, dnq, dnk, dnv)


def _adamw(w, m, v, gs, name):
    nl, r, c = w.shape
    stacked = not isinstance(gs, (list, tuple))
    if stacked:
        gs = [gs]
    assert stacked or len(gs) == nl
    g_n = gs[0].shape[-3]
    block_bytes = 2 * 1024 * 1024
    rows = min(block_bytes // (4 * c), block_bytes // (g_n * c * gs[0].dtype.itemsize))
    tr = _tile(r, max(2 * SUBLANES, rows // (2 * SUBLANES) * (2 * SUBLANES)), 2 * SUBLANES)
    nt = r // tr
    c1 = 1.0 - ADAM_B1 ** ADAM_STEP
    c2 = 1.0 - ADAM_B2 ** ADAM_STEP

    def body(w_ref, m_ref, v_ref, *rest):
        g_refs, (go_ref, d_ref, mo_ref, vo_ref) = rest[:len(gs)], rest[len(gs):]
        layer = pl.program_id(0)
        for ll in range(len(gs)):
            @pl.when(jnp.logical_or(stacked, layer == ll))
            def _():
                g_ref = g_refs[ll]
                g = g_ref[0].astype(F32)
                for j in range(1, g_n):
                    g = g + g_ref[j].astype(F32)
                mn = ADAM_B1 * m_ref[...] + (1.0 - ADAM_B1) * g
                vn = ADAM_B2 * v_ref[...] + (1.0 - ADAM_B2) * (g * g)
                m_hat = mn / c1
                v_hat = vn / c2
                go_ref[...] = g
                d_ref[...] = -ADAM_LR * (m_hat / (jnp.sqrt(v_hat) + ADAM_EPS) + ADAM_WD * w_ref[...])
                mo_ref[...] = mn
                vo_ref[...] = vn

    def g_spec(ll):
        if stacked:
            return pl.BlockSpec((None, g_n, tr, c), lambda l, i: (l, 0, i, 0))
        return pl.BlockSpec((g_n, tr, c), lambda l, i: (0, jnp.where(l == ll, i, jnp.where(l < ll, 0, nt - 1)), 0))

    blk = pl.BlockSpec((None, tr, c), lambda l, i: (l, i, 0))
    sds = jax.ShapeDtypeStruct((nl, r, c), F32)
    return pl.pallas_call(
        body, name=name, grid=(nl, nt),
        in_specs=[blk, blk, blk] + [g_spec(ll) for ll in range(len(gs))],
        out_specs=[blk, blk, blk, blk], out_shape=[sds, sds, sds, sds],
        compiler_params=_params(("arbitrary", "arbitrary")))(w, m, v, *gs)


def _sum_devices(g, name):
    _, r, c = g.shape
    tr = _tile(r, 512, SUBLANES)

    def body(g_ref, o_ref):
        acc = g_ref[0]
        for j in range(1, N_DEV):
            acc = acc + g_ref[j]
        o_ref[...] = acc

    return pl.pallas_call(body, name=name, grid=(r // tr,), in_specs=[pl.BlockSpec((N_DEV, tr, c), lambda i: (0, i, 0))],
                          out_specs=pl.BlockSpec((tr, c), lambda i: (i, 0)), out_shape=jax.ShapeDtypeStruct((r, c), F32),
                          compiler_params=_params(("parallel",)))(g)


def _ada_fwd(c16, w_ada, b_shard, name):
    nl, d, cs = w_ada.shape
    tk = _tile(d, 512, LANES)
    nk = d // tk

    def body(c_ref, w_ref, b_ref, o_ref):
        kk = pl.program_id(1)

        @pl.when(kk == 0)
        def _():
            o_ref[...] = jnp.broadcast_to(b_ref[...], o_ref.shape)

        o_ref[...] += _dg(_silu(c_ref[...]), w_ref[...], 1, 0)

    return pl.pallas_call(
        body, name=name, grid=(nl, nk),
        in_specs=[pl.BlockSpec((16, tk), lambda l, kk: (0, kk)), pl.BlockSpec((None, tk, cs), lambda l, kk: (l, kk, 0)),
                  pl.BlockSpec((None, 1, cs), lambda l, kk: (l, 0, 0))],
        out_specs=pl.BlockSpec((None, 16, cs), lambda l, kk: (l, 0, 0)),
        out_shape=jax.ShapeDtypeStruct((nl, 16, cs), F32),
        compiler_params=_params(("parallel", "arbitrary")))(c16, w_ada, b_shard)


def _ada_bwd(c16, dm16, w_ada, name):
    nl, d, cs = w_ada.shape
    td = _tile(d, 512, LANES)

    def body(c_ref, dm_ref, w_ref, gw_ref, dc_ref):
        cv = c_ref[...]
        s, vjp = jax.vjp(_silu, cv)
        gw_ref[...] = _dg(s, dm_ref[...], 0, 0)
        ds = _dg(dm_ref[...], w_ref[...], 1, 1)
        dc_ref[...] = vjp(ds)[0]

    return pl.pallas_call(
        body, name=name, grid=(nl, d // td),
        in_specs=[pl.BlockSpec((16, td), lambda l, i: (0, i)), pl.BlockSpec((None, 16, cs), lambda l, i: (l, 0, 0)),
                  pl.BlockSpec((None, td, cs), lambda l, i: (l, i, 0))],
        out_specs=[pl.BlockSpec((None, td, cs), lambda l, i: (l, i, 0)), pl.BlockSpec((None, 16, td), lambda l, i: (l, 0, i))],
        out_shape=[jax.ShapeDtypeStruct((nl, d, cs), F32), jax.ShapeDtypeStruct((nl, 16, d), F32)],
        compiler_params=_params(("parallel", "parallel")))(c16, dm16, w_ada)


def _pack(arrays, row_align):
    flat = jnp.concatenate([a.reshape(-1).astype(F32) for a in arrays])
    n = flat.shape[0]
    per = LANES * row_align
    padded = ((n + per - 1) // per) * per
    return jnp.pad(flat, (0, padded - n)).reshape(padded // LANES, LANES)


def _unpack(packed, shapes):
    flat = packed.reshape(-1)
    out, o = [], 0
    for s in shapes:
        n = int(np.prod(s))
        out.append(flat[o:o + n].reshape(s))
        o += n
    return out


def _rope_tables():
    half, nf = RET_DK // 2, RET_DK // 4
    pos = jnp.arange(SEQ)
    row = (pos // GRID_W).astype(F32)
    col = (pos % GRID_W).astype(F32)
    inv = ROPE_BASE ** (-jnp.arange(nf, dtype=F32) / nf)
    ar, ac = row[:, None] * inv[None, :], col[:, None] * inv[None, :]
    cos = jnp.concatenate([jnp.cos(ar), jnp.cos(ar), jnp.cos(ac), jnp.cos(ac)], axis=-1)
    sin = jnp.concatenate([-jnp.sin(ar), jnp.sin(ar), -jnp.sin(ac), jnp.sin(ac)], axis=-1)
    cos = jnp.concatenate([jnp.ones((CTX_LEN, RET_DK), F32), cos], axis=0)
    sin = jnp.concatenate([jnp.zeros((CTX_LEN, RET_DK), F32), sin], axis=0)
    return cos, sin


def _layer_fwd(l, nl, x, mod4, w, cst, shards, full):
    n = lambda s: f"l{l}_{s}"
    d = D_MODEL

    def gather(keys):
        return _Gather([shards[k] for k in keys]) if keys else None

    w["w_in"] = _cols_from_shards(full[("w_in", l)], n("w_in_cols"))
    h1 = _normmod_fwd(x, w["norm1_g"], mod4, 0, n("norm1"))
    keys = [("ffn_up", l)] + ([("w_out", l)] if l == 0 else [])
    p, got = _mm(h1, w["w_in"], n("proj_in"), comm=gather(keys))
    full.update(zip(keys, got))
    o2, states = _ret_fwd(p, cst["cos"], cst["sin"], w["ret_decay"], cst["order"], n("ret_fwd"))
    ret_out = _ggn_fwd(o2, p, w["ret_gn_g"], n("ret_gn"))
    u2, conv_out = _conv_fwd(p, w["conv_dw_w"], w["conv_dw_b"], w["conv_ln_g"], w["conv_ln_b"], w["conv_pw"], n("conv_fwd"))
    na_out = _na_fwd(p, w["rb"], n("na_fwd"))
    mix = jnp.concatenate([ret_out, conv_out, na_out], axis=1)
    w["w_out"] = full[("w_out", l)].reshape(_d_mix(), d)
    g1, _ = _mm(mix, w["w_out"], n("proj_out"))
    x1 = _gate_res_fwd(x, g1, mod4, 2, n("res1"))
    h2 = _normmod_fwd(x1, w["norm2_g"], mod4, 1, n("norm2"))
    w["ffn_up"] = full[("ffn_up", l)]
    keys = [("ffn_down", l)] + ([("w_in", l + 1)] if l + 1 < nl else [])
    u, got = _mm(h2, w["ffn_up"], n("ffn_up"), b3=True, comm=gather(keys))
    full.update(zip(keys, got))
    a = _ffn_act_fwd(u, w["ffn_dw_w"], w["ffn_dw_b"], n("ffn_act"))
    w["ffn_down"] = full[("ffn_down", l)].reshape(D_FF, d)
    keys = [("w_out", l + 1)] if l + 1 < nl else []
    f, got = _mm(a, w["ffn_down"], n("ffn_down"), comm=gather(keys))
    full.update(zip(keys, got or []))
    x2 = _gate_res_fwd(x1, f, mod4, 5, n("res2"))
    saved = dict(x=x, h1=h1, p=p, o2=o2, states=states, u2=u2, mix=mix, g1=g1, x1=x1, h2=h2, u=u, a=a, f=f)
    return x2, saved


def _layer_bwd(l, dx2, s, mod4, w, cst, pending):
    n = lambda t: f"l{l}_{t}"
    d = D_MODEL
    nj = D_FF // _ffn_tc()
    done = {}

    def exchange(named):
        return _Exchange([a for _, a in named]) if named else None

    def collect(named, got):
        done.update({k: g for (k, _), g in zip(named, got or [])})

    dfg, dg2 = _gate_res_bwd(dx2, s["f"], mod4, 5, n("res2_bwd"))
    da, got = _mm(dfg, w["ffn_down"], n("ffn_down_dx"), tb=True, comm=exchange(pending))
    collect(pending, got)
    d_ffn_down, _ = _mm_tn(s["a"], dfg, n("ffn_down_dw"))
    dcv, dcg = _ffn_act_bwd1(s["u"], da, w["ffn_dw_w"], w["ffn_dw_b"], n("ffn_act_bwd"))
    duv, dwv, dbv = _dwconv_bwd(dcv, s["u"], w["ffn_dw_w"], 0, n("ffn_dw_bwd_val"))
    dug, dwg, dbg = _dwconv_bwd(dcg, s["u"], w["ffn_dw_w"], nj, n("ffn_dw_bwd_gate"))
    du = jnp.concatenate([duv, dug], axis=1)
    d_ffn_dw_w = jnp.concatenate([dwv, dwg], axis=1)
    d_ffn_dw_b = jnp.concatenate([dbv, dbg], axis=1)[0]
    named = [(("ffn_down", l), d_ffn_down.reshape(N_DEV, D_FF // N_DEV, d))]
    dh2, got = _mm(du, w["ffn_up"], n("ffn_up_dx"), tb=True, b3=True, comm=exchange(named))
    collect(named, got)
    d_ffn_up, _ = _mm_tn(s["h2"], du, n("ffn_up_dw"), cs=2 * D_FF // N_DEV)
    dx1, dn2, dsh2, dsc2 = _normmod_bwd(s["x1"], w["norm2_g"], mod4, 1, dh2, dx2, n("norm2_bwd"))
    dgg, dg1 = _gate_res_bwd(dx1, s["g1"], mod4, 2, n("res1_bwd"))
    dmix, _ = _mm(dgg, w["w_out"], n("proj_out_dx"), tb=True)
    d_w_out, _ = _mm_tn(s["mix"], dgg, n("proj_out_dw"))
    do, dgate, dgn = _ggn_bwd(s["o2"], s["p"], w["ret_gn_g"], dmix, n("ret_gn_bwd"))
    dqr, dkr, dvr, ddec = _ret_bwd(s["p"], cst["cos"], cst["sin"], w["ret_decay"], cst["order"], s["states"], do, n("ret_bwd"))
    du2, dlng, dlnb, dpw = _conv_bwd1(s["u2"], dmix, w["conv_ln_g"], w["conv_ln_b"], w["conv_pw"], n("conv_bwd1"))
    dca, dcb, ddww, ddwb = _conv_bwd2(du2, s["p"], w["conv_dw_w"], n("conv_bwd2"))
    named = [(("ffn_up", l), d_ffn_up)]
    (dnq, dnk, dnv, drb), got = _na_bwd(s["p"], w["rb"], dmix, n("na_bwd"), comm=exchange(named))
    collect(named, got)
    dp = _assemble_dp(dqr, dkr, dvr, dgate, dca, dcb, dnq, dnk, dnv, n("dproj"))
    named = [(("w_out", l), d_w_out.reshape(N_DEV, _d_mix() // N_DEV, d))]
    dh1, got = _mm(dp, w["w_in"], n("proj_in_dx"), tb=True, comm=exchange(named))
    collect(named, got)
    d_w_in, _ = _mm_tn(s["h1"], dp, n("proj_in_dw"), cs=_d_in() // N_DEV)
    dx, dn1, dsh1, dsc1 = _normmod_bwd(s["x"], w["norm1_g"], mod4, 0, dh1, dx1, n("norm1_bwd"))
    dmod = jnp.concatenate([dsh1, dsc1, dg1, dsh2, dsc2, dg2], axis=1)
    small = dict(norm1_g=dn1[0], ret_decay=ddec[:, :, 0, 0], ret_gn_g=dgn[0], conv_dw_w=ddww, conv_dw_b=ddwb[0],
                 conv_ln_g=dlng[0], conv_ln_b=dlnb[0], conv_pw=dpw, na_rpb=_rpb_rows_t(drb), norm2_g=dn2[0],
                 ffn_dw_w=d_ffn_dw_w, ffn_dw_b=d_ffn_dw_b)
    return dx, dmod, done, small, [(("w_in", l), d_w_in)]


def _d_mix():
    return _ret_w() + CONV_W + _na_w()


_SMALL = ["c_ctx", "b_ada", "norm1_g", "ret_decay", "ret_gn_g", "conv_dw_w", "conv_dw_b", "conv_ln_g", "conv_ln_b",
          "conv_pw", "na_rpb", "norm2_g", "ffn_dw_w", "ffn_dw_b", "final_g"]
_SMALL_SHARD_AXIS = {"conv_dw_w": 2, "conv_pw": 1, "ffn_dw_w": 2}


def kernel(x, c, ctx, c_ctx, w_ada, b_ada, norm1_g, w_in, ret_decay, ret_gn_g, conv_dw_w, conv_dw_b, conv_ln_g, conv_ln_b, conv_pw, na_rpb, w_out, norm2_g, ffn_up, ffn_dw_w, ffn_dw_b, ffn_down, final_g, loss_target, m_c_ctx, m_w_ada, m_b_ada, m_norm1_g, m_w_in, m_ret_decay, m_ret_gn_g, m_conv_dw_w, m_conv_dw_b, m_conv_ln_g, m_conv_ln_b, m_conv_pw, m_na_rpb, m_w_out, m_norm2_g, m_ffn_up, m_ffn_dw_w, m_ffn_dw_b, m_ffn_down, m_final_g, v_c_ctx, v_w_ada, v_b_ada, v_norm1_g, v_w_in, v_ret_decay, v_ret_gn_g, v_conv_dw_w, v_conv_dw_b, v_conv_ln_g, v_conv_ln_b, v_conv_pw, v_na_rpb, v_w_out, v_norm2_g, v_ffn_up, v_ffn_dw_w, v_ffn_dw_b, v_ffn_down, v_final_g):
    d, nl = D_MODEL, DEPTH
    cs = 6 * d // N_DEV
    me = _my_index()
    weights = dict(c_ctx=c_ctx, w_ada=w_ada, b_ada=b_ada, norm1_g=norm1_g, w_in=w_in, ret_decay=ret_decay, ret_gn_g=ret_gn_g,
                   conv_dw_w=conv_dw_w, conv_dw_b=conv_dw_b, conv_ln_g=conv_ln_g, conv_ln_b=conv_ln_b, conv_pw=conv_pw,
                   na_rpb=na_rpb, w_out=w_out, norm2_g=norm2_g, ffn_up=ffn_up, ffn_dw_w=ffn_dw_w, ffn_dw_b=ffn_dw_b,
                   ffn_down=ffn_down, final_g=final_g)
    mom = dict(c_ctx=m_c_ctx, w_ada=m_w_ada, b_ada=m_b_ada, norm1_g=m_norm1_g, w_in=m_w_in, ret_decay=m_ret_decay,
               ret_gn_g=m_ret_gn_g, conv_dw_w=m_conv_dw_w, conv_dw_b=m_conv_dw_b, conv_ln_g=m_conv_ln_g,
               conv_ln_b=m_conv_ln_b, conv_pw=m_conv_pw, na_rpb=m_na_rpb, w_out=m_w_out, norm2_g=m_norm2_g,
               ffn_up=m_ffn_up, ffn_dw_w=m_ffn_dw_w, ffn_dw_b=m_ffn_dw_b, ffn_down=m_ffn_down, final_g=m_final_g)
    var = dict(c_ctx=v_c_ctx, w_ada=v_w_ada, b_ada=v_b_ada, norm1_g=v_norm1_g, w_in=v_w_in, ret_decay=v_ret_decay,
               ret_gn_g=v_ret_gn_g, conv_dw_w=v_conv_dw_w, conv_dw_b=v_conv_dw_b, conv_ln_g=v_conv_ln_g,
               conv_ln_b=v_conv_ln_b, conv_pw=v_conv_pw, na_rpb=v_na_rpb, w_out=v_w_out, norm2_g=v_norm2_g,
               ffn_up=v_ffn_up, ffn_dw_w=v_ffn_dw_w, ffn_dw_b=v_ffn_dw_b, ffn_down=v_ffn_down, final_g=v_final_g)

    big_names = ["w_in", "w_out", "ffn_up", "ffn_down"]
    shards = {(nm, l): _cast_bf16(weights[nm][l], f"cast_{nm}{l}") for l in range(nl) for nm in big_names}
    small_sharded = _pack([conv_dw_w, conv_pw, ffn_dw_w], SUBLANES)
    c_rows = jnp.pad(c, ((0, SUBLANES - 1), (0, 0)))
    gathered = _run_comm(_Gather([c_rows, small_sharded, shards[("w_in", 0)]]), "gather_first")
    c_all = gathered[0][:, 0, :]
    full = {("w_in", 0): gathered[2]}
    sm = [_unpack(gathered[1][j], [conv_dw_w.shape, conv_pw.shape, ffn_dw_w.shape]) for j in range(N_DEV)]
    full_conv_dw_w = jnp.concatenate([s[0] for s in sm], axis=2)
    full_conv_pw = jnp.concatenate([s[1] for s in sm], axis=1)
    full_ffn_dw_w = jnp.concatenate([s[2] for s in sm], axis=2)

    c16 = jnp.concatenate([c_all, jnp.broadcast_to(c_ctx[None, :], (N_DEV, d))], axis=0)
    b_shard = lax.dynamic_slice_in_dim(b_ada, me * cs, cs, axis=1)[:, None, :]
    m_shard = _ada_fwd(c16, w_ada, b_shard, "ada_fwd")
    m_all = _run_comm(_Gather([m_shard.reshape(nl * 16, cs)]), "gather_mod")[0]
    m_full = m_all.reshape(N_DEV, nl, 16, cs).transpose(1, 2, 0, 3).reshape(nl, 16, 6 * d)
    m_lat = lax.dynamic_index_in_dim(m_full, me, axis=1, keepdims=False)
    mod = jnp.stack([m_full[:, N_DEV], m_lat], axis=1).reshape(nl, 2, 6, 1, d)

    cos, sin = _rope_tables()
    cst = dict(cos=cos, sin=sin, order=_chunk_order())
    layer_w = []
    for l in range(nl):
        layer_w.append(dict(
            norm1_g=norm1_g[l][None], norm2_g=norm2_g[l][None], ret_decay=ret_decay[l], ret_gn_g=ret_gn_g[l][None],
            conv_dw_w=full_conv_dw_w[l], conv_dw_b=conv_dw_b[l][None], conv_ln_g=conv_ln_g[l][None],
            conv_ln_b=conv_ln_b[l][None], conv_pw=full_conv_pw[l], rb=_rpb_rows(na_rpb[l]),
            ffn_dw_w=full_ffn_dw_w[l], ffn_dw_b=ffn_dw_b[l][None]))

    xs = jnp.concatenate([ctx[0], x[0]], axis=0)
    saved = []
    for l in range(nl):
        xs, sv = _layer_fwd(l, nl, xs, mod[l], layer_w[l], cst, shards, full)
        saved.append(sv)
    loss_tile, dxs, dfinal = _loss_head(xs, final_g[None], loss_target[0], "loss_head")
    loss = lax.psum(loss_tile[0, 0], ("x", "y", "c"))

    dmods, smalls = [None] * nl, [None] * nl
    exchanged, pending = {}, []
    for l in reversed(range(nl)):
        dxs, dmods[l], done, smalls[l], pending = _layer_bwd(l, dxs, saved[l], mod[l], layer_w[l], cst, pending)
        exchanged.update(done)
    got = _run_comm(_Exchange([a for _, a in pending]), "exchange_last")
    exchanged.update({k: g for (k, _), g in zip(pending, got)})
    grad_x = dxs[CTX_LEN:][None]

    dm_mine = jnp.stack(dmods).reshape(nl * 2, 6 * d)
    dm_rows = jnp.pad(dm_mine, ((0, SUBLANES - nl * 2), (0, 0)))
    dm_all = _run_comm(_Gather([dm_rows]), "gather_dmod")[0][:, :nl * 2].reshape(N_DEV, nl, 2, 6 * d)
    dm16_full = jnp.concatenate([dm_all[:, :, 1].transpose(1, 0, 2), dm_all[:, :, 0].transpose(1, 0, 2)], axis=1)
    dm16 = lax.dynamic_slice_in_dim(dm16_full, me * cs, cs, axis=2)
    g_w_ada, dc16 = _ada_bwd(c16, dm16, w_ada, "ada_bwd")

    small_grads = dict(
        c_ctx=jnp.sum(dc16[:, N_DEV:], axis=(0, 1)),
        b_ada=jnp.sum(jnp.stack(dmods).reshape(nl, 2, 6 * d), axis=1),
        final_g=dfinal[0])
    for nm in _SMALL:
        if nm not in small_grads:
            small_grads[nm] = jnp.stack([smalls[l][nm] for l in range(nl)])
    shapes_full = [small_grads[nm].shape for nm in _SMALL]
    packed = _pack([small_grads[nm] for nm in _SMALL], 512)
    summed = _sum_devices(_run_comm(_Gather([packed]), "gather_small_grads")[0], "sum_small_grads")
    g_small = dict(zip(_SMALL, _unpack(summed, shapes_full)))
    for nm, ax in _SMALL_SHARD_AXIS.items():
        n_sh = weights[nm].shape[ax]
        g_small[nm] = lax.dynamic_slice_in_dim(g_small[nm], me * n_sh, n_sh, axis=ax)
    shapes_own = [weights[nm].shape for nm in _SMALL]
    pk = lambda src: _pack([src[nm] for nm in _SMALL], 2 * SUBLANES)[None]
    res_small = _adamw(pk(weights), pk(mom), pk(var), pk(g_small)[:, None], "adamw_small")
    out_small = [dict(zip(_SMALL, _unpack(r[0], shapes_own))) for r in res_small]

    out_big = {}
    for nm in big_names:
        out_big[nm] = _adamw(weights[nm], mom[nm], var[nm], [exchanged[(nm, l)] for l in range(nl)], f"adamw_{nm}")
    out_big["w_ada"] = _adamw(w_ada, m_w_ada, v_w_ada, g_w_ada[:, None], "adamw_w_ada")

    names = ["c_ctx", "w_ada", "b_ada", "norm1_g", "w_in", "ret_decay", "ret_gn_g", "conv_dw_w", "conv_dw_b", "conv_ln_g",
             "conv_ln_b", "conv_pw", "na_rpb", "w_out", "norm2_g", "ffn_up", "ffn_dw_w", "ffn_dw_b", "ffn_down", "final_g"]
    outs = [loss, grad_x]
    for kind in range(4):
        for nm in names:
            outs.append(out_big[nm][kind] if nm in out_big else out_small[kind][nm])
    return tuple(outs)
```

```python
import functools
import math

import numpy as np
import jax
import jax.numpy as jnp
from jax import lax
from jax.experimental import pallas as pl
from jax.experimental.pallas import tpu as pltpu

D_MODEL = 2048
SEQ = 4096
DEPTH = 2
GRID_W = 64
CTX_LEN = 256
RET_HEADS = 4
RET_DK = 128
RET_DV = 256
RET_CHUNK = 128
CONV_W = 512
CONV_K = 31
NA_HEADS = 4
NA_DH = 128
NA_ROWS = 8
NA_COLS = 16
D_FF = 5632
FFN_K = 3
ROPE_BASE = 10000.0
EPS = 1e-6
ADAM_LR = 0.001
ADAM_B1 = 0.9
ADAM_B2 = 0.999
ADAM_EPS = 1e-08
ADAM_WD = 0.01
ADAM_STEP = 10
N_DEV = 8

LANES = 128
SUBLANES = 8
VMEM_LIMIT = 56 * 1024 * 1024
ROW_CHUNK = 16

F32 = jnp.float32
BF16 = jnp.bfloat16
MESH = pl.DeviceIdType.MESH
NEG = -1e30


def _ret_qk_w():
    return RET_HEADS * RET_DK


def _ret_w():
    return RET_HEADS * RET_DV


def _na_w():
    return NA_HEADS * NA_DH


def _d_in():
    return 2 * _ret_qk_w() + 2 * _ret_w() + 2 * CONV_W + 3 * _na_w()


def _offsets():
    sizes = [_ret_qk_w(), _ret_qk_w(), _ret_w(), _ret_w(), CONV_W, CONV_W, _na_w(), _na_w(), _na_w()]
    offs = [0]
    for s in sizes[:-1]:
        offs.append(offs[-1] + s)
    return dict(zip(["q", "k", "v", "g", "a", "b", "nq", "nk", "nv"], offs))


def _t_rows():
    return CTX_LEN + SEQ


def _tm():
    return CTX_LEN


def _params(sem=None):
    kw = dict(vmem_limit_bytes=VMEM_LIMIT)
    if sem is not None:
        kw["dimension_semantics"] = sem
    return pltpu.CompilerParams(**kw)


def _tile(n, pref, align):
    best = None
    for t in range(align, min(n, pref) + 1, align):
        if n % t == 0:
            best = t
    return best if best is not None else n


def _dg(a, b, ca, cb):
    return lax.dot_general(a.astype(BF16), b.astype(BF16), (((ca,), (cb,)), ((), ())), preferred_element_type=F32)


@jax.custom_vjp
def dot_nn(a, b):
    return _dg(a, b, 1, 0)


dot_nn.defvjp(lambda a, b: (_dg(a, b, 1, 0), (a, b)),
              lambda r, g: (_dg(g, r[1], 1, 1), _dg(r[0], g, 0, 0)))


@jax.custom_vjp
def dot_nt(a, b):
    return _dg(a, b, 1, 1)


dot_nt.defvjp(lambda a, b: (_dg(a, b, 1, 1), (a, b)),
              lambda r, g: (_dg(g, r[1], 1, 0), _dg(g, r[0], 0, 0)))


@jax.custom_vjp
def dot_tn(a, b):
    return _dg(a, b, 0, 0)


dot_tn.defvjp(lambda a, b: (_dg(a, b, 0, 0), (a, b)),
              lambda r, g: (_dg(r[1], g, 1, 1), _dg(r[0], g, 1, 0)))


def _sigmoid(x):
    return 1.0 / (1.0 + jnp.exp(-x))


def _silu(x):
    return x * _sigmoid(x)


def _my_pos():
    return lax.axis_index("x"), lax.axis_index("y"), lax.axis_index("c")


def _my_index():
    x, y, c = _my_pos()
    return 4 * x + 2 * y + c


_ANY = pl.BlockSpec(memory_space=pl.ANY)


class _Gather:
    def __init__(self, arrays):
        self.arrays = list(arrays)
        n = len(self.arrays)
        self.out_shape = [jax.ShapeDtypeStruct((N_DEV,) + a.shape, a.dtype) for a in self.arrays]
        self.scratch = [pltpu.SemaphoreType.DMA((n, 7)), pltpu.SemaphoreType.DMA((n, 7)), pltpu.SemaphoreType.DMA((n,))]

    def _plan(self, xs, outs, sems):
        send_sems, recv_sems, local_sems = sems
        n = len(self.arrays)
        x, y, c = _my_pos()
        me, sibling = (x, y, c), (x, y, 1 - c)
        chips = [(1 - x, y), (x, 1 - y), (1 - x, 1 - y)]

        def slot(a, p):
            return outs[a].at[4 * p[0] + 2 * p[1] + p[2]]

        def copy(a, k, block, to, src=None):
            return pltpu.make_async_remote_copy(
                src_ref=slot(a, block) if src is None else src, dst_ref=slot(a, block),
                send_sem=send_sems.at[a, k], recv_sem=recv_sems.at[a, k], device_id=to, device_id_type=MESH)

        mine = [pltpu.make_async_copy(xs[a], slot(a, me), local_sems.at[a]) for a in range(n)]
        first = []
        for a in range(n):
            first.append(copy(a, 0, me, sibling, src=xs[a]))
            first += [copy(a, 1 + j, me, (*chip, c), src=xs[a]) for j, chip in enumerate(chips)]
        return n, c, me, sibling, chips, copy, mine, first

    def start(self, xs, outs, sems):
        _, _, _, _, _, _, mine, first = self._plan(xs, outs, sems)
        for m in mine:
            m.start()
        for cp in first:
            cp.start()

    def finish(self, xs, outs, sems):
        n, c, me, sibling, chips, copy, mine, first = self._plan(xs, outs, sems)
        passed = []
        for a in range(n):
            for j, chip in enumerate(chips):
                copy(a, 1 + j, (*chip, c), me).wait_recv()
                p = copy(a, 4 + j, (*chip, c), sibling)
                p.start()
                passed.append(p)
        for a in range(n):
            copy(a, 0, sibling, me).wait_recv()
            for j, chip in enumerate(chips):
                copy(a, 4 + j, (*chip, 1 - c), me).wait_recv()
        for cp in first + passed:
            cp.wait_send()
        for m in mine:
            m.wait()


class _Exchange:
    def __init__(self, arrays):
        self.arrays = list(arrays)
        n = len(self.arrays)
        self.out_shape = [jax.ShapeDtypeStruct(a.shape, a.dtype) for a in self.arrays]
        self.scratch = [pltpu.SemaphoreType.DMA((n, 7)), pltpu.SemaphoreType.DMA((n, 7)), pltpu.SemaphoreType.DMA((n,))]

    def _plan(self, xs, outs, sems):
        send_sems, recv_sems, local_sems = sems
        x, y, c = _my_pos()
        me = 4 * x + 2 * y + c
        mine, sends, recvs = [], [], []
        for a in range(len(self.arrays)):
            mine.append(pltpu.make_async_copy(xs[a].at[me], outs[a].at[me], local_sems.at[a]))
        for k in range(1, N_DEV):
            px = 1 - x if (k >> 2) & 1 else x
            py = 1 - y if (k >> 1) & 1 else y
            pc = 1 - c if k & 1 else c
            peer = 4 * px + 2 * py + pc
            for a in range(len(self.arrays)):
                sends.append(pltpu.make_async_remote_copy(
                    src_ref=xs[a].at[peer], dst_ref=outs[a].at[me], send_sem=send_sems.at[a, k - 1],
                    recv_sem=recv_sems.at[a, k - 1], device_id=(px, py, pc), device_id_type=MESH))
                recvs.append(pltpu.make_async_remote_copy(
                    src_ref=xs[a].at[me], dst_ref=outs[a].at[peer], send_sem=send_sems.at[a, k - 1],
                    recv_sem=recv_sems.at[a, k - 1], device_id=(px, py, pc), device_id_type=MESH))
        return mine, sends, recvs

    def start(self, xs, outs, sems):
        mine, sends, _ = self._plan(xs, outs, sems)
        for m in mine:
            m.start()
        for s in sends:
            s.start()

    def finish(self, xs, outs, sems):
        mine, sends, recvs = self._plan(xs, outs, sems)
        for r in recvs:
            r.wait_recv()
        for s in sends:
            s.wait_send()
        for m in mine:
            m.wait()


def _run_comm(comm, name):
    n = len(comm.arrays)

    def body(*refs):
        xs, outs, sems = refs[:n], refs[n:2 * n], refs[2 * n:]
        comm.start(xs, outs, sems)
        comm.finish(xs, outs, sems)

    return pl.pallas_call(body, name=name, out_shape=comm.out_shape, in_specs=[_ANY] * n, out_specs=[_ANY] * n,
                          scratch_shapes=comm.scratch)(*comm.arrays)


def _call(body, *, name, grid, in_specs, out_specs, out_shape, args, scratch=(), sem=None, comm=None):
    if comm is None:
        res = pl.pallas_call(body, name=name, grid=grid, in_specs=list(in_specs), out_specs=list(out_specs),
                             out_shape=list(out_shape), scratch_shapes=list(scratch), compiler_params=_params(sem))(*args)
        return list(res), None
    n_in, n_out, n_scr = len(in_specs), len(out_specs), len(scratch)
    c_n = len(comm.arrays)

    def wrapped(*refs):
        ins, cin = refs[:n_in], refs[n_in:n_in + c_n]
        o0 = n_in + c_n
        outs, cout = refs[o0:o0 + n_out], refs[o0 + n_out:o0 + n_out + c_n]
        s0 = o0 + n_out + c_n
        scr, cscr = refs[s0:s0 + n_scr], refs[s0 + n_scr:]
        ids = [pl.program_id(ax) for ax in range(len(grid))]
        first = functools.reduce(jnp.logical_and, [i == 0 for i in ids])
        last = functools.reduce(jnp.logical_and, [i == g - 1 for i, g in zip(ids, grid)])

        @pl.when(first)
        def _():
            comm.start(cin, cout, cscr)

        body(*ins, *outs, *scr)

        @pl.when(last)
        def _():
            comm.finish(cin, cout, cscr)

    res = pl.pallas_call(
        wrapped, name=name, grid=grid, in_specs=list(in_specs) + [_ANY] * c_n, out_specs=list(out_specs) + [_ANY] * c_n,
        out_shape=list(out_shape) + list(comm.out_shape), scratch_shapes=list(scratch) + list(comm.scratch),
        compiler_params=_params(("arbitrary",) * len(grid)))(*args, *comm.arrays)
    return list(res[:n_out]), list(res[n_out:])


MM_B_BLOCK_BYTES = 6 * 1024 * 1024
MM_O_BLOCK_BYTES = 13 * 1024 * 1024 // 2


def _mm(a, b, name, tb=False, out_dtype=F32, b3=False, o_cs=None, tm_max=1088, comm=None):
    m, k = a.shape
    if b3:
        cs = b.shape[2]
        n, kb = (b.shape[1], N_DEV * cs) if tb else (N_DEV * cs, b.shape[1])
    else:
        n, kb = (b.shape[0], b.shape[1]) if tb else (b.shape[1], b.shape[0])
    assert k == kb, (a.shape, b.shape, tb)
    tm = _tile(m, tm_max, 2 * SUBLANES)
    tk = cs if (b3 and tb) else k
    nk = k // tk
    if b3 and not tb:
        tn = cs
    elif o_cs is not None:
        tn = o_cs if o_cs % LANES == 0 else 2 * o_cs
    else:
        tn = _tile(n, min(MM_B_BLOCK_BYTES // (2 * tk), MM_O_BLOCK_BYTES // (4 * tm)), LANES)
    cb = 1 if tb else 0
    dn = (((1,), (cb,)), ((), ()))

    def body_one(a_ref, b_ref, o_ref):
        r = lax.dot_general(a_ref[...], b_ref[...], dn, preferred_element_type=F32)
        if o_cs is None:
            o_ref[...] = r.astype(o_ref.dtype)
        else:
            for j in range(tn // o_cs):
                o_ref[j] = r[:, j * o_cs:(j + 1) * o_cs].astype(o_ref.dtype)

    def body_acc(a_ref, b_ref, o_ref, acc_ref):
        kk = pl.program_id(2)
        prod = lax.dot_general(a_ref[...], b_ref[...], dn, preferred_element_type=F32)

        @pl.when(kk == 0)
        def _():
            acc_ref[...] = prod

        @pl.when(kk > 0)
        def _():
            acc_ref[...] += prod

        @pl.when(kk == nk - 1)
        def _():
            o_ref[...] = acc_ref[...].astype(o_ref.dtype)

    a_spec = pl.BlockSpec((tm, tk), lambda i, j, kk: (i, kk))
    if b3:
        b_spec = (pl.BlockSpec((None, tn, cs), lambda i, j, kk: (kk, j, 0)) if tb
                  else pl.BlockSpec((None, tk, cs), lambda i, j, kk: (j, kk, 0)))
    else:
        b_spec = pl.BlockSpec((tn, tk), lambda i, j, kk: (j, kk)) if tb else pl.BlockSpec((tk, tn), lambda i, j, kk: (kk, j))
    if o_cs is None:
        o_spec = pl.BlockSpec((tm, tn), lambda i, j, kk: (i, j))
        o_shape = jax.ShapeDtypeStruct((m, n), out_dtype)
    else:
        assert nk == 1
        o_spec = pl.BlockSpec((tn // o_cs, tm, o_cs), lambda i, j, kk: (j, i, 0))
        o_shape = jax.ShapeDtypeStruct((n // o_cs, m, o_cs), out_dtype)
    res, cres = _call(
        body_one if nk == 1 else body_acc, name=name, grid=(m // tm, n // tn, nk), in_specs=[a_spec, b_spec],
        out_specs=[o_spec], out_shape=[o_shape],
        scratch=[] if nk == 1 else [pltpu.VMEM((tm, tn), F32)], sem=("parallel", "parallel", "arbitrary"),
        args=(a, b), comm=comm)
    return res[0], cres


DW_TM = 512


def _transpose_bf16(x, name):
    t, c = x.shape
    tt = _tm()

    def body(x_ref, o_ref):
        o_ref[...] = x_ref[...].T

    return pl.pallas_call(body, name=name, grid=(t // tt,), in_specs=[pl.BlockSpec((tt, c), lambda i: (i, 0))],
                          out_specs=pl.BlockSpec((c, tt), lambda i: (0, i)),
                          out_shape=jax.ShapeDtypeStruct((c, t), BF16), compiler_params=_params(("parallel",)))(x)


def _cast_bf16(x, name):
    r, c = x.shape
    tr = _tile(r, 512, 2 * SUBLANES)

    def body(x_ref, o_ref):
        o_ref[...] = x_ref[...].astype(BF16)

    return pl.pallas_call(body, name=name, grid=(r // tr,), in_specs=[pl.BlockSpec((tr, c), lambda i: (i, 0))],
                          out_specs=pl.BlockSpec((tr, c), lambda i: (i, 0)),
                          out_shape=jax.ShapeDtypeStruct((r, c), BF16), compiler_params=_params(("parallel",)))(x)


def _cols_from_shards(wg, name):
    _, k, cs = wg.shape
    tk = _tile(k, 256, 2 * SUBLANES)

    def body(w_ref, o_ref):
        for j in range(N_DEV):
            o_ref[:, j * cs:(j + 1) * cs] = w_ref[j]

    return pl.pallas_call(body, name=name, grid=(k // tk,),
                          in_specs=[pl.BlockSpec((N_DEV, tk, cs), lambda i: (0, i, 0))],
                          out_specs=pl.BlockSpec((tk, N_DEV * cs), lambda i: (i, 0)),
                          out_shape=jax.ShapeDtypeStruct((k, N_DEV * cs), wg.dtype),
                          compiler_params=_params(("parallel",)))(wg)


def _stream(i):
    return jnp.minimum(i, 1)


def _normmod(x, g, sh, sc):
    y = x * lax.rsqrt(jnp.mean(x * x, axis=-1, keepdims=True) + EPS)
    return (y * g) * (1.0 + sc) + sh


def _mod_spec(chunk, d):
    return pl.BlockSpec((None, None, 1, d), lambda i: (_stream(i), chunk, 0, 0))


def _normmod_fwd(x, g, mod4, which, name):
    t, d = x.shape
    tm = _tm()
    ish, isc = (0, 1) if which == 0 else (3, 4)

    def body(x_ref, g_ref, sh_ref, sc_ref, o_ref):
        o_ref[...] = _normmod(x_ref[...], g_ref[...], sh_ref[...], sc_ref[...]).astype(BF16)

    row = pl.BlockSpec((tm, d), lambda i: (i, 0))
    return pl.pallas_call(body, name=name, grid=(t // tm,),
                          in_specs=[row, pl.BlockSpec((1, d), lambda i: (0, 0)), _mod_spec(ish, d), _mod_spec(isc, d)],
                          out_specs=row, out_shape=jax.ShapeDtypeStruct((t, d), BF16),
                          compiler_params=_params(("parallel",)))(x, g, mod4, mod4)


def _normmod_bwd(x, g, mod4, which, dh, dres, name):
    t, d = x.shape
    tm = _tm()
    ish, isc = (0, 1) if which == 0 else (3, 4)

    def body(x_ref, g_ref, sh_ref, sc_ref, dh_ref, dres_ref, dx_ref, dg_ref, dsh_ref, dsc_ref):
        i = pl.program_id(0)
        _, vjp = jax.vjp(_normmod, x_ref[...], g_ref[...], sh_ref[...], sc_ref[...])
        dx, dg, dsh, dsc = vjp(dh_ref[...])
        dx_ref[...] = dres_ref[...] + dx

        @pl.when(i == 0)
        def _():
            dg_ref[...] = jnp.zeros_like(dg_ref)

        @pl.when(i <= 1)
        def _():
            dsh_ref[...] = jnp.zeros_like(dsh_ref)
            dsc_ref[...] = jnp.zeros_like(dsc_ref)

        dg_ref[...] += dg
        dsh_ref[...] += dsh
        dsc_ref[...] += dsc

    row = pl.BlockSpec((tm, d), lambda i: (i, 0))
    vec = pl.BlockSpec((1, d), lambda i: (0, 0))
    svec = pl.BlockSpec((None, 1, d), lambda i: (_stream(i), 0, 0))
    return pl.pallas_call(
        body, name=name, grid=(t // tm,),
        in_specs=[row, vec, _mod_spec(ish, d), _mod_spec(isc, d), row, row],
        out_specs=[row, vec, svec, svec],
        out_shape=[jax.ShapeDtypeStruct((t, d), F32), jax.ShapeDtypeStruct((1, d), F32),
                   jax.ShapeDtypeStruct((2, 1, d), F32), jax.ShapeDtypeStruct((2, 1, d), F32)],
        compiler_params=_params(("arbitrary",)))(x, g, mod4, mod4, dh, dres)


def _gate_res_fwd(x, f, mod4, chunk, name):
    t, d = x.shape
    tm = _tm()

    def body(x_ref, f_ref, g_ref, o_ref):
        o_ref[...] = x_ref[...] + g_ref[...] * f_ref[...]

    row = pl.BlockSpec((tm, d), lambda i: (i, 0))
    return pl.pallas_call(body, name=name, grid=(t // tm,), in_specs=[row, row, _mod_spec(chunk, d)], out_specs=row,
                          out_shape=jax.ShapeDtypeStruct((t, d), F32), compiler_params=_params(("parallel",)))(x, f, mod4)


def _gate_res_bwd(dx, f, mod4, chunk, name):
    t, d = dx.shape
    tm = _tm()

    def body(dx_ref, f_ref, g_ref, o_ref, dg_ref):
        i = pl.program_id(0)
        dxv = dx_ref[...]
        o_ref[...] = (dxv * g_ref[...]).astype(BF16)

        @pl.when(i <= 1)
        def _():
            dg_ref[...] = jnp.zeros_like(dg_ref)

        dg_ref[...] += jnp.sum(dxv * f_ref[...], axis=0, keepdims=True)

    row = pl.BlockSpec((tm, d), lambda i: (i, 0))
    return pl.pallas_call(
        body, name=name, grid=(t // tm,), in_specs=[row, row, _mod_spec(chunk, d)],
        out_specs=[row, pl.BlockSpec((None, 1, d), lambda i: (_stream(i), 0, 0))],
        out_shape=[jax.ShapeDtypeStruct((t, d), BF16), jax.ShapeDtypeStruct((2, 1, d), F32)],
        compiler_params=_params(("arbitrary",)))(dx, f, mod4)


def _loss_head(x, final_g, target, name):
    t, d = x.shape
    tm = _tm()

    def loss_fn(xv, g, tgt):
        y = (xv * lax.rsqrt(jnp.mean(xv * xv, axis=-1, keepdims=True) + EPS)) * g
        err = y - tgt
        return 0.5 * jnp.sum(jnp.mean(err * err, axis=-1, keepdims=True))

    def body(x_ref, g_ref, t_ref, l_ref, dx_ref, dg_ref):
        i = pl.program_id(0)

        @pl.when(i == 0)
        def _():
            l_ref[...] = jnp.zeros_like(l_ref)
            dg_ref[...] = jnp.zeros_like(dg_ref)
            dx_ref[...] = jnp.zeros_like(dx_ref)

        @pl.when(i > 0)
        def _():
            l, (dx, dg) = jax.value_and_grad(loss_fn, argnums=(0, 1))(x_ref[...], g_ref[...], t_ref[...])
            l_ref[...] += jnp.full(l_ref.shape, l, F32)
            dx_ref[...] = dx
            dg_ref[...] += dg

    row = pl.BlockSpec((tm, d), lambda i: (i, 0))
    vec = pl.BlockSpec((1, d), lambda i: (0, 0))
    return pl.pallas_call(
        body, name=name, grid=(t // tm,),
        in_specs=[row, vec, pl.BlockSpec((tm, d), lambda i: (jnp.maximum(i - 1, 0), 0))],
        out_specs=[pl.BlockSpec((SUBLANES, LANES), lambda i: (0, 0)), row, vec],
        out_shape=[jax.ShapeDtypeStruct((SUBLANES, LANES), F32), jax.ShapeDtypeStruct((t, d), F32),
                   jax.ShapeDtypeStruct((1, d), F32)],
        compiler_params=_params(("arbitrary",)))(x, final_g, target)


def _swap_quarters(x):
    half, nf = RET_DK // 2, RET_DK // 4
    lane = lax.broadcasted_iota(jnp.int32, x.shape, 1)
    return jnp.where((lane % half) < nf, pltpu.roll(x, RET_DK - nf, 1), pltpu.roll(x, nf, 1))


def _rope(x, cos, sin):
    return x * cos + _swap_quarters(x) * sin


def _rope_t(y, cos, sin):
    return y * cos + _swap_quarters(y * sin)


def _ret_consts(d):
    c = RET_CHUNK
    ii = lax.broadcasted_iota(jnp.int32, (c, 1), 0).astype(F32)
    jj = lax.broadcasted_iota(jnp.int32, (1, c), 1).astype(F32)
    fwd = d == 0
    sgn = jnp.where(fwd, 1.0, -1.0).astype(F32)
    pos = jnp.where(fwd, ii, c - 1.0 - ii)
    return sgn * (ii - jj), pos


def _ret_step(lgt, state, q, k, v, diff, pos):
    c = float(RET_CHUNK)
    lg = -(jnp.maximum(-lgt, 0.0) + jnp.log1p(jnp.exp(-jnp.abs(lgt))))
    lower = diff >= 0
    decay = jnp.where(lower, jnp.exp(jnp.where(lower, diff, 0.0) * lg), 0.0)
    xi = jnp.exp((pos + 1.0) * lg)
    zeta = jnp.exp((c - 1.0 - pos) * lg)
    gch = jnp.exp(c * lg)
    inner = dot_nt(q, k) * decay
    out = dot_nn(inner, v) + dot_nn(q, state) * xi
    new_state = state * gch + dot_tn(k * zeta, v)
    return out, new_state


def _chunk_order():
    nc, nch = CTX_LEN // RET_CHUNK, _t_rows() // RET_CHUNK
    fwd = list(range(nch))
    bwd = list(range(nc - 1, -1, -1)) + list(range(nch - 1, nc - 1, -1))
    return jnp.asarray(np.array([fwd, bwd], np.int32))


def _ret_fwd(p, cos, sin, decay, order, name):
    t = p.shape[0]
    c, dk, dv, nh = RET_CHUNK, RET_DK, RET_DV, RET_HEADS
    nch = t // c
    off = _offsets()
    qb, kb, vb = off["q"] // dk, off["k"] // dk, off["v"] // dv
    scale = RET_DK ** -0.5

    def body(ord_ref, dec_ref, q_ref, k_ref, v_ref, cos_ref, sin_ref, o_ref, st_ref, state):
        d, h, s = pl.program_id(0), pl.program_id(1), pl.program_id(2)

        @pl.when(s == 0)
        def _():
            state[...] = jnp.zeros_like(state)

        st_ref[...] = state[...]
        diff, pos = _ret_consts(d)
        lgt = jnp.full((1, 1), dec_ref[d, h], F32)
        q = _rope(q_ref[...], cos_ref[...], sin_ref[...]) * scale
        k = _rope(k_ref[...], cos_ref[...], sin_ref[...])
        out, ns = _ret_step(lgt, state[...], q, k, v_ref[...], diff, pos)
        o_ref[...] = out
        state[...] = ns

    grid_spec = pltpu.PrefetchScalarGridSpec(
        num_scalar_prefetch=1, grid=(2, nh, nch),
        in_specs=[pl.BlockSpec(memory_space=pltpu.SMEM),
                  pl.BlockSpec((c, dk), lambda d, h, s, o: (o[d, s], qb + h)),
                  pl.BlockSpec((c, dk), lambda d, h, s, o: (o[d, s], kb + h)),
                  pl.BlockSpec((c, dv), lambda d, h, s, o: (o[d, s], vb + h)),
                  pl.BlockSpec((c, dk), lambda d, h, s, o: (o[d, s], 0)),
                  pl.BlockSpec((c, dk), lambda d, h, s, o: (o[d, s], 0))],
        out_specs=[pl.BlockSpec((None, c, dv), lambda d, h, s, o: (d, o[d, s], h)),
                   pl.BlockSpec((None, None, None, dk, dv), lambda d, h, s, o: (d, h, s, 0, 0))],
        scratch_shapes=[pltpu.VMEM((dk, dv), F32)])
    return pl.pallas_call(
        body, name=name, grid_spec=grid_spec,
        out_shape=[jax.ShapeDtypeStruct((2, t, nh * dv), F32), jax.ShapeDtypeStruct((2, nh, nch, dk, dv), F32)],
        compiler_params=_params(("arbitrary", "arbitrary", "arbitrary")))(order, decay, p, p, p, cos, sin)


def _ret_bwd(p, cos, sin, decay, order, states, do, name):
    t = p.shape[0]
    c, dk, dv, nh = RET_CHUNK, RET_DK, RET_DV, RET_HEADS
    nch = t // c
    off = _offsets()
    qb, kb, vb = off["q"] // dk, off["k"] // dk, off["v"] // dv
    scale = RET_DK ** -0.5

    def body(ord_ref, dec_ref, q_ref, k_ref, v_ref, cos_ref, sin_ref, st_ref, do_ref,
             dq_ref, dk_ref, dv_ref, dd_ref, dstate):
        d, h, s = pl.program_id(0), pl.program_id(1), pl.program_id(2)

        @pl.when(s == 0)
        def _():
            dstate[...] = jnp.zeros_like(dstate)
            dd_ref[...] = jnp.zeros_like(dd_ref)

        diff, pos = _ret_consts(d)
        lgt = jnp.full((1, 1), dec_ref[d, h], F32)
        cosv, sinv = cos_ref[...], sin_ref[...]
        q = _rope(q_ref[...], cosv, sinv) * scale
        k = _rope(k_ref[...], cosv, sinv)
        _, vjp = jax.vjp(lambda a, b, cq, ck, cv: _ret_step(a, b, cq, ck, cv, diff, pos),
                         lgt, st_ref[...], q, k, v_ref[...])
        dlgt, dst, dq, dkk, dvv = vjp((do_ref[...], dstate[...]))
        dstate[...] = dst
        dq_ref[...] = _rope_t(dq * scale, cosv, sinv)
        dk_ref[...] = _rope_t(dkk, cosv, sinv)
        dv_ref[...] = dvv
        dd_ref[...] += jnp.broadcast_to(dlgt, dd_ref.shape)

    rev = lambda o, d, s: o[d, nch - 1 - s]
    grid_spec = pltpu.PrefetchScalarGridSpec(
        num_scalar_prefetch=1, grid=(2, nh, nch),
        in_specs=[pl.BlockSpec(memory_space=pltpu.SMEM),
                  pl.BlockSpec((c, dk), lambda d, h, s, o: (rev(o, d, s), qb + h)),
                  pl.BlockSpec((c, dk), lambda d, h, s, o: (rev(o, d, s), kb + h)),
                  pl.BlockSpec((c, dv), lambda d, h, s, o: (rev(o, d, s), vb + h)),
                  pl.BlockSpec((c, dk), lambda d, h, s, o: (rev(o, d, s), 0)),
                  pl.BlockSpec((c, dk), lambda d, h, s, o: (rev(o, d, s), 0)),
                  pl.BlockSpec((None, None, None, dk, dv), lambda d, h, s, o: (d, h, nch - 1 - s, 0, 0)),
                  pl.BlockSpec((c, dv), lambda d, h, s, o: (rev(o, d, s), h))],
        out_specs=[pl.BlockSpec((None, c, dk), lambda d, h, s, o: (d, rev(o, d, s), h)),
                   pl.BlockSpec((None, c, dk), lambda d, h, s, o: (d, rev(o, d, s), h)),
                   pl.BlockSpec((None, c, dv), lambda d, h, s, o: (d, rev(o, d, s), h)),
                   pl.BlockSpec((None, None, SUBLANES, LANES), lambda d, h, s, o: (d, h, 0, 0))],
        scratch_shapes=[pltpu.VMEM((dk, dv), F32)])
    return pl.pallas_call(
        body, name=name, grid_spec=grid_spec,
        out_shape=[jax.ShapeDtypeStruct((2, t, nh * dk), F32), jax.ShapeDtypeStruct((2, t, nh * dk), F32),
                   jax.ShapeDtypeStruct((2, t, nh * dv), F32), jax.ShapeDtypeStruct((2, nh, SUBLANES, LANES), F32)],
        compiler_params=_params(("arbitrary", "arbitrary", "arbitrary")))(order, decay, p, p, p, cos, sin, states, do)


def _ggn_head(of, ob, gate, g):
    o = of + ob
    mu = jnp.mean(o, axis=-1, keepdims=True)
    var = jnp.mean(jnp.square(o - mu), axis=-1, keepdims=True)
    return ((o - mu) * lax.rsqrt(var + EPS) * g) * _silu(gate)


def _ggn_fwd(o2, p, gn_g, name):
    t = p.shape[0]
    tm, w, dv = _tm(), _ret_w(), RET_DV
    gb = _offsets()["g"] // w

    def body(o_ref, gate_ref, g_ref, out_ref):
        for h in range(RET_HEADS):
            sl = slice(h * dv, (h + 1) * dv)
            out_ref[:, sl] = _ggn_head(o_ref[0, :, sl], o_ref[1, :, sl], gate_ref[:, sl], g_ref[:, sl]).astype(BF16)

    return pl.pallas_call(
        body, name=name, grid=(t // tm,),
        in_specs=[pl.BlockSpec((2, tm, w), lambda i: (0, i, 0)), pl.BlockSpec((tm, w), lambda i: (i, gb)),
                  pl.BlockSpec((1, w), lambda i: (0, 0))],
        out_specs=pl.BlockSpec((tm, w), lambda i: (i, 0)), out_shape=jax.ShapeDtypeStruct((t, w), BF16),
        compiler_params=_params(("parallel",)))(o2, p, gn_g)


def _ggn_bwd(o2, p, gn_g, dmix, name):
    t = p.shape[0]
    tm, w, dv = _tm(), _ret_w(), RET_DV
    gb = _offsets()["g"] // w

    def body(o_ref, gate_ref, g_ref, dy_ref, do_ref, dgate_ref, dg_ref):
        i = pl.program_id(0)

        @pl.when(i == 0)
        def _():
            dg_ref[...] = jnp.zeros_like(dg_ref)

        for h in range(RET_HEADS):
            sl = slice(h * dv, (h + 1) * dv)
            _, vjp = jax.vjp(_ggn_head, o_ref[0, :, sl], o_ref[1, :, sl], gate_ref[:, sl], g_ref[:, sl])
            do, _, dgate, dg = vjp(dy_ref[:, sl])
            do_ref[:, sl] = do
            dgate_ref[:, sl] = dgate
            dg_ref[:, sl] += dg

    row = pl.BlockSpec((tm, w), lambda i: (i, 0))
    return pl.pallas_call(
        body, name=name, grid=(t // tm,),
        in_specs=[pl.BlockSpec((2, tm, w), lambda i: (0, i, 0)), pl.BlockSpec((tm, w), lambda i: (i, gb)),
                  pl.BlockSpec((1, w), lambda i: (0, 0)), row],
        out_specs=[row, row, pl.BlockSpec((1, w), lambda i: (0, 0))],
        out_shape=[jax.ShapeDtypeStruct((t, w), F32), jax.ShapeDtypeStruct((t, w), F32),
                   jax.ShapeDtypeStruct((1, w), F32)],
        compiler_params=_params(("arbitrary",)))(o2, p, gn_g, dmix)


def _halo(k):
    return SUBLANES * ((k // 2 + SUBLANES - 1) // SUBLANES)


def _halo_specs(width, colblock, h, tm):
    r = tm // h
    return [pl.BlockSpec((h, width), lambda i, *_: (jnp.maximum(i * r - 1, 0), colblock(*_))),
            pl.BlockSpec((tm, width), lambda i, *_: (i, colblock(*_))),
            pl.BlockSpec((h, width), lambda i, *_: (jnp.minimum((i + 1) * r, (_t_rows() // h) - 1), colblock(*_)))]


def _fill_ext(ext_ref, prev, cur, nxt, i, h, tm):
    nt = _t_rows() // tm
    ext_ref[0:h, :] = jnp.where(i >= 2, prev, 0.0)
    ext_ref[h:h + tm, :] = cur
    ext_ref[h + tm:h + tm + h, :] = jnp.where((i >= 1) & (i <= nt - 2), nxt, 0.0)


def _corr(ext_ref, w_ref, k, h, tm, flip):
    pad = k // 2
    acc = None
    for kk in range(k):
        o = h + (pad - kk if flip else kk - pad)
        term = w_ref[kk:kk + 1, :] * ext_ref[o:o + tm, :]
        acc = term if acc is None else acc + term
    return acc


def _conv_post(u2, ln_g, ln_b, pw):
    mu = jnp.mean(u2, axis=-1, keepdims=True)
    var = jnp.mean(jnp.square(u2 - mu), axis=-1, keepdims=True)
    y = (u2 - mu) * lax.rsqrt(var + EPS) * ln_g + ln_b
    return dot_nn(_silu(y), pw)


def _conv_fwd(p, dw_w, dw_b, ln_g, ln_b, pw, name):
    t = p.shape[0]
    tm, w, k = _tm(), CONV_W, CONV_K
    h = _halo(k)
    off = _offsets()
    ab, bb = off["a"] // w, off["b"] // w

    def body(ap, ac, an, bp, bc, bn, w_ref, b_ref, g_ref, beta_ref, pw_ref, u2_ref, out_ref, ext):
        i = pl.program_id(0)
        glu = lambda a, b: a * _sigmoid(b)
        _fill_ext(ext, glu(ap[...], bp[...]), glu(ac[...], bc[...]), glu(an[...], bn[...]), i, h, tm)
        u2 = _corr(ext, w_ref, k, h, tm, False) + b_ref[...]
        u2_ref[...] = u2
        out_ref[...] = _conv_post(u2, g_ref[...], beta_ref[...], pw_ref[...]).astype(BF16)

    vec = pl.BlockSpec((1, w), lambda i: (0, 0))
    row = pl.BlockSpec((tm, w), lambda i: (i, 0))
    return pl.pallas_call(
        body, name=name, grid=(t // tm,),
        in_specs=_halo_specs(w, lambda: ab, h, tm) + _halo_specs(w, lambda: bb, h, tm)
        + [pl.BlockSpec((k, w), lambda i: (0, 0)), vec, vec, vec, pl.BlockSpec((w, w), lambda i: (0, 0))],
        out_specs=[row, row],
        out_shape=[jax.ShapeDtypeStruct((t, w), F32), jax.ShapeDtypeStruct((t, w), BF16)],
        scratch_shapes=[pltpu.VMEM((tm + 2 * h, w), F32)],
        compiler_params=_params(("parallel",)))(p, p, p, p, p, p, dw_w, dw_b, ln_g, ln_b, pw)


def _conv_bwd1(u2, dmix, ln_g, ln_b, pw, name):
    t = u2.shape[0]
    tm, w = _tm(), CONV_W
    cb = _ret_w() // w

    def body(u2_ref, dy_ref, g_ref, beta_ref, pw_ref, du2_ref, dg_ref, db_ref, dpw_ref):
        i = pl.program_id(0)

        @pl.when(i == 0)
        def _():
            dg_ref[...] = jnp.zeros_like(dg_ref)
            db_ref[...] = jnp.zeros_like(db_ref)
            dpw_ref[...] = jnp.zeros_like(dpw_ref)

        _, vjp = jax.vjp(_conv_post, u2_ref[...], g_ref[...], beta_ref[...], pw_ref[...])
        du2, dg, db, dpw = vjp(dy_ref[...])
        du2_ref[...] = du2
        dg_ref[...] += dg
        db_ref[...] += db
        dpw_ref[...] += dpw

    vec = pl.BlockSpec((1, w), lambda i: (0, 0))
    row = pl.BlockSpec((tm, w), lambda i: (i, 0))
    mat = pl.BlockSpec((w, w), lambda i: (0, 0))
    return pl.pallas_call(
        body, name=name, grid=(t // tm,),
        in_specs=[row, pl.BlockSpec((tm, w), lambda i: (i, cb)), vec, vec, mat],
        out_specs=[row, vec, vec, mat],
        out_shape=[jax.ShapeDtypeStruct((t, w), F32), jax.ShapeDtypeStruct((1, w), F32),
                   jax.ShapeDtypeStruct((1, w), F32), jax.ShapeDtypeStruct((w, w), F32)],
        compiler_params=_params(("arbitrary",)))(u2, dmix, ln_g, ln_b, pw)


def _conv_bwd2(du2, p, dw_w, name):
    t = p.shape[0]
    tm, w, k = _tm(), CONV_W, CONV_K
    h = _halo(k)
    pad = k // 2
    off = _offsets()
    ab, bb = off["a"] // w, off["b"] // w

    def body(dp, dc, dn, ap, ac, an, bp, bc, bn, w_ref, da_ref, db_ref, dw_ref, dbias_ref, ext_d, ext_u):
        i = pl.program_id(0)

        @pl.when(i == 0)
        def _():
            dw_ref[...] = jnp.zeros_like(dw_ref)
            dbias_ref[...] = jnp.zeros_like(dbias_ref)

        glu = lambda a, b: a * _sigmoid(b)
        a, b, d = ac[...], bc[...], dc[...]
        _fill_ext(ext_d, dp[...], d, dn[...], i, h, tm)
        _fill_ext(ext_u, glu(ap[...], bp[...]), glu(a, b), glu(an[...], bn[...]), i, h, tm)
        du = _corr(ext_d, w_ref, k, h, tm, True)
        sg = _sigmoid(b)
        da_ref[...] = du * sg
        db_ref[...] = du * a * sg * (1.0 - sg)
        dbias_ref[...] += jnp.sum(d, axis=0, keepdims=True)
        for kk in range(k):
            o = h + kk - pad
            dw_ref[kk:kk + 1, :] += jnp.sum(d * ext_u[o:o + tm, :], axis=0, keepdims=True)

    vec = pl.BlockSpec((1, w), lambda i: (0, 0))
    row = pl.BlockSpec((tm, w), lambda i: (i, 0))
    kw = pl.BlockSpec((k, w), lambda i: (0, 0))
    return pl.pallas_call(
        body, name=name, grid=(t // tm,),
        in_specs=_halo_specs(w, lambda: 0, h, tm) + _halo_specs(w, lambda: ab, h, tm)
        + _halo_specs(w, lambda: bb, h, tm) + [kw],
        out_specs=[row, row, kw, vec],
        out_shape=[jax.ShapeDtypeStruct((t, w), F32), jax.ShapeDtypeStruct((t, w), F32),
                   jax.ShapeDtypeStruct((k, w), F32), jax.ShapeDtypeStruct((1, w), F32)],
        scratch_shapes=[pltpu.VMEM((tm + 2 * h, w), F32), pltpu.VMEM((tm + 2 * h, w), F32)],
        compiler_params=_params(("arbitrary",)))(du2, du2, du2, p, p, p, p, p, p, dw_w)


def _ffn_tc():
    return _tile(D_FF, 512, LANES)


def _ffn_act_fwd(u, dw_w, dw_b, name):
    t = u.shape[0]
    tm, k, tc = _tm(), FFN_K, _ffn_tc()
    h = _halo(k)
    nj = D_FF // tc

    def body(vp, vc, vn, gp, gc, gn, wv, wg, bv, bg, out_ref, ext_v, ext_g):
        i = pl.program_id(0)
        _fill_ext(ext_v, vp[...], vc[...], vn[...], i, h, tm)
        _fill_ext(ext_g, gp[...], gc[...], gn[...], i, h, tm)
        for r0 in range(0, tm, ROW_CHUNK):
            val = _corr(ext_v, wv, k, h + r0, ROW_CHUNK, False) + bv[...]
            gate = _corr(ext_g, wg, k, h + r0, ROW_CHUNK, False) + bg[...]
            out_ref[r0:r0 + ROW_CHUNK, :] = (_silu(gate) * val).astype(BF16)

    wspec = lambda s: pl.BlockSpec((k, tc), lambda i, j: (0, j + s))
    bspec = lambda s: pl.BlockSpec((1, tc), lambda i, j: (0, j + s))
    return pl.pallas_call(
        body, name=name, grid=(t // tm, nj),
        in_specs=_halo_specs(tc, lambda j: j, h, tm) + _halo_specs(tc, lambda j: j + nj, h, tm)
        + [wspec(0), wspec(nj), bspec(0), bspec(nj)],
        out_specs=pl.BlockSpec((tm, tc), lambda i, j: (i, j)),
        out_shape=jax.ShapeDtypeStruct((t, D_FF), BF16),
        scratch_shapes=[pltpu.VMEM((tm + 2 * h, tc), F32), pltpu.VMEM((tm + 2 * h, tc), F32)],
        compiler_params=_params(("parallel", "parallel")))(u, u, u, u, u, u, dw_w, dw_w, dw_b, dw_b)


def _ffn_act_bwd1(u, da, dw_w, dw_b, name):
    t = u.shape[0]
    tm, k, tc = _tm(), FFN_K, _ffn_tc()
    h = _halo(k)
    nj = D_FF // tc

    def body(vp, vc, vn, gp, gc, gn, wv, wg, bv, bg, da_ref, dv_ref, dg_ref, ext_v, ext_g):
        i = pl.program_id(0)
        _fill_ext(ext_v, vp[...], vc[...], vn[...], i, h, tm)
        _fill_ext(ext_g, gp[...], gc[...], gn[...], i, h, tm)
        for r0 in range(0, tm, ROW_CHUNK):
            rows = slice(r0, r0 + ROW_CHUNK)
            val = _corr(ext_v, wv, k, h + r0, ROW_CHUNK, False) + bv[...]
            gate = _corr(ext_g, wg, k, h + r0, ROW_CHUNK, False) + bg[...]
            _, vjp = jax.vjp(lambda a, b: _silu(b) * a, val, gate)
            dval, dgate = vjp(da_ref[rows, :])
            dv_ref[rows, :] = dval
            dg_ref[rows, :] = dgate

    wspec = lambda s: pl.BlockSpec((k, tc), lambda i, j: (0, j + s))
    bspec = lambda s: pl.BlockSpec((1, tc), lambda i, j: (0, j + s))
    dc = pl.pallas_call(
        body, name=name, grid=(t // tm, nj),
        in_specs=_halo_specs(tc, lambda j: j, h, tm) + _halo_specs(tc, lambda j: j + nj, h, tm)
        + [wspec(0), wspec(nj), bspec(0), bspec(nj), pl.BlockSpec((tm, tc), lambda i, j: (i, j))],
        out_specs=[pl.BlockSpec((tm, tc), lambda i, j: (i, j)), pl.BlockSpec((tm, tc), lambda i, j: (i, j))],
        out_shape=[jax.ShapeDtypeStruct((t, D_FF), F32), jax.ShapeDtypeStruct((t, D_FF), F32)],
        scratch_shapes=[pltpu.VMEM((tm + 2 * h, tc), F32), pltpu.VMEM((tm + 2 * h, tc), F32)],
        compiler_params=_params(("parallel", "parallel")))(u, u, u, u, u, u, dw_w, dw_w, dw_b, dw_b, da)
    return dc


def _dwconv_bwd(dc, u, dw_w, colblock, name):
    t = u.shape[0]
    tm, k, tc = _tm(), FFN_K, _ffn_tc()
    h = _halo(k)
    pad = k // 2
    nj = D_FF // tc

    def body(dp, dcur, dn, up, uc, un, w_ref, du_ref, dw_ref, dbias_ref, ext_d, ext_u):
        i = pl.program_id(1)

        @pl.when(i == 0)
        def _():
            dw_ref[...] = jnp.zeros_like(dw_ref)
            dbias_ref[...] = jnp.zeros_like(dbias_ref)

        _fill_ext(ext_d, dp[...], dcur[...], dn[...], i, h, tm)
        _fill_ext(ext_u, up[...], uc[...], un[...], i, h, tm)
        acc_b = jnp.zeros((ROW_CHUNK, tc), F32)
        acc_w = [jnp.zeros((ROW_CHUNK, tc), F32) for _ in range(k)]
        for r0 in range(0, tm, ROW_CHUNK):
            d = ext_d[h + r0:h + r0 + ROW_CHUNK, :]
            du_ref[r0:r0 + ROW_CHUNK, :] = _corr(ext_d, w_ref, k, h + r0, ROW_CHUNK, True).astype(BF16)
            acc_b = acc_b + d
            for kk in range(k):
                o = h + r0 + kk - pad
                acc_w[kk] = acc_w[kk] + d * ext_u[o:o + ROW_CHUNK, :]
        dbias_ref[...] += jnp.sum(acc_b, axis=0, keepdims=True)
        for kk in range(k):
            dw_ref[kk:kk + 1, :] += jnp.sum(acc_w[kk], axis=0, keepdims=True)

    def hs(cb):
        r = tm // h
        return [pl.BlockSpec((h, tc), lambda j, i: (jnp.maximum(i * r - 1, 0), cb(j))),
                pl.BlockSpec((tm, tc), lambda j, i: (i, cb(j))),
                pl.BlockSpec((h, tc), lambda j, i: (jnp.minimum((i + 1) * r, (_t_rows() // h) - 1), cb(j)))]

    return pl.pallas_call(
        body, name=name, grid=(nj, t // tm),
        in_specs=hs(lambda j: j) + hs(lambda j: j + colblock) + [pl.BlockSpec((k, tc), lambda j, i: (0, j + colblock))],
        out_specs=[pl.BlockSpec((tm, tc), lambda j, i: (i, j)), pl.BlockSpec((k, tc), lambda j, i: (0, j)),
                   pl.BlockSpec((1, tc), lambda j, i: (0, j))],
        out_shape=[jax.ShapeDtypeStruct((t, D_FF), BF16), jax.ShapeDtypeStruct((k, D_FF), F32),
                   jax.ShapeDtypeStruct((1, D_FF), F32)],
        scratch_shapes=[pltpu.VMEM((tm + 2 * h, tc), F32), pltpu.VMEM((tm + 2 * h, tc), F32)],
        compiler_params=_params(("parallel", "arbitrary")))(dc, dc, dc, u, u, u, dw_w)


def _na_geometry(rq):
    ncb = CTX_LEN // GRID_W
    rows_n = SEQ // GRID_W
    r = jnp.maximum(rq - ncb, 0)
    kstart = jnp.clip(r - NA_ROWS // 2, 0, rows_n - NA_ROWS)
    base = kstart - r + NA_ROWS - 1
    return rq >= ncb, kstart, base


def _na_core(q, kl, vl, kc, vc, bias, mask):
    qs = q * (NA_DH ** -0.5)
    s_l = jnp.where(mask, dot_nt(qs, kl) + bias, NEG)
    s_c = dot_nt(qs, kc)
    m = lax.stop_gradient(jnp.maximum(jnp.max(s_l, axis=1, keepdims=True), jnp.max(s_c, axis=1, keepdims=True)))
    e_l, e_c = jnp.exp(s_l - m), jnp.exp(s_c - m)
    inv = 1.0 / (jnp.sum(e_l, axis=1, keepdims=True) + jnp.sum(e_c, axis=1, keepdims=True))
    return dot_nn(e_l * inv, vl) + dot_nn(e_c * inv, vc)


def _na_mask(is_lat):
    nl = NA_ROWS * GRID_W
    q = lax.broadcasted_iota(jnp.int32, (GRID_W, nl), 0)
    w = lax.broadcasted_iota(jnp.int32, (GRID_W, nl), 1) % GRID_W
    cs = jnp.clip(q - NA_COLS // 2, 0, GRID_W - NA_COLS)
    return (w >= cs) & (w < cs + NA_COLS) & is_lat


def _na_bias(rb_ref):
    assert 2 * GRID_W == LANES
    lane = lax.broadcasted_iota(jnp.int32, (GRID_W, LANES), 1)
    tiles = []
    for kp in range(NA_ROWS // 2):
        ev = jnp.broadcast_to(rb_ref[2 * kp:2 * kp + 1, :], (GRID_W, LANES))
        od = jnp.broadcast_to(rb_ref[2 * kp + 1:2 * kp + 2, :], (GRID_W, LANES))
        ev = pltpu.roll(ev, LANES - (NA_COLS - 1), 1, stride=1, stride_axis=0)
        od = pltpu.roll(od, LANES - (NA_COLS - 1) - GRID_W, 1, stride=1, stride_axis=0)
        tiles.append(jnp.where(lane < GRID_W, ev, od))
    return jnp.concatenate(tiles, axis=1)


def _na_dbias(dbias, drb_ref):
    qi = lax.broadcasted_iota(jnp.int32, (GRID_W, GRID_W), 0)
    qj = lax.broadcasted_iota(jnp.int32, (GRID_W, GRID_W), 1)
    flip = (qi + qj == GRID_W - 1).astype(F32)
    rev = lax.dot_general(flip, dbias, (((1,), (0,)), ((), ())), precision=lax.Precision.HIGHEST,
                          preferred_element_type=F32)
    lane = lax.broadcasted_iota(jnp.int32, (GRID_W, LANES), 1)
    s_ev = LANES - (GRID_W - NA_COLS)
    for kp in range(NA_ROWS // 2):
        tile = rev[:, kp * LANES:(kp + 1) * LANES]
        ev = pltpu.roll(jnp.where(lane < GRID_W, tile, 0.0), s_ev, 1, stride=1, stride_axis=0)
        od = pltpu.roll(jnp.where(lane >= GRID_W, tile, 0.0), s_ev - GRID_W, 1, stride=1, stride_axis=0)
        drb_ref[2 * kp:2 * kp + 1, :] += jnp.sum(ev, axis=0, keepdims=True)
        drb_ref[2 * kp + 1:2 * kp + 2, :] += jnp.sum(od, axis=0, keepdims=True)


def _na_specs(p_offsets):
    dh = NA_DH
    t = _t_rows()
    qb, kb, vb = (p_offsets[n] // dh for n in ("nq", "nk", "nv"))
    return [pl.BlockSpec((GRID_W, dh), lambda h, r: (r, qb + h)),
            pl.BlockSpec((t, dh), lambda h, r: (0, kb + h)),
            pl.BlockSpec((t, dh), lambda h, r: (0, vb + h)),
            pl.BlockSpec((None, None, NA_ROWS, LANES), lambda h, r: (h, _na_geometry(r)[2], 0, 0))]


def _na_fwd(p, rb, name, comm=None):
    t = p.shape[0]
    dh, nl = NA_DH, NA_ROWS * GRID_W

    def body(q_ref, k_ref, v_ref, rb_ref, out_ref):
        rq = pl.program_id(1)
        is_lat, kstart, _ = _na_geometry(rq)
        start = pl.multiple_of(CTX_LEN + kstart * GRID_W, GRID_W)
        out = _na_core(q_ref[...], k_ref[pl.ds(start, nl), :], v_ref[pl.ds(start, nl), :],
                       k_ref[0:CTX_LEN, :], v_ref[0:CTX_LEN, :], _na_bias(rb_ref), _na_mask(is_lat))
        out_ref[...] = out.astype(BF16)

    res, cres = _call(
        body, name=name, grid=(NA_HEADS, t // GRID_W), in_specs=_na_specs(_offsets()),
        out_specs=[pl.BlockSpec((GRID_W, dh), lambda h, r: (r, h))],
        out_shape=[jax.ShapeDtypeStruct((t, _na_w()), BF16)],
        sem=("parallel", "arbitrary"), args=(p, p, p, rb), comm=comm)
    return res[0], cres


def _na_bwd(p, rb, dmix, name, comm=None):
    t = p.shape[0]
    dh, nl = NA_DH, NA_ROWS * GRID_W
    ob = (_ret_w() + CONV_W) // dh

    def body(q_ref, k_ref, v_ref, rb_ref, dy_ref, dq_ref, dk_ref, dv_ref, drb_ref):
        rq = pl.program_id(1)
        is_lat, kstart, base = _na_geometry(rq)
        _, _, prev_base = _na_geometry(rq - 1)
        start = pl.multiple_of(CTX_LEN + kstart * GRID_W, GRID_W)

        @pl.when(rq == 0)
        def _():
            dk_ref[...] = jnp.zeros_like(dk_ref)
            dv_ref[...] = jnp.zeros_like(dv_ref)

        @pl.when((rq == 0) | (base != prev_base))
        def _():
            drb_ref[...] = jnp.zeros_like(drb_ref)

        mask = _na_mask(is_lat)
        _, vjp = jax.vjp(lambda *a: _na_core(*a, mask), q_ref[...], k_ref[pl.ds(start, nl), :],
                         v_ref[pl.ds(start, nl), :], k_ref[0:CTX_LEN, :], v_ref[0:CTX_LEN, :], _na_bias(rb_ref))
        dq, dkl, dvl, dkc, dvc, dbias = vjp(dy_ref[...])
        dq_ref[...] = dq
        dk_ref[pl.ds(start, nl), :] += dkl
        dv_ref[pl.ds(start, nl), :] += dvl
        dk_ref[0:CTX_LEN, :] += dkc
        dv_ref[0:CTX_LEN, :] += dvc
        _na_dbias(dbias, drb_ref)

    return _call(
        body, name=name, grid=(NA_HEADS, t // GRID_W),
        in_specs=_na_specs(_offsets()) + [pl.BlockSpec((GRID_W, dh), lambda h, r: (r, ob + h))],
        out_specs=[pl.BlockSpec((GRID_W, dh), lambda h, r: (r, h)), pl.BlockSpec((t, dh), lambda h, r: (0, h)),
                   pl.BlockSpec((t, dh), lambda h, r: (0, h)),
                   pl.BlockSpec((None, None, NA_ROWS, LANES), lambda h, r: (h, _na_geometry(r)[2], 0, 0))],
        out_shape=[jax.ShapeDtypeStruct((t, _na_w()), F32), jax.ShapeDtypeStruct((t, _na_w()), F32),
                   jax.ShapeDtypeStruct((t, _na_w()), F32),
                   jax.ShapeDtypeStruct((NA_HEADS, NA_ROWS, NA_ROWS, LANES), F32)],
        sem=("parallel", "arbitrary"), args=(p, p, p, rb, dmix), comm=comm)


def _rpb_rows(rpb):
    nr, nc = 2 * NA_ROWS - 1, 2 * NA_COLS - 1
    pad = jnp.pad(rpb, ((0, 0), (0, 0), (0, LANES - nc)))
    return jnp.stack([pad[:, b:b + NA_ROWS] for b in range(NA_ROWS)], axis=1)


def _rpb_rows_t(drb):
    nr, nc = 2 * NA_ROWS - 1, 2 * NA_COLS - 1
    out = jnp.zeros((NA_HEADS, nr, LANES), F32)
    for b in range(NA_ROWS):
        out = out.at[:, b:b + NA_ROWS].add(drb[:, b])
    return out[:, :, :nc]


def _assemble_dp(dqr, dkr, dvr, dgate, da, db, dnq, dnk, dnv, name):
    t = dgate.shape[0]
    tm = _tm()
    off = _offsets()
    sizes = dict(q=_ret_qk_w(), k=_ret_qk_w(), v=_ret_w(), g=_ret_w(), a=CONV_W, b=CONV_W, nq=_na_w(), nk=_na_w(), nv=_na_w())

    def body(q_ref, k_ref, v_ref, g_ref, a_ref, b_ref, nq_ref, nk_ref, nv_ref, o_ref):
        def put(n, val):
            o_ref[:, off[n]:off[n] + sizes[n]] = val.astype(BF16)

        put("q", q_ref[0] + q_ref[1])
        put("k", k_ref[0] + k_ref[1])
        put("v", v_ref[0] + v_ref[1])
        put("g", g_ref[...])
        put("a", a_ref[...])
        put("b", b_ref[...])
        put("nq", nq_ref[...])
        put("nk", nk_ref[...])
        put("nv", nv_ref[...])

    two = lambda w: pl.BlockSpec((2, tm, w), lambda i: (0, i, 0))
    one = lambda w: pl.BlockSpec((tm, w), lambda i: (i, 0))
    return pl.pallas_call(
        body, name=name, grid=(t // tm,),
        in_specs=[two(sizes["q"]), two(sizes["k"]), two(sizes["v"]), one(sizes["g"]), one(CONV_W), one(CONV_W),
                  one(_na_w()), one(_na_w()), one(_na_w())],
        out_specs=one(_d_in()), out_shape=jax.ShapeDtypeStruct((t, _d_in()), BF16),
        compiler_params=_params(("parallel",)))(dqr, dkr, dvr, dgate, da, db, dnq, dnk, dnv)


def _adamw(w, m, v, gs, name, comm=None):
    nl, r, c = w.shape
    stacked = not isinstance(gs, (list, tuple))
    if stacked:
        gs = [gs]
    assert stacked or len(gs) == nl
    g_n = gs[0].shape[-3]
    block_bytes = 2 * 1024 * 1024
    rows = min(block_bytes // (4 * c), block_bytes // (g_n * c * gs[0].dtype.itemsize))
    tr = _tile(r, max(2 * SUBLANES, rows // (2 * SUBLANES) * (2 * SUBLANES)), 2 * SUBLANES)
    nt = r // tr
    c1 = 1.0 - ADAM_B1 ** ADAM_STEP
    c2 = 1.0 - ADAM_B2 ** ADAM_STEP

    def body(w_ref, m_ref, v_ref, *rest):
        g_refs, (go_ref, d_ref, mo_ref, vo_ref) = rest[:len(gs)], rest[len(gs):]
        layer = pl.program_id(0)
        for ll in range(len(gs)):
            @pl.when(jnp.logical_or(stacked, layer == ll))
            def _():
                g_ref = g_refs[ll]
                g = g_ref[0].astype(F32)
                for j in range(1, g_n):
                    g = g + g_ref[j].astype(F32)
                mn = ADAM_B1 * m_ref[...] + (1.0 - ADAM_B1) * g
                vn = ADAM_B2 * v_ref[...] + (1.0 - ADAM_B2) * (g * g)
                m_hat = mn / c1
                v_hat = vn / c2
                go_ref[...] = g
                d_ref[...] = -ADAM_LR * (m_hat / (jnp.sqrt(v_hat) + ADAM_EPS) + ADAM_WD * w_ref[...])
                mo_ref[...] = mn
                vo_ref[...] = vn

    def g_spec(ll):
        if stacked:
            return pl.BlockSpec((None, g_n, tr, c), lambda l, i: (l, 0, i, 0))
        return pl.BlockSpec((g_n, tr, c), lambda l, i: (0, jnp.where(l == ll, i, jnp.where(l < ll, 0, nt - 1)), 0))

    blk = pl.BlockSpec((None, tr, c), lambda l, i: (l, i, 0))
    sds = jax.ShapeDtypeStruct((nl, r, c), F32)
    return _call(
        body, name=name, grid=(nl, nt),
        in_specs=[blk, blk, blk] + [g_spec(ll) for ll in range(len(gs))],
        out_specs=[blk, blk, blk, blk], out_shape=[sds, sds, sds, sds],
        sem=("arbitrary", "arbitrary"), args=(w, m, v, *gs), comm=comm)


def _sum_devices(g, name):
    _, r, c = g.shape
    tr = _tile(r, 512, SUBLANES)

    def body(g_ref, o_ref):
        acc = g_ref[0]
        for j in range(1, N_DEV):
            acc = acc + g_ref[j]
        o_ref[...] = acc

    return pl.pallas_call(body, name=name, grid=(r // tr,), in_specs=[pl.BlockSpec((N_DEV, tr, c), lambda i: (0, i, 0))],
                          out_specs=pl.BlockSpec((tr, c), lambda i: (i, 0)), out_shape=jax.ShapeDtypeStruct((r, c), F32),
                          compiler_params=_params(("parallel",)))(g)


def _ada_fwd(c16, w_ada, b_shard, name):
    nl, d, cs = w_ada.shape
    tk = _tile(d, 512, LANES)
    nk = d // tk

    def body(c_ref, w_ref, b_ref, o_ref):
        kk = pl.program_id(1)

        @pl.when(kk == 0)
        def _():
            o_ref[...] = jnp.broadcast_to(b_ref[...], o_ref.shape)

        o_ref[...] += _dg(_silu(c_ref[...]), w_ref[...], 1, 0)

    return pl.pallas_call(
        body, name=name, grid=(nl, nk),
        in_specs=[pl.BlockSpec((16, tk), lambda l, kk: (0, kk)), pl.BlockSpec((None, tk, cs), lambda l, kk: (l, kk, 0)),
                  pl.BlockSpec((None, 1, cs), lambda l, kk: (l, 0, 0))],
        out_specs=pl.BlockSpec((None, 16, cs), lambda l, kk: (l, 0, 0)),
        out_shape=jax.ShapeDtypeStruct((nl, 16, cs), F32),
        compiler_params=_params(("parallel", "arbitrary")))(c16, w_ada, b_shard)


def _ada_bwd(c16, dm16, w_ada, name):
    nl, d, cs = w_ada.shape
    td = _tile(d, 512, LANES)

    def body(c_ref, dm_ref, w_ref, gw_ref, dc_ref):
        cv = c_ref[...]
        s, vjp = jax.vjp(_silu, cv)
        gw_ref[...] = _dg(s, dm_ref[...], 0, 0)
        ds = _dg(dm_ref[...], w_ref[...], 1, 1)
        dc_ref[...] = vjp(ds)[0]

    return pl.pallas_call(
        body, name=name, grid=(nl, d // td),
        in_specs=[pl.BlockSpec((16, td), lambda l, i: (0, i)), pl.BlockSpec((None, 16, cs), lambda l, i: (l, 0, 0)),
                  pl.BlockSpec((None, td, cs), lambda l, i: (l, i, 0))],
        out_specs=[pl.BlockSpec((None, td, cs), lambda l, i: (l, i, 0)), pl.BlockSpec((None, 16, td), lambda l, i: (l, 0, i))],
        out_shape=[jax.ShapeDtypeStruct((nl, d, cs), F32), jax.ShapeDtypeStruct((nl, 16, d), F32)],
        compiler_params=_params(("parallel", "parallel")))(c16, dm16, w_ada)


def _pack(arrays, row_align):
    flat = jnp.concatenate([a.reshape(-1).astype(F32) for a in arrays])
    n = flat.shape[0]
    per = LANES * row_align
    padded = ((n + per - 1) // per) * per
    return jnp.pad(flat, (0, padded - n)).reshape(padded // LANES, LANES)


def _unpack(packed, shapes):
    flat = packed.reshape(-1)
    out, o = [], 0
    for s in shapes:
        n = int(np.prod(s))
        out.append(flat[o:o + n].reshape(s))
        o += n
    return out


def _rope_tables():
    half, nf = RET_DK // 2, RET_DK // 4
    pos = jnp.arange(SEQ)
    row = (pos // GRID_W).astype(F32)
    col = (pos % GRID_W).astype(F32)
    inv = ROPE_BASE ** (-jnp.arange(nf, dtype=F32) / nf)
    ar, ac = row[:, None] * inv[None, :], col[:, None] * inv[None, :]
    cos = jnp.concatenate([jnp.cos(ar), jnp.cos(ar), jnp.cos(ac), jnp.cos(ac)], axis=-1)
    sin = jnp.concatenate([-jnp.sin(ar), jnp.sin(ar), -jnp.sin(ac), jnp.sin(ac)], axis=-1)
    cos = jnp.concatenate([jnp.ones((CTX_LEN, RET_DK), F32), cos], axis=0)
    sin = jnp.concatenate([jnp.zeros((CTX_LEN, RET_DK), F32), sin], axis=0)
    return cos, sin


def _layer_fwd(l, nl, x, mod4, w, cst, shards, full):
    n = lambda s: f"l{l}_{s}"
    d = D_MODEL

    def gather(keys):
        return _Gather([shards[k] for k in keys]) if keys else None

    w["w_in"] = _cols_from_shards(full[("w_in", l)], n("w_in_cols"))
    h1 = _normmod_fwd(x, w["norm1_g"], mod4, 0, n("norm1"))
    keys = [("w_out", l), ("ffn_down", l)] if l == 0 else [("ffn_up", l)]
    p, got = _mm(h1, w["w_in"], n("proj_in"), comm=gather(keys))
    full.update(zip(keys, got))
    o2, states = _ret_fwd(p, cst["cos"], cst["sin"], w["ret_decay"], cst["order"], n("ret_fwd"))
    ret_out = _ggn_fwd(o2, p, w["ret_gn_g"], n("ret_gn"))
    u2, conv_out = _conv_fwd(p, w["conv_dw_w"], w["conv_dw_b"], w["conv_ln_g"], w["conv_ln_b"], w["conv_pw"], n("conv_fwd"))
    keys = [("ffn_up", l)] if l == 0 else [("ffn_down", l)]
    na_out, got = _na_fwd(p, w["rb"], n("na_fwd"), comm=gather(keys))
    full.update(zip(keys, got))
    mix = jnp.concatenate([ret_out, conv_out, na_out], axis=1)
    w["w_out"] = full[("w_out", l)].reshape(_d_mix(), d)
    g1, _ = _mm(mix, w["w_out"], n("proj_out"))
    x1 = _gate_res_fwd(x, g1, mod4, 2, n("res1"))
    h2 = _normmod_fwd(x1, w["norm2_g"], mod4, 1, n("norm2"))
    w["ffn_up"] = full[("ffn_up", l)]
    keys = [("w_in", l + 1), ("w_out", l + 1)] if l + 1 < nl else []
    u, got = _mm(h2, w["ffn_up"], n("ffn_up"), b3=True, comm=gather(keys))
    full.update(zip(keys, got or []))
    a = _ffn_act_fwd(u, w["ffn_dw_w"], w["ffn_dw_b"], n("ffn_act"))
    w["ffn_down"] = full[("ffn_down", l)].reshape(D_FF, d)
    f, _ = _mm(a, w["ffn_down"], n("ffn_down"))
    x2 = _gate_res_fwd(x1, f, mod4, 5, n("res2"))
    saved = dict(x=x, h1=h1, p=p, o2=o2, states=states, u2=u2, mix=mix, g1=g1, x1=x1, h2=h2, u=u, a=a, f=f)
    return x2, saved


def _layer_bwd(l, dx2, s, mod4, w, cst, pending):
    n = lambda t: f"l{l}_{t}"
    d = D_MODEL
    nj = D_FF // _ffn_tc()
    done = {}

    def exchange(named):
        return _Exchange([a for _, a in named]) if named else None

    def collect(named, got):
        done.update({k: g for (k, _), g in zip(named, got or [])})

    dfg, dg2 = _gate_res_bwd(dx2, s["f"], mod4, 5, n("res2_bwd"))
    da, got = _mm(dfg, w["ffn_down"], n("ffn_down_dx"), tb=True, comm=exchange(pending[:1]))
    collect(pending[:1], got)
    d_ffn_down, got = _mm(_transpose_bf16(s["a"], n("act_t")), dfg, n("ffn_down_dw"), out_dtype=BF16, tm_max=DW_TM,
                          comm=exchange(pending[1:]))
    collect(pending[1:], got)
    dcv, dcg = _ffn_act_bwd1(s["u"], da, w["ffn_dw_w"], w["ffn_dw_b"], n("ffn_act_bwd"))
    duv, dwv, dbv = _dwconv_bwd(dcv, s["u"], w["ffn_dw_w"], 0, n("ffn_dw_bwd_val"))
    dug, dwg, dbg = _dwconv_bwd(dcg, s["u"], w["ffn_dw_w"], nj, n("ffn_dw_bwd_gate"))
    du = jnp.concatenate([duv, dug], axis=1)
    d_ffn_dw_w = jnp.concatenate([dwv, dwg], axis=1)
    d_ffn_dw_b = jnp.concatenate([dbv, dbg], axis=1)[0]
    named = [(("ffn_down", l), d_ffn_down.reshape(N_DEV, D_FF // N_DEV, d))]
    dh2, got = _mm(du, w["ffn_up"], n("ffn_up_dx"), tb=True, b3=True, comm=exchange(named))
    collect(named, got)
    d_ffn_up, _ = _mm(_transpose_bf16(s["h2"], n("h2_t")), du, n("ffn_up_dw"), out_dtype=BF16, tm_max=DW_TM,
                      o_cs=2 * D_FF // N_DEV)
    dx1, dn2, dsh2, dsc2 = _normmod_bwd(s["x1"], w["norm2_g"], mod4, 1, dh2, dx2, n("norm2_bwd"))
    dgg, dg1 = _gate_res_bwd(dx1, s["g1"], mod4, 2, n("res1_bwd"))
    dmix, _ = _mm(dgg, w["w_out"], n("proj_out_dx"), tb=True)
    d_w_out, _ = _mm(_transpose_bf16(s["mix"], n("mix_t")), dgg, n("proj_out_dw"), out_dtype=BF16, tm_max=DW_TM)
    do, dgate, dgn = _ggn_bwd(s["o2"], s["p"], w["ret_gn_g"], dmix, n("ret_gn_bwd"))
    dqr, dkr, dvr, ddec = _ret_bwd(s["p"], cst["cos"], cst["sin"], w["ret_decay"], cst["order"], s["states"], do, n("ret_bwd"))
    du2, dlng, dlnb, dpw = _conv_bwd1(s["u2"], dmix, w["conv_ln_g"], w["conv_ln_b"], w["conv_pw"], n("conv_bwd1"))
    dca, dcb, ddww, ddwb = _conv_bwd2(du2, s["p"], w["conv_dw_w"], n("conv_bwd2"))
    named = [(("ffn_up", l), d_ffn_up)]
    (dnq, dnk, dnv, drb), got = _na_bwd(s["p"], w["rb"], dmix, n("na_bwd"), comm=exchange(named))
    collect(named, got)
    dp = _assemble_dp(dqr, dkr, dvr, dgate, dca, dcb, dnq, dnk, dnv, n("dproj"))
    named = [(("w_out", l), d_w_out.reshape(N_DEV, _d_mix() // N_DEV, d))]
    dh1, got = _mm(dp, w["w_in"], n("proj_in_dx"), tb=True, comm=exchange(named))
    collect(named, got)
    d_w_in, _ = _mm(_transpose_bf16(s["h1"], n("h1_t")), dp, n("proj_in_dw"), out_dtype=BF16, tm_max=DW_TM,
                    o_cs=_d_in() // N_DEV)
    dx, dn1, dsh1, dsc1 = _normmod_bwd(s["x"], w["norm1_g"], mod4, 0, dh1, dx1, n("norm1_bwd"))
    dmod = jnp.concatenate([dsh1, dsc1, dg1, dsh2, dsc2, dg2], axis=1)
    small = dict(norm1_g=dn1[0], ret_decay=ddec[:, :, 0, 0], ret_gn_g=dgn[0], conv_dw_w=ddww, conv_dw_b=ddwb[0],
                 conv_ln_g=dlng[0], conv_ln_b=dlnb[0], conv_pw=dpw, na_rpb=_rpb_rows_t(drb), norm2_g=dn2[0],
                 ffn_dw_w=d_ffn_dw_w, ffn_dw_b=d_ffn_dw_b)
    half = d // 2
    return dx, dmod, done, small, [(("w_in", l, 0), d_w_in[:, :half]), (("w_in", l, 1), d_w_in[:, half:])]


def _d_mix():
    return _ret_w() + CONV_W + _na_w()


_SMALL = ["c_ctx", "b_ada", "norm1_g", "ret_decay", "ret_gn_g", "conv_dw_w", "conv_dw_b", "conv_ln_g", "conv_ln_b",
          "conv_pw", "na_rpb", "norm2_g", "ffn_dw_w", "ffn_dw_b", "final_g"]
_SMALL_SHARD_AXIS = {"conv_dw_w": 2, "conv_pw": 1, "ffn_dw_w": 2}


def kernel(x, c, ctx, c_ctx, w_ada, b_ada, norm1_g, w_in, ret_decay, ret_gn_g, conv_dw_w, conv_dw_b, conv_ln_g, conv_ln_b, conv_pw, na_rpb, w_out, norm2_g, ffn_up, ffn_dw_w, ffn_dw_b, ffn_down, final_g, loss_target, m_c_ctx, m_w_ada, m_b_ada, m_norm1_g, m_w_in, m_ret_decay, m_ret_gn_g, m_conv_dw_w, m_conv_dw_b, m_conv_ln_g, m_conv_ln_b, m_conv_pw, m_na_rpb, m_w_out, m_norm2_g, m_ffn_up, m_ffn_dw_w, m_ffn_dw_b, m_ffn_down, m_final_g, v_c_ctx, v_w_ada, v_b_ada, v_norm1_g, v_w_in, v_ret_decay, v_ret_gn_g, v_conv_dw_w, v_conv_dw_b, v_conv_ln_g, v_conv_ln_b, v_conv_pw, v_na_rpb, v_w_out, v_norm2_g, v_ffn_up, v_ffn_dw_w, v_ffn_dw_b, v_ffn_down, v_final_g):
    d, nl = D_MODEL, DEPTH
    cs = 6 * d // N_DEV
    me = _my_index()
    weights = dict(c_ctx=c_ctx, w_ada=w_ada, b_ada=b_ada, norm1_g=norm1_g, w_in=w_in, ret_decay=ret_decay, ret_gn_g=ret_gn_g,
                   conv_dw_w=conv_dw_w, conv_dw_b=conv_dw_b, conv_ln_g=conv_ln_g, conv_ln_b=conv_ln_b, conv_pw=conv_pw,
                   na_rpb=na_rpb, w_out=w_out, norm2_g=norm2_g, ffn_up=ffn_up, ffn_dw_w=ffn_dw_w, ffn_dw_b=ffn_dw_b,
                   ffn_down=ffn_down, final_g=final_g)
    mom = dict(c_ctx=m_c_ctx, w_ada=m_w_ada, b_ada=m_b_ada, norm1_g=m_norm1_g, w_in=m_w_in, ret_decay=m_ret_decay,
               ret_gn_g=m_ret_gn_g, conv_dw_w=m_conv_dw_w, conv_dw_b=m_conv_dw_b, conv_ln_g=m_conv_ln_g,
               conv_ln_b=m_conv_ln_b, conv_pw=m_conv_pw, na_rpb=m_na_rpb, w_out=m_w_out, norm2_g=m_norm2_g,
               ffn_up=m_ffn_up, ffn_dw_w=m_ffn_dw_w, ffn_dw_b=m_ffn_dw_b, ffn_down=m_ffn_down, final_g=m_final_g)
    var = dict(c_ctx=v_c_ctx, w_ada=v_w_ada, b_ada=v_b_ada, norm1_g=v_norm1_g, w_in=v_w_in, ret_decay=v_ret_decay,
               ret_gn_g=v_ret_gn_g, conv_dw_w=v_conv_dw_w, conv_dw_b=v_conv_dw_b, conv_ln_g=v_conv_ln_g,
               conv_ln_b=v_conv_ln_b, conv_pw=v_conv_pw, na_rpb=v_na_rpb, w_out=v_w_out, norm2_g=v_norm2_g,
               ffn_up=v_ffn_up, ffn_dw_w=v_ffn_dw_w, ffn_dw_b=v_ffn_dw_b, ffn_down=v_ffn_down, final_g=v_final_g)

    big_names = ["w_in", "w_out", "ffn_up", "ffn_down"]
    shards = {(nm, l): _cast_bf16(weights[nm][l], f"cast_{nm}{l}") for l in range(nl) for nm in big_names}
    small_sharded = _pack([conv_dw_w, conv_pw, ffn_dw_w], SUBLANES)
    c_rows = jnp.pad(c, ((0, SUBLANES - 1), (0, 0)))
    gathered = _run_comm(_Gather([c_rows, small_sharded, shards[("w_in", 0)]]), "gather_first")
    c_all = gathered[0][:, 0, :]
    full = {("w_in", 0): gathered[2]}
    sm = [_unpack(gathered[1][j], [conv_dw_w.shape, conv_pw.shape, ffn_dw_w.shape]) for j in range(N_DEV)]
    full_conv_dw_w = jnp.concatenate([s[0] for s in sm], axis=2)
    full_conv_pw = jnp.concatenate([s[1] for s in sm], axis=1)
    full_ffn_dw_w = jnp.concatenate([s[2] for s in sm], axis=2)

    c16 = jnp.concatenate([c_all, jnp.broadcast_to(c_ctx[None, :], (N_DEV, d))], axis=0)
    b_shard = lax.dynamic_slice_in_dim(b_ada, me * cs, cs, axis=1)[:, None, :]
    m_shard = _ada_fwd(c16, w_ada, b_shard, "ada_fwd")
    m_all = _run_comm(_Gather([m_shard.reshape(nl * 16, cs)]), "gather_mod")[0]
    m_full = m_all.reshape(N_DEV, nl, 16, cs).transpose(1, 2, 0, 3).reshape(nl, 16, 6 * d)
    m_lat = lax.dynamic_index_in_dim(m_full, me, axis=1, keepdims=False)
    mod = jnp.stack([m_full[:, N_DEV], m_lat], axis=1).reshape(nl, 2, 6, 1, d)

    cos, sin = _rope_tables()
    cst = dict(cos=cos, sin=sin, order=_chunk_order())
    layer_w = []
    for l in range(nl):
        layer_w.append(dict(
            norm1_g=norm1_g[l][None], norm2_g=norm2_g[l][None], ret_decay=ret_decay[l], ret_gn_g=ret_gn_g[l][None],
            conv_dw_w=full_conv_dw_w[l], conv_dw_b=conv_dw_b[l][None], conv_ln_g=conv_ln_g[l][None],
            conv_ln_b=conv_ln_b[l][None], conv_pw=full_conv_pw[l], rb=_rpb_rows(na_rpb[l]),
            ffn_dw_w=full_ffn_dw_w[l], ffn_dw_b=ffn_dw_b[l][None]))

    xs = jnp.concatenate([ctx[0], x[0]], axis=0)
    saved = []
    for l in range(nl):
        xs, sv = _layer_fwd(l, nl, xs, mod[l], layer_w[l], cst, shards, full)
        saved.append(sv)
    loss_tile, dxs, dfinal = _loss_head(xs, final_g[None], loss_target[0], "loss_head")
    loss = lax.psum(loss_tile[0, 0], ("x", "y", "c"))

    dmods, smalls = [None] * nl, [None] * nl
    exchanged, pending = {}, []
    for l in reversed(range(nl)):
        dxs, dmods[l], done, smalls[l], pending = _layer_bwd(l, dxs, saved[l], mod[l], layer_w[l], cst, pending)
        exchanged.update(done)
    grad_x = dxs[CTX_LEN:][None]

    dm_mine = jnp.stack(dmods).reshape(nl * 2, 6 * d)
    dm_rows = jnp.pad(dm_mine, ((0, SUBLANES - nl * 2), (0, 0)))
    dm_all = _run_comm(_Gather([dm_rows]), "gather_dmod")[0][:, :nl * 2].reshape(N_DEV, nl, 2, 6 * d)
    dm16_full = jnp.concatenate([dm_all[:, :, 1].transpose(1, 0, 2), dm_all[:, :, 0].transpose(1, 0, 2)], axis=1)
    dm16 = lax.dynamic_slice_in_dim(dm16_full, me * cs, cs, axis=2)
    g_w_ada, dc16 = _ada_bwd(c16, dm16, w_ada, "ada_bwd")

    small_grads = dict(
        c_ctx=jnp.sum(dc16[:, N_DEV:], axis=(0, 1)),
        b_ada=jnp.sum(jnp.stack(dmods).reshape(nl, 2, 6 * d), axis=1),
        final_g=dfinal[0])
    for nm in _SMALL:
        if nm not in small_grads:
            small_grads[nm] = jnp.stack([smalls[l][nm] for l in range(nl)])
    shapes_full = [small_grads[nm].shape for nm in _SMALL]
    packed = _pack([small_grads[nm] for nm in _SMALL], 512)

    riders = {"w_ada": _Exchange([pending[0][1]]), "ffn_up": _Exchange([pending[1][1]]), "ffn_down": _Gather([packed])}
    out_big, rode = {}, {}
    out_big["w_ada"], rode["w_ada"] = _adamw(w_ada, m_w_ada, v_w_ada, g_w_ada[:, None], "adamw_w_ada", comm=riders["w_ada"])
    for nm in ["ffn_up", "ffn_down", "w_out"]:
        out_big[nm], rode[nm] = _adamw(weights[nm], mom[nm], var[nm], [exchanged[(nm, l)] for l in range(nl)],
                                       f"adamw_{nm}", comm=riders.get(nm))
    exchanged[pending[0][0]] = rode["w_ada"][0]
    exchanged[pending[1][0]] = rode["ffn_up"][0]
    g_w_in = [jnp.concatenate([exchanged[("w_in", l, 0)], exchanged[("w_in", l, 1)]], axis=1) for l in range(nl)]
    out_big["w_in"], _ = _adamw(w_in, m_w_in, v_w_in, g_w_in, "adamw_w_in")

    summed = _sum_devices(rode["ffn_down"][0], "sum_small_grads")
    g_small = dict(zip(_SMALL, _unpack(summed, shapes_full)))
    for nm, ax in _SMALL_SHARD_AXIS.items():
        n_sh = weights[nm].shape[ax]
        g_small[nm] = lax.dynamic_slice_in_dim(g_small[nm], me * n_sh, n_sh, axis=ax)
    shapes_own = [weights[nm].shape for nm in _SMALL]
    pk = lambda src: _pack([src[nm] for nm in _SMALL], 2 * SUBLANES)[None]
    res_small, _ = _adamw(pk(weights), pk(mom), pk(var), pk(g_small)[:, None], "adamw_small")
    out_small = [dict(zip(_SMALL, _unpack(r[0], shapes_own))) for r in res_small]

    names = ["c_ctx", "w_ada", "b_ada", "norm1_g", "w_in", "ret_decay", "ret_gn_g", "conv_dw_w", "conv_dw_b", "conv_ln_g",
             "conv_ln_b", "conv_pw", "na_rpb", "w_out", "norm2_g", "ffn_up", "ffn_dw_w", "ffn_dw_b", "ffn_down", "final_g"]
    outs = [loss, grad_x]
    for kind in range(4):
        for nm in names:
            outs.append(out_big[nm][kind] if nm in out_big else out_small[kind][nm])
    return tuple(outs)
```

```python
import functools
import math

import numpy as np
import jax
import jax.numpy as jnp
from jax import lax
from jax.experimental import pallas as pl
from jax.experimental.pallas import tpu as pltpu

D_MODEL = 2048
SEQ = 4096
DEPTH = 2
GRID_W = 64
CTX_LEN = 256
RET_HEADS = 4
RET_DK = 128
RET_DV = 256
RET_CHUNK = 128
CONV_W = 512
CONV_K = 31
NA_HEADS = 4
NA_DH = 128
NA_ROWS = 8
NA_COLS = 16
D_FF = 5632
FFN_K = 3
ROPE_BASE = 10000.0
EPS = 1e-6
ADAM_LR = 0.001
ADAM_B1 = 0.9
ADAM_B2 = 0.999
ADAM_EPS = 1e-08
ADAM_WD = 0.01
ADAM_STEP = 10
N_DEV = 8

LANES = 128
SUBLANES = 8
VMEM_LIMIT = 56 * 1024 * 1024
ROW_CHUNK = 16

F32 = jnp.float32
BF16 = jnp.bfloat16
MESH = pl.DeviceIdType.MESH
NEG = -1e30


def _ret_qk_w():
    return RET_HEADS * RET_DK


def _ret_w():
    return RET_HEADS * RET_DV


def _na_w():
    return NA_HEADS * NA_DH


def _d_in():
    return 2 * _ret_qk_w() + 2 * _ret_w() + 2 * CONV_W + 3 * _na_w()


def _offsets():
    sizes = [_ret_qk_w(), _ret_qk_w(), _ret_w(), _ret_w(), CONV_W, CONV_W, _na_w(), _na_w(), _na_w()]
    offs = [0]
    for s in sizes[:-1]:
        offs.append(offs[-1] + s)
    return dict(zip(["q", "k", "v", "g", "a", "b", "nq", "nk", "nv"], offs))


def _t_rows():
    return CTX_LEN + SEQ


def _tm():
    return CTX_LEN


def _params(sem=None):
    kw = dict(vmem_limit_bytes=VMEM_LIMIT)
    if sem is not None:
        kw["dimension_semantics"] = sem
    return pltpu.CompilerParams(**kw)


def _tile(n, pref, align):
    best = None
    for t in range(align, min(n, pref) + 1, align):
        if n % t == 0:
            best = t
    return best if best is not None else n


def _dg(a, b, ca, cb):
    return lax.dot_general(a.astype(BF16), b.astype(BF16), (((ca,), (cb,)), ((), ())), preferred_element_type=F32)


@jax.custom_vjp
def dot_nn(a, b):
    return _dg(a, b, 1, 0)


dot_nn.defvjp(lambda a, b: (_dg(a, b, 1, 0), (a, b)),
              lambda r, g: (_dg(g, r[1], 1, 1), _dg(r[0], g, 0, 0)))


@jax.custom_vjp
def dot_nt(a, b):
    return _dg(a, b, 1, 1)


dot_nt.defvjp(lambda a, b: (_dg(a, b, 1, 1), (a, b)),
              lambda r, g: (_dg(g, r[1], 1, 0), _dg(g, r[0], 0, 0)))


@jax.custom_vjp
def dot_tn(a, b):
    return _dg(a, b, 0, 0)


dot_tn.defvjp(lambda a, b: (_dg(a, b, 0, 0), (a, b)),
              lambda r, g: (_dg(r[1], g, 1, 1), _dg(r[0], g, 1, 0)))


def _sigmoid(x):
    return 1.0 / (1.0 + jnp.exp(-x))


def _silu(x):
    return x * _sigmoid(x)


def _my_pos():
    return lax.axis_index("x"), lax.axis_index("y"), lax.axis_index("c")


def _my_index():
    x, y, c = _my_pos()
    return 4 * x + 2 * y + c


_ANY = pl.BlockSpec(memory_space=pl.ANY)


class _Gather:
    def __init__(self, arrays):
        self.arrays = list(arrays)
        n = len(self.arrays)
        self.out_shape = [jax.ShapeDtypeStruct((N_DEV,) + a.shape, a.dtype) for a in self.arrays]
        self.scratch = [pltpu.SemaphoreType.DMA((n, 7)), pltpu.SemaphoreType.DMA((n, 7)), pltpu.SemaphoreType.DMA((n,))]

    def _plan(self, xs, outs, sems):
        send_sems, recv_sems, local_sems = sems
        n = len(self.arrays)
        x, y, c = _my_pos()
        me, sibling = (x, y, c), (x, y, 1 - c)
        chips = [(1 - x, y), (x, 1 - y), (1 - x, 1 - y)]

        def slot(a, p):
            return outs[a].at[4 * p[0] + 2 * p[1] + p[2]]

        def copy(a, k, block, to, src=None):
            return pltpu.make_async_remote_copy(
                src_ref=slot(a, block) if src is None else src, dst_ref=slot(a, block),
                send_sem=send_sems.at[a, k], recv_sem=recv_sems.at[a, k], device_id=to, device_id_type=MESH)

        mine = [pltpu.make_async_copy(xs[a], slot(a, me), local_sems.at[a]) for a in range(n)]
        first = []
        for a in range(n):
            first.append(copy(a, 0, me, sibling, src=xs[a]))
            first += [copy(a, 1 + j, me, (*chip, c), src=xs[a]) for j, chip in enumerate(chips)]
        return n, c, me, sibling, chips, copy, mine, first

    def start(self, xs, outs, sems):
        _, _, _, _, _, _, mine, first = self._plan(xs, outs, sems)
        for m in mine:
            m.start()
        for cp in first:
            cp.start()

    def finish(self, xs, outs, sems):
        n, c, me, sibling, chips, copy, mine, first = self._plan(xs, outs, sems)
        passed = []
        for a in range(n):
            for j, chip in enumerate(chips):
                copy(a, 1 + j, (*chip, c), me).wait_recv()
                p = copy(a, 4 + j, (*chip, c), sibling)
                p.start()
                passed.append(p)
        for a in range(n):
            copy(a, 0, sibling, me).wait_recv()
            for j, chip in enumerate(chips):
                copy(a, 4 + j, (*chip, 1 - c), me).wait_recv()
        for cp in first + passed:
            cp.wait_send()
        for m in mine:
            m.wait()


class _Exchange:
    def __init__(self, arrays):
        self.arrays = list(arrays)
        n = len(self.arrays)
        self.out_shape = [jax.ShapeDtypeStruct(a.shape, a.dtype) for a in self.arrays]
        self.scratch = [pltpu.SemaphoreType.DMA((n, 7)), pltpu.SemaphoreType.DMA((n, 7)), pltpu.SemaphoreType.DMA((n,))]

    def _plan(self, xs, outs, sems):
        send_sems, recv_sems, local_sems = sems
        x, y, c = _my_pos()
        me = 4 * x + 2 * y + c
        mine, sends, recvs = [], [], []
        for a in range(len(self.arrays)):
            mine.append(pltpu.make_async_copy(xs[a].at[me], outs[a].at[me], local_sems.at[a]))
        for k in range(1, N_DEV):
            px = 1 - x if (k >> 2) & 1 else x
            py = 1 - y if (k >> 1) & 1 else y
            pc = 1 - c if k & 1 else c
            peer = 4 * px + 2 * py + pc
            for a in range(len(self.arrays)):
                sends.append(pltpu.make_async_remote_copy(
                    src_ref=xs[a].at[peer], dst_ref=outs[a].at[me], send_sem=send_sems.at[a, k - 1],
                    recv_sem=recv_sems.at[a, k - 1], device_id=(px, py, pc), device_id_type=MESH))
                recvs.append(pltpu.make_async_remote_copy(
                    src_ref=xs[a].at[me], dst_ref=outs[a].at[peer], send_sem=send_sems.at[a, k - 1],
                    recv_sem=recv_sems.at[a, k - 1], device_id=(px, py, pc), device_id_type=MESH))
        return mine, sends, recvs

    def start(self, xs, outs, sems):
        mine, sends, _ = self._plan(xs, outs, sems)
        for m in mine:
            m.start()
        for s in sends:
            s.start()

    def finish(self, xs, outs, sems):
        mine, sends, recvs = self._plan(xs, outs, sems)
        for r in recvs:
            r.wait_recv()
        for s in sends:
            s.wait_send()
        for m in mine:
            m.wait()


def _run_comm(comm, name):
    n = len(comm.arrays)

    def body(*refs):
        xs, outs, sems = refs[:n], refs[n:2 * n], refs[2 * n:]
        comm.start(xs, outs, sems)
        comm.finish(xs, outs, sems)

    return pl.pallas_call(body, name=name, out_shape=comm.out_shape, in_specs=[_ANY] * n, out_specs=[_ANY] * n,
                          scratch_shapes=comm.scratch)(*comm.arrays)


def _call(body, *, name, grid, in_specs, out_specs, out_shape, args, scratch=(), sem=None, comm=None):
    if comm is None:
        res = pl.pallas_call(body, name=name, grid=grid, in_specs=list(in_specs), out_specs=list(out_specs),
                             out_shape=list(out_shape), scratch_shapes=list(scratch), compiler_params=_params(sem))(*args)
        return list(res), None
    n_in, n_out, n_scr = len(in_specs), len(out_specs), len(scratch)
    c_n = len(comm.arrays)

    def wrapped(*refs):
        ins, cin = refs[:n_in], refs[n_in:n_in + c_n]
        o0 = n_in + c_n
        outs, cout = refs[o0:o0 + n_out], refs[o0 + n_out:o0 + n_out + c_n]
        s0 = o0 + n_out + c_n
        scr, cscr = refs[s0:s0 + n_scr], refs[s0 + n_scr:]
        ids = [pl.program_id(ax) for ax in range(len(grid))]
        first = functools.reduce(jnp.logical_and, [i == 0 for i in ids])
        last = functools.reduce(jnp.logical_and, [i == g - 1 for i, g in zip(ids, grid)])

        @pl.when(first)
        def _():
            comm.start(cin, cout, cscr)

        body(*ins, *outs, *scr)

        @pl.when(last)
        def _():
            comm.finish(cin, cout, cscr)

    res = pl.pallas_call(
        wrapped, name=name, grid=grid, in_specs=list(in_specs) + [_ANY] * c_n, out_specs=list(out_specs) + [_ANY] * c_n,
        out_shape=list(out_shape) + list(comm.out_shape), scratch_shapes=list(scratch) + list(comm.scratch),
        compiler_params=_params(("arbitrary",) * len(grid)))(*args, *comm.arrays)
    return list(res[:n_out]), list(res[n_out:])


MM_B_BLOCK_BYTES = 6 * 1024 * 1024
MM_O_BLOCK_BYTES = 13 * 1024 * 1024 // 2


def _mm(a, b, name, tb=False, out_dtype=F32, b3=False, o_cs=None, tm_max=1088, comm=None):
    m, k = a.shape
    if b3:
        cs = b.shape[2]
        n, kb = (b.shape[1], N_DEV * cs) if tb else (N_DEV * cs, b.shape[1])
    else:
        n, kb = (b.shape[0], b.shape[1]) if tb else (b.shape[1], b.shape[0])
    assert k == kb, (a.shape, b.shape, tb)
    tm = _tile(m, tm_max, 2 * SUBLANES)
    tk = cs if (b3 and tb) else k
    nk = k // tk
    if b3 and not tb:
        tn = cs
    elif o_cs is not None:
        tn = o_cs if o_cs % LANES == 0 else 2 * o_cs
    else:
        tn = _tile(n, min(MM_B_BLOCK_BYTES // (2 * tk), MM_O_BLOCK_BYTES // (4 * tm)), LANES)
    cb = 1 if tb else 0
    dn = (((1,), (cb,)), ((), ()))

    def body_one(a_ref, b_ref, o_ref):
        r = lax.dot_general(a_ref[...], b_ref[...], dn, preferred_element_type=F32)
        if o_cs is None:
            o_ref[...] = r.astype(o_ref.dtype)
        else:
            for j in range(tn // o_cs):
                o_ref[j] = r[:, j * o_cs:(j + 1) * o_cs].astype(o_ref.dtype)

    def body_acc(a_ref, b_ref, o_ref, acc_ref):
        kk = pl.program_id(2)
        prod = lax.dot_general(a_ref[...], b_ref[...], dn, preferred_element_type=F32)

        @pl.when(kk == 0)
        def _():
            acc_ref[...] = prod

        @pl.when(kk > 0)
        def _():
            acc_ref[...] += prod

        @pl.when(kk == nk - 1)
        def _():
            o_ref[...] = acc_ref[...].astype(o_ref.dtype)

    a_spec = pl.BlockSpec((tm, tk), lambda i, j, kk: (i, kk))
    if b3:
        b_spec = (pl.BlockSpec((None, tn, cs), lambda i, j, kk: (kk, j, 0)) if tb
                  else pl.BlockSpec((None, tk, cs), lambda i, j, kk: (j, kk, 0)))
    else:
        b_spec = pl.BlockSpec((tn, tk), lambda i, j, kk: (j, kk)) if tb else pl.BlockSpec((tk, tn), lambda i, j, kk: (kk, j))
    if o_cs is None:
        o_spec = pl.BlockSpec((tm, tn), lambda i, j, kk: (i, j))
        o_shape = jax.ShapeDtypeStruct((m, n), out_dtype)
    else:
        assert nk == 1
        o_spec = pl.BlockSpec((tn // o_cs, tm, o_cs), lambda i, j, kk: (j, i, 0))
        o_shape = jax.ShapeDtypeStruct((n // o_cs, m, o_cs), out_dtype)
    res, cres = _call(
        body_one if nk == 1 else body_acc, name=name, grid=(m // tm, n // tn, nk), in_specs=[a_spec, b_spec],
        out_specs=[o_spec], out_shape=[o_shape],
        scratch=[] if nk == 1 else [pltpu.VMEM((tm, tn), F32)], sem=("parallel", "parallel", "arbitrary"),
        args=(a, b), comm=comm)
    return res[0], cres


DW_TM = 512


def _transpose_bf16(x, name):
    t, c = x.shape
    tt = _tm()

    def body(x_ref, o_ref):
        o_ref[...] = x_ref[...].T

    return pl.pallas_call(body, name=name, grid=(t // tt,), in_specs=[pl.BlockSpec((tt, c), lambda i: (i, 0))],
                          out_specs=pl.BlockSpec((c, tt), lambda i: (0, i)),
                          out_shape=jax.ShapeDtypeStruct((c, t), BF16), compiler_params=_params(("parallel",)))(x)


def _cast_bf16(x, name):
    r, c = x.shape
    tr = _tile(r, 512, 2 * SUBLANES)

    def body(x_ref, o_ref):
        o_ref[...] = x_ref[...].astype(BF16)

    return pl.pallas_call(body, name=name, grid=(r // tr,), in_specs=[pl.BlockSpec((tr, c), lambda i: (i, 0))],
                          out_specs=pl.BlockSpec((tr, c), lambda i: (i, 0)),
                          out_shape=jax.ShapeDtypeStruct((r, c), BF16), compiler_params=_params(("parallel",)))(x)


def _cols_from_shards(wg, name):
    _, k, cs = wg.shape
    tk = _tile(k, 256, 2 * SUBLANES)

    def body(w_ref, o_ref):
        for j in range(N_DEV):
            o_ref[:, j * cs:(j + 1) * cs] = w_ref[j]

    return pl.pallas_call(body, name=name, grid=(k // tk,),
                          in_specs=[pl.BlockSpec((N_DEV, tk, cs), lambda i: (0, i, 0))],
                          out_specs=pl.BlockSpec((tk, N_DEV * cs), lambda i: (i, 0)),
                          out_shape=jax.ShapeDtypeStruct((k, N_DEV * cs), wg.dtype),
                          compiler_params=_params(("parallel",)))(wg)


def _stream(i):
    return jnp.minimum(i, 1)


def _normmod(x, g, sh, sc):
    y = x * lax.rsqrt(jnp.mean(x * x, axis=-1, keepdims=True) + EPS)
    return (y * g) * (1.0 + sc) + sh


def _mod_spec(chunk, d):
    return pl.BlockSpec((None, None, 1, d), lambda i: (_stream(i), chunk, 0, 0))


def _normmod_fwd(x, g, mod4, which, name):
    t, d = x.shape
    tm = _tm()
    ish, isc = (0, 1) if which == 0 else (3, 4)

    def body(x_ref, g_ref, sh_ref, sc_ref, o_ref):
        o_ref[...] = _normmod(x_ref[...], g_ref[...], sh_ref[...], sc_ref[...]).astype(BF16)

    row = pl.BlockSpec((tm, d), lambda i: (i, 0))
    return pl.pallas_call(body, name=name, grid=(t // tm,),
                          in_specs=[row, pl.BlockSpec((1, d), lambda i: (0, 0)), _mod_spec(ish, d), _mod_spec(isc, d)],
                          out_specs=row, out_shape=jax.ShapeDtypeStruct((t, d), BF16),
                          compiler_params=_params(("parallel",)))(x, g, mod4, mod4)


def _normmod_bwd(x, g, mod4, which, dh, dres, name, comm=None):
    t, d = x.shape
    tm = _tm()
    ish, isc = (0, 1) if which == 0 else (3, 4)

    def body(x_ref, g_ref, sh_ref, sc_ref, dh_ref, dres_ref, dx_ref, dg_ref, dsh_ref, dsc_ref):
        i = pl.program_id(0)
        _, vjp = jax.vjp(_normmod, x_ref[...], g_ref[...], sh_ref[...], sc_ref[...])
        dx, dg, dsh, dsc = vjp(dh_ref[...])
        dx_ref[...] = dres_ref[...] + dx

        @pl.when(i == 0)
        def _():
            dg_ref[...] = jnp.zeros_like(dg_ref)

        @pl.when(i <= 1)
        def _():
            dsh_ref[...] = jnp.zeros_like(dsh_ref)
            dsc_ref[...] = jnp.zeros_like(dsc_ref)

        dg_ref[...] += dg
        dsh_ref[...] += dsh
        dsc_ref[...] += dsc

    row = pl.BlockSpec((tm, d), lambda i: (i, 0))
    vec = pl.BlockSpec((1, d), lambda i: (0, 0))
    svec = pl.BlockSpec((None, 1, d), lambda i: (_stream(i), 0, 0))
    return _call(
        body, name=name, grid=(t // tm,),
        in_specs=[row, vec, _mod_spec(ish, d), _mod_spec(isc, d), row, row],
        out_specs=[row, vec, svec, svec],
        out_shape=[jax.ShapeDtypeStruct((t, d), F32), jax.ShapeDtypeStruct((1, d), F32),
                   jax.ShapeDtypeStruct((2, 1, d), F32), jax.ShapeDtypeStruct((2, 1, d), F32)],
        sem=("arbitrary",), args=(x, g, mod4, mod4, dh, dres), comm=comm)


def _gate_res_fwd(x, f, mod4, chunk, name):
    t, d = x.shape
    tm = _tm()

    def body(x_ref, f_ref, g_ref, o_ref):
        o_ref[...] = x_ref[...] + g_ref[...] * f_ref[...]

    row = pl.BlockSpec((tm, d), lambda i: (i, 0))
    return pl.pallas_call(body, name=name, grid=(t // tm,), in_specs=[row, row, _mod_spec(chunk, d)], out_specs=row,
                          out_shape=jax.ShapeDtypeStruct((t, d), F32), compiler_params=_params(("parallel",)))(x, f, mod4)


def _gate_res_bwd(dx, f, mod4, chunk, name):
    t, d = dx.shape
    tm = _tm()

    def body(dx_ref, f_ref, g_ref, o_ref, dg_ref):
        i = pl.program_id(0)
        dxv = dx_ref[...]
        o_ref[...] = (dxv * g_ref[...]).astype(BF16)

        @pl.when(i <= 1)
        def _():
            dg_ref[...] = jnp.zeros_like(dg_ref)

        dg_ref[...] += jnp.sum(dxv * f_ref[...], axis=0, keepdims=True)

    row = pl.BlockSpec((tm, d), lambda i: (i, 0))
    return pl.pallas_call(
        body, name=name, grid=(t // tm,), in_specs=[row, row, _mod_spec(chunk, d)],
        out_specs=[row, pl.BlockSpec((None, 1, d), lambda i: (_stream(i), 0, 0))],
        out_shape=[jax.ShapeDtypeStruct((t, d), BF16), jax.ShapeDtypeStruct((2, 1, d), F32)],
        compiler_params=_params(("arbitrary",)))(dx, f, mod4)


def _loss_head(x, final_g, target, name):
    t, d = x.shape
    tm = _tm()

    def loss_fn(xv, g, tgt):
        y = (xv * lax.rsqrt(jnp.mean(xv * xv, axis=-1, keepdims=True) + EPS)) * g
        err = y - tgt
        return 0.5 * jnp.sum(jnp.mean(err * err, axis=-1, keepdims=True))

    def body(x_ref, g_ref, t_ref, l_ref, dx_ref, dg_ref):
        i = pl.program_id(0)

        @pl.when(i == 0)
        def _():
            l_ref[...] = jnp.zeros_like(l_ref)
            dg_ref[...] = jnp.zeros_like(dg_ref)
            dx_ref[...] = jnp.zeros_like(dx_ref)

        @pl.when(i > 0)
        def _():
            l, (dx, dg) = jax.value_and_grad(loss_fn, argnums=(0, 1))(x_ref[...], g_ref[...], t_ref[...])
            l_ref[...] += jnp.full(l_ref.shape, l, F32)
            dx_ref[...] = dx
            dg_ref[...] += dg

    row = pl.BlockSpec((tm, d), lambda i: (i, 0))
    vec = pl.BlockSpec((1, d), lambda i: (0, 0))
    return pl.pallas_call(
        body, name=name, grid=(t // tm,),
        in_specs=[row, vec, pl.BlockSpec((tm, d), lambda i: (jnp.maximum(i - 1, 0), 0))],
        out_specs=[pl.BlockSpec((SUBLANES, LANES), lambda i: (0, 0)), row, vec],
        out_shape=[jax.ShapeDtypeStruct((SUBLANES, LANES), F32), jax.ShapeDtypeStruct((t, d), F32),
                   jax.ShapeDtypeStruct((1, d), F32)],
        compiler_params=_params(("arbitrary",)))(x, final_g, target)


def _swap_quarters(x):
    half, nf = RET_DK // 2, RET_DK // 4
    lane = lax.broadcasted_iota(jnp.int32, x.shape, 1)
    return jnp.where((lane % half) < nf, pltpu.roll(x, RET_DK - nf, 1), pltpu.roll(x, nf, 1))


def _rope(x, cos, sin):
    return x * cos + _swap_quarters(x) * sin


def _rope_t(y, cos, sin):
    return y * cos + _swap_quarters(y * sin)


def _ret_consts(d):
    c = RET_CHUNK
    ii = lax.broadcasted_iota(jnp.int32, (c, 1), 0).astype(F32)
    jj = lax.broadcasted_iota(jnp.int32, (1, c), 1).astype(F32)
    fwd = d == 0
    sgn = jnp.where(fwd, 1.0, -1.0).astype(F32)
    pos = jnp.where(fwd, ii, c - 1.0 - ii)
    return sgn * (ii - jj), pos


def _ret_step(lgt, state, q, k, v, diff, pos):
    c = float(RET_CHUNK)
    lg = -(jnp.maximum(-lgt, 0.0) + jnp.log1p(jnp.exp(-jnp.abs(lgt))))
    lower = diff >= 0
    decay = jnp.where(lower, jnp.exp(jnp.where(lower, diff, 0.0) * lg), 0.0)
    xi = jnp.exp((pos + 1.0) * lg)
    zeta = jnp.exp((c - 1.0 - pos) * lg)
    gch = jnp.exp(c * lg)
    inner = dot_nt(q, k) * decay
    out = dot_nn(inner, v) + dot_nn(q, state) * xi
    new_state = state * gch + dot_tn(k * zeta, v)
    return out, new_state


def _chunk_order():
    nc, nch = CTX_LEN // RET_CHUNK, _t_rows() // RET_CHUNK
    fwd = list(range(nch))
    bwd = list(range(nc - 1, -1, -1)) + list(range(nch - 1, nc - 1, -1))
    return jnp.asarray(np.array([fwd, bwd], np.int32))


def _ret_fwd(p, cos, sin, decay, order, name):
    t = p.shape[0]
    c, dk, dv, nh = RET_CHUNK, RET_DK, RET_DV, RET_HEADS
    nch = t // c
    off = _offsets()
    wqk, wv = nh * dk, nh * dv
    assert off["q"] % wqk == 0 and off["k"] % wqk == 0 and off["v"] % wv == 0
    qb, kb, vb = off["q"] // wqk, off["k"] // wqk, off["v"] // wv
    scale = RET_DK ** -0.5

    def body(ord_ref, dec_ref, q_ref, k_ref, v_ref, cos_ref, sin_ref, o_ref, st_ref, state):
        d, s = pl.program_id(0), pl.program_id(1)

        @pl.when(s == 0)
        def _():
            state[...] = jnp.zeros_like(state)

        diff, pos = _ret_consts(d)
        cosv, sinv = cos_ref[...], sin_ref[...]
        for h in range(nh):
            st = state[h]
            st_ref[h] = st
            lgt = jnp.full((1, 1), dec_ref[d, h], F32)
            q = _rope(q_ref[:, h * dk:(h + 1) * dk], cosv, sinv) * scale
            k = _rope(k_ref[:, h * dk:(h + 1) * dk], cosv, sinv)
            out, ns = _ret_step(lgt, st, q, k, v_ref[:, h * dv:(h + 1) * dv], diff, pos)
            o_ref[:, h * dv:(h + 1) * dv] = out
            state[h] = ns

    grid_spec = pltpu.PrefetchScalarGridSpec(
        num_scalar_prefetch=1, grid=(2, nch),
        in_specs=[pl.BlockSpec(memory_space=pltpu.SMEM),
                  pl.BlockSpec((c, wqk), lambda d, s, o: (o[d, s], qb)),
                  pl.BlockSpec((c, wqk), lambda d, s, o: (o[d, s], kb)),
                  pl.BlockSpec((c, wv), lambda d, s, o: (o[d, s], vb)),
                  pl.BlockSpec((c, dk), lambda d, s, o: (o[d, s], 0)),
                  pl.BlockSpec((c, dk), lambda d, s, o: (o[d, s], 0))],
        out_specs=[pl.BlockSpec((None, c, wv), lambda d, s, o: (d, o[d, s], 0)),
                   pl.BlockSpec((None, nh, None, dk, dv), lambda d, s, o: (d, 0, s, 0, 0))],
        scratch_shapes=[pltpu.VMEM((nh, dk, dv), F32)])
    return pl.pallas_call(
        body, name=name, grid_spec=grid_spec,
        out_shape=[jax.ShapeDtypeStruct((2, t, wv), F32), jax.ShapeDtypeStruct((2, nh, nch, dk, dv), F32)],
        compiler_params=_params(("arbitrary", "arbitrary")))(order, decay, p, p, p, cos, sin)


def _ret_bwd(p, cos, sin, decay, order, states, do, name):
    t = p.shape[0]
    c, dk, dv, nh = RET_CHUNK, RET_DK, RET_DV, RET_HEADS
    nch = t // c
    off = _offsets()
    wqk, wv = nh * dk, nh * dv
    qb, kb, vb = off["q"] // wqk, off["k"] // wqk, off["v"] // wv
    scale = RET_DK ** -0.5

    def body(ord_ref, dec_ref, q_ref, k_ref, v_ref, cos_ref, sin_ref, st_ref, do_ref,
             dq_ref, dk_ref, dv_ref, dd_ref, dstate):
        d, s = pl.program_id(0), pl.program_id(1)

        @pl.when(s == 0)
        def _():
            dstate[...] = jnp.zeros_like(dstate)
            dd_ref[...] = jnp.zeros_like(dd_ref)

        diff, pos = _ret_consts(d)
        cosv, sinv = cos_ref[...], sin_ref[...]
        for h in range(nh):
            qk, vv = slice(h * dk, (h + 1) * dk), slice(h * dv, (h + 1) * dv)
            lgt = jnp.full((1, 1), dec_ref[d, h], F32)
            q = _rope(q_ref[:, qk], cosv, sinv) * scale
            k = _rope(k_ref[:, qk], cosv, sinv)
            _, vjp = jax.vjp(lambda a, b, cq, ck, cv: _ret_step(a, b, cq, ck, cv, diff, pos),
                             lgt, st_ref[h], q, k, v_ref[:, vv])
            dlgt, dst, dq, dkk, dvv = vjp((do_ref[:, vv], dstate[h]))
            dstate[h] = dst
            dq_ref[:, qk] = _rope_t(dq * scale, cosv, sinv)
            dk_ref[:, qk] = _rope_t(dkk, cosv, sinv)
            dv_ref[:, vv] = dvv
            dd_ref[h] += jnp.broadcast_to(dlgt, (SUBLANES, LANES))

    rev = lambda o, d, s: o[d, nch - 1 - s]
    grid_spec = pltpu.PrefetchScalarGridSpec(
        num_scalar_prefetch=1, grid=(2, nch),
        in_specs=[pl.BlockSpec(memory_space=pltpu.SMEM),
                  pl.BlockSpec((c, wqk), lambda d, s, o: (rev(o, d, s), qb)),
                  pl.BlockSpec((c, wqk), lambda d, s, o: (rev(o, d, s), kb)),
                  pl.BlockSpec((c, wv), lambda d, s, o: (rev(o, d, s), vb)),
                  pl.BlockSpec((c, dk), lambda d, s, o: (rev(o, d, s), 0)),
                  pl.BlockSpec((c, dk), lambda d, s, o: (rev(o, d, s), 0)),
                  pl.BlockSpec((None, nh, None, dk, dv), lambda d, s, o: (d, 0, nch - 1 - s, 0, 0)),
                  pl.BlockSpec((c, wv), lambda d, s, o: (rev(o, d, s), 0))],
        out_specs=[pl.BlockSpec((None, c, wqk), lambda d, s, o: (d, rev(o, d, s), 0)),
                   pl.BlockSpec((None, c, wqk), lambda d, s, o: (d, rev(o, d, s), 0)),
                   pl.BlockSpec((None, c, wv), lambda d, s, o: (d, rev(o, d, s), 0)),
                   pl.BlockSpec((None, nh, SUBLANES, LANES), lambda d, s, o: (d, 0, 0, 0))],
        scratch_shapes=[pltpu.VMEM((nh, dk, dv), F32)])
    return pl.pallas_call(
        body, name=name, grid_spec=grid_spec,
        out_shape=[jax.ShapeDtypeStruct((2, t, wqk), F32), jax.ShapeDtypeStruct((2, t, wqk), F32),
                   jax.ShapeDtypeStruct((2, t, wv), F32), jax.ShapeDtypeStruct((2, nh, SUBLANES, LANES), F32)],
        compiler_params=_params(("arbitrary", "arbitrary")))(order, decay, p, p, p, cos, sin, states, do)


def _ggn_head(of, ob, gate, g):
    o = of + ob
    mu = jnp.mean(o, axis=-1, keepdims=True)
    var = jnp.mean(jnp.square(o - mu), axis=-1, keepdims=True)
    return ((o - mu) * lax.rsqrt(var + EPS) * g) * _silu(gate)


def _ggn_fwd(o2, p, gn_g, name):
    t = p.shape[0]
    tm, w, dv = _tm(), _ret_w(), RET_DV
    gb = _offsets()["g"] // w

    def body(o_ref, gate_ref, g_ref, out_ref):
        for h in range(RET_HEADS):
            sl = slice(h * dv, (h + 1) * dv)
            out_ref[:, sl] = _ggn_head(o_ref[0, :, sl], o_ref[1, :, sl], gate_ref[:, sl], g_ref[:, sl]).astype(BF16)

    return pl.pallas_call(
        body, name=name, grid=(t // tm,),
        in_specs=[pl.BlockSpec((2, tm, w), lambda i: (0, i, 0)), pl.BlockSpec((tm, w), lambda i: (i, gb)),
                  pl.BlockSpec((1, w), lambda i: (0, 0))],
        out_specs=pl.BlockSpec((tm, w), lambda i: (i, 0)), out_shape=jax.ShapeDtypeStruct((t, w), BF16),
        compiler_params=_params(("parallel",)))(o2, p, gn_g)


def _ggn_bwd(o2, p, gn_g, dmix, name):
    t = p.shape[0]
    tm, w, dv = _tm(), _ret_w(), RET_DV
    gb = _offsets()["g"] // w

    def body(o_ref, gate_ref, g_ref, dy_ref, do_ref, dgate_ref, dg_ref):
        i = pl.program_id(0)

        @pl.when(i == 0)
        def _():
            dg_ref[...] = jnp.zeros_like(dg_ref)

        for h in range(RET_HEADS):
            sl = slice(h * dv, (h + 1) * dv)
            _, vjp = jax.vjp(_ggn_head, o_ref[0, :, sl], o_ref[1, :, sl], gate_ref[:, sl], g_ref[:, sl])
            do, _, dgate, dg = vjp(dy_ref[:, sl])
            do_ref[:, sl] = do
            dgate_ref[:, sl] = dgate
            dg_ref[:, sl] += dg

    row = pl.BlockSpec((tm, w), lambda i: (i, 0))
    return pl.pallas_call(
        body, name=name, grid=(t // tm,),
        in_specs=[pl.BlockSpec((2, tm, w), lambda i: (0, i, 0)), pl.BlockSpec((tm, w), lambda i: (i, gb)),
                  pl.BlockSpec((1, w), lambda i: (0, 0)), row],
        out_specs=[row, row, pl.BlockSpec((1, w), lambda i: (0, 0))],
        out_shape=[jax.ShapeDtypeStruct((t, w), F32), jax.ShapeDtypeStruct((t, w), F32),
                   jax.ShapeDtypeStruct((1, w), F32)],
        compiler_params=_params(("arbitrary",)))(o2, p, gn_g, dmix)


def _halo(k):
    return SUBLANES * ((k // 2 + SUBLANES - 1) // SUBLANES)


def _halo_specs(width, colblock, h, tm):
    r = tm // h
    return [pl.BlockSpec((h, width), lambda i, *_: (jnp.maximum(i * r - 1, 0), colblock(*_))),
            pl.BlockSpec((tm, width), lambda i, *_: (i, colblock(*_))),
            pl.BlockSpec((h, width), lambda i, *_: (jnp.minimum((i + 1) * r, (_t_rows() // h) - 1), colblock(*_)))]


def _fill_ext(ext_ref, prev, cur, nxt, i, h, tm):
    nt = _t_rows() // tm
    ext_ref[0:h, :] = jnp.where(i >= 2, prev, 0.0)
    ext_ref[h:h + tm, :] = cur
    ext_ref[h + tm:h + tm + h, :] = jnp.where((i >= 1) & (i <= nt - 2), nxt, 0.0)


def _corr(ext_ref, w_ref, k, h, tm, flip):
    pad = k // 2
    acc = None
    for kk in range(k):
        o = h + (pad - kk if flip else kk - pad)
        term = w_ref[kk:kk + 1, :] * ext_ref[o:o + tm, :]
        acc = term if acc is None else acc + term
    return acc


def _conv_post(u2, ln_g, ln_b, pw):
    mu = jnp.mean(u2, axis=-1, keepdims=True)
    var = jnp.mean(jnp.square(u2 - mu), axis=-1, keepdims=True)
    y = (u2 - mu) * lax.rsqrt(var + EPS) * ln_g + ln_b
    return dot_nn(_silu(y), pw)


def _conv_fwd(p, dw_w, dw_b, ln_g, ln_b, pw, name):
    t = p.shape[0]
    tm, w, k = _tm(), CONV_W, CONV_K
    h = _halo(k)
    off = _offsets()
    ab, bb = off["a"] // w, off["b"] // w

    def body(ap, ac, an, bp, bc, bn, w_ref, b_ref, g_ref, beta_ref, pw_ref, u2_ref, out_ref, ext):
        i = pl.program_id(0)
        glu = lambda a, b: a * _sigmoid(b)
        _fill_ext(ext, glu(ap[...], bp[...]), glu(ac[...], bc[...]), glu(an[...], bn[...]), i, h, tm)
        u2 = _corr(ext, w_ref, k, h, tm, False) + b_ref[...]
        u2_ref[...] = u2
        out_ref[...] = _conv_post(u2, g_ref[...], beta_ref[...], pw_ref[...]).astype(BF16)

    vec = pl.BlockSpec((1, w), lambda i: (0, 0))
    row = pl.BlockSpec((tm, w), lambda i: (i, 0))
    return pl.pallas_call(
        body, name=name, grid=(t // tm,),
        in_specs=_halo_specs(w, lambda: ab, h, tm) + _halo_specs(w, lambda: bb, h, tm)
        + [pl.BlockSpec((k, w), lambda i: (0, 0)), vec, vec, vec, pl.BlockSpec((w, w), lambda i: (0, 0))],
        out_specs=[row, row],
        out_shape=[jax.ShapeDtypeStruct((t, w), F32), jax.ShapeDtypeStruct((t, w), BF16)],
        scratch_shapes=[pltpu.VMEM((tm + 2 * h, w), F32)],
        compiler_params=_params(("parallel",)))(p, p, p, p, p, p, dw_w, dw_b, ln_g, ln_b, pw)


def _conv_bwd1(u2, dmix, ln_g, ln_b, pw, name):
    t = u2.shape[0]
    tm, w = _tm(), CONV_W
    cb = _ret_w() // w

    def body(u2_ref, dy_ref, g_ref, beta_ref, pw_ref, du2_ref, dg_ref, db_ref, dpw_ref):
        i = pl.program_id(0)

        @pl.when(i == 0)
        def _():
            dg_ref[...] = jnp.zeros_like(dg_ref)
            db_ref[...] = jnp.zeros_like(db_ref)
            dpw_ref[...] = jnp.zeros_like(dpw_ref)

        _, vjp = jax.vjp(_conv_post, u2_ref[...], g_ref[...], beta_ref[...], pw_ref[...])
        du2, dg, db, dpw = vjp(dy_ref[...])
        du2_ref[...] = du2
        dg_ref[...] += dg
        db_ref[...] += db
        dpw_ref[...] += dpw

    vec = pl.BlockSpec((1, w), lambda i: (0, 0))
    row = pl.BlockSpec((tm, w), lambda i: (i, 0))
    mat = pl.BlockSpec((w, w), lambda i: (0, 0))
    return pl.pallas_call(
        body, name=name, grid=(t // tm,),
        in_specs=[row, pl.BlockSpec((tm, w), lambda i: (i, cb)), vec, vec, mat],
        out_specs=[row, vec, vec, mat],
        out_shape=[jax.ShapeDtypeStruct((t, w), F32), jax.ShapeDtypeStruct((1, w), F32),
                   jax.ShapeDtypeStruct((1, w), F32), jax.ShapeDtypeStruct((w, w), F32)],
        compiler_params=_params(("arbitrary",)))(u2, dmix, ln_g, ln_b, pw)


def _conv_bwd2(du2, p, dw_w, name, comm=None):
    t = p.shape[0]
    tm, w, k = _tm(), CONV_W, CONV_K
    h = _halo(k)
    pad = k // 2
    off = _offsets()
    ab, bb = off["a"] // w, off["b"] // w

    def body(dp, dc, dn, ap, ac, an, bp, bc, bn, w_ref, da_ref, db_ref, dw_ref, dbias_ref, ext_d, ext_u):
        i = pl.program_id(0)

        @pl.when(i == 0)
        def _():
            dw_ref[...] = jnp.zeros_like(dw_ref)
            dbias_ref[...] = jnp.zeros_like(dbias_ref)

        glu = lambda a, b: a * _sigmoid(b)
        a, b, d = ac[...], bc[...], dc[...]
        _fill_ext(ext_d, dp[...], d, dn[...], i, h, tm)
        _fill_ext(ext_u, glu(ap[...], bp[...]), glu(a, b), glu(an[...], bn[...]), i, h, tm)
        du = _corr(ext_d, w_ref, k, h, tm, True)
        sg = _sigmoid(b)
        da_ref[...] = du * sg
        db_ref[...] = du * a * sg * (1.0 - sg)
        dbias_ref[...] += jnp.sum(d, axis=0, keepdims=True)
        for kk in range(k):
            o = h + kk - pad
            dw_ref[kk:kk + 1, :] += jnp.sum(d * ext_u[o:o + tm, :], axis=0, keepdims=True)

    vec = pl.BlockSpec((1, w), lambda i: (0, 0))
    row = pl.BlockSpec((tm, w), lambda i: (i, 0))
    kw = pl.BlockSpec((k, w), lambda i: (0, 0))
    return _call(
        body, name=name, grid=(t // tm,),
        in_specs=_halo_specs(w, lambda: 0, h, tm) + _halo_specs(w, lambda: ab, h, tm)
        + _halo_specs(w, lambda: bb, h, tm) + [kw],
        out_specs=[row, row, kw, vec],
        out_shape=[jax.ShapeDtypeStruct((t, w), F32), jax.ShapeDtypeStruct((t, w), F32),
                   jax.ShapeDtypeStruct((k, w), F32), jax.ShapeDtypeStruct((1, w), F32)],
        scratch=[pltpu.VMEM((tm + 2 * h, w), F32), pltpu.VMEM((tm + 2 * h, w), F32)],
        sem=("arbitrary",), args=(du2, du2, du2, p, p, p, p, p, p, dw_w), comm=comm)


def _ffn_tc():
    return _tile(D_FF, 512, LANES)


def _ffn_act_fwd(u, dw_w, dw_b, name):
    t = u.shape[0]
    tm, k, tc = _tm(), FFN_K, _ffn_tc()
    h = _halo(k)
    nj = D_FF // tc

    def body(vp, vc, vn, gp, gc, gn, wv, wg, bv, bg, out_ref, ext_v, ext_g):
        i = pl.program_id(0)
        _fill_ext(ext_v, vp[...], vc[...], vn[...], i, h, tm)
        _fill_ext(ext_g, gp[...], gc[...], gn[...], i, h, tm)
        for r0 in range(0, tm, ROW_CHUNK):
            val = _corr(ext_v, wv, k, h + r0, ROW_CHUNK, False) + bv[...]
            gate = _corr(ext_g, wg, k, h + r0, ROW_CHUNK, False) + bg[...]
            out_ref[r0:r0 + ROW_CHUNK, :] = (_silu(gate) * val).astype(BF16)

    wspec = lambda s: pl.BlockSpec((k, tc), lambda i, j: (0, j + s))
    bspec = lambda s: pl.BlockSpec((1, tc), lambda i, j: (0, j + s))
    return pl.pallas_call(
        body, name=name, grid=(t // tm, nj),
        in_specs=_halo_specs(tc, lambda j: j, h, tm) + _halo_specs(tc, lambda j: j + nj, h, tm)
        + [wspec(0), wspec(nj), bspec(0), bspec(nj)],
        out_specs=pl.BlockSpec((tm, tc), lambda i, j: (i, j)),
        out_shape=jax.ShapeDtypeStruct((t, D_FF), BF16),
        scratch_shapes=[pltpu.VMEM((tm + 2 * h, tc), F32), pltpu.VMEM((tm + 2 * h, tc), F32)],
        compiler_params=_params(("parallel", "parallel")))(u, u, u, u, u, u, dw_w, dw_w, dw_b, dw_b)


def _ffn_act_bwd1(u, da, dw_w, dw_b, name):
    t = u.shape[0]
    tm, k, tc = _tm(), FFN_K, _ffn_tc()
    h = _halo(k)
    nj = D_FF // tc

    def body(vp, vc, vn, gp, gc, gn, wv, wg, bv, bg, da_ref, dv_ref, dg_ref, ext_v, ext_g):
        i = pl.program_id(0)
        _fill_ext(ext_v, vp[...], vc[...], vn[...], i, h, tm)
        _fill_ext(ext_g, gp[...], gc[...], gn[...], i, h, tm)
        for r0 in range(0, tm, ROW_CHUNK):
            rows = slice(r0, r0 + ROW_CHUNK)
            val = _corr(ext_v, wv, k, h + r0, ROW_CHUNK, False) + bv[...]
            gate = _corr(ext_g, wg, k, h + r0, ROW_CHUNK, False) + bg[...]
            _, vjp = jax.vjp(lambda a, b: _silu(b) * a, val, gate)
            dval, dgate = vjp(da_ref[rows, :])
            dv_ref[rows, :] = dval
            dg_ref[rows, :] = dgate

    wspec = lambda s: pl.BlockSpec((k, tc), lambda i, j: (0, j + s))
    bspec = lambda s: pl.BlockSpec((1, tc), lambda i, j: (0, j + s))
    dc = pl.pallas_call(
        body, name=name, grid=(t // tm, nj),
        in_specs=_halo_specs(tc, lambda j: j, h, tm) + _halo_specs(tc, lambda j: j + nj, h, tm)
        + [wspec(0), wspec(nj), bspec(0), bspec(nj), pl.BlockSpec((tm, tc), lambda i, j: (i, j))],
        out_specs=[pl.BlockSpec((tm, tc), lambda i, j: (i, j)), pl.BlockSpec((tm, tc), lambda i, j: (i, j))],
        out_shape=[jax.ShapeDtypeStruct((t, D_FF), F32), jax.ShapeDtypeStruct((t, D_FF), F32)],
        scratch_shapes=[pltpu.VMEM((tm + 2 * h, tc), F32), pltpu.VMEM((tm + 2 * h, tc), F32)],
        compiler_params=_params(("parallel", "parallel")))(u, u, u, u, u, u, dw_w, dw_w, dw_b, dw_b, da)
    return dc


def _dwconv_bwd(dc, u, dw_w, colblock, name):
    t = u.shape[0]
    tm, k, tc = _tm(), FFN_K, _ffn_tc()
    h = _halo(k)
    pad = k // 2
    nj = D_FF // tc

    def body(dp, dcur, dn, up, uc, un, w_ref, du_ref, dw_ref, dbias_ref, ext_d, ext_u):
        i = pl.program_id(1)

        @pl.when(i == 0)
        def _():
            dw_ref[...] = jnp.zeros_like(dw_ref)
            dbias_ref[...] = jnp.zeros_like(dbias_ref)

        _fill_ext(ext_d, dp[...], dcur[...], dn[...], i, h, tm)
        _fill_ext(ext_u, up[...], uc[...], un[...], i, h, tm)
        acc_b = jnp.zeros((ROW_CHUNK, tc), F32)
        acc_w = [jnp.zeros((ROW_CHUNK, tc), F32) for _ in range(k)]
        for r0 in range(0, tm, ROW_CHUNK):
            d = ext_d[h + r0:h + r0 + ROW_CHUNK, :]
            du_ref[r0:r0 + ROW_CHUNK, :] = _corr(ext_d, w_ref, k, h + r0, ROW_CHUNK, True).astype(BF16)
            acc_b = acc_b + d
            for kk in range(k):
                o = h + r0 + kk - pad
                acc_w[kk] = acc_w[kk] + d * ext_u[o:o + ROW_CHUNK, :]
        dbias_ref[...] += jnp.sum(acc_b, axis=0, keepdims=True)
        for kk in range(k):
            dw_ref[kk:kk + 1, :] += jnp.sum(acc_w[kk], axis=0, keepdims=True)

    def hs(cb):
        r = tm // h
        return [pl.BlockSpec((h, tc), lambda j, i: (jnp.maximum(i * r - 1, 0), cb(j))),
                pl.BlockSpec((tm, tc), lambda j, i: (i, cb(j))),
                pl.BlockSpec((h, tc), lambda j, i: (jnp.minimum((i + 1) * r, (_t_rows() // h) - 1), cb(j)))]

    return pl.pallas_call(
        body, name=name, grid=(nj, t // tm),
        in_specs=hs(lambda j: j) + hs(lambda j: j + colblock) + [pl.BlockSpec((k, tc), lambda j, i: (0, j + colblock))],
        out_specs=[pl.BlockSpec((tm, tc), lambda j, i: (i, j)), pl.BlockSpec((k, tc), lambda j, i: (0, j)),
                   pl.BlockSpec((1, tc), lambda j, i: (0, j))],
        out_shape=[jax.ShapeDtypeStruct((t, D_FF), BF16), jax.ShapeDtypeStruct((k, D_FF), F32),
                   jax.ShapeDtypeStruct((1, D_FF), F32)],
        scratch_shapes=[pltpu.VMEM((tm + 2 * h, tc), F32), pltpu.VMEM((tm + 2 * h, tc), F32)],
        compiler_params=_params(("parallel", "arbitrary")))(dc, dc, dc, u, u, u, dw_w)


def _na_geometry(rq):
    ncb = CTX_LEN // GRID_W
    rows_n = SEQ // GRID_W
    r = jnp.maximum(rq - ncb, 0)
    kstart = jnp.clip(r - NA_ROWS // 2, 0, rows_n - NA_ROWS)
    base = kstart - r + NA_ROWS - 1
    return rq >= ncb, kstart, base


def _na_core(q, kl, vl, kc, vc, bias, mask):
    qs = q * (NA_DH ** -0.5)
    s_l = jnp.where(mask, dot_nt(qs, kl) + bias, NEG)
    s_c = dot_nt(qs, kc)
    m = lax.stop_gradient(jnp.maximum(jnp.max(s_l, axis=1, keepdims=True), jnp.max(s_c, axis=1, keepdims=True)))
    e_l, e_c = jnp.exp(s_l - m), jnp.exp(s_c - m)
    inv = 1.0 / (jnp.sum(e_l, axis=1, keepdims=True) + jnp.sum(e_c, axis=1, keepdims=True))
    return dot_nn(e_l * inv, vl) + dot_nn(e_c * inv, vc)


def _na_mask(is_lat):
    nl = NA_ROWS * GRID_W
    q = lax.broadcasted_iota(jnp.int32, (GRID_W, nl), 0)
    w = lax.broadcasted_iota(jnp.int32, (GRID_W, nl), 1) % GRID_W
    cs = jnp.clip(q - NA_COLS // 2, 0, GRID_W - NA_COLS)
    return (w >= cs) & (w < cs + NA_COLS) & is_lat


def _na_bias(rb_ref):
    assert 2 * GRID_W == LANES
    lane = lax.broadcasted_iota(jnp.int32, (GRID_W, LANES), 1)
    tiles = []
    for kp in range(NA_ROWS // 2):
        ev = jnp.broadcast_to(rb_ref[2 * kp:2 * kp + 1, :], (GRID_W, LANES))
        od = jnp.broadcast_to(rb_ref[2 * kp + 1:2 * kp + 2, :], (GRID_W, LANES))
        ev = pltpu.roll(ev, LANES - (NA_COLS - 1), 1, stride=1, stride_axis=0)
        od = pltpu.roll(od, LANES - (NA_COLS - 1) - GRID_W, 1, stride=1, stride_axis=0)
        tiles.append(jnp.where(lane < GRID_W, ev, od))
    return jnp.concatenate(tiles, axis=1)


def _na_dbias(dbias, drb_ref):
    qi = lax.broadcasted_iota(jnp.int32, (GRID_W, GRID_W), 0)
    qj = lax.broadcasted_iota(jnp.int32, (GRID_W, GRID_W), 1)
    flip = (qi + qj == GRID_W - 1).astype(F32)
    rev = lax.dot_general(flip, dbias, (((1,), (0,)), ((), ())), precision=lax.Precision.HIGHEST,
                          preferred_element_type=F32)
    lane = lax.broadcasted_iota(jnp.int32, (GRID_W, LANES), 1)
    s_ev = LANES - (GRID_W - NA_COLS)
    for kp in range(NA_ROWS // 2):
        tile = rev[:, kp * LANES:(kp + 1) * LANES]
        ev = pltpu.roll(jnp.where(lane < GRID_W, tile, 0.0), s_ev, 1, stride=1, stride_axis=0)
        od = pltpu.roll(jnp.where(lane >= GRID_W, tile, 0.0), s_ev - GRID_W, 1, stride=1, stride_axis=0)
        drb_ref[2 * kp:2 * kp + 1, :] += jnp.sum(ev, axis=0, keepdims=True)
        drb_ref[2 * kp + 1:2 * kp + 2, :] += jnp.sum(od, axis=0, keepdims=True)


def _na_hps():
    return 2 if NA_HEADS % 2 == 0 else 1


def _na_specs(p_offsets):
    t = _t_rows()
    hps = _na_hps()
    wd = hps * NA_DH
    assert all(p_offsets[n] % wd == 0 for n in ("nq", "nk", "nv"))
    qb, kb, vb = (p_offsets[n] // wd for n in ("nq", "nk", "nv"))
    return [pl.BlockSpec((GRID_W, wd), lambda h, r: (r, qb + h)),
            pl.BlockSpec((t, wd), lambda h, r: (0, kb + h)),
            pl.BlockSpec((t, wd), lambda h, r: (0, vb + h)),
            pl.BlockSpec((hps, None, NA_ROWS, LANES), lambda h, r: (h, _na_geometry(r)[2], 0, 0))]


def _na_fwd(p, rb, name, comm=None):
    t = p.shape[0]
    dh, nl = NA_DH, NA_ROWS * GRID_W

    hps = _na_hps()

    def body(q_ref, k_ref, v_ref, rb_ref, out_ref):
        rq = pl.program_id(1)
        is_lat, kstart, _ = _na_geometry(rq)
        start = pl.multiple_of(CTX_LEN + kstart * GRID_W, GRID_W)
        mask = _na_mask(is_lat)
        for hh in range(hps):
            cols = slice(hh * dh, (hh + 1) * dh)
            out = _na_core(q_ref[:, cols], k_ref[pl.ds(start, nl), cols], v_ref[pl.ds(start, nl), cols],
                           k_ref[0:CTX_LEN, cols], v_ref[0:CTX_LEN, cols], _na_bias(rb_ref.at[hh]), mask)
            out_ref[:, cols] = out.astype(BF16)

    res, cres = _call(
        body, name=name, grid=(NA_HEADS // hps, t // GRID_W), in_specs=_na_specs(_offsets()),
        out_specs=[pl.BlockSpec((GRID_W, hps * dh), lambda h, r: (r, h))],
        out_shape=[jax.ShapeDtypeStruct((t, _na_w()), BF16)],
        sem=("parallel", "arbitrary"), args=(p, p, p, rb), comm=comm)
    return res[0], cres


def _na_bwd(p, rb, dmix, name, comm=None):
    t = p.shape[0]
    dh, nl = NA_DH, NA_ROWS * GRID_W

    hps = _na_hps()
    wd = hps * dh
    assert ((_ret_w() + CONV_W) // dh) % hps == 0
    ob = (_ret_w() + CONV_W) // wd

    def body(q_ref, k_ref, v_ref, rb_ref, dy_ref, dq_ref, dk_ref, dv_ref, drb_ref):
        rq = pl.program_id(1)
        is_lat, kstart, base = _na_geometry(rq)
        _, _, prev_base = _na_geometry(rq - 1)
        start = pl.multiple_of(CTX_LEN + kstart * GRID_W, GRID_W)

        @pl.when(rq == 0)
        def _():
            dk_ref[...] = jnp.zeros_like(dk_ref)
            dv_ref[...] = jnp.zeros_like(dv_ref)

        @pl.when((rq == 0) | (base != prev_base))
        def _():
            drb_ref[...] = jnp.zeros_like(drb_ref)

        mask = _na_mask(is_lat)
        for hh in range(hps):
            cols = slice(hh * dh, (hh + 1) * dh)
            _, vjp = jax.vjp(lambda *a: _na_core(*a, mask), q_ref[:, cols], k_ref[pl.ds(start, nl), cols],
                             v_ref[pl.ds(start, nl), cols], k_ref[0:CTX_LEN, cols], v_ref[0:CTX_LEN, cols],
                             _na_bias(rb_ref.at[hh]))
            dq, dkl, dvl, dkc, dvc, dbias = vjp(dy_ref[:, cols])
            dq_ref[:, cols] = dq
            dk_ref[pl.ds(start, nl), cols] += dkl
            dv_ref[pl.ds(start, nl), cols] += dvl
            dk_ref[0:CTX_LEN, cols] += dkc
            dv_ref[0:CTX_LEN, cols] += dvc
            _na_dbias(dbias, drb_ref.at[hh])

    return _call(
        body, name=name, grid=(NA_HEADS // hps, t // GRID_W),
        in_specs=_na_specs(_offsets()) + [pl.BlockSpec((GRID_W, wd), lambda h, r: (r, ob + h))],
        out_specs=[pl.BlockSpec((GRID_W, wd), lambda h, r: (r, h)), pl.BlockSpec((t, wd), lambda h, r: (0, h)),
                   pl.BlockSpec((t, wd), lambda h, r: (0, h)),
                   pl.BlockSpec((hps, None, NA_ROWS, LANES), lambda h, r: (h, _na_geometry(r)[2], 0, 0))],
        out_shape=[jax.ShapeDtypeStruct((t, _na_w()), F32), jax.ShapeDtypeStruct((t, _na_w()), F32),
                   jax.ShapeDtypeStruct((t, _na_w()), F32),
                   jax.ShapeDtypeStruct((NA_HEADS, NA_ROWS, NA_ROWS, LANES), F32)],
        sem=("parallel", "arbitrary"), args=(p, p, p, rb, dmix), comm=comm)


def _rpb_select():
    sel = np.zeros((2 * NA_ROWS - 1, NA_ROWS * NA_ROWS), np.float32)
    for b in range(NA_ROWS):
        for kh in range(NA_ROWS):
            sel[b + kh, b * NA_ROWS + kh] = 1.0
    return jnp.asarray(sel)


def _rpb_rows(rpb):
    pad = jnp.pad(rpb, ((0, 0), (0, 0), (0, LANES - (2 * NA_COLS - 1))))
    rows = jnp.einsum("rk,hrc->hkc", _rpb_select(), pad, precision=lax.Precision.HIGHEST)
    return rows.reshape(NA_HEADS, NA_ROWS, NA_ROWS, LANES)


def _rpb_rows_t(drb):
    flat = drb.reshape(NA_HEADS, NA_ROWS * NA_ROWS, LANES)
    out = jnp.einsum("rk,hkc->hrc", _rpb_select(), flat, precision=lax.Precision.HIGHEST)
    return out[:, :, :2 * NA_COLS - 1]


def _assemble_dp(dqr, dkr, dvr, dgate, da, db, dnq, dnk, dnv, name):
    t = dgate.shape[0]
    tm = _tm()
    off = _offsets()
    sizes = dict(q=_ret_qk_w(), k=_ret_qk_w(), v=_ret_w(), g=_ret_w(), a=CONV_W, b=CONV_W, nq=_na_w(), nk=_na_w(), nv=_na_w())

    def body(q_ref, k_ref, v_ref, g_ref, a_ref, b_ref, nq_ref, nk_ref, nv_ref, o_ref):
        def put(n, val):
            o_ref[:, off[n]:off[n] + sizes[n]] = val.astype(BF16)

        put("q", q_ref[0] + q_ref[1])
        put("k", k_ref[0] + k_ref[1])
        put("v", v_ref[0] + v_ref[1])
        put("g", g_ref[...])
        put("a", a_ref[...])
        put("b", b_ref[...])
        put("nq", nq_ref[...])
        put("nk", nk_ref[...])
        put("nv", nv_ref[...])

    two = lambda w: pl.BlockSpec((2, tm, w), lambda i: (0, i, 0))
    one = lambda w: pl.BlockSpec((tm, w), lambda i: (i, 0))
    return pl.pallas_call(
        body, name=name, grid=(t // tm,),
        in_specs=[two(sizes["q"]), two(sizes["k"]), two(sizes["v"]), one(sizes["g"]), one(CONV_W), one(CONV_W),
                  one(_na_w()), one(_na_w()), one(_na_w())],
        out_specs=one(_d_in()), out_shape=jax.ShapeDtypeStruct((t, _d_in()), BF16),
        compiler_params=_params(("parallel",)))(dqr, dkr, dvr, dgate, da, db, dnq, dnk, dnv)


def _adamw(w, m, v, gs, name, comm=None):
    nl, r, c = w.shape
    stacked = not isinstance(gs, (list, tuple))
    if stacked:
        gs = [gs]
    assert stacked or len(gs) == nl
    g_n = gs[0].shape[-3]
    block_bytes = 2 * 1024 * 1024
    rows = min(block_bytes // (4 * c), block_bytes // (g_n * c * gs[0].dtype.itemsize))
    tr = _tile(r, max(2 * SUBLANES, rows // (2 * SUBLANES) * (2 * SUBLANES)), 2 * SUBLANES)
    nt = r // tr
    c1 = 1.0 - ADAM_B1 ** ADAM_STEP
    c2 = 1.0 - ADAM_B2 ** ADAM_STEP

    def body(w_ref, m_ref, v_ref, *rest):
        g_refs, (go_ref, d_ref, mo_ref, vo_ref) = rest[:len(gs)], rest[len(gs):]
        layer = pl.program_id(0)
        for ll in range(len(gs)):
            @pl.when(jnp.logical_or(stacked, layer == ll))
            def _():
                g_ref = g_refs[ll]
                g = g_ref[0].astype(F32)
                for j in range(1, g_n):
                    g = g + g_ref[j].astype(F32)
                mn = ADAM_B1 * m_ref[...] + (1.0 - ADAM_B1) * g
                vn = ADAM_B2 * v_ref[...] + (1.0 - ADAM_B2) * (g * g)
                m_hat = mn / c1
                v_hat = vn / c2
                go_ref[...] = g
                d_ref[...] = -ADAM_LR * (m_hat / (jnp.sqrt(v_hat) + ADAM_EPS) + ADAM_WD * w_ref[...])
                mo_ref[...] = mn
                vo_ref[...] = vn

    def g_spec(ll):
        if stacked:
            return pl.BlockSpec((None, g_n, tr, c), lambda l, i: (l, 0, i, 0))
        return pl.BlockSpec((g_n, tr, c), lambda l, i: (0, jnp.where(l == ll, i, jnp.where(l < ll, 0, nt - 1)), 0))

    blk = pl.BlockSpec((None, tr, c), lambda l, i: (l, i, 0))
    sds = jax.ShapeDtypeStruct((nl, r, c), F32)
    return _call(
        body, name=name, grid=(nl, nt),
        in_specs=[blk, blk, blk] + [g_spec(ll) for ll in range(len(gs))],
        out_specs=[blk, blk, blk, blk], out_shape=[sds, sds, sds, sds],
        sem=("arbitrary", "arbitrary"), args=(w, m, v, *gs), comm=comm)


def _sum_devices(g, name):
    _, r, c = g.shape
    tr = _tile(r, 512, SUBLANES)

    def body(g_ref, o_ref):
        acc = g_ref[0]
        for j in range(1, N_DEV):
            acc = acc + g_ref[j]
        o_ref[...] = acc

    return pl.pallas_call(body, name=name, grid=(r // tr,), in_specs=[pl.BlockSpec((N_DEV, tr, c), lambda i: (0, i, 0))],
                          out_specs=pl.BlockSpec((tr, c), lambda i: (i, 0)), out_shape=jax.ShapeDtypeStruct((r, c), F32),
                          compiler_params=_params(("parallel",)))(g)


def _ada_fwd(c16, w_ada, b_shard, name):
    nl, d, cs = w_ada.shape
    tk = _tile(d, 512, LANES)
    nk = d // tk

    def body(c_ref, w_ref, b_ref, o_ref):
        kk = pl.program_id(1)

        @pl.when(kk == 0)
        def _():
            o_ref[...] = jnp.broadcast_to(b_ref[...], o_ref.shape)

        o_ref[...] += _dg(_silu(c_ref[...]), w_ref[...], 1, 0)

    return pl.pallas_call(
        body, name=name, grid=(nl, nk),
        in_specs=[pl.BlockSpec((16, tk), lambda l, kk: (0, kk)), pl.BlockSpec((None, tk, cs), lambda l, kk: (l, kk, 0)),
                  pl.BlockSpec((None, 1, cs), lambda l, kk: (l, 0, 0))],
        out_specs=pl.BlockSpec((None, 16, cs), lambda l, kk: (l, 0, 0)),
        out_shape=jax.ShapeDtypeStruct((nl, 16, cs), F32),
        compiler_params=_params(("parallel", "arbitrary")))(c16, w_ada, b_shard)


def _ada_bwd(c16, dm16, w_ada, name):
    nl, d, cs = w_ada.shape
    td = _tile(d, 512, LANES)

    def body(c_ref, dm_ref, w_ref, gw_ref, dc_ref):
        cv = c_ref[...]
        s, vjp = jax.vjp(_silu, cv)
        gw_ref[...] = _dg(s, dm_ref[...], 0, 0)
        ds = _dg(dm_ref[...], w_ref[...], 1, 1)
        dc_ref[...] = vjp(ds)[0]

    return pl.pallas_call(
        body, name=name, grid=(nl, d // td),
        in_specs=[pl.BlockSpec((16, td), lambda l, i: (0, i)), pl.BlockSpec((None, 16, cs), lambda l, i: (l, 0, 0)),
                  pl.BlockSpec((None, td, cs), lambda l, i: (l, i, 0))],
        out_specs=[pl.BlockSpec((None, td, cs), lambda l, i: (l, i, 0)), pl.BlockSpec((None, 16, td), lambda l, i: (l, 0, i))],
        out_shape=[jax.ShapeDtypeStruct((nl, d, cs), F32), jax.ShapeDtypeStruct((nl, 16, d), F32)],
        compiler_params=_params(("parallel", "parallel")))(c16, dm16, w_ada)


def _pack_rows(shape):
    n = int(np.prod(shape))
    return SUBLANES * (-(-n // (LANES * SUBLANES)))


def _pack(arrays, row_align):
    parts = []
    for a in arrays:
        flat = a.reshape(-1).astype(F32)
        rows = _pack_rows(a.shape)
        parts.append(jnp.pad(flat, (0, rows * LANES - flat.shape[0])).reshape(rows, LANES))
    total = sum(p.shape[0] for p in parts)
    extra = -total % row_align
    if extra:
        parts.append(jnp.zeros((extra, LANES), F32))
    return jnp.concatenate(parts, axis=0)


def _unpack(packed, shapes):
    out, r = [], 0
    for s in shapes:
        rows = _pack_rows(s)
        out.append(packed[r:r + rows].reshape(-1)[:int(np.prod(s))].reshape(s))
        r += rows
    return out


def _rope_tables():
    half, nf = RET_DK // 2, RET_DK // 4
    pos = jnp.arange(SEQ)
    row = (pos // GRID_W).astype(F32)
    col = (pos % GRID_W).astype(F32)
    inv = ROPE_BASE ** (-jnp.arange(nf, dtype=F32) / nf)
    ar, ac = row[:, None] * inv[None, :], col[:, None] * inv[None, :]
    cos = jnp.concatenate([jnp.cos(ar), jnp.cos(ar), jnp.cos(ac), jnp.cos(ac)], axis=-1)
    sin = jnp.concatenate([-jnp.sin(ar), jnp.sin(ar), -jnp.sin(ac), jnp.sin(ac)], axis=-1)
    cos = jnp.concatenate([jnp.ones((CTX_LEN, RET_DK), F32), cos], axis=0)
    sin = jnp.concatenate([jnp.zeros((CTX_LEN, RET_DK), F32), sin], axis=0)
    return cos, sin


def _layer_fwd(l, nl, x, mod4, w, cst, shards, full):
    n = lambda s: f"l{l}_{s}"
    d = D_MODEL

    def gather(keys):
        return _Gather([shards[k] for k in keys]) if keys else None

    w["w_in"] = _cols_from_shards(full[("w_in", l)], n("w_in_cols"))
    h1 = _normmod_fwd(x, w["norm1_g"], mod4, 0, n("norm1"))
    keys = [("w_out", l), ("ffn_down", l)] if l == 0 else [("ffn_up", l)]
    p, got = _mm(h1, w["w_in"], n("proj_in"), comm=gather(keys))
    full.update(zip(keys, got))
    o2, states = _ret_fwd(p, cst["cos"], cst["sin"], w["ret_decay"], cst["order"], n("ret_fwd"))
    ret_out = _ggn_fwd(o2, p, w["ret_gn_g"], n("ret_gn"))
    u2, conv_out = _conv_fwd(p, w["conv_dw_w"], w["conv_dw_b"], w["conv_ln_g"], w["conv_ln_b"], w["conv_pw"], n("conv_fwd"))
    keys = [("ffn_up", l)] if l == 0 else [("ffn_down", l)]
    na_out, got = _na_fwd(p, w["rb"], n("na_fwd"), comm=gather(keys))
    full.update(zip(keys, got))
    mix = jnp.concatenate([ret_out, conv_out, na_out], axis=1)
    w["w_out"] = full[("w_out", l)].reshape(_d_mix(), d)
    g1, _ = _mm(mix, w["w_out"], n("proj_out"))
    x1 = _gate_res_fwd(x, g1, mod4, 2, n("res1"))
    h2 = _normmod_fwd(x1, w["norm2_g"], mod4, 1, n("norm2"))
    w["ffn_up"] = full[("ffn_up", l)]
    keys = [("w_in", l + 1), ("w_out", l + 1)] if l + 1 < nl else []
    u, got = _mm(h2, w["ffn_up"], n("ffn_up"), b3=True, comm=gather(keys))
    full.update(zip(keys, got or []))
    a = _ffn_act_fwd(u, w["ffn_dw_w"], w["ffn_dw_b"], n("ffn_act"))
    w["ffn_down"] = full[("ffn_down", l)].reshape(D_FF, d)
    f, _ = _mm(a, w["ffn_down"], n("ffn_down"))
    x2 = _gate_res_fwd(x1, f, mod4, 5, n("res2"))
    saved = dict(x=x, h1=h1, p=p, o2=o2, states=states, u2=u2, mix=mix, g1=g1, x1=x1, h2=h2, u=u, a=a, f=f)
    return x2, saved


def _layer_bwd(l, dx2, s, mod4, w, cst, pending):
    n = lambda t: f"l{l}_{t}"
    d = D_MODEL
    nj = D_FF // _ffn_tc()
    done = {}

    def exchange(named):
        return _Exchange([a for _, a in named]) if named else None

    def collect(named, got):
        done.update({k: g for (k, _), g in zip(named, got or [])})

    dfg, dg2 = _gate_res_bwd(dx2, s["f"], mod4, 5, n("res2_bwd"))
    da, got = _mm(dfg, w["ffn_down"], n("ffn_down_dx"), tb=True, comm=exchange(pending[:1]))
    collect(pending[:1], got)
    d_ffn_down, got = _mm(_transpose_bf16(s["a"], n("act_t")), dfg, n("ffn_down_dw"), out_dtype=BF16, tm_max=DW_TM,
                          comm=exchange(pending[1:]))
    collect(pending[1:], got)
    dcv, dcg = _ffn_act_bwd1(s["u"], da, w["ffn_dw_w"], w["ffn_dw_b"], n("ffn_act_bwd"))
    duv, dwv, dbv = _dwconv_bwd(dcv, s["u"], w["ffn_dw_w"], 0, n("ffn_dw_bwd_val"))
    dug, dwg, dbg = _dwconv_bwd(dcg, s["u"], w["ffn_dw_w"], nj, n("ffn_dw_bwd_gate"))
    du = jnp.concatenate([duv, dug], axis=1)
    d_ffn_dw_w = jnp.concatenate([dwv, dwg], axis=1)
    d_ffn_dw_b = jnp.concatenate([dbv, dbg], axis=1)[0]
    named = [(("ffn_down", l), d_ffn_down.reshape(N_DEV, D_FF // N_DEV, d))]
    dh2, got = _mm(du, w["ffn_up"], n("ffn_up_dx"), tb=True, b3=True, comm=exchange(named))
    collect(named, got)
    d_ffn_up, _ = _mm(_transpose_bf16(s["h2"], n("h2_t")), du, n("ffn_up_dw"), out_dtype=BF16, tm_max=DW_TM,
                      o_cs=2 * D_FF // N_DEV)
    (dx1, dn2, dsh2, dsc2), _ = _normmod_bwd(s["x1"], w["norm2_g"], mod4, 1, dh2, dx2, n("norm2_bwd"))
    dgg, dg1 = _gate_res_bwd(dx1, s["g1"], mod4, 2, n("res1_bwd"))
    dmix, _ = _mm(dgg, w["w_out"], n("proj_out_dx"), tb=True)
    d_w_out, _ = _mm(_transpose_bf16(s["mix"], n("mix_t")), dgg, n("proj_out_dw"), out_dtype=BF16, tm_max=DW_TM)
    do, dgate, dgn = _ggn_bwd(s["o2"], s["p"], w["ret_gn_g"], dmix, n("ret_gn_bwd"))
    dqr, dkr, dvr, ddec = _ret_bwd(s["p"], cst["cos"], cst["sin"], w["ret_decay"], cst["order"], s["states"], do, n("ret_bwd"))
    du2, dlng, dlnb, dpw = _conv_bwd1(s["u2"], dmix, w["conv_ln_g"], w["conv_ln_b"], w["conv_pw"], n("conv_bwd1"))
    named = [(("w_out", l), d_w_out.reshape(N_DEV, _d_mix() // N_DEV, d))]
    (dca, dcb, ddww, ddwb), got = _conv_bwd2(du2, s["p"], w["conv_dw_w"], n("conv_bwd2"), comm=exchange(named))
    collect(named, got)
    named = [(("ffn_up", l), d_ffn_up)]
    (dnq, dnk, dnv, drb), got = _na_bwd(s["p"], w["rb"], dmix, n("na_bwd"), comm=exchange(named))
    collect(named, got)
    dp = _assemble_dp(dqr, dkr, dvr, dgate, dca, dcb, dnq, dnk, dnv, n("dproj"))
    d_w_in, _ = _mm(_transpose_bf16(s["h1"], n("h1_t")), dp, n("proj_in_dw"), out_dtype=BF16, tm_max=DW_TM,
                    o_cs=_d_in() // N_DEV)
    half = d // 2
    halves = [(("w_in", l, 0), d_w_in[:, :half]), (("w_in", l, 1), d_w_in[:, half:])]
    here = halves if l == 0 else []
    dh1, got = _mm(dp, w["w_in"], n("proj_in_dx"), tb=True, comm=exchange(here[:1]))
    collect(here[:1], got)
    (dx, dn1, dsh1, dsc1), got = _normmod_bwd(s["x"], w["norm1_g"], mod4, 0, dh1, dx1, n("norm1_bwd"), comm=exchange(here[1:]))
    collect(here[1:], got)
    dmod = jnp.concatenate([dsh1, dsc1, dg1, dsh2, dsc2, dg2], axis=1)
    small = dict(norm1_g=dn1[0], ret_decay=ddec[:, :, 0, 0], ret_gn_g=dgn[0], conv_dw_w=ddww, conv_dw_b=ddwb[0],
                 conv_ln_g=dlng[0], conv_ln_b=dlnb[0], conv_pw=dpw, na_rpb=_rpb_rows_t(drb), norm2_g=dn2[0],
                 ffn_dw_w=d_ffn_dw_w, ffn_dw_b=d_ffn_dw_b)
    return dx, dmod, done, small, ([] if l == 0 else halves)


def _d_mix():
    return _ret_w() + CONV_W + _na_w()


_SMALL = ["c_ctx", "b_ada", "norm1_g", "ret_decay", "ret_gn_g", "conv_dw_w", "conv_dw_b", "conv_ln_g", "conv_ln_b",
          "conv_pw", "na_rpb", "norm2_g", "ffn_dw_w", "ffn_dw_b", "final_g"]
_SMALL_SHARD_AXIS = {"conv_dw_w": 2, "conv_pw": 1, "ffn_dw_w": 2}


def kernel(x, c, ctx, c_ctx, w_ada, b_ada, norm1_g, w_in, ret_decay, ret_gn_g, conv_dw_w, conv_dw_b, conv_ln_g, conv_ln_b, conv_pw, na_rpb, w_out, norm2_g, ffn_up, ffn_dw_w, ffn_dw_b, ffn_down, final_g, loss_target, m_c_ctx, m_w_ada, m_b_ada, m_norm1_g, m_w_in, m_ret_decay, m_ret_gn_g, m_conv_dw_w, m_conv_dw_b, m_conv_ln_g, m_conv_ln_b, m_conv_pw, m_na_rpb, m_w_out, m_norm2_g, m_ffn_up, m_ffn_dw_w, m_ffn_dw_b, m_ffn_down, m_final_g, v_c_ctx, v_w_ada, v_b_ada, v_norm1_g, v_w_in, v_ret_decay, v_ret_gn_g, v_conv_dw_w, v_conv_dw_b, v_conv_ln_g, v_conv_ln_b, v_conv_pw, v_na_rpb, v_w_out, v_norm2_g, v_ffn_up, v_ffn_dw_w, v_ffn_dw_b, v_ffn_down, v_final_g):
    d, nl = D_MODEL, DEPTH
    cs = 6 * d // N_DEV
    me = _my_index()
    weights = dict(c_ctx=c_ctx, w_ada=w_ada, b_ada=b_ada, norm1_g=norm1_g, w_in=w_in, ret_decay=ret_decay, ret_gn_g=ret_gn_g,
                   conv_dw_w=conv_dw_w, conv_dw_b=conv_dw_b, conv_ln_g=conv_ln_g, conv_ln_b=conv_ln_b, conv_pw=conv_pw,
                   na_rpb=na_rpb, w_out=w_out, norm2_g=norm2_g, ffn_up=ffn_up, ffn_dw_w=ffn_dw_w, ffn_dw_b=ffn_dw_b,
                   ffn_down=ffn_down, final_g=final_g)
    mom = dict(c_ctx=m_c_ctx, w_ada=m_w_ada, b_ada=m_b_ada, norm1_g=m_norm1_g, w_in=m_w_in, ret_decay=m_ret_decay,
               ret_gn_g=m_ret_gn_g, conv_dw_w=m_conv_dw_w, conv_dw_b=m_conv_dw_b, conv_ln_g=m_conv_ln_g,
               conv_ln_b=m_conv_ln_b, conv_pw=m_conv_pw, na_rpb=m_na_rpb, w_out=m_w_out, norm2_g=m_norm2_g,
               ffn_up=m_ffn_up, ffn_dw_w=m_ffn_dw_w, ffn_dw_b=m_ffn_dw_b, ffn_down=m_ffn_down, final_g=m_final_g)
    var = dict(c_ctx=v_c_ctx, w_ada=v_w_ada, b_ada=v_b_ada, norm1_g=v_norm1_g, w_in=v_w_in, ret_decay=v_ret_decay,
               ret_gn_g=v_ret_gn_g, conv_dw_w=v_conv_dw_w, conv_dw_b=v_conv_dw_b, conv_ln_g=v_conv_ln_g,
               conv_ln_b=v_conv_ln_b, conv_pw=v_conv_pw, na_rpb=v_na_rpb, w_out=v_w_out, norm2_g=v_norm2_g,
               ffn_up=v_ffn_up, ffn_dw_w=v_ffn_dw_w, ffn_dw_b=v_ffn_dw_b, ffn_down=v_ffn_down, final_g=v_final_g)

    big_names = ["w_in", "w_out", "ffn_up", "ffn_down"]
    shards = {(nm, l): _cast_bf16(weights[nm][l], f"cast_{nm}{l}") for l in range(nl) for nm in big_names}
    small_sharded = _pack([conv_dw_w, conv_pw, ffn_dw_w], SUBLANES)
    c_rows = jnp.pad(c, ((0, SUBLANES - 1), (0, 0)))
    gathered = _run_comm(_Gather([c_rows, small_sharded, shards[("w_in", 0)]]), "gather_first")
    c_all = gathered[0][:, 0, :]
    full = {("w_in", 0): gathered[2]}
    sm = [_unpack(gathered[1][j], [conv_dw_w.shape, conv_pw.shape, ffn_dw_w.shape]) for j in range(N_DEV)]
    full_conv_dw_w = jnp.concatenate([s[0] for s in sm], axis=2)
    full_conv_pw = jnp.concatenate([s[1] for s in sm], axis=1)
    full_ffn_dw_w = jnp.concatenate([s[2] for s in sm], axis=2)

    c16 = jnp.concatenate([c_all, jnp.broadcast_to(c_ctx[None, :], (N_DEV, d))], axis=0)
    b_shard = lax.dynamic_slice_in_dim(b_ada, me * cs, cs, axis=1)[:, None, :]
    m_shard = _ada_fwd(c16, w_ada, b_shard, "ada_fwd")
    m_all = _run_comm(_Gather([m_shard.reshape(nl * 16, cs)]), "gather_mod")[0]
    m_full = m_all.reshape(N_DEV, nl, 16, cs).transpose(1, 2, 0, 3).reshape(nl, 16, 6 * d)
    m_lat = lax.dynamic_index_in_dim(m_full, me, axis=1, keepdims=False)
    mod = jnp.stack([m_full[:, N_DEV], m_lat], axis=1).reshape(nl, 2, 6, 1, d)

    cos, sin = _rope_tables()
    cst = dict(cos=cos, sin=sin, order=_chunk_order())
    layer_w = []
    for l in range(nl):
        layer_w.append(dict(
            norm1_g=norm1_g[l][None], norm2_g=norm2_g[l][None], ret_decay=ret_decay[l], ret_gn_g=ret_gn_g[l][None],
            conv_dw_w=full_conv_dw_w[l], conv_dw_b=conv_dw_b[l][None], conv_ln_g=conv_ln_g[l][None],
            conv_ln_b=conv_ln_b[l][None], conv_pw=full_conv_pw[l], rb=_rpb_rows(na_rpb[l]),
            ffn_dw_w=full_ffn_dw_w[l], ffn_dw_b=ffn_dw_b[l][None]))

    xs = jnp.concatenate([ctx[0], x[0]], axis=0)
    saved = []
    for l in range(nl):
        xs, sv = _layer_fwd(l, nl, xs, mod[l], layer_w[l], cst, shards, full)
        saved.append(sv)
    loss_tile, dxs, dfinal = _loss_head(xs, final_g[None], loss_target[0], "loss_head")
    loss = lax.psum(loss_tile[0, 0], ("x", "y", "c"))

    dmods, smalls = [None] * nl, [None] * nl
    exchanged, pending = {}, []
    for l in reversed(range(nl)):
        dxs, dmods[l], done, smalls[l], pending = _layer_bwd(l, dxs, saved[l], mod[l], layer_w[l], cst, pending)
        exchanged.update(done)
    grad_x = dxs[CTX_LEN:][None]

    dm_mine = jnp.stack(dmods).reshape(nl * 2, 6 * d)
    dm_rows = jnp.pad(dm_mine, ((0, SUBLANES - nl * 2), (0, 0)))
    dm_all = _run_comm(_Gather([dm_rows]), "gather_dmod")[0][:, :nl * 2].reshape(N_DEV, nl, 2, 6 * d)
    dm16_full = jnp.concatenate([dm_all[:, :, 1].transpose(1, 0, 2), dm_all[:, :, 0].transpose(1, 0, 2)], axis=1)
    dm16 = lax.dynamic_slice_in_dim(dm16_full, me * cs, cs, axis=2)
    g_w_ada, dc16 = _ada_bwd(c16, dm16, w_ada, "ada_bwd")

    small_grads = dict(
        c_ctx=jnp.sum(dc16[:, N_DEV:], axis=(0, 1)),
        b_ada=jnp.sum(jnp.stack(dmods).reshape(nl, 2, 6 * d), axis=1),
        final_g=dfinal[0])
    for nm in _SMALL:
        if nm not in small_grads:
            small_grads[nm] = jnp.stack([smalls[l][nm] for l in range(nl)])
    shapes_full = [small_grads[nm].shape for nm in _SMALL]
    packed = _pack([small_grads[nm] for nm in _SMALL], 512)

    out_big = {}
    out_big["w_ada"], _ = _adamw(w_ada, m_w_ada, v_w_ada, g_w_ada[:, None], "adamw_w_ada")
    for nm in ["ffn_up", "ffn_down", "w_out"]:
        out_big[nm], _ = _adamw(weights[nm], mom[nm], var[nm], [exchanged[(nm, l)] for l in range(nl)], f"adamw_{nm}")
    halves = lambda a: a.reshape(2 * nl, d // 2, a.shape[2])
    res, _ = _adamw(halves(w_in), halves(m_w_in), halves(v_w_in),
                    [exchanged[("w_in", l, i)] for l in range(nl) for i in range(2)], "adamw_w_in")
    out_big["w_in"] = [r.reshape(w_in.shape) for r in res]

    summed = _sum_devices(_run_comm(_Gather([packed]), "gather_small_grads")[0], "sum_small_grads")
    g_small = dict(zip(_SMALL, _unpack(summed, shapes_full)))
    for nm, ax in _SMALL_SHARD_AXIS.items():
        n_sh = weights[nm].shape[ax]
        g_small[nm] = lax.dynamic_slice_in_dim(g_small[nm], me * n_sh, n_sh, axis=ax)
    shapes_own = [weights[nm].shape for nm in _SMALL]
    pk = lambda src: _pack([src[nm] for nm in _SMALL], 2 * SUBLANES)[None]
    res_small, _ = _adamw(pk(weights), pk(mom), pk(var), pk(g_small)[:, None], "adamw_small")
    out_small = [dict(zip(_SMALL, _unpack(r[0], shapes_own))) for r in res_small]

    names = ["c_ctx", "w_ada", "b_ada", "norm1_g", "w_in", "ret_decay", "ret_gn_g", "conv_dw_w", "conv_dw_b", "conv_ln_g",
             "conv_ln_b", "conv_pw", "na_rpb", "w_out", "norm2_g", "ffn_up", "ffn_dw_w", "ffn_dw_b", "ffn_down", "final_g"]
    outs = [loss, grad_x]
    for kind in range(4):
        for nm in names:
            outs.append(out_big[nm][kind] if nm in out_big else out_small[kind][nm])
    return tuple(outs)
```

```python
import functools
import math

import numpy as np
import jax
import jax.numpy as jnp
from jax import lax
from jax.experimental import pallas as pl
from jax.experimental.pallas import tpu as pltpu

D_MODEL = 2048
SEQ = 4096
DEPTH = 2
GRID_W = 64
CTX_LEN = 256
RET_HEADS = 4
RET_DK = 128
RET_DV = 256
RET_CHUNK = 128
CONV_W = 512
CONV_K = 31
NA_HEADS = 4
NA_DH = 128
NA_ROWS = 8
NA_COLS = 16
D_FF = 5632
FFN_K = 3
ROPE_BASE = 10000.0
EPS = 1e-6
ADAM_LR = 0.001
ADAM_B1 = 0.9
ADAM_B2 = 0.999
ADAM_EPS = 1e-08
ADAM_WD = 0.01
ADAM_STEP = 10
N_DEV = 8

LANES = 128
SUBLANES = 8
VMEM_LIMIT = 56 * 1024 * 1024
ROW_CHUNK = 16

F32 = jnp.float32
BF16 = jnp.bfloat16
MESH = pl.DeviceIdType.MESH
NEG = -1e30


def _ret_qk_w():
    return RET_HEADS * RET_DK


def _ret_w():
    return RET_HEADS * RET_DV


def _na_w():
    return NA_HEADS * NA_DH


def _d_in():
    return 2 * _ret_qk_w() + 2 * _ret_w() + 2 * CONV_W + 3 * _na_w()


def _offsets():
    sizes = [_ret_qk_w(), _ret_qk_w(), _ret_w(), _ret_w(), CONV_W, CONV_W, _na_w(), _na_w(), _na_w()]
    offs = [0]
    for s in sizes[:-1]:
        offs.append(offs[-1] + s)
    return dict(zip(["q", "k", "v", "g", "a", "b", "nq", "nk", "nv"], offs))


def _t_rows():
    return CTX_LEN + SEQ


def _tm():
    return CTX_LEN


def _params(sem=None):
    kw = dict(vmem_limit_bytes=VMEM_LIMIT)
    if sem is not None:
        kw["dimension_semantics"] = sem
    return pltpu.CompilerParams(**kw)


def _tile(n, pref, align):
    best = None
    for t in range(align, min(n, pref) + 1, align):
        if n % t == 0:
            best = t
    return best if best is not None else n


def _dg(a, b, ca, cb):
    return lax.dot_general(a.astype(BF16), b.astype(BF16), (((ca,), (cb,)), ((), ())), preferred_element_type=F32)


@jax.custom_vjp
def dot_nn(a, b):
    return _dg(a, b, 1, 0)


dot_nn.defvjp(lambda a, b: (_dg(a, b, 1, 0), (a, b)),
              lambda r, g: (_dg(g, r[1], 1, 1), _dg(r[0], g, 0, 0)))


@jax.custom_vjp
def dot_nt(a, b):
    return _dg(a, b, 1, 1)


dot_nt.defvjp(lambda a, b: (_dg(a, b, 1, 1), (a, b)),
              lambda r, g: (_dg(g, r[1], 1, 0), _dg(g, r[0], 0, 0)))


@jax.custom_vjp
def dot_tn(a, b):
    return _dg(a, b, 0, 0)


dot_tn.defvjp(lambda a, b: (_dg(a, b, 0, 0), (a, b)),
              lambda r, g: (_dg(r[1], g, 1, 1), _dg(r[0], g, 1, 0)))


def _sigmoid(x):
    return 1.0 / (1.0 + jnp.exp(-x))


def _silu(x):
    return x * _sigmoid(x)


def _my_pos():
    return lax.axis_index("x"), lax.axis_index("y"), lax.axis_index("c")


def _my_index():
    x, y, c = _my_pos()
    return 4 * x + 2 * y + c


_ANY = pl.BlockSpec(memory_space=pl.ANY)


class _Gather:
    def __init__(self, arrays):
        self.arrays = list(arrays)
        n = len(self.arrays)
        self.out_shape = [jax.ShapeDtypeStruct((N_DEV,) + a.shape, a.dtype) for a in self.arrays]
        self.scratch = [pltpu.SemaphoreType.DMA((n, 7)), pltpu.SemaphoreType.DMA((n, 7)), pltpu.SemaphoreType.DMA((n,))]

    def _plan(self, xs, outs, sems):
        send_sems, recv_sems, local_sems = sems
        n = len(self.arrays)
        x, y, c = _my_pos()
        me, sibling = (x, y, c), (x, y, 1 - c)
        chips = [(1 - x, y), (x, 1 - y), (1 - x, 1 - y)]

        def slot(a, p):
            return outs[a].at[4 * p[0] + 2 * p[1] + p[2]]

        def copy(a, k, block, to, src=None):
            return pltpu.make_async_remote_copy(
                src_ref=slot(a, block) if src is None else src, dst_ref=slot(a, block),
                send_sem=send_sems.at[a, k], recv_sem=recv_sems.at[a, k], device_id=to, device_id_type=MESH)

        mine = [pltpu.make_async_copy(xs[a], slot(a, me), local_sems.at[a]) for a in range(n)]
        first = []
        for a in range(n):
            first.append(copy(a, 0, me, sibling, src=xs[a]))
            first += [copy(a, 1 + j, me, (*chip, c), src=xs[a]) for j, chip in enumerate(chips)]
        return n, c, me, sibling, chips, copy, mine, first

    def start(self, xs, outs, sems):
        _, _, _, _, _, _, mine, first = self._plan(xs, outs, sems)
        for m in mine:
            m.start()
        for cp in first:
            cp.start()

    def finish(self, xs, outs, sems):
        n, c, me, sibling, chips, copy, mine, first = self._plan(xs, outs, sems)
        passed = []
        for a in range(n):
            for j, chip in enumerate(chips):
                copy(a, 1 + j, (*chip, c), me).wait_recv()
                p = copy(a, 4 + j, (*chip, c), sibling)
                p.start()
                passed.append(p)
        for a in range(n):
            copy(a, 0, sibling, me).wait_recv()
            for j, chip in enumerate(chips):
                copy(a, 4 + j, (*chip, 1 - c), me).wait_recv()
        for cp in first + passed:
            cp.wait_send()
        for m in mine:
            m.wait()


class _Exchange:
    def __init__(self, arrays):
        self.arrays = list(arrays)
        n = len(self.arrays)
        self.out_shape = [jax.ShapeDtypeStruct(a.shape, a.dtype) for a in self.arrays]
        self.scratch = [pltpu.SemaphoreType.DMA((n, 7)), pltpu.SemaphoreType.DMA((n, 7)), pltpu.SemaphoreType.DMA((n,))]

    def _plan(self, xs, outs, sems):
        send_sems, recv_sems, local_sems = sems
        x, y, c = _my_pos()
        me = 4 * x + 2 * y + c
        mine, sends, recvs = [], [], []
        for a in range(len(self.arrays)):
            mine.append(pltpu.make_async_copy(xs[a].at[me], outs[a].at[me], local_sems.at[a]))
        for k in range(1, N_DEV):
            px = 1 - x if (k >> 2) & 1 else x
            py = 1 - y if (k >> 1) & 1 else y
            pc = 1 - c if k & 1 else c
            peer = 4 * px + 2 * py + pc
            for a in range(len(self.arrays)):
                sends.append(pltpu.make_async_remote_copy(
                    src_ref=xs[a].at[peer], dst_ref=outs[a].at[me], send_sem=send_sems.at[a, k - 1],
                    recv_sem=recv_sems.at[a, k - 1], device_id=(px, py, pc), device_id_type=MESH))
                recvs.append(pltpu.make_async_remote_copy(
                    src_ref=xs[a].at[me], dst_ref=outs[a].at[peer], send_sem=send_sems.at[a, k - 1],
                    recv_sem=recv_sems.at[a, k - 1], device_id=(px, py, pc), device_id_type=MESH))
        return mine, sends, recvs

    def start(self, xs, outs, sems):
        mine, sends, _ = self._plan(xs, outs, sems)
        for m in mine:
            m.start()
        for s in sends:
            s.start()

    def finish(self, xs, outs, sems):
        mine, sends, recvs = self._plan(xs, outs, sems)
        for r in recvs:
            r.wait_recv()
        for s in sends:
            s.wait_send()
        for m in mine:
            m.wait()


def _run_comm(comm, name):
    n = len(comm.arrays)

    def body(*refs):
        xs, outs, sems = refs[:n], refs[n:2 * n], refs[2 * n:]
        comm.start(xs, outs, sems)
        comm.finish(xs, outs, sems)

    return pl.pallas_call(body, name=name, out_shape=comm.out_shape, in_specs=[_ANY] * n, out_specs=[_ANY] * n,
                          scratch_shapes=comm.scratch)(*comm.arrays)


_HBM = pl.BlockSpec(memory_space=pltpu.HBM)
_SEMS = pl.BlockSpec(memory_space=pltpu.SEMAPHORE)
_EFFECT = pltpu.SideEffectType.DATAFLOW_SIDE_EFFECTING


def _own_slot(x, gathering, name):
    shape = (N_DEV,) + x.shape if gathering else x.shape

    def body(x_ref, o_ref, sem):
        me = _my_index()
        cp = pltpu.make_async_copy(x_ref if gathering else x_ref.at[me], o_ref.at[me], sem)
        cp.start()
        cp.wait()

    return pl.pallas_call(body, name=name, out_shape=jax.ShapeDtypeStruct(shape, x.dtype), in_specs=[_ANY],
                          out_specs=_ANY, scratch_shapes=[pltpu.SemaphoreType.DMA])(x)


def _split_plan(x_ref, land_ref, send_sems, recv_sems, gathering):
    x, y, c = _my_pos()
    me = 4 * x + 2 * y + c
    sends, recvs = [], []
    for k in range(1, N_DEV):
        px = 1 - x if (k >> 2) & 1 else x
        py = 1 - y if (k >> 1) & 1 else y
        pc = 1 - c if k & 1 else c
        peer = 4 * px + 2 * py + pc
        mine, theirs = (x_ref, x_ref) if gathering else (x_ref.at[peer], x_ref.at[me])
        sends.append(pltpu.make_async_remote_copy(
            src_ref=mine, dst_ref=land_ref.at[me], send_sem=send_sems.at[k - 1], recv_sem=recv_sems.at[k - 1],
            device_id=(px, py, pc), device_id_type=MESH))
        recvs.append(pltpu.make_async_remote_copy(
            src_ref=theirs, dst_ref=land_ref.at[peer], send_sem=send_sems.at[k - 1], recv_sem=recv_sems.at[k - 1],
            device_id=(px, py, pc), device_id_type=MESH))
    return sends, recvs


def _split_start(x, gathering, name):
    land = _own_slot(x, gathering, name + "_own")

    def body(x_ref, land_ref, send_sems, recv_sems, x_thru, land_thru):
        sends, _ = _split_plan(x_ref, land_ref, send_sems, recv_sems, gathering)
        for s in sends:
            s.start()

    sems = pltpu.SemaphoreType.DMA((N_DEV - 1,))
    send_sems, recv_sems, x_thru, land_thru = pl.pallas_call(
        body, name=name, out_shape=(sems, sems, pltpu.HBM(x.shape, x.dtype), pltpu.HBM(land.shape, land.dtype)),
        in_specs=(_HBM, _HBM), out_specs=(_SEMS, _SEMS, _HBM, _HBM), input_output_aliases={0: 2, 1: 3},
        compiler_params=pltpu.CompilerParams(has_side_effects=_EFFECT),
    )(pltpu.with_memory_space_constraint(x, pltpu.HBM), pltpu.with_memory_space_constraint(land, pltpu.HBM))
    return send_sems, recv_sems, x_thru, land_thru, gathering


def _split_wait(handle, after, name):
    send_sems, recv_sems, x_thru, land_thru, gathering = handle

    def body(x_ref, land_ref, send_sems, recv_sems, after_ref, x_dead, got_ref):
        sends, recvs = _split_plan(x_ref, land_ref, send_sems, recv_sems, gathering)
        for s in sends:
            s.wait_send()
        for r in recvs:
            r.wait_recv()

    return pl.pallas_call(
        body, name=name, out_shape=(pltpu.HBM(x_thru.shape, x_thru.dtype), pltpu.HBM(land_thru.shape, land_thru.dtype)),
        in_specs=(_HBM, _HBM, _SEMS, _SEMS, _ANY), out_specs=(_HBM, _HBM), input_output_aliases={0: 0, 1: 1},
        compiler_params=pltpu.CompilerParams(has_side_effects=_EFFECT),
    )(x_thru, land_thru, send_sems, recv_sems, after)[1]


def _call(body, *, name, grid, in_specs, out_specs, out_shape, args, scratch=(), sem=None, comm=None):
    if comm is None:
        res = pl.pallas_call(body, name=name, grid=grid, in_specs=list(in_specs), out_specs=list(out_specs),
                             out_shape=list(out_shape), scratch_shapes=list(scratch), compiler_params=_params(sem))(*args)
        return list(res), None
    n_in, n_out, n_scr = len(in_specs), len(out_specs), len(scratch)
    c_n = len(comm.arrays)

    def wrapped(*refs):
        ins, cin = refs[:n_in], refs[n_in:n_in + c_n]
        o0 = n_in + c_n
        outs, cout = refs[o0:o0 + n_out], refs[o0 + n_out:o0 + n_out + c_n]
        s0 = o0 + n_out + c_n
        scr, cscr = refs[s0:s0 + n_scr], refs[s0 + n_scr:]
        ids = [pl.program_id(ax) for ax in range(len(grid))]
        first = functools.reduce(jnp.logical_and, [i == 0 for i in ids])
        last = functools.reduce(jnp.logical_and, [i == g - 1 for i, g in zip(ids, grid)])

        @pl.when(first)
        def _():
            comm.start(cin, cout, cscr)

        body(*ins, *outs, *scr)

        @pl.when(last)
        def _():
            comm.finish(cin, cout, cscr)

    res = pl.pallas_call(
        wrapped, name=name, grid=grid, in_specs=list(in_specs) + [_ANY] * c_n, out_specs=list(out_specs) + [_ANY] * c_n,
        out_shape=list(out_shape) + list(comm.out_shape), scratch_shapes=list(scratch) + list(comm.scratch),
        compiler_params=_params(("arbitrary",) * len(grid)))(*args, *comm.arrays)
    return list(res[:n_out]), list(res[n_out:])


MM_B_BLOCK_BYTES = 6 * 1024 * 1024
MM_O_BLOCK_BYTES = 13 * 1024 * 1024 // 2


def _mm(a, b, name, tb=False, out_dtype=F32, b3=False, o_cs=None, tm_max=1088, comm=None):
    m, k = a.shape
    if b3:
        cs = b.shape[2]
        n, kb = (b.shape[1], N_DEV * cs) if tb else (N_DEV * cs, b.shape[1])
    else:
        n, kb = (b.shape[0], b.shape[1]) if tb else (b.shape[1], b.shape[0])
    assert k == kb, (a.shape, b.shape, tb)
    tm = _tile(m, tm_max, 2 * SUBLANES)
    tk = cs if (b3 and tb) else k
    nk = k // tk
    if b3 and not tb:
        tn = cs
    elif o_cs is not None:
        tn = o_cs if o_cs % LANES == 0 else 2 * o_cs
    else:
        tn = _tile(n, min(MM_B_BLOCK_BYTES // (2 * tk), MM_O_BLOCK_BYTES // (4 * tm)), LANES)
    cb = 1 if tb else 0
    dn = (((1,), (cb,)), ((), ()))

    def body_one(a_ref, b_ref, o_ref):
        r = lax.dot_general(a_ref[...], b_ref[...], dn, preferred_element_type=F32)
        if o_cs is None:
            o_ref[...] = r.astype(o_ref.dtype)
        else:
            for j in range(tn // o_cs):
                o_ref[j] = r[:, j * o_cs:(j + 1) * o_cs].astype(o_ref.dtype)

    def body_acc(a_ref, b_ref, o_ref, acc_ref):
        kk = pl.program_id(2)
        prod = lax.dot_general(a_ref[...], b_ref[...], dn, preferred_element_type=F32)

        @pl.when(kk == 0)
        def _():
            acc_ref[...] = prod

        @pl.when(kk > 0)
        def _():
            acc_ref[...] += prod

        @pl.when(kk == nk - 1)
        def _():
            o_ref[...] = acc_ref[...].astype(o_ref.dtype)

    a_spec = pl.BlockSpec((tm, tk), lambda i, j, kk: (i, kk))
    if b3:
        b_spec = (pl.BlockSpec((None, tn, cs), lambda i, j, kk: (kk, j, 0)) if tb
                  else pl.BlockSpec((None, tk, cs), lambda i, j, kk: (j, kk, 0)))
    else:
        b_spec = pl.BlockSpec((tn, tk), lambda i, j, kk: (j, kk)) if tb else pl.BlockSpec((tk, tn), lambda i, j, kk: (kk, j))
    if o_cs is None:
        o_spec = pl.BlockSpec((tm, tn), lambda i, j, kk: (i, j))
        o_shape = jax.ShapeDtypeStruct((m, n), out_dtype)
    else:
        assert nk == 1
        o_spec = pl.BlockSpec((tn // o_cs, tm, o_cs), lambda i, j, kk: (j, i, 0))
        o_shape = jax.ShapeDtypeStruct((n // o_cs, m, o_cs), out_dtype)
    res, cres = _call(
        body_one if nk == 1 else body_acc, name=name, grid=(m // tm, n // tn, nk), in_specs=[a_spec, b_spec],
        out_specs=[o_spec], out_shape=[o_shape],
        scratch=[] if nk == 1 else [pltpu.VMEM((tm, tn), F32)], sem=("parallel", "parallel", "arbitrary"),
        args=(a, b), comm=comm)
    return res[0], cres


DW_TM = 512


def _transpose_bf16(x, name):
    t, c = x.shape
    tt = _tm()

    def body(x_ref, o_ref):
        o_ref[...] = x_ref[...].T

    return pl.pallas_call(body, name=name, grid=(t // tt,), in_specs=[pl.BlockSpec((tt, c), lambda i: (i, 0))],
                          out_specs=pl.BlockSpec((c, tt), lambda i: (0, i)),
                          out_shape=jax.ShapeDtypeStruct((c, t), BF16), compiler_params=_params(("parallel",)))(x)


def _cast_bf16(x, name):
    r, c = x.shape
    tr = _tile(r, 512, 2 * SUBLANES)

    def body(x_ref, o_ref):
        o_ref[...] = x_ref[...].astype(BF16)

    return pl.pallas_call(body, name=name, grid=(r // tr,), in_specs=[pl.BlockSpec((tr, c), lambda i: (i, 0))],
                          out_specs=pl.BlockSpec((tr, c), lambda i: (i, 0)),
                          out_shape=jax.ShapeDtypeStruct((r, c), BF16), compiler_params=_params(("parallel",)))(x)


def _cols_from_shards(wg, name):
    _, k, cs = wg.shape
    tk = _tile(k, 256, 2 * SUBLANES)

    def body(w_ref, o_ref):
        for j in range(N_DEV):
            o_ref[:, j * cs:(j + 1) * cs] = w_ref[j]

    return pl.pallas_call(body, name=name, grid=(k // tk,),
                          in_specs=[pl.BlockSpec((N_DEV, tk, cs), lambda i: (0, i, 0))],
                          out_specs=pl.BlockSpec((tk, N_DEV * cs), lambda i: (i, 0)),
                          out_shape=jax.ShapeDtypeStruct((k, N_DEV * cs), wg.dtype),
                          compiler_params=_params(("parallel",)))(wg)


def _stream(i):
    return jnp.minimum(i, 1)


def _normmod(x, g, sh, sc):
    y = x * lax.rsqrt(jnp.mean(x * x, axis=-1, keepdims=True) + EPS)
    return (y * g) * (1.0 + sc) + sh


def _mod_spec(chunk, d):
    return pl.BlockSpec((None, None, 1, d), lambda i: (_stream(i), chunk, 0, 0))


def _normmod_fwd(x, g, mod4, which, name):
    t, d = x.shape
    tm = _tm()
    ish, isc = (0, 1) if which == 0 else (3, 4)

    def body(x_ref, g_ref, sh_ref, sc_ref, o_ref):
        o_ref[...] = _normmod(x_ref[...], g_ref[...], sh_ref[...], sc_ref[...]).astype(BF16)

    row = pl.BlockSpec((tm, d), lambda i: (i, 0))
    return pl.pallas_call(body, name=name, grid=(t // tm,),
                          in_specs=[row, pl.BlockSpec((1, d), lambda i: (0, 0)), _mod_spec(ish, d), _mod_spec(isc, d)],
                          out_specs=row, out_shape=jax.ShapeDtypeStruct((t, d), BF16),
                          compiler_params=_params(("parallel",)))(x, g, mod4, mod4)


def _normmod_bwd(x, g, mod4, which, dh, dres, name, comm=None):
    t, d = x.shape
    tm = _tm()
    ish, isc = (0, 1) if which == 0 else (3, 4)

    def body(x_ref, g_ref, sh_ref, sc_ref, dh_ref, dres_ref, dx_ref, dg_ref, dsh_ref, dsc_ref):
        i = pl.program_id(0)
        _, vjp = jax.vjp(_normmod, x_ref[...], g_ref[...], sh_ref[...], sc_ref[...])
        dx, dg, dsh, dsc = vjp(dh_ref[...])
        dx_ref[...] = dres_ref[...] + dx

        @pl.when(i == 0)
        def _():
            dg_ref[...] = jnp.zeros_like(dg_ref)

        @pl.when(i <= 1)
        def _():
            dsh_ref[...] = jnp.zeros_like(dsh_ref)
            dsc_ref[...] = jnp.zeros_like(dsc_ref)

        dg_ref[...] += dg
        dsh_ref[...] += dsh
        dsc_ref[...] += dsc

    row = pl.BlockSpec((tm, d), lambda i: (i, 0))
    vec = pl.BlockSpec((1, d), lambda i: (0, 0))
    svec = pl.BlockSpec((None, 1, d), lambda i: (_stream(i), 0, 0))
    return _call(
        body, name=name, grid=(t // tm,),
        in_specs=[row, vec, _mod_spec(ish, d), _mod_spec(isc, d), row, row],
        out_specs=[row, vec, svec, svec],
        out_shape=[jax.ShapeDtypeStruct((t, d), F32), jax.ShapeDtypeStruct((1, d), F32),
                   jax.ShapeDtypeStruct((2, 1, d), F32), jax.ShapeDtypeStruct((2, 1, d), F32)],
        sem=("arbitrary",), args=(x, g, mod4, mod4, dh, dres), comm=comm)


def _gate_res_fwd(x, f, mod4, chunk, name):
    t, d = x.shape
    tm = _tm()

    def body(x_ref, f_ref, g_ref, o_ref):
        o_ref[...] = x_ref[...] + g_ref[...] * f_ref[...]

    row = pl.BlockSpec((tm, d), lambda i: (i, 0))
    return pl.pallas_call(body, name=name, grid=(t // tm,), in_specs=[row, row, _mod_spec(chunk, d)], out_specs=row,
                          out_shape=jax.ShapeDtypeStruct((t, d), F32), compiler_params=_params(("parallel",)))(x, f, mod4)


def _gate_res_bwd(dx, f, mod4, chunk, name):
    t, d = dx.shape
    tm = _tm()

    def body(dx_ref, f_ref, g_ref, o_ref, dg_ref):
        i = pl.program_id(0)
        dxv = dx_ref[...]
        o_ref[...] = (dxv * g_ref[...]).astype(BF16)

        @pl.when(i <= 1)
        def _():
            dg_ref[...] = jnp.zeros_like(dg_ref)

        dg_ref[...] += jnp.sum(dxv * f_ref[...], axis=0, keepdims=True)

    row = pl.BlockSpec((tm, d), lambda i: (i, 0))
    return pl.pallas_call(
        body, name=name, grid=(t // tm,), in_specs=[row, row, _mod_spec(chunk, d)],
        out_specs=[row, pl.BlockSpec((None, 1, d), lambda i: (_stream(i), 0, 0))],
        out_shape=[jax.ShapeDtypeStruct((t, d), BF16), jax.ShapeDtypeStruct((2, 1, d), F32)],
        compiler_params=_params(("arbitrary",)))(dx, f, mod4)


def _loss_head(x, final_g, target, name):
    t, d = x.shape
    tm = _tm()

    def loss_fn(xv, g, tgt):
        y = (xv * lax.rsqrt(jnp.mean(xv * xv, axis=-1, keepdims=True) + EPS)) * g
        err = y - tgt
        return 0.5 * jnp.sum(jnp.mean(err * err, axis=-1, keepdims=True))

    def body(x_ref, g_ref, t_ref, l_ref, dx_ref, dg_ref):
        i = pl.program_id(0)

        @pl.when(i == 0)
        def _():
            l_ref[...] = jnp.zeros_like(l_ref)
            dg_ref[...] = jnp.zeros_like(dg_ref)
            dx_ref[...] = jnp.zeros_like(dx_ref)

        @pl.when(i > 0)
        def _():
            l, (dx, dg) = jax.value_and_grad(loss_fn, argnums=(0, 1))(x_ref[...], g_ref[...], t_ref[...])
            l_ref[...] += jnp.full(l_ref.shape, l, F32)
            dx_ref[...] = dx
            dg_ref[...] += dg

    row = pl.BlockSpec((tm, d), lambda i: (i, 0))
    vec = pl.BlockSpec((1, d), lambda i: (0, 0))
    return pl.pallas_call(
        body, name=name, grid=(t // tm,),
        in_specs=[row, vec, pl.BlockSpec((tm, d), lambda i: (jnp.maximum(i - 1, 0), 0))],
        out_specs=[pl.BlockSpec((SUBLANES, LANES), lambda i: (0, 0)), row, vec],
        out_shape=[jax.ShapeDtypeStruct((SUBLANES, LANES), F32), jax.ShapeDtypeStruct((t, d), F32),
                   jax.ShapeDtypeStruct((1, d), F32)],
        compiler_params=_params(("arbitrary",)))(x, final_g, target)


def _swap_quarters(x):
    half, nf = RET_DK // 2, RET_DK // 4
    lane = lax.broadcasted_iota(jnp.int32, x.shape, 1)
    return jnp.where((lane % half) < nf, pltpu.roll(x, RET_DK - nf, 1), pltpu.roll(x, nf, 1))


def _rope(x, cos, sin):
    return x * cos + _swap_quarters(x) * sin


def _rope_t(y, cos, sin):
    return y * cos + _swap_quarters(y * sin)


def _ret_consts(d):
    c = RET_CHUNK
    ii = lax.broadcasted_iota(jnp.int32, (c, 1), 0).astype(F32)
    jj = lax.broadcasted_iota(jnp.int32, (1, c), 1).astype(F32)
    fwd = d == 0
    sgn = jnp.where(fwd, 1.0, -1.0).astype(F32)
    pos = jnp.where(fwd, ii, c - 1.0 - ii)
    return sgn * (ii - jj), pos


def _ret_step(lgt, state, q, k, v, diff, pos):
    c = float(RET_CHUNK)
    lg = -(jnp.maximum(-lgt, 0.0) + jnp.log1p(jnp.exp(-jnp.abs(lgt))))
    lower = diff >= 0
    decay = jnp.where(lower, jnp.exp(jnp.where(lower, diff, 0.0) * lg), 0.0)
    xi = jnp.exp((pos + 1.0) * lg)
    zeta = jnp.exp((c - 1.0 - pos) * lg)
    gch = jnp.exp(c * lg)
    inner = dot_nt(q, k) * decay
    out = dot_nn(inner, v) + dot_nn(q, state) * xi
    new_state = state * gch + dot_tn(k * zeta, v)
    return out, new_state


def _chunk_order():
    nc, nch = CTX_LEN // RET_CHUNK, _t_rows() // RET_CHUNK
    fwd = list(range(nch))
    bwd = list(range(nc - 1, -1, -1)) + list(range(nch - 1, nc - 1, -1))
    return jnp.asarray(np.array([fwd, bwd], np.int32))


def _ret_fwd(p, cos, sin, decay, order, name):
    t = p.shape[0]
    c, dk, dv, nh = RET_CHUNK, RET_DK, RET_DV, RET_HEADS
    nch = t // c
    off = _offsets()
    wqk, wv = nh * dk, nh * dv
    assert off["q"] % wqk == 0 and off["k"] % wqk == 0 and off["v"] % wv == 0
    qb, kb, vb = off["q"] // wqk, off["k"] // wqk, off["v"] // wv
    scale = RET_DK ** -0.5

    def body(ord_ref, dec_ref, q_ref, k_ref, v_ref, cos_ref, sin_ref, o_ref, st_ref, state):
        d, s = pl.program_id(0), pl.program_id(1)

        @pl.when(s == 0)
        def _():
            state[...] = jnp.zeros_like(state)

        diff, pos = _ret_consts(d)
        cosv, sinv = cos_ref[...], sin_ref[...]
        for h in range(nh):
            st = state[h]
            st_ref[h] = st
            lgt = jnp.full((1, 1), dec_ref[d, h], F32)
            q = _rope(q_ref[:, h * dk:(h + 1) * dk], cosv, sinv) * scale
            k = _rope(k_ref[:, h * dk:(h + 1) * dk], cosv, sinv)
            out, ns = _ret_step(lgt, st, q, k, v_ref[:, h * dv:(h + 1) * dv], diff, pos)
            o_ref[:, h * dv:(h + 1) * dv] = out
            state[h] = ns

    grid_spec = pltpu.PrefetchScalarGridSpec(
        num_scalar_prefetch=1, grid=(2, nch),
        in_specs=[pl.BlockSpec(memory_space=pltpu.SMEM),
                  pl.BlockSpec((c, wqk), lambda d, s, o: (o[d, s], qb)),
                  pl.BlockSpec((c, wqk), lambda d, s, o: (o[d, s], kb)),
                  pl.BlockSpec((c, wv), lambda d, s, o: (o[d, s], vb)),
                  pl.BlockSpec((c, dk), lambda d, s, o: (o[d, s], 0)),
                  pl.BlockSpec((c, dk), lambda d, s, o: (o[d, s], 0))],
        out_specs=[pl.BlockSpec((None, c, wv), lambda d, s, o: (d, o[d, s], 0)),
                   pl.BlockSpec((None, nh, None, dk, dv), lambda d, s, o: (d, 0, s, 0, 0))],
        scratch_shapes=[pltpu.VMEM((nh, dk, dv), F32)])
    return pl.pallas_call(
        body, name=name, grid_spec=grid_spec,
        out_shape=[jax.ShapeDtypeStruct((2, t, wv), F32), jax.ShapeDtypeStruct((2, nh, nch, dk, dv), F32)],
        compiler_params=_params(("arbitrary", "arbitrary")))(order, decay, p, p, p, cos, sin)


def _ret_bwd(p, cos, sin, decay, order, states, do, name):
    t = p.shape[0]
    c, dk, dv, nh = RET_CHUNK, RET_DK, RET_DV, RET_HEADS
    nch = t // c
    off = _offsets()
    wqk, wv = nh * dk, nh * dv
    qb, kb, vb = off["q"] // wqk, off["k"] // wqk, off["v"] // wv
    scale = RET_DK ** -0.5

    def body(ord_ref, dec_ref, q_ref, k_ref, v_ref, cos_ref, sin_ref, st_ref, do_ref,
             dq_ref, dk_ref, dv_ref, dd_ref, dstate):
        d, s = pl.program_id(0), pl.program_id(1)

        @pl.when(s == 0)
        def _():
            dstate[...] = jnp.zeros_like(dstate)
            dd_ref[...] = jnp.zeros_like(dd_ref)

        diff, pos = _ret_consts(d)
        cosv, sinv = cos_ref[...], sin_ref[...]
        for h in range(nh):
            qk, vv = slice(h * dk, (h + 1) * dk), slice(h * dv, (h + 1) * dv)
            lgt = jnp.full((1, 1), dec_ref[d, h], F32)
            q = _rope(q_ref[:, qk], cosv, sinv) * scale
            k = _rope(k_ref[:, qk], cosv, sinv)
            _, vjp = jax.vjp(lambda a, b, cq, ck, cv: _ret_step(a, b, cq, ck, cv, diff, pos),
                             lgt, st_ref[h], q, k, v_ref[:, vv])
            dlgt, dst, dq, dkk, dvv = vjp((do_ref[:, vv], dstate[h]))
            dstate[h] = dst
            dq_ref[:, qk] = _rope_t(dq * scale, cosv, sinv)
            dk_ref[:, qk] = _rope_t(dkk, cosv, sinv)
            dv_ref[:, vv] = dvv
            dd_ref[h] += jnp.broadcast_to(dlgt, (SUBLANES, LANES))

    rev = lambda o, d, s: o[d, nch - 1 - s]
    grid_spec = pltpu.PrefetchScalarGridSpec(
        num_scalar_prefetch=1, grid=(2, nch),
        in_specs=[pl.BlockSpec(memory_space=pltpu.SMEM),
                  pl.BlockSpec((c, wqk), lambda d, s, o: (rev(o, d, s), qb)),
                  pl.BlockSpec((c, wqk), lambda d, s, o: (rev(o, d, s), kb)),
                  pl.BlockSpec((c, wv), lambda d, s, o: (rev(o, d, s), vb)),
                  pl.BlockSpec((c, dk), lambda d, s, o: (rev(o, d, s), 0)),
                  pl.BlockSpec((c, dk), lambda d, s, o: (rev(o, d, s), 0)),
                  pl.BlockSpec((None, nh, None, dk, dv), lambda d, s, o: (d, 0, nch - 1 - s, 0, 0)),
                  pl.BlockSpec((c, wv), lambda d, s, o: (rev(o, d, s), 0))],
        out_specs=[pl.BlockSpec((None, c, wqk), lambda d, s, o: (d, rev(o, d, s), 0)),
                   pl.BlockSpec((None, c, wqk), lambda d, s, o: (d, rev(o, d, s), 0)),
                   pl.BlockSpec((None, c, wv), lambda d, s, o: (d, rev(o, d, s), 0)),
                   pl.BlockSpec((None, nh, SUBLANES, LANES), lambda d, s, o: (d, 0, 0, 0))],
        scratch_shapes=[pltpu.VMEM((nh, dk, dv), F32)])
    return pl.pallas_call(
        body, name=name, grid_spec=grid_spec,
        out_shape=[jax.ShapeDtypeStruct((2, t, wqk), F32), jax.ShapeDtypeStruct((2, t, wqk), F32),
                   jax.ShapeDtypeStruct((2, t, wv), F32), jax.ShapeDtypeStruct((2, nh, SUBLANES, LANES), F32)],
        compiler_params=_params(("arbitrary", "arbitrary")))(order, decay, p, p, p, cos, sin, states, do)


def _ggn_head(of, ob, gate, g):
    o = of + ob
    mu = jnp.mean(o, axis=-1, keepdims=True)
    var = jnp.mean(jnp.square(o - mu), axis=-1, keepdims=True)
    return ((o - mu) * lax.rsqrt(var + EPS) * g) * _silu(gate)


def _ggn_fwd(o2, p, gn_g, name):
    t = p.shape[0]
    tm, w, dv = _tm(), _ret_w(), RET_DV
    gb = _offsets()["g"] // w

    def body(o_ref, gate_ref, g_ref, out_ref):
        for h in range(RET_HEADS):
            sl = slice(h * dv, (h + 1) * dv)
            out_ref[:, sl] = _ggn_head(o_ref[0, :, sl], o_ref[1, :, sl], gate_ref[:, sl], g_ref[:, sl]).astype(BF16)

    return pl.pallas_call(
        body, name=name, grid=(t // tm,),
        in_specs=[pl.BlockSpec((2, tm, w), lambda i: (0, i, 0)), pl.BlockSpec((tm, w), lambda i: (i, gb)),
                  pl.BlockSpec((1, w), lambda i: (0, 0))],
        out_specs=pl.BlockSpec((tm, w), lambda i: (i, 0)), out_shape=jax.ShapeDtypeStruct((t, w), BF16),
        compiler_params=_params(("parallel",)))(o2, p, gn_g)


def _ggn_bwd(o2, p, gn_g, dmix, name):
    t = p.shape[0]
    tm, w, dv = _tm(), _ret_w(), RET_DV
    gb = _offsets()["g"] // w

    def body(o_ref, gate_ref, g_ref, dy_ref, do_ref, dgate_ref, dg_ref):
        i = pl.program_id(0)

        @pl.when(i == 0)
        def _():
            dg_ref[...] = jnp.zeros_like(dg_ref)

        for h in range(RET_HEADS):
            sl = slice(h * dv, (h + 1) * dv)
            _, vjp = jax.vjp(_ggn_head, o_ref[0, :, sl], o_ref[1, :, sl], gate_ref[:, sl], g_ref[:, sl])
            do, _, dgate, dg = vjp(dy_ref[:, sl])
            do_ref[:, sl] = do
            dgate_ref[:, sl] = dgate
            dg_ref[:, sl] += dg

    row = pl.BlockSpec((tm, w), lambda i: (i, 0))
    return pl.pallas_call(
        body, name=name, grid=(t // tm,),
        in_specs=[pl.BlockSpec((2, tm, w), lambda i: (0, i, 0)), pl.BlockSpec((tm, w), lambda i: (i, gb)),
                  pl.BlockSpec((1, w), lambda i: (0, 0)), row],
        out_specs=[row, row, pl.BlockSpec((1, w), lambda i: (0, 0))],
        out_shape=[jax.ShapeDtypeStruct((t, w), F32), jax.ShapeDtypeStruct((t, w), F32),
                   jax.ShapeDtypeStruct((1, w), F32)],
        compiler_params=_params(("arbitrary",)))(o2, p, gn_g, dmix)


def _halo(k):
    return SUBLANES * ((k // 2 + SUBLANES - 1) // SUBLANES)


def _halo_specs(width, colblock, h, tm):
    r = tm // h
    return [pl.BlockSpec((h, width), lambda i, *_: (jnp.maximum(i * r - 1, 0), colblock(*_))),
            pl.BlockSpec((tm, width), lambda i, *_: (i, colblock(*_))),
            pl.BlockSpec((h, width), lambda i, *_: (jnp.minimum((i + 1) * r, (_t_rows() // h) - 1), colblock(*_)))]


def _fill_ext(ext_ref, prev, cur, nxt, i, h, tm):
    nt = _t_rows() // tm
    ext_ref[0:h, :] = jnp.where(i >= 2, prev, 0.0)
    ext_ref[h:h + tm, :] = cur
    ext_ref[h + tm:h + tm + h, :] = jnp.where((i >= 1) & (i <= nt - 2), nxt, 0.0)


def _corr(ext_ref, w_ref, k, h, tm, flip):
    pad = k // 2
    acc = None
    for kk in range(k):
        o = h + (pad - kk if flip else kk - pad)
        term = w_ref[kk:kk + 1, :] * ext_ref[o:o + tm, :]
        acc = term if acc is None else acc + term
    return acc


def _conv_post(u2, ln_g, ln_b, pw):
    mu = jnp.mean(u2, axis=-1, keepdims=True)
    var = jnp.mean(jnp.square(u2 - mu), axis=-1, keepdims=True)
    y = (u2 - mu) * lax.rsqrt(var + EPS) * ln_g + ln_b
    return dot_nn(_silu(y), pw)


def _conv_fwd(p, dw_w, dw_b, ln_g, ln_b, pw, name):
    t = p.shape[0]
    tm, w, k = _tm(), CONV_W, CONV_K
    h = _halo(k)
    off = _offsets()
    ab, bb = off["a"] // w, off["b"] // w

    def body(ap, ac, an, bp, bc, bn, w_ref, b_ref, g_ref, beta_ref, pw_ref, u2_ref, out_ref, ext):
        i = pl.program_id(0)
        glu = lambda a, b: a * _sigmoid(b)
        _fill_ext(ext, glu(ap[...], bp[...]), glu(ac[...], bc[...]), glu(an[...], bn[...]), i, h, tm)
        u2 = _corr(ext, w_ref, k, h, tm, False) + b_ref[...]
        u2_ref[...] = u2
        out_ref[...] = _conv_post(u2, g_ref[...], beta_ref[...], pw_ref[...]).astype(BF16)

    vec = pl.BlockSpec((1, w), lambda i: (0, 0))
    row = pl.BlockSpec((tm, w), lambda i: (i, 0))
    return pl.pallas_call(
        body, name=name, grid=(t // tm,),
        in_specs=_halo_specs(w, lambda: ab, h, tm) + _halo_specs(w, lambda: bb, h, tm)
        + [pl.BlockSpec((k, w), lambda i: (0, 0)), vec, vec, vec, pl.BlockSpec((w, w), lambda i: (0, 0))],
        out_specs=[row, row],
        out_shape=[jax.ShapeDtypeStruct((t, w), F32), jax.ShapeDtypeStruct((t, w), BF16)],
        scratch_shapes=[pltpu.VMEM((tm + 2 * h, w), F32)],
        compiler_params=_params(("parallel",)))(p, p, p, p, p, p, dw_w, dw_b, ln_g, ln_b, pw)


def _conv_bwd1(u2, dmix, ln_g, ln_b, pw, name):
    t = u2.shape[0]
    tm, w = _tm(), CONV_W
    cb = _ret_w() // w

    def body(u2_ref, dy_ref, g_ref, beta_ref, pw_ref, du2_ref, dg_ref, db_ref, dpw_ref):
        i = pl.program_id(0)

        @pl.when(i == 0)
        def _():
            dg_ref[...] = jnp.zeros_like(dg_ref)
            db_ref[...] = jnp.zeros_like(db_ref)
            dpw_ref[...] = jnp.zeros_like(dpw_ref)

        _, vjp = jax.vjp(_conv_post, u2_ref[...], g_ref[...], beta_ref[...], pw_ref[...])
        du2, dg, db, dpw = vjp(dy_ref[...])
        du2_ref[...] = du2
        dg_ref[...] += dg
        db_ref[...] += db
        dpw_ref[...] += dpw

    vec = pl.BlockSpec((1, w), lambda i: (0, 0))
    row = pl.BlockSpec((tm, w), lambda i: (i, 0))
    mat = pl.BlockSpec((w, w), lambda i: (0, 0))
    return pl.pallas_call(
        body, name=name, grid=(t // tm,),
        in_specs=[row, pl.BlockSpec((tm, w), lambda i: (i, cb)), vec, vec, mat],
        out_specs=[row, vec, vec, mat],
        out_shape=[jax.ShapeDtypeStruct((t, w), F32), jax.ShapeDtypeStruct((1, w), F32),
                   jax.ShapeDtypeStruct((1, w), F32), jax.ShapeDtypeStruct((w, w), F32)],
        compiler_params=_params(("arbitrary",)))(u2, dmix, ln_g, ln_b, pw)


def _conv_bwd2(du2, p, dw_w, name, comm=None):
    t = p.shape[0]
    tm, w, k = _tm(), CONV_W, CONV_K
    h = _halo(k)
    pad = k // 2
    off = _offsets()
    ab, bb = off["a"] // w, off["b"] // w

    def body(dp, dc, dn, ap, ac, an, bp, bc, bn, w_ref, da_ref, db_ref, dw_ref, dbias_ref, ext_d, ext_u):
        i = pl.program_id(0)

        @pl.when(i == 0)
        def _():
            dw_ref[...] = jnp.zeros_like(dw_ref)
            dbias_ref[...] = jnp.zeros_like(dbias_ref)

        glu = lambda a, b: a * _sigmoid(b)
        a, b, d = ac[...], bc[...], dc[...]
        _fill_ext(ext_d, dp[...], d, dn[...], i, h, tm)
        _fill_ext(ext_u, glu(ap[...], bp[...]), glu(a, b), glu(an[...], bn[...]), i, h, tm)
        du = _corr(ext_d, w_ref, k, h, tm, True)
        sg = _sigmoid(b)
        da_ref[...] = du * sg
        db_ref[...] = du * a * sg * (1.0 - sg)
        dbias_ref[...] += jnp.sum(d, axis=0, keepdims=True)
        for kk in range(k):
            o = h + kk - pad
            dw_ref[kk:kk + 1, :] += jnp.sum(d * ext_u[o:o + tm, :], axis=0, keepdims=True)

    vec = pl.BlockSpec((1, w), lambda i: (0, 0))
    row = pl.BlockSpec((tm, w), lambda i: (i, 0))
    kw = pl.BlockSpec((k, w), lambda i: (0, 0))
    return _call(
        body, name=name, grid=(t // tm,),
        in_specs=_halo_specs(w, lambda: 0, h, tm) + _halo_specs(w, lambda: ab, h, tm)
        + _halo_specs(w, lambda: bb, h, tm) + [kw],
        out_specs=[row, row, kw, vec],
        out_shape=[jax.ShapeDtypeStruct((t, w), F32), jax.ShapeDtypeStruct((t, w), F32),
                   jax.ShapeDtypeStruct((k, w), F32), jax.ShapeDtypeStruct((1, w), F32)],
        scratch=[pltpu.VMEM((tm + 2 * h, w), F32), pltpu.VMEM((tm + 2 * h, w), F32)],
        sem=("arbitrary",), args=(du2, du2, du2, p, p, p, p, p, p, dw_w), comm=comm)


def _ffn_tc():
    return _tile(D_FF, 512, LANES)


def _ffn_act_fwd(u, dw_w, dw_b, name):
    t = u.shape[0]
    tm, k, tc = _tm(), FFN_K, _ffn_tc()
    h = _halo(k)
    nj = D_FF // tc

    def body(vp, vc, vn, gp, gc, gn, wv, wg, bv, bg, out_ref, ext_v, ext_g):
        i = pl.program_id(0)
        _fill_ext(ext_v, vp[...], vc[...], vn[...], i, h, tm)
        _fill_ext(ext_g, gp[...], gc[...], gn[...], i, h, tm)
        for r0 in range(0, tm, ROW_CHUNK):
            val = _corr(ext_v, wv, k, h + r0, ROW_CHUNK, False) + bv[...]
            gate = _corr(ext_g, wg, k, h + r0, ROW_CHUNK, False) + bg[...]
            out_ref[r0:r0 + ROW_CHUNK, :] = (_silu(gate) * val).astype(BF16)

    wspec = lambda s: pl.BlockSpec((k, tc), lambda i, j: (0, j + s))
    bspec = lambda s: pl.BlockSpec((1, tc), lambda i, j: (0, j + s))
    return pl.pallas_call(
        body, name=name, grid=(t // tm, nj),
        in_specs=_halo_specs(tc, lambda j: j, h, tm) + _halo_specs(tc, lambda j: j + nj, h, tm)
        + [wspec(0), wspec(nj), bspec(0), bspec(nj)],
        out_specs=pl.BlockSpec((tm, tc), lambda i, j: (i, j)),
        out_shape=jax.ShapeDtypeStruct((t, D_FF), BF16),
        scratch_shapes=[pltpu.VMEM((tm + 2 * h, tc), F32), pltpu.VMEM((tm + 2 * h, tc), F32)],
        compiler_params=_params(("parallel", "parallel")))(u, u, u, u, u, u, dw_w, dw_w, dw_b, dw_b)


def _ffn_act_bwd1(u, da, dw_w, dw_b, name):
    t = u.shape[0]
    tm, k, tc = _tm(), FFN_K, _ffn_tc()
    h = _halo(k)
    nj = D_FF // tc

    def body(vp, vc, vn, gp, gc, gn, wv, wg, bv, bg, da_ref, dv_ref, dg_ref, ext_v, ext_g):
        i = pl.program_id(0)
        _fill_ext(ext_v, vp[...], vc[...], vn[...], i, h, tm)
        _fill_ext(ext_g, gp[...], gc[...], gn[...], i, h, tm)
        for r0 in range(0, tm, ROW_CHUNK):
            rows = slice(r0, r0 + ROW_CHUNK)
            val = _corr(ext_v, wv, k, h + r0, ROW_CHUNK, False) + bv[...]
            gate = _corr(ext_g, wg, k, h + r0, ROW_CHUNK, False) + bg[...]
            _, vjp = jax.vjp(lambda a, b: _silu(b) * a, val, gate)
            dval, dgate = vjp(da_ref[rows, :])
            dv_ref[rows, :] = dval
            dg_ref[rows, :] = dgate

    wspec = lambda s: pl.BlockSpec((k, tc), lambda i, j: (0, j + s))
    bspec = lambda s: pl.BlockSpec((1, tc), lambda i, j: (0, j + s))
    dc = pl.pallas_call(
        body, name=name, grid=(t // tm, nj),
        in_specs=_halo_specs(tc, lambda j: j, h, tm) + _halo_specs(tc, lambda j: j + nj, h, tm)
        + [wspec(0), wspec(nj), bspec(0), bspec(nj), pl.BlockSpec((tm, tc), lambda i, j: (i, j))],
        out_specs=[pl.BlockSpec((tm, tc), lambda i, j: (i, j)), pl.BlockSpec((tm, tc), lambda i, j: (i, j))],
        out_shape=[jax.ShapeDtypeStruct((t, D_FF), F32), jax.ShapeDtypeStruct((t, D_FF), F32)],
        scratch_shapes=[pltpu.VMEM((tm + 2 * h, tc), F32), pltpu.VMEM((tm + 2 * h, tc), F32)],
        compiler_params=_params(("parallel", "parallel")))(u, u, u, u, u, u, dw_w, dw_w, dw_b, dw_b, da)
    return dc


def _dwconv_bwd(dc, u, dw_w, colblock, name):
    t = u.shape[0]
    tm, k, tc = _tm(), FFN_K, _ffn_tc()
    h = _halo(k)
    pad = k // 2
    nj = D_FF // tc

    def body(dp, dcur, dn, up, uc, un, w_ref, du_ref, dw_ref, dbias_ref, ext_d, ext_u):
        i = pl.program_id(1)

        @pl.when(i == 0)
        def _():
            dw_ref[...] = jnp.zeros_like(dw_ref)
            dbias_ref[...] = jnp.zeros_like(dbias_ref)

        _fill_ext(ext_d, dp[...], dcur[...], dn[...], i, h, tm)
        _fill_ext(ext_u, up[...], uc[...], un[...], i, h, tm)
        acc_b = jnp.zeros((ROW_CHUNK, tc), F32)
        acc_w = [jnp.zeros((ROW_CHUNK, tc), F32) for _ in range(k)]
        for r0 in range(0, tm, ROW_CHUNK):
            d = ext_d[h + r0:h + r0 + ROW_CHUNK, :]
            du_ref[r0:r0 + ROW_CHUNK, :] = _corr(ext_d, w_ref, k, h + r0, ROW_CHUNK, True).astype(BF16)
            acc_b = acc_b + d
            for kk in range(k):
                o = h + r0 + kk - pad
                acc_w[kk] = acc_w[kk] + d * ext_u[o:o + ROW_CHUNK, :]
        dbias_ref[...] += jnp.sum(acc_b, axis=0, keepdims=True)
        for kk in range(k):
            dw_ref[kk:kk + 1, :] += jnp.sum(acc_w[kk], axis=0, keepdims=True)

    def hs(cb):
        r = tm // h
        return [pl.BlockSpec((h, tc), lambda j, i: (jnp.maximum(i * r - 1, 0), cb(j))),
                pl.BlockSpec((tm, tc), lambda j, i: (i, cb(j))),
                pl.BlockSpec((h, tc), lambda j, i: (jnp.minimum((i + 1) * r, (_t_rows() // h) - 1), cb(j)))]

    return pl.pallas_call(
        body, name=name, grid=(nj, t // tm),
        in_specs=hs(lambda j: j) + hs(lambda j: j + colblock) + [pl.BlockSpec((k, tc), lambda j, i: (0, j + colblock))],
        out_specs=[pl.BlockSpec((tm, tc), lambda j, i: (i, j)), pl.BlockSpec((k, tc), lambda j, i: (0, j)),
                   pl.BlockSpec((1, tc), lambda j, i: (0, j))],
        out_shape=[jax.ShapeDtypeStruct((t, D_FF), BF16), jax.ShapeDtypeStruct((k, D_FF), F32),
                   jax.ShapeDtypeStruct((1, D_FF), F32)],
        scratch_shapes=[pltpu.VMEM((tm + 2 * h, tc), F32), pltpu.VMEM((tm + 2 * h, tc), F32)],
        compiler_params=_params(("parallel", "arbitrary")))(dc, dc, dc, u, u, u, dw_w)


def _na_geometry(rq):
    ncb = CTX_LEN // GRID_W
    rows_n = SEQ // GRID_W
    r = jnp.maximum(rq - ncb, 0)
    kstart = jnp.clip(r - NA_ROWS // 2, 0, rows_n - NA_ROWS)
    base = kstart - r + NA_ROWS - 1
    return rq >= ncb, kstart, base


def _na_core(q, kl, vl, kc, vc, bias, mask):
    qs = q * (NA_DH ** -0.5)
    s_l = jnp.where(mask, dot_nt(qs, kl) + bias, NEG)
    s_c = dot_nt(qs, kc)
    m = lax.stop_gradient(jnp.maximum(jnp.max(s_l, axis=1, keepdims=True), jnp.max(s_c, axis=1, keepdims=True)))
    e_l, e_c = jnp.exp(s_l - m), jnp.exp(s_c - m)
    inv = 1.0 / (jnp.sum(e_l, axis=1, keepdims=True) + jnp.sum(e_c, axis=1, keepdims=True))
    return dot_nn(e_l * inv, vl) + dot_nn(e_c * inv, vc)


def _na_mask(is_lat):
    nl = NA_ROWS * GRID_W
    q = lax.broadcasted_iota(jnp.int32, (GRID_W, nl), 0)
    w = lax.broadcasted_iota(jnp.int32, (GRID_W, nl), 1) % GRID_W
    cs = jnp.clip(q - NA_COLS // 2, 0, GRID_W - NA_COLS)
    return (w >= cs) & (w < cs + NA_COLS) & is_lat


def _na_bias(rb_ref):
    assert 2 * GRID_W == LANES
    lane = lax.broadcasted_iota(jnp.int32, (GRID_W, LANES), 1)
    tiles = []
    for kp in range(NA_ROWS // 2):
        ev = jnp.broadcast_to(rb_ref[2 * kp:2 * kp + 1, :], (GRID_W, LANES))
        od = jnp.broadcast_to(rb_ref[2 * kp + 1:2 * kp + 2, :], (GRID_W, LANES))
        ev = pltpu.roll(ev, LANES - (NA_COLS - 1), 1, stride=1, stride_axis=0)
        od = pltpu.roll(od, LANES - (NA_COLS - 1) - GRID_W, 1, stride=1, stride_axis=0)
        tiles.append(jnp.where(lane < GRID_W, ev, od))
    return jnp.concatenate(tiles, axis=1)


def _na_dbias(dbias, drb_ref):
    qi = lax.broadcasted_iota(jnp.int32, (GRID_W, GRID_W), 0)
    qj = lax.broadcasted_iota(jnp.int32, (GRID_W, GRID_W), 1)
    flip = (qi + qj == GRID_W - 1).astype(F32)
    rev = lax.dot_general(flip, dbias, (((1,), (0,)), ((), ())), precision=lax.Precision.HIGHEST,
                          preferred_element_type=F32)
    lane = lax.broadcasted_iota(jnp.int32, (GRID_W, LANES), 1)
    s_ev = LANES - (GRID_W - NA_COLS)
    for kp in range(NA_ROWS // 2):
        tile = rev[:, kp * LANES:(kp + 1) * LANES]
        ev = pltpu.roll(jnp.where(lane < GRID_W, tile, 0.0), s_ev, 1, stride=1, stride_axis=0)
        od = pltpu.roll(jnp.where(lane >= GRID_W, tile, 0.0), s_ev - GRID_W, 1, stride=1, stride_axis=0)
        drb_ref[2 * kp:2 * kp + 1, :] += jnp.sum(ev, axis=0, keepdims=True)
        drb_ref[2 * kp + 1:2 * kp + 2, :] += jnp.sum(od, axis=0, keepdims=True)


def _na_hps():
    return 2 if NA_HEADS % 2 == 0 else 1


def _na_specs(p_offsets):
    t = _t_rows()
    hps = _na_hps()
    wd = hps * NA_DH
    assert all(p_offsets[n] % wd == 0 for n in ("nq", "nk", "nv"))
    qb, kb, vb = (p_offsets[n] // wd for n in ("nq", "nk", "nv"))
    return [pl.BlockSpec((GRID_W, wd), lambda h, r: (r, qb + h)),
            pl.BlockSpec((t, wd), lambda h, r: (0, kb + h)),
            pl.BlockSpec((t, wd), lambda h, r: (0, vb + h)),
            pl.BlockSpec((hps, None, NA_ROWS, LANES), lambda h, r: (h, _na_geometry(r)[2], 0, 0))]


def _na_fwd(p, rb, name, comm=None):
    t = p.shape[0]
    dh, nl = NA_DH, NA_ROWS * GRID_W

    hps = _na_hps()

    def body(q_ref, k_ref, v_ref, rb_ref, out_ref):
        rq = pl.program_id(1)
        is_lat, kstart, _ = _na_geometry(rq)
        start = pl.multiple_of(CTX_LEN + kstart * GRID_W, GRID_W)
        mask = _na_mask(is_lat)
        for hh in range(hps):
            cols = slice(hh * dh, (hh + 1) * dh)
            out = _na_core(q_ref[:, cols], k_ref[pl.ds(start, nl), cols], v_ref[pl.ds(start, nl), cols],
                           k_ref[0:CTX_LEN, cols], v_ref[0:CTX_LEN, cols], _na_bias(rb_ref.at[hh]), mask)
            out_ref[:, cols] = out.astype(BF16)

    res, cres = _call(
        body, name=name, grid=(NA_HEADS // hps, t // GRID_W), in_specs=_na_specs(_offsets()),
        out_specs=[pl.BlockSpec((GRID_W, hps * dh), lambda h, r: (r, h))],
        out_shape=[jax.ShapeDtypeStruct((t, _na_w()), BF16)],
        sem=("parallel", "arbitrary"), args=(p, p, p, rb), comm=comm)
    return res[0], cres


def _na_bwd(p, rb, dmix, name, comm=None):
    t = p.shape[0]
    dh, nl = NA_DH, NA_ROWS * GRID_W

    hps = _na_hps()
    wd = hps * dh
    assert ((_ret_w() + CONV_W) // dh) % hps == 0
    ob = (_ret_w() + CONV_W) // wd

    def body(q_ref, k_ref, v_ref, rb_ref, dy_ref, dq_ref, dk_ref, dv_ref, drb_ref):
        rq = pl.program_id(1)
        is_lat, kstart, base = _na_geometry(rq)
        _, _, prev_base = _na_geometry(rq - 1)
        start = pl.multiple_of(CTX_LEN + kstart * GRID_W, GRID_W)

        @pl.when(rq == 0)
        def _():
            dk_ref[...] = jnp.zeros_like(dk_ref)
            dv_ref[...] = jnp.zeros_like(dv_ref)

        @pl.when((rq == 0) | (base != prev_base))
        def _():
            drb_ref[...] = jnp.zeros_like(drb_ref)

        mask = _na_mask(is_lat)
        for hh in range(hps):
            cols = slice(hh * dh, (hh + 1) * dh)
            _, vjp = jax.vjp(lambda *a: _na_core(*a, mask), q_ref[:, cols], k_ref[pl.ds(start, nl), cols],
                             v_ref[pl.ds(start, nl), cols], k_ref[0:CTX_LEN, cols], v_ref[0:CTX_LEN, cols],
                             _na_bias(rb_ref.at[hh]))
            dq, dkl, dvl, dkc, dvc, dbias = vjp(dy_ref[:, cols])
            dq_ref[:, cols] = dq
            dk_ref[pl.ds(start, nl), cols] += dkl
            dv_ref[pl.ds(start, nl), cols] += dvl
            dk_ref[0:CTX_LEN, cols] += dkc
            dv_ref[0:CTX_LEN, cols] += dvc
            _na_dbias(dbias, drb_ref.at[hh])

    return _call(
        body, name=name, grid=(NA_HEADS // hps, t // GRID_W),
        in_specs=_na_specs(_offsets()) + [pl.BlockSpec((GRID_W, wd), lambda h, r: (r, ob + h))],
        out_specs=[pl.BlockSpec((GRID_W, wd), lambda h, r: (r, h)), pl.BlockSpec((t, wd), lambda h, r: (0, h)),
                   pl.BlockSpec((t, wd), lambda h, r: (0, h)),
                   pl.BlockSpec((hps, None, NA_ROWS, LANES), lambda h, r: (h, _na_geometry(r)[2], 0, 0))],
        out_shape=[jax.ShapeDtypeStruct((t, _na_w()), F32), jax.ShapeDtypeStruct((t, _na_w()), F32),
                   jax.ShapeDtypeStruct((t, _na_w()), F32),
                   jax.ShapeDtypeStruct((NA_HEADS, NA_ROWS, NA_ROWS, LANES), F32)],
        sem=("parallel", "arbitrary"), args=(p, p, p, rb, dmix), comm=comm)


def _rpb_select():
    sel = np.zeros((2 * NA_ROWS - 1, NA_ROWS * NA_ROWS), np.float32)
    for b in range(NA_ROWS):
        for kh in range(NA_ROWS):
            sel[b + kh, b * NA_ROWS + kh] = 1.0
    return jnp.asarray(sel)


def _rpb_rows(rpb):
    pad = jnp.pad(rpb, ((0, 0), (0, 0), (0, LANES - (2 * NA_COLS - 1))))
    rows = jnp.einsum("rk,hrc->hkc", _rpb_select(), pad, precision=lax.Precision.HIGHEST)
    return rows.reshape(NA_HEADS, NA_ROWS, NA_ROWS, LANES)


def _rpb_rows_t(drb):
    flat = drb.reshape(NA_HEADS, NA_ROWS * NA_ROWS, LANES)
    out = jnp.einsum("rk,hkc->hrc", _rpb_select(), flat, precision=lax.Precision.HIGHEST)
    return out[:, :, :2 * NA_COLS - 1]


def _assemble_dp(dqr, dkr, dvr, dgate, da, db, dnq, dnk, dnv, name):
    t = dgate.shape[0]
    tm = _tm()
    off = _offsets()
    sizes = dict(q=_ret_qk_w(), k=_ret_qk_w(), v=_ret_w(), g=_ret_w(), a=CONV_W, b=CONV_W, nq=_na_w(), nk=_na_w(), nv=_na_w())

    def body(q_ref, k_ref, v_ref, g_ref, a_ref, b_ref, nq_ref, nk_ref, nv_ref, o_ref):
        def put(n, val):
            o_ref[:, off[n]:off[n] + sizes[n]] = val.astype(BF16)

        put("q", q_ref[0] + q_ref[1])
        put("k", k_ref[0] + k_ref[1])
        put("v", v_ref[0] + v_ref[1])
        put("g", g_ref[...])
        put("a", a_ref[...])
        put("b", b_ref[...])
        put("nq", nq_ref[...])
        put("nk", nk_ref[...])
        put("nv", nv_ref[...])

    two = lambda w: pl.BlockSpec((2, tm, w), lambda i: (0, i, 0))
    one = lambda w: pl.BlockSpec((tm, w), lambda i: (i, 0))
    return pl.pallas_call(
        body, name=name, grid=(t // tm,),
        in_specs=[two(sizes["q"]), two(sizes["k"]), two(sizes["v"]), one(sizes["g"]), one(CONV_W), one(CONV_W),
                  one(_na_w()), one(_na_w()), one(_na_w())],
        out_specs=one(_d_in()), out_shape=jax.ShapeDtypeStruct((t, _d_in()), BF16),
        compiler_params=_params(("parallel",)))(dqr, dkr, dvr, dgate, da, db, dnq, dnk, dnv)


def _adamw(w, m, v, gs, name, comm=None):
    nl, r, c = w.shape
    stacked = not isinstance(gs, (list, tuple))
    if stacked:
        gs = [gs]
    assert stacked or len(gs) == nl
    g_n = gs[0].shape[-3]
    block_bytes = 2 * 1024 * 1024
    rows = min(block_bytes // (4 * c), block_bytes // (g_n * c * gs[0].dtype.itemsize))
    tr = _tile(r, max(2 * SUBLANES, rows // (2 * SUBLANES) * (2 * SUBLANES)), 2 * SUBLANES)
    nt = r // tr
    c1 = 1.0 - ADAM_B1 ** ADAM_STEP
    c2 = 1.0 - ADAM_B2 ** ADAM_STEP

    def body(w_ref, m_ref, v_ref, *rest):
        g_refs, (go_ref, d_ref, mo_ref, vo_ref) = rest[:len(gs)], rest[len(gs):]
        layer = pl.program_id(0)
        for ll in range(len(gs)):
            @pl.when(jnp.logical_or(stacked, layer == ll))
            def _():
                g_ref = g_refs[ll]
                g = g_ref[0].astype(F32)
                for j in range(1, g_n):
                    g = g + g_ref[j].astype(F32)
                mn = ADAM_B1 * m_ref[...] + (1.0 - ADAM_B1) * g
                vn = ADAM_B2 * v_ref[...] + (1.0 - ADAM_B2) * (g * g)
                m_hat = mn / c1
                v_hat = vn / c2
                go_ref[...] = g
                d_ref[...] = -ADAM_LR * (m_hat / (jnp.sqrt(v_hat) + ADAM_EPS) + ADAM_WD * w_ref[...])
                mo_ref[...] = mn
                vo_ref[...] = vn

    def g_spec(ll):
        if stacked:
            return pl.BlockSpec((None, g_n, tr, c), lambda l, i: (l, 0, i, 0))
        return pl.BlockSpec((g_n, tr, c), lambda l, i: (0, jnp.where(l == ll, i, jnp.where(l < ll, 0, nt - 1)), 0))

    blk = pl.BlockSpec((None, tr, c), lambda l, i: (l, i, 0))
    sds = jax.ShapeDtypeStruct((nl, r, c), F32)
    return _call(
        body, name=name, grid=(nl, nt),
        in_specs=[blk, blk, blk] + [g_spec(ll) for ll in range(len(gs))],
        out_specs=[blk, blk, blk, blk], out_shape=[sds, sds, sds, sds],
        sem=("arbitrary", "arbitrary"), args=(w, m, v, *gs), comm=comm)


def _sum_devices(g, name):
    _, r, c = g.shape
    tr = _tile(r, 512, SUBLANES)

    def body(g_ref, o_ref):
        acc = g_ref[0]
        for j in range(1, N_DEV):
            acc = acc + g_ref[j]
        o_ref[...] = acc

    return pl.pallas_call(body, name=name, grid=(r // tr,), in_specs=[pl.BlockSpec((N_DEV, tr, c), lambda i: (0, i, 0))],
                          out_specs=pl.BlockSpec((tr, c), lambda i: (i, 0)), out_shape=jax.ShapeDtypeStruct((r, c), F32),
                          compiler_params=_params(("parallel",)))(g)


def _ada_fwd(c16, w_ada, b_shard, name):
    nl, d, cs = w_ada.shape
    tk = _tile(d, 512, LANES)
    nk = d // tk

    def body(c_ref, w_ref, b_ref, o_ref):
        kk = pl.program_id(1)

        @pl.when(kk == 0)
        def _():
            o_ref[...] = jnp.broadcast_to(b_ref[...], o_ref.shape)

        o_ref[...] += _dg(_silu(c_ref[...]), w_ref[...], 1, 0)

    return pl.pallas_call(
        body, name=name, grid=(nl, nk),
        in_specs=[pl.BlockSpec((16, tk), lambda l, kk: (0, kk)), pl.BlockSpec((None, tk, cs), lambda l, kk: (l, kk, 0)),
                  pl.BlockSpec((None, 1, cs), lambda l, kk: (l, 0, 0))],
        out_specs=pl.BlockSpec((None, 16, cs), lambda l, kk: (l, 0, 0)),
        out_shape=jax.ShapeDtypeStruct((nl, 16, cs), F32),
        compiler_params=_params(("parallel", "arbitrary")))(c16, w_ada, b_shard)


def _ada_bwd(c16, dm16, w_ada, name):
    nl, d, cs = w_ada.shape
    td = _tile(d, 512, LANES)

    def body(c_ref, dm_ref, w_ref, gw_ref, dc_ref):
        cv = c_ref[...]
        s, vjp = jax.vjp(_silu, cv)
        gw_ref[...] = _dg(s, dm_ref[...], 0, 0)
        ds = _dg(dm_ref[...], w_ref[...], 1, 1)
        dc_ref[...] = vjp(ds)[0]

    return pl.pallas_call(
        body, name=name, grid=(nl, d // td),
        in_specs=[pl.BlockSpec((16, td), lambda l, i: (0, i)), pl.BlockSpec((None, 16, cs), lambda l, i: (l, 0, 0)),
                  pl.BlockSpec((None, td, cs), lambda l, i: (l, i, 0))],
        out_specs=[pl.BlockSpec((None, td, cs), lambda l, i: (l, i, 0)), pl.BlockSpec((None, 16, td), lambda l, i: (l, 0, i))],
        out_shape=[jax.ShapeDtypeStruct((nl, d, cs), F32), jax.ShapeDtypeStruct((nl, 16, d), F32)],
        compiler_params=_params(("parallel", "parallel")))(c16, dm16, w_ada)


def _pack_rows(shape):
    n = int(np.prod(shape))
    return SUBLANES * (-(-n // (LANES * SUBLANES)))


def _pack(arrays, row_align):
    parts = []
    for a in arrays:
        flat = a.reshape(-1).astype(F32)
        rows = _pack_rows(a.shape)
        parts.append(jnp.pad(flat, (0, rows * LANES - flat.shape[0])).reshape(rows, LANES))
    total = sum(p.shape[0] for p in parts)
    extra = -total % row_align
    if extra:
        parts.append(jnp.zeros((extra, LANES), F32))
    return jnp.concatenate(parts, axis=0)


def _unpack(packed, shapes):
    out, r = [], 0
    for s in shapes:
        rows = _pack_rows(s)
        out.append(packed[r:r + rows].reshape(-1)[:int(np.prod(s))].reshape(s))
        r += rows
    return out


def _rope_tables():
    half, nf = RET_DK // 2, RET_DK // 4
    pos = jnp.arange(SEQ)
    row = (pos // GRID_W).astype(F32)
    col = (pos % GRID_W).astype(F32)
    inv = ROPE_BASE ** (-jnp.arange(nf, dtype=F32) / nf)
    ar, ac = row[:, None] * inv[None, :], col[:, None] * inv[None, :]
    cos = jnp.concatenate([jnp.cos(ar), jnp.cos(ar), jnp.cos(ac), jnp.cos(ac)], axis=-1)
    sin = jnp.concatenate([-jnp.sin(ar), jnp.sin(ar), -jnp.sin(ac), jnp.sin(ac)], axis=-1)
    cos = jnp.concatenate([jnp.ones((CTX_LEN, RET_DK), F32), cos], axis=0)
    sin = jnp.concatenate([jnp.zeros((CTX_LEN, RET_DK), F32), sin], axis=0)
    return cos, sin


def _layer_fwd(l, x, mod4, w, cst, arrived):
    n = lambda s: f"l{l}_{s}"
    d = D_MODEL
    h1 = _normmod_fwd(x, w["norm1_g"], mod4, 0, n("norm1"))
    w["w_in"] = _cols_from_shards(arrived("w_in", h1), n("w_in_cols"))
    p, _ = _mm(h1, w["w_in"], n("proj_in"))
    o2, states = _ret_fwd(p, cst["cos"], cst["sin"], w["ret_decay"], cst["order"], n("ret_fwd"))
    ret_out = _ggn_fwd(o2, p, w["ret_gn_g"], n("ret_gn"))
    u2, conv_out = _conv_fwd(p, w["conv_dw_w"], w["conv_dw_b"], w["conv_ln_g"], w["conv_ln_b"], w["conv_pw"], n("conv_fwd"))
    na_out, _ = _na_fwd(p, w["rb"], n("na_fwd"))
    mix = jnp.concatenate([ret_out, conv_out, na_out], axis=1)
    w["w_out"] = arrived("w_out", mix).reshape(_d_mix(), d)
    g1, _ = _mm(mix, w["w_out"], n("proj_out"))
    x1 = _gate_res_fwd(x, g1, mod4, 2, n("res1"))
    h2 = _normmod_fwd(x1, w["norm2_g"], mod4, 1, n("norm2"))
    w["ffn_up"] = arrived("ffn_up", h2)
    u, _ = _mm(h2, w["ffn_up"], n("ffn_up"), b3=True)
    a = _ffn_act_fwd(u, w["ffn_dw_w"], w["ffn_dw_b"], n("ffn_act"))
    w["ffn_down"] = arrived("ffn_down", a).reshape(D_FF, d)
    f, _ = _mm(a, w["ffn_down"], n("ffn_down"))
    x2 = _gate_res_fwd(x1, f, mod4, 5, n("res2"))
    saved = dict(x=x, h1=h1, p=p, o2=o2, states=states, u2=u2, mix=mix, g1=g1, x1=x1, h2=h2, u=u, a=a, f=f)
    return x2, saved


def _layer_bwd(l, dx2, s, mod4, w, cst, send):
    n = lambda t: f"l{l}_{t}"
    d = D_MODEL
    nj = D_FF // _ffn_tc()
    dfg, dg2 = _gate_res_bwd(dx2, s["f"], mod4, 5, n("res2_bwd"))
    da, _ = _mm(dfg, w["ffn_down"], n("ffn_down_dx"), tb=True)
    d_ffn_down, _ = _mm(_transpose_bf16(s["a"], n("act_t")), dfg, n("ffn_down_dw"), out_dtype=BF16, tm_max=DW_TM)
    send(("ffn_down", l), d_ffn_down.reshape(N_DEV, D_FF // N_DEV, d))
    dcv, dcg = _ffn_act_bwd1(s["u"], da, w["ffn_dw_w"], w["ffn_dw_b"], n("ffn_act_bwd"))
    duv, dwv, dbv = _dwconv_bwd(dcv, s["u"], w["ffn_dw_w"], 0, n("ffn_dw_bwd_val"))
    dug, dwg, dbg = _dwconv_bwd(dcg, s["u"], w["ffn_dw_w"], nj, n("ffn_dw_bwd_gate"))
    du = jnp.concatenate([duv, dug], axis=1)
    d_ffn_dw_w = jnp.concatenate([dwv, dwg], axis=1)
    d_ffn_dw_b = jnp.concatenate([dbv, dbg], axis=1)[0]
    dh2, _ = _mm(du, w["ffn_up"], n("ffn_up_dx"), tb=True, b3=True)
    d_ffn_up, _ = _mm(_transpose_bf16(s["h2"], n("h2_t")), du, n("ffn_up_dw"), out_dtype=BF16, tm_max=DW_TM,
                      o_cs=2 * D_FF // N_DEV)
    send(("ffn_up", l), d_ffn_up)
    (dx1, dn2, dsh2, dsc2), _ = _normmod_bwd(s["x1"], w["norm2_g"], mod4, 1, dh2, dx2, n("norm2_bwd"))
    dgg, dg1 = _gate_res_bwd(dx1, s["g1"], mod4, 2, n("res1_bwd"))
    dmix, _ = _mm(dgg, w["w_out"], n("proj_out_dx"), tb=True)
    d_w_out, _ = _mm(_transpose_bf16(s["mix"], n("mix_t")), dgg, n("proj_out_dw"), out_dtype=BF16, tm_max=DW_TM)
    send(("w_out", l), d_w_out.reshape(N_DEV, _d_mix() // N_DEV, d))
    do, dgate, dgn = _ggn_bwd(s["o2"], s["p"], w["ret_gn_g"], dmix, n("ret_gn_bwd"))
    dqr, dkr, dvr, ddec = _ret_bwd(s["p"], cst["cos"], cst["sin"], w["ret_decay"], cst["order"], s["states"], do, n("ret_bwd"))
    du2, dlng, dlnb, dpw = _conv_bwd1(s["u2"], dmix, w["conv_ln_g"], w["conv_ln_b"], w["conv_pw"], n("conv_bwd1"))
    (dca, dcb, ddww, ddwb), _ = _conv_bwd2(du2, s["p"], w["conv_dw_w"], n("conv_bwd2"))
    (dnq, dnk, dnv, drb), _ = _na_bwd(s["p"], w["rb"], dmix, n("na_bwd"))
    dp = _assemble_dp(dqr, dkr, dvr, dgate, dca, dcb, dnq, dnk, dnv, n("dproj"))
    h1_t = _transpose_bf16(s["h1"], n("h1_t"))
    half = d // 2
    for i in range(2):
        d_w_in, _ = _mm(h1_t[i * half:(i + 1) * half], dp, n(f"proj_in_dw{i}"), out_dtype=BF16, tm_max=DW_TM,
                        o_cs=_d_in() // N_DEV)
        send(("w_in", l, i), d_w_in)
    dh1, _ = _mm(dp, w["w_in"], n("proj_in_dx"), tb=True)
    (dx, dn1, dsh1, dsc1), _ = _normmod_bwd(s["x"], w["norm1_g"], mod4, 0, dh1, dx1, n("norm1_bwd"))
    dmod = jnp.concatenate([dsh1, dsc1, dg1, dsh2, dsc2, dg2], axis=1)
    small = dict(norm1_g=dn1[0], ret_decay=ddec[:, :, 0, 0], ret_gn_g=dgn[0], conv_dw_w=ddww, conv_dw_b=ddwb[0],
                 conv_ln_g=dlng[0], conv_ln_b=dlnb[0], conv_pw=dpw, na_rpb=_rpb_rows_t(drb), norm2_g=dn2[0],
                 ffn_dw_w=d_ffn_dw_w, ffn_dw_b=d_ffn_dw_b)
    return dx, dmod, small


def _d_mix():
    return _ret_w() + CONV_W + _na_w()


_SMALL = ["c_ctx", "b_ada", "norm1_g", "ret_decay", "ret_gn_g", "conv_dw_w", "conv_dw_b", "conv_ln_g", "conv_ln_b",
          "conv_pw", "na_rpb", "norm2_g", "ffn_dw_w", "ffn_dw_b", "final_g"]
_SMALL_SHARD_AXIS = {"conv_dw_w": 2, "conv_pw": 1, "ffn_dw_w": 2}


def kernel(x, c, ctx, c_ctx, w_ada, b_ada, norm1_g, w_in, ret_decay, ret_gn_g, conv_dw_w, conv_dw_b, conv_ln_g, conv_ln_b, conv_pw, na_rpb, w_out, norm2_g, ffn_up, ffn_dw_w, ffn_dw_b, ffn_down, final_g, loss_target, m_c_ctx, m_w_ada, m_b_ada, m_norm1_g, m_w_in, m_ret_decay, m_ret_gn_g, m_conv_dw_w, m_conv_dw_b, m_conv_ln_g, m_conv_ln_b, m_conv_pw, m_na_rpb, m_w_out, m_norm2_g, m_ffn_up, m_ffn_dw_w, m_ffn_dw_b, m_ffn_down, m_final_g, v_c_ctx, v_w_ada, v_b_ada, v_norm1_g, v_w_in, v_ret_decay, v_ret_gn_g, v_conv_dw_w, v_conv_dw_b, v_conv_ln_g, v_conv_ln_b, v_conv_pw, v_na_rpb, v_w_out, v_norm2_g, v_ffn_up, v_ffn_dw_w, v_ffn_dw_b, v_ffn_down, v_final_g):
    d, nl = D_MODEL, DEPTH
    cs = 6 * d // N_DEV
    me = _my_index()
    weights = dict(c_ctx=c_ctx, w_ada=w_ada, b_ada=b_ada, norm1_g=norm1_g, w_in=w_in, ret_decay=ret_decay, ret_gn_g=ret_gn_g,
                   conv_dw_w=conv_dw_w, conv_dw_b=conv_dw_b, conv_ln_g=conv_ln_g, conv_ln_b=conv_ln_b, conv_pw=conv_pw,
                   na_rpb=na_rpb, w_out=w_out, norm2_g=norm2_g, ffn_up=ffn_up, ffn_dw_w=ffn_dw_w, ffn_dw_b=ffn_dw_b,
                   ffn_down=ffn_down, final_g=final_g)
    mom = dict(c_ctx=m_c_ctx, w_ada=m_w_ada, b_ada=m_b_ada, norm1_g=m_norm1_g, w_in=m_w_in, ret_decay=m_ret_decay,
               ret_gn_g=m_ret_gn_g, conv_dw_w=m_conv_dw_w, conv_dw_b=m_conv_dw_b, conv_ln_g=m_conv_ln_g,
               conv_ln_b=m_conv_ln_b, conv_pw=m_conv_pw, na_rpb=m_na_rpb, w_out=m_w_out, norm2_g=m_norm2_g,
               ffn_up=m_ffn_up, ffn_dw_w=m_ffn_dw_w, ffn_dw_b=m_ffn_dw_b, ffn_down=m_ffn_down, final_g=m_final_g)
    var = dict(c_ctx=v_c_ctx, w_ada=v_w_ada, b_ada=v_b_ada, norm1_g=v_norm1_g, w_in=v_w_in, ret_decay=v_ret_decay,
               ret_gn_g=v_ret_gn_g, conv_dw_w=v_conv_dw_w, conv_dw_b=v_conv_dw_b, conv_ln_g=v_conv_ln_g,
               conv_ln_b=v_conv_ln_b, conv_pw=v_conv_pw, na_rpb=v_na_rpb, w_out=v_w_out, norm2_g=v_norm2_g,
               ffn_up=v_ffn_up, ffn_dw_w=v_ffn_dw_w, ffn_dw_b=v_ffn_dw_b, ffn_down=v_ffn_down, final_g=v_final_g)

    big_names = ["w_in", "w_out", "ffn_up", "ffn_down"]
    shards = {(nm, l): _cast_bf16(weights[nm][l], f"cast_{nm}{l}") for l in range(nl) for nm in big_names}
    small_sharded = _pack([conv_dw_w, conv_pw, ffn_dw_w], SUBLANES)
    c_rows = jnp.pad(c, ((0, SUBLANES - 1), (0, 0)))
    gathered = _run_comm(_Gather([c_rows, small_sharded]), "gather_first")
    arriving = {}
    for l in range(nl):
        for nm in big_names:
            arriving[(nm, l)] = _split_start(shards[(nm, l)], True, f"gather_{nm}{l}")
    c_all = gathered[0][:, 0, :]
    sm = [_unpack(gathered[1][j], [conv_dw_w.shape, conv_pw.shape, ffn_dw_w.shape]) for j in range(N_DEV)]
    full_conv_dw_w = jnp.concatenate([s[0] for s in sm], axis=2)
    full_conv_pw = jnp.concatenate([s[1] for s in sm], axis=1)
    full_ffn_dw_w = jnp.concatenate([s[2] for s in sm], axis=2)

    c16 = jnp.concatenate([c_all, jnp.broadcast_to(c_ctx[None, :], (N_DEV, d))], axis=0)
    b_shard = lax.dynamic_slice_in_dim(b_ada, me * cs, cs, axis=1)[:, None, :]
    m_shard = _ada_fwd(c16, w_ada, b_shard, "ada_fwd")
    m_all = _run_comm(_Gather([m_shard.reshape(nl * 16, cs)]), "gather_mod")[0]
    m_full = m_all.reshape(N_DEV, nl, 16, cs).transpose(1, 2, 0, 3).reshape(nl, 16, 6 * d)
    m_lat = lax.dynamic_index_in_dim(m_full, me, axis=1, keepdims=False)
    mod = jnp.stack([m_full[:, N_DEV], m_lat], axis=1).reshape(nl, 2, 6, 1, d)

    cos, sin = _rope_tables()
    cst = dict(cos=cos, sin=sin, order=_chunk_order())
    layer_w = []
    for l in range(nl):
        layer_w.append(dict(
            norm1_g=norm1_g[l][None], norm2_g=norm2_g[l][None], ret_decay=ret_decay[l], ret_gn_g=ret_gn_g[l][None],
            conv_dw_w=full_conv_dw_w[l], conv_dw_b=conv_dw_b[l][None], conv_ln_g=conv_ln_g[l][None],
            conv_ln_b=conv_ln_b[l][None], conv_pw=full_conv_pw[l], rb=_rpb_rows(na_rpb[l]),
            ffn_dw_w=full_ffn_dw_w[l], ffn_dw_b=ffn_dw_b[l][None]))

    xs = jnp.concatenate([ctx[0], x[0]], axis=0)
    saved = []
    for l in range(nl):
        arrived = lambda nm, after, l=l: _split_wait(arriving[(nm, l)], after, f"arrived_{nm}{l}")
        xs, sv = _layer_fwd(l, xs, mod[l], layer_w[l], cst, arrived)
        saved.append(sv)
    loss_tile, dxs, dfinal = _loss_head(xs, final_g[None], loss_target[0], "loss_head")
    loss = lax.psum(loss_tile[0, 0], ("x", "y", "c"))

    dmods, smalls = [None] * nl, [None] * nl
    leaving = {}

    def send(key, partial):
        leaving[key] = _split_start(partial, False, "send_" + "_".join(str(k) for k in key))

    for l in reversed(range(nl)):
        dxs, dmods[l], smalls[l] = _layer_bwd(l, dxs, saved[l], mod[l], layer_w[l], cst, send)
    grad_x = dxs[CTX_LEN:][None]

    dm_mine = jnp.stack(dmods).reshape(nl * 2, 6 * d)
    dm_rows = jnp.pad(dm_mine, ((0, SUBLANES - nl * 2), (0, 0)))
    dm_all = _run_comm(_Gather([dm_rows]), "gather_dmod")[0][:, :nl * 2].reshape(N_DEV, nl, 2, 6 * d)
    dm16_full = jnp.concatenate([dm_all[:, :, 1].transpose(1, 0, 2), dm_all[:, :, 0].transpose(1, 0, 2)], axis=1)
    dm16 = lax.dynamic_slice_in_dim(dm16_full, me * cs, cs, axis=2)
    g_w_ada, dc16 = _ada_bwd(c16, dm16, w_ada, "ada_bwd")

    arrived_grad = lambda key, after: _split_wait(leaving[key], after, "got_" + "_".join(str(k) for k in key))
    out_big = {}
    out_big["w_ada"], _ = _adamw(w_ada, m_w_ada, v_w_ada, g_w_ada[:, None], "adamw_w_ada")
    after = out_big["w_ada"][0]
    for nm in ["ffn_down", "ffn_up", "w_out"]:
        out_big[nm], _ = _adamw(weights[nm], mom[nm], var[nm], [arrived_grad((nm, l), after) for l in range(nl)],
                                f"adamw_{nm}")
        after = out_big[nm][0]
    halves = lambda a: a.reshape(2 * nl, d // 2, a.shape[2])
    res, _ = _adamw(halves(w_in), halves(m_w_in), halves(v_w_in),
                    [arrived_grad(("w_in", l, i), after) for l in range(nl) for i in range(2)], "adamw_w_in")
    out_big["w_in"] = [r.reshape(w_in.shape) for r in res]

    small_grads = dict(
        c_ctx=jnp.sum(dc16[:, N_DEV:], axis=(0, 1)),
        b_ada=jnp.sum(jnp.stack(dmods).reshape(nl, 2, 6 * d), axis=1),
        final_g=dfinal[0])
    for nm in _SMALL:
        if nm not in small_grads:
            small_grads[nm] = jnp.stack([smalls[l][nm] for l in range(nl)])
    shapes_full = [small_grads[nm].shape for nm in _SMALL]
    packed = _pack([small_grads[nm] for nm in _SMALL], 512)
    summed = _sum_devices(_run_comm(_Gather([packed]), "gather_small_grads")[0], "sum_small_grads")
    g_small = dict(zip(_SMALL, _unpack(summed, shapes_full)))
    for nm, ax in _SMALL_SHARD_AXIS.items():
        n_sh = weights[nm].shape[ax]
        g_small[nm] = lax.dynamic_slice_in_dim(g_small[nm], me * n_sh, n_sh, axis=ax)
    shapes_own = [weights[nm].shape for nm in _SMALL]
    pk = lambda src: _pack([src[nm] for nm in _SMALL], 2 * SUBLANES)[None]
    res_small, _ = _adamw(pk(weights), pk(mom), pk(var), pk(g_small)[:, None], "adamw_small")
    out_small = [dict(zip(_SMALL, _unpack(r[0], shapes_own))) for r in res_small]

    names = ["c_ctx", "w_ada", "b_ada", "norm1_g", "w_in", "ret_decay", "ret_gn_g", "conv_dw_w", "conv_dw_b", "conv_ln_g",
             "conv_ln_b", "conv_pw", "na_rpb", "w_out", "norm2_g", "ffn_up", "ffn_dw_w", "ffn_dw_b", "ffn_down", "final_g"]
    outs = [loss, grad_x]
    for kind in range(4):
        for nm in names:
            outs.append(out_big[nm][kind] if nm in out_big else out_small[kind][nm])
    return tuple(outs)
```

```python
import functools
import math

import numpy as np
import jax
import jax.numpy as jnp
from jax import lax
from jax.experimental import pallas as pl
from jax.experimental.pallas import tpu as pltpu

D_MODEL = 2048
SEQ = 4096
DEPTH = 2
GRID_W = 64
CTX_LEN = 256
RET_HEADS = 4
RET_DK = 128
RET_DV = 256
RET_CHUNK = 128
CONV_W = 512
CONV_K = 31
NA_HEADS = 4
NA_DH = 128
NA_ROWS = 8
NA_COLS = 16
D_FF = 5632
FFN_K = 3
ROPE_BASE = 10000.0
EPS = 1e-6
ADAM_LR = 0.001
ADAM_B1 = 0.9
ADAM_B2 = 0.999
ADAM_EPS = 1e-08
ADAM_WD = 0.01
ADAM_STEP = 10
N_DEV = 8

LANES = 128
SUBLANES = 8
VMEM_LIMIT = 56 * 1024 * 1024
ROW_CHUNK = 16

F32 = jnp.float32
BF16 = jnp.bfloat16
MESH = pl.DeviceIdType.MESH
NEG = -1e30


def _ret_qk_w():
    return RET_HEADS * RET_DK


def _ret_w():
    return RET_HEADS * RET_DV


def _na_w():
    return NA_HEADS * NA_DH


def _d_in():
    return 2 * _ret_qk_w() + 2 * _ret_w() + 2 * CONV_W + 3 * _na_w()


def _offsets():
    sizes = [_ret_qk_w(), _ret_qk_w(), _ret_w(), _ret_w(), CONV_W, CONV_W, _na_w(), _na_w(), _na_w()]
    offs = [0]
    for s in sizes[:-1]:
        offs.append(offs[-1] + s)
    return dict(zip(["q", "k", "v", "g", "a", "b", "nq", "nk", "nv"], offs))


def _t_rows():
    return CTX_LEN + SEQ


def _tm():
    return CTX_LEN


def _params(sem=None):
    kw = dict(vmem_limit_bytes=VMEM_LIMIT)
    if sem is not None:
        kw["dimension_semantics"] = sem
    return pltpu.CompilerParams(**kw)


def _tile(n, pref, align):
    best = None
    for t in range(align, min(n, pref) + 1, align):
        if n % t == 0:
            best = t
    return best if best is not None else n


def _dg(a, b, ca, cb):
    return lax.dot_general(a.astype(BF16), b.astype(BF16), (((ca,), (cb,)), ((), ())), preferred_element_type=F32)


@jax.custom_vjp
def dot_nn(a, b):
    return _dg(a, b, 1, 0)


dot_nn.defvjp(lambda a, b: (_dg(a, b, 1, 0), (a, b)),
              lambda r, g: (_dg(g, r[1], 1, 1), _dg(r[0], g, 0, 0)))


@jax.custom_vjp
def dot_nt(a, b):
    return _dg(a, b, 1, 1)


dot_nt.defvjp(lambda a, b: (_dg(a, b, 1, 1), (a, b)),
              lambda r, g: (_dg(g, r[1], 1, 0), _dg(g, r[0], 0, 0)))


@jax.custom_vjp
def dot_tn(a, b):
    return _dg(a, b, 0, 0)


dot_tn.defvjp(lambda a, b: (_dg(a, b, 0, 0), (a, b)),
              lambda r, g: (_dg(r[1], g, 1, 1), _dg(r[0], g, 1, 0)))


def _sigmoid(x):
    return 1.0 / (1.0 + jnp.exp(-x))


def _silu(x):
    return x * _sigmoid(x)


def _my_pos():
    return lax.axis_index("x"), lax.axis_index("y"), lax.axis_index("c")


def _my_index():
    x, y, c = _my_pos()
    return 4 * x + 2 * y + c


_ANY = pl.BlockSpec(memory_space=pl.ANY)


class _Gather:
    def __init__(self, arrays):
        self.arrays = list(arrays)
        n = len(self.arrays)
        self.out_shape = [jax.ShapeDtypeStruct((N_DEV,) + a.shape, a.dtype) for a in self.arrays]
        self.scratch = [pltpu.SemaphoreType.DMA((n, 7)), pltpu.SemaphoreType.DMA((n, 7)), pltpu.SemaphoreType.DMA((n,))]

    def _plan(self, xs, outs, sems):
        send_sems, recv_sems, local_sems = sems
        n = len(self.arrays)
        x, y, c = _my_pos()
        me, sibling = (x, y, c), (x, y, 1 - c)
        chips = [(1 - x, y), (x, 1 - y), (1 - x, 1 - y)]

        def slot(a, p):
            return outs[a].at[4 * p[0] + 2 * p[1] + p[2]]

        def copy(a, k, block, to, src=None):
            return pltpu.make_async_remote_copy(
                src_ref=slot(a, block) if src is None else src, dst_ref=slot(a, block),
                send_sem=send_sems.at[a, k], recv_sem=recv_sems.at[a, k], device_id=to, device_id_type=MESH)

        mine = [pltpu.make_async_copy(xs[a], slot(a, me), local_sems.at[a]) for a in range(n)]
        first = []
        for a in range(n):
            first.append(copy(a, 0, me, sibling, src=xs[a]))
            first += [copy(a, 1 + j, me, (*chip, c), src=xs[a]) for j, chip in enumerate(chips)]
        return n, c, me, sibling, chips, copy, mine, first

    def start(self, xs, outs, sems):
        _, _, _, _, _, _, mine, first = self._plan(xs, outs, sems)
        for m in mine:
            m.start()
        for cp in first:
            cp.start()

    def finish(self, xs, outs, sems):
        n, c, me, sibling, chips, copy, mine, first = self._plan(xs, outs, sems)
        passed = []
        for a in range(n):
            for j, chip in enumerate(chips):
                copy(a, 1 + j, (*chip, c), me).wait_recv()
                p = copy(a, 4 + j, (*chip, c), sibling)
                p.start()
                passed.append(p)
        for a in range(n):
            copy(a, 0, sibling, me).wait_recv()
            for j, chip in enumerate(chips):
                copy(a, 4 + j, (*chip, 1 - c), me).wait_recv()
        for cp in first + passed:
            cp.wait_send()
        for m in mine:
            m.wait()


class _Exchange:
    def __init__(self, arrays):
        self.arrays = list(arrays)
        n = len(self.arrays)
        self.out_shape = [jax.ShapeDtypeStruct(a.shape, a.dtype) for a in self.arrays]
        self.scratch = [pltpu.SemaphoreType.DMA((n, 7)), pltpu.SemaphoreType.DMA((n, 7)), pltpu.SemaphoreType.DMA((n,))]

    def _plan(self, xs, outs, sems):
        send_sems, recv_sems, local_sems = sems
        x, y, c = _my_pos()
        me = 4 * x + 2 * y + c
        mine, sends, recvs = [], [], []
        for a in range(len(self.arrays)):
            mine.append(pltpu.make_async_copy(xs[a].at[me], outs[a].at[me], local_sems.at[a]))
        for k in range(1, N_DEV):
            px = 1 - x if (k >> 2) & 1 else x
            py = 1 - y if (k >> 1) & 1 else y
            pc = 1 - c if k & 1 else c
            peer = 4 * px + 2 * py + pc
            for a in range(len(self.arrays)):
                sends.append(pltpu.make_async_remote_copy(
                    src_ref=xs[a].at[peer], dst_ref=outs[a].at[me], send_sem=send_sems.at[a, k - 1],
                    recv_sem=recv_sems.at[a, k - 1], device_id=(px, py, pc), device_id_type=MESH))
                recvs.append(pltpu.make_async_remote_copy(
                    src_ref=xs[a].at[me], dst_ref=outs[a].at[peer], send_sem=send_sems.at[a, k - 1],
                    recv_sem=recv_sems.at[a, k - 1], device_id=(px, py, pc), device_id_type=MESH))
        return mine, sends, recvs

    def start(self, xs, outs, sems):
        mine, sends, _ = self._plan(xs, outs, sems)
        for m in mine:
            m.start()
        for s in sends:
            s.start()

    def finish(self, xs, outs, sems):
        mine, sends, recvs = self._plan(xs, outs, sems)
        for r in recvs:
            r.wait_recv()
        for s in sends:
            s.wait_send()
        for m in mine:
            m.wait()


def _run_comm(comm, name):
    n = len(comm.arrays)

    def body(*refs):
        xs, outs, sems = refs[:n], refs[n:2 * n], refs[2 * n:]
        comm.start(xs, outs, sems)
        comm.finish(xs, outs, sems)

    return pl.pallas_call(body, name=name, out_shape=comm.out_shape, in_specs=[_ANY] * n, out_specs=[_ANY] * n,
                          scratch_shapes=comm.scratch)(*comm.arrays)


_HBM = pl.BlockSpec(memory_space=pltpu.HBM)
_SEMS = pl.BlockSpec(memory_space=pltpu.SEMAPHORE)
_EFFECT = pltpu.SideEffectType.DATAFLOW_SIDE_EFFECTING


def _own_slot(x, gathering, name):
    shape = (N_DEV,) + x.shape if gathering else x.shape
    r, c = shape[1], shape[2]
    tr = _tile(r, 256, 2 * SUBLANES)
    me = jnp.reshape(_my_index(), (1,)).astype(jnp.int32)

    def body(me_ref, x_ref, o_ref):
        o_ref[...] = x_ref[...]

    src = (pl.BlockSpec((tr, c), lambda i, m: (i, 0)) if gathering
           else pl.BlockSpec((None, tr, c), lambda i, m: (m[0], i, 0)))
    grid_spec = pltpu.PrefetchScalarGridSpec(
        num_scalar_prefetch=1, grid=(r // tr,), in_specs=[src],
        out_specs=pl.BlockSpec((None, tr, c), lambda i, m: (m[0], i, 0)))
    return pl.pallas_call(body, name=name, grid_spec=grid_spec, out_shape=jax.ShapeDtypeStruct(shape, x.dtype),
                          compiler_params=_params(("arbitrary",)))(me, x)


def _split_plan(x_ref, land_ref, send_sems, recv_sems, gathering):
    x, y, c = _my_pos()
    me = 4 * x + 2 * y + c
    sends, recvs = [], []
    for k in range(1, N_DEV):
        px = 1 - x if (k >> 2) & 1 else x
        py = 1 - y if (k >> 1) & 1 else y
        pc = 1 - c if k & 1 else c
        peer = 4 * px + 2 * py + pc
        mine, theirs = (x_ref, x_ref) if gathering else (x_ref.at[peer], x_ref.at[me])
        sends.append(pltpu.make_async_remote_copy(
            src_ref=mine, dst_ref=land_ref.at[me], send_sem=send_sems.at[k - 1], recv_sem=recv_sems.at[k - 1],
            device_id=(px, py, pc), device_id_type=MESH))
        recvs.append(pltpu.make_async_remote_copy(
            src_ref=theirs, dst_ref=land_ref.at[peer], send_sem=send_sems.at[k - 1], recv_sem=recv_sems.at[k - 1],
            device_id=(px, py, pc), device_id_type=MESH))
    return sends, recvs


def _split_start(x, gathering, name, prev):
    land = _own_slot(x, gathering, name + "_own")

    def body(x_ref, land_ref, prev_ref, send_sems, recv_sems, x_thru, land_thru, token):
        sends, _ = _split_plan(x_ref, land_ref, send_sems, recv_sems, gathering)
        for s in sends:
            s.start()
        token[...] = jnp.zeros_like(token)

    sems = pltpu.SemaphoreType.DMA((N_DEV - 1,))
    send_sems, recv_sems, x_thru, land_thru, token = pl.pallas_call(
        body, name=name,
        out_shape=(sems, sems, pltpu.HBM(x.shape, x.dtype), pltpu.HBM(land.shape, land.dtype),
                   jax.ShapeDtypeStruct((SUBLANES, LANES), F32)),
        in_specs=(_HBM, _HBM, _ANY), out_specs=(_SEMS, _SEMS, _HBM, _HBM, pl.BlockSpec(memory_space=pltpu.VMEM)),
        input_output_aliases={0: 2, 1: 3},
        compiler_params=pltpu.CompilerParams(has_side_effects=_EFFECT),
    )(pltpu.with_memory_space_constraint(x, pltpu.HBM), pltpu.with_memory_space_constraint(land, pltpu.HBM), prev)
    return (send_sems, recv_sems, x_thru, land_thru, gathering), token


def _after(a, token):
    return a + token[0, 0].astype(a.dtype)


def _split_wait(handle, after, name):
    send_sems, recv_sems, x_thru, land_thru, gathering = handle

    def body(x_ref, land_ref, send_sems, recv_sems, after_ref, x_dead, got_ref):
        sends, recvs = _split_plan(x_ref, land_ref, send_sems, recv_sems, gathering)
        for s in sends:
            s.wait_send()
        for r in recvs:
            r.wait_recv()

    return pl.pallas_call(
        body, name=name, out_shape=(pltpu.HBM(x_thru.shape, x_thru.dtype), pltpu.HBM(land_thru.shape, land_thru.dtype)),
        in_specs=(_HBM, _HBM, _SEMS, _SEMS, _ANY), out_specs=(_HBM, _HBM), input_output_aliases={0: 0, 1: 1},
        compiler_params=pltpu.CompilerParams(has_side_effects=_EFFECT),
    )(x_thru, land_thru, send_sems, recv_sems, after)[1]


def _call(body, *, name, grid, in_specs, out_specs, out_shape, args, scratch=(), sem=None, comm=None):
    if comm is None:
        res = pl.pallas_call(body, name=name, grid=grid, in_specs=list(in_specs), out_specs=list(out_specs),
                             out_shape=list(out_shape), scratch_shapes=list(scratch), compiler_params=_params(sem))(*args)
        return list(res), None
    n_in, n_out, n_scr = len(in_specs), len(out_specs), len(scratch)
    c_n = len(comm.arrays)

    def wrapped(*refs):
        ins, cin = refs[:n_in], refs[n_in:n_in + c_n]
        o0 = n_in + c_n
        outs, cout = refs[o0:o0 + n_out], refs[o0 + n_out:o0 + n_out + c_n]
        s0 = o0 + n_out + c_n
        scr, cscr = refs[s0:s0 + n_scr], refs[s0 + n_scr:]
        ids = [pl.program_id(ax) for ax in range(len(grid))]
        first = functools.reduce(jnp.logical_and, [i == 0 for i in ids])
        last = functools.reduce(jnp.logical_and, [i == g - 1 for i, g in zip(ids, grid)])

        @pl.when(first)
        def _():
            comm.start(cin, cout, cscr)

        body(*ins, *outs, *scr)

        @pl.when(last)
        def _():
            comm.finish(cin, cout, cscr)

    res = pl.pallas_call(
        wrapped, name=name, grid=grid, in_specs=list(in_specs) + [_ANY] * c_n, out_specs=list(out_specs) + [_ANY] * c_n,
        out_shape=list(out_shape) + list(comm.out_shape), scratch_shapes=list(scratch) + list(comm.scratch),
        compiler_params=_params(("arbitrary",) * len(grid)))(*args, *comm.arrays)
    return list(res[:n_out]), list(res[n_out:])


MM_B_BLOCK_BYTES = 6 * 1024 * 1024
MM_O_BLOCK_BYTES = 13 * 1024 * 1024 // 2


def _mm(a, b, name, tb=False, out_dtype=F32, b3=False, o_cs=None, tm_max=1088, comm=None):
    m, k = a.shape
    if b3:
        cs = b.shape[2]
        n, kb = (b.shape[1], N_DEV * cs) if tb else (N_DEV * cs, b.shape[1])
    else:
        n, kb = (b.shape[0], b.shape[1]) if tb else (b.shape[1], b.shape[0])
    assert k == kb, (a.shape, b.shape, tb)
    tm = _tile(m, tm_max, 2 * SUBLANES)
    tk = cs if (b3 and tb) else k
    nk = k // tk
    if b3 and not tb:
        tn = cs
    elif o_cs is not None:
        tn = o_cs if o_cs % LANES == 0 else 2 * o_cs
    else:
        tn = _tile(n, min(MM_B_BLOCK_BYTES // (2 * tk), MM_O_BLOCK_BYTES // (4 * tm)), LANES)
    cb = 1 if tb else 0
    dn = (((1,), (cb,)), ((), ()))

    def body_one(a_ref, b_ref, o_ref):
        r = lax.dot_general(a_ref[...], b_ref[...], dn, preferred_element_type=F32)
        if o_cs is None:
            o_ref[...] = r.astype(o_ref.dtype)
        else:
            for j in range(tn // o_cs):
                o_ref[j] = r[:, j * o_cs:(j + 1) * o_cs].astype(o_ref.dtype)

    def body_acc(a_ref, b_ref, o_ref, acc_ref):
        kk = pl.program_id(2)
        prod = lax.dot_general(a_ref[...], b_ref[...], dn, preferred_element_type=F32)

        @pl.when(kk == 0)
        def _():
            acc_ref[...] = prod

        @pl.when(kk > 0)
        def _():
            acc_ref[...] += prod

        @pl.when(kk == nk - 1)
        def _():
            o_ref[...] = acc_ref[...].astype(o_ref.dtype)

    a_spec = pl.BlockSpec((tm, tk), lambda i, j, kk: (i, kk))
    if b3:
        b_spec = (pl.BlockSpec((None, tn, cs), lambda i, j, kk: (kk, j, 0)) if tb
                  else pl.BlockSpec((None, tk, cs), lambda i, j, kk: (j, kk, 0)))
    else:
        b_spec = pl.BlockSpec((tn, tk), lambda i, j, kk: (j, kk)) if tb else pl.BlockSpec((tk, tn), lambda i, j, kk: (kk, j))
    if o_cs is None:
        o_spec = pl.BlockSpec((tm, tn), lambda i, j, kk: (i, j))
        o_shape = jax.ShapeDtypeStruct((m, n), out_dtype)
    else:
        assert nk == 1
        o_spec = pl.BlockSpec((tn // o_cs, tm, o_cs), lambda i, j, kk: (j, i, 0))
        o_shape = jax.ShapeDtypeStruct((n // o_cs, m, o_cs), out_dtype)
    res, cres = _call(
        body_one if nk == 1 else body_acc, name=name, grid=(m // tm, n // tn, nk), in_specs=[a_spec, b_spec],
        out_specs=[o_spec], out_shape=[o_shape],
        scratch=[] if nk == 1 else [pltpu.VMEM((tm, tn), F32)], sem=("parallel", "parallel", "arbitrary"),
        args=(a, b), comm=comm)
    return res[0], cres


DW_TM = 512


def _transpose_bf16(x, name):
    t, c = x.shape
    tt = _tm()

    def body(x_ref, o_ref):
        o_ref[...] = x_ref[...].T

    return pl.pallas_call(body, name=name, grid=(t // tt,), in_specs=[pl.BlockSpec((tt, c), lambda i: (i, 0))],
                          out_specs=pl.BlockSpec((c, tt), lambda i: (0, i)),
                          out_shape=jax.ShapeDtypeStruct((c, t), BF16), compiler_params=_params(("parallel",)))(x)


def _cast_bf16(x, name):
    r, c = x.shape
    tr = _tile(r, 512, 2 * SUBLANES)

    def body(x_ref, o_ref):
        o_ref[...] = x_ref[...].astype(BF16)

    return pl.pallas_call(body, name=name, grid=(r // tr,), in_specs=[pl.BlockSpec((tr, c), lambda i: (i, 0))],
                          out_specs=pl.BlockSpec((tr, c), lambda i: (i, 0)),
                          out_shape=jax.ShapeDtypeStruct((r, c), BF16), compiler_params=_params(("parallel",)))(x)


def _cols_from_shards(wg, name):
    _, k, cs = wg.shape
    tk = _tile(k, 256, 2 * SUBLANES)

    def body(w_ref, o_ref):
        for j in range(N_DEV):
            o_ref[:, j * cs:(j + 1) * cs] = w_ref[j]

    return pl.pallas_call(body, name=name, grid=(k // tk,),
                          in_specs=[pl.BlockSpec((N_DEV, tk, cs), lambda i: (0, i, 0))],
                          out_specs=pl.BlockSpec((tk, N_DEV * cs), lambda i: (i, 0)),
                          out_shape=jax.ShapeDtypeStruct((k, N_DEV * cs), wg.dtype),
                          compiler_params=_params(("parallel",)))(wg)


def _stream(i):
    return jnp.minimum(i, 1)


def _normmod(x, g, sh, sc):
    y = x * lax.rsqrt(jnp.mean(x * x, axis=-1, keepdims=True) + EPS)
    return (y * g) * (1.0 + sc) + sh


def _mod_spec(chunk, d):
    return pl.BlockSpec((None, None, 1, d), lambda i: (_stream(i), chunk, 0, 0))


def _normmod_fwd(x, g, mod4, which, name):
    t, d = x.shape
    tm = _tm()
    ish, isc = (0, 1) if which == 0 else (3, 4)

    def body(x_ref, g_ref, sh_ref, sc_ref, o_ref):
        o_ref[...] = _normmod(x_ref[...], g_ref[...], sh_ref[...], sc_ref[...]).astype(BF16)

    row = pl.BlockSpec((tm, d), lambda i: (i, 0))
    return pl.pallas_call(body, name=name, grid=(t // tm,),
                          in_specs=[row, pl.BlockSpec((1, d), lambda i: (0, 0)), _mod_spec(ish, d), _mod_spec(isc, d)],
                          out_specs=row, out_shape=jax.ShapeDtypeStruct((t, d), BF16),
                          compiler_params=_params(("parallel",)))(x, g, mod4, mod4)


def _normmod_bwd(x, g, mod4, which, dh, dres, name, comm=None):
    t, d = x.shape
    tm = _tm()
    ish, isc = (0, 1) if which == 0 else (3, 4)

    def body(x_ref, g_ref, sh_ref, sc_ref, dh_ref, dres_ref, dx_ref, dg_ref, dsh_ref, dsc_ref):
        i = pl.program_id(0)
        _, vjp = jax.vjp(_normmod, x_ref[...], g_ref[...], sh_ref[...], sc_ref[...])
        dx, dg, dsh, dsc = vjp(dh_ref[...])
        dx_ref[...] = dres_ref[...] + dx

        @pl.when(i == 0)
        def _():
            dg_ref[...] = jnp.zeros_like(dg_ref)

        @pl.when(i <= 1)
        def _():
            dsh_ref[...] = jnp.zeros_like(dsh_ref)
            dsc_ref[...] = jnp.zeros_like(dsc_ref)

        dg_ref[...] += dg
        dsh_ref[...] += dsh
        dsc_ref[...] += dsc

    row = pl.BlockSpec((tm, d), lambda i: (i, 0))
    vec = pl.BlockSpec((1, d), lambda i: (0, 0))
    svec = pl.BlockSpec((None, 1, d), lambda i: (_stream(i), 0, 0))
    return _call(
        body, name=name, grid=(t // tm,),
        in_specs=[row, vec, _mod_spec(ish, d), _mod_spec(isc, d), row, row],
        out_specs=[row, vec, svec, svec],
        out_shape=[jax.ShapeDtypeStruct((t, d), F32), jax.ShapeDtypeStruct((1, d), F32),
                   jax.ShapeDtypeStruct((2, 1, d), F32), jax.ShapeDtypeStruct((2, 1, d), F32)],
        sem=("arbitrary",), args=(x, g, mod4, mod4, dh, dres), comm=comm)


def _gate_res_fwd(x, f, mod4, chunk, name):
    t, d = x.shape
    tm = _tm()

    def body(x_ref, f_ref, g_ref, o_ref):
        o_ref[...] = x_ref[...] + g_ref[...] * f_ref[...]

    row = pl.BlockSpec((tm, d), lambda i: (i, 0))
    return pl.pallas_call(body, name=name, grid=(t // tm,), in_specs=[row, row, _mod_spec(chunk, d)], out_specs=row,
                          out_shape=jax.ShapeDtypeStruct((t, d), F32), compiler_params=_params(("parallel",)))(x, f, mod4)


def _gate_res_bwd(dx, f, mod4, chunk, name):
    t, d = dx.shape
    tm = _tm()

    def body(dx_ref, f_ref, g_ref, o_ref, dg_ref):
        i = pl.program_id(0)
        dxv = dx_ref[...]
        o_ref[...] = (dxv * g_ref[...]).astype(BF16)

        @pl.when(i <= 1)
        def _():
            dg_ref[...] = jnp.zeros_like(dg_ref)

        dg_ref[...] += jnp.sum(dxv * f_ref[...], axis=0, keepdims=True)

    row = pl.BlockSpec((tm, d), lambda i: (i, 0))
    return pl.pallas_call(
        body, name=name, grid=(t // tm,), in_specs=[row, row, _mod_spec(chunk, d)],
        out_specs=[row, pl.BlockSpec((None, 1, d), lambda i: (_stream(i), 0, 0))],
        out_shape=[jax.ShapeDtypeStruct((t, d), BF16), jax.ShapeDtypeStruct((2, 1, d), F32)],
        compiler_params=_params(("arbitrary",)))(dx, f, mod4)


def _loss_head(x, final_g, target, name):
    t, d = x.shape
    tm = _tm()

    def loss_fn(xv, g, tgt):
        y = (xv * lax.rsqrt(jnp.mean(xv * xv, axis=-1, keepdims=True) + EPS)) * g
        err = y - tgt
        return 0.5 * jnp.sum(jnp.mean(err * err, axis=-1, keepdims=True))

    def body(x_ref, g_ref, t_ref, l_ref, dx_ref, dg_ref):
        i = pl.program_id(0)

        @pl.when(i == 0)
        def _():
            l_ref[...] = jnp.zeros_like(l_ref)
            dg_ref[...] = jnp.zeros_like(dg_ref)
            dx_ref[...] = jnp.zeros_like(dx_ref)

        @pl.when(i > 0)
        def _():
            l, (dx, dg) = jax.value_and_grad(loss_fn, argnums=(0, 1))(x_ref[...], g_ref[...], t_ref[...])
            l_ref[...] += jnp.full(l_ref.shape, l, F32)
            dx_ref[...] = dx
            dg_ref[...] += dg

    row = pl.BlockSpec((tm, d), lambda i: (i, 0))
    vec = pl.BlockSpec((1, d), lambda i: (0, 0))
    return pl.pallas_call(
        body, name=name, grid=(t // tm,),
        in_specs=[row, vec, pl.BlockSpec((tm, d), lambda i: (jnp.maximum(i - 1, 0), 0))],
        out_specs=[pl.BlockSpec((SUBLANES, LANES), lambda i: (0, 0)), row, vec],
        out_shape=[jax.ShapeDtypeStruct((SUBLANES, LANES), F32), jax.ShapeDtypeStruct((t, d), F32),
                   jax.ShapeDtypeStruct((1, d), F32)],
        compiler_params=_params(("arbitrary",)))(x, final_g, target)


def _swap_quarters(x):
    half, nf = RET_DK // 2, RET_DK // 4
    lane = lax.broadcasted_iota(jnp.int32, x.shape, 1)
    return jnp.where((lane % half) < nf, pltpu.roll(x, RET_DK - nf, 1), pltpu.roll(x, nf, 1))


def _rope(x, cos, sin):
    return x * cos + _swap_quarters(x) * sin


def _rope_t(y, cos, sin):
    return y * cos + _swap_quarters(y * sin)


def _ret_consts(d):
    c = RET_CHUNK
    ii = lax.broadcasted_iota(jnp.int32, (c, 1), 0).astype(F32)
    jj = lax.broadcasted_iota(jnp.int32, (1, c), 1).astype(F32)
    fwd = d == 0
    sgn = jnp.where(fwd, 1.0, -1.0).astype(F32)
    pos = jnp.where(fwd, ii, c - 1.0 - ii)
    return sgn * (ii - jj), pos


def _ret_step(lgt, state, q, k, v, diff, pos):
    c = float(RET_CHUNK)
    lg = -(jnp.maximum(-lgt, 0.0) + jnp.log1p(jnp.exp(-jnp.abs(lgt))))
    lower = diff >= 0
    decay = jnp.where(lower, jnp.exp(jnp.where(lower, diff, 0.0) * lg), 0.0)
    xi = jnp.exp((pos + 1.0) * lg)
    zeta = jnp.exp((c - 1.0 - pos) * lg)
    gch = jnp.exp(c * lg)
    inner = dot_nt(q, k) * decay
    out = dot_nn(inner, v) + dot_nn(q, state) * xi
    new_state = state * gch + dot_tn(k * zeta, v)
    return out, new_state


def _chunk_order():
    nc, nch = CTX_LEN // RET_CHUNK, _t_rows() // RET_CHUNK
    fwd = list(range(nch))
    bwd = list(range(nc - 1, -1, -1)) + list(range(nch - 1, nc - 1, -1))
    return jnp.asarray(np.array([fwd, bwd], np.int32))


def _ret_fwd(p, cos, sin, decay, order, name):
    t = p.shape[0]
    c, dk, dv, nh = RET_CHUNK, RET_DK, RET_DV, RET_HEADS
    nch = t // c
    off = _offsets()
    wqk, wv = nh * dk, nh * dv
    assert off["q"] % wqk == 0 and off["k"] % wqk == 0 and off["v"] % wv == 0
    qb, kb, vb = off["q"] // wqk, off["k"] // wqk, off["v"] // wv
    scale = RET_DK ** -0.5

    def body(ord_ref, dec_ref, q_ref, k_ref, v_ref, cos_ref, sin_ref, o_ref, st_ref, state):
        d, s = pl.program_id(0), pl.program_id(1)

        @pl.when(s == 0)
        def _():
            state[...] = jnp.zeros_like(state)

        diff, pos = _ret_consts(d)
        cosv, sinv = cos_ref[...], sin_ref[...]
        for h in range(nh):
            st = state[h]
            st_ref[h] = st
            lgt = jnp.full((1, 1), dec_ref[d, h], F32)
            q = _rope(q_ref[:, h * dk:(h + 1) * dk], cosv, sinv) * scale
            k = _rope(k_ref[:, h * dk:(h + 1) * dk], cosv, sinv)
            out, ns = _ret_step(lgt, st, q, k, v_ref[:, h * dv:(h + 1) * dv], diff, pos)
            o_ref[:, h * dv:(h + 1) * dv] = out
            state[h] = ns

    grid_spec = pltpu.PrefetchScalarGridSpec(
        num_scalar_prefetch=1, grid=(2, nch),
        in_specs=[pl.BlockSpec(memory_space=pltpu.SMEM),
                  pl.BlockSpec((c, wqk), lambda d, s, o: (o[d, s], qb)),
                  pl.BlockSpec((c, wqk), lambda d, s, o: (o[d, s], kb)),
                  pl.BlockSpec((c, wv), lambda d, s, o: (o[d, s], vb)),
                  pl.BlockSpec((c, dk), lambda d, s, o: (o[d, s], 0)),
                  pl.BlockSpec((c, dk), lambda d, s, o: (o[d, s], 0))],
        out_specs=[pl.BlockSpec((None, c, wv), lambda d, s, o: (d, o[d, s], 0)),
                   pl.BlockSpec((None, nh, None, dk, dv), lambda d, s, o: (d, 0, s, 0, 0))],
        scratch_shapes=[pltpu.VMEM((nh, dk, dv), F32)])
    return pl.pallas_call(
        body, name=name, grid_spec=grid_spec,
        out_shape=[jax.ShapeDtypeStruct((2, t, wv), F32), jax.ShapeDtypeStruct((2, nh, nch, dk, dv), F32)],
        compiler_params=_params(("arbitrary", "arbitrary")))(order, decay, p, p, p, cos, sin)


def _ret_bwd(p, cos, sin, decay, order, states, do, name):
    t = p.shape[0]
    c, dk, dv, nh = RET_CHUNK, RET_DK, RET_DV, RET_HEADS
    nch = t // c
    off = _offsets()
    wqk, wv = nh * dk, nh * dv
    qb, kb, vb = off["q"] // wqk, off["k"] // wqk, off["v"] // wv
    scale = RET_DK ** -0.5

    def body(ord_ref, dec_ref, q_ref, k_ref, v_ref, cos_ref, sin_ref, st_ref, do_ref,
             dq_ref, dk_ref, dv_ref, dd_ref, dstate):
        d, s = pl.program_id(0), pl.program_id(1)

        @pl.when(s == 0)
        def _():
            dstate[...] = jnp.zeros_like(dstate)
            dd_ref[...] = jnp.zeros_like(dd_ref)

        diff, pos = _ret_consts(d)
        cosv, sinv = cos_ref[...], sin_ref[...]
        for h in range(nh):
            qk, vv = slice(h * dk, (h + 1) * dk), slice(h * dv, (h + 1) * dv)
            lgt = jnp.full((1, 1), dec_ref[d, h], F32)
            q = _rope(q_ref[:, qk], cosv, sinv) * scale
            k = _rope(k_ref[:, qk], cosv, sinv)
            _, vjp = jax.vjp(lambda a, b, cq, ck, cv: _ret_step(a, b, cq, ck, cv, diff, pos),
                             lgt, st_ref[h], q, k, v_ref[:, vv])
            dlgt, dst, dq, dkk, dvv = vjp((do_ref[:, vv], dstate[h]))
            dstate[h] = dst
            dq_ref[:, qk] = _rope_t(dq * scale, cosv, sinv)
            dk_ref[:, qk] = _rope_t(dkk, cosv, sinv)
            dv_ref[:, vv] = dvv
            dd_ref[h] += jnp.broadcast_to(dlgt, (SUBLANES, LANES))

    rev = lambda o, d, s: o[d, nch - 1 - s]
    grid_spec = pltpu.PrefetchScalarGridSpec(
        num_scalar_prefetch=1, grid=(2, nch),
        in_specs=[pl.BlockSpec(memory_space=pltpu.SMEM),
                  pl.BlockSpec((c, wqk), lambda d, s, o: (rev(o, d, s), qb)),
                  pl.BlockSpec((c, wqk), lambda d, s, o: (rev(o, d, s), kb)),
                  pl.BlockSpec((c, wv), lambda d, s, o: (rev(o, d, s), vb)),
                  pl.BlockSpec((c, dk), lambda d, s, o: (rev(o, d, s), 0)),
                  pl.BlockSpec((c, dk), lambda d, s, o: (rev(o, d, s), 0)),
                  pl.BlockSpec((None, nh, None, dk, dv), lambda d, s, o: (d, 0, nch - 1 - s, 0, 0)),
                  pl.BlockSpec((c, wv), lambda d, s, o: (rev(o, d, s), 0))],
        out_specs=[pl.BlockSpec((None, c, wqk), lambda d, s, o: (d, rev(o, d, s), 0)),
                   pl.BlockSpec((None, c, wqk), lambda d, s, o: (d, rev(o, d, s), 0)),
                   pl.BlockSpec((None, c, wv), lambda d, s, o: (d, rev(o, d, s), 0)),
                   pl.BlockSpec((None, nh, SUBLANES, LANES), lambda d, s, o: (d, 0, 0, 0))],
        scratch_shapes=[pltpu.VMEM((nh, dk, dv), F32)])
    return pl.pallas_call(
        body, name=name, grid_spec=grid_spec,
        out_shape=[jax.ShapeDtypeStruct((2, t, wqk), F32), jax.ShapeDtypeStruct((2, t, wqk), F32),
                   jax.ShapeDtypeStruct((2, t, wv), F32), jax.ShapeDtypeStruct((2, nh, SUBLANES, LANES), F32)],
        compiler_params=_params(("arbitrary", "arbitrary")))(order, decay, p, p, p, cos, sin, states, do)


def _ggn_head(of, ob, gate, g):
    o = of + ob
    mu = jnp.mean(o, axis=-1, keepdims=True)
    var = jnp.mean(jnp.square(o - mu), axis=-1, keepdims=True)
    return ((o - mu) * lax.rsqrt(var + EPS) * g) * _silu(gate)


def _ggn_fwd(o2, p, gn_g, name):
    t = p.shape[0]
    tm, w, dv = _tm(), _ret_w(), RET_DV
    gb = _offsets()["g"] // w

    def body(o_ref, gate_ref, g_ref, out_ref):
        for h in range(RET_HEADS):
            sl = slice(h * dv, (h + 1) * dv)
            out_ref[:, sl] = _ggn_head(o_ref[0, :, sl], o_ref[1, :, sl], gate_ref[:, sl], g_ref[:, sl]).astype(BF16)

    return pl.pallas_call(
        body, name=name, grid=(t // tm,),
        in_specs=[pl.BlockSpec((2, tm, w), lambda i: (0, i, 0)), pl.BlockSpec((tm, w), lambda i: (i, gb)),
                  pl.BlockSpec((1, w), lambda i: (0, 0))],
        out_specs=pl.BlockSpec((tm, w), lambda i: (i, 0)), out_shape=jax.ShapeDtypeStruct((t, w), BF16),
        compiler_params=_params(("parallel",)))(o2, p, gn_g)


def _ggn_bwd(o2, p, gn_g, dmix, name):
    t = p.shape[0]
    tm, w, dv = _tm(), _ret_w(), RET_DV
    gb = _offsets()["g"] // w

    def body(o_ref, gate_ref, g_ref, dy_ref, do_ref, dgate_ref, dg_ref):
        i = pl.program_id(0)

        @pl.when(i == 0)
        def _():
            dg_ref[...] = jnp.zeros_like(dg_ref)

        for h in range(RET_HEADS):
            sl = slice(h * dv, (h + 1) * dv)
            _, vjp = jax.vjp(_ggn_head, o_ref[0, :, sl], o_ref[1, :, sl], gate_ref[:, sl], g_ref[:, sl])
            do, _, dgate, dg = vjp(dy_ref[:, sl])
            do_ref[:, sl] = do
            dgate_ref[:, sl] = dgate
            dg_ref[:, sl] += dg

    row = pl.BlockSpec((tm, w), lambda i: (i, 0))
    return pl.pallas_call(
        body, name=name, grid=(t // tm,),
        in_specs=[pl.BlockSpec((2, tm, w), lambda i: (0, i, 0)), pl.BlockSpec((tm, w), lambda i: (i, gb)),
                  pl.BlockSpec((1, w), lambda i: (0, 0)), row],
        out_specs=[row, row, pl.BlockSpec((1, w), lambda i: (0, 0))],
        out_shape=[jax.ShapeDtypeStruct((t, w), F32), jax.ShapeDtypeStruct((t, w), F32),
                   jax.ShapeDtypeStruct((1, w), F32)],
        compiler_params=_params(("arbitrary",)))(o2, p, gn_g, dmix)


def _halo(k):
    return SUBLANES * ((k // 2 + SUBLANES - 1) // SUBLANES)


def _halo_specs(width, colblock, h, tm):
    r = tm // h
    return [pl.BlockSpec((h, width), lambda i, *_: (jnp.maximum(i * r - 1, 0), colblock(*_))),
            pl.BlockSpec((tm, width), lambda i, *_: (i, colblock(*_))),
            pl.BlockSpec((h, width), lambda i, *_: (jnp.minimum((i + 1) * r, (_t_rows() // h) - 1), colblock(*_)))]


def _fill_ext(ext_ref, prev, cur, nxt, i, h, tm):
    nt = _t_rows() // tm
    ext_ref[0:h, :] = jnp.where(i >= 2, prev, 0.0)
    ext_ref[h:h + tm, :] = cur
    ext_ref[h + tm:h + tm + h, :] = jnp.where((i >= 1) & (i <= nt - 2), nxt, 0.0)


def _corr(ext_ref, w_ref, k, h, tm, flip):
    pad = k // 2
    acc = None
    for kk in range(k):
        o = h + (pad - kk if flip else kk - pad)
        term = w_ref[kk:kk + 1, :] * ext_ref[o:o + tm, :]
        acc = term if acc is None else acc + term
    return acc


def _conv_post(u2, ln_g, ln_b, pw):
    mu = jnp.mean(u2, axis=-1, keepdims=True)
    var = jnp.mean(jnp.square(u2 - mu), axis=-1, keepdims=True)
    y = (u2 - mu) * lax.rsqrt(var + EPS) * ln_g + ln_b
    return dot_nn(_silu(y), pw)


def _conv_fwd(p, dw_w, dw_b, ln_g, ln_b, pw, name):
    t = p.shape[0]
    tm, w, k = _tm(), CONV_W, CONV_K
    h = _halo(k)
    off = _offsets()
    ab, bb = off["a"] // w, off["b"] // w

    def body(ap, ac, an, bp, bc, bn, w_ref, b_ref, g_ref, beta_ref, pw_ref, u2_ref, out_ref, ext):
        i = pl.program_id(0)
        glu = lambda a, b: a * _sigmoid(b)
        _fill_ext(ext, glu(ap[...], bp[...]), glu(ac[...], bc[...]), glu(an[...], bn[...]), i, h, tm)
        u2 = _corr(ext, w_ref, k, h, tm, False) + b_ref[...]
        u2_ref[...] = u2
        out_ref[...] = _conv_post(u2, g_ref[...], beta_ref[...], pw_ref[...]).astype(BF16)

    vec = pl.BlockSpec((1, w), lambda i: (0, 0))
    row = pl.BlockSpec((tm, w), lambda i: (i, 0))
    return pl.pallas_call(
        body, name=name, grid=(t // tm,),
        in_specs=_halo_specs(w, lambda: ab, h, tm) + _halo_specs(w, lambda: bb, h, tm)
        + [pl.BlockSpec((k, w), lambda i: (0, 0)), vec, vec, vec, pl.BlockSpec((w, w), lambda i: (0, 0))],
        out_specs=[row, row],
        out_shape=[jax.ShapeDtypeStruct((t, w), F32), jax.ShapeDtypeStruct((t, w), BF16)],
        scratch_shapes=[pltpu.VMEM((tm + 2 * h, w), F32)],
        compiler_params=_params(("parallel",)))(p, p, p, p, p, p, dw_w, dw_b, ln_g, ln_b, pw)


def _conv_bwd1(u2, dmix, ln_g, ln_b, pw, name):
    t = u2.shape[0]
    tm, w = _tm(), CONV_W
    cb = _ret_w() // w

    def body(u2_ref, dy_ref, g_ref, beta_ref, pw_ref, du2_ref, dg_ref, db_ref, dpw_ref):
        i = pl.program_id(0)

        @pl.when(i == 0)
        def _():
            dg_ref[...] = jnp.zeros_like(dg_ref)
            db_ref[...] = jnp.zeros_like(db_ref)
            dpw_ref[...] = jnp.zeros_like(dpw_ref)

        _, vjp = jax.vjp(_conv_post, u2_ref[...], g_ref[...], beta_ref[...], pw_ref[...])
        du2, dg, db, dpw = vjp(dy_ref[...])
        du2_ref[...] = du2
        dg_ref[...] += dg
        db_ref[...] += db
        dpw_ref[...] += dpw

    vec = pl.BlockSpec((1, w), lambda i: (0, 0))
    row = pl.BlockSpec((tm, w), lambda i: (i, 0))
    mat = pl.BlockSpec((w, w), lambda i: (0, 0))
    return pl.pallas_call(
        body, name=name, grid=(t // tm,),
        in_specs=[row, pl.BlockSpec((tm, w), lambda i: (i, cb)), vec, vec, mat],
        out_specs=[row, vec, vec, mat],
        out_shape=[jax.ShapeDtypeStruct((t, w), F32), jax.ShapeDtypeStruct((1, w), F32),
                   jax.ShapeDtypeStruct((1, w), F32), jax.ShapeDtypeStruct((w, w), F32)],
        compiler_params=_params(("arbitrary",)))(u2, dmix, ln_g, ln_b, pw)


def _conv_bwd2(du2, p, dw_w, name, comm=None):
    t = p.shape[0]
    tm, w, k = _tm(), CONV_W, CONV_K
    h = _halo(k)
    pad = k // 2
    off = _offsets()
    ab, bb = off["a"] // w, off["b"] // w

    def body(dp, dc, dn, ap, ac, an, bp, bc, bn, w_ref, da_ref, db_ref, dw_ref, dbias_ref, ext_d, ext_u):
        i = pl.program_id(0)

        @pl.when(i == 0)
        def _():
            dw_ref[...] = jnp.zeros_like(dw_ref)
            dbias_ref[...] = jnp.zeros_like(dbias_ref)

        glu = lambda a, b: a * _sigmoid(b)
        a, b, d = ac[...], bc[...], dc[...]
        _fill_ext(ext_d, dp[...], d, dn[...], i, h, tm)
        _fill_ext(ext_u, glu(ap[...], bp[...]), glu(a, b), glu(an[...], bn[...]), i, h, tm)
        du = _corr(ext_d, w_ref, k, h, tm, True)
        sg = _sigmoid(b)
        da_ref[...] = du * sg
        db_ref[...] = du * a * sg * (1.0 - sg)
        dbias_ref[...] += jnp.sum(d, axis=0, keepdims=True)
        for kk in range(k):
            o = h + kk - pad
            dw_ref[kk:kk + 1, :] += jnp.sum(d * ext_u[o:o + tm, :], axis=0, keepdims=True)

    vec = pl.BlockSpec((1, w), lambda i: (0, 0))
    row = pl.BlockSpec((tm, w), lambda i: (i, 0))
    kw = pl.BlockSpec((k, w), lambda i: (0, 0))
    return _call(
        body, name=name, grid=(t // tm,),
        in_specs=_halo_specs(w, lambda: 0, h, tm) + _halo_specs(w, lambda: ab, h, tm)
        + _halo_specs(w, lambda: bb, h, tm) + [kw],
        out_specs=[row, row, kw, vec],
        out_shape=[jax.ShapeDtypeStruct((t, w), F32), jax.ShapeDtypeStruct((t, w), F32),
                   jax.ShapeDtypeStruct((k, w), F32), jax.ShapeDtypeStruct((1, w), F32)],
        scratch=[pltpu.VMEM((tm + 2 * h, w), F32), pltpu.VMEM((tm + 2 * h, w), F32)],
        sem=("arbitrary",), args=(du2, du2, du2, p, p, p, p, p, p, dw_w), comm=comm)


def _ffn_tc():
    return _tile(D_FF, 512, LANES)


def _ffn_act_fwd(u, dw_w, dw_b, name):
    t = u.shape[0]
    tm, k, tc = _tm(), FFN_K, _ffn_tc()
    h = _halo(k)
    nj = D_FF // tc

    def body(vp, vc, vn, gp, gc, gn, wv, wg, bv, bg, out_ref, ext_v, ext_g):
        i = pl.program_id(0)
        _fill_ext(ext_v, vp[...], vc[...], vn[...], i, h, tm)
        _fill_ext(ext_g, gp[...], gc[...], gn[...], i, h, tm)
        for r0 in range(0, tm, ROW_CHUNK):
            val = _corr(ext_v, wv, k, h + r0, ROW_CHUNK, False) + bv[...]
            gate = _corr(ext_g, wg, k, h + r0, ROW_CHUNK, False) + bg[...]
            out_ref[r0:r0 + ROW_CHUNK, :] = (_silu(gate) * val).astype(BF16)

    wspec = lambda s: pl.BlockSpec((k, tc), lambda i, j: (0, j + s))
    bspec = lambda s: pl.BlockSpec((1, tc), lambda i, j: (0, j + s))
    return pl.pallas_call(
        body, name=name, grid=(t // tm, nj),
        in_specs=_halo_specs(tc, lambda j: j, h, tm) + _halo_specs(tc, lambda j: j + nj, h, tm)
        + [wspec(0), wspec(nj), bspec(0), bspec(nj)],
        out_specs=pl.BlockSpec((tm, tc), lambda i, j: (i, j)),
        out_shape=jax.ShapeDtypeStruct((t, D_FF), BF16),
        scratch_shapes=[pltpu.VMEM((tm + 2 * h, tc), F32), pltpu.VMEM((tm + 2 * h, tc), F32)],
        compiler_params=_params(("parallel", "parallel")))(u, u, u, u, u, u, dw_w, dw_w, dw_b, dw_b)


def _ffn_act_bwd1(u, da, dw_w, dw_b, name):
    t = u.shape[0]
    tm, k, tc = _tm(), FFN_K, _ffn_tc()
    h = _halo(k)
    nj = D_FF // tc

    def body(vp, vc, vn, gp, gc, gn, wv, wg, bv, bg, da_ref, dv_ref, dg_ref, ext_v, ext_g):
        i = pl.program_id(0)
        _fill_ext(ext_v, vp[...], vc[...], vn[...], i, h, tm)
        _fill_ext(ext_g, gp[...], gc[...], gn[...], i, h, tm)
        for r0 in range(0, tm, ROW_CHUNK):
            rows = slice(r0, r0 + ROW_CHUNK)
            val = _corr(ext_v, wv, k, h + r0, ROW_CHUNK, False) + bv[...]
            gate = _corr(ext_g, wg, k, h + r0, ROW_CHUNK, False) + bg[...]
            _, vjp = jax.vjp(lambda a, b: _silu(b) * a, val, gate)
            dval, dgate = vjp(da_ref[rows, :])
            dv_ref[rows, :] = dval
            dg_ref[rows, :] = dgate

    wspec = lambda s: pl.BlockSpec((k, tc), lambda i, j: (0, j + s))
    bspec = lambda s: pl.BlockSpec((1, tc), lambda i, j: (0, j + s))
    dc = pl.pallas_call(
        body, name=name, grid=(t // tm, nj),
        in_specs=_halo_specs(tc, lambda j: j, h, tm) + _halo_specs(tc, lambda j: j + nj, h, tm)
        + [wspec(0), wspec(nj), bspec(0), bspec(nj), pl.BlockSpec((tm, tc), lambda i, j: (i, j))],
        out_specs=[pl.BlockSpec((tm, tc), lambda i, j: (i, j)), pl.BlockSpec((tm, tc), lambda i, j: (i, j))],
        out_shape=[jax.ShapeDtypeStruct((t, D_FF), F32), jax.ShapeDtypeStruct((t, D_FF), F32)],
        scratch_shapes=[pltpu.VMEM((tm + 2 * h, tc), F32), pltpu.VMEM((tm + 2 * h, tc), F32)],
        compiler_params=_params(("parallel", "parallel")))(u, u, u, u, u, u, dw_w, dw_w, dw_b, dw_b, da)
    return dc


def _dwconv_bwd(dc, u, dw_w, colblock, name):
    t = u.shape[0]
    tm, k, tc = _tm(), FFN_K, _ffn_tc()
    h = _halo(k)
    pad = k // 2
    nj = D_FF // tc

    def body(dp, dcur, dn, up, uc, un, w_ref, du_ref, dw_ref, dbias_ref, ext_d, ext_u):
        i = pl.program_id(1)

        @pl.when(i == 0)
        def _():
            dw_ref[...] = jnp.zeros_like(dw_ref)
            dbias_ref[...] = jnp.zeros_like(dbias_ref)

        _fill_ext(ext_d, dp[...], dcur[...], dn[...], i, h, tm)
        _fill_ext(ext_u, up[...], uc[...], un[...], i, h, tm)
        acc_b = jnp.zeros((ROW_CHUNK, tc), F32)
        acc_w = [jnp.zeros((ROW_CHUNK, tc), F32) for _ in range(k)]
        for r0 in range(0, tm, ROW_CHUNK):
            d = ext_d[h + r0:h + r0 + ROW_CHUNK, :]
            du_ref[r0:r0 + ROW_CHUNK, :] = _corr(ext_d, w_ref, k, h + r0, ROW_CHUNK, True).astype(BF16)
            acc_b = acc_b + d
            for kk in range(k):
                o = h + r0 + kk - pad
                acc_w[kk] = acc_w[kk] + d * ext_u[o:o + ROW_CHUNK, :]
        dbias_ref[...] += jnp.sum(acc_b, axis=0, keepdims=True)
        for kk in range(k):
            dw_ref[kk:kk + 1, :] += jnp.sum(acc_w[kk], axis=0, keepdims=True)

    def hs(cb):
        r = tm // h
        return [pl.BlockSpec((h, tc), lambda j, i: (jnp.maximum(i * r - 1, 0), cb(j))),
                pl.BlockSpec((tm, tc), lambda j, i: (i, cb(j))),
                pl.BlockSpec((h, tc), lambda j, i: (jnp.minimum((i + 1) * r, (_t_rows() // h) - 1), cb(j)))]

    return pl.pallas_call(
        body, name=name, grid=(nj, t // tm),
        in_specs=hs(lambda j: j) + hs(lambda j: j + colblock) + [pl.BlockSpec((k, tc), lambda j, i: (0, j + colblock))],
        out_specs=[pl.BlockSpec((tm, tc), lambda j, i: (i, j)), pl.BlockSpec((k, tc), lambda j, i: (0, j)),
                   pl.BlockSpec((1, tc), lambda j, i: (0, j))],
        out_shape=[jax.ShapeDtypeStruct((t, D_FF), BF16), jax.ShapeDtypeStruct((k, D_FF), F32),
                   jax.ShapeDtypeStruct((1, D_FF), F32)],
        scratch_shapes=[pltpu.VMEM((tm + 2 * h, tc), F32), pltpu.VMEM((tm + 2 * h, tc), F32)],
        compiler_params=_params(("parallel", "arbitrary")))(dc, dc, dc, u, u, u, dw_w)


def _na_geometry(rq):
    ncb = CTX_LEN // GRID_W
    rows_n = SEQ // GRID_W
    r = jnp.maximum(rq - ncb, 0)
    kstart = jnp.clip(r - NA_ROWS // 2, 0, rows_n - NA_ROWS)
    base = kstart - r + NA_ROWS - 1
    return rq >= ncb, kstart, base


def _na_core(q, kl, vl, kc, vc, bias, mask):
    qs = q * (NA_DH ** -0.5)
    s_l = jnp.where(mask, dot_nt(qs, kl) + bias, NEG)
    s_c = dot_nt(qs, kc)
    m = lax.stop_gradient(jnp.maximum(jnp.max(s_l, axis=1, keepdims=True), jnp.max(s_c, axis=1, keepdims=True)))
    e_l, e_c = jnp.exp(s_l - m), jnp.exp(s_c - m)
    inv = 1.0 / (jnp.sum(e_l, axis=1, keepdims=True) + jnp.sum(e_c, axis=1, keepdims=True))
    return dot_nn(e_l * inv, vl) + dot_nn(e_c * inv, vc)


def _na_mask(is_lat):
    nl = NA_ROWS * GRID_W
    q = lax.broadcasted_iota(jnp.int32, (GRID_W, nl), 0)
    w = lax.broadcasted_iota(jnp.int32, (GRID_W, nl), 1) % GRID_W
    cs = jnp.clip(q - NA_COLS // 2, 0, GRID_W - NA_COLS)
    return (w >= cs) & (w < cs + NA_COLS) & is_lat


def _na_bias(rb_ref):
    assert 2 * GRID_W == LANES
    lane = lax.broadcasted_iota(jnp.int32, (GRID_W, LANES), 1)
    tiles = []
    for kp in range(NA_ROWS // 2):
        ev = jnp.broadcast_to(rb_ref[2 * kp:2 * kp + 1, :], (GRID_W, LANES))
        od = jnp.broadcast_to(rb_ref[2 * kp + 1:2 * kp + 2, :], (GRID_W, LANES))
        ev = pltpu.roll(ev, LANES - (NA_COLS - 1), 1, stride=1, stride_axis=0)
        od = pltpu.roll(od, LANES - (NA_COLS - 1) - GRID_W, 1, stride=1, stride_axis=0)
        tiles.append(jnp.where(lane < GRID_W, ev, od))
    return jnp.concatenate(tiles, axis=1)


def _na_dbias(dbias, drb_ref):
    qi = lax.broadcasted_iota(jnp.int32, (GRID_W, GRID_W), 0)
    qj = lax.broadcasted_iota(jnp.int32, (GRID_W, GRID_W), 1)
    flip = (qi + qj == GRID_W - 1).astype(F32)
    rev = lax.dot_general(flip, dbias, (((1,), (0,)), ((), ())), precision=lax.Precision.HIGHEST,
                          preferred_element_type=F32)
    lane = lax.broadcasted_iota(jnp.int32, (GRID_W, LANES), 1)
    s_ev = LANES - (GRID_W - NA_COLS)
    for kp in range(NA_ROWS // 2):
        tile = rev[:, kp * LANES:(kp + 1) * LANES]
        ev = pltpu.roll(jnp.where(lane < GRID_W, tile, 0.0), s_ev, 1, stride=1, stride_axis=0)
        od = pltpu.roll(jnp.where(lane >= GRID_W, tile, 0.0), s_ev - GRID_W, 1, stride=1, stride_axis=0)
        drb_ref[2 * kp:2 * kp + 1, :] += jnp.sum(ev, axis=0, keepdims=True)
        drb_ref[2 * kp + 1:2 * kp + 2, :] += jnp.sum(od, axis=0, keepdims=True)


def _na_hps():
    return 2 if NA_HEADS % 2 == 0 else 1


def _na_specs(p_offsets):
    t = _t_rows()
    hps = _na_hps()
    wd = hps * NA_DH
    assert all(p_offsets[n] % wd == 0 for n in ("nq", "nk", "nv"))
    qb, kb, vb = (p_offsets[n] // wd for n in ("nq", "nk", "nv"))
    return [pl.BlockSpec((GRID_W, wd), lambda h, r: (r, qb + h)),
            pl.BlockSpec((t, wd), lambda h, r: (0, kb + h)),
            pl.BlockSpec((t, wd), lambda h, r: (0, vb + h)),
            pl.BlockSpec((hps, None, NA_ROWS, LANES), lambda h, r: (h, _na_geometry(r)[2], 0, 0))]


def _na_fwd(p, rb, name, comm=None):
    t = p.shape[0]
    dh, nl = NA_DH, NA_ROWS * GRID_W

    hps = _na_hps()

    def body(q_ref, k_ref, v_ref, rb_ref, out_ref):
        rq = pl.program_id(1)
        is_lat, kstart, _ = _na_geometry(rq)
        start = pl.multiple_of(CTX_LEN + kstart * GRID_W, GRID_W)
        mask = _na_mask(is_lat)
        for hh in range(hps):
            cols = slice(hh * dh, (hh + 1) * dh)
            out = _na_core(q_ref[:, cols], k_ref[pl.ds(start, nl), cols], v_ref[pl.ds(start, nl), cols],
                           k_ref[0:CTX_LEN, cols], v_ref[0:CTX_LEN, cols], _na_bias(rb_ref.at[hh]), mask)
            out_ref[:, cols] = out.astype(BF16)

    res, cres = _call(
        body, name=name, grid=(NA_HEADS // hps, t // GRID_W), in_specs=_na_specs(_offsets()),
        out_specs=[pl.BlockSpec((GRID_W, hps * dh), lambda h, r: (r, h))],
        out_shape=[jax.ShapeDtypeStruct((t, _na_w()), BF16)],
        sem=("parallel", "arbitrary"), args=(p, p, p, rb), comm=comm)
    return res[0], cres


def _na_bwd(p, rb, dmix, name, comm=None):
    t = p.shape[0]
    dh, nl = NA_DH, NA_ROWS * GRID_W

    hps = _na_hps()
    wd = hps * dh
    assert ((_ret_w() + CONV_W) // dh) % hps == 0
    ob = (_ret_w() + CONV_W) // wd

    def body(q_ref, k_ref, v_ref, rb_ref, dy_ref, dq_ref, dk_ref, dv_ref, drb_ref):
        rq = pl.program_id(1)
        is_lat, kstart, base = _na_geometry(rq)
        _, _, prev_base = _na_geometry(rq - 1)
        start = pl.multiple_of(CTX_LEN + kstart * GRID_W, GRID_W)

        @pl.when(rq == 0)
        def _():
            dk_ref[...] = jnp.zeros_like(dk_ref)
            dv_ref[...] = jnp.zeros_like(dv_ref)

        @pl.when((rq == 0) | (base != prev_base))
        def _():
            drb_ref[...] = jnp.zeros_like(drb_ref)

        mask = _na_mask(is_lat)
        for hh in range(hps):
            cols = slice(hh * dh, (hh + 1) * dh)
            _, vjp = jax.vjp(lambda *a: _na_core(*a, mask), q_ref[:, cols], k_ref[pl.ds(start, nl), cols],
                             v_ref[pl.ds(start, nl), cols], k_ref[0:CTX_LEN, cols], v_ref[0:CTX_LEN, cols],
                             _na_bias(rb_ref.at[hh]))
            dq, dkl, dvl, dkc, dvc, dbias = vjp(dy_ref[:, cols])
            dq_ref[:, cols] = dq
            dk_ref[pl.ds(start, nl), cols] += dkl
            dv_ref[pl.ds(start, nl), cols] += dvl
            dk_ref[0:CTX_LEN, cols] += dkc
            dv_ref[0:CTX_LEN, cols] += dvc
            _na_dbias(dbias, drb_ref.at[hh])

    return _call(
        body, name=name, grid=(NA_HEADS // hps, t // GRID_W),
        in_specs=_na_specs(_offsets()) + [pl.BlockSpec((GRID_W, wd), lambda h, r: (r, ob + h))],
        out_specs=[pl.BlockSpec((GRID_W, wd), lambda h, r: (r, h)), pl.BlockSpec((t, wd), lambda h, r: (0, h)),
                   pl.BlockSpec((t, wd), lambda h, r: (0, h)),
                   pl.BlockSpec((hps, None, NA_ROWS, LANES), lambda h, r: (h, _na_geometry(r)[2], 0, 0))],
        out_shape=[jax.ShapeDtypeStruct((t, _na_w()), F32), jax.ShapeDtypeStruct((t, _na_w()), F32),
                   jax.ShapeDtypeStruct((t, _na_w()), F32),
                   jax.ShapeDtypeStruct((NA_HEADS, NA_ROWS, NA_ROWS, LANES), F32)],
        sem=("parallel", "arbitrary"), args=(p, p, p, rb, dmix), comm=comm)


def _rpb_select():
    sel = np.zeros((2 * NA_ROWS - 1, NA_ROWS * NA_ROWS), np.float32)
    for b in range(NA_ROWS):
        for kh in range(NA_ROWS):
            sel[b + kh, b * NA_ROWS + kh] = 1.0
    return jnp.asarray(sel)


def _rpb_rows(rpb):
    pad = jnp.pad(rpb, ((0, 0), (0, 0), (0, LANES - (2 * NA_COLS - 1))))
    rows = jnp.einsum("rk,hrc->hkc", _rpb_select(), pad, precision=lax.Precision.HIGHEST)
    return rows.reshape(NA_HEADS, NA_ROWS, NA_ROWS, LANES)


def _rpb_rows_t(drb):
    flat = drb.reshape(NA_HEADS, NA_ROWS * NA_ROWS, LANES)
    out = jnp.einsum("rk,hkc->hrc", _rpb_select(), flat, precision=lax.Precision.HIGHEST)
    return out[:, :, :2 * NA_COLS - 1]


def _assemble_dp(dqr, dkr, dvr, dgate, da, db, dnq, dnk, dnv, name):
    t = dgate.shape[0]
    tm = _tm()
    off = _offsets()
    sizes = dict(q=_ret_qk_w(), k=_ret_qk_w(), v=_ret_w(), g=_ret_w(), a=CONV_W, b=CONV_W, nq=_na_w(), nk=_na_w(), nv=_na_w())

    def body(q_ref, k_ref, v_ref, g_ref, a_ref, b_ref, nq_ref, nk_ref, nv_ref, o_ref):
        def put(n, val):
            o_ref[:, off[n]:off[n] + sizes[n]] = val.astype(BF16)

        put("q", q_ref[0] + q_ref[1])
        put("k", k_ref[0] + k_ref[1])
        put("v", v_ref[0] + v_ref[1])
        put("g", g_ref[...])
        put("a", a_ref[...])
        put("b", b_ref[...])
        put("nq", nq_ref[...])
        put("nk", nk_ref[...])
        put("nv", nv_ref[...])

    two = lambda w: pl.BlockSpec((2, tm, w), lambda i: (0, i, 0))
    one = lambda w: pl.BlockSpec((tm, w), lambda i: (i, 0))
    return pl.pallas_call(
        body, name=name, grid=(t // tm,),
        in_specs=[two(sizes["q"]), two(sizes["k"]), two(sizes["v"]), one(sizes["g"]), one(CONV_W), one(CONV_W),
                  one(_na_w()), one(_na_w()), one(_na_w())],
        out_specs=one(_d_in()), out_shape=jax.ShapeDtypeStruct((t, _d_in()), BF16),
        compiler_params=_params(("parallel",)))(dqr, dkr, dvr, dgate, da, db, dnq, dnk, dnv)


def _adamw(w, m, v, gs, name, comm=None):
    nl, r, c = w.shape
    stacked = not isinstance(gs, (list, tuple))
    if stacked:
        gs = [gs]
    assert stacked or len(gs) == nl
    g_n = gs[0].shape[-3]
    block_bytes = 2 * 1024 * 1024
    rows = min(block_bytes // (4 * c), block_bytes // (g_n * c * gs[0].dtype.itemsize))
    tr = _tile(r, max(2 * SUBLANES, rows // (2 * SUBLANES) * (2 * SUBLANES)), 2 * SUBLANES)
    nt = r // tr
    c1 = 1.0 - ADAM_B1 ** ADAM_STEP
    c2 = 1.0 - ADAM_B2 ** ADAM_STEP

    def body(w_ref, m_ref, v_ref, *rest):
        g_refs, (go_ref, d_ref, mo_ref, vo_ref) = rest[:len(gs)], rest[len(gs):]
        layer = pl.program_id(0)
        for ll in range(len(gs)):
            @pl.when(jnp.logical_or(stacked, layer == ll))
            def _():
                g_ref = g_refs[ll]
                g = g_ref[0].astype(F32)
                for j in range(1, g_n):
                    g = g + g_ref[j].astype(F32)
                mn = ADAM_B1 * m_ref[...] + (1.0 - ADAM_B1) * g
                vn = ADAM_B2 * v_ref[...] + (1.0 - ADAM_B2) * (g * g)
                m_hat = mn / c1
                v_hat = vn / c2
                go_ref[...] = g
                d_ref[...] = -ADAM_LR * (m_hat / (jnp.sqrt(v_hat) + ADAM_EPS) + ADAM_WD * w_ref[...])
                mo_ref[...] = mn
                vo_ref[...] = vn

    def g_spec(ll):
        if stacked:
            return pl.BlockSpec((None, g_n, tr, c), lambda l, i: (l, 0, i, 0))
        return pl.BlockSpec((g_n, tr, c), lambda l, i: (0, jnp.where(l == ll, i, jnp.where(l < ll, 0, nt - 1)), 0))

    blk = pl.BlockSpec((None, tr, c), lambda l, i: (l, i, 0))
    sds = jax.ShapeDtypeStruct((nl, r, c), F32)
    return _call(
        body, name=name, grid=(nl, nt),
        in_specs=[blk, blk, blk] + [g_spec(ll) for ll in range(len(gs))],
        out_specs=[blk, blk, blk, blk], out_shape=[sds, sds, sds, sds],
        sem=("arbitrary", "arbitrary"), args=(w, m, v, *gs), comm=comm)


def _sum_devices(g, name):
    _, r, c = g.shape
    tr = _tile(r, 512, SUBLANES)

    def body(g_ref, o_ref):
        acc = g_ref[0]
        for j in range(1, N_DEV):
            acc = acc + g_ref[j]
        o_ref[...] = acc

    return pl.pallas_call(body, name=name, grid=(r // tr,), in_specs=[pl.BlockSpec((N_DEV, tr, c), lambda i: (0, i, 0))],
                          out_specs=pl.BlockSpec((tr, c), lambda i: (i, 0)), out_shape=jax.ShapeDtypeStruct((r, c), F32),
                          compiler_params=_params(("parallel",)))(g)


def _ada_fwd(c16, w_ada, b_shard, name):
    nl, d, cs = w_ada.shape
    tk = _tile(d, 512, LANES)
    nk = d // tk

    def body(c_ref, w_ref, b_ref, o_ref):
        kk = pl.program_id(1)

        @pl.when(kk == 0)
        def _():
            o_ref[...] = jnp.broadcast_to(b_ref[...], o_ref.shape)

        o_ref[...] += _dg(_silu(c_ref[...]), w_ref[...], 1, 0)

    return pl.pallas_call(
        body, name=name, grid=(nl, nk),
        in_specs=[pl.BlockSpec((16, tk), lambda l, kk: (0, kk)), pl.BlockSpec((None, tk, cs), lambda l, kk: (l, kk, 0)),
                  pl.BlockSpec((None, 1, cs), lambda l, kk: (l, 0, 0))],
        out_specs=pl.BlockSpec((None, 16, cs), lambda l, kk: (l, 0, 0)),
        out_shape=jax.ShapeDtypeStruct((nl, 16, cs), F32),
        compiler_params=_params(("parallel", "arbitrary")))(c16, w_ada, b_shard)


def _ada_bwd(c16, dm16, w_ada, name):
    nl, d, cs = w_ada.shape
    td = _tile(d, 512, LANES)

    def body(c_ref, dm_ref, w_ref, gw_ref, dc_ref):
        cv = c_ref[...]
        s, vjp = jax.vjp(_silu, cv)
        gw_ref[...] = _dg(s, dm_ref[...], 0, 0)
        ds = _dg(dm_ref[...], w_ref[...], 1, 1)
        dc_ref[...] = vjp(ds)[0]

    return pl.pallas_call(
        body, name=name, grid=(nl, d // td),
        in_specs=[pl.BlockSpec((16, td), lambda l, i: (0, i)), pl.BlockSpec((None, 16, cs), lambda l, i: (l, 0, 0)),
                  pl.BlockSpec((None, td, cs), lambda l, i: (l, i, 0))],
        out_specs=[pl.BlockSpec((None, td, cs), lambda l, i: (l, i, 0)), pl.BlockSpec((None, 16, td), lambda l, i: (l, 0, i))],
        out_shape=[jax.ShapeDtypeStruct((nl, d, cs), F32), jax.ShapeDtypeStruct((nl, 16, d), F32)],
        compiler_params=_params(("parallel", "parallel")))(c16, dm16, w_ada)


def _pack_rows(shape):
    n = int(np.prod(shape))
    return SUBLANES * (-(-n // (LANES * SUBLANES)))


def _pack(arrays, row_align):
    parts = []
    for a in arrays:
        flat = a.reshape(-1).astype(F32)
        rows = _pack_rows(a.shape)
        parts.append(jnp.pad(flat, (0, rows * LANES - flat.shape[0])).reshape(rows, LANES))
    total = sum(p.shape[0] for p in parts)
    extra = -total % row_align
    if extra:
        parts.append(jnp.zeros((extra, LANES), F32))
    return jnp.concatenate(parts, axis=0)


def _unpack(packed, shapes):
    out, r = [], 0
    for s in shapes:
        rows = _pack_rows(s)
        out.append(packed[r:r + rows].reshape(-1)[:int(np.prod(s))].reshape(s))
        r += rows
    return out


def _rope_tables():
    half, nf = RET_DK // 2, RET_DK // 4
    pos = jnp.arange(SEQ)
    row = (pos // GRID_W).astype(F32)
    col = (pos % GRID_W).astype(F32)
    inv = ROPE_BASE ** (-jnp.arange(nf, dtype=F32) / nf)
    ar, ac = row[:, None] * inv[None, :], col[:, None] * inv[None, :]
    cos = jnp.concatenate([jnp.cos(ar), jnp.cos(ar), jnp.cos(ac), jnp.cos(ac)], axis=-1)
    sin = jnp.concatenate([-jnp.sin(ar), jnp.sin(ar), -jnp.sin(ac), jnp.sin(ac)], axis=-1)
    cos = jnp.concatenate([jnp.ones((CTX_LEN, RET_DK), F32), cos], axis=0)
    sin = jnp.concatenate([jnp.zeros((CTX_LEN, RET_DK), F32), sin], axis=0)
    return cos, sin


def _layer_fwd(l, x, mod4, w, cst, arrived):
    n = lambda s: f"l{l}_{s}"
    d = D_MODEL
    h1 = _normmod_fwd(x, w["norm1_g"], mod4, 0, n("norm1"))
    w["w_in"] = _cols_from_shards(arrived("w_in", h1), n("w_in_cols"))
    p, _ = _mm(h1, w["w_in"], n("proj_in"))
    o2, states = _ret_fwd(p, cst["cos"], cst["sin"], w["ret_decay"], cst["order"], n("ret_fwd"))
    ret_out = _ggn_fwd(o2, p, w["ret_gn_g"], n("ret_gn"))
    u2, conv_out = _conv_fwd(p, w["conv_dw_w"], w["conv_dw_b"], w["conv_ln_g"], w["conv_ln_b"], w["conv_pw"], n("conv_fwd"))
    na_out, _ = _na_fwd(p, w["rb"], n("na_fwd"))
    mix = jnp.concatenate([ret_out, conv_out, na_out], axis=1)
    w["w_out"] = arrived("w_out", mix).reshape(_d_mix(), d)
    g1, _ = _mm(mix, w["w_out"], n("proj_out"))
    x1 = _gate_res_fwd(x, g1, mod4, 2, n("res1"))
    h2 = _normmod_fwd(x1, w["norm2_g"], mod4, 1, n("norm2"))
    w["ffn_up"] = arrived("ffn_up", h2)
    u, _ = _mm(h2, w["ffn_up"], n("ffn_up"), b3=True)
    a = _ffn_act_fwd(u, w["ffn_dw_w"], w["ffn_dw_b"], n("ffn_act"))
    w["ffn_down"] = arrived("ffn_down", a).reshape(D_FF, d)
    f, _ = _mm(a, w["ffn_down"], n("ffn_down"))
    x2 = _gate_res_fwd(x1, f, mod4, 5, n("res2"))
    saved = dict(x=x, h1=h1, p=p, o2=o2, states=states, u2=u2, mix=mix, g1=g1, x1=x1, h2=h2, u=u, a=a, f=f)
    return x2, saved


def _layer_bwd(l, dx2, s, mod4, w, cst, send):
    n = lambda t: f"l{l}_{t}"
    d = D_MODEL
    nj = D_FF // _ffn_tc()
    dfg, dg2 = _gate_res_bwd(dx2, s["f"], mod4, 5, n("res2_bwd"))
    da, _ = _mm(dfg, w["ffn_down"], n("ffn_down_dx"), tb=True)
    d_ffn_down, _ = _mm(_transpose_bf16(s["a"], n("act_t")), dfg, n("ffn_down_dw"), out_dtype=BF16, tm_max=DW_TM)
    tok = send(("ffn_down", l), d_ffn_down.reshape(N_DEV, D_FF // N_DEV, d))
    dcv, dcg = _ffn_act_bwd1(s["u"], da, w["ffn_dw_w"], _after(w["ffn_dw_b"], tok), n("ffn_act_bwd"))
    duv, dwv, dbv = _dwconv_bwd(dcv, s["u"], w["ffn_dw_w"], 0, n("ffn_dw_bwd_val"))
    dug, dwg, dbg = _dwconv_bwd(dcg, s["u"], w["ffn_dw_w"], nj, n("ffn_dw_bwd_gate"))
    du = jnp.concatenate([duv, dug], axis=1)
    d_ffn_dw_w = jnp.concatenate([dwv, dwg], axis=1)
    d_ffn_dw_b = jnp.concatenate([dbv, dbg], axis=1)[0]
    dh2, _ = _mm(du, w["ffn_up"], n("ffn_up_dx"), tb=True, b3=True)
    d_ffn_up, _ = _mm(_transpose_bf16(s["h2"], n("h2_t")), du, n("ffn_up_dw"), out_dtype=BF16, tm_max=DW_TM,
                      o_cs=2 * D_FF // N_DEV)
    tok = send(("ffn_up", l), d_ffn_up)
    (dx1, dn2, dsh2, dsc2), _ = _normmod_bwd(s["x1"], _after(w["norm2_g"], tok), mod4, 1, dh2, dx2, n("norm2_bwd"))
    dgg, dg1 = _gate_res_bwd(dx1, s["g1"], mod4, 2, n("res1_bwd"))
    dmix, _ = _mm(dgg, w["w_out"], n("proj_out_dx"), tb=True)
    d_w_out, _ = _mm(_transpose_bf16(s["mix"], n("mix_t")), dgg, n("proj_out_dw"), out_dtype=BF16, tm_max=DW_TM)
    tok = send(("w_out", l), d_w_out.reshape(N_DEV, _d_mix() // N_DEV, d))
    do, dgate, dgn = _ggn_bwd(s["o2"], s["p"], _after(w["ret_gn_g"], tok), dmix, n("ret_gn_bwd"))
    dqr, dkr, dvr, ddec = _ret_bwd(s["p"], cst["cos"], cst["sin"], w["ret_decay"], cst["order"], s["states"], do, n("ret_bwd"))
    du2, dlng, dlnb, dpw = _conv_bwd1(s["u2"], dmix, w["conv_ln_g"], w["conv_ln_b"], w["conv_pw"], n("conv_bwd1"))
    (dca, dcb, ddww, ddwb), _ = _conv_bwd2(du2, s["p"], w["conv_dw_w"], n("conv_bwd2"))
    (dnq, dnk, dnv, drb), _ = _na_bwd(s["p"], w["rb"], dmix, n("na_bwd"))
    dp = _assemble_dp(dqr, dkr, dvr, dgate, dca, dcb, dnq, dnk, dnv, n("dproj"))
    h1_t = _transpose_bf16(s["h1"], n("h1_t"))
    half = d // 2
    for i in range(2):
        d_w_in, _ = _mm(h1_t[i * half:(i + 1) * half], dp, n(f"proj_in_dw{i}"), out_dtype=BF16, tm_max=DW_TM,
                        o_cs=_d_in() // N_DEV)
        tok = send(("w_in", l, i), d_w_in)
    dh1, _ = _mm(dp, w["w_in"], n("proj_in_dx"), tb=True)
    (dx, dn1, dsh1, dsc1), _ = _normmod_bwd(s["x"], _after(w["norm1_g"], tok), mod4, 0, dh1, dx1, n("norm1_bwd"))
    dmod = jnp.concatenate([dsh1, dsc1, dg1, dsh2, dsc2, dg2], axis=1)
    small = dict(norm1_g=dn1[0], ret_decay=ddec[:, :, 0, 0], ret_gn_g=dgn[0], conv_dw_w=ddww, conv_dw_b=ddwb[0],
                 conv_ln_g=dlng[0], conv_ln_b=dlnb[0], conv_pw=dpw, na_rpb=_rpb_rows_t(drb), norm2_g=dn2[0],
                 ffn_dw_w=d_ffn_dw_w, ffn_dw_b=d_ffn_dw_b)
    return dx, dmod, small


def _d_mix():
    return _ret_w() + CONV_W + _na_w()


_SMALL = ["c_ctx", "b_ada", "norm1_g", "ret_decay", "ret_gn_g", "conv_dw_w", "conv_dw_b", "conv_ln_g", "conv_ln_b",
          "conv_pw", "na_rpb", "norm2_g", "ffn_dw_w", "ffn_dw_b", "final_g"]
_SMALL_SHARD_AXIS = {"conv_dw_w": 2, "conv_pw": 1, "ffn_dw_w": 2}


def kernel(x, c, ctx, c_ctx, w_ada, b_ada, norm1_g, w_in, ret_decay, ret_gn_g, conv_dw_w, conv_dw_b, conv_ln_g, conv_ln_b, conv_pw, na_rpb, w_out, norm2_g, ffn_up, ffn_dw_w, ffn_dw_b, ffn_down, final_g, loss_target, m_c_ctx, m_w_ada, m_b_ada, m_norm1_g, m_w_in, m_ret_decay, m_ret_gn_g, m_conv_dw_w, m_conv_dw_b, m_conv_ln_g, m_conv_ln_b, m_conv_pw, m_na_rpb, m_w_out, m_norm2_g, m_ffn_up, m_ffn_dw_w, m_ffn_dw_b, m_ffn_down, m_final_g, v_c_ctx, v_w_ada, v_b_ada, v_norm1_g, v_w_in, v_ret_decay, v_ret_gn_g, v_conv_dw_w, v_conv_dw_b, v_conv_ln_g, v_conv_ln_b, v_conv_pw, v_na_rpb, v_w_out, v_norm2_g, v_ffn_up, v_ffn_dw_w, v_ffn_dw_b, v_ffn_down, v_final_g):
    d, nl = D_MODEL, DEPTH
    cs = 6 * d // N_DEV
    me = _my_index()
    weights = dict(c_ctx=c_ctx, w_ada=w_ada, b_ada=b_ada, norm1_g=norm1_g, w_in=w_in, ret_decay=ret_decay, ret_gn_g=ret_gn_g,
                   conv_dw_w=conv_dw_w, conv_dw_b=conv_dw_b, conv_ln_g=conv_ln_g, conv_ln_b=conv_ln_b, conv_pw=conv_pw,
                   na_rpb=na_rpb, w_out=w_out, norm2_g=norm2_g, ffn_up=ffn_up, ffn_dw_w=ffn_dw_w, ffn_dw_b=ffn_dw_b,
                   ffn_down=ffn_down, final_g=final_g)
    mom = dict(c_ctx=m_c_ctx, w_ada=m_w_ada, b_ada=m_b_ada, norm1_g=m_norm1_g, w_in=m_w_in, ret_decay=m_ret_decay,
               ret_gn_g=m_ret_gn_g, conv_dw_w=m_conv_dw_w, conv_dw_b=m_conv_dw_b, conv_ln_g=m_conv_ln_g,
               conv_ln_b=m_conv_ln_b, conv_pw=m_conv_pw, na_rpb=m_na_rpb, w_out=m_w_out, norm2_g=m_norm2_g,
               ffn_up=m_ffn_up, ffn_dw_w=m_ffn_dw_w, ffn_dw_b=m_ffn_dw_b, ffn_down=m_ffn_down, final_g=m_final_g)
    var = dict(c_ctx=v_c_ctx, w_ada=v_w_ada, b_ada=v_b_ada, norm1_g=v_norm1_g, w_in=v_w_in, ret_decay=v_ret_decay,
               ret_gn_g=v_ret_gn_g, conv_dw_w=v_conv_dw_w, conv_dw_b=v_conv_dw_b, conv_ln_g=v_conv_ln_g,
               conv_ln_b=v_conv_ln_b, conv_pw=v_conv_pw, na_rpb=v_na_rpb, w_out=v_w_out, norm2_g=v_norm2_g,
               ffn_up=v_ffn_up, ffn_dw_w=v_ffn_dw_w, ffn_dw_b=v_ffn_dw_b, ffn_down=v_ffn_down, final_g=v_final_g)

    big_names = ["w_in", "w_out", "ffn_up", "ffn_down"]
    shards = {(nm, l): _cast_bf16(weights[nm][l], f"cast_{nm}{l}") for l in range(nl) for nm in big_names}
    small_sharded = _pack([conv_dw_w, conv_pw, ffn_dw_w], SUBLANES)
    c_rows = jnp.pad(c, ((0, SUBLANES - 1), (0, 0)))
    gathered = _run_comm(_Gather([c_rows, small_sharded, shards[("w_in", 0)]]), "gather_first")
    c_all = gathered[0][:, 0, :]
    sm = [_unpack(gathered[1][j], [conv_dw_w.shape, conv_pw.shape, ffn_dw_w.shape]) for j in range(N_DEV)]
    full_conv_dw_w = jnp.concatenate([s[0] for s in sm], axis=2)
    full_conv_pw = jnp.concatenate([s[1] for s in sm], axis=1)
    full_ffn_dw_w = jnp.concatenate([s[2] for s in sm], axis=2)

    c16 = jnp.concatenate([c_all, jnp.broadcast_to(c_ctx[None, :], (N_DEV, d))], axis=0)
    b_shard = lax.dynamic_slice_in_dim(b_ada, me * cs, cs, axis=1)[:, None, :]
    m_shard = _ada_fwd(c16, w_ada, b_shard, "ada_fwd")
    m_all = _run_comm(_Gather([m_shard.reshape(nl * 16, cs)]), "gather_mod")[0]
    m_full = m_all.reshape(N_DEV, nl, 16, cs).transpose(1, 2, 0, 3).reshape(nl, 16, 6 * d)
    m_lat = lax.dynamic_index_in_dim(m_full, me, axis=1, keepdims=False)
    mod = jnp.stack([m_full[:, N_DEV], m_lat], axis=1).reshape(nl, 2, 6, 1, d)

    arriving, token = {}, m_all
    for l in range(nl):
        for nm in big_names:
            if (nm, l) != ("w_in", 0):
                arriving[(nm, l)], token = _split_start(shards[(nm, l)], True, f"gather_{nm}{l}", token)
    mod = _after(mod, token)

    cos, sin = _rope_tables()
    cst = dict(cos=cos, sin=sin, order=_chunk_order())
    layer_w = []
    for l in range(nl):
        layer_w.append(dict(
            norm1_g=norm1_g[l][None], norm2_g=norm2_g[l][None], ret_decay=ret_decay[l], ret_gn_g=ret_gn_g[l][None],
            conv_dw_w=full_conv_dw_w[l], conv_dw_b=conv_dw_b[l][None], conv_ln_g=conv_ln_g[l][None],
            conv_ln_b=conv_ln_b[l][None], conv_pw=full_conv_pw[l], rb=_rpb_rows(na_rpb[l]),
            ffn_dw_w=full_ffn_dw_w[l], ffn_dw_b=ffn_dw_b[l][None]))

    xs = jnp.concatenate([ctx[0], x[0]], axis=0)
    saved = []
    for l in range(nl):
        def arrived(nm, after, l=l):
            if (nm, l) == ("w_in", 0):
                return gathered[2]
            return _split_wait(arriving[(nm, l)], after, f"arrived_{nm}{l}")

        xs, sv = _layer_fwd(l, xs, mod[l], layer_w[l], cst, arrived)
        saved.append(sv)
    loss_tile, dxs, dfinal = _loss_head(xs, final_g[None], loss_target[0], "loss_head")
    loss = lax.psum(loss_tile[0, 0], ("x", "y", "c"))

    dmods, smalls = [None] * nl, [None] * nl
    leaving, last = {}, [loss_tile]

    def send(key, partial):
        leaving[key], token = _split_start(partial, False, "send_" + "_".join(str(k) for k in key), last[0])
        last[0] = token
        return token

    for l in reversed(range(nl)):
        dxs, dmods[l], smalls[l] = _layer_bwd(l, dxs, saved[l], mod[l], layer_w[l], cst, send)
    grad_x = dxs[CTX_LEN:][None]

    dm_mine = jnp.stack(dmods).reshape(nl * 2, 6 * d)
    dm_rows = jnp.pad(dm_mine, ((0, SUBLANES - nl * 2), (0, 0)))
    dm_all = _run_comm(_Gather([dm_rows]), "gather_dmod")[0][:, :nl * 2].reshape(N_DEV, nl, 2, 6 * d)
    dm16_full = jnp.concatenate([dm_all[:, :, 1].transpose(1, 0, 2), dm_all[:, :, 0].transpose(1, 0, 2)], axis=1)
    dm16 = lax.dynamic_slice_in_dim(dm16_full, me * cs, cs, axis=2)
    g_w_ada, dc16 = _ada_bwd(c16, dm16, w_ada, "ada_bwd")

    arrived_grad = lambda key, after: _split_wait(leaving[key], after, "got_" + "_".join(str(k) for k in key))
    out_big = {}
    out_big["w_ada"], _ = _adamw(w_ada, m_w_ada, v_w_ada, g_w_ada[:, None], "adamw_w_ada")
    after = out_big["w_ada"][0]
    for nm in ["ffn_down", "ffn_up", "w_out"]:
        out_big[nm], _ = _adamw(weights[nm], mom[nm], var[nm], [arrived_grad((nm, l), after) for l in range(nl)],
                                f"adamw_{nm}")
        after = out_big[nm][0]
    halves = lambda a: a.reshape(2 * nl, d // 2, a.shape[2])
    res, _ = _adamw(halves(w_in), halves(m_w_in), halves(v_w_in),
                    [arrived_grad(("w_in", l, i), after) for l in range(nl) for i in range(2)], "adamw_w_in")
    out_big["w_in"] = [r.reshape(w_in.shape) for r in res]

    small_grads = dict(
        c_ctx=jnp.sum(dc16[:, N_DEV:], axis=(0, 1)),
        b_ada=jnp.sum(jnp.stack(dmods).reshape(nl, 2, 6 * d), axis=1),
        final_g=dfinal[0])
    for nm in _SMALL:
        if nm not in small_grads:
            small_grads[nm] = jnp.stack([smalls[l][nm] for l in range(nl)])
    shapes_full = [small_grads[nm].shape for nm in _SMALL]
    packed = _pack([small_grads[nm] for nm in _SMALL], 512)
    summed = _sum_devices(_run_comm(_Gather([packed]), "gather_small_grads")[0], "sum_small_grads")
    g_small = dict(zip(_SMALL, _unpack(summed, shapes_full)))
    for nm, ax in _SMALL_SHARD_AXIS.items():
        n_sh = weights[nm].shape[ax]
        g_small[nm] = lax.dynamic_slice_in_dim(g_small[nm], me * n_sh, n_sh, axis=ax)
    shapes_own = [weights[nm].shape for nm in _SMALL]
    pk = lambda src: _pack([src[nm] for nm in _SMALL], 2 * SUBLANES)[None]
    res_small, _ = _adamw(pk(weights), pk(mom), pk(var), pk(g_small)[:, None], "adamw_small")
    out_small = [dict(zip(_SMALL, _unpack(r[0], shapes_own))) for r in res_small]

    names = ["c_ctx", "w_ada", "b_ada", "norm1_g", "w_in", "ret_decay", "ret_gn_g", "conv_dw_w", "conv_dw_b", "conv_ln_g",
             "conv_ln_b", "conv_pw", "na_rpb", "w_out", "norm2_g", "ffn_up", "ffn_dw_w", "ffn_dw_b", "ffn_down", "final_g"]
    outs = [loss, grad_x]
    for kind in range(4):
        for nm in names:
            outs.append(out_big[nm][kind] if nm in out_big else out_small[kind][nm])
    return tuple(outs)
```

```python
import functools
import math

import numpy as np
import jax
import jax.numpy as jnp
from jax import lax
from jax.experimental import pallas as pl
from jax.experimental.pallas import tpu as pltpu

D_MODEL = 2048
SEQ = 4096
DEPTH = 2
GRID_W = 64
CTX_LEN = 256
RET_HEADS = 4
RET_DK = 128
RET_DV = 256
RET_CHUNK = 128
CONV_W = 512
CONV_K = 31
NA_HEADS = 4
NA_DH = 128
NA_ROWS = 8
NA_COLS = 16
D_FF = 5632
FFN_K = 3
ROPE_BASE = 10000.0
EPS = 1e-6
ADAM_LR = 0.001
ADAM_B1 = 0.9
ADAM_B2 = 0.999
ADAM_EPS = 1e-08
ADAM_WD = 0.01
ADAM_STEP = 10
N_DEV = 8

LANES = 128
SUBLANES = 8
VMEM_LIMIT = 56 * 1024 * 1024
ROW_CHUNK = 16

F32 = jnp.float32
BF16 = jnp.bfloat16
MESH = pl.DeviceIdType.MESH
NEG = -1e30


def _ret_qk_w():
    return RET_HEADS * RET_DK


def _ret_w():
    return RET_HEADS * RET_DV


def _na_w():
    return NA_HEADS * NA_DH


def _d_in():
    return 2 * _ret_qk_w() + 2 * _ret_w() + 2 * CONV_W + 3 * _na_w()


def _offsets():
    sizes = [_ret_qk_w(), _ret_qk_w(), _ret_w(), _ret_w(), CONV_W, CONV_W, _na_w(), _na_w(), _na_w()]
    offs = [0]
    for s in sizes[:-1]:
        offs.append(offs[-1] + s)
    return dict(zip(["q", "k", "v", "g", "a", "b", "nq", "nk", "nv"], offs))


def _t_rows():
    return CTX_LEN + SEQ


def _tm():
    return CTX_LEN


def _params(sem=None):
    kw = dict(vmem_limit_bytes=VMEM_LIMIT)
    if sem is not None:
        kw["dimension_semantics"] = sem
    return pltpu.CompilerParams(**kw)


def _tile(n, pref, align):
    best = None
    for t in range(align, min(n, pref) + 1, align):
        if n % t == 0:
            best = t
    return best if best is not None else n


def _dg(a, b, ca, cb):
    return lax.dot_general(a.astype(BF16), b.astype(BF16), (((ca,), (cb,)), ((), ())), preferred_element_type=F32)


@jax.custom_vjp
def dot_nn(a, b):
    return _dg(a, b, 1, 0)


dot_nn.defvjp(lambda a, b: (_dg(a, b, 1, 0), (a, b)),
              lambda r, g: (_dg(g, r[1], 1, 1), _dg(r[0], g, 0, 0)))


@jax.custom_vjp
def dot_nt(a, b):
    return _dg(a, b, 1, 1)


dot_nt.defvjp(lambda a, b: (_dg(a, b, 1, 1), (a, b)),
              lambda r, g: (_dg(g, r[1], 1, 0), _dg(g, r[0], 0, 0)))


@jax.custom_vjp
def dot_tn(a, b):
    return _dg(a, b, 0, 0)


dot_tn.defvjp(lambda a, b: (_dg(a, b, 0, 0), (a, b)),
              lambda r, g: (_dg(r[1], g, 1, 1), _dg(r[0], g, 1, 0)))


def _sigmoid(x):
    return 0.5 * jnp.tanh(0.5 * x) + 0.5


def _silu(x):
    return x * _sigmoid(x)


def _my_pos():
    return lax.axis_index("x"), lax.axis_index("y"), lax.axis_index("c")


def _my_index():
    x, y, c = _my_pos()
    return 4 * x + 2 * y + c


_ANY = pl.BlockSpec(memory_space=pl.ANY)


class _Gather:
    def __init__(self, arrays):
        self.arrays = list(arrays)
        n = len(self.arrays)
        self.out_shape = [jax.ShapeDtypeStruct((N_DEV,) + a.shape, a.dtype) for a in self.arrays]
        self.scratch = [pltpu.SemaphoreType.DMA((n, 7)), pltpu.SemaphoreType.DMA((n, 7)), pltpu.SemaphoreType.DMA((n,))]

    def _plan(self, xs, outs, sems):
        send_sems, recv_sems, local_sems = sems
        n = len(self.arrays)
        x, y, c = _my_pos()
        me, sibling = (x, y, c), (x, y, 1 - c)
        chips = [(1 - x, y), (x, 1 - y), (1 - x, 1 - y)]

        def slot(a, p):
            return outs[a].at[4 * p[0] + 2 * p[1] + p[2]]

        def copy(a, k, block, to, src=None):
            return pltpu.make_async_remote_copy(
                src_ref=slot(a, block) if src is None else src, dst_ref=slot(a, block),
                send_sem=send_sems.at[a, k], recv_sem=recv_sems.at[a, k], device_id=to, device_id_type=MESH)

        mine = [pltpu.make_async_copy(xs[a], slot(a, me), local_sems.at[a]) for a in range(n)]
        first = []
        for a in range(n):
            first.append(copy(a, 0, me, sibling, src=xs[a]))
            first += [copy(a, 1 + j, me, (*chip, c), src=xs[a]) for j, chip in enumerate(chips)]
        return n, c, me, sibling, chips, copy, mine, first

    def start(self, xs, outs, sems):
        _, _, _, _, _, _, mine, first = self._plan(xs, outs, sems)
        for m in mine:
            m.start()
        for cp in first:
            cp.start()

    def finish(self, xs, outs, sems):
        n, c, me, sibling, chips, copy, mine, first = self._plan(xs, outs, sems)
        passed = []
        for a in range(n):
            for j, chip in enumerate(chips):
                copy(a, 1 + j, (*chip, c), me).wait_recv()
                p = copy(a, 4 + j, (*chip, c), sibling)
                p.start()
                passed.append(p)
        for a in range(n):
            copy(a, 0, sibling, me).wait_recv()
            for j, chip in enumerate(chips):
                copy(a, 4 + j, (*chip, 1 - c), me).wait_recv()
        for cp in first + passed:
            cp.wait_send()
        for m in mine:
            m.wait()


class _Exchange:
    def __init__(self, arrays):
        self.arrays = list(arrays)
        n = len(self.arrays)
        self.out_shape = [jax.ShapeDtypeStruct(a.shape, a.dtype) for a in self.arrays]
        self.scratch = [pltpu.SemaphoreType.DMA((n, 7)), pltpu.SemaphoreType.DMA((n, 7)), pltpu.SemaphoreType.DMA((n,))]

    def _plan(self, xs, outs, sems):
        send_sems, recv_sems, local_sems = sems
        x, y, c = _my_pos()
        me = 4 * x + 2 * y + c
        mine, sends, recvs = [], [], []
        for a in range(len(self.arrays)):
            mine.append(pltpu.make_async_copy(xs[a].at[me], outs[a].at[me], local_sems.at[a]))
        for k in range(1, N_DEV):
            px = 1 - x if (k >> 2) & 1 else x
            py = 1 - y if (k >> 1) & 1 else y
            pc = 1 - c if k & 1 else c
            peer = 4 * px + 2 * py + pc
            for a in range(len(self.arrays)):
                sends.append(pltpu.make_async_remote_copy(
                    src_ref=xs[a].at[peer], dst_ref=outs[a].at[me], send_sem=send_sems.at[a, k - 1],
                    recv_sem=recv_sems.at[a, k - 1], device_id=(px, py, pc), device_id_type=MESH))
                recvs.append(pltpu.make_async_remote_copy(
                    src_ref=xs[a].at[me], dst_ref=outs[a].at[peer], send_sem=send_sems.at[a, k - 1],
                    recv_sem=recv_sems.at[a, k - 1], device_id=(px, py, pc), device_id_type=MESH))
        return mine, sends, recvs

    def start(self, xs, outs, sems):
        mine, sends, _ = self._plan(xs, outs, sems)
        for m in mine:
            m.start()
        for s in sends:
            s.start()

    def finish(self, xs, outs, sems):
        mine, sends, recvs = self._plan(xs, outs, sems)
        for r in recvs:
            r.wait_recv()
        for s in sends:
            s.wait_send()
        for m in mine:
            m.wait()


def _run_comm(comm, name, after=None):
    n = len(comm.arrays)
    extra = [] if after is None else [after]

    def body(*refs):
        xs, outs, sems = refs[:n], refs[n + len(extra):2 * n + len(extra)], refs[2 * n + len(extra):]
        comm.start(xs, outs, sems)
        comm.finish(xs, outs, sems)

    return pl.pallas_call(body, name=name, out_shape=comm.out_shape, in_specs=[_ANY] * (n + len(extra)),
                          out_specs=[_ANY] * n, scratch_shapes=comm.scratch)(*comm.arrays, *extra)


_HBM = pl.BlockSpec(memory_space=pltpu.HBM)
_SEMS = pl.BlockSpec(memory_space=pltpu.SEMAPHORE)
_EFFECT = pltpu.SideEffectType.DATAFLOW_SIDE_EFFECTING


def _own_slot(x, gathering, name):
    shape = (N_DEV,) + x.shape if gathering else x.shape
    r, c = shape[1], shape[2]
    tr = _tile(r, 256, 2 * SUBLANES)
    me = jnp.reshape(_my_index(), (1,)).astype(jnp.int32)

    def body(me_ref, x_ref, o_ref):
        o_ref[...] = x_ref[...]

    src = (pl.BlockSpec((tr, c), lambda i, m: (i, 0)) if gathering
           else pl.BlockSpec((None, tr, c), lambda i, m: (m[0], i, 0)))
    grid_spec = pltpu.PrefetchScalarGridSpec(
        num_scalar_prefetch=1, grid=(r // tr,), in_specs=[src],
        out_specs=pl.BlockSpec((None, tr, c), lambda i, m: (m[0], i, 0)))
    return pl.pallas_call(body, name=name, grid_spec=grid_spec, out_shape=jax.ShapeDtypeStruct(shape, x.dtype),
                          compiler_params=_params(("arbitrary",)))(me, x)


def _split_plan(x_ref, land_ref, send_sems, recv_sems, gathering):
    x, y, c = _my_pos()
    me = 4 * x + 2 * y + c
    sends, recvs = [], []
    for k in range(1, N_DEV):
        px = 1 - x if (k >> 2) & 1 else x
        py = 1 - y if (k >> 1) & 1 else y
        pc = 1 - c if k & 1 else c
        peer = 4 * px + 2 * py + pc
        mine, theirs = (x_ref, x_ref) if gathering else (x_ref.at[peer], x_ref.at[me])
        sends.append(pltpu.make_async_remote_copy(
            src_ref=mine, dst_ref=land_ref.at[me], send_sem=send_sems.at[k - 1], recv_sem=recv_sems.at[k - 1],
            device_id=(px, py, pc), device_id_type=MESH))
        recvs.append(pltpu.make_async_remote_copy(
            src_ref=theirs, dst_ref=land_ref.at[peer], send_sem=send_sems.at[k - 1], recv_sem=recv_sems.at[k - 1],
            device_id=(px, py, pc), device_id_type=MESH))
    return sends, recvs


def _split_start(x, gathering, name, prev):
    land = _own_slot(x, gathering, name + "_own")

    def body(x_ref, land_ref, prev_ref, send_sems, recv_sems, x_thru, land_thru, token):
        sends, _ = _split_plan(x_ref, land_ref, send_sems, recv_sems, gathering)
        for s in sends:
            s.start()
        token[...] = jnp.zeros_like(token)

    sems = pltpu.SemaphoreType.DMA((N_DEV - 1,))
    send_sems, recv_sems, x_thru, land_thru, token = pl.pallas_call(
        body, name=name,
        out_shape=(sems, sems, pltpu.HBM(x.shape, x.dtype), pltpu.HBM(land.shape, land.dtype),
                   jax.ShapeDtypeStruct((SUBLANES, LANES), F32)),
        in_specs=(_HBM, _HBM, _ANY), out_specs=(_SEMS, _SEMS, _HBM, _HBM, pl.BlockSpec(memory_space=pltpu.VMEM)),
        input_output_aliases={0: 2, 1: 3},
        compiler_params=pltpu.CompilerParams(has_side_effects=_EFFECT),
    )(pltpu.with_memory_space_constraint(x, pltpu.HBM), pltpu.with_memory_space_constraint(land, pltpu.HBM), prev)
    return (send_sems, recv_sems, x_thru, land_thru, gathering), token


def _after(a, token):
    return a + token[0, 0].astype(a.dtype)


def _split_wait(handle, after, name):
    send_sems, recv_sems, x_thru, land_thru, gathering = handle

    def body(x_ref, land_ref, send_sems, recv_sems, after_ref, x_dead, got_ref):
        sends, recvs = _split_plan(x_ref, land_ref, send_sems, recv_sems, gathering)
        for s in sends:
            s.wait_send()
        for r in recvs:
            r.wait_recv()

    return pl.pallas_call(
        body, name=name, out_shape=(pltpu.HBM(x_thru.shape, x_thru.dtype), pltpu.HBM(land_thru.shape, land_thru.dtype)),
        in_specs=(_HBM, _HBM, _SEMS, _SEMS, _ANY), out_specs=(_HBM, _HBM), input_output_aliases={0: 0, 1: 1},
        compiler_params=pltpu.CompilerParams(has_side_effects=_EFFECT),
    )(x_thru, land_thru, send_sems, recv_sems, after)[1]


def _call(body, *, name, grid, in_specs, out_specs, out_shape, args, scratch=(), sem=None, comm=None):
    if comm is None:
        res = pl.pallas_call(body, name=name, grid=grid, in_specs=list(in_specs), out_specs=list(out_specs),
                             out_shape=list(out_shape), scratch_shapes=list(scratch), compiler_params=_params(sem))(*args)
        return list(res), None
    n_in, n_out, n_scr = len(in_specs), len(out_specs), len(scratch)
    c_n = len(comm.arrays)

    def wrapped(*refs):
        ins, cin = refs[:n_in], refs[n_in:n_in + c_n]
        o0 = n_in + c_n
        outs, cout = refs[o0:o0 + n_out], refs[o0 + n_out:o0 + n_out + c_n]
        s0 = o0 + n_out + c_n
        scr, cscr = refs[s0:s0 + n_scr], refs[s0 + n_scr:]
        ids = [pl.program_id(ax) for ax in range(len(grid))]
        first = functools.reduce(jnp.logical_and, [i == 0 for i in ids])
        last = functools.reduce(jnp.logical_and, [i == g - 1 for i, g in zip(ids, grid)])

        @pl.when(first)
        def _():
            comm.start(cin, cout, cscr)

        body(*ins, *outs, *scr)

        @pl.when(last)
        def _():
            comm.finish(cin, cout, cscr)

    res = pl.pallas_call(
        wrapped, name=name, grid=grid, in_specs=list(in_specs) + [_ANY] * c_n, out_specs=list(out_specs) + [_ANY] * c_n,
        out_shape=list(out_shape) + list(comm.out_shape), scratch_shapes=list(scratch) + list(comm.scratch),
        compiler_params=_params(("arbitrary",) * len(grid)))(*args, *comm.arrays)
    return list(res[:n_out]), list(res[n_out:])


MM_B_BLOCK_BYTES = 6 * 1024 * 1024
MM_O_BLOCK_BYTES = 13 * 1024 * 1024 // 2


def _mm(a, b, name, tb=False, out_dtype=F32, b3=False, o_cs=None, tm_max=1088, comm=None):
    m, k = a.shape
    if b3:
        cs = b.shape[2]
        n, kb = (b.shape[1], N_DEV * cs) if tb else (N_DEV * cs, b.shape[1])
    else:
        n, kb = (b.shape[0], b.shape[1]) if tb else (b.shape[1], b.shape[0])
    assert k == kb, (a.shape, b.shape, tb)
    tm = _tile(m, tm_max, 2 * SUBLANES)
    tk = cs if (b3 and tb) else k
    nk = k // tk
    if b3 and not tb:
        tn = cs
    elif o_cs is not None:
        tn = o_cs if o_cs % LANES == 0 else 2 * o_cs
    else:
        tn = _tile(n, min(MM_B_BLOCK_BYTES // (2 * tk), MM_O_BLOCK_BYTES // (4 * tm)), LANES)
    cb = 1 if tb else 0
    dn = (((1,), (cb,)), ((), ()))

    def body_one(a_ref, b_ref, o_ref):
        r = lax.dot_general(a_ref[...], b_ref[...], dn, preferred_element_type=F32)
        if o_cs is None:
            o_ref[...] = r.astype(o_ref.dtype)
        else:
            for j in range(tn // o_cs):
                o_ref[j] = r[:, j * o_cs:(j + 1) * o_cs].astype(o_ref.dtype)

    def body_acc(a_ref, b_ref, o_ref, acc_ref):
        kk = pl.program_id(2)
        prod = lax.dot_general(a_ref[...], b_ref[...], dn, preferred_element_type=F32)

        @pl.when(kk == 0)
        def _():
            acc_ref[...] = prod

        @pl.when(kk > 0)
        def _():
            acc_ref[...] += prod

        @pl.when(kk == nk - 1)
        def _():
            o_ref[...] = acc_ref[...].astype(o_ref.dtype)

    a_spec = pl.BlockSpec((tm, tk), lambda i, j, kk: (i, kk))
    if b3:
        b_spec = (pl.BlockSpec((None, tn, cs), lambda i, j, kk: (kk, j, 0)) if tb
                  else pl.BlockSpec((None, tk, cs), lambda i, j, kk: (j, kk, 0)))
    else:
        b_spec = pl.BlockSpec((tn, tk), lambda i, j, kk: (j, kk)) if tb else pl.BlockSpec((tk, tn), lambda i, j, kk: (kk, j))
    if o_cs is None:
        o_spec = pl.BlockSpec((tm, tn), lambda i, j, kk: (i, j))
        o_shape = jax.ShapeDtypeStruct((m, n), out_dtype)
    else:
        assert nk == 1
        o_spec = pl.BlockSpec((tn // o_cs, tm, o_cs), lambda i, j, kk: (j, i, 0))
        o_shape = jax.ShapeDtypeStruct((n // o_cs, m, o_cs), out_dtype)
    res, cres = _call(
        body_one if nk == 1 else body_acc, name=name, grid=(m // tm, n // tn, nk), in_specs=[a_spec, b_spec],
        out_specs=[o_spec], out_shape=[o_shape],
        scratch=[] if nk == 1 else [pltpu.VMEM((tm, tn), F32)], sem=("parallel", "parallel", "arbitrary"),
        args=(a, b), comm=comm)
    return res[0], cres


DW_TM = 512


def _transpose_bf16(x, name):
    t, c = x.shape
    tt = _tm()

    def body(x_ref, o_ref):
        o_ref[...] = x_ref[...].T

    return pl.pallas_call(body, name=name, grid=(t // tt,), in_specs=[pl.BlockSpec((tt, c), lambda i: (i, 0))],
                          out_specs=pl.BlockSpec((c, tt), lambda i: (0, i)),
                          out_shape=jax.ShapeDtypeStruct((c, t), BF16), compiler_params=_params(("parallel",)))(x)


def _cast_bf16(x, name):
    r, c = x.shape
    tr = _tile(r, 512, 2 * SUBLANES)

    def body(x_ref, o_ref):
        o_ref[...] = x_ref[...].astype(BF16)

    return pl.pallas_call(body, name=name, grid=(r // tr,), in_specs=[pl.BlockSpec((tr, c), lambda i: (i, 0))],
                          out_specs=pl.BlockSpec((tr, c), lambda i: (i, 0)),
                          out_shape=jax.ShapeDtypeStruct((r, c), BF16), compiler_params=_params(("parallel",)))(x)


def _cols_from_shards(wg, name):
    _, k, cs = wg.shape
    tk = _tile(k, 256, 2 * SUBLANES)

    def body(w_ref, o_ref):
        for j in range(N_DEV):
            o_ref[:, j * cs:(j + 1) * cs] = w_ref[j]

    return pl.pallas_call(body, name=name, grid=(k // tk,),
                          in_specs=[pl.BlockSpec((N_DEV, tk, cs), lambda i: (0, i, 0))],
                          out_specs=pl.BlockSpec((tk, N_DEV * cs), lambda i: (i, 0)),
                          out_shape=jax.ShapeDtypeStruct((k, N_DEV * cs), wg.dtype),
                          compiler_params=_params(("parallel",)))(wg)


def _stream(i):
    return jnp.minimum(i, 1)


def _normmod(x, g, sh, sc):
    y = x * lax.rsqrt(jnp.mean(x * x, axis=-1, keepdims=True) + EPS)
    return (y * g) * (1.0 + sc) + sh


def _mod_spec(chunk, d):
    return pl.BlockSpec((None, None, 1, d), lambda i: (_stream(i), chunk, 0, 0))


def _normmod_fwd(x, g, mod4, which, name):
    t, d = x.shape
    tm = _tm()
    ish, isc = (0, 1) if which == 0 else (3, 4)

    def body(x_ref, g_ref, sh_ref, sc_ref, o_ref):
        o_ref[...] = _normmod(x_ref[...], g_ref[...], sh_ref[...], sc_ref[...]).astype(BF16)

    row = pl.BlockSpec((tm, d), lambda i: (i, 0))
    return pl.pallas_call(body, name=name, grid=(t // tm,),
                          in_specs=[row, pl.BlockSpec((1, d), lambda i: (0, 0)), _mod_spec(ish, d), _mod_spec(isc, d)],
                          out_specs=row, out_shape=jax.ShapeDtypeStruct((t, d), BF16),
                          compiler_params=_params(("parallel",)))(x, g, mod4, mod4)


def _normmod_bwd(x, g, mod4, which, dh, dres, name, comm=None):
    t, d = x.shape
    tm = _tm()
    ish, isc = (0, 1) if which == 0 else (3, 4)

    def body(x_ref, g_ref, sh_ref, sc_ref, dh_ref, dres_ref, dx_ref, dg_ref, dsh_ref, dsc_ref):
        i = pl.program_id(0)
        _, vjp = jax.vjp(_normmod, x_ref[...], g_ref[...], sh_ref[...], sc_ref[...])
        dx, dg, dsh, dsc = vjp(dh_ref[...])
        dx_ref[...] = dres_ref[...] + dx

        @pl.when(i == 0)
        def _():
            dg_ref[...] = jnp.zeros_like(dg_ref)

        @pl.when(i <= 1)
        def _():
            dsh_ref[...] = jnp.zeros_like(dsh_ref)
            dsc_ref[...] = jnp.zeros_like(dsc_ref)

        dg_ref[...] += dg
        dsh_ref[...] += dsh
        dsc_ref[...] += dsc

    row = pl.BlockSpec((tm, d), lambda i: (i, 0))
    vec = pl.BlockSpec((1, d), lambda i: (0, 0))
    svec = pl.BlockSpec((None, 1, d), lambda i: (_stream(i), 0, 0))
    return _call(
        body, name=name, grid=(t // tm,),
        in_specs=[row, vec, _mod_spec(ish, d), _mod_spec(isc, d), row, row],
        out_specs=[row, vec, svec, svec],
        out_shape=[jax.ShapeDtypeStruct((t, d), F32), jax.ShapeDtypeStruct((1, d), F32),
                   jax.ShapeDtypeStruct((2, 1, d), F32), jax.ShapeDtypeStruct((2, 1, d), F32)],
        sem=("arbitrary",), args=(x, g, mod4, mod4, dh, dres), comm=comm)


def _gate_res_fwd(x, f, mod4, chunk, name):
    t, d = x.shape
    tm = _tm()

    def body(x_ref, f_ref, g_ref, o_ref):
        o_ref[...] = x_ref[...] + g_ref[...] * f_ref[...]

    row = pl.BlockSpec((tm, d), lambda i: (i, 0))
    return pl.pallas_call(body, name=name, grid=(t // tm,), in_specs=[row, row, _mod_spec(chunk, d)], out_specs=row,
                          out_shape=jax.ShapeDtypeStruct((t, d), F32), compiler_params=_params(("parallel",)))(x, f, mod4)


def _gate_res_bwd(dx, f, mod4, chunk, name):
    t, d = dx.shape
    tm = _tm()

    def body(dx_ref, f_ref, g_ref, o_ref, dg_ref):
        i = pl.program_id(0)
        dxv = dx_ref[...]
        o_ref[...] = (dxv * g_ref[...]).astype(BF16)

        @pl.when(i <= 1)
        def _():
            dg_ref[...] = jnp.zeros_like(dg_ref)

        dg_ref[...] += jnp.sum(dxv * f_ref[...], axis=0, keepdims=True)

    row = pl.BlockSpec((tm, d), lambda i: (i, 0))
    return pl.pallas_call(
        body, name=name, grid=(t // tm,), in_specs=[row, row, _mod_spec(chunk, d)],
        out_specs=[row, pl.BlockSpec((None, 1, d), lambda i: (_stream(i), 0, 0))],
        out_shape=[jax.ShapeDtypeStruct((t, d), BF16), jax.ShapeDtypeStruct((2, 1, d), F32)],
        compiler_params=_params(("arbitrary",)))(dx, f, mod4)


def _loss_head(x, final_g, target, name):
    t, d = x.shape
    tm = _tm()

    def loss_fn(xv, g, tgt):
        y = (xv * lax.rsqrt(jnp.mean(xv * xv, axis=-1, keepdims=True) + EPS)) * g
        err = y - tgt
        return 0.5 * jnp.sum(jnp.mean(err * err, axis=-1, keepdims=True))

    def body(x_ref, g_ref, t_ref, l_ref, dx_ref, dg_ref):
        i = pl.program_id(0)

        @pl.when(i == 0)
        def _():
            l_ref[...] = jnp.zeros_like(l_ref)
            dg_ref[...] = jnp.zeros_like(dg_ref)
            dx_ref[...] = jnp.zeros_like(dx_ref)

        @pl.when(i > 0)
        def _():
            l, (dx, dg) = jax.value_and_grad(loss_fn, argnums=(0, 1))(x_ref[...], g_ref[...], t_ref[...])
            l_ref[...] += jnp.full(l_ref.shape, l, F32)
            dx_ref[...] = dx
            dg_ref[...] += dg

    row = pl.BlockSpec((tm, d), lambda i: (i, 0))
    vec = pl.BlockSpec((1, d), lambda i: (0, 0))
    return pl.pallas_call(
        body, name=name, grid=(t // tm,),
        in_specs=[row, vec, pl.BlockSpec((tm, d), lambda i: (jnp.maximum(i - 1, 0), 0))],
        out_specs=[pl.BlockSpec((SUBLANES, LANES), lambda i: (0, 0)), row, vec],
        out_shape=[jax.ShapeDtypeStruct((SUBLANES, LANES), F32), jax.ShapeDtypeStruct((t, d), F32),
                   jax.ShapeDtypeStruct((1, d), F32)],
        compiler_params=_params(("arbitrary",)))(x, final_g, target)


def _swap_quarters(x):
    half, nf = RET_DK // 2, RET_DK // 4
    lane = lax.broadcasted_iota(jnp.int32, x.shape, 1)
    return jnp.where((lane % half) < nf, pltpu.roll(x, RET_DK - nf, 1), pltpu.roll(x, nf, 1))


def _rope(x, cos, sin):
    return x * cos + _swap_quarters(x) * sin


def _rope_t(y, cos, sin):
    return y * cos + _swap_quarters(y * sin)


def _ret_consts(d):
    c = RET_CHUNK
    ii = lax.broadcasted_iota(jnp.int32, (c, 1), 0).astype(F32)
    jj = lax.broadcasted_iota(jnp.int32, (1, c), 1).astype(F32)
    fwd = d == 0
    sgn = jnp.where(fwd, 1.0, -1.0).astype(F32)
    pos = jnp.where(fwd, ii, c - 1.0 - ii)
    return sgn * (ii - jj), pos


def _ret_step(lgt, state, q, k, v, diff, pos):
    c = float(RET_CHUNK)
    lg = -(jnp.maximum(-lgt, 0.0) + jnp.log1p(jnp.exp(-jnp.abs(lgt))))
    lower = diff >= 0
    decay = jnp.where(lower, jnp.exp(jnp.where(lower, diff, 0.0) * lg), 0.0)
    xi = jnp.exp((pos + 1.0) * lg)
    zeta = jnp.exp((c - 1.0 - pos) * lg)
    gch = jnp.exp(c * lg)
    inner = dot_nt(q, k) * decay
    out = dot_nn(inner, v) + dot_nn(q, state) * xi
    new_state = state * gch + dot_tn(k * zeta, v)
    return out, new_state


def _chunk_order():
    nc, nch = CTX_LEN // RET_CHUNK, _t_rows() // RET_CHUNK
    fwd = list(range(nch))
    bwd = list(range(nc - 1, -1, -1)) + list(range(nch - 1, nc - 1, -1))
    return jnp.asarray(np.array([fwd, bwd], np.int32))


def _ret_fwd(p, cos, sin, decay, order, name):
    t = p.shape[0]
    c, dk, dv, nh = RET_CHUNK, RET_DK, RET_DV, RET_HEADS
    nch = t // c
    off = _offsets()
    wqk, wv = nh * dk, nh * dv
    assert off["q"] % wqk == 0 and off["k"] % wqk == 0 and off["v"] % wv == 0
    qb, kb, vb = off["q"] // wqk, off["k"] // wqk, off["v"] // wv
    scale = RET_DK ** -0.5

    def body(ord_ref, dec_ref, q_ref, k_ref, v_ref, cos_ref, sin_ref, o_ref, st_ref, state):
        d, s = pl.program_id(0), pl.program_id(1)

        @pl.when(s == 0)
        def _():
            state[...] = jnp.zeros_like(state)

        diff, pos = _ret_consts(d)
        cosv, sinv = cos_ref[...], sin_ref[...]
        for h in range(nh):
            st = state[h]
            st_ref[h] = st
            lgt = jnp.full((1, 1), dec_ref[d, h], F32)
            q = _rope(q_ref[:, h * dk:(h + 1) * dk], cosv, sinv) * scale
            k = _rope(k_ref[:, h * dk:(h + 1) * dk], cosv, sinv)
            out, ns = _ret_step(lgt, st, q, k, v_ref[:, h * dv:(h + 1) * dv], diff, pos)
            o_ref[:, h * dv:(h + 1) * dv] = out
            state[h] = ns

    grid_spec = pltpu.PrefetchScalarGridSpec(
        num_scalar_prefetch=1, grid=(2, nch),
        in_specs=[pl.BlockSpec(memory_space=pltpu.SMEM),
                  pl.BlockSpec((c, wqk), lambda d, s, o: (o[d, s], qb)),
                  pl.BlockSpec((c, wqk), lambda d, s, o: (o[d, s], kb)),
                  pl.BlockSpec((c, wv), lambda d, s, o: (o[d, s], vb)),
                  pl.BlockSpec((c, dk), lambda d, s, o: (o[d, s], 0)),
                  pl.BlockSpec((c, dk), lambda d, s, o: (o[d, s], 0))],
        out_specs=[pl.BlockSpec((None, c, wv), lambda d, s, o: (d, o[d, s], 0)),
                   pl.BlockSpec((None, nh, None, dk, dv), lambda d, s, o: (d, 0, s, 0, 0))],
        scratch_shapes=[pltpu.VMEM((nh, dk, dv), F32)])
    return pl.pallas_call(
        body, name=name, grid_spec=grid_spec,
        out_shape=[jax.ShapeDtypeStruct((2, t, wv), F32), jax.ShapeDtypeStruct((2, nh, nch, dk, dv), F32)],
        compiler_params=_params(("arbitrary", "arbitrary")))(order, decay, p, p, p, cos, sin)


def _ret_bwd(p, cos, sin, decay, order, states, do, name):
    t = p.shape[0]
    c, dk, dv, nh = RET_CHUNK, RET_DK, RET_DV, RET_HEADS
    nch = t // c
    off = _offsets()
    wqk, wv = nh * dk, nh * dv
    qb, kb, vb = off["q"] // wqk, off["k"] // wqk, off["v"] // wv
    scale = RET_DK ** -0.5

    def body(ord_ref, dec_ref, q_ref, k_ref, v_ref, cos_ref, sin_ref, st_ref, do_ref,
             dq_ref, dk_ref, dv_ref, dd_ref, dstate):
        d, s = pl.program_id(0), pl.program_id(1)

        @pl.when(s == 0)
        def _():
            dstate[...] = jnp.zeros_like(dstate)
            dd_ref[...] = jnp.zeros_like(dd_ref)

        diff, pos = _ret_consts(d)
        cosv, sinv = cos_ref[...], sin_ref[...]
        for h in range(nh):
            qk, vv = slice(h * dk, (h + 1) * dk), slice(h * dv, (h + 1) * dv)
            lgt = jnp.full((1, 1), dec_ref[d, h], F32)
            q = _rope(q_ref[:, qk], cosv, sinv) * scale
            k = _rope(k_ref[:, qk], cosv, sinv)
            _, vjp = jax.vjp(lambda a, b, cq, ck, cv: _ret_step(a, b, cq, ck, cv, diff, pos),
                             lgt, st_ref[h], q, k, v_ref[:, vv])
            dlgt, dst, dq, dkk, dvv = vjp((do_ref[:, vv], dstate[h]))
            dstate[h] = dst
            dq_ref[:, qk] = _rope_t(dq * scale, cosv, sinv)
            dk_ref[:, qk] = _rope_t(dkk, cosv, sinv)
            dv_ref[:, vv] = dvv
            dd_ref[h] += jnp.broadcast_to(dlgt, (SUBLANES, LANES))

    rev = lambda o, d, s: o[d, nch - 1 - s]
    grid_spec = pltpu.PrefetchScalarGridSpec(
        num_scalar_prefetch=1, grid=(2, nch),
        in_specs=[pl.BlockSpec(memory_space=pltpu.SMEM),
                  pl.BlockSpec((c, wqk), lambda d, s, o: (rev(o, d, s), qb)),
                  pl.BlockSpec((c, wqk), lambda d, s, o: (rev(o, d, s), kb)),
                  pl.BlockSpec((c, wv), lambda d, s, o: (rev(o, d, s), vb)),
                  pl.BlockSpec((c, dk), lambda d, s, o: (rev(o, d, s), 0)),
                  pl.BlockSpec((c, dk), lambda d, s, o: (rev(o, d, s), 0)),
                  pl.BlockSpec((None, nh, None, dk, dv), lambda d, s, o: (d, 0, nch - 1 - s, 0, 0)),
                  pl.BlockSpec((c, wv), lambda d, s, o: (rev(o, d, s), 0))],
        out_specs=[pl.BlockSpec((None, c, wqk), lambda d, s, o: (d, rev(o, d, s), 0)),
                   pl.BlockSpec((None, c, wqk), lambda d, s, o: (d, rev(o, d, s), 0)),
                   pl.BlockSpec((None, c, wv), lambda d, s, o: (d, rev(o, d, s), 0)),
                   pl.BlockSpec((None, nh, SUBLANES, LANES), lambda d, s, o: (d, 0, 0, 0))],
        scratch_shapes=[pltpu.VMEM((nh, dk, dv), F32)])
    return pl.pallas_call(
        body, name=name, grid_spec=grid_spec,
        out_shape=[jax.ShapeDtypeStruct((2, t, wqk), F32), jax.ShapeDtypeStruct((2, t, wqk), F32),
                   jax.ShapeDtypeStruct((2, t, wv), F32), jax.ShapeDtypeStruct((2, nh, SUBLANES, LANES), F32)],
        compiler_params=_params(("arbitrary", "arbitrary")))(order, decay, p, p, p, cos, sin, states, do)


def _ggn_head(of, ob, gate, g):
    o = of + ob
    mu = jnp.mean(o, axis=-1, keepdims=True)
    var = jnp.mean(jnp.square(o - mu), axis=-1, keepdims=True)
    return ((o - mu) * lax.rsqrt(var + EPS) * g) * _silu(gate)


def _ggn_fwd(o2, p, gn_g, name):
    t = p.shape[0]
    tm, w, dv = _tm(), _ret_w(), RET_DV
    gb = _offsets()["g"] // w

    def body(o_ref, gate_ref, g_ref, out_ref):
        for h in range(RET_HEADS):
            sl = slice(h * dv, (h + 1) * dv)
            out_ref[:, sl] = _ggn_head(o_ref[0, :, sl], o_ref[1, :, sl], gate_ref[:, sl], g_ref[:, sl]).astype(BF16)

    return pl.pallas_call(
        body, name=name, grid=(t // tm,),
        in_specs=[pl.BlockSpec((2, tm, w), lambda i: (0, i, 0)), pl.BlockSpec((tm, w), lambda i: (i, gb)),
                  pl.BlockSpec((1, w), lambda i: (0, 0))],
        out_specs=pl.BlockSpec((tm, w), lambda i: (i, 0)), out_shape=jax.ShapeDtypeStruct((t, w), BF16),
        compiler_params=_params(("parallel",)))(o2, p, gn_g)


def _ggn_bwd(o2, p, gn_g, dmix, name):
    t = p.shape[0]
    tm, w, dv = _tm(), _ret_w(), RET_DV
    gb = _offsets()["g"] // w

    def body(o_ref, gate_ref, g_ref, dy_ref, do_ref, dgate_ref, dg_ref):
        i = pl.program_id(0)

        @pl.when(i == 0)
        def _():
            dg_ref[...] = jnp.zeros_like(dg_ref)

        for h in range(RET_HEADS):
            sl = slice(h * dv, (h + 1) * dv)
            _, vjp = jax.vjp(_ggn_head, o_ref[0, :, sl], o_ref[1, :, sl], gate_ref[:, sl], g_ref[:, sl])
            do, _, dgate, dg = vjp(dy_ref[:, sl])
            do_ref[:, sl] = do
            dgate_ref[:, sl] = dgate
            dg_ref[:, sl] += dg

    row = pl.BlockSpec((tm, w), lambda i: (i, 0))
    return pl.pallas_call(
        body, name=name, grid=(t // tm,),
        in_specs=[pl.BlockSpec((2, tm, w), lambda i: (0, i, 0)), pl.BlockSpec((tm, w), lambda i: (i, gb)),
                  pl.BlockSpec((1, w), lambda i: (0, 0)), row],
        out_specs=[row, row, pl.BlockSpec((1, w), lambda i: (0, 0))],
        out_shape=[jax.ShapeDtypeStruct((t, w), F32), jax.ShapeDtypeStruct((t, w), F32),
                   jax.ShapeDtypeStruct((1, w), F32)],
        compiler_params=_params(("arbitrary",)))(o2, p, gn_g, dmix)


def _halo(k):
    return SUBLANES * ((k // 2 + SUBLANES - 1) // SUBLANES)


def _halo_specs(width, colblock, h, tm):
    r = tm // h
    return [pl.BlockSpec((h, width), lambda i, *_: (jnp.maximum(i * r - 1, 0), colblock(*_))),
            pl.BlockSpec((tm, width), lambda i, *_: (i, colblock(*_))),
            pl.BlockSpec((h, width), lambda i, *_: (jnp.minimum((i + 1) * r, (_t_rows() // h) - 1), colblock(*_)))]


def _fill_ext(ext_ref, prev, cur, nxt, i, h, tm):
    nt = _t_rows() // tm
    ext_ref[0:h, :] = jnp.where(i >= 2, prev, 0.0)
    ext_ref[h:h + tm, :] = cur
    ext_ref[h + tm:h + tm + h, :] = jnp.where((i >= 1) & (i <= nt - 2), nxt, 0.0)


def _corr(ext_ref, w_ref, k, h, tm, flip):
    pad = k // 2
    acc = None
    for kk in range(k):
        o = h + (pad - kk if flip else kk - pad)
        term = w_ref[kk:kk + 1, :] * ext_ref[o:o + tm, :]
        acc = term if acc is None else acc + term
    return acc


def _conv_post(u2, ln_g, ln_b, pw):
    mu = jnp.mean(u2, axis=-1, keepdims=True)
    var = jnp.mean(jnp.square(u2 - mu), axis=-1, keepdims=True)
    y = (u2 - mu) * lax.rsqrt(var + EPS) * ln_g + ln_b
    return dot_nn(_silu(y), pw)


def _conv_fwd(p, dw_w, dw_b, ln_g, ln_b, pw, name):
    t = p.shape[0]
    tm, w, k = _tm(), CONV_W, CONV_K
    h = _halo(k)
    off = _offsets()
    ab, bb = off["a"] // w, off["b"] // w

    def body(ap, ac, an, bp, bc, bn, w_ref, b_ref, g_ref, beta_ref, pw_ref, u2_ref, out_ref, ext):
        i = pl.program_id(0)
        glu = lambda a, b: a * _sigmoid(b)
        _fill_ext(ext, glu(ap[...], bp[...]), glu(ac[...], bc[...]), glu(an[...], bn[...]), i, h, tm)
        u2 = _corr(ext, w_ref, k, h, tm, False) + b_ref[...]
        u2_ref[...] = u2
        out_ref[...] = _conv_post(u2, g_ref[...], beta_ref[...], pw_ref[...]).astype(BF16)

    vec = pl.BlockSpec((1, w), lambda i: (0, 0))
    row = pl.BlockSpec((tm, w), lambda i: (i, 0))
    return pl.pallas_call(
        body, name=name, grid=(t // tm,),
        in_specs=_halo_specs(w, lambda: ab, h, tm) + _halo_specs(w, lambda: bb, h, tm)
        + [pl.BlockSpec((k, w), lambda i: (0, 0)), vec, vec, vec, pl.BlockSpec((w, w), lambda i: (0, 0))],
        out_specs=[row, row],
        out_shape=[jax.ShapeDtypeStruct((t, w), F32), jax.ShapeDtypeStruct((t, w), BF16)],
        scratch_shapes=[pltpu.VMEM((tm + 2 * h, w), F32)],
        compiler_params=_params(("parallel",)))(p, p, p, p, p, p, dw_w, dw_b, ln_g, ln_b, pw)


def _conv_bwd1(u2, dmix, ln_g, ln_b, pw, name):
    t = u2.shape[0]
    tm, w = _tm(), CONV_W
    cb = _ret_w() // w

    def body(u2_ref, dy_ref, g_ref, beta_ref, pw_ref, du2_ref, dg_ref, db_ref, dpw_ref):
        i = pl.program_id(0)

        @pl.when(i == 0)
        def _():
            dg_ref[...] = jnp.zeros_like(dg_ref)
            db_ref[...] = jnp.zeros_like(db_ref)
            dpw_ref[...] = jnp.zeros_like(dpw_ref)

        _, vjp = jax.vjp(_conv_post, u2_ref[...], g_ref[...], beta_ref[...], pw_ref[...])
        du2, dg, db, dpw = vjp(dy_ref[...])
        du2_ref[...] = du2
        dg_ref[...] += dg
        db_ref[...] += db
        dpw_ref[...] += dpw

    vec = pl.BlockSpec((1, w), lambda i: (0, 0))
    row = pl.BlockSpec((tm, w), lambda i: (i, 0))
    mat = pl.BlockSpec((w, w), lambda i: (0, 0))
    return pl.pallas_call(
        body, name=name, grid=(t // tm,),
        in_specs=[row, pl.BlockSpec((tm, w), lambda i: (i, cb)), vec, vec, mat],
        out_specs=[row, vec, vec, mat],
        out_shape=[jax.ShapeDtypeStruct((t, w), F32), jax.ShapeDtypeStruct((1, w), F32),
                   jax.ShapeDtypeStruct((1, w), F32), jax.ShapeDtypeStruct((w, w), F32)],
        compiler_params=_params(("arbitrary",)))(u2, dmix, ln_g, ln_b, pw)


def _conv_bwd2(du2, p, dw_w, name, comm=None):
    t = p.shape[0]
    tm, w, k = _tm(), CONV_W, CONV_K
    h = _halo(k)
    pad = k // 2
    off = _offsets()
    ab, bb = off["a"] // w, off["b"] // w

    def body(dp, dc, dn, ap, ac, an, bp, bc, bn, w_ref, da_ref, db_ref, dw_ref, dbias_ref, ext_d, ext_u):
        i = pl.program_id(0)

        @pl.when(i == 0)
        def _():
            dw_ref[...] = jnp.zeros_like(dw_ref)
            dbias_ref[...] = jnp.zeros_like(dbias_ref)

        glu = lambda a, b: a * _sigmoid(b)
        a, b, d = ac[...], bc[...], dc[...]
        _fill_ext(ext_d, dp[...], d, dn[...], i, h, tm)
        _fill_ext(ext_u, glu(ap[...], bp[...]), glu(a, b), glu(an[...], bn[...]), i, h, tm)
        du = _corr(ext_d, w_ref, k, h, tm, True)
        sg = _sigmoid(b)
        da_ref[...] = du * sg
        db_ref[...] = du * a * sg * (1.0 - sg)
        dbias_ref[...] += jnp.sum(d, axis=0, keepdims=True)
        for kk in range(k):
            o = h + kk - pad
            dw_ref[kk:kk + 1, :] += jnp.sum(d * ext_u[o:o + tm, :], axis=0, keepdims=True)

    vec = pl.BlockSpec((1, w), lambda i: (0, 0))
    row = pl.BlockSpec((tm, w), lambda i: (i, 0))
    kw = pl.BlockSpec((k, w), lambda i: (0, 0))
    return _call(
        body, name=name, grid=(t // tm,),
        in_specs=_halo_specs(w, lambda: 0, h, tm) + _halo_specs(w, lambda: ab, h, tm)
        + _halo_specs(w, lambda: bb, h, tm) + [kw],
        out_specs=[row, row, kw, vec],
        out_shape=[jax.ShapeDtypeStruct((t, w), F32), jax.ShapeDtypeStruct((t, w), F32),
                   jax.ShapeDtypeStruct((k, w), F32), jax.ShapeDtypeStruct((1, w), F32)],
        scratch=[pltpu.VMEM((tm + 2 * h, w), F32), pltpu.VMEM((tm + 2 * h, w), F32)],
        sem=("arbitrary",), args=(du2, du2, du2, p, p, p, p, p, p, dw_w), comm=comm)


def _ffn_tc():
    return _tile(D_FF, 512, LANES)


def _ffn_act_fwd(u, dw_w, dw_b, name):
    t = u.shape[0]
    tm, k, tc = _tm(), FFN_K, _ffn_tc()
    h = _halo(k)
    nj = D_FF // tc

    def body(vp, vc, vn, gp, gc, gn, wv, wg, bv, bg, out_ref, ext_v, ext_g):
        i = pl.program_id(0)
        _fill_ext(ext_v, vp[...], vc[...], vn[...], i, h, tm)
        _fill_ext(ext_g, gp[...], gc[...], gn[...], i, h, tm)
        for r0 in range(0, tm, ROW_CHUNK):
            val = _corr(ext_v, wv, k, h + r0, ROW_CHUNK, False) + bv[...]
            gate = _corr(ext_g, wg, k, h + r0, ROW_CHUNK, False) + bg[...]
            out_ref[r0:r0 + ROW_CHUNK, :] = (_silu(gate) * val).astype(BF16)

    wspec = lambda s: pl.BlockSpec((k, tc), lambda i, j: (0, j + s))
    bspec = lambda s: pl.BlockSpec((1, tc), lambda i, j: (0, j + s))
    return pl.pallas_call(
        body, name=name, grid=(t // tm, nj),
        in_specs=_halo_specs(tc, lambda j: j, h, tm) + _halo_specs(tc, lambda j: j + nj, h, tm)
        + [wspec(0), wspec(nj), bspec(0), bspec(nj)],
        out_specs=pl.BlockSpec((tm, tc), lambda i, j: (i, j)),
        out_shape=jax.ShapeDtypeStruct((t, D_FF), BF16),
        scratch_shapes=[pltpu.VMEM((tm + 2 * h, tc), F32), pltpu.VMEM((tm + 2 * h, tc), F32)],
        compiler_params=_params(("parallel", "parallel")))(u, u, u, u, u, u, dw_w, dw_w, dw_b, dw_b)


def _ffn_act_bwd1(u, da, dw_w, dw_b, name):
    t = u.shape[0]
    tm, k, tc = _tm(), FFN_K, _ffn_tc()
    h = _halo(k)
    nj = D_FF // tc

    def body(vp, vc, vn, gp, gc, gn, wv, wg, bv, bg, da_ref, dv_ref, dg_ref, ext_v, ext_g):
        i = pl.program_id(0)
        _fill_ext(ext_v, vp[...], vc[...], vn[...], i, h, tm)
        _fill_ext(ext_g, gp[...], gc[...], gn[...], i, h, tm)
        for r0 in range(0, tm, ROW_CHUNK):
            rows = slice(r0, r0 + ROW_CHUNK)
            val = _corr(ext_v, wv, k, h + r0, ROW_CHUNK, False) + bv[...]
            gate = _corr(ext_g, wg, k, h + r0, ROW_CHUNK, False) + bg[...]
            _, vjp = jax.vjp(lambda a, b: _silu(b) * a, val, gate)
            dval, dgate = vjp(da_ref[rows, :])
            dv_ref[rows, :] = dval
            dg_ref[rows, :] = dgate

    wspec = lambda s: pl.BlockSpec((k, tc), lambda i, j: (0, j + s))
    bspec = lambda s: pl.BlockSpec((1, tc), lambda i, j: (0, j + s))
    dc = pl.pallas_call(
        body, name=name, grid=(t // tm, nj),
        in_specs=_halo_specs(tc, lambda j: j, h, tm) + _halo_specs(tc, lambda j: j + nj, h, tm)
        + [wspec(0), wspec(nj), bspec(0), bspec(nj), pl.BlockSpec((tm, tc), lambda i, j: (i, j))],
        out_specs=[pl.BlockSpec((tm, tc), lambda i, j: (i, j)), pl.BlockSpec((tm, tc), lambda i, j: (i, j))],
        out_shape=[jax.ShapeDtypeStruct((t, D_FF), F32), jax.ShapeDtypeStruct((t, D_FF), F32)],
        scratch_shapes=[pltpu.VMEM((tm + 2 * h, tc), F32), pltpu.VMEM((tm + 2 * h, tc), F32)],
        compiler_params=_params(("parallel", "parallel")))(u, u, u, u, u, u, dw_w, dw_w, dw_b, dw_b, da)
    return dc


def _dwconv_bwd(dc, u, dw_w, colblock, name):
    t = u.shape[0]
    tm, k, tc = _tm(), FFN_K, _ffn_tc()
    h = _halo(k)
    pad = k // 2
    nj = D_FF // tc

    def body(dp, dcur, dn, up, uc, un, w_ref, du_ref, dw_ref, dbias_ref, ext_d, ext_u):
        i = pl.program_id(1)

        @pl.when(i == 0)
        def _():
            dw_ref[...] = jnp.zeros_like(dw_ref)
            dbias_ref[...] = jnp.zeros_like(dbias_ref)

        _fill_ext(ext_d, dp[...], dcur[...], dn[...], i, h, tm)
        _fill_ext(ext_u, up[...], uc[...], un[...], i, h, tm)
        acc_b = jnp.zeros((ROW_CHUNK, tc), F32)
        acc_w = [jnp.zeros((ROW_CHUNK, tc), F32) for _ in range(k)]
        for r0 in range(0, tm, ROW_CHUNK):
            d = ext_d[h + r0:h + r0 + ROW_CHUNK, :]
            du_ref[r0:r0 + ROW_CHUNK, :] = _corr(ext_d, w_ref, k, h + r0, ROW_CHUNK, True).astype(BF16)
            acc_b = acc_b + d
            for kk in range(k):
                o = h + r0 + kk - pad
                acc_w[kk] = acc_w[kk] + d * ext_u[o:o + ROW_CHUNK, :]
        dbias_ref[...] += jnp.sum(acc_b, axis=0, keepdims=True)
        for kk in range(k):
            dw_ref[kk:kk + 1, :] += jnp.sum(acc_w[kk], axis=0, keepdims=True)

    def hs(cb):
        r = tm // h
        return [pl.BlockSpec((h, tc), lambda j, i: (jnp.maximum(i * r - 1, 0), cb(j))),
                pl.BlockSpec((tm, tc), lambda j, i: (i, cb(j))),
                pl.BlockSpec((h, tc), lambda j, i: (jnp.minimum((i + 1) * r, (_t_rows() // h) - 1), cb(j)))]

    return pl.pallas_call(
        body, name=name, grid=(nj, t // tm),
        in_specs=hs(lambda j: j) + hs(lambda j: j + colblock) + [pl.BlockSpec((k, tc), lambda j, i: (0, j + colblock))],
        out_specs=[pl.BlockSpec((tm, tc), lambda j, i: (i, j)), pl.BlockSpec((k, tc), lambda j, i: (0, j)),
                   pl.BlockSpec((1, tc), lambda j, i: (0, j))],
        out_shape=[jax.ShapeDtypeStruct((t, D_FF), BF16), jax.ShapeDtypeStruct((k, D_FF), F32),
                   jax.ShapeDtypeStruct((1, D_FF), F32)],
        scratch_shapes=[pltpu.VMEM((tm + 2 * h, tc), F32), pltpu.VMEM((tm + 2 * h, tc), F32)],
        compiler_params=_params(("parallel", "arbitrary")))(dc, dc, dc, u, u, u, dw_w)


def _na_geometry(rq):
    ncb = CTX_LEN // GRID_W
    rows_n = SEQ // GRID_W
    r = jnp.maximum(rq - ncb, 0)
    kstart = jnp.clip(r - NA_ROWS // 2, 0, rows_n - NA_ROWS)
    base = kstart - r + NA_ROWS - 1
    return rq >= ncb, kstart, base


def _na_core(q, kl, vl, kc, vc, bias, mask):
    qs = q * (NA_DH ** -0.5)
    s_l = jnp.where(mask, dot_nt(qs, kl) + bias, NEG)
    s_c = dot_nt(qs, kc)
    m = lax.stop_gradient(jnp.maximum(jnp.max(s_l, axis=1, keepdims=True), jnp.max(s_c, axis=1, keepdims=True)))
    e_l, e_c = jnp.exp(s_l - m), jnp.exp(s_c - m)
    inv = 1.0 / (jnp.sum(e_l, axis=1, keepdims=True) + jnp.sum(e_c, axis=1, keepdims=True))
    return dot_nn(e_l * inv, vl) + dot_nn(e_c * inv, vc)


def _na_mask(is_lat):
    nl = NA_ROWS * GRID_W
    q = lax.broadcasted_iota(jnp.int32, (GRID_W, nl), 0)
    w = lax.broadcasted_iota(jnp.int32, (GRID_W, nl), 1) % GRID_W
    cs = jnp.clip(q - NA_COLS // 2, 0, GRID_W - NA_COLS)
    return (w >= cs) & (w < cs + NA_COLS) & is_lat


def _na_bias(rb_ref):
    assert 2 * GRID_W == LANES
    lane = lax.broadcasted_iota(jnp.int32, (GRID_W, LANES), 1)
    tiles = []
    for kp in range(NA_ROWS // 2):
        ev = jnp.broadcast_to(rb_ref[2 * kp:2 * kp + 1, :], (GRID_W, LANES))
        od = jnp.broadcast_to(rb_ref[2 * kp + 1:2 * kp + 2, :], (GRID_W, LANES))
        ev = pltpu.roll(ev, LANES - (NA_COLS - 1), 1, stride=1, stride_axis=0)
        od = pltpu.roll(od, LANES - (NA_COLS - 1) - GRID_W, 1, stride=1, stride_axis=0)
        tiles.append(jnp.where(lane < GRID_W, ev, od))
    return jnp.concatenate(tiles, axis=1)


def _na_dbias(dbias, drb_ref):
    qi = lax.broadcasted_iota(jnp.int32, (GRID_W, GRID_W), 0)
    qj = lax.broadcasted_iota(jnp.int32, (GRID_W, GRID_W), 1)
    flip = (qi + qj == GRID_W - 1).astype(F32)
    rev = lax.dot_general(flip, dbias, (((1,), (0,)), ((), ())), precision=lax.Precision.HIGHEST,
                          preferred_element_type=F32)
    lane = lax.broadcasted_iota(jnp.int32, (GRID_W, LANES), 1)
    s_ev = LANES - (GRID_W - NA_COLS)
    for kp in range(NA_ROWS // 2):
        tile = rev[:, kp * LANES:(kp + 1) * LANES]
        ev = pltpu.roll(jnp.where(lane < GRID_W, tile, 0.0), s_ev, 1, stride=1, stride_axis=0)
        od = pltpu.roll(jnp.where(lane >= GRID_W, tile, 0.0), s_ev - GRID_W, 1, stride=1, stride_axis=0)
        drb_ref[2 * kp:2 * kp + 1, :] += jnp.sum(ev, axis=0, keepdims=True)
        drb_ref[2 * kp + 1:2 * kp + 2, :] += jnp.sum(od, axis=0, keepdims=True)


def _na_hps():
    return 2 if NA_HEADS % 2 == 0 else 1


def _na_specs(p_offsets):
    t = _t_rows()
    hps = _na_hps()
    wd = hps * NA_DH
    assert all(p_offsets[n] % wd == 0 for n in ("nq", "nk", "nv"))
    qb, kb, vb = (p_offsets[n] // wd for n in ("nq", "nk", "nv"))
    return [pl.BlockSpec((GRID_W, wd), lambda h, r: (r, qb + h)),
            pl.BlockSpec((t, wd), lambda h, r: (0, kb + h)),
            pl.BlockSpec((t, wd), lambda h, r: (0, vb + h)),
            pl.BlockSpec((hps, None, NA_ROWS, LANES), lambda h, r: (h, _na_geometry(r)[2], 0, 0))]


def _na_fwd(p, rb, name, comm=None):
    t = p.shape[0]
    dh, nl = NA_DH, NA_ROWS * GRID_W

    hps = _na_hps()

    def body(q_ref, k_ref, v_ref, rb_ref, out_ref):
        rq = pl.program_id(1)
        is_lat, kstart, _ = _na_geometry(rq)
        start = pl.multiple_of(CTX_LEN + kstart * GRID_W, GRID_W)
        mask = _na_mask(is_lat)
        for hh in range(hps):
            cols = slice(hh * dh, (hh + 1) * dh)
            out = _na_core(q_ref[:, cols], k_ref[pl.ds(start, nl), cols], v_ref[pl.ds(start, nl), cols],
                           k_ref[0:CTX_LEN, cols], v_ref[0:CTX_LEN, cols], _na_bias(rb_ref.at[hh]), mask)
            out_ref[:, cols] = out.astype(BF16)

    res, cres = _call(
        body, name=name, grid=(NA_HEADS // hps, t // GRID_W), in_specs=_na_specs(_offsets()),
        out_specs=[pl.BlockSpec((GRID_W, hps * dh), lambda h, r: (r, h))],
        out_shape=[jax.ShapeDtypeStruct((t, _na_w()), BF16)],
        sem=("parallel", "arbitrary"), args=(p, p, p, rb), comm=comm)
    return res[0], cres


def _na_bwd(p, rb, dmix, name, comm=None):
    t = p.shape[0]
    dh, nl = NA_DH, NA_ROWS * GRID_W

    hps = _na_hps()
    wd = hps * dh
    assert ((_ret_w() + CONV_W) // dh) % hps == 0
    ob = (_ret_w() + CONV_W) // wd

    def body(q_ref, k_ref, v_ref, rb_ref, dy_ref, dq_ref, dk_ref, dv_ref, drb_ref):
        rq = pl.program_id(1)
        is_lat, kstart, base = _na_geometry(rq)
        _, _, prev_base = _na_geometry(rq - 1)
        start = pl.multiple_of(CTX_LEN + kstart * GRID_W, GRID_W)

        @pl.when(rq == 0)
        def _():
            dk_ref[...] = jnp.zeros_like(dk_ref)
            dv_ref[...] = jnp.zeros_like(dv_ref)

        @pl.when((rq == 0) | (base != prev_base))
        def _():
            drb_ref[...] = jnp.zeros_like(drb_ref)

        mask = _na_mask(is_lat)
        for hh in range(hps):
            cols = slice(hh * dh, (hh + 1) * dh)
            _, vjp = jax.vjp(lambda *a: _na_core(*a, mask), q_ref[:, cols], k_ref[pl.ds(start, nl), cols],
                             v_ref[pl.ds(start, nl), cols], k_ref[0:CTX_LEN, cols], v_ref[0:CTX_LEN, cols],
                             _na_bias(rb_ref.at[hh]))
            dq, dkl, dvl, dkc, dvc, dbias = vjp(dy_ref[:, cols])
            dq_ref[:, cols] = dq
            dk_ref[pl.ds(start, nl), cols] += dkl
            dv_ref[pl.ds(start, nl), cols] += dvl
            dk_ref[0:CTX_LEN, cols] += dkc
            dv_ref[0:CTX_LEN, cols] += dvc
            _na_dbias(dbias, drb_ref.at[hh])

    return _call(
        body, name=name, grid=(NA_HEADS // hps, t // GRID_W),
        in_specs=_na_specs(_offsets()) + [pl.BlockSpec((GRID_W, wd), lambda h, r: (r, ob + h))],
        out_specs=[pl.BlockSpec((GRID_W, wd), lambda h, r: (r, h)), pl.BlockSpec((t, wd), lambda h, r: (0, h)),
                   pl.BlockSpec((t, wd), lambda h, r: (0, h)),
                   pl.BlockSpec((hps, None, NA_ROWS, LANES), lambda h, r: (h, _na_geometry(r)[2], 0, 0))],
        out_shape=[jax.ShapeDtypeStruct((t, _na_w()), F32), jax.ShapeDtypeStruct((t, _na_w()), F32),
                   jax.ShapeDtypeStruct((t, _na_w()), F32),
                   jax.ShapeDtypeStruct((NA_HEADS, NA_ROWS, NA_ROWS, LANES), F32)],
        sem=("parallel", "arbitrary"), args=(p, p, p, rb, dmix), comm=comm)


def _rpb_select():
    sel = np.zeros((2 * NA_ROWS - 1, NA_ROWS * NA_ROWS), np.float32)
    for b in range(NA_ROWS):
        for kh in range(NA_ROWS):
            sel[b + kh, b * NA_ROWS + kh] = 1.0
    return jnp.asarray(sel)


def _rpb_rows(rpb):
    pad = jnp.pad(rpb, ((0, 0), (0, 0), (0, LANES - (2 * NA_COLS - 1))))
    rows = jnp.einsum("rk,hrc->hkc", _rpb_select(), pad, precision=lax.Precision.HIGHEST)
    return rows.reshape(NA_HEADS, NA_ROWS, NA_ROWS, LANES)


def _rpb_rows_t(drb):
    flat = drb.reshape(NA_HEADS, NA_ROWS * NA_ROWS, LANES)
    out = jnp.einsum("rk,hkc->hrc", _rpb_select(), flat, precision=lax.Precision.HIGHEST)
    return out[:, :, :2 * NA_COLS - 1]


def _assemble_dp(dqr, dkr, dvr, dgate, da, db, dnq, dnk, dnv, name):
    t = dgate.shape[0]
    tm = _tm()
    off = _offsets()
    sizes = dict(q=_ret_qk_w(), k=_ret_qk_w(), v=_ret_w(), g=_ret_w(), a=CONV_W, b=CONV_W, nq=_na_w(), nk=_na_w(), nv=_na_w())

    def body(q_ref, k_ref, v_ref, g_ref, a_ref, b_ref, nq_ref, nk_ref, nv_ref, o_ref):
        def put(n, val):
            o_ref[:, off[n]:off[n] + sizes[n]] = val.astype(BF16)

        put("q", q_ref[0] + q_ref[1])
        put("k", k_ref[0] + k_ref[1])
        put("v", v_ref[0] + v_ref[1])
        put("g", g_ref[...])
        put("a", a_ref[...])
        put("b", b_ref[...])
        put("nq", nq_ref[...])
        put("nk", nk_ref[...])
        put("nv", nv_ref[...])

    two = lambda w: pl.BlockSpec((2, tm, w), lambda i: (0, i, 0))
    one = lambda w: pl.BlockSpec((tm, w), lambda i: (i, 0))
    return pl.pallas_call(
        body, name=name, grid=(t // tm,),
        in_specs=[two(sizes["q"]), two(sizes["k"]), two(sizes["v"]), one(sizes["g"]), one(CONV_W), one(CONV_W),
                  one(_na_w()), one(_na_w()), one(_na_w())],
        out_specs=one(_d_in()), out_shape=jax.ShapeDtypeStruct((t, _d_in()), BF16),
        compiler_params=_params(("parallel",)))(dqr, dkr, dvr, dgate, da, db, dnq, dnk, dnv)


def _adamw(w, m, v, gs, name, comm=None):
    nl, r, c = w.shape
    stacked = not isinstance(gs, (list, tuple))
    if stacked:
        gs = [gs]
    assert stacked or len(gs) == nl
    g_n = gs[0].shape[-3]
    block_bytes = 2 * 1024 * 1024
    rows = min(block_bytes // (4 * c), block_bytes // (g_n * c * gs[0].dtype.itemsize))
    tr = _tile(r, max(2 * SUBLANES, rows // (2 * SUBLANES) * (2 * SUBLANES)), 2 * SUBLANES)
    nt = r // tr
    c1 = 1.0 - ADAM_B1 ** ADAM_STEP
    c2 = 1.0 - ADAM_B2 ** ADAM_STEP

    def body(w_ref, m_ref, v_ref, *rest):
        g_refs, (go_ref, d_ref, mo_ref, vo_ref) = rest[:len(gs)], rest[len(gs):]
        layer = pl.program_id(0)
        for ll in range(len(gs)):
            @pl.when(jnp.logical_or(stacked, layer == ll))
            def _():
                g_ref = g_refs[ll]
                g = g_ref[0].astype(F32)
                for j in range(1, g_n):
                    g = g + g_ref[j].astype(F32)
                mn = ADAM_B1 * m_ref[...] + (1.0 - ADAM_B1) * g
                vn = ADAM_B2 * v_ref[...] + (1.0 - ADAM_B2) * (g * g)
                m_hat = mn / c1
                v_hat = vn / c2
                go_ref[...] = g
                d_ref[...] = -ADAM_LR * (m_hat / (jnp.sqrt(v_hat) + ADAM_EPS) + ADAM_WD * w_ref[...])
                mo_ref[...] = mn
                vo_ref[...] = vn

    def g_spec(ll):
        if stacked:
            return pl.BlockSpec((None, g_n, tr, c), lambda l, i: (l, 0, i, 0))
        return pl.BlockSpec((g_n, tr, c), lambda l, i: (0, jnp.where(l == ll, i, jnp.where(l < ll, 0, nt - 1)), 0))

    blk = pl.BlockSpec((None, tr, c), lambda l, i: (l, i, 0))
    sds = jax.ShapeDtypeStruct((nl, r, c), F32)
    return _call(
        body, name=name, grid=(nl, nt),
        in_specs=[blk, blk, blk] + [g_spec(ll) for ll in range(len(gs))],
        out_specs=[blk, blk, blk, blk], out_shape=[sds, sds, sds, sds],
        sem=("arbitrary", "arbitrary"), args=(w, m, v, *gs), comm=comm)


def _sum_devices(g, name):
    _, r, c = g.shape
    tr = _tile(r, 512, SUBLANES)

    def body(g_ref, o_ref):
        acc = g_ref[0]
        for j in range(1, N_DEV):
            acc = acc + g_ref[j]
        o_ref[...] = acc

    return pl.pallas_call(body, name=name, grid=(r // tr,), in_specs=[pl.BlockSpec((N_DEV, tr, c), lambda i: (0, i, 0))],
                          out_specs=pl.BlockSpec((tr, c), lambda i: (i, 0)), out_shape=jax.ShapeDtypeStruct((r, c), F32),
                          compiler_params=_params(("parallel",)))(g)


def _ada_fwd(c16, w_ada, b_shard, name):
    nl, d, cs = w_ada.shape
    tk = _tile(d, 512, LANES)
    nk = d // tk

    def body(c_ref, w_ref, b_ref, o_ref):
        kk = pl.program_id(1)

        @pl.when(kk == 0)
        def _():
            o_ref[...] = jnp.broadcast_to(b_ref[...], o_ref.shape)

        o_ref[...] += _dg(_silu(c_ref[...]), w_ref[...], 1, 0)

    return pl.pallas_call(
        body, name=name, grid=(nl, nk),
        in_specs=[pl.BlockSpec((16, tk), lambda l, kk: (0, kk)), pl.BlockSpec((None, tk, cs), lambda l, kk: (l, kk, 0)),
                  pl.BlockSpec((None, 1, cs), lambda l, kk: (l, 0, 0))],
        out_specs=pl.BlockSpec((None, 16, cs), lambda l, kk: (l, 0, 0)),
        out_shape=jax.ShapeDtypeStruct((nl, 16, cs), F32),
        compiler_params=_params(("parallel", "arbitrary")))(c16, w_ada, b_shard)


def _ada_bwd(c16, dm16, w_ada, name):
    nl, d, cs = w_ada.shape
    td = _tile(d, 512, LANES)

    def body(c_ref, dm_ref, w_ref, gw_ref, dc_ref):
        cv = c_ref[...]
        s, vjp = jax.vjp(_silu, cv)
        gw_ref[...] = _dg(s, dm_ref[...], 0, 0)
        ds = _dg(dm_ref[...], w_ref[...], 1, 1)
        dc_ref[...] = vjp(ds)[0]

    return pl.pallas_call(
        body, name=name, grid=(nl, d // td),
        in_specs=[pl.BlockSpec((16, td), lambda l, i: (0, i)), pl.BlockSpec((None, 16, cs), lambda l, i: (l, 0, 0)),
                  pl.BlockSpec((None, td, cs), lambda l, i: (l, i, 0))],
        out_specs=[pl.BlockSpec((None, td, cs), lambda l, i: (l, i, 0)), pl.BlockSpec((None, 16, td), lambda l, i: (l, 0, i))],
        out_shape=[jax.ShapeDtypeStruct((nl, d, cs), F32), jax.ShapeDtypeStruct((nl, 16, d), F32)],
        compiler_params=_params(("parallel", "parallel")))(c16, dm16, w_ada)


def _pack_rows(shape):
    n = int(np.prod(shape))
    return SUBLANES * (-(-n // (LANES * SUBLANES)))


def _pack(arrays, row_align):
    parts = []
    for a in arrays:
        flat = a.reshape(-1).astype(F32)
        rows = _pack_rows(a.shape)
        parts.append(jnp.pad(flat, (0, rows * LANES - flat.shape[0])).reshape(rows, LANES))
    total = sum(p.shape[0] for p in parts)
    extra = -total % row_align
    if extra:
        parts.append(jnp.zeros((extra, LANES), F32))
    return jnp.concatenate(parts, axis=0)


def _unpack(packed, shapes):
    out, r = [], 0
    for s in shapes:
        rows = _pack_rows(s)
        out.append(packed[r:r + rows].reshape(-1)[:int(np.prod(s))].reshape(s))
        r += rows
    return out


def _rope_tables():
    half, nf = RET_DK // 2, RET_DK // 4
    pos = jnp.arange(SEQ)
    row = (pos // GRID_W).astype(F32)
    col = (pos % GRID_W).astype(F32)
    inv = ROPE_BASE ** (-jnp.arange(nf, dtype=F32) / nf)
    ar, ac = row[:, None] * inv[None, :], col[:, None] * inv[None, :]
    cos = jnp.concatenate([jnp.cos(ar), jnp.cos(ar), jnp.cos(ac), jnp.cos(ac)], axis=-1)
    sin = jnp.concatenate([-jnp.sin(ar), jnp.sin(ar), -jnp.sin(ac), jnp.sin(ac)], axis=-1)
    cos = jnp.concatenate([jnp.ones((CTX_LEN, RET_DK), F32), cos], axis=0)
    sin = jnp.concatenate([jnp.zeros((CTX_LEN, RET_DK), F32), sin], axis=0)
    return cos, sin


def _layer_fwd(l, x, mod4, w, cst, arrived):
    n = lambda s: f"l{l}_{s}"
    d = D_MODEL
    h1 = _normmod_fwd(x, w["norm1_g"], mod4, 0, n("norm1"))
    w["w_in"] = _cols_from_shards(arrived("w_in", h1), n("w_in_cols"))
    p, _ = _mm(h1, w["w_in"], n("proj_in"))
    o2, states = _ret_fwd(p, cst["cos"], cst["sin"], w["ret_decay"], cst["order"], n("ret_fwd"))
    ret_out = _ggn_fwd(o2, p, w["ret_gn_g"], n("ret_gn"))
    u2, conv_out = _conv_fwd(p, w["conv_dw_w"], w["conv_dw_b"], w["conv_ln_g"], w["conv_ln_b"], w["conv_pw"], n("conv_fwd"))
    na_out, _ = _na_fwd(p, w["rb"], n("na_fwd"))
    mix = jnp.concatenate([ret_out, conv_out, na_out], axis=1)
    w["w_out"] = arrived("w_out", mix).reshape(_d_mix(), d)
    g1, _ = _mm(mix, w["w_out"], n("proj_out"))
    x1 = _gate_res_fwd(x, g1, mod4, 2, n("res1"))
    h2 = _normmod_fwd(x1, w["norm2_g"], mod4, 1, n("norm2"))
    w["ffn_up"] = arrived("ffn_up", h2)
    u, _ = _mm(h2, w["ffn_up"], n("ffn_up"), b3=True)
    a = _ffn_act_fwd(u, w["ffn_dw_w"], w["ffn_dw_b"], n("ffn_act"))
    w["ffn_down"] = arrived("ffn_down", a).reshape(D_FF, d)
    f, _ = _mm(a, w["ffn_down"], n("ffn_down"))
    x2 = _gate_res_fwd(x1, f, mod4, 5, n("res2"))
    saved = dict(x=x, h1=h1, p=p, o2=o2, states=states, u2=u2, mix=mix, g1=g1, x1=x1, h2=h2, u=u, a=a, f=f)
    return x2, saved


def _layer_bwd(l, dx2, s, mod4, w, cst, send):
    n = lambda t: f"l{l}_{t}"
    d = D_MODEL
    nj = D_FF // _ffn_tc()
    dfg, dg2 = _gate_res_bwd(dx2, s["f"], mod4, 5, n("res2_bwd"))
    da, _ = _mm(dfg, w["ffn_down"], n("ffn_down_dx"), tb=True)
    d_ffn_down, _ = _mm(_transpose_bf16(s["a"], n("act_t")), dfg, n("ffn_down_dw"), out_dtype=BF16, tm_max=DW_TM)
    tok = send(("ffn_down", l), d_ffn_down.reshape(N_DEV, D_FF // N_DEV, d))
    dcv, dcg = _ffn_act_bwd1(s["u"], da, w["ffn_dw_w"], _after(w["ffn_dw_b"], tok), n("ffn_act_bwd"))
    duv, dwv, dbv = _dwconv_bwd(dcv, s["u"], w["ffn_dw_w"], 0, n("ffn_dw_bwd_val"))
    dug, dwg, dbg = _dwconv_bwd(dcg, s["u"], w["ffn_dw_w"], nj, n("ffn_dw_bwd_gate"))
    du = jnp.concatenate([duv, dug], axis=1)
    d_ffn_dw_w = jnp.concatenate([dwv, dwg], axis=1)
    d_ffn_dw_b = jnp.concatenate([dbv, dbg], axis=1)[0]
    dh2, _ = _mm(du, w["ffn_up"], n("ffn_up_dx"), tb=True, b3=True)
    d_ffn_up, _ = _mm(_transpose_bf16(s["h2"], n("h2_t")), du, n("ffn_up_dw"), out_dtype=BF16, tm_max=DW_TM,
                      o_cs=2 * D_FF // N_DEV)
    tok = send(("ffn_up", l), d_ffn_up)
    (dx1, dn2, dsh2, dsc2), _ = _normmod_bwd(s["x1"], _after(w["norm2_g"], tok), mod4, 1, dh2, dx2, n("norm2_bwd"))
    dgg, dg1 = _gate_res_bwd(dx1, s["g1"], mod4, 2, n("res1_bwd"))
    dmix, _ = _mm(dgg, w["w_out"], n("proj_out_dx"), tb=True)
    d_w_out, _ = _mm(_transpose_bf16(s["mix"], n("mix_t")), dgg, n("proj_out_dw"), out_dtype=BF16, tm_max=DW_TM)
    tok = send(("w_out", l), d_w_out.reshape(N_DEV, _d_mix() // N_DEV, d))
    do, dgate, dgn = _ggn_bwd(s["o2"], s["p"], _after(w["ret_gn_g"], tok), dmix, n("ret_gn_bwd"))
    dqr, dkr, dvr, ddec = _ret_bwd(s["p"], cst["cos"], cst["sin"], w["ret_decay"], cst["order"], s["states"], do, n("ret_bwd"))
    du2, dlng, dlnb, dpw = _conv_bwd1(s["u2"], dmix, w["conv_ln_g"], w["conv_ln_b"], w["conv_pw"], n("conv_bwd1"))
    (dca, dcb, ddww, ddwb), _ = _conv_bwd2(du2, s["p"], w["conv_dw_w"], n("conv_bwd2"))
    (dnq, dnk, dnv, drb), _ = _na_bwd(s["p"], w["rb"], dmix, n("na_bwd"))
    dp = _assemble_dp(dqr, dkr, dvr, dgate, dca, dcb, dnq, dnk, dnv, n("dproj"))
    h1_t = _transpose_bf16(s["h1"], n("h1_t"))
    half = d // 2
    for i in range(2):
        d_w_in, _ = _mm(h1_t[i * half:(i + 1) * half], dp, n(f"proj_in_dw{i}"), out_dtype=BF16, tm_max=DW_TM,
                        o_cs=_d_in() // N_DEV)
        tok = send(("w_in", l, i), d_w_in)
    dh1, _ = _mm(dp, w["w_in"], n("proj_in_dx"), tb=True)
    (dx, dn1, dsh1, dsc1), _ = _normmod_bwd(s["x"], _after(w["norm1_g"], tok), mod4, 0, dh1, dx1, n("norm1_bwd"))
    dmod = jnp.concatenate([dsh1, dsc1, dg1, dsh2, dsc2, dg2], axis=1)
    small = dict(norm1_g=dn1[0], ret_decay=ddec[:, :, 0, 0], ret_gn_g=dgn[0], conv_dw_w=ddww, conv_dw_b=ddwb[0],
                 conv_ln_g=dlng[0], conv_ln_b=dlnb[0], conv_pw=dpw, na_rpb=_rpb_rows_t(drb), norm2_g=dn2[0],
                 ffn_dw_w=d_ffn_dw_w, ffn_dw_b=d_ffn_dw_b)
    return dx, dmod, small


def _d_mix():
    return _ret_w() + CONV_W + _na_w()


_SMALL = ["c_ctx", "b_ada", "norm1_g", "ret_decay", "ret_gn_g", "conv_dw_w", "conv_dw_b", "conv_ln_g", "conv_ln_b",
          "conv_pw", "na_rpb", "norm2_g", "ffn_dw_w", "ffn_dw_b", "final_g"]
_SMALL_SHARD_AXIS = {"conv_dw_w": 2, "conv_pw": 1, "ffn_dw_w": 2}


def kernel(x, c, ctx, c_ctx, w_ada, b_ada, norm1_g, w_in, ret_decay, ret_gn_g, conv_dw_w, conv_dw_b, conv_ln_g, conv_ln_b, conv_pw, na_rpb, w_out, norm2_g, ffn_up, ffn_dw_w, ffn_dw_b, ffn_down, final_g, loss_target, m_c_ctx, m_w_ada, m_b_ada, m_norm1_g, m_w_in, m_ret_decay, m_ret_gn_g, m_conv_dw_w, m_conv_dw_b, m_conv_ln_g, m_conv_ln_b, m_conv_pw, m_na_rpb, m_w_out, m_norm2_g, m_ffn_up, m_ffn_dw_w, m_ffn_dw_b, m_ffn_down, m_final_g, v_c_ctx, v_w_ada, v_b_ada, v_norm1_g, v_w_in, v_ret_decay, v_ret_gn_g, v_conv_dw_w, v_conv_dw_b, v_conv_ln_g, v_conv_ln_b, v_conv_pw, v_na_rpb, v_w_out, v_norm2_g, v_ffn_up, v_ffn_dw_w, v_ffn_dw_b, v_ffn_down, v_final_g):
    d, nl = D_MODEL, DEPTH
    cs = 6 * d // N_DEV
    me = _my_index()
    weights = dict(c_ctx=c_ctx, w_ada=w_ada, b_ada=b_ada, norm1_g=norm1_g, w_in=w_in, ret_decay=ret_decay, ret_gn_g=ret_gn_g,
                   conv_dw_w=conv_dw_w, conv_dw_b=conv_dw_b, conv_ln_g=conv_ln_g, conv_ln_b=conv_ln_b, conv_pw=conv_pw,
                   na_rpb=na_rpb, w_out=w_out, norm2_g=norm2_g, ffn_up=ffn_up, ffn_dw_w=ffn_dw_w, ffn_dw_b=ffn_dw_b,
                   ffn_down=ffn_down, final_g=final_g)
    mom = dict(c_ctx=m_c_ctx, w_ada=m_w_ada, b_ada=m_b_ada, norm1_g=m_norm1_g, w_in=m_w_in, ret_decay=m_ret_decay,
               ret_gn_g=m_ret_gn_g, conv_dw_w=m_conv_dw_w, conv_dw_b=m_conv_dw_b, conv_ln_g=m_conv_ln_g,
               conv_ln_b=m_conv_ln_b, conv_pw=m_conv_pw, na_rpb=m_na_rpb, w_out=m_w_out, norm2_g=m_norm2_g,
               ffn_up=m_ffn_up, ffn_dw_w=m_ffn_dw_w, ffn_dw_b=m_ffn_dw_b, ffn_down=m_ffn_down, final_g=m_final_g)
    var = dict(c_ctx=v_c_ctx, w_ada=v_w_ada, b_ada=v_b_ada, norm1_g=v_norm1_g, w_in=v_w_in, ret_decay=v_ret_decay,
               ret_gn_g=v_ret_gn_g, conv_dw_w=v_conv_dw_w, conv_dw_b=v_conv_dw_b, conv_ln_g=v_conv_ln_g,
               conv_ln_b=v_conv_ln_b, conv_pw=v_conv_pw, na_rpb=v_na_rpb, w_out=v_w_out, norm2_g=v_norm2_g,
               ffn_up=v_ffn_up, ffn_dw_w=v_ffn_dw_w, ffn_dw_b=v_ffn_dw_b, ffn_down=v_ffn_down, final_g=v_final_g)

    big_names = ["w_in", "w_out", "ffn_up", "ffn_down"]
    shards = {(nm, l): _cast_bf16(weights[nm][l], f"cast_{nm}{l}") for l in range(nl) for nm in big_names}
    small_sharded = _pack([conv_dw_w, conv_pw, ffn_dw_w], SUBLANES)
    c_rows = jnp.pad(c, ((0, SUBLANES - 1), (0, 0)))
    gathered = _run_comm(_Gather([c_rows, small_sharded, shards[("w_in", 0)]]), "gather_first")
    c_all = gathered[0][:, 0, :]
    def whole(rows, shard_shape, axis):
        n_el = int(np.prod(shard_shape))
        parts = rows.reshape(N_DEV, -1)[:, :n_el].reshape((N_DEV,) + tuple(shard_shape))
        parts = jnp.moveaxis(parts, 0, axis)
        return parts.reshape(shard_shape[:axis] + (N_DEV * shard_shape[axis],) + shard_shape[axis + 1:])

    r0 = _pack_rows(conv_dw_w.shape)
    r1 = r0 + _pack_rows(conv_pw.shape)
    r2 = r1 + _pack_rows(ffn_dw_w.shape)
    full_conv_dw_w = whole(gathered[1][:, :r0], conv_dw_w.shape, 2)
    full_conv_pw = whole(gathered[1][:, r0:r1], conv_pw.shape, 1)
    full_ffn_dw_w = whole(gathered[1][:, r1:r2], ffn_dw_w.shape, 2)

    c16 = jnp.concatenate([c_all, jnp.broadcast_to(c_ctx[None, :], (N_DEV, d))], axis=0)
    b_shard = lax.dynamic_slice_in_dim(b_ada, me * cs, cs, axis=1)[:, None, :]
    m_shard = _ada_fwd(c16, w_ada, b_shard, "ada_fwd")
    m_all = _run_comm(_Gather([m_shard.reshape(nl * 16, cs)]), "gather_mod")[0]
    m_full = m_all.reshape(N_DEV, nl, 16, cs).transpose(1, 2, 0, 3).reshape(nl, 16, 6 * d)
    m_lat = lax.dynamic_index_in_dim(m_full, me, axis=1, keepdims=False)
    mod = jnp.stack([m_full[:, N_DEV], m_lat], axis=1).reshape(nl, 2, 6, 1, d)

    arriving, token = {}, m_all
    for l in range(nl):
        for nm in big_names:
            if (nm, l) != ("w_in", 0):
                arriving[(nm, l)], token = _split_start(shards[(nm, l)], True, f"gather_{nm}{l}", token)
    mod = _after(mod, token)

    cos, sin = _rope_tables()
    cst = dict(cos=cos, sin=sin, order=_chunk_order())
    layer_w = []
    for l in range(nl):
        layer_w.append(dict(
            norm1_g=norm1_g[l][None], norm2_g=norm2_g[l][None], ret_decay=ret_decay[l], ret_gn_g=ret_gn_g[l][None],
            conv_dw_w=full_conv_dw_w[l], conv_dw_b=conv_dw_b[l][None], conv_ln_g=conv_ln_g[l][None],
            conv_ln_b=conv_ln_b[l][None], conv_pw=full_conv_pw[l], rb=_rpb_rows(na_rpb[l]),
            ffn_dw_w=full_ffn_dw_w[l], ffn_dw_b=ffn_dw_b[l][None]))

    xs = jnp.concatenate([ctx[0], x[0]], axis=0)
    saved = []
    for l in range(nl):
        def arrived(nm, after, l=l):
            if (nm, l) == ("w_in", 0):
                return gathered[2]
            return _split_wait(arriving[(nm, l)], after, f"arrived_{nm}{l}")

        xs, sv = _layer_fwd(l, xs, mod[l], layer_w[l], cst, arrived)
        saved.append(sv)
    loss_tile, dxs, dfinal = _loss_head(xs, final_g[None], loss_target[0], "loss_head")
    loss = lax.psum(loss_tile[0, 0], ("x", "y", "c"))

    dmods, smalls = [None] * nl, [None] * nl
    leaving, last = {}, [loss_tile]

    def send(key, partial):
        leaving[key], token = _split_start(partial, False, "send_" + "_".join(str(k) for k in key), last[0])
        last[0] = token
        return token

    for l in reversed(range(nl)):
        dxs, dmods[l], smalls[l] = _layer_bwd(l, dxs, saved[l], mod[l], layer_w[l], cst, send)
    grad_x = dxs[CTX_LEN:][None]

    small_grads = dict(b_ada=jnp.sum(jnp.stack(dmods).reshape(nl, 2, 6 * d), axis=1), final_g=dfinal[0])
    early = [nm for nm in _SMALL if nm != "c_ctx"]
    for nm in early:
        if nm not in small_grads:
            small_grads[nm] = jnp.stack([smalls[l][nm] for l in range(nl)])
    packed = _pack([small_grads[nm] for nm in early], 512)
    small_arriving, token = _split_start(packed, True, "gather_small_grads", last[0])

    arrived_grad = lambda key, after: _split_wait(leaving[key], after, "got_" + "_".join(str(k) for k in key))
    out_big = {}
    after = token
    for nm in ["ffn_down", "ffn_up", "w_out"]:
        out_big[nm], _ = _adamw(weights[nm], mom[nm], var[nm], [arrived_grad((nm, l), after) for l in range(nl)],
                                f"adamw_{nm}")
        after = out_big[nm][0]

    dm_mine = jnp.stack(dmods).reshape(nl * 2, 6 * d)
    dm_rows = jnp.pad(dm_mine, ((0, SUBLANES - nl * 2), (0, 0)))
    dm_all = _run_comm(_Gather([dm_rows]), "gather_dmod", after=after)[0][:, :nl * 2].reshape(N_DEV, nl, 2, 6 * d)
    dm16_full = jnp.concatenate([dm_all[:, :, 1].transpose(1, 0, 2), dm_all[:, :, 0].transpose(1, 0, 2)], axis=1)
    dm16 = lax.dynamic_slice_in_dim(dm16_full, me * cs, cs, axis=2)
    g_w_ada, dc16 = _ada_bwd(c16, dm16, w_ada, "ada_bwd")
    out_big["w_ada"], _ = _adamw(w_ada, m_w_ada, v_w_ada, g_w_ada[:, None], "adamw_w_ada")
    halves = lambda a: a.reshape(2 * nl, d // 2, a.shape[2])
    after = out_big["w_ada"][0]
    res, _ = _adamw(halves(w_in), halves(m_w_in), halves(v_w_in),
                    [arrived_grad(("w_in", l, i), after) for l in range(nl) for i in range(2)], "adamw_w_in")
    out_big["w_in"] = [r.reshape(w_in.shape) for r in res]

    small_grads["c_ctx"] = jnp.sum(dc16[:, N_DEV:], axis=(0, 1))
    late = _run_comm(_Gather([_pack([small_grads["c_ctx"]], SUBLANES)]), "gather_c_ctx_grad", after=res[0])[0]
    g_small = dict(zip(["c_ctx"], _unpack(_sum_devices(late, "sum_c_ctx_grad"), [small_grads["c_ctx"].shape])))
    summed = _sum_devices(_split_wait(small_arriving, late, "arrived_small_grads"), "sum_small_grads")
    g_small.update(zip(early, _unpack(summed, [small_grads[nm].shape for nm in early])))
    for nm, ax in _SMALL_SHARD_AXIS.items():
        n_sh = weights[nm].shape[ax]
        g_small[nm] = lax.dynamic_slice_in_dim(g_small[nm], me * n_sh, n_sh, axis=ax)
    shapes_own = [weights[nm].shape for nm in _SMALL]
    pk = lambda src: _pack([src[nm] for nm in _SMALL], 2 * SUBLANES)[None]
    res_small, _ = _adamw(pk(weights), pk(mom), pk(var), pk(g_small)[:, None], "adamw_small")
    out_small = [dict(zip(_SMALL, _unpack(r[0], shapes_own))) for r in res_small]

    names = ["c_ctx", "w_ada", "b_ada", "norm1_g", "w_in", "ret_decay", "ret_gn_g", "conv_dw_w", "conv_dw_b", "conv_ln_g",
             "conv_ln_b", "conv_pw", "na_rpb", "w_out", "norm2_g", "ffn_up", "ffn_dw_w", "ffn_dw_b", "ffn_down", "final_g"]
    outs = [loss, grad_x]
    for kind in range(4):
        for nm in names:
            outs.append(out_big[nm][kind] if nm in out_big else out_small[kind][nm])
    return tuple(outs)
```

```python
import functools
import math

import numpy as np
import jax
import jax.numpy as jnp
from jax import lax
from jax.experimental import pallas as pl
from jax.experimental.pallas import tpu as pltpu

D_MODEL = 2048
SEQ = 4096
DEPTH = 2
GRID_W = 64
CTX_LEN = 256
RET_HEADS = 4
RET_DK = 128
RET_DV = 256
RET_CHUNK = 128
CONV_W = 512
CONV_K = 31
NA_HEADS = 4
NA_DH = 128
NA_ROWS = 8
NA_COLS = 16
D_FF = 5632
FFN_K = 3
ROPE_BASE = 10000.0
EPS = 1e-6
ADAM_LR = 0.001
ADAM_B1 = 0.9
ADAM_B2 = 0.999
ADAM_EPS = 1e-08
ADAM_WD = 0.01
ADAM_STEP = 10
N_DEV = 8

LANES = 128
SUBLANES = 8
VMEM_LIMIT = 56 * 1024 * 1024
ROW_CHUNK = 16

F32 = jnp.float32
BF16 = jnp.bfloat16
MESH = pl.DeviceIdType.MESH
NEG = -1e30


def _ret_qk_w():
    return RET_HEADS * RET_DK


def _ret_w():
    return RET_HEADS * RET_DV


def _na_w():
    return NA_HEADS * NA_DH


def _d_in():
    return 2 * _ret_qk_w() + 2 * _ret_w() + 2 * CONV_W + 3 * _na_w()


def _offsets():
    sizes = [_ret_qk_w(), _ret_qk_w(), _ret_w(), _ret_w(), CONV_W, CONV_W, _na_w(), _na_w(), _na_w()]
    offs = [0]
    for s in sizes[:-1]:
        offs.append(offs[-1] + s)
    return dict(zip(["q", "k", "v", "g", "a", "b", "nq", "nk", "nv"], offs))


def _t_rows():
    return CTX_LEN + SEQ


def _tm():
    return CTX_LEN


def _params(sem=None):
    kw = dict(vmem_limit_bytes=VMEM_LIMIT)
    if sem is not None:
        kw["dimension_semantics"] = sem
    return pltpu.CompilerParams(**kw)


def _tile(n, pref, align):
    best = None
    for t in range(align, min(n, pref) + 1, align):
        if n % t == 0:
            best = t
    return best if best is not None else n


def _dg(a, b, ca, cb):
    return lax.dot_general(a.astype(BF16), b.astype(BF16), (((ca,), (cb,)), ((), ())), preferred_element_type=F32)


@jax.custom_vjp
def dot_nn(a, b):
    return _dg(a, b, 1, 0)


dot_nn.defvjp(lambda a, b: (_dg(a, b, 1, 0), (a, b)),
              lambda r, g: (_dg(g, r[1], 1, 1), _dg(r[0], g, 0, 0)))


@jax.custom_vjp
def dot_nt(a, b):
    return _dg(a, b, 1, 1)


dot_nt.defvjp(lambda a, b: (_dg(a, b, 1, 1), (a, b)),
              lambda r, g: (_dg(g, r[1], 1, 0), _dg(g, r[0], 0, 0)))


@jax.custom_vjp
def dot_tn(a, b):
    return _dg(a, b, 0, 0)


dot_tn.defvjp(lambda a, b: (_dg(a, b, 0, 0), (a, b)),
              lambda r, g: (_dg(r[1], g, 1, 1), _dg(r[0], g, 1, 0)))


def _sigmoid(x):
    return 0.5 * jnp.tanh(0.5 * x) + 0.5


def _silu(x):
    return x * _sigmoid(x)


def _my_pos():
    return lax.axis_index("x"), lax.axis_index("y"), lax.axis_index("c")


def _my_index():
    x, y, c = _my_pos()
    return 4 * x + 2 * y + c


_ANY = pl.BlockSpec(memory_space=pl.ANY)


class _Gather:
    def __init__(self, arrays):
        self.arrays = list(arrays)
        n = len(self.arrays)
        self.out_shape = [jax.ShapeDtypeStruct((N_DEV,) + a.shape, a.dtype) for a in self.arrays]
        self.scratch = [pltpu.SemaphoreType.DMA((n, 7)), pltpu.SemaphoreType.DMA((n, 7)), pltpu.SemaphoreType.DMA((n,))]

    def _plan(self, xs, outs, sems):
        send_sems, recv_sems, local_sems = sems
        n = len(self.arrays)
        x, y, c = _my_pos()
        me, sibling = (x, y, c), (x, y, 1 - c)
        chips = [(1 - x, y), (x, 1 - y), (1 - x, 1 - y)]

        def slot(a, p):
            return outs[a].at[4 * p[0] + 2 * p[1] + p[2]]

        def copy(a, k, block, to, src=None):
            return pltpu.make_async_remote_copy(
                src_ref=slot(a, block) if src is None else src, dst_ref=slot(a, block),
                send_sem=send_sems.at[a, k], recv_sem=recv_sems.at[a, k], device_id=to, device_id_type=MESH)

        mine = [pltpu.make_async_copy(xs[a], slot(a, me), local_sems.at[a]) for a in range(n)]
        first = []
        for a in range(n):
            first.append(copy(a, 0, me, sibling, src=xs[a]))
            first += [copy(a, 1 + j, me, (*chip, c), src=xs[a]) for j, chip in enumerate(chips)]
        return n, c, me, sibling, chips, copy, mine, first

    def start(self, xs, outs, sems):
        _, _, _, _, _, _, mine, first = self._plan(xs, outs, sems)
        for m in mine:
            m.start()
        for cp in first:
            cp.start()

    def finish(self, xs, outs, sems):
        n, c, me, sibling, chips, copy, mine, first = self._plan(xs, outs, sems)
        passed = []
        for a in range(n):
            for j, chip in enumerate(chips):
                copy(a, 1 + j, (*chip, c), me).wait_recv()
                p = copy(a, 4 + j, (*chip, c), sibling)
                p.start()
                passed.append(p)
        for a in range(n):
            copy(a, 0, sibling, me).wait_recv()
            for j, chip in enumerate(chips):
                copy(a, 4 + j, (*chip, 1 - c), me).wait_recv()
        for cp in first + passed:
            cp.wait_send()
        for m in mine:
            m.wait()


def _run_comm(comm, name, after=None):
    n = len(comm.arrays)
    extra = [] if after is None else [after]

    def body(*refs):
        xs, outs, sems = refs[:n], refs[n + len(extra):2 * n + len(extra)], refs[2 * n + len(extra):]
        comm.start(xs, outs, sems)
        comm.finish(xs, outs, sems)

    return pl.pallas_call(body, name=name, out_shape=comm.out_shape, in_specs=[_ANY] * (n + len(extra)),
                          out_specs=[_ANY] * n, scratch_shapes=comm.scratch)(*comm.arrays, *extra)


_HBM = pl.BlockSpec(memory_space=pltpu.HBM)
_SEMS = pl.BlockSpec(memory_space=pltpu.SEMAPHORE)
_EFFECT = pltpu.SideEffectType.DATAFLOW_SIDE_EFFECTING


def _own_slot(x, gathering, name):
    shape = (N_DEV,) + x.shape if gathering else x.shape
    r, c = shape[1], shape[2]
    tr = _tile(r, 256, 2 * SUBLANES)
    me = jnp.reshape(_my_index(), (1,)).astype(jnp.int32)

    def body(me_ref, x_ref, o_ref):
        o_ref[...] = x_ref[...]

    src = (pl.BlockSpec((tr, c), lambda i, m: (i, 0)) if gathering
           else pl.BlockSpec((None, tr, c), lambda i, m: (m[0], i, 0)))
    grid_spec = pltpu.PrefetchScalarGridSpec(
        num_scalar_prefetch=1, grid=(r // tr,), in_specs=[src],
        out_specs=pl.BlockSpec((None, tr, c), lambda i, m: (m[0], i, 0)))
    return pl.pallas_call(body, name=name, grid_spec=grid_spec, out_shape=jax.ShapeDtypeStruct(shape, x.dtype),
                          compiler_params=_params(("arbitrary",)))(me, x)


def _split_plan(x_ref, land_ref, send_sems, recv_sems, gathering):
    x, y, c = _my_pos()
    me = 4 * x + 2 * y + c
    sends, recvs = [], []
    for k in range(1, N_DEV):
        px = 1 - x if (k >> 2) & 1 else x
        py = 1 - y if (k >> 1) & 1 else y
        pc = 1 - c if k & 1 else c
        peer = 4 * px + 2 * py + pc
        mine, theirs = (x_ref, x_ref) if gathering else (x_ref.at[peer], x_ref.at[me])
        sends.append(pltpu.make_async_remote_copy(
            src_ref=mine, dst_ref=land_ref.at[me], send_sem=send_sems.at[k - 1], recv_sem=recv_sems.at[k - 1],
            device_id=(px, py, pc), device_id_type=MESH))
        recvs.append(pltpu.make_async_remote_copy(
            src_ref=theirs, dst_ref=land_ref.at[peer], send_sem=send_sems.at[k - 1], recv_sem=recv_sems.at[k - 1],
            device_id=(px, py, pc), device_id_type=MESH))
    return sends, recvs


def _split_start(x, gathering, name, prev):
    land = _own_slot(x, gathering, name + "_own")

    def body(x_ref, land_ref, prev_ref, send_sems, recv_sems, x_thru, land_thru, token):
        sends, _ = _split_plan(x_ref, land_ref, send_sems, recv_sems, gathering)
        for s in sends:
            s.start()
        token[...] = jnp.zeros_like(token)

    sems = pltpu.SemaphoreType.DMA((N_DEV - 1,))
    send_sems, recv_sems, x_thru, land_thru, token = pl.pallas_call(
        body, name=name,
        out_shape=(sems, sems, pltpu.HBM(x.shape, x.dtype), pltpu.HBM(land.shape, land.dtype),
                   jax.ShapeDtypeStruct((SUBLANES, LANES), F32)),
        in_specs=(_HBM, _HBM, _ANY), out_specs=(_SEMS, _SEMS, _HBM, _HBM, pl.BlockSpec(memory_space=pltpu.VMEM)),
        input_output_aliases={0: 2, 1: 3},
        compiler_params=pltpu.CompilerParams(has_side_effects=_EFFECT),
    )(pltpu.with_memory_space_constraint(x, pltpu.HBM), pltpu.with_memory_space_constraint(land, pltpu.HBM), prev)
    return (send_sems, recv_sems, x_thru, land_thru, gathering), token


def _after(a, token):
    return a + token[0, 0].astype(a.dtype)


def _split_wait(handle, after, name):
    send_sems, recv_sems, x_thru, land_thru, gathering = handle

    def body(x_ref, land_ref, send_sems, recv_sems, after_ref, x_dead, got_ref):
        sends, recvs = _split_plan(x_ref, land_ref, send_sems, recv_sems, gathering)
        for s in sends:
            s.wait_send()
        for r in recvs:
            r.wait_recv()

    return pl.pallas_call(
        body, name=name, out_shape=(pltpu.HBM(x_thru.shape, x_thru.dtype), pltpu.HBM(land_thru.shape, land_thru.dtype)),
        in_specs=(_HBM, _HBM, _SEMS, _SEMS, _ANY), out_specs=(_HBM, _HBM), input_output_aliases={0: 0, 1: 1},
        compiler_params=pltpu.CompilerParams(has_side_effects=_EFFECT),
    )(x_thru, land_thru, send_sems, recv_sems, after)[1]


def _call(body, *, name, grid, in_specs, out_specs, out_shape, args, scratch=(), sem=None, comm=None, after=None):
    assert comm is None
    if after is None:
        res = pl.pallas_call(body, name=name, grid=grid, in_specs=list(in_specs), out_specs=list(out_specs),
                             out_shape=list(out_shape), scratch_shapes=list(scratch), compiler_params=_params(sem))(*args)
        return list(res), None
    n_in = len(in_specs)

    def wrapped(*refs):
        body(*refs[:n_in], *refs[n_in + 1:])

    res = pl.pallas_call(wrapped, name=name, grid=grid, in_specs=list(in_specs) + [_ANY], out_specs=list(out_specs),
                         out_shape=list(out_shape), scratch_shapes=list(scratch), compiler_params=_params(sem))(*args, after)
    return list(res), None


MM_B_BLOCK_BYTES = 6 * 1024 * 1024
MM_O_BLOCK_BYTES = 13 * 1024 * 1024 // 2


def _mm(a, b, name, tb=False, out_dtype=F32, b3=False, o_cs=None, tm_max=1088, comm=None, after=None):
    m, k = a.shape
    if b3:
        cs = b.shape[2]
        n, kb = (b.shape[1], N_DEV * cs) if tb else (N_DEV * cs, b.shape[1])
    else:
        n, kb = (b.shape[0], b.shape[1]) if tb else (b.shape[1], b.shape[0])
    assert k == kb, (a.shape, b.shape, tb)
    if b3 and tb:
        tm, tn = _tile(m, 544, 2 * SUBLANES), _tile(n, 256, LANES)

        def body_shards(a_ref, b_ref, o_ref):
            r = None
            for j in range(N_DEV):
                part = lax.dot_general(a_ref[:, j * cs:(j + 1) * cs], b_ref[j], (((1,), (1,)), ((), ())),
                                       preferred_element_type=F32)
                r = part if r is None else r + part
            o_ref[...] = r.astype(o_ref.dtype)

        res, cres = _call(
            body_shards, name=name, grid=(m // tm, n // tn),
            in_specs=[pl.BlockSpec((tm, k), lambda i, j: (i, 0)), pl.BlockSpec((N_DEV, tn, cs), lambda i, j: (0, j, 0))],
            out_specs=[pl.BlockSpec((tm, tn), lambda i, j: (i, j))], out_shape=[jax.ShapeDtypeStruct((m, n), out_dtype)],
            sem=("parallel", "parallel"), args=(a, b), comm=comm, after=after)
        return res[0], cres
    tm = _tile(m, tm_max, 2 * SUBLANES)
    tk = k
    if b3:
        tn = cs
    elif o_cs is not None:
        tn = o_cs if o_cs % LANES == 0 else 2 * o_cs
    else:
        tn = _tile(n, min(MM_B_BLOCK_BYTES // (2 * tk), MM_O_BLOCK_BYTES // (4 * tm)), LANES)
    cb = 1 if tb else 0
    dn = (((1,), (cb,)), ((), ()))

    def body_one(a_ref, b_ref, o_ref):
        r = lax.dot_general(a_ref[...], b_ref[...], dn, preferred_element_type=F32)
        if o_cs is None:
            o_ref[...] = r.astype(o_ref.dtype)
        else:
            for j in range(tn // o_cs):
                o_ref[j] = r[:, j * o_cs:(j + 1) * o_cs].astype(o_ref.dtype)

    a_spec = pl.BlockSpec((tm, tk), lambda i, j: (i, 0))
    if b3:
        b_spec = pl.BlockSpec((None, tk, cs), lambda i, j: (j, 0, 0))
    else:
        b_spec = pl.BlockSpec((tn, tk), lambda i, j: (j, 0)) if tb else pl.BlockSpec((tk, tn), lambda i, j: (0, j))
    if o_cs is None:
        o_spec = pl.BlockSpec((tm, tn), lambda i, j: (i, j))
        o_shape = jax.ShapeDtypeStruct((m, n), out_dtype)
    else:
        o_spec = pl.BlockSpec((tn // o_cs, tm, o_cs), lambda i, j: (j, i, 0))
        o_shape = jax.ShapeDtypeStruct((n // o_cs, m, o_cs), out_dtype)
    res, cres = _call(
        body_one, name=name, grid=(m // tm, n // tn), in_specs=[a_spec, b_spec], out_specs=[o_spec], out_shape=[o_shape],
        sem=("parallel", "parallel"), args=(a, b), comm=comm, after=after)
    return res[0], cres


DW_TM = 512


def _transpose_bf16(x, name):
    t, c = x.shape
    tt = _tm()

    def body(x_ref, o_ref):
        o_ref[...] = x_ref[...].T

    return pl.pallas_call(body, name=name, grid=(t // tt,), in_specs=[pl.BlockSpec((tt, c), lambda i: (i, 0))],
                          out_specs=pl.BlockSpec((c, tt), lambda i: (0, i)),
                          out_shape=jax.ShapeDtypeStruct((c, t), BF16), compiler_params=_params(("parallel",)))(x)


def _cast_bf16(x, name):
    r, c = x.shape
    tr = _tile(r, 512, 2 * SUBLANES)

    def body(x_ref, o_ref):
        o_ref[...] = x_ref[...].astype(BF16)

    return pl.pallas_call(body, name=name, grid=(r // tr,), in_specs=[pl.BlockSpec((tr, c), lambda i: (i, 0))],
                          out_specs=pl.BlockSpec((tr, c), lambda i: (i, 0)),
                          out_shape=jax.ShapeDtypeStruct((r, c), BF16), compiler_params=_params(("parallel",)))(x)


def _cols_from_shards(wg, name):
    _, k, cs = wg.shape
    tk = _tile(k, 256, 2 * SUBLANES)

    def body(w_ref, o_ref):
        for j in range(N_DEV):
            o_ref[:, j * cs:(j + 1) * cs] = w_ref[j]

    return pl.pallas_call(body, name=name, grid=(k // tk,),
                          in_specs=[pl.BlockSpec((N_DEV, tk, cs), lambda i: (0, i, 0))],
                          out_specs=pl.BlockSpec((tk, N_DEV * cs), lambda i: (i, 0)),
                          out_shape=jax.ShapeDtypeStruct((k, N_DEV * cs), wg.dtype),
                          compiler_params=_params(("parallel",)))(wg)


def _stream(i):
    return jnp.minimum(i, 1)


def _normmod(x, g, sh, sc):
    y = x * lax.rsqrt(jnp.mean(x * x, axis=-1, keepdims=True) + EPS)
    return (y * g) * (1.0 + sc) + sh


def _mod_spec(chunk, d):
    return pl.BlockSpec((None, None, 1, d), lambda i: (_stream(i), chunk, 0, 0))


def _normmod_fwd(x, g, mod4, which, name):
    t, d = x.shape
    tm = _tm()
    ish, isc = (0, 1) if which == 0 else (3, 4)

    def body(x_ref, g_ref, sh_ref, sc_ref, o_ref):
        o_ref[...] = _normmod(x_ref[...], g_ref[...], sh_ref[...], sc_ref[...]).astype(BF16)

    row = pl.BlockSpec((tm, d), lambda i: (i, 0))
    return pl.pallas_call(body, name=name, grid=(t // tm,),
                          in_specs=[row, pl.BlockSpec((1, d), lambda i: (0, 0)), _mod_spec(ish, d), _mod_spec(isc, d)],
                          out_specs=row, out_shape=jax.ShapeDtypeStruct((t, d), BF16),
                          compiler_params=_params(("parallel",)))(x, g, mod4, mod4)


def _normmod_bwd(x, g, mod4, which, dh, dres, name, comm=None):
    t, d = x.shape
    tm = _tm()
    ish, isc = (0, 1) if which == 0 else (3, 4)

    def body(x_ref, g_ref, sh_ref, sc_ref, dh_ref, dres_ref, dx_ref, dg_ref, dsh_ref, dsc_ref):
        i = pl.program_id(0)
        _, vjp = jax.vjp(_normmod, x_ref[...], g_ref[...], sh_ref[...], sc_ref[...])
        dx, dg, dsh, dsc = vjp(dh_ref[...])
        dx_ref[...] = dres_ref[...] + dx

        @pl.when(i == 0)
        def _():
            dg_ref[...] = jnp.zeros_like(dg_ref)

        @pl.when(i <= 1)
        def _():
            dsh_ref[...] = jnp.zeros_like(dsh_ref)
            dsc_ref[...] = jnp.zeros_like(dsc_ref)

        dg_ref[...] += dg
        dsh_ref[...] += dsh
        dsc_ref[...] += dsc

    row = pl.BlockSpec((tm, d), lambda i: (i, 0))
    vec = pl.BlockSpec((1, d), lambda i: (0, 0))
    svec = pl.BlockSpec((None, 1, d), lambda i: (_stream(i), 0, 0))
    return _call(
        body, name=name, grid=(t // tm,),
        in_specs=[row, vec, _mod_spec(ish, d), _mod_spec(isc, d), row, row],
        out_specs=[row, vec, svec, svec],
        out_shape=[jax.ShapeDtypeStruct((t, d), F32), jax.ShapeDtypeStruct((1, d), F32),
                   jax.ShapeDtypeStruct((2, 1, d), F32), jax.ShapeDtypeStruct((2, 1, d), F32)],
        sem=("arbitrary",), args=(x, g, mod4, mod4, dh, dres), comm=comm)


def _gate_res_fwd(x, f, mod4, chunk, name):
    t, d = x.shape
    tm = _tm()

    def body(x_ref, f_ref, g_ref, o_ref):
        o_ref[...] = x_ref[...] + g_ref[...] * f_ref[...]

    row = pl.BlockSpec((tm, d), lambda i: (i, 0))
    return pl.pallas_call(body, name=name, grid=(t // tm,), in_specs=[row, row, _mod_spec(chunk, d)], out_specs=row,
                          out_shape=jax.ShapeDtypeStruct((t, d), F32), compiler_params=_params(("parallel",)))(x, f, mod4)


def _gate_res_bwd(dx, f, mod4, chunk, name):
    t, d = dx.shape
    tm = _tm()

    def body(dx_ref, f_ref, g_ref, o_ref, dg_ref):
        i = pl.program_id(0)
        dxv = dx_ref[...]
        o_ref[...] = (dxv * g_ref[...]).astype(BF16)

        @pl.when(i <= 1)
        def _():
            dg_ref[...] = jnp.zeros_like(dg_ref)

        dg_ref[...] += jnp.sum(dxv * f_ref[...], axis=0, keepdims=True)

    row = pl.BlockSpec((tm, d), lambda i: (i, 0))
    return pl.pallas_call(
        body, name=name, grid=(t // tm,), in_specs=[row, row, _mod_spec(chunk, d)],
        out_specs=[row, pl.BlockSpec((None, 1, d), lambda i: (_stream(i), 0, 0))],
        out_shape=[jax.ShapeDtypeStruct((t, d), BF16), jax.ShapeDtypeStruct((2, 1, d), F32)],
        compiler_params=_params(("arbitrary",)))(dx, f, mod4)


def _loss_head(x, final_g, target, name):
    t, d = x.shape
    tm = _tm()

    def loss_fn(xv, g, tgt):
        y = (xv * lax.rsqrt(jnp.mean(xv * xv, axis=-1, keepdims=True) + EPS)) * g
        err = y - tgt
        return 0.5 * jnp.sum(jnp.mean(err * err, axis=-1, keepdims=True))

    def body(x_ref, g_ref, t_ref, l_ref, dx_ref, dg_ref):
        i = pl.program_id(0)

        @pl.when(i == 0)
        def _():
            l_ref[...] = jnp.zeros_like(l_ref)
            dg_ref[...] = jnp.zeros_like(dg_ref)
            dx_ref[...] = jnp.zeros_like(dx_ref)

        @pl.when(i > 0)
        def _():
            l, (dx, dg) = jax.value_and_grad(loss_fn, argnums=(0, 1))(x_ref[...], g_ref[...], t_ref[...])
            l_ref[...] += jnp.full(l_ref.shape, l, F32)
            dx_ref[...] = dx
            dg_ref[...] += dg

    row = pl.BlockSpec((tm, d), lambda i: (i, 0))
    vec = pl.BlockSpec((1, d), lambda i: (0, 0))
    return pl.pallas_call(
        body, name=name, grid=(t // tm,),
        in_specs=[row, vec, pl.BlockSpec((tm, d), lambda i: (jnp.maximum(i - 1, 0), 0))],
        out_specs=[pl.BlockSpec((SUBLANES, LANES), lambda i: (0, 0)), row, vec],
        out_shape=[jax.ShapeDtypeStruct((SUBLANES, LANES), F32), jax.ShapeDtypeStruct((t, d), F32),
                   jax.ShapeDtypeStruct((1, d), F32)],
        compiler_params=_params(("arbitrary",)))(x, final_g, target)


def _swap_quarters(x):
    half, nf = RET_DK // 2, RET_DK // 4
    lane = lax.broadcasted_iota(jnp.int32, x.shape, 1)
    return jnp.where((lane % half) < nf, pltpu.roll(x, RET_DK - nf, 1), pltpu.roll(x, nf, 1))


def _rope(x, cos, sin):
    return x * cos + _swap_quarters(x) * sin


def _rope_t(y, cos, sin):
    return y * cos + _swap_quarters(y * sin)


def _ret_consts(d):
    c = RET_CHUNK
    ii = lax.broadcasted_iota(jnp.int32, (c, 1), 0).astype(F32)
    jj = lax.broadcasted_iota(jnp.int32, (1, c), 1).astype(F32)
    fwd = d == 0
    sgn = jnp.where(fwd, 1.0, -1.0).astype(F32)
    pos = jnp.where(fwd, ii, c - 1.0 - ii)
    return sgn * (ii - jj), pos


def _ret_step(lgt, state, q, k, v, diff, pos):
    c = float(RET_CHUNK)
    lg = -(jnp.maximum(-lgt, 0.0) + jnp.log1p(jnp.exp(-jnp.abs(lgt))))
    lower = diff >= 0
    decay = jnp.where(lower, jnp.exp(jnp.where(lower, diff, 0.0) * lg), 0.0)
    xi = jnp.exp((pos + 1.0) * lg)
    zeta = jnp.exp((c - 1.0 - pos) * lg)
    gch = jnp.exp(c * lg)
    inner = dot_nt(q, k) * decay
    out = dot_nn(inner, v) + dot_nn(q, state) * xi
    new_state = state * gch + dot_tn(k * zeta, v)
    return out, new_state


def _chunk_order():
    nc, nch = CTX_LEN // RET_CHUNK, _t_rows() // RET_CHUNK
    fwd = list(range(nch))
    bwd = list(range(nc - 1, -1, -1)) + list(range(nch - 1, nc - 1, -1))
    return jnp.asarray(np.array([fwd, bwd], np.int32))


def _ret_fwd(p, cos, sin, decay, order, name):
    t = p.shape[0]
    c, dk, dv, nh = RET_CHUNK, RET_DK, RET_DV, RET_HEADS
    nch = t // c
    off = _offsets()
    wqk, wv = nh * dk, nh * dv
    assert off["q"] % wqk == 0 and off["k"] % wqk == 0 and off["v"] % wv == 0
    qb, kb, vb = off["q"] // wqk, off["k"] // wqk, off["v"] // wv
    scale = RET_DK ** -0.5

    def body(ord_ref, dec_ref, q_ref, k_ref, v_ref, cos_ref, sin_ref, o_ref, st_ref, state):
        d, s = pl.program_id(0), pl.program_id(1)

        @pl.when(s == 0)
        def _():
            state[...] = jnp.zeros_like(state)

        diff, pos = _ret_consts(d)
        cosv, sinv = cos_ref[...], sin_ref[...]
        for h in range(nh):
            st = state[h]
            st_ref[h] = st
            lgt = jnp.full((1, 1), dec_ref[d, h], F32)
            q = _rope(q_ref[:, h * dk:(h + 1) * dk], cosv, sinv) * scale
            k = _rope(k_ref[:, h * dk:(h + 1) * dk], cosv, sinv)
            out, ns = _ret_step(lgt, st, q, k, v_ref[:, h * dv:(h + 1) * dv], diff, pos)
            o_ref[:, h * dv:(h + 1) * dv] = out
            state[h] = ns

    grid_spec = pltpu.PrefetchScalarGridSpec(
        num_scalar_prefetch=1, grid=(2, nch),
        in_specs=[pl.BlockSpec(memory_space=pltpu.SMEM),
                  pl.BlockSpec((c, wqk), lambda d, s, o: (o[d, s], qb)),
                  pl.BlockSpec((c, wqk), lambda d, s, o: (o[d, s], kb)),
                  pl.BlockSpec((c, wv), lambda d, s, o: (o[d, s], vb)),
                  pl.BlockSpec((c, dk), lambda d, s, o: (o[d, s], 0)),
                  pl.BlockSpec((c, dk), lambda d, s, o: (o[d, s], 0))],
        out_specs=[pl.BlockSpec((None, c, wv), lambda d, s, o: (d, o[d, s], 0)),
                   pl.BlockSpec((None, nh, None, dk, dv), lambda d, s, o: (d, 0, s, 0, 0))],
        scratch_shapes=[pltpu.VMEM((nh, dk, dv), F32)])
    return pl.pallas_call(
        body, name=name, grid_spec=grid_spec,
        out_shape=[jax.ShapeDtypeStruct((2, t, wv), F32), jax.ShapeDtypeStruct((2, nh, nch, dk, dv), F32)],
        compiler_params=_params(("arbitrary", "arbitrary")))(order, decay, p, p, p, cos, sin)


def _ret_bwd(p, cos, sin, decay, order, states, do, name):
    t = p.shape[0]
    c, dk, dv, nh = RET_CHUNK, RET_DK, RET_DV, RET_HEADS
    nch = t // c
    off = _offsets()
    wqk, wv = nh * dk, nh * dv
    qb, kb, vb = off["q"] // wqk, off["k"] // wqk, off["v"] // wv
    scale = RET_DK ** -0.5

    def body(ord_ref, dec_ref, q_ref, k_ref, v_ref, cos_ref, sin_ref, st_ref, do_ref,
             dq_ref, dk_ref, dv_ref, dd_ref, dstate):
        d, s = pl.program_id(0), pl.program_id(1)

        @pl.when(s == 0)
        def _():
            dstate[...] = jnp.zeros_like(dstate)
            dd_ref[...] = jnp.zeros_like(dd_ref)

        diff, pos = _ret_consts(d)
        cosv, sinv = cos_ref[...], sin_ref[...]
        for h in range(nh):
            qk, vv = slice(h * dk, (h + 1) * dk), slice(h * dv, (h + 1) * dv)
            lgt = jnp.full((1, 1), dec_ref[d, h], F32)
            q = _rope(q_ref[:, qk], cosv, sinv) * scale
            k = _rope(k_ref[:, qk], cosv, sinv)
            _, vjp = jax.vjp(lambda a, b, cq, ck, cv: _ret_step(a, b, cq, ck, cv, diff, pos),
                             lgt, st_ref[h], q, k, v_ref[:, vv])
            dlgt, dst, dq, dkk, dvv = vjp((do_ref[:, vv], dstate[h]))
            dstate[h] = dst
            dq_ref[:, qk] = _rope_t(dq * scale, cosv, sinv)
            dk_ref[:, qk] = _rope_t(dkk, cosv, sinv)
            dv_ref[:, vv] = dvv
            dd_ref[h] += jnp.broadcast_to(dlgt, (SUBLANES, LANES))

    rev = lambda o, d, s: o[d, nch - 1 - s]
    grid_spec = pltpu.PrefetchScalarGridSpec(
        num_scalar_prefetch=1, grid=(2, nch),
        in_specs=[pl.BlockSpec(memory_space=pltpu.SMEM),
                  pl.BlockSpec((c, wqk), lambda d, s, o: (rev(o, d, s), qb)),
                  pl.BlockSpec((c, wqk), lambda d, s, o: (rev(o, d, s), kb)),
                  pl.BlockSpec((c, wv), lambda d, s, o: (rev(o, d, s), vb)),
                  pl.BlockSpec((c, dk), lambda d, s, o: (rev(o, d, s), 0)),
                  pl.BlockSpec((c, dk), lambda d, s, o: (rev(o, d, s), 0)),
                  pl.BlockSpec((None, nh, None, dk, dv), lambda d, s, o: (d, 0, nch - 1 - s, 0, 0)),
                  pl.BlockSpec((c, wv), lambda d, s, o: (rev(o, d, s), 0))],
        out_specs=[pl.BlockSpec((None, c, wqk), lambda d, s, o: (d, rev(o, d, s), 0)),
                   pl.BlockSpec((None, c, wqk), lambda d, s, o: (d, rev(o, d, s), 0)),
                   pl.BlockSpec((None, c, wv), lambda d, s, o: (d, rev(o, d, s), 0)),
                   pl.BlockSpec((None, nh, SUBLANES, LANES), lambda d, s, o: (d, 0, 0, 0))],
        scratch_shapes=[pltpu.VMEM((nh, dk, dv), F32)])
    return pl.pallas_call(
        body, name=name, grid_spec=grid_spec,
        out_shape=[jax.ShapeDtypeStruct((2, t, wqk), F32), jax.ShapeDtypeStruct((2, t, wqk), F32),
                   jax.ShapeDtypeStruct((2, t, wv), F32), jax.ShapeDtypeStruct((2, nh, SUBLANES, LANES), F32)],
        compiler_params=_params(("arbitrary", "arbitrary")))(order, decay, p, p, p, cos, sin, states, do)


def _ggn_head(of, ob, gate, g):
    o = of + ob
    mu = jnp.mean(o, axis=-1, keepdims=True)
    var = jnp.mean(jnp.square(o - mu), axis=-1, keepdims=True)
    return ((o - mu) * lax.rsqrt(var + EPS) * g) * _silu(gate)


def _ggn_fwd(o2, p, gn_g, name):
    t = p.shape[0]
    tm, w, dv = _tm(), _ret_w(), RET_DV
    gb = _offsets()["g"] // w

    def body(o_ref, gate_ref, g_ref, out_ref):
        for h in range(RET_HEADS):
            sl = slice(h * dv, (h + 1) * dv)
            out_ref[:, sl] = _ggn_head(o_ref[0, :, sl], o_ref[1, :, sl], gate_ref[:, sl], g_ref[:, sl]).astype(BF16)

    return pl.pallas_call(
        body, name=name, grid=(t // tm,),
        in_specs=[pl.BlockSpec((2, tm, w), lambda i: (0, i, 0)), pl.BlockSpec((tm, w), lambda i: (i, gb)),
                  pl.BlockSpec((1, w), lambda i: (0, 0))],
        out_specs=pl.BlockSpec((tm, w), lambda i: (i, 0)), out_shape=jax.ShapeDtypeStruct((t, w), BF16),
        compiler_params=_params(("parallel",)))(o2, p, gn_g)


def _ggn_bwd(o2, p, gn_g, dmix, name):
    t = p.shape[0]
    tm, w, dv = _tm(), _ret_w(), RET_DV
    gb = _offsets()["g"] // w

    def body(o_ref, gate_ref, g_ref, dy_ref, do_ref, dgate_ref, dg_ref):
        i = pl.program_id(0)

        @pl.when(i == 0)
        def _():
            dg_ref[...] = jnp.zeros_like(dg_ref)

        for h in range(RET_HEADS):
            sl = slice(h * dv, (h + 1) * dv)
            _, vjp = jax.vjp(_ggn_head, o_ref[0, :, sl], o_ref[1, :, sl], gate_ref[:, sl], g_ref[:, sl])
            do, _, dgate, dg = vjp(dy_ref[:, sl])
            do_ref[:, sl] = do
            dgate_ref[:, sl] = dgate
            dg_ref[:, sl] += dg

    row = pl.BlockSpec((tm, w), lambda i: (i, 0))
    return pl.pallas_call(
        body, name=name, grid=(t // tm,),
        in_specs=[pl.BlockSpec((2, tm, w), lambda i: (0, i, 0)), pl.BlockSpec((tm, w), lambda i: (i, gb)),
                  pl.BlockSpec((1, w), lambda i: (0, 0)), row],
        out_specs=[row, row, pl.BlockSpec((1, w), lambda i: (0, 0))],
        out_shape=[jax.ShapeDtypeStruct((t, w), F32), jax.ShapeDtypeStruct((t, w), F32),
                   jax.ShapeDtypeStruct((1, w), F32)],
        compiler_params=_params(("arbitrary",)))(o2, p, gn_g, dmix)


def _halo(k):
    return SUBLANES * ((k // 2 + SUBLANES - 1) // SUBLANES)


def _halo_specs(width, colblock, h, tm):
    r = tm // h
    return [pl.BlockSpec((h, width), lambda i, *_: (jnp.maximum(i * r - 1, 0), colblock(*_))),
            pl.BlockSpec((tm, width), lambda i, *_: (i, colblock(*_))),
            pl.BlockSpec((h, width), lambda i, *_: (jnp.minimum((i + 1) * r, (_t_rows() // h) - 1), colblock(*_)))]


def _fill_ext(ext_ref, prev, cur, nxt, i, h, tm):
    nt = _t_rows() // tm
    ext_ref[0:h, :] = jnp.where(i >= 2, prev, 0.0)
    ext_ref[h:h + tm, :] = cur
    ext_ref[h + tm:h + tm + h, :] = jnp.where((i >= 1) & (i <= nt - 2), nxt, 0.0)


def _corr(ext_ref, w_ref, k, h, tm, flip):
    pad = k // 2
    acc = None
    for kk in range(k):
        o = h + (pad - kk if flip else kk - pad)
        term = w_ref[kk:kk + 1, :] * ext_ref[o:o + tm, :]
        acc = term if acc is None else acc + term
    return acc


def _conv_post(u2, ln_g, ln_b, pw):
    mu = jnp.mean(u2, axis=-1, keepdims=True)
    var = jnp.mean(jnp.square(u2 - mu), axis=-1, keepdims=True)
    y = (u2 - mu) * lax.rsqrt(var + EPS) * ln_g + ln_b
    return dot_nn(_silu(y), pw)


def _conv_fwd(p, dw_w, dw_b, ln_g, ln_b, pw, name):
    t = p.shape[0]
    tm, w, k = _tm(), CONV_W, CONV_K
    h = _halo(k)
    off = _offsets()
    ab, bb = off["a"] // w, off["b"] // w

    def body(ap, ac, an, bp, bc, bn, w_ref, b_ref, g_ref, beta_ref, pw_ref, u2_ref, out_ref, ext):
        i = pl.program_id(0)
        glu = lambda a, b: a * _sigmoid(b)
        _fill_ext(ext, glu(ap[...], bp[...]), glu(ac[...], bc[...]), glu(an[...], bn[...]), i, h, tm)
        u2 = _corr(ext, w_ref, k, h, tm, False) + b_ref[...]
        u2_ref[...] = u2
        out_ref[...] = _conv_post(u2, g_ref[...], beta_ref[...], pw_ref[...]).astype(BF16)

    vec = pl.BlockSpec((1, w), lambda i: (0, 0))
    row = pl.BlockSpec((tm, w), lambda i: (i, 0))
    return pl.pallas_call(
        body, name=name, grid=(t // tm,),
        in_specs=_halo_specs(w, lambda: ab, h, tm) + _halo_specs(w, lambda: bb, h, tm)
        + [pl.BlockSpec((k, w), lambda i: (0, 0)), vec, vec, vec, pl.BlockSpec((w, w), lambda i: (0, 0))],
        out_specs=[row, row],
        out_shape=[jax.ShapeDtypeStruct((t, w), F32), jax.ShapeDtypeStruct((t, w), BF16)],
        scratch_shapes=[pltpu.VMEM((tm + 2 * h, w), F32)],
        compiler_params=_params(("parallel",)))(p, p, p, p, p, p, dw_w, dw_b, ln_g, ln_b, pw)


def _conv_bwd1(u2, dmix, ln_g, ln_b, pw, name):
    t = u2.shape[0]
    tm, w = _tm(), CONV_W
    cb = _ret_w() // w

    def body(u2_ref, dy_ref, g_ref, beta_ref, pw_ref, du2_ref, dg_ref, db_ref, dpw_ref):
        i = pl.program_id(0)

        @pl.when(i == 0)
        def _():
            dg_ref[...] = jnp.zeros_like(dg_ref)
            db_ref[...] = jnp.zeros_like(db_ref)
            dpw_ref[...] = jnp.zeros_like(dpw_ref)

        _, vjp = jax.vjp(_conv_post, u2_ref[...], g_ref[...], beta_ref[...], pw_ref[...])
        du2, dg, db, dpw = vjp(dy_ref[...])
        du2_ref[...] = du2
        dg_ref[...] += dg
        db_ref[...] += db
        dpw_ref[...] += dpw

    vec = pl.BlockSpec((1, w), lambda i: (0, 0))
    row = pl.BlockSpec((tm, w), lambda i: (i, 0))
    mat = pl.BlockSpec((w, w), lambda i: (0, 0))
    return pl.pallas_call(
        body, name=name, grid=(t // tm,),
        in_specs=[row, pl.BlockSpec((tm, w), lambda i: (i, cb)), vec, vec, mat],
        out_specs=[row, vec, vec, mat],
        out_shape=[jax.ShapeDtypeStruct((t, w), F32), jax.ShapeDtypeStruct((1, w), F32),
                   jax.ShapeDtypeStruct((1, w), F32), jax.ShapeDtypeStruct((w, w), F32)],
        compiler_params=_params(("arbitrary",)))(u2, dmix, ln_g, ln_b, pw)


def _conv_bwd2(du2, p, dw_w, name, comm=None):
    t = p.shape[0]
    tm, w, k = _tm(), CONV_W, CONV_K
    h = _halo(k)
    pad = k // 2
    off = _offsets()
    ab, bb = off["a"] // w, off["b"] // w

    def body(dp, dc, dn, ap, ac, an, bp, bc, bn, w_ref, da_ref, db_ref, dw_ref, dbias_ref, ext_d, ext_u):
        i = pl.program_id(0)

        @pl.when(i == 0)
        def _():
            dw_ref[...] = jnp.zeros_like(dw_ref)
            dbias_ref[...] = jnp.zeros_like(dbias_ref)

        glu = lambda a, b: a * _sigmoid(b)
        a, b, d = ac[...], bc[...], dc[...]
        _fill_ext(ext_d, dp[...], d, dn[...], i, h, tm)
        _fill_ext(ext_u, glu(ap[...], bp[...]), glu(a, b), glu(an[...], bn[...]), i, h, tm)
        du = _corr(ext_d, w_ref, k, h, tm, True)
        sg = _sigmoid(b)
        da_ref[...] = du * sg
        db_ref[...] = du * a * sg * (1.0 - sg)
        dbias_ref[...] += jnp.sum(d, axis=0, keepdims=True)
        for kk in range(k):
            o = h + kk - pad
            dw_ref[kk:kk + 1, :] += jnp.sum(d * ext_u[o:o + tm, :], axis=0, keepdims=True)

    vec = pl.BlockSpec((1, w), lambda i: (0, 0))
    row = pl.BlockSpec((tm, w), lambda i: (i, 0))
    kw = pl.BlockSpec((k, w), lambda i: (0, 0))
    return _call(
        body, name=name, grid=(t // tm,),
        in_specs=_halo_specs(w, lambda: 0, h, tm) + _halo_specs(w, lambda: ab, h, tm)
        + _halo_specs(w, lambda: bb, h, tm) + [kw],
        out_specs=[row, row, kw, vec],
        out_shape=[jax.ShapeDtypeStruct((t, w), F32), jax.ShapeDtypeStruct((t, w), F32),
                   jax.ShapeDtypeStruct((k, w), F32), jax.ShapeDtypeStruct((1, w), F32)],
        scratch=[pltpu.VMEM((tm + 2 * h, w), F32), pltpu.VMEM((tm + 2 * h, w), F32)],
        sem=("arbitrary",), args=(du2, du2, du2, p, p, p, p, p, p, dw_w), comm=comm)


def _ffn_tc():
    return _tile(D_FF, 512, LANES)


def _ffn_act_fwd(u, dw_w, dw_b, name):
    t = u.shape[0]
    tm, k, tc = _tm(), FFN_K, _ffn_tc()
    h = _halo(k)
    nj = D_FF // tc

    def body(vp, vc, vn, gp, gc, gn, wv, wg, bv, bg, out_ref, ext_v, ext_g):
        i = pl.program_id(0)
        _fill_ext(ext_v, vp[...], vc[...], vn[...], i, h, tm)
        _fill_ext(ext_g, gp[...], gc[...], gn[...], i, h, tm)
        for r0 in range(0, tm, ROW_CHUNK):
            val = _corr(ext_v, wv, k, h + r0, ROW_CHUNK, False) + bv[...]
            gate = _corr(ext_g, wg, k, h + r0, ROW_CHUNK, False) + bg[...]
            out_ref[r0:r0 + ROW_CHUNK, :] = (_silu(gate) * val).astype(BF16)

    wspec = lambda s: pl.BlockSpec((k, tc), lambda i, j: (0, j + s))
    bspec = lambda s: pl.BlockSpec((1, tc), lambda i, j: (0, j + s))
    return pl.pallas_call(
        body, name=name, grid=(t // tm, nj),
        in_specs=_halo_specs(tc, lambda j: j, h, tm) + _halo_specs(tc, lambda j: j + nj, h, tm)
        + [wspec(0), wspec(nj), bspec(0), bspec(nj)],
        out_specs=pl.BlockSpec((tm, tc), lambda i, j: (i, j)),
        out_shape=jax.ShapeDtypeStruct((t, D_FF), BF16),
        scratch_shapes=[pltpu.VMEM((tm + 2 * h, tc), F32), pltpu.VMEM((tm + 2 * h, tc), F32)],
        compiler_params=_params(("parallel", "parallel")))(u, u, u, u, u, u, dw_w, dw_w, dw_b, dw_b)


def _ffn_act_bwd1(u, da, dw_w, dw_b, name):
    t = u.shape[0]
    tm, k, tc = _tm(), FFN_K, _ffn_tc()
    h = _halo(k)
    nj = D_FF // tc

    def body(vp, vc, vn, gp, gc, gn, wv, wg, bv, bg, da_ref, dv_ref, dg_ref, ext_v, ext_g):
        i = pl.program_id(0)
        _fill_ext(ext_v, vp[...], vc[...], vn[...], i, h, tm)
        _fill_ext(ext_g, gp[...], gc[...], gn[...], i, h, tm)
        for r0 in range(0, tm, ROW_CHUNK):
            rows = slice(r0, r0 + ROW_CHUNK)
            val = _corr(ext_v, wv, k, h + r0, ROW_CHUNK, False) + bv[...]
            gate = _corr(ext_g, wg, k, h + r0, ROW_CHUNK, False) + bg[...]
            _, vjp = jax.vjp(lambda a, b: _silu(b) * a, val, gate)
            dval, dgate = vjp(da_ref[rows, :])
            dv_ref[rows, :] = dval
            dg_ref[rows, :] = dgate

    wspec = lambda s: pl.BlockSpec((k, tc), lambda i, j: (0, j + s))
    bspec = lambda s: pl.BlockSpec((1, tc), lambda i, j: (0, j + s))
    dc = pl.pallas_call(
        body, name=name, grid=(t // tm, nj),
        in_specs=_halo_specs(tc, lambda j: j, h, tm) + _halo_specs(tc, lambda j: j + nj, h, tm)
        + [wspec(0), wspec(nj), bspec(0), bspec(nj), pl.BlockSpec((tm, tc), lambda i, j: (i, j))],
        out_specs=[pl.BlockSpec((tm, tc), lambda i, j: (i, j)), pl.BlockSpec((tm, tc), lambda i, j: (i, j))],
        out_shape=[jax.ShapeDtypeStruct((t, D_FF), F32), jax.ShapeDtypeStruct((t, D_FF), F32)],
        scratch_shapes=[pltpu.VMEM((tm + 2 * h, tc), F32), pltpu.VMEM((tm + 2 * h, tc), F32)],
        compiler_params=_params(("parallel", "parallel")))(u, u, u, u, u, u, dw_w, dw_w, dw_b, dw_b, da)
    return dc


def _dwconv_bwd(dc, u, dw_w, colblock, name):
    t = u.shape[0]
    tm, k, tc = _tm(), FFN_K, _ffn_tc()
    h = _halo(k)
    pad = k // 2
    nj = D_FF // tc

    def body(dp, dcur, dn, up, uc, un, w_ref, du_ref, dw_ref, dbias_ref, ext_d, ext_u):
        i = pl.program_id(1)

        @pl.when(i == 0)
        def _():
            dw_ref[...] = jnp.zeros_like(dw_ref)
            dbias_ref[...] = jnp.zeros_like(dbias_ref)

        _fill_ext(ext_d, dp[...], dcur[...], dn[...], i, h, tm)
        _fill_ext(ext_u, up[...], uc[...], un[...], i, h, tm)
        acc_b = jnp.zeros((ROW_CHUNK, tc), F32)
        acc_w = [jnp.zeros((ROW_CHUNK, tc), F32) for _ in range(k)]
        for r0 in range(0, tm, ROW_CHUNK):
            d = ext_d[h + r0:h + r0 + ROW_CHUNK, :]
            du_ref[r0:r0 + ROW_CHUNK, :] = _corr(ext_d, w_ref, k, h + r0, ROW_CHUNK, True).astype(BF16)
            acc_b = acc_b + d
            for kk in range(k):
                o = h + r0 + kk - pad
                acc_w[kk] = acc_w[kk] + d * ext_u[o:o + ROW_CHUNK, :]
        dbias_ref[...] += jnp.sum(acc_b, axis=0, keepdims=True)
        for kk in range(k):
            dw_ref[kk:kk + 1, :] += jnp.sum(acc_w[kk], axis=0, keepdims=True)

    def hs(cb):
        r = tm // h
        return [pl.BlockSpec((h, tc), lambda j, i: (jnp.maximum(i * r - 1, 0), cb(j))),
                pl.BlockSpec((tm, tc), lambda j, i: (i, cb(j))),
                pl.BlockSpec((h, tc), lambda j, i: (jnp.minimum((i + 1) * r, (_t_rows() // h) - 1), cb(j)))]

    return pl.pallas_call(
        body, name=name, grid=(nj, t // tm),
        in_specs=hs(lambda j: j) + hs(lambda j: j + colblock) + [pl.BlockSpec((k, tc), lambda j, i: (0, j + colblock))],
        out_specs=[pl.BlockSpec((tm, tc), lambda j, i: (i, j)), pl.BlockSpec((k, tc), lambda j, i: (0, j)),
                   pl.BlockSpec((1, tc), lambda j, i: (0, j))],
        out_shape=[jax.ShapeDtypeStruct((t, D_FF), BF16), jax.ShapeDtypeStruct((k, D_FF), F32),
                   jax.ShapeDtypeStruct((1, D_FF), F32)],
        scratch_shapes=[pltpu.VMEM((tm + 2 * h, tc), F32), pltpu.VMEM((tm + 2 * h, tc), F32)],
        compiler_params=_params(("parallel", "arbitrary")))(dc, dc, dc, u, u, u, dw_w)


def _na_geometry(rq):
    ncb = CTX_LEN // GRID_W
    rows_n = SEQ // GRID_W
    r = jnp.maximum(rq - ncb, 0)
    kstart = jnp.clip(r - NA_ROWS // 2, 0, rows_n - NA_ROWS)
    base = kstart - r + NA_ROWS - 1
    return rq >= ncb, kstart, base


def _na_core(q, kl, vl, kc, vc, bias, mask):
    qs = q * (NA_DH ** -0.5)
    s_l = jnp.where(mask, dot_nt(qs, kl) + bias, NEG)
    s_c = dot_nt(qs, kc)
    m = lax.stop_gradient(jnp.maximum(jnp.max(s_l, axis=1, keepdims=True), jnp.max(s_c, axis=1, keepdims=True)))
    e_l, e_c = jnp.exp(s_l - m), jnp.exp(s_c - m)
    inv = 1.0 / (jnp.sum(e_l, axis=1, keepdims=True) + jnp.sum(e_c, axis=1, keepdims=True))
    return dot_nn(e_l * inv, vl) + dot_nn(e_c * inv, vc)


def _na_mask(is_lat):
    nl = NA_ROWS * GRID_W
    q = lax.broadcasted_iota(jnp.int32, (GRID_W, nl), 0)
    w = lax.broadcasted_iota(jnp.int32, (GRID_W, nl), 1) % GRID_W
    cs = jnp.clip(q - NA_COLS // 2, 0, GRID_W - NA_COLS)
    return (w >= cs) & (w < cs + NA_COLS) & is_lat


def _na_bias(rb_ref):
    assert 2 * GRID_W == LANES
    lane = lax.broadcasted_iota(jnp.int32, (GRID_W, LANES), 1)
    tiles = []
    for kp in range(NA_ROWS // 2):
        ev = jnp.broadcast_to(rb_ref[2 * kp:2 * kp + 1, :], (GRID_W, LANES))
        od = jnp.broadcast_to(rb_ref[2 * kp + 1:2 * kp + 2, :], (GRID_W, LANES))
        ev = pltpu.roll(ev, LANES - (NA_COLS - 1), 1, stride=1, stride_axis=0)
        od = pltpu.roll(od, LANES - (NA_COLS - 1) - GRID_W, 1, stride=1, stride_axis=0)
        tiles.append(jnp.where(lane < GRID_W, ev, od))
    return jnp.concatenate(tiles, axis=1)


def _na_dbias(dbias, drb_ref):
    qi = lax.broadcasted_iota(jnp.int32, (GRID_W, GRID_W), 0)
    qj = lax.broadcasted_iota(jnp.int32, (GRID_W, GRID_W), 1)
    flip = (qi + qj == GRID_W - 1).astype(F32)
    rev = lax.dot_general(flip, dbias, (((1,), (0,)), ((), ())), precision=lax.Precision.HIGHEST,
                          preferred_element_type=F32)
    lane = lax.broadcasted_iota(jnp.int32, (GRID_W, LANES), 1)
    s_ev = LANES - (GRID_W - NA_COLS)
    for kp in range(NA_ROWS // 2):
        tile = rev[:, kp * LANES:(kp + 1) * LANES]
        ev = pltpu.roll(jnp.where(lane < GRID_W, tile, 0.0), s_ev, 1, stride=1, stride_axis=0)
        od = pltpu.roll(jnp.where(lane >= GRID_W, tile, 0.0), s_ev - GRID_W, 1, stride=1, stride_axis=0)
        drb_ref[2 * kp:2 * kp + 1, :] += jnp.sum(ev, axis=0, keepdims=True)
        drb_ref[2 * kp + 1:2 * kp + 2, :] += jnp.sum(od, axis=0, keepdims=True)


def _na_hps():
    return 2 if NA_HEADS % 2 == 0 else 1


def _na_specs(p_offsets):
    t = _t_rows()
    hps = _na_hps()
    wd = hps * NA_DH
    assert all(p_offsets[n] % wd == 0 for n in ("nq", "nk", "nv"))
    qb, kb, vb = (p_offsets[n] // wd for n in ("nq", "nk", "nv"))
    return [pl.BlockSpec((GRID_W, wd), lambda h, r: (r, qb + h)),
            pl.BlockSpec((t, wd), lambda h, r: (0, kb + h)),
            pl.BlockSpec((t, wd), lambda h, r: (0, vb + h)),
            pl.BlockSpec((hps, None, NA_ROWS, LANES), lambda h, r: (h, _na_geometry(r)[2], 0, 0))]


def _na_fwd(p, rb, name, comm=None):
    t = p.shape[0]
    dh, nl = NA_DH, NA_ROWS * GRID_W

    hps = _na_hps()

    def body(q_ref, k_ref, v_ref, rb_ref, out_ref):
        rq = pl.program_id(1)
        is_lat, kstart, _ = _na_geometry(rq)
        start = pl.multiple_of(CTX_LEN + kstart * GRID_W, GRID_W)
        mask = _na_mask(is_lat)
        for hh in range(hps):
            cols = slice(hh * dh, (hh + 1) * dh)
            out = _na_core(q_ref[:, cols], k_ref[pl.ds(start, nl), cols], v_ref[pl.ds(start, nl), cols],
                           k_ref[0:CTX_LEN, cols], v_ref[0:CTX_LEN, cols], _na_bias(rb_ref.at[hh]), mask)
            out_ref[:, cols] = out.astype(BF16)

    res, cres = _call(
        body, name=name, grid=(NA_HEADS // hps, t // GRID_W), in_specs=_na_specs(_offsets()),
        out_specs=[pl.BlockSpec((GRID_W, hps * dh), lambda h, r: (r, h))],
        out_shape=[jax.ShapeDtypeStruct((t, _na_w()), BF16)],
        sem=("parallel", "arbitrary"), args=(p, p, p, rb), comm=comm)
    return res[0], cres


def _na_bwd(p, rb, dmix, name, comm=None):
    t = p.shape[0]
    dh, nl = NA_DH, NA_ROWS * GRID_W

    hps = _na_hps()
    wd = hps * dh
    assert ((_ret_w() + CONV_W) // dh) % hps == 0
    ob = (_ret_w() + CONV_W) // wd

    def body(q_ref, k_ref, v_ref, rb_ref, dy_ref, dq_ref, dk_ref, dv_ref, drb_ref):
        rq = pl.program_id(1)
        is_lat, kstart, base = _na_geometry(rq)
        _, _, prev_base = _na_geometry(rq - 1)
        start = pl.multiple_of(CTX_LEN + kstart * GRID_W, GRID_W)

        @pl.when(rq == 0)
        def _():
            dk_ref[...] = jnp.zeros_like(dk_ref)
            dv_ref[...] = jnp.zeros_like(dv_ref)

        @pl.when((rq == 0) | (base != prev_base))
        def _():
            drb_ref[...] = jnp.zeros_like(drb_ref)

        mask = _na_mask(is_lat)
        for hh in range(hps):
            cols = slice(hh * dh, (hh + 1) * dh)
            _, vjp = jax.vjp(lambda *a: _na_core(*a, mask), q_ref[:, cols], k_ref[pl.ds(start, nl), cols],
                             v_ref[pl.ds(start, nl), cols], k_ref[0:CTX_LEN, cols], v_ref[0:CTX_LEN, cols],
                             _na_bias(rb_ref.at[hh]))
            dq, dkl, dvl, dkc, dvc, dbias = vjp(dy_ref[:, cols])
            dq_ref[:, cols] = dq
            dk_ref[pl.ds(start, nl), cols] += dkl
            dv_ref[pl.ds(start, nl), cols] += dvl
            dk_ref[0:CTX_LEN, cols] += dkc
            dv_ref[0:CTX_LEN, cols] += dvc
            _na_dbias(dbias, drb_ref.at[hh])

    return _call(
        body, name=name, grid=(NA_HEADS // hps, t // GRID_W),
        in_specs=_na_specs(_offsets()) + [pl.BlockSpec((GRID_W, wd), lambda h, r: (r, ob + h))],
        out_specs=[pl.BlockSpec((GRID_W, wd), lambda h, r: (r, h)), pl.BlockSpec((t, wd), lambda h, r: (0, h)),
                   pl.BlockSpec((t, wd), lambda h, r: (0, h)),
                   pl.BlockSpec((hps, None, NA_ROWS, LANES), lambda h, r: (h, _na_geometry(r)[2], 0, 0))],
        out_shape=[jax.ShapeDtypeStruct((t, _na_w()), F32), jax.ShapeDtypeStruct((t, _na_w()), F32),
                   jax.ShapeDtypeStruct((t, _na_w()), F32),
                   jax.ShapeDtypeStruct((NA_HEADS, NA_ROWS, NA_ROWS, LANES), F32)],
        sem=("parallel", "arbitrary"), args=(p, p, p, rb, dmix), comm=comm)


def _rpb_select():
    sel = np.zeros((2 * NA_ROWS - 1, NA_ROWS * NA_ROWS), np.float32)
    for b in range(NA_ROWS):
        for kh in range(NA_ROWS):
            sel[b + kh, b * NA_ROWS + kh] = 1.0
    return jnp.asarray(sel)


def _rpb_rows(rpb):
    pad = jnp.pad(rpb, ((0, 0), (0, 0), (0, LANES - (2 * NA_COLS - 1))))
    rows = jnp.einsum("rk,hrc->hkc", _rpb_select(), pad, precision=lax.Precision.HIGHEST)
    return rows.reshape(NA_HEADS, NA_ROWS, NA_ROWS, LANES)


def _rpb_rows_t(drb):
    flat = drb.reshape(NA_HEADS, NA_ROWS * NA_ROWS, LANES)
    out = jnp.einsum("rk,hkc->hrc", _rpb_select(), flat, precision=lax.Precision.HIGHEST)
    return out[:, :, :2 * NA_COLS - 1]


def _assemble_dp(dqr, dkr, dvr, dgate, da, db, dnq, dnk, dnv, name):
    t = dgate.shape[0]
    tm = _tm()
    off = _offsets()
    sizes = dict(q=_ret_qk_w(), k=_ret_qk_w(), v=_ret_w(), g=_ret_w(), a=CONV_W, b=CONV_W, nq=_na_w(), nk=_na_w(), nv=_na_w())

    def body(q_ref, k_ref, v_ref, g_ref, a_ref, b_ref, nq_ref, nk_ref, nv_ref, o_ref):
        def put(n, val):
            o_ref[:, off[n]:off[n] + sizes[n]] = val.astype(BF16)

        put("q", q_ref[0] + q_ref[1])
        put("k", k_ref[0] + k_ref[1])
        put("v", v_ref[0] + v_ref[1])
        put("g", g_ref[...])
        put("a", a_ref[...])
        put("b", b_ref[...])
        put("nq", nq_ref[...])
        put("nk", nk_ref[...])
        put("nv", nv_ref[...])

    two = lambda w: pl.BlockSpec((2, tm, w), lambda i: (0, i, 0))
    one = lambda w: pl.BlockSpec((tm, w), lambda i: (i, 0))
    return pl.pallas_call(
        body, name=name, grid=(t // tm,),
        in_specs=[two(sizes["q"]), two(sizes["k"]), two(sizes["v"]), one(sizes["g"]), one(CONV_W), one(CONV_W),
                  one(_na_w()), one(_na_w()), one(_na_w())],
        out_specs=one(_d_in()), out_shape=jax.ShapeDtypeStruct((t, _d_in()), BF16),
        compiler_params=_params(("parallel",)))(dqr, dkr, dvr, dgate, da, db, dnq, dnk, dnv)


def _adamw(w, m, v, gs, name, comm=None):
    nl, r, c = w.shape
    stacked = not isinstance(gs, (list, tuple))
    if stacked:
        gs = [gs]
    assert stacked or len(gs) == nl
    g_n = gs[0].shape[-3]
    block_bytes = 2 * 1024 * 1024
    rows = min(block_bytes // (4 * c), block_bytes // (g_n * c * gs[0].dtype.itemsize))
    tr = _tile(r, max(2 * SUBLANES, rows // (2 * SUBLANES) * (2 * SUBLANES)), 2 * SUBLANES)
    nt = r // tr
    c1 = 1.0 - ADAM_B1 ** ADAM_STEP
    c2 = 1.0 - ADAM_B2 ** ADAM_STEP

    def body(w_ref, m_ref, v_ref, *rest):
        g_refs, (go_ref, d_ref, mo_ref, vo_ref) = rest[:len(gs)], rest[len(gs):]
        layer = pl.program_id(0)
        for ll in range(len(gs)):
            @pl.when(jnp.logical_or(stacked, layer == ll))
            def _():
                g_ref = g_refs[ll]
                g = g_ref[0].astype(F32)
                for j in range(1, g_n):
                    g = g + g_ref[j].astype(F32)
                mn = ADAM_B1 * m_ref[...] + (1.0 - ADAM_B1) * g
                vn = ADAM_B2 * v_ref[...] + (1.0 - ADAM_B2) * (g * g)
                m_hat = mn / c1
                v_hat = vn / c2
                go_ref[...] = g
                d_ref[...] = -ADAM_LR * (m_hat / (jnp.sqrt(v_hat) + ADAM_EPS) + ADAM_WD * w_ref[...])
                mo_ref[...] = mn
                vo_ref[...] = vn

    def g_spec(ll):
        if stacked:
            return pl.BlockSpec((None, g_n, tr, c), lambda l, i: (l, 0, i, 0))
        return pl.BlockSpec((g_n, tr, c), lambda l, i: (0, jnp.where(l == ll, i, jnp.where(l < ll, 0, nt - 1)), 0))

    blk = pl.BlockSpec((None, tr, c), lambda l, i: (l, i, 0))
    sds = jax.ShapeDtypeStruct((nl, r, c), F32)
    return _call(
        body, name=name, grid=(nl, nt),
        in_specs=[blk, blk, blk] + [g_spec(ll) for ll in range(len(gs))],
        out_specs=[blk, blk, blk, blk], out_shape=[sds, sds, sds, sds],
        sem=("arbitrary", "arbitrary"), args=(w, m, v, *gs), comm=comm)


def _sum_devices(g, name):
    _, r, c = g.shape
    tr = _tile(r, 512, SUBLANES)

    def body(g_ref, o_ref):
        acc = g_ref[0]
        for j in range(1, N_DEV):
            acc = acc + g_ref[j]
        o_ref[...] = acc

    return pl.pallas_call(body, name=name, grid=(r // tr,), in_specs=[pl.BlockSpec((N_DEV, tr, c), lambda i: (0, i, 0))],
                          out_specs=pl.BlockSpec((tr, c), lambda i: (i, 0)), out_shape=jax.ShapeDtypeStruct((r, c), F32),
                          compiler_params=_params(("parallel",)))(g)


def _ada_fwd(c16, w_ada, b_shard, name):
    nl, d, cs = w_ada.shape
    tk = _tile(d, 512, LANES)
    nk = d // tk

    def body(c_ref, w_ref, b_ref, o_ref):
        kk = pl.program_id(1)

        @pl.when(kk == 0)
        def _():
            o_ref[...] = jnp.broadcast_to(b_ref[...], o_ref.shape)

        o_ref[...] += _dg(_silu(c_ref[...]), w_ref[...], 1, 0)

    return pl.pallas_call(
        body, name=name, grid=(nl, nk),
        in_specs=[pl.BlockSpec((16, tk), lambda l, kk: (0, kk)), pl.BlockSpec((None, tk, cs), lambda l, kk: (l, kk, 0)),
                  pl.BlockSpec((None, 1, cs), lambda l, kk: (l, 0, 0))],
        out_specs=pl.BlockSpec((None, 16, cs), lambda l, kk: (l, 0, 0)),
        out_shape=jax.ShapeDtypeStruct((nl, 16, cs), F32),
        compiler_params=_params(("parallel", "arbitrary")))(c16, w_ada, b_shard)


def _ada_bwd(c16, dm16, w_ada, name):
    nl, d, cs = w_ada.shape
    td = _tile(d, 512, LANES)

    def body(c_ref, dm_ref, w_ref, gw_ref, dc_ref):
        cv = c_ref[...]
        s, vjp = jax.vjp(_silu, cv)
        gw_ref[...] = _dg(s, dm_ref[...], 0, 0)
        ds = _dg(dm_ref[...], w_ref[...], 1, 1)
        dc_ref[...] = vjp(ds)[0]

    return pl.pallas_call(
        body, name=name, grid=(nl, d // td),
        in_specs=[pl.BlockSpec((16, td), lambda l, i: (0, i)), pl.BlockSpec((None, 16, cs), lambda l, i: (l, 0, 0)),
                  pl.BlockSpec((None, td, cs), lambda l, i: (l, i, 0))],
        out_specs=[pl.BlockSpec((None, td, cs), lambda l, i: (l, i, 0)), pl.BlockSpec((None, 16, td), lambda l, i: (l, 0, i))],
        out_shape=[jax.ShapeDtypeStruct((nl, d, cs), F32), jax.ShapeDtypeStruct((nl, 16, d), F32)],
        compiler_params=_params(("parallel", "parallel")))(c16, dm16, w_ada)


def _pack_rows(shape):
    n = int(np.prod(shape))
    return SUBLANES * (-(-n // (LANES * SUBLANES)))


def _pack(arrays, row_align):
    parts = []
    for a in arrays:
        flat = a.reshape(-1).astype(F32)
        rows = _pack_rows(a.shape)
        parts.append(jnp.pad(flat, (0, rows * LANES - flat.shape[0])).reshape(rows, LANES))
    total = sum(p.shape[0] for p in parts)
    extra = -total % row_align
    if extra:
        parts.append(jnp.zeros((extra, LANES), F32))
    return jnp.concatenate(parts, axis=0)


def _unpack(packed, shapes):
    out, r = [], 0
    for s in shapes:
        rows = _pack_rows(s)
        out.append(packed[r:r + rows].reshape(-1)[:int(np.prod(s))].reshape(s))
        r += rows
    return out


def _rope_tables():
    half, nf = RET_DK // 2, RET_DK // 4
    pos = jnp.arange(SEQ)
    row = (pos // GRID_W).astype(F32)
    col = (pos % GRID_W).astype(F32)
    inv = ROPE_BASE ** (-jnp.arange(nf, dtype=F32) / nf)
    ar, ac = row[:, None] * inv[None, :], col[:, None] * inv[None, :]
    cos = jnp.concatenate([jnp.cos(ar), jnp.cos(ar), jnp.cos(ac), jnp.cos(ac)], axis=-1)
    sin = jnp.concatenate([-jnp.sin(ar), jnp.sin(ar), -jnp.sin(ac), jnp.sin(ac)], axis=-1)
    cos = jnp.concatenate([jnp.ones((CTX_LEN, RET_DK), F32), cos], axis=0)
    sin = jnp.concatenate([jnp.zeros((CTX_LEN, RET_DK), F32), sin], axis=0)
    return cos, sin


def _layer_fwd(l, x, mod4, w, cst, arrived):
    n = lambda s: f"l{l}_{s}"
    d = D_MODEL
    h1 = _normmod_fwd(x, w["norm1_g"], mod4, 0, n("norm1"))
    w["w_in"] = _cols_from_shards(arrived("w_in", h1), n("w_in_cols"))
    p, _ = _mm(h1, w["w_in"], n("proj_in"))
    o2, states = _ret_fwd(p, cst["cos"], cst["sin"], w["ret_decay"], cst["order"], n("ret_fwd"))
    ret_out = _ggn_fwd(o2, p, w["ret_gn_g"], n("ret_gn"))
    u2, conv_out = _conv_fwd(p, w["conv_dw_w"], w["conv_dw_b"], w["conv_ln_g"], w["conv_ln_b"], w["conv_pw"], n("conv_fwd"))
    na_out, _ = _na_fwd(p, w["rb"], n("na_fwd"))
    mix = jnp.concatenate([ret_out, conv_out, na_out], axis=1)
    w["w_out"] = arrived("w_out", mix).reshape(_d_mix(), d)
    g1, _ = _mm(mix, w["w_out"], n("proj_out"))
    x1 = _gate_res_fwd(x, g1, mod4, 2, n("res1"))
    h2 = _normmod_fwd(x1, w["norm2_g"], mod4, 1, n("norm2"))
    w["ffn_up"] = arrived("ffn_up", h2)
    u, _ = _mm(h2, w["ffn_up"], n("ffn_up"), b3=True)
    a = _ffn_act_fwd(u, w["ffn_dw_w"], w["ffn_dw_b"], n("ffn_act"))
    w["ffn_down"] = arrived("ffn_down", a).reshape(D_FF, d)
    f, _ = _mm(a, w["ffn_down"], n("ffn_down"))
    x2 = _gate_res_fwd(x1, f, mod4, 5, n("res2"))
    saved = dict(x=x, h1=h1, p=p, o2=o2, states=states, u2=u2, mix=mix, g1=g1, x1=x1, h2=h2, u=u, a=a, f=f)
    return x2, saved


def _layer_bwd(l, dx2, s, mod4, w, cst, send):
    n = lambda t: f"l{l}_{t}"
    d = D_MODEL
    nj = D_FF // _ffn_tc()
    dfg, dg2 = _gate_res_bwd(dx2, s["f"], mod4, 5, n("res2_bwd"))
    da, _ = _mm(dfg, w["ffn_down"], n("ffn_down_dx"), tb=True)
    d_ffn_down, _ = _mm(_transpose_bf16(s["a"], n("act_t")), dfg, n("ffn_down_dw"), out_dtype=BF16, tm_max=DW_TM)
    tok = send(("ffn_down", l), d_ffn_down.reshape(N_DEV, D_FF // N_DEV, d))
    dcv, dcg = _ffn_act_bwd1(s["u"], da, w["ffn_dw_w"], _after(w["ffn_dw_b"], tok), n("ffn_act_bwd"))
    duv, dwv, dbv = _dwconv_bwd(dcv, s["u"], w["ffn_dw_w"], 0, n("ffn_dw_bwd_val"))
    dug, dwg, dbg = _dwconv_bwd(dcg, s["u"], w["ffn_dw_w"], nj, n("ffn_dw_bwd_gate"))
    du = jnp.concatenate([duv, dug], axis=1)
    d_ffn_dw_w = jnp.concatenate([dwv, dwg], axis=1)
    d_ffn_dw_b = jnp.concatenate([dbv, dbg], axis=1)[0]
    dh2, _ = _mm(du, w["ffn_up"], n("ffn_up_dx"), tb=True, b3=True)
    d_ffn_up, _ = _mm(_transpose_bf16(s["h2"], n("h2_t")), du, n("ffn_up_dw"), out_dtype=BF16, tm_max=DW_TM,
                      o_cs=2 * D_FF // N_DEV)
    tok = send(("ffn_up", l), d_ffn_up)
    (dx1, dn2, dsh2, dsc2), _ = _normmod_bwd(s["x1"], _after(w["norm2_g"], tok), mod4, 1, dh2, dx2, n("norm2_bwd"))
    dgg, dg1 = _gate_res_bwd(dx1, s["g1"], mod4, 2, n("res1_bwd"))
    dmix, _ = _mm(dgg, w["w_out"], n("proj_out_dx"), tb=True)
    d_w_out, _ = _mm(_transpose_bf16(s["mix"], n("mix_t")), dgg, n("proj_out_dw"), out_dtype=BF16, tm_max=DW_TM)
    tok = send(("w_out", l), d_w_out.reshape(N_DEV, _d_mix() // N_DEV, d))
    do, dgate, dgn = _ggn_bwd(s["o2"], s["p"], _after(w["ret_gn_g"], tok), dmix, n("ret_gn_bwd"))
    dqr, dkr, dvr, ddec = _ret_bwd(s["p"], cst["cos"], cst["sin"], w["ret_decay"], cst["order"], s["states"], do, n("ret_bwd"))
    du2, dlng, dlnb, dpw = _conv_bwd1(s["u2"], dmix, w["conv_ln_g"], w["conv_ln_b"], w["conv_pw"], n("conv_bwd1"))
    (dca, dcb, ddww, ddwb), _ = _conv_bwd2(du2, s["p"], w["conv_dw_w"], n("conv_bwd2"))
    (dnq, dnk, dnv, drb), _ = _na_bwd(s["p"], w["rb"], dmix, n("na_bwd"))
    dp = _assemble_dp(dqr, dkr, dvr, dgate, dca, dcb, dnq, dnk, dnv, n("dproj"))
    h1_t = _transpose_bf16(s["h1"], n("h1_t"))
    half = d // 2
    for i in range(2):
        d_w_in, _ = _mm(h1_t[i * half:(i + 1) * half], dp, n(f"proj_in_dw{i}"), out_dtype=BF16, tm_max=DW_TM,
                        o_cs=_d_in() // N_DEV)
        tok = send(("w_in", l, i), d_w_in)
    dh1, _ = _mm(dp, w["w_in"], n("proj_in_dx"), tb=True, after=tok)
    (dx, dn1, dsh1, dsc1), _ = _normmod_bwd(s["x"], _after(w["norm1_g"], tok), mod4, 0, dh1, dx1, n("norm1_bwd"))
    dmod = jnp.concatenate([dsh1, dsc1, dg1, dsh2, dsc2, dg2], axis=1)
    small = dict(norm1_g=dn1[0], ret_decay=ddec[:, :, 0, 0], ret_gn_g=dgn[0], conv_dw_w=ddww, conv_dw_b=ddwb[0],
                 conv_ln_g=dlng[0], conv_ln_b=dlnb[0], conv_pw=dpw, na_rpb=_rpb_rows_t(drb), norm2_g=dn2[0],
                 ffn_dw_w=d_ffn_dw_w, ffn_dw_b=d_ffn_dw_b)
    return dx, dmod, small


def _d_mix():
    return _ret_w() + CONV_W + _na_w()


_SMALL = ["c_ctx", "b_ada", "norm1_g", "ret_decay", "ret_gn_g", "conv_dw_w", "conv_dw_b", "conv_ln_g", "conv_ln_b",
          "conv_pw", "na_rpb", "norm2_g", "ffn_dw_w", "ffn_dw_b", "final_g"]
_SMALL_SHARD_AXIS = {"conv_dw_w": 2, "conv_pw": 1, "ffn_dw_w": 2}


def kernel(x, c, ctx, c_ctx, w_ada, b_ada, norm1_g, w_in, ret_decay, ret_gn_g, conv_dw_w, conv_dw_b, conv_ln_g, conv_ln_b, conv_pw, na_rpb, w_out, norm2_g, ffn_up, ffn_dw_w, ffn_dw_b, ffn_down, final_g, loss_target, m_c_ctx, m_w_ada, m_b_ada, m_norm1_g, m_w_in, m_ret_decay, m_ret_gn_g, m_conv_dw_w, m_conv_dw_b, m_conv_ln_g, m_conv_ln_b, m_conv_pw, m_na_rpb, m_w_out, m_norm2_g, m_ffn_up, m_ffn_dw_w, m_ffn_dw_b, m_ffn_down, m_final_g, v_c_ctx, v_w_ada, v_b_ada, v_norm1_g, v_w_in, v_ret_decay, v_ret_gn_g, v_conv_dw_w, v_conv_dw_b, v_conv_ln_g, v_conv_ln_b, v_conv_pw, v_na_rpb, v_w_out, v_norm2_g, v_ffn_up, v_ffn_dw_w, v_ffn_dw_b, v_ffn_down, v_final_g):
    d, nl = D_MODEL, DEPTH
    cs = 6 * d // N_DEV
    me = _my_index()
    weights = dict(c_ctx=c_ctx, w_ada=w_ada, b_ada=b_ada, norm1_g=norm1_g, w_in=w_in, ret_decay=ret_decay, ret_gn_g=ret_gn_g,
                   conv_dw_w=conv_dw_w, conv_dw_b=conv_dw_b, conv_ln_g=conv_ln_g, conv_ln_b=conv_ln_b, conv_pw=conv_pw,
                   na_rpb=na_rpb, w_out=w_out, norm2_g=norm2_g, ffn_up=ffn_up, ffn_dw_w=ffn_dw_w, ffn_dw_b=ffn_dw_b,
                   ffn_down=ffn_down, final_g=final_g)
    mom = dict(c_ctx=m_c_ctx, w_ada=m_w_ada, b_ada=m_b_ada, norm1_g=m_norm1_g, w_in=m_w_in, ret_decay=m_ret_decay,
               ret_gn_g=m_ret_gn_g, conv_dw_w=m_conv_dw_w, conv_dw_b=m_conv_dw_b, conv_ln_g=m_conv_ln_g,
               conv_ln_b=m_conv_ln_b, conv_pw=m_conv_pw, na_rpb=m_na_rpb, w_out=m_w_out, norm2_g=m_norm2_g,
               ffn_up=m_ffn_up, ffn_dw_w=m_ffn_dw_w, ffn_dw_b=m_ffn_dw_b, ffn_down=m_ffn_down, final_g=m_final_g)
    var = dict(c_ctx=v_c_ctx, w_ada=v_w_ada, b_ada=v_b_ada, norm1_g=v_norm1_g, w_in=v_w_in, ret_decay=v_ret_decay,
               ret_gn_g=v_ret_gn_g, conv_dw_w=v_conv_dw_w, conv_dw_b=v_conv_dw_b, conv_ln_g=v_conv_ln_g,
               conv_ln_b=v_conv_ln_b, conv_pw=v_conv_pw, na_rpb=v_na_rpb, w_out=v_w_out, norm2_g=v_norm2_g,
               ffn_up=v_ffn_up, ffn_dw_w=v_ffn_dw_w, ffn_dw_b=v_ffn_dw_b, ffn_down=v_ffn_down, final_g=v_final_g)

    big_names = ["w_in", "w_out", "ffn_up", "ffn_down"]
    shards = {(nm, l): _cast_bf16(weights[nm][l], f"cast_{nm}{l}") for l in range(nl) for nm in big_names}
    small_sharded = _pack([conv_dw_w, conv_pw, ffn_dw_w], SUBLANES)
    c_rows = jnp.pad(c, ((0, SUBLANES - 1), (0, 0)))
    gathered = _run_comm(_Gather([c_rows, small_sharded, shards[("w_in", 0)]]), "gather_first")
    c_all = gathered[0][:, 0, :]
    def whole(rows, shard_shape, axis):
        n_el = int(np.prod(shard_shape))
        parts = rows.reshape(N_DEV, -1)[:, :n_el].reshape((N_DEV,) + tuple(shard_shape))
        parts = jnp.moveaxis(parts, 0, axis)
        return parts.reshape(shard_shape[:axis] + (N_DEV * shard_shape[axis],) + shard_shape[axis + 1:])

    r0 = _pack_rows(conv_dw_w.shape)
    r1 = r0 + _pack_rows(conv_pw.shape)
    r2 = r1 + _pack_rows(ffn_dw_w.shape)
    full_conv_dw_w = whole(gathered[1][:, :r0], conv_dw_w.shape, 2)
    full_conv_pw = whole(gathered[1][:, r0:r1], conv_pw.shape, 1)
    full_ffn_dw_w = whole(gathered[1][:, r1:r2], ffn_dw_w.shape, 2)

    c16 = jnp.concatenate([c_all, jnp.broadcast_to(c_ctx[None, :], (N_DEV, d))], axis=0)
    b_shard = lax.dynamic_slice_in_dim(b_ada, me * cs, cs, axis=1)[:, None, :]
    m_shard = _ada_fwd(c16, w_ada, b_shard, "ada_fwd")
    m_all = _run_comm(_Gather([m_shard.reshape(nl * 16, cs)]), "gather_mod")[0]
    m_full = m_all.reshape(N_DEV, nl, 16, cs).transpose(1, 2, 0, 3).reshape(nl, 16, 6 * d)
    m_lat = lax.dynamic_index_in_dim(m_full, me, axis=1, keepdims=False)
    mod = jnp.stack([m_full[:, N_DEV], m_lat], axis=1).reshape(nl, 2, 6, 1, d)

    arriving, token = {}, m_all
    for l in range(nl):
        for nm in big_names:
            if (nm, l) != ("w_in", 0):
                arriving[(nm, l)], token = _split_start(shards[(nm, l)], True, f"gather_{nm}{l}", token)
    mod = _after(mod, token)

    cos, sin = _rope_tables()
    cst = dict(cos=cos, sin=sin, order=_chunk_order())
    layer_w = []
    for l in range(nl):
        layer_w.append(dict(
            norm1_g=norm1_g[l][None], norm2_g=norm2_g[l][None], ret_decay=ret_decay[l], ret_gn_g=ret_gn_g[l][None],
            conv_dw_w=full_conv_dw_w[l], conv_dw_b=conv_dw_b[l][None], conv_ln_g=conv_ln_g[l][None],
            conv_ln_b=conv_ln_b[l][None], conv_pw=full_conv_pw[l], rb=_rpb_rows(na_rpb[l]),
            ffn_dw_w=full_ffn_dw_w[l], ffn_dw_b=ffn_dw_b[l][None]))

    xs = jnp.concatenate([ctx[0], x[0]], axis=0)
    saved = []
    for l in range(nl):
        def arrived(nm, after, l=l):
            if (nm, l) == ("w_in", 0):
                return gathered[2]
            return _split_wait(arriving[(nm, l)], after, f"arrived_{nm}{l}")

        xs, sv = _layer_fwd(l, xs, mod[l], layer_w[l], cst, arrived)
        saved.append(sv)
    loss_tile, dxs, dfinal = _loss_head(xs, final_g[None], loss_target[0], "loss_head")
    loss = lax.psum(loss_tile[0, 0], ("x", "y", "c"))

    dmods, smalls = [None] * nl, [None] * nl
    leaving, last = {}, [loss_tile]

    def send(key, partial):
        leaving[key], token = _split_start(partial, False, "send_" + "_".join(str(k) for k in key), last[0])
        last[0] = token
        return token

    per_layer = [nm for nm in _SMALL if nm not in ("c_ctx", "b_ada", "final_g")]
    small_packs, small_arriving = [None] * nl, [None] * nl
    for l in reversed(range(nl)):
        dxs, dmods[l], smalls[l] = _layer_bwd(l, dxs, saved[l], mod[l], layer_w[l], cst, send)
        small_packs[l] = _pack([smalls[l][nm] for nm in per_layer], 512)
        if l > 0:
            small_arriving[l], last[0] = _split_start(small_packs[l], True, f"gather_small_grads{l}", last[0])
    grad_x = dxs[CTX_LEN:][None]

    arrived_grad = lambda key, after: _split_wait(leaving[key], after, "got_" + "_".join(str(k) for k in key))
    out_big = {}
    after = dxs
    for nm in ["ffn_down", "ffn_up", "w_out"]:
        out_big[nm], _ = _adamw(weights[nm], mom[nm], var[nm], [arrived_grad((nm, l), after) for l in range(nl)],
                                f"adamw_{nm}")
        after = out_big[nm][0]

    dm_mine = jnp.stack(dmods).reshape(nl * 2, 6 * d)
    dm_rows = jnp.pad(dm_mine, ((0, SUBLANES - nl * 2), (0, 0)))
    dm_all = _run_comm(_Gather([dm_rows]), "gather_dmod", after=after)[0][:, :nl * 2].reshape(N_DEV, nl, 2, 6 * d)
    dm16_full = jnp.concatenate([dm_all[:, :, 1].transpose(1, 0, 2), dm_all[:, :, 0].transpose(1, 0, 2)], axis=1)
    dm16 = lax.dynamic_slice_in_dim(dm16_full, me * cs, cs, axis=2)
    g_w_ada, dc16 = _ada_bwd(c16, dm16, w_ada, "ada_bwd")

    shared = dict(c_ctx=jnp.sum(dc16[:, N_DEV:], axis=(0, 1)),
                  b_ada=jnp.sum(jnp.stack(dmods).reshape(nl, 2, 6 * d), axis=1), final_g=dfinal[0])
    shared_all = _run_comm(_Gather([_pack(list(shared.values()), SUBLANES)]), "gather_shared_grads")[0]
    small_arriving[0], token = _split_start(small_packs[0], True, "gather_small_grads0", shared_all)

    out_big["w_ada"], _ = _adamw(w_ada, m_w_ada, v_w_ada, g_w_ada[:, None], "adamw_w_ada")
    halves = lambda a: a.reshape(2 * nl, d // 2, a.shape[2])
    res, _ = _adamw(halves(w_in), halves(m_w_in), halves(v_w_in),
                    [arrived_grad(("w_in", l, i), token) for l in range(nl) for i in range(2)], "adamw_w_in")
    out_big["w_in"] = [r.reshape(w_in.shape) for r in res]

    g_small = dict(zip(shared, _unpack(_sum_devices(shared_all, "sum_shared_grads"), [v.shape for v in shared.values()])))
    per = []
    for l in range(nl):
        got = _split_wait(small_arriving[l], res[0], f"arrived_small_grads{l}")
        per.append(_unpack(_sum_devices(got, f"sum_small_grads{l}"), [smalls[l][nm].shape for nm in per_layer]))
    g_small.update({nm: jnp.stack([per[l][i] for l in range(nl)]) for i, nm in enumerate(per_layer)})
    for nm, ax in _SMALL_SHARD_AXIS.items():
        n_sh = weights[nm].shape[ax]
        g_small[nm] = lax.dynamic_slice_in_dim(g_small[nm], me * n_sh, n_sh, axis=ax)
    shapes_own = [weights[nm].shape for nm in _SMALL]
    pk = lambda src: _pack([src[nm] for nm in _SMALL], 2 * SUBLANES)[None]
    res_small, _ = _adamw(pk(weights), pk(mom), pk(var), pk(g_small)[:, None], "adamw_small")
    out_small = [dict(zip(_SMALL, _unpack(r[0], shapes_own))) for r in res_small]

    names = ["c_ctx", "w_ada", "b_ada", "norm1_g", "w_in", "ret_decay", "ret_gn_g", "conv_dw_w", "conv_dw_b", "conv_ln_g",
             "conv_ln_b", "conv_pw", "na_rpb", "w_out", "norm2_g", "ffn_up", "ffn_dw_w", "ffn_dw_b", "ffn_down", "final_g"]
    outs = [loss, grad_x]
    for kind in range(4):
        for nm in names:
            outs.append(out_big[nm][kind] if nm in out_big else out_small[kind][nm])
    return tuple(outs)
```

```python
import functools
import math

import numpy as np
import jax
import jax.numpy as jnp
from jax import lax
from jax.experimental import pallas as pl
from jax.experimental.pallas import tpu as pltpu

D_MODEL = 2048
SEQ = 4096
DEPTH = 2
GRID_W = 64
CTX_LEN = 256
RET_HEADS = 4
RET_DK = 128
RET_DV = 256
RET_CHUNK = 128
CONV_W = 512
CONV_K = 31
NA_HEADS = 4
NA_DH = 128
NA_ROWS = 8
NA_COLS = 16
D_FF = 5632
FFN_K = 3
ROPE_BASE = 10000.0
EPS = 1e-6
ADAM_LR = 0.001
ADAM_B1 = 0.9
ADAM_B2 = 0.999
ADAM_EPS = 1e-08
ADAM_WD = 0.01
ADAM_STEP = 10
N_DEV = 8

LANES = 128
SUBLANES = 8
VMEM_LIMIT = 56 * 1024 * 1024
ROW_CHUNK = 16

F32 = jnp.float32
BF16 = jnp.bfloat16
MESH = pl.DeviceIdType.MESH
NEG = -1e30


def _ret_qk_w():
    return RET_HEADS * RET_DK


def _ret_w():
    return RET_HEADS * RET_DV


def _na_w():
    return NA_HEADS * NA_DH


def _d_in():
    return 2 * _ret_qk_w() + 2 * _ret_w() + 2 * CONV_W + 3 * _na_w()


def _offsets():
    sizes = [_ret_qk_w(), _ret_qk_w(), _ret_w(), _ret_w(), CONV_W, CONV_W, _na_w(), _na_w(), _na_w()]
    offs = [0]
    for s in sizes[:-1]:
        offs.append(offs[-1] + s)
    return dict(zip(["q", "k", "v", "g", "a", "b", "nq", "nk", "nv"], offs))


def _t_rows():
    return CTX_LEN + SEQ


def _tm():
    return CTX_LEN


def _params(sem=None):
    kw = dict(vmem_limit_bytes=VMEM_LIMIT)
    if sem is not None:
        kw["dimension_semantics"] = sem
    return pltpu.CompilerParams(**kw)


def _tile(n, pref, align):
    best = None
    for t in range(align, min(n, pref) + 1, align):
        if n % t == 0:
            best = t
    return best if best is not None else n


def _dg(a, b, ca, cb):
    return lax.dot_general(a.astype(BF16), b.astype(BF16), (((ca,), (cb,)), ((), ())), preferred_element_type=F32)


@jax.custom_vjp
def dot_nn(a, b):
    return _dg(a, b, 1, 0)


dot_nn.defvjp(lambda a, b: (_dg(a, b, 1, 0), (a, b)),
              lambda r, g: (_dg(g, r[1], 1, 1), _dg(r[0], g, 0, 0)))


@jax.custom_vjp
def dot_nt(a, b):
    return _dg(a, b, 1, 1)


dot_nt.defvjp(lambda a, b: (_dg(a, b, 1, 1), (a, b)),
              lambda r, g: (_dg(g, r[1], 1, 0), _dg(g, r[0], 0, 0)))


@jax.custom_vjp
def dot_tn(a, b):
    return _dg(a, b, 0, 0)


dot_tn.defvjp(lambda a, b: (_dg(a, b, 0, 0), (a, b)),
              lambda r, g: (_dg(r[1], g, 1, 1), _dg(r[0], g, 1, 0)))


def _sigmoid(x):
    return 0.5 * jnp.tanh(0.5 * x) + 0.5


def _silu(x):
    return x * _sigmoid(x)


def _my_pos():
    return lax.axis_index("x"), lax.axis_index("y"), lax.axis_index("c")


def _my_index():
    x, y, c = _my_pos()
    return 4 * x + 2 * y + c


_ANY = pl.BlockSpec(memory_space=pl.ANY)


class _Gather:
    def __init__(self, arrays):
        self.arrays = list(arrays)
        n = len(self.arrays)
        self.out_shape = [jax.ShapeDtypeStruct((N_DEV,) + a.shape, a.dtype) for a in self.arrays]
        self.scratch = [pltpu.SemaphoreType.DMA((n, 7)), pltpu.SemaphoreType.DMA((n, 7)), pltpu.SemaphoreType.DMA((n,))]

    def _plan(self, xs, outs, sems):
        send_sems, recv_sems, local_sems = sems
        n = len(self.arrays)
        x, y, c = _my_pos()
        me, sibling = (x, y, c), (x, y, 1 - c)
        chips = [(1 - x, y), (x, 1 - y), (1 - x, 1 - y)]

        def slot(a, p):
            return outs[a].at[4 * p[0] + 2 * p[1] + p[2]]

        def copy(a, k, block, to, src=None):
            return pltpu.make_async_remote_copy(
                src_ref=slot(a, block) if src is None else src, dst_ref=slot(a, block),
                send_sem=send_sems.at[a, k], recv_sem=recv_sems.at[a, k], device_id=to, device_id_type=MESH)

        mine = [pltpu.make_async_copy(xs[a], slot(a, me), local_sems.at[a]) for a in range(n)]
        first = []
        for a in range(n):
            first.append(copy(a, 0, me, sibling, src=xs[a]))
            first += [copy(a, 1 + j, me, (*chip, c), src=xs[a]) for j, chip in enumerate(chips)]
        return n, c, me, sibling, chips, copy, mine, first

    def start(self, xs, outs, sems):
        _, _, _, _, _, _, mine, first = self._plan(xs, outs, sems)
        for m in mine:
            m.start()
        for cp in first:
            cp.start()

    def finish(self, xs, outs, sems):
        n, c, me, sibling, chips, copy, mine, first = self._plan(xs, outs, sems)
        passed = []
        for a in range(n):
            for j, chip in enumerate(chips):
                copy(a, 1 + j, (*chip, c), me).wait_recv()
                p = copy(a, 4 + j, (*chip, c), sibling)
                p.start()
                passed.append(p)
        for a in range(n):
            copy(a, 0, sibling, me).wait_recv()
            for j, chip in enumerate(chips):
                copy(a, 4 + j, (*chip, 1 - c), me).wait_recv()
        for cp in first + passed:
            cp.wait_send()
        for m in mine:
            m.wait()


def _run_comm(comm, name, after=None):
    n = len(comm.arrays)
    extra = [] if after is None else [after]

    def body(*refs):
        xs, outs, sems = refs[:n], refs[n + len(extra):2 * n + len(extra)], refs[2 * n + len(extra):]
        comm.start(xs, outs, sems)
        comm.finish(xs, outs, sems)

    return pl.pallas_call(body, name=name, out_shape=comm.out_shape, in_specs=[_ANY] * (n + len(extra)),
                          out_specs=[_ANY] * n, scratch_shapes=comm.scratch)(*comm.arrays, *extra)


_HBM = pl.BlockSpec(memory_space=pltpu.HBM)
_SEMS = pl.BlockSpec(memory_space=pltpu.SEMAPHORE)
_EFFECT = pltpu.SideEffectType.DATAFLOW_SIDE_EFFECTING


def _own_slot(x, gathering, name):
    shape = (N_DEV,) + x.shape if gathering else x.shape
    r, c = shape[1], shape[2]
    tr = _tile(r, 256, 2 * SUBLANES)
    me = jnp.reshape(_my_index(), (1,)).astype(jnp.int32)

    def body(me_ref, x_ref, o_ref):
        o_ref[...] = x_ref[...]

    src = (pl.BlockSpec((tr, c), lambda i, m: (i, 0)) if gathering
           else pl.BlockSpec((None, tr, c), lambda i, m: (m[0], i, 0)))
    grid_spec = pltpu.PrefetchScalarGridSpec(
        num_scalar_prefetch=1, grid=(r // tr,), in_specs=[src],
        out_specs=pl.BlockSpec((None, tr, c), lambda i, m: (m[0], i, 0)))
    return pl.pallas_call(body, name=name, grid_spec=grid_spec, out_shape=jax.ShapeDtypeStruct(shape, x.dtype),
                          compiler_params=_params(("arbitrary",)))(me, x)


def _split_plan(x_ref, land_ref, send_sems, recv_sems, gathering):
    x, y, c = _my_pos()
    me = 4 * x + 2 * y + c
    sends, recvs = [], []
    for k in range(1, N_DEV):
        px = 1 - x if (k >> 2) & 1 else x
        py = 1 - y if (k >> 1) & 1 else y
        pc = 1 - c if k & 1 else c
        peer = 4 * px + 2 * py + pc
        mine, theirs = (x_ref, x_ref) if gathering else (x_ref.at[peer], x_ref.at[me])
        sends.append(pltpu.make_async_remote_copy(
            src_ref=mine, dst_ref=land_ref.at[me], send_sem=send_sems.at[k - 1], recv_sem=recv_sems.at[k - 1],
            device_id=(px, py, pc), device_id_type=MESH))
        recvs.append(pltpu.make_async_remote_copy(
            src_ref=theirs, dst_ref=land_ref.at[peer], send_sem=send_sems.at[k - 1], recv_sem=recv_sems.at[k - 1],
            device_id=(px, py, pc), device_id_type=MESH))
    return sends, recvs


def _split_start(x, gathering, name, prev):
    land = _own_slot(x, gathering, name + "_own")

    def body(x_ref, land_ref, prev_ref, send_sems, recv_sems, x_thru, land_thru, token):
        sends, _ = _split_plan(x_ref, land_ref, send_sems, recv_sems, gathering)
        for s in sends:
            s.start()
        token[...] = jnp.zeros_like(token)

    sems = pltpu.SemaphoreType.DMA((N_DEV - 1,))
    send_sems, recv_sems, x_thru, land_thru, token = pl.pallas_call(
        body, name=name,
        out_shape=(sems, sems, pltpu.HBM(x.shape, x.dtype), pltpu.HBM(land.shape, land.dtype),
                   jax.ShapeDtypeStruct((SUBLANES, LANES), F32)),
        in_specs=(_HBM, _HBM, _ANY), out_specs=(_SEMS, _SEMS, _HBM, _HBM, pl.BlockSpec(memory_space=pltpu.VMEM)),
        input_output_aliases={0: 2, 1: 3},
        compiler_params=pltpu.CompilerParams(has_side_effects=_EFFECT),
    )(pltpu.with_memory_space_constraint(x, pltpu.HBM), pltpu.with_memory_space_constraint(land, pltpu.HBM), prev)
    return (send_sems, recv_sems, x_thru, land_thru, gathering), token


def _after(a, token):
    return a + token[0, 0].astype(a.dtype)


def _split_wait(handle, after, name):
    send_sems, recv_sems, x_thru, land_thru, gathering = handle

    def body(x_ref, land_ref, send_sems, recv_sems, after_ref, x_dead, got_ref):
        sends, recvs = _split_plan(x_ref, land_ref, send_sems, recv_sems, gathering)
        for s in sends:
            s.wait_send()
        for r in recvs:
            r.wait_recv()

    return pl.pallas_call(
        body, name=name, out_shape=(pltpu.HBM(x_thru.shape, x_thru.dtype), pltpu.HBM(land_thru.shape, land_thru.dtype)),
        in_specs=(_HBM, _HBM, _SEMS, _SEMS, _ANY), out_specs=(_HBM, _HBM), input_output_aliases={0: 0, 1: 1},
        compiler_params=pltpu.CompilerParams(has_side_effects=_EFFECT),
    )(x_thru, land_thru, send_sems, recv_sems, after)[1]


def _call(body, *, name, grid, in_specs, out_specs, out_shape, args, scratch=(), sem=None, comm=None, after=None):
    assert comm is None
    if after is None:
        res = pl.pallas_call(body, name=name, grid=grid, in_specs=list(in_specs), out_specs=list(out_specs),
                             out_shape=list(out_shape), scratch_shapes=list(scratch), compiler_params=_params(sem))(*args)
        return list(res), None
    n_in = len(in_specs)

    def wrapped(*refs):
        body(*refs[:n_in], *refs[n_in + 1:])

    res = pl.pallas_call(wrapped, name=name, grid=grid, in_specs=list(in_specs) + [_ANY], out_specs=list(out_specs),
                         out_shape=list(out_shape), scratch_shapes=list(scratch), compiler_params=_params(sem))(*args, after)
    return list(res), None


MM_B_BLOCK_BYTES = 6 * 1024 * 1024
MM_O_BLOCK_BYTES = 13 * 1024 * 1024 // 2


def _mm(a, b, name, tb=False, out_dtype=F32, b3=False, o_cs=None, tm_max=1088, comm=None, after=None):
    m, k = a.shape
    if b3:
        cs = b.shape[2]
        n, kb = (b.shape[1], N_DEV * cs) if tb else (N_DEV * cs, b.shape[1])
    else:
        n, kb = (b.shape[0], b.shape[1]) if tb else (b.shape[1], b.shape[0])
    assert k == kb, (a.shape, b.shape, tb)
    if b3 and tb:
        tm, tn = _tile(m, 544, 2 * SUBLANES), _tile(n, 256, LANES)

        def body_shards(a_ref, b_ref, o_ref):
            r = None
            for j in range(N_DEV):
                part = lax.dot_general(a_ref[:, j * cs:(j + 1) * cs], b_ref[j], (((1,), (1,)), ((), ())),
                                       preferred_element_type=F32)
                r = part if r is None else r + part
            o_ref[...] = r.astype(o_ref.dtype)

        res, cres = _call(
            body_shards, name=name, grid=(m // tm, n // tn),
            in_specs=[pl.BlockSpec((tm, k), lambda i, j: (i, 0)), pl.BlockSpec((N_DEV, tn, cs), lambda i, j: (0, j, 0))],
            out_specs=[pl.BlockSpec((tm, tn), lambda i, j: (i, j))], out_shape=[jax.ShapeDtypeStruct((m, n), out_dtype)],
            sem=("parallel", "parallel"), args=(a, b), comm=comm, after=after)
        return res[0], cres
    tm = _tile(m, tm_max, 2 * SUBLANES)
    tk = k
    if b3:
        tn = cs
    elif o_cs is not None:
        tn = o_cs if o_cs % LANES == 0 else 2 * o_cs
    else:
        tn = _tile(n, min(MM_B_BLOCK_BYTES // (2 * tk), MM_O_BLOCK_BYTES // (4 * tm)), LANES)
    cb = 1 if tb else 0
    dn = (((1,), (cb,)), ((), ()))

    def body_one(a_ref, b_ref, o_ref):
        r = lax.dot_general(a_ref[...], b_ref[...], dn, preferred_element_type=F32)
        if o_cs is None:
            o_ref[...] = r.astype(o_ref.dtype)
        else:
            for j in range(tn // o_cs):
                o_ref[j] = r[:, j * o_cs:(j + 1) * o_cs].astype(o_ref.dtype)

    a_spec = pl.BlockSpec((tm, tk), lambda i, j: (i, 0))
    if b3:
        b_spec = pl.BlockSpec((None, tk, cs), lambda i, j: (j, 0, 0))
    else:
        b_spec = pl.BlockSpec((tn, tk), lambda i, j: (j, 0)) if tb else pl.BlockSpec((tk, tn), lambda i, j: (0, j))
    if o_cs is None:
        o_spec = pl.BlockSpec((tm, tn), lambda i, j: (i, j))
        o_shape = jax.ShapeDtypeStruct((m, n), out_dtype)
    else:
        o_spec = pl.BlockSpec((tn // o_cs, tm, o_cs), lambda i, j: (j, i, 0))
        o_shape = jax.ShapeDtypeStruct((n // o_cs, m, o_cs), out_dtype)
    res, cres = _call(
        body_one, name=name, grid=(m // tm, n // tn), in_specs=[a_spec, b_spec], out_specs=[o_spec], out_shape=[o_shape],
        sem=("parallel", "parallel"), args=(a, b), comm=comm, after=after)
    return res[0], cres


DW_TM = 512


def _transpose_bf16(x, name):
    t, c = x.shape
    tt = _tm()

    def body(x_ref, o_ref):
        o_ref[...] = x_ref[...].T

    return pl.pallas_call(body, name=name, grid=(t // tt,), in_specs=[pl.BlockSpec((tt, c), lambda i: (i, 0))],
                          out_specs=pl.BlockSpec((c, tt), lambda i: (0, i)),
                          out_shape=jax.ShapeDtypeStruct((c, t), BF16), compiler_params=_params(("parallel",)))(x)


def _cast_bf16(x, name):
    r, c = x.shape
    tr = _tile(r, 512, 2 * SUBLANES)

    def body(x_ref, o_ref):
        o_ref[...] = x_ref[...].astype(BF16)

    return pl.pallas_call(body, name=name, grid=(r // tr,), in_specs=[pl.BlockSpec((tr, c), lambda i: (i, 0))],
                          out_specs=pl.BlockSpec((tr, c), lambda i: (i, 0)),
                          out_shape=jax.ShapeDtypeStruct((r, c), BF16), compiler_params=_params(("parallel",)))(x)


def _cols_from_shards(wg, name):
    _, k, cs = wg.shape
    tk = _tile(k, 256, 2 * SUBLANES)

    def body(w_ref, o_ref):
        for j in range(N_DEV):
            o_ref[:, j * cs:(j + 1) * cs] = w_ref[j]

    return pl.pallas_call(body, name=name, grid=(k // tk,),
                          in_specs=[pl.BlockSpec((N_DEV, tk, cs), lambda i: (0, i, 0))],
                          out_specs=pl.BlockSpec((tk, N_DEV * cs), lambda i: (i, 0)),
                          out_shape=jax.ShapeDtypeStruct((k, N_DEV * cs), wg.dtype),
                          compiler_params=_params(("parallel",)))(wg)


def _stream(i):
    return jnp.minimum(i, 1)


def _normmod(x, g, sh, sc):
    y = x * lax.rsqrt(jnp.mean(x * x, axis=-1, keepdims=True) + EPS)
    return (y * g) * (1.0 + sc) + sh


def _mod_spec(chunk, d):
    return pl.BlockSpec((None, None, 1, d), lambda i: (_stream(i), chunk, 0, 0))


def _normmod_fwd(x, g, mod4, which, name):
    t, d = x.shape
    tm = _tm()
    ish, isc = (0, 1) if which == 0 else (3, 4)

    def body(x_ref, g_ref, sh_ref, sc_ref, o_ref):
        o_ref[...] = _normmod(x_ref[...], g_ref[...], sh_ref[...], sc_ref[...]).astype(BF16)

    row = pl.BlockSpec((tm, d), lambda i: (i, 0))
    return pl.pallas_call(body, name=name, grid=(t // tm,),
                          in_specs=[row, pl.BlockSpec((1, d), lambda i: (0, 0)), _mod_spec(ish, d), _mod_spec(isc, d)],
                          out_specs=row, out_shape=jax.ShapeDtypeStruct((t, d), BF16),
                          compiler_params=_params(("parallel",)))(x, g, mod4, mod4)


def _normmod_bwd(x, g, mod4, which, dh, dres, name, comm=None):
    t, d = x.shape
    tm = _tm()
    ish, isc = (0, 1) if which == 0 else (3, 4)

    def body(x_ref, g_ref, sh_ref, sc_ref, dh_ref, dres_ref, dx_ref, dg_ref, dsh_ref, dsc_ref):
        i = pl.program_id(0)
        _, vjp = jax.vjp(_normmod, x_ref[...], g_ref[...], sh_ref[...], sc_ref[...])
        dx, dg, dsh, dsc = vjp(dh_ref[...])
        dx_ref[...] = dres_ref[...] + dx

        @pl.when(i == 0)
        def _():
            dg_ref[...] = jnp.zeros_like(dg_ref)

        @pl.when(i <= 1)
        def _():
            dsh_ref[...] = jnp.zeros_like(dsh_ref)
            dsc_ref[...] = jnp.zeros_like(dsc_ref)

        dg_ref[...] += dg
        dsh_ref[...] += dsh
        dsc_ref[...] += dsc

    row = pl.BlockSpec((tm, d), lambda i: (i, 0))
    vec = pl.BlockSpec((1, d), lambda i: (0, 0))
    svec = pl.BlockSpec((None, 1, d), lambda i: (_stream(i), 0, 0))
    return _call(
        body, name=name, grid=(t // tm,),
        in_specs=[row, vec, _mod_spec(ish, d), _mod_spec(isc, d), row, row],
        out_specs=[row, vec, svec, svec],
        out_shape=[jax.ShapeDtypeStruct((t, d), F32), jax.ShapeDtypeStruct((1, d), F32),
                   jax.ShapeDtypeStruct((2, 1, d), F32), jax.ShapeDtypeStruct((2, 1, d), F32)],
        sem=("arbitrary",), args=(x, g, mod4, mod4, dh, dres), comm=comm)


def _gate_res_fwd(x, f, mod4, chunk, name):
    t, d = x.shape
    tm = _tm()

    def body(x_ref, f_ref, g_ref, o_ref):
        o_ref[...] = x_ref[...] + g_ref[...] * f_ref[...]

    row = pl.BlockSpec((tm, d), lambda i: (i, 0))
    return pl.pallas_call(body, name=name, grid=(t // tm,), in_specs=[row, row, _mod_spec(chunk, d)], out_specs=row,
                          out_shape=jax.ShapeDtypeStruct((t, d), F32), compiler_params=_params(("parallel",)))(x, f, mod4)


def _gate_res_bwd(dx, f, mod4, chunk, name):
    t, d = dx.shape
    tm = _tm()

    def body(dx_ref, f_ref, g_ref, o_ref, dg_ref):
        i = pl.program_id(0)
        dxv = dx_ref[...]
        o_ref[...] = (dxv * g_ref[...]).astype(BF16)

        @pl.when(i <= 1)
        def _():
            dg_ref[...] = jnp.zeros_like(dg_ref)

        dg_ref[...] += jnp.sum(dxv * f_ref[...], axis=0, keepdims=True)

    row = pl.BlockSpec((tm, d), lambda i: (i, 0))
    return pl.pallas_call(
        body, name=name, grid=(t // tm,), in_specs=[row, row, _mod_spec(chunk, d)],
        out_specs=[row, pl.BlockSpec((None, 1, d), lambda i: (_stream(i), 0, 0))],
        out_shape=[jax.ShapeDtypeStruct((t, d), BF16), jax.ShapeDtypeStruct((2, 1, d), F32)],
        compiler_params=_params(("arbitrary",)))(dx, f, mod4)


def _loss_head(x, final_g, target, name):
    t, d = x.shape
    tm = _tm()

    def loss_fn(xv, g, tgt):
        y = (xv * lax.rsqrt(jnp.mean(xv * xv, axis=-1, keepdims=True) + EPS)) * g
        err = y - tgt
        return 0.5 * jnp.sum(jnp.mean(err * err, axis=-1, keepdims=True))

    def body(x_ref, g_ref, t_ref, l_ref, dx_ref, dg_ref):
        i = pl.program_id(0)

        @pl.when(i == 0)
        def _():
            l_ref[...] = jnp.zeros_like(l_ref)
            dg_ref[...] = jnp.zeros_like(dg_ref)
            dx_ref[...] = jnp.zeros_like(dx_ref)

        @pl.when(i > 0)
        def _():
            l, (dx, dg) = jax.value_and_grad(loss_fn, argnums=(0, 1))(x_ref[...], g_ref[...], t_ref[...])
            l_ref[...] += jnp.full(l_ref.shape, l, F32)
            dx_ref[...] = dx
            dg_ref[...] += dg

    row = pl.BlockSpec((tm, d), lambda i: (i, 0))
    vec = pl.BlockSpec((1, d), lambda i: (0, 0))
    return pl.pallas_call(
        body, name=name, grid=(t // tm,),
        in_specs=[row, vec, pl.BlockSpec((tm, d), lambda i: (jnp.maximum(i - 1, 0), 0))],
        out_specs=[pl.BlockSpec((SUBLANES, LANES), lambda i: (0, 0)), row, vec],
        out_shape=[jax.ShapeDtypeStruct((SUBLANES, LANES), F32), jax.ShapeDtypeStruct((t, d), F32),
                   jax.ShapeDtypeStruct((1, d), F32)],
        compiler_params=_params(("arbitrary",)))(x, final_g, target)


def _swap_quarters(x):
    half, nf = RET_DK // 2, RET_DK // 4
    lane = lax.broadcasted_iota(jnp.int32, x.shape, 1)
    return jnp.where((lane % half) < nf, pltpu.roll(x, RET_DK - nf, 1), pltpu.roll(x, nf, 1))


def _rope(x, cos, sin):
    return x * cos + _swap_quarters(x) * sin


def _rope_t(y, cos, sin):
    return y * cos + _swap_quarters(y * sin)


def _ret_consts(d):
    c = RET_CHUNK
    ii = lax.broadcasted_iota(jnp.int32, (c, 1), 0).astype(F32)
    jj = lax.broadcasted_iota(jnp.int32, (1, c), 1).astype(F32)
    fwd = d == 0
    sgn = jnp.where(fwd, 1.0, -1.0).astype(F32)
    pos = jnp.where(fwd, ii, c - 1.0 - ii)
    return sgn * (ii - jj), pos


def _ret_step(lgt, state, q, k, v, diff, pos):
    c = float(RET_CHUNK)
    lg = -(jnp.maximum(-lgt, 0.0) + jnp.log1p(jnp.exp(-jnp.abs(lgt))))
    lower = diff >= 0
    decay = jnp.where(lower, jnp.exp(jnp.where(lower, diff, 0.0) * lg), 0.0)
    xi = jnp.exp((pos + 1.0) * lg)
    zeta = jnp.exp((c - 1.0 - pos) * lg)
    gch = jnp.exp(c * lg)
    inner = dot_nt(q, k) * decay
    out = dot_nn(inner, v) + dot_nn(q, state) * xi
    new_state = state * gch + dot_tn(k * zeta, v)
    return out, new_state


def _chunk_order():
    nc, nch = CTX_LEN // RET_CHUNK, _t_rows() // RET_CHUNK
    fwd = list(range(nch))
    bwd = list(range(nc - 1, -1, -1)) + list(range(nch - 1, nc - 1, -1))
    return jnp.asarray(np.array([fwd, bwd], np.int32))


def _ret_fwd(p, cos, sin, decay, order, name):
    t = p.shape[0]
    c, dk, dv, nh = RET_CHUNK, RET_DK, RET_DV, RET_HEADS
    nch = t // c
    off = _offsets()
    wqk, wv = nh * dk, nh * dv
    assert off["q"] % wqk == 0 and off["k"] % wqk == 0 and off["v"] % wv == 0
    qb, kb, vb = off["q"] // wqk, off["k"] // wqk, off["v"] // wv
    scale = RET_DK ** -0.5

    def body(ord_ref, dec_ref, q_ref, k_ref, v_ref, cos_ref, sin_ref, o_ref, st_ref, state):
        d, s = pl.program_id(0), pl.program_id(1)

        @pl.when(s == 0)
        def _():
            state[...] = jnp.zeros_like(state)

        diff, pos = _ret_consts(d)
        cosv, sinv = cos_ref[...], sin_ref[...]
        for h in range(nh):
            st = state[h]
            st_ref[h] = st
            lgt = jnp.full((1, 1), dec_ref[d, h], F32)
            q = _rope(q_ref[:, h * dk:(h + 1) * dk], cosv, sinv) * scale
            k = _rope(k_ref[:, h * dk:(h + 1) * dk], cosv, sinv)
            out, ns = _ret_step(lgt, st, q, k, v_ref[:, h * dv:(h + 1) * dv], diff, pos)
            o_ref[:, h * dv:(h + 1) * dv] = out
            state[h] = ns

    grid_spec = pltpu.PrefetchScalarGridSpec(
        num_scalar_prefetch=1, grid=(2, nch),
        in_specs=[pl.BlockSpec(memory_space=pltpu.SMEM),
                  pl.BlockSpec((c, wqk), lambda d, s, o: (o[d, s], qb)),
                  pl.BlockSpec((c, wqk), lambda d, s, o: (o[d, s], kb)),
                  pl.BlockSpec((c, wv), lambda d, s, o: (o[d, s], vb)),
                  pl.BlockSpec((c, dk), lambda d, s, o: (o[d, s], 0)),
                  pl.BlockSpec((c, dk), lambda d, s, o: (o[d, s], 0))],
        out_specs=[pl.BlockSpec((None, c, wv), lambda d, s, o: (d, o[d, s], 0)),
                   pl.BlockSpec((None, nh, None, dk, dv), lambda d, s, o: (d, 0, s, 0, 0))],
        scratch_shapes=[pltpu.VMEM((nh, dk, dv), F32)])
    return pl.pallas_call(
        body, name=name, grid_spec=grid_spec,
        out_shape=[jax.ShapeDtypeStruct((2, t, wv), F32), jax.ShapeDtypeStruct((2, nh, nch, dk, dv), F32)],
        compiler_params=_params(("arbitrary", "arbitrary")))(order, decay, p, p, p, cos, sin)


def _ret_bwd(p, cos, sin, decay, order, states, do, name):
    t = p.shape[0]
    c, dk, dv, nh = RET_CHUNK, RET_DK, RET_DV, RET_HEADS
    nch = t // c
    off = _offsets()
    wqk, wv = nh * dk, nh * dv
    qb, kb, vb = off["q"] // wqk, off["k"] // wqk, off["v"] // wv
    scale = RET_DK ** -0.5

    def body(ord_ref, dec_ref, q_ref, k_ref, v_ref, cos_ref, sin_ref, st_ref, do_ref,
             dq_ref, dk_ref, dv_ref, dd_ref, dstate):
        d, s = pl.program_id(0), pl.program_id(1)

        @pl.when(s == 0)
        def _():
            dstate[...] = jnp.zeros_like(dstate)
            dd_ref[...] = jnp.zeros_like(dd_ref)

        diff, pos = _ret_consts(d)
        cosv, sinv = cos_ref[...], sin_ref[...]
        for h in range(nh):
            qk, vv = slice(h * dk, (h + 1) * dk), slice(h * dv, (h + 1) * dv)
            lgt = jnp.full((1, 1), dec_ref[d, h], F32)
            q = _rope(q_ref[:, qk], cosv, sinv) * scale
            k = _rope(k_ref[:, qk], cosv, sinv)
            _, vjp = jax.vjp(lambda a, b, cq, ck, cv: _ret_step(a, b, cq, ck, cv, diff, pos),
                             lgt, st_ref[h], q, k, v_ref[:, vv])
            dlgt, dst, dq, dkk, dvv = vjp((do_ref[:, vv], dstate[h]))
            dstate[h] = dst
            dq_ref[:, qk] = _rope_t(dq * scale, cosv, sinv)
            dk_ref[:, qk] = _rope_t(dkk, cosv, sinv)
            dv_ref[:, vv] = dvv
            dd_ref[h] += jnp.broadcast_to(dlgt, (SUBLANES, LANES))

    rev = lambda o, d, s: o[d, nch - 1 - s]
    grid_spec = pltpu.PrefetchScalarGridSpec(
        num_scalar_prefetch=1, grid=(2, nch),
        in_specs=[pl.BlockSpec(memory_space=pltpu.SMEM),
                  pl.BlockSpec((c, wqk), lambda d, s, o: (rev(o, d, s), qb)),
                  pl.BlockSpec((c, wqk), lambda d, s, o: (rev(o, d, s), kb)),
                  pl.BlockSpec((c, wv), lambda d, s, o: (rev(o, d, s), vb)),
                  pl.BlockSpec((c, dk), lambda d, s, o: (rev(o, d, s), 0)),
                  pl.BlockSpec((c, dk), lambda d, s, o: (rev(o, d, s), 0)),
                  pl.BlockSpec((None, nh, None, dk, dv), lambda d, s, o: (d, 0, nch - 1 - s, 0, 0)),
                  pl.BlockSpec((c, wv), lambda d, s, o: (rev(o, d, s), 0))],
        out_specs=[pl.BlockSpec((None, c, wqk), lambda d, s, o: (d, rev(o, d, s), 0)),
                   pl.BlockSpec((None, c, wqk), lambda d, s, o: (d, rev(o, d, s), 0)),
                   pl.BlockSpec((None, c, wv), lambda d, s, o: (d, rev(o, d, s), 0)),
                   pl.BlockSpec((None, nh, SUBLANES, LANES), lambda d, s, o: (d, 0, 0, 0))],
        scratch_shapes=[pltpu.VMEM((nh, dk, dv), F32)])
    return pl.pallas_call(
        body, name=name, grid_spec=grid_spec,
        out_shape=[jax.ShapeDtypeStruct((2, t, wqk), F32), jax.ShapeDtypeStruct((2, t, wqk), F32),
                   jax.ShapeDtypeStruct((2, t, wv), F32), jax.ShapeDtypeStruct((2, nh, SUBLANES, LANES), F32)],
        compiler_params=_params(("arbitrary", "arbitrary")))(order, decay, p, p, p, cos, sin, states, do)


def _ggn_head(of, ob, gate, g):
    o = of + ob
    mu = jnp.mean(o, axis=-1, keepdims=True)
    var = jnp.mean(jnp.square(o - mu), axis=-1, keepdims=True)
    return ((o - mu) * lax.rsqrt(var + EPS) * g) * _silu(gate)


def _ggn_fwd(o2, p, gn_g, name):
    t = p.shape[0]
    tm, w, dv = _tm(), _ret_w(), RET_DV
    gb = _offsets()["g"] // w

    def body(o_ref, gate_ref, g_ref, out_ref):
        for h in range(RET_HEADS):
            sl = slice(h * dv, (h + 1) * dv)
            out_ref[:, sl] = _ggn_head(o_ref[0, :, sl], o_ref[1, :, sl], gate_ref[:, sl], g_ref[:, sl]).astype(BF16)

    return pl.pallas_call(
        body, name=name, grid=(t // tm,),
        in_specs=[pl.BlockSpec((2, tm, w), lambda i: (0, i, 0)), pl.BlockSpec((tm, w), lambda i: (i, gb)),
                  pl.BlockSpec((1, w), lambda i: (0, 0))],
        out_specs=pl.BlockSpec((tm, w), lambda i: (i, 0)), out_shape=jax.ShapeDtypeStruct((t, w), BF16),
        compiler_params=_params(("parallel",)))(o2, p, gn_g)


def _ggn_bwd(o2, p, gn_g, dmix, name):
    t = p.shape[0]
    tm, w, dv = _tm(), _ret_w(), RET_DV
    gb = _offsets()["g"] // w

    def body(o_ref, gate_ref, g_ref, dy_ref, do_ref, dgate_ref, dg_ref):
        i = pl.program_id(0)

        @pl.when(i == 0)
        def _():
            dg_ref[...] = jnp.zeros_like(dg_ref)

        for h in range(RET_HEADS):
            sl = slice(h * dv, (h + 1) * dv)
            _, vjp = jax.vjp(_ggn_head, o_ref[0, :, sl], o_ref[1, :, sl], gate_ref[:, sl], g_ref[:, sl])
            do, _, dgate, dg = vjp(dy_ref[:, sl])
            do_ref[:, sl] = do
            dgate_ref[:, sl] = dgate
            dg_ref[:, sl] += dg

    row = pl.BlockSpec((tm, w), lambda i: (i, 0))
    return pl.pallas_call(
        body, name=name, grid=(t // tm,),
        in_specs=[pl.BlockSpec((2, tm, w), lambda i: (0, i, 0)), pl.BlockSpec((tm, w), lambda i: (i, gb)),
                  pl.BlockSpec((1, w), lambda i: (0, 0)), row],
        out_specs=[row, row, pl.BlockSpec((1, w), lambda i: (0, 0))],
        out_shape=[jax.ShapeDtypeStruct((t, w), F32), jax.ShapeDtypeStruct((t, w), F32),
                   jax.ShapeDtypeStruct((1, w), F32)],
        compiler_params=_params(("arbitrary",)))(o2, p, gn_g, dmix)


def _halo(k):
    return SUBLANES * ((k // 2 + SUBLANES - 1) // SUBLANES)


def _halo_specs(width, colblock, h, tm):
    r = tm // h
    return [pl.BlockSpec((h, width), lambda i, *_: (jnp.maximum(i * r - 1, 0), colblock(*_))),
            pl.BlockSpec((tm, width), lambda i, *_: (i, colblock(*_))),
            pl.BlockSpec((h, width), lambda i, *_: (jnp.minimum((i + 1) * r, (_t_rows() // h) - 1), colblock(*_)))]


def _fill_ext(ext_ref, prev, cur, nxt, i, h, tm):
    nt = _t_rows() // tm
    ext_ref[0:h, :] = jnp.where(i >= 2, prev, 0.0)
    ext_ref[h:h + tm, :] = cur
    ext_ref[h + tm:h + tm + h, :] = jnp.where((i >= 1) & (i <= nt - 2), nxt, 0.0)


def _corr(ext_ref, w_ref, k, h, tm, flip):
    pad = k // 2
    acc = None
    for kk in range(k):
        o = h + (pad - kk if flip else kk - pad)
        term = w_ref[kk:kk + 1, :] * ext_ref[o:o + tm, :]
        acc = term if acc is None else acc + term
    return acc


def _conv_post(u2, ln_g, ln_b, pw):
    mu = jnp.mean(u2, axis=-1, keepdims=True)
    var = jnp.mean(jnp.square(u2 - mu), axis=-1, keepdims=True)
    y = (u2 - mu) * lax.rsqrt(var + EPS) * ln_g + ln_b
    return dot_nn(_silu(y), pw)


def _conv_fwd(p, dw_w, dw_b, ln_g, ln_b, pw, name):
    t = p.shape[0]
    tm, w, k = _tm(), CONV_W, CONV_K
    h = _halo(k)
    off = _offsets()
    ab, bb = off["a"] // w, off["b"] // w

    def body(ap, ac, an, bp, bc, bn, w_ref, b_ref, g_ref, beta_ref, pw_ref, u2_ref, out_ref, ext):
        i = pl.program_id(0)
        glu = lambda a, b: a * _sigmoid(b)
        _fill_ext(ext, glu(ap[...], bp[...]), glu(ac[...], bc[...]), glu(an[...], bn[...]), i, h, tm)
        u2 = _corr(ext, w_ref, k, h, tm, False) + b_ref[...]
        u2_ref[...] = u2
        out_ref[...] = _conv_post(u2, g_ref[...], beta_ref[...], pw_ref[...]).astype(BF16)

    vec = pl.BlockSpec((1, w), lambda i: (0, 0))
    row = pl.BlockSpec((tm, w), lambda i: (i, 0))
    return pl.pallas_call(
        body, name=name, grid=(t // tm,),
        in_specs=_halo_specs(w, lambda: ab, h, tm) + _halo_specs(w, lambda: bb, h, tm)
        + [pl.BlockSpec((k, w), lambda i: (0, 0)), vec, vec, vec, pl.BlockSpec((w, w), lambda i: (0, 0))],
        out_specs=[row, row],
        out_shape=[jax.ShapeDtypeStruct((t, w), F32), jax.ShapeDtypeStruct((t, w), BF16)],
        scratch_shapes=[pltpu.VMEM((tm + 2 * h, w), F32)],
        compiler_params=_params(("parallel",)))(p, p, p, p, p, p, dw_w, dw_b, ln_g, ln_b, pw)


def _conv_bwd1(u2, dmix, ln_g, ln_b, pw, name):
    t = u2.shape[0]
    tm, w = _tm(), CONV_W
    cb = _ret_w() // w

    def body(u2_ref, dy_ref, g_ref, beta_ref, pw_ref, du2_ref, dg_ref, db_ref, dpw_ref):
        i = pl.program_id(0)

        @pl.when(i == 0)
        def _():
            dg_ref[...] = jnp.zeros_like(dg_ref)
            db_ref[...] = jnp.zeros_like(db_ref)
            dpw_ref[...] = jnp.zeros_like(dpw_ref)

        _, vjp = jax.vjp(_conv_post, u2_ref[...], g_ref[...], beta_ref[...], pw_ref[...])
        du2, dg, db, dpw = vjp(dy_ref[...])
        du2_ref[...] = du2
        dg_ref[...] += dg
        db_ref[...] += db
        dpw_ref[...] += dpw

    vec = pl.BlockSpec((1, w), lambda i: (0, 0))
    row = pl.BlockSpec((tm, w), lambda i: (i, 0))
    mat = pl.BlockSpec((w, w), lambda i: (0, 0))
    return pl.pallas_call(
        body, name=name, grid=(t // tm,),
        in_specs=[row, pl.BlockSpec((tm, w), lambda i: (i, cb)), vec, vec, mat],
        out_specs=[row, vec, vec, mat],
        out_shape=[jax.ShapeDtypeStruct((t, w), F32), jax.ShapeDtypeStruct((1, w), F32),
                   jax.ShapeDtypeStruct((1, w), F32), jax.ShapeDtypeStruct((w, w), F32)],
        compiler_params=_params(("arbitrary",)))(u2, dmix, ln_g, ln_b, pw)


def _conv_bwd2(du2, p, dw_w, name, comm=None):
    t = p.shape[0]
    tm, w, k = _tm(), CONV_W, CONV_K
    h = _halo(k)
    pad = k // 2
    off = _offsets()
    ab, bb = off["a"] // w, off["b"] // w

    def body(dp, dc, dn, ap, ac, an, bp, bc, bn, w_ref, da_ref, db_ref, dw_ref, dbias_ref, ext_d, ext_u):
        i = pl.program_id(0)

        @pl.when(i == 0)
        def _():
            dw_ref[...] = jnp.zeros_like(dw_ref)
            dbias_ref[...] = jnp.zeros_like(dbias_ref)

        glu = lambda a, b: a * _sigmoid(b)
        a, b, d = ac[...], bc[...], dc[...]
        _fill_ext(ext_d, dp[...], d, dn[...], i, h, tm)
        _fill_ext(ext_u, glu(ap[...], bp[...]), glu(a, b), glu(an[...], bn[...]), i, h, tm)
        du = _corr(ext_d, w_ref, k, h, tm, True)
        sg = _sigmoid(b)
        da_ref[...] = du * sg
        db_ref[...] = du * a * sg * (1.0 - sg)
        dbias_ref[...] += jnp.sum(d, axis=0, keepdims=True)
        for kk in range(k):
            o = h + kk - pad
            dw_ref[kk:kk + 1, :] += jnp.sum(d * ext_u[o:o + tm, :], axis=0, keepdims=True)

    vec = pl.BlockSpec((1, w), lambda i: (0, 0))
    row = pl.BlockSpec((tm, w), lambda i: (i, 0))
    kw = pl.BlockSpec((k, w), lambda i: (0, 0))
    return _call(
        body, name=name, grid=(t // tm,),
        in_specs=_halo_specs(w, lambda: 0, h, tm) + _halo_specs(w, lambda: ab, h, tm)
        + _halo_specs(w, lambda: bb, h, tm) + [kw],
        out_specs=[row, row, kw, vec],
        out_shape=[jax.ShapeDtypeStruct((t, w), F32), jax.ShapeDtypeStruct((t, w), F32),
                   jax.ShapeDtypeStruct((k, w), F32), jax.ShapeDtypeStruct((1, w), F32)],
        scratch=[pltpu.VMEM((tm + 2 * h, w), F32), pltpu.VMEM((tm + 2 * h, w), F32)],
        sem=("arbitrary",), args=(du2, du2, du2, p, p, p, p, p, p, dw_w), comm=comm)


def _ffn_tc():
    return _tile(D_FF, 512, LANES)


def _ffn_act_fwd(u, dw_w, dw_b, name):
    t = u.shape[0]
    tm, k, tc = _tm(), FFN_K, _ffn_tc()
    h = _halo(k)
    nj = D_FF // tc

    def body(vp, vc, vn, gp, gc, gn, wv, wg, bv, bg, out_ref, ext_v, ext_g):
        i = pl.program_id(0)
        _fill_ext(ext_v, vp[...], vc[...], vn[...], i, h, tm)
        _fill_ext(ext_g, gp[...], gc[...], gn[...], i, h, tm)
        for r0 in range(0, tm, ROW_CHUNK):
            val = _corr(ext_v, wv, k, h + r0, ROW_CHUNK, False) + bv[...]
            gate = _corr(ext_g, wg, k, h + r0, ROW_CHUNK, False) + bg[...]
            out_ref[r0:r0 + ROW_CHUNK, :] = (_silu(gate) * val).astype(BF16)

    wspec = lambda s: pl.BlockSpec((k, tc), lambda i, j: (0, j + s))
    bspec = lambda s: pl.BlockSpec((1, tc), lambda i, j: (0, j + s))
    return pl.pallas_call(
        body, name=name, grid=(t // tm, nj),
        in_specs=_halo_specs(tc, lambda j: j, h, tm) + _halo_specs(tc, lambda j: j + nj, h, tm)
        + [wspec(0), wspec(nj), bspec(0), bspec(nj)],
        out_specs=pl.BlockSpec((tm, tc), lambda i, j: (i, j)),
        out_shape=jax.ShapeDtypeStruct((t, D_FF), BF16),
        scratch_shapes=[pltpu.VMEM((tm + 2 * h, tc), F32), pltpu.VMEM((tm + 2 * h, tc), F32)],
        compiler_params=_params(("parallel", "parallel")))(u, u, u, u, u, u, dw_w, dw_w, dw_b, dw_b)


def _ffn_act_bwd1(u, da, dw_w, dw_b, name):
    t = u.shape[0]
    tm, k, tc = _tm(), FFN_K, _ffn_tc()
    h = _halo(k)
    nj = D_FF // tc

    def body(vp, vc, vn, gp, gc, gn, wv, wg, bv, bg, da_ref, dv_ref, dg_ref, ext_v, ext_g):
        i = pl.program_id(0)
        _fill_ext(ext_v, vp[...], vc[...], vn[...], i, h, tm)
        _fill_ext(ext_g, gp[...], gc[...], gn[...], i, h, tm)
        for r0 in range(0, tm, ROW_CHUNK):
            rows = slice(r0, r0 + ROW_CHUNK)
            val = _corr(ext_v, wv, k, h + r0, ROW_CHUNK, False) + bv[...]
            gate = _corr(ext_g, wg, k, h + r0, ROW_CHUNK, False) + bg[...]
            _, vjp = jax.vjp(lambda a, b: _silu(b) * a, val, gate)
            dval, dgate = vjp(da_ref[rows, :])
            dv_ref[rows, :] = dval
            dg_ref[rows, :] = dgate

    wspec = lambda s: pl.BlockSpec((k, tc), lambda i, j: (0, j + s))
    bspec = lambda s: pl.BlockSpec((1, tc), lambda i, j: (0, j + s))
    dc = pl.pallas_call(
        body, name=name, grid=(t // tm, nj),
        in_specs=_halo_specs(tc, lambda j: j, h, tm) + _halo_specs(tc, lambda j: j + nj, h, tm)
        + [wspec(0), wspec(nj), bspec(0), bspec(nj), pl.BlockSpec((tm, tc), lambda i, j: (i, j))],
        out_specs=[pl.BlockSpec((tm, tc), lambda i, j: (i, j)), pl.BlockSpec((tm, tc), lambda i, j: (i, j))],
        out_shape=[jax.ShapeDtypeStruct((t, D_FF), F32), jax.ShapeDtypeStruct((t, D_FF), F32)],
        scratch_shapes=[pltpu.VMEM((tm + 2 * h, tc), F32), pltpu.VMEM((tm + 2 * h, tc), F32)],
        compiler_params=_params(("parallel", "parallel")))(u, u, u, u, u, u, dw_w, dw_w, dw_b, dw_b, da)
    return dc


def _dwconv_bwd(dcv, dcg, u, dw_w, name):
    t = u.shape[0]
    tm, k, tc = _tm(), FFN_K, _ffn_tc()
    h = _halo(k)
    pad = k // 2
    nj = D_FF // tc

    def body(vp, vc, vn, gp, gc, gn, up, uc, un, w_ref, du_ref, dw_ref, dbias_ref, ext_d, ext_u):
        jj, i = pl.program_id(0), pl.program_id(1)

        @pl.when(i == 0)
        def _():
            dw_ref[...] = jnp.zeros_like(dw_ref)
            dbias_ref[...] = jnp.zeros_like(dbias_ref)

        @pl.when(jj < nj)
        def _():
            _fill_ext(ext_d, vp[...], vc[...], vn[...], i, h, tm)

        @pl.when(jj >= nj)
        def _():
            _fill_ext(ext_d, gp[...], gc[...], gn[...], i, h, tm)

        _fill_ext(ext_u, up[...], uc[...], un[...], i, h, tm)
        acc_b = jnp.zeros((ROW_CHUNK, tc), F32)
        acc_w = [jnp.zeros((ROW_CHUNK, tc), F32) for _ in range(k)]
        for r0 in range(0, tm, ROW_CHUNK):
            d = ext_d[h + r0:h + r0 + ROW_CHUNK, :]
            du_ref[r0:r0 + ROW_CHUNK, :] = _corr(ext_d, w_ref, k, h + r0, ROW_CHUNK, True).astype(BF16)
            acc_b = acc_b + d
            for kk in range(k):
                o = h + r0 + kk - pad
                acc_w[kk] = acc_w[kk] + d * ext_u[o:o + ROW_CHUNK, :]
        dbias_ref[...] += jnp.sum(acc_b, axis=0, keepdims=True)
        for kk in range(k):
            dw_ref[kk:kk + 1, :] += jnp.sum(acc_w[kk], axis=0, keepdims=True)

    def hs(cb, live):
        r = tm // h
        row = lambda j, i: jnp.where(live(j), i, 0)
        return [pl.BlockSpec((h, tc), lambda j, i: (jnp.maximum(row(j, i) * r - 1, 0), cb(j))),
                pl.BlockSpec((tm, tc), lambda j, i: (row(j, i), cb(j))),
                pl.BlockSpec((h, tc), lambda j, i: (jnp.minimum((row(j, i) + 1) * r, (_t_rows() // h) - 1), cb(j)))]

    return pl.pallas_call(
        body, name=name, grid=(2 * nj, t // tm),
        in_specs=hs(lambda j: jnp.minimum(j, nj - 1), lambda j: j < nj)
        + hs(lambda j: jnp.maximum(j - nj, 0), lambda j: j >= nj)
        + hs(lambda j: j, lambda j: True) + [pl.BlockSpec((k, tc), lambda j, i: (0, j))],
        out_specs=[pl.BlockSpec((tm, tc), lambda j, i: (i, j)), pl.BlockSpec((k, tc), lambda j, i: (0, j)),
                   pl.BlockSpec((1, tc), lambda j, i: (0, j))],
        out_shape=[jax.ShapeDtypeStruct((t, 2 * D_FF), BF16), jax.ShapeDtypeStruct((k, 2 * D_FF), F32),
                   jax.ShapeDtypeStruct((1, 2 * D_FF), F32)],
        scratch_shapes=[pltpu.VMEM((tm + 2 * h, tc), F32), pltpu.VMEM((tm + 2 * h, tc), F32)],
        compiler_params=_params(("parallel", "arbitrary")))(dcv, dcv, dcv, dcg, dcg, dcg, u, u, u, dw_w)


def _na_geometry(rq):
    ncb = CTX_LEN // GRID_W
    rows_n = SEQ // GRID_W
    r = jnp.maximum(rq - ncb, 0)
    kstart = jnp.clip(r - NA_ROWS // 2, 0, rows_n - NA_ROWS)
    base = kstart - r + NA_ROWS - 1
    return rq >= ncb, kstart, base


def _na_core(q, kl, vl, kc, vc, bias, mask):
    qs = q * (NA_DH ** -0.5)
    s_l = jnp.where(mask, dot_nt(qs, kl) + bias, NEG)
    s_c = dot_nt(qs, kc)
    m = lax.stop_gradient(jnp.maximum(jnp.max(s_l, axis=1, keepdims=True), jnp.max(s_c, axis=1, keepdims=True)))
    e_l, e_c = jnp.exp(s_l - m), jnp.exp(s_c - m)
    inv = 1.0 / (jnp.sum(e_l, axis=1, keepdims=True) + jnp.sum(e_c, axis=1, keepdims=True))
    return dot_nn(e_l * inv, vl) + dot_nn(e_c * inv, vc)


def _na_mask(is_lat):
    nl = NA_ROWS * GRID_W
    q = lax.broadcasted_iota(jnp.int32, (GRID_W, nl), 0)
    w = lax.broadcasted_iota(jnp.int32, (GRID_W, nl), 1) % GRID_W
    cs = jnp.clip(q - NA_COLS // 2, 0, GRID_W - NA_COLS)
    return (w >= cs) & (w < cs + NA_COLS) & is_lat


def _na_bias(rb_ref):
    assert 2 * GRID_W == LANES
    lane = lax.broadcasted_iota(jnp.int32, (GRID_W, LANES), 1)
    tiles = []
    for kp in range(NA_ROWS // 2):
        ev = jnp.broadcast_to(rb_ref[2 * kp:2 * kp + 1, :], (GRID_W, LANES))
        od = jnp.broadcast_to(rb_ref[2 * kp + 1:2 * kp + 2, :], (GRID_W, LANES))
        ev = pltpu.roll(ev, LANES - (NA_COLS - 1), 1, stride=1, stride_axis=0)
        od = pltpu.roll(od, LANES - (NA_COLS - 1) - GRID_W, 1, stride=1, stride_axis=0)
        tiles.append(jnp.where(lane < GRID_W, ev, od))
    return jnp.concatenate(tiles, axis=1)


def _na_dbias(dbias, drb_ref):
    qi = lax.broadcasted_iota(jnp.int32, (GRID_W, GRID_W), 0)
    qj = lax.broadcasted_iota(jnp.int32, (GRID_W, GRID_W), 1)
    flip = (qi + qj == GRID_W - 1).astype(F32)
    rev = lax.dot_general(flip, dbias, (((1,), (0,)), ((), ())), precision=lax.Precision.HIGHEST,
                          preferred_element_type=F32)
    lane = lax.broadcasted_iota(jnp.int32, (GRID_W, LANES), 1)
    s_ev = LANES - (GRID_W - NA_COLS)
    for kp in range(NA_ROWS // 2):
        tile = rev[:, kp * LANES:(kp + 1) * LANES]
        ev = pltpu.roll(jnp.where(lane < GRID_W, tile, 0.0), s_ev, 1, stride=1, stride_axis=0)
        od = pltpu.roll(jnp.where(lane >= GRID_W, tile, 0.0), s_ev - GRID_W, 1, stride=1, stride_axis=0)
        drb_ref[2 * kp:2 * kp + 1, :] += jnp.sum(ev, axis=0, keepdims=True)
        drb_ref[2 * kp + 1:2 * kp + 2, :] += jnp.sum(od, axis=0, keepdims=True)


def _na_hps():
    return 2 if NA_HEADS % 2 == 0 else 1


def _na_specs(p_offsets):
    t = _t_rows()
    hps = _na_hps()
    wd = hps * NA_DH
    assert all(p_offsets[n] % wd == 0 for n in ("nq", "nk", "nv"))
    qb, kb, vb = (p_offsets[n] // wd for n in ("nq", "nk", "nv"))
    return [pl.BlockSpec((GRID_W, wd), lambda h, r: (r, qb + h)),
            pl.BlockSpec((t, wd), lambda h, r: (0, kb + h)),
            pl.BlockSpec((t, wd), lambda h, r: (0, vb + h)),
            pl.BlockSpec((hps, None, NA_ROWS, LANES), lambda h, r: (h, _na_geometry(r)[2], 0, 0))]


def _na_fwd(p, rb, name, comm=None):
    t = p.shape[0]
    dh, nl = NA_DH, NA_ROWS * GRID_W

    hps = _na_hps()

    def body(q_ref, k_ref, v_ref, rb_ref, out_ref):
        rq = pl.program_id(1)
        is_lat, kstart, _ = _na_geometry(rq)
        start = pl.multiple_of(CTX_LEN + kstart * GRID_W, GRID_W)
        mask = _na_mask(is_lat)
        for hh in range(hps):
            cols = slice(hh * dh, (hh + 1) * dh)
            out = _na_core(q_ref[:, cols], k_ref[pl.ds(start, nl), cols], v_ref[pl.ds(start, nl), cols],
                           k_ref[0:CTX_LEN, cols], v_ref[0:CTX_LEN, cols], _na_bias(rb_ref.at[hh]), mask)
            out_ref[:, cols] = out.astype(BF16)

    res, cres = _call(
        body, name=name, grid=(NA_HEADS // hps, t // GRID_W), in_specs=_na_specs(_offsets()),
        out_specs=[pl.BlockSpec((GRID_W, hps * dh), lambda h, r: (r, h))],
        out_shape=[jax.ShapeDtypeStruct((t, _na_w()), BF16)],
        sem=("parallel", "arbitrary"), args=(p, p, p, rb), comm=comm)
    return res[0], cres


def _na_bwd(p, rb, dmix, name, comm=None):
    t = p.shape[0]
    dh, nl = NA_DH, NA_ROWS * GRID_W

    hps = _na_hps()
    wd = hps * dh
    assert ((_ret_w() + CONV_W) // dh) % hps == 0
    ob = (_ret_w() + CONV_W) // wd

    def body(q_ref, k_ref, v_ref, rb_ref, dy_ref, dq_ref, dk_ref, dv_ref, drb_ref):
        rq = pl.program_id(1)
        is_lat, kstart, base = _na_geometry(rq)
        _, _, prev_base = _na_geometry(rq - 1)
        start = pl.multiple_of(CTX_LEN + kstart * GRID_W, GRID_W)

        @pl.when(rq == 0)
        def _():
            dk_ref[...] = jnp.zeros_like(dk_ref)
            dv_ref[...] = jnp.zeros_like(dv_ref)

        @pl.when((rq == 0) | (base != prev_base))
        def _():
            drb_ref[...] = jnp.zeros_like(drb_ref)

        mask = _na_mask(is_lat)
        for hh in range(hps):
            cols = slice(hh * dh, (hh + 1) * dh)
            _, vjp = jax.vjp(lambda *a: _na_core(*a, mask), q_ref[:, cols], k_ref[pl.ds(start, nl), cols],
                             v_ref[pl.ds(start, nl), cols], k_ref[0:CTX_LEN, cols], v_ref[0:CTX_LEN, cols],
                             _na_bias(rb_ref.at[hh]))
            dq, dkl, dvl, dkc, dvc, dbias = vjp(dy_ref[:, cols])
            dq_ref[:, cols] = dq
            dk_ref[pl.ds(start, nl), cols] += dkl
            dv_ref[pl.ds(start, nl), cols] += dvl
            dk_ref[0:CTX_LEN, cols] += dkc
            dv_ref[0:CTX_LEN, cols] += dvc
            _na_dbias(dbias, drb_ref.at[hh])

    return _call(
        body, name=name, grid=(NA_HEADS // hps, t // GRID_W),
        in_specs=_na_specs(_offsets()) + [pl.BlockSpec((GRID_W, wd), lambda h, r: (r, ob + h))],
        out_specs=[pl.BlockSpec((GRID_W, wd), lambda h, r: (r, h)), pl.BlockSpec((t, wd), lambda h, r: (0, h)),
                   pl.BlockSpec((t, wd), lambda h, r: (0, h)),
                   pl.BlockSpec((hps, None, NA_ROWS, LANES), lambda h, r: (h, _na_geometry(r)[2], 0, 0))],
        out_shape=[jax.ShapeDtypeStruct((t, _na_w()), F32), jax.ShapeDtypeStruct((t, _na_w()), F32),
                   jax.ShapeDtypeStruct((t, _na_w()), F32),
                   jax.ShapeDtypeStruct((NA_HEADS, NA_ROWS, NA_ROWS, LANES), F32)],
        sem=("parallel", "arbitrary"), args=(p, p, p, rb, dmix), comm=comm)


def _rpb_select():
    sel = np.zeros((2 * NA_ROWS - 1, NA_ROWS * NA_ROWS), np.float32)
    for b in range(NA_ROWS):
        for kh in range(NA_ROWS):
            sel[b + kh, b * NA_ROWS + kh] = 1.0
    return jnp.asarray(sel)


def _rpb_rows(rpb):
    pad = jnp.pad(rpb, ((0, 0), (0, 0), (0, LANES - (2 * NA_COLS - 1))))
    rows = jnp.einsum("rk,hrc->hkc", _rpb_select(), pad, precision=lax.Precision.HIGHEST)
    return rows.reshape(NA_HEADS, NA_ROWS, NA_ROWS, LANES)


def _rpb_rows_t(drb):
    flat = drb.reshape(NA_HEADS, NA_ROWS * NA_ROWS, LANES)
    out = jnp.einsum("rk,hkc->hrc", _rpb_select(), flat, precision=lax.Precision.HIGHEST)
    return out[:, :, :2 * NA_COLS - 1]


def _assemble_dp(dqr, dkr, dvr, dgate, da, db, dnq, dnk, dnv, name):
    t = dgate.shape[0]
    tm = _tm()
    off = _offsets()
    sizes = dict(q=_ret_qk_w(), k=_ret_qk_w(), v=_ret_w(), g=_ret_w(), a=CONV_W, b=CONV_W, nq=_na_w(), nk=_na_w(), nv=_na_w())

    def body(q_ref, k_ref, v_ref, g_ref, a_ref, b_ref, nq_ref, nk_ref, nv_ref, o_ref):
        def put(n, val):
            o_ref[:, off[n]:off[n] + sizes[n]] = val.astype(BF16)

        put("q", q_ref[0] + q_ref[1])
        put("k", k_ref[0] + k_ref[1])
        put("v", v_ref[0] + v_ref[1])
        put("g", g_ref[...])
        put("a", a_ref[...])
        put("b", b_ref[...])
        put("nq", nq_ref[...])
        put("nk", nk_ref[...])
        put("nv", nv_ref[...])

    two = lambda w: pl.BlockSpec((2, tm, w), lambda i: (0, i, 0))
    one = lambda w: pl.BlockSpec((tm, w), lambda i: (i, 0))
    return pl.pallas_call(
        body, name=name, grid=(t // tm,),
        in_specs=[two(sizes["q"]), two(sizes["k"]), two(sizes["v"]), one(sizes["g"]), one(CONV_W), one(CONV_W),
                  one(_na_w()), one(_na_w()), one(_na_w())],
        out_specs=one(_d_in()), out_shape=jax.ShapeDtypeStruct((t, _d_in()), BF16),
        compiler_params=_params(("parallel",)))(dqr, dkr, dvr, dgate, da, db, dnq, dnk, dnv)


def _adamw(w, m, v, gs, name, comm=None):
    nl, r, c = w.shape
    stacked = not isinstance(gs, (list, tuple))
    if stacked:
        gs = [gs]
    assert stacked or len(gs) == nl
    g_n = gs[0].shape[-3]
    block_bytes = 2 * 1024 * 1024
    rows = min(block_bytes // (4 * c), block_bytes // (g_n * c * gs[0].dtype.itemsize))
    tr = _tile(r, max(2 * SUBLANES, rows // (2 * SUBLANES) * (2 * SUBLANES)), 2 * SUBLANES)
    nt = r // tr
    c1 = 1.0 - ADAM_B1 ** ADAM_STEP
    c2 = 1.0 - ADAM_B2 ** ADAM_STEP

    def body(w_ref, m_ref, v_ref, *rest):
        g_refs, (go_ref, d_ref, mo_ref, vo_ref) = rest[:len(gs)], rest[len(gs):]
        layer = pl.program_id(0)
        for ll in range(len(gs)):
            @pl.when(jnp.logical_or(stacked, layer == ll))
            def _():
                g_ref = g_refs[ll]
                g = g_ref[0].astype(F32)
                for j in range(1, g_n):
                    g = g + g_ref[j].astype(F32)
                mn = ADAM_B1 * m_ref[...] + (1.0 - ADAM_B1) * g
                vn = ADAM_B2 * v_ref[...] + (1.0 - ADAM_B2) * (g * g)
                m_hat = mn / c1
                v_hat = vn / c2
                go_ref[...] = g
                d_ref[...] = -ADAM_LR * (m_hat / (jnp.sqrt(v_hat) + ADAM_EPS) + ADAM_WD * w_ref[...])
                mo_ref[...] = mn
                vo_ref[...] = vn

    def g_spec(ll):
        if stacked:
            return pl.BlockSpec((None, g_n, tr, c), lambda l, i: (l, 0, i, 0))
        return pl.BlockSpec((g_n, tr, c), lambda l, i: (0, jnp.where(l == ll, i, jnp.where(l < ll, 0, nt - 1)), 0))

    blk = pl.BlockSpec((None, tr, c), lambda l, i: (l, i, 0))
    sds = jax.ShapeDtypeStruct((nl, r, c), F32)
    return _call(
        body, name=name, grid=(nl, nt),
        in_specs=[blk, blk, blk] + [g_spec(ll) for ll in range(len(gs))],
        out_specs=[blk, blk, blk, blk], out_shape=[sds, sds, sds, sds],
        sem=("arbitrary", "arbitrary"), args=(w, m, v, *gs), comm=comm)


def _sum_devices(g, name):
    _, r, c = g.shape
    tr = _tile(r, 512, SUBLANES)

    def body(g_ref, o_ref):
        acc = g_ref[0]
        for j in range(1, N_DEV):
            acc = acc + g_ref[j]
        o_ref[...] = acc

    return pl.pallas_call(body, name=name, grid=(r // tr,), in_specs=[pl.BlockSpec((N_DEV, tr, c), lambda i: (0, i, 0))],
                          out_specs=pl.BlockSpec((tr, c), lambda i: (i, 0)), out_shape=jax.ShapeDtypeStruct((r, c), F32),
                          compiler_params=_params(("parallel",)))(g)


def _ada_fwd(c16, w_ada, b_shard, name):
    nl, d, cs = w_ada.shape
    tk = _tile(d, 512, LANES)
    nk = d // tk

    def body(c_ref, w_ref, b_ref, o_ref):
        kk = pl.program_id(1)

        @pl.when(kk == 0)
        def _():
            o_ref[...] = jnp.broadcast_to(b_ref[...], o_ref.shape)

        o_ref[...] += _dg(_silu(c_ref[...]), w_ref[...], 1, 0)

    return pl.pallas_call(
        body, name=name, grid=(nl, nk),
        in_specs=[pl.BlockSpec((16, tk), lambda l, kk: (0, kk)), pl.BlockSpec((None, tk, cs), lambda l, kk: (l, kk, 0)),
                  pl.BlockSpec((None, 1, cs), lambda l, kk: (l, 0, 0))],
        out_specs=pl.BlockSpec((None, 16, cs), lambda l, kk: (l, 0, 0)),
        out_shape=jax.ShapeDtypeStruct((nl, 16, cs), F32),
        compiler_params=_params(("parallel", "arbitrary")))(c16, w_ada, b_shard)


def _ada_bwd(c16, dm16, w_ada, name):
    nl, d, cs = w_ada.shape
    td = _tile(d, 512, LANES)

    def body(c_ref, dm_ref, w_ref, gw_ref, dc_ref):
        cv = c_ref[...]
        s, vjp = jax.vjp(_silu, cv)
        gw_ref[...] = _dg(s, dm_ref[...], 0, 0)
        ds = _dg(dm_ref[...], w_ref[...], 1, 1)
        dc_ref[...] = vjp(ds)[0]

    return pl.pallas_call(
        body, name=name, grid=(nl, d // td),
        in_specs=[pl.BlockSpec((16, td), lambda l, i: (0, i)), pl.BlockSpec((None, 16, cs), lambda l, i: (l, 0, 0)),
                  pl.BlockSpec((None, td, cs), lambda l, i: (l, i, 0))],
        out_specs=[pl.BlockSpec((None, td, cs), lambda l, i: (l, i, 0)), pl.BlockSpec((None, 16, td), lambda l, i: (l, 0, i))],
        out_shape=[jax.ShapeDtypeStruct((nl, d, cs), F32), jax.ShapeDtypeStruct((nl, 16, d), F32)],
        compiler_params=_params(("parallel", "parallel")))(c16, dm16, w_ada)


def _pack_rows(shape):
    n = int(np.prod(shape))
    return SUBLANES * (-(-n // (LANES * SUBLANES)))


def _pack(arrays, row_align):
    parts, total = [], 0
    for a in arrays:
        flat = a.reshape(-1).astype(F32)
        rows = _pack_rows(a.shape)
        total += rows
        parts += [flat, jnp.zeros((rows * LANES - flat.shape[0],), F32)]
    parts.append(jnp.zeros(((-total % row_align) * LANES,), F32))
    return jnp.concatenate([p for p in parts if p.shape[0]]).reshape(-1, LANES)


def _unpack(packed, shapes):
    out, r = [], 0
    for s in shapes:
        rows = _pack_rows(s)
        out.append(packed[r:r + rows].reshape(-1)[:int(np.prod(s))].reshape(s))
        r += rows
    return out


def _rope_tables():
    half, nf = RET_DK // 2, RET_DK // 4
    pos = jnp.arange(SEQ)
    row = (pos // GRID_W).astype(F32)
    col = (pos % GRID_W).astype(F32)
    inv = ROPE_BASE ** (-jnp.arange(nf, dtype=F32) / nf)
    ar, ac = row[:, None] * inv[None, :], col[:, None] * inv[None, :]
    cos = jnp.concatenate([jnp.cos(ar), jnp.cos(ar), jnp.cos(ac), jnp.cos(ac)], axis=-1)
    sin = jnp.concatenate([-jnp.sin(ar), jnp.sin(ar), -jnp.sin(ac), jnp.sin(ac)], axis=-1)
    cos = jnp.concatenate([jnp.ones((CTX_LEN, RET_DK), F32), cos], axis=0)
    sin = jnp.concatenate([jnp.zeros((CTX_LEN, RET_DK), F32), sin], axis=0)
    return cos, sin


def _layer_fwd(l, x, mod4, w, cst, arrived):
    n = lambda s: f"l{l}_{s}"
    d = D_MODEL
    h1 = _normmod_fwd(x, w["norm1_g"], mod4, 0, n("norm1"))
    w["w_in"] = _cols_from_shards(arrived("w_in", h1), n("w_in_cols"))
    p, _ = _mm(h1, w["w_in"], n("proj_in"))
    o2, states = _ret_fwd(p, cst["cos"], cst["sin"], w["ret_decay"], cst["order"], n("ret_fwd"))
    ret_out = _ggn_fwd(o2, p, w["ret_gn_g"], n("ret_gn"))
    u2, conv_out = _conv_fwd(p, w["conv_dw_w"], w["conv_dw_b"], w["conv_ln_g"], w["conv_ln_b"], w["conv_pw"], n("conv_fwd"))
    na_out, _ = _na_fwd(p, w["rb"], n("na_fwd"))
    mix = jnp.concatenate([ret_out, conv_out, na_out], axis=1)
    w["w_out"] = arrived("w_out", mix).reshape(_d_mix(), d)
    g1, _ = _mm(mix, w["w_out"], n("proj_out"))
    x1 = _gate_res_fwd(x, g1, mod4, 2, n("res1"))
    h2 = _normmod_fwd(x1, w["norm2_g"], mod4, 1, n("norm2"))
    w["ffn_up"] = arrived("ffn_up", h2)
    u, _ = _mm(h2, w["ffn_up"], n("ffn_up"), b3=True)
    a = _ffn_act_fwd(u, w["ffn_dw_w"], w["ffn_dw_b"], n("ffn_act"))
    w["ffn_down"] = arrived("ffn_down", a).reshape(D_FF, d)
    f, _ = _mm(a, w["ffn_down"], n("ffn_down"))
    x2 = _gate_res_fwd(x1, f, mod4, 5, n("res2"))
    saved = dict(x=x, h1=h1, p=p, o2=o2, states=states, u2=u2, mix=mix, g1=g1, x1=x1, h2=h2, u=u, a=a, f=f)
    return x2, saved


def _layer_bwd(l, dx2, s, mod4, w, cst, send):
    n = lambda t: f"l{l}_{t}"
    d = D_MODEL
    nj = D_FF // _ffn_tc()
    dfg, dg2 = _gate_res_bwd(dx2, s["f"], mod4, 5, n("res2_bwd"))
    da, _ = _mm(dfg, w["ffn_down"], n("ffn_down_dx"), tb=True)
    d_ffn_down, _ = _mm(_transpose_bf16(s["a"], n("act_t")), dfg, n("ffn_down_dw"), out_dtype=BF16, tm_max=DW_TM)
    tok = send(("ffn_down", l), d_ffn_down.reshape(N_DEV, D_FF // N_DEV, d))
    dcv, dcg = _ffn_act_bwd1(s["u"], da, w["ffn_dw_w"], _after(w["ffn_dw_b"], tok), n("ffn_act_bwd"))
    du, d_ffn_dw_w, d_ffn_dw_b = _dwconv_bwd(dcv, dcg, s["u"], w["ffn_dw_w"], n("ffn_dw_bwd"))
    d_ffn_dw_b = d_ffn_dw_b[0]
    dh2, _ = _mm(du, w["ffn_up"], n("ffn_up_dx"), tb=True, b3=True)
    d_ffn_up, _ = _mm(_transpose_bf16(s["h2"], n("h2_t")), du, n("ffn_up_dw"), out_dtype=BF16, tm_max=DW_TM,
                      o_cs=2 * D_FF // N_DEV)
    tok = send(("ffn_up", l), d_ffn_up)
    (dx1, dn2, dsh2, dsc2), _ = _normmod_bwd(s["x1"], _after(w["norm2_g"], tok), mod4, 1, dh2, dx2, n("norm2_bwd"))
    dgg, dg1 = _gate_res_bwd(dx1, s["g1"], mod4, 2, n("res1_bwd"))
    dmix, _ = _mm(dgg, w["w_out"], n("proj_out_dx"), tb=True)
    d_w_out, _ = _mm(_transpose_bf16(s["mix"], n("mix_t")), dgg, n("proj_out_dw"), out_dtype=BF16, tm_max=DW_TM)
    tok = send(("w_out", l), d_w_out.reshape(N_DEV, _d_mix() // N_DEV, d))
    do, dgate, dgn = _ggn_bwd(s["o2"], s["p"], _after(w["ret_gn_g"], tok), dmix, n("ret_gn_bwd"))
    dqr, dkr, dvr, ddec = _ret_bwd(s["p"], cst["cos"], cst["sin"], w["ret_decay"], cst["order"], s["states"], do, n("ret_bwd"))
    du2, dlng, dlnb, dpw = _conv_bwd1(s["u2"], dmix, w["conv_ln_g"], w["conv_ln_b"], w["conv_pw"], n("conv_bwd1"))
    (dca, dcb, ddww, ddwb), _ = _conv_bwd2(du2, s["p"], w["conv_dw_w"], n("conv_bwd2"))
    (dnq, dnk, dnv, drb), _ = _na_bwd(s["p"], w["rb"], dmix, n("na_bwd"))
    dp = _assemble_dp(dqr, dkr, dvr, dgate, dca, dcb, dnq, dnk, dnv, n("dproj"))
    h1_t = _transpose_bf16(s["h1"], n("h1_t"))
    half = d // 2
    for i in range(2):
        d_w_in, _ = _mm(h1_t[i * half:(i + 1) * half], dp, n(f"proj_in_dw{i}"), out_dtype=BF16, tm_max=DW_TM,
                        o_cs=_d_in() // N_DEV)
        tok = send(("w_in", l, i), d_w_in)
    dh1, _ = _mm(dp, w["w_in"], n("proj_in_dx"), tb=True, after=tok)
    (dx, dn1, dsh1, dsc1), _ = _normmod_bwd(s["x"], _after(w["norm1_g"], tok), mod4, 0, dh1, dx1, n("norm1_bwd"))
    dmod = jnp.concatenate([dsh1, dsc1, dg1, dsh2, dsc2, dg2], axis=1)
    small = dict(norm1_g=dn1[0], ret_decay=ddec[:, :, 0, 0], ret_gn_g=dgn[0], conv_dw_w=ddww, conv_dw_b=ddwb[0],
                 conv_ln_g=dlng[0], conv_ln_b=dlnb[0], conv_pw=dpw, na_rpb=_rpb_rows_t(drb), norm2_g=dn2[0],
                 ffn_dw_w=d_ffn_dw_w, ffn_dw_b=d_ffn_dw_b)
    return dx, dmod, small


def _d_mix():
    return _ret_w() + CONV_W + _na_w()


_SMALL = ["c_ctx", "b_ada", "norm1_g", "ret_decay", "ret_gn_g", "conv_dw_w", "conv_dw_b", "conv_ln_g", "conv_ln_b",
          "conv_pw", "na_rpb", "norm2_g", "ffn_dw_w", "ffn_dw_b", "final_g"]
_SMALL_SHARD_AXIS = {"conv_dw_w": 2, "conv_pw": 1, "ffn_dw_w": 2}


def kernel(x, c, ctx, c_ctx, w_ada, b_ada, norm1_g, w_in, ret_decay, ret_gn_g, conv_dw_w, conv_dw_b, conv_ln_g, conv_ln_b, conv_pw, na_rpb, w_out, norm2_g, ffn_up, ffn_dw_w, ffn_dw_b, ffn_down, final_g, loss_target, m_c_ctx, m_w_ada, m_b_ada, m_norm1_g, m_w_in, m_ret_decay, m_ret_gn_g, m_conv_dw_w, m_conv_dw_b, m_conv_ln_g, m_conv_ln_b, m_conv_pw, m_na_rpb, m_w_out, m_norm2_g, m_ffn_up, m_ffn_dw_w, m_ffn_dw_b, m_ffn_down, m_final_g, v_c_ctx, v_w_ada, v_b_ada, v_norm1_g, v_w_in, v_ret_decay, v_ret_gn_g, v_conv_dw_w, v_conv_dw_b, v_conv_ln_g, v_conv_ln_b, v_conv_pw, v_na_rpb, v_w_out, v_norm2_g, v_ffn_up, v_ffn_dw_w, v_ffn_dw_b, v_ffn_down, v_final_g):
    d, nl = D_MODEL, DEPTH
    cs = 6 * d // N_DEV
    me = _my_index()
    weights = dict(c_ctx=c_ctx, w_ada=w_ada, b_ada=b_ada, norm1_g=norm1_g, w_in=w_in, ret_decay=ret_decay, ret_gn_g=ret_gn_g,
                   conv_dw_w=conv_dw_w, conv_dw_b=conv_dw_b, conv_ln_g=conv_ln_g, conv_ln_b=conv_ln_b, conv_pw=conv_pw,
                   na_rpb=na_rpb, w_out=w_out, norm2_g=norm2_g, ffn_up=ffn_up, ffn_dw_w=ffn_dw_w, ffn_dw_b=ffn_dw_b,
                   ffn_down=ffn_down, final_g=final_g)
    mom = dict(c_ctx=m_c_ctx, w_ada=m_w_ada, b_ada=m_b_ada, norm1_g=m_norm1_g, w_in=m_w_in, ret_decay=m_ret_decay,
               ret_gn_g=m_ret_gn_g, conv_dw_w=m_conv_dw_w, conv_dw_b=m_conv_dw_b, conv_ln_g=m_conv_ln_g,
               conv_ln_b=m_conv_ln_b, conv_pw=m_conv_pw, na_rpb=m_na_rpb, w_out=m_w_out, norm2_g=m_norm2_g,
               ffn_up=m_ffn_up, ffn_dw_w=m_ffn_dw_w, ffn_dw_b=m_ffn_dw_b, ffn_down=m_ffn_down, final_g=m_final_g)
    var = dict(c_ctx=v_c_ctx, w_ada=v_w_ada, b_ada=v_b_ada, norm1_g=v_norm1_g, w_in=v_w_in, ret_decay=v_ret_decay,
               ret_gn_g=v_ret_gn_g, conv_dw_w=v_conv_dw_w, conv_dw_b=v_conv_dw_b, conv_ln_g=v_conv_ln_g,
               conv_ln_b=v_conv_ln_b, conv_pw=v_conv_pw, na_rpb=v_na_rpb, w_out=v_w_out, norm2_g=v_norm2_g,
               ffn_up=v_ffn_up, ffn_dw_w=v_ffn_dw_w, ffn_dw_b=v_ffn_dw_b, ffn_down=v_ffn_down, final_g=v_final_g)

    big_names = ["w_in", "w_out", "ffn_up", "ffn_down"]
    shards = {(nm, l): _cast_bf16(weights[nm][l], f"cast_{nm}{l}") for l in range(nl) for nm in big_names}
    small_sharded = _pack([conv_dw_w, conv_pw, ffn_dw_w], SUBLANES)
    c_rows = jnp.pad(c, ((0, SUBLANES - 1), (0, 0)))
    gathered = _run_comm(_Gather([c_rows, small_sharded, shards[("w_in", 0)]]), "gather_first")
    c_all = gathered[0][:, 0, :]
    def whole(rows, shard_shape, axis):
        n_el = int(np.prod(shard_shape))
        parts = rows.reshape(N_DEV, -1)[:, :n_el].reshape((N_DEV,) + tuple(shard_shape))
        parts = jnp.moveaxis(parts, 0, axis)
        return parts.reshape(shard_shape[:axis] + (N_DEV * shard_shape[axis],) + shard_shape[axis + 1:])

    r0 = _pack_rows(conv_dw_w.shape)
    r1 = r0 + _pack_rows(conv_pw.shape)
    r2 = r1 + _pack_rows(ffn_dw_w.shape)
    full_conv_dw_w = whole(gathered[1][:, :r0], conv_dw_w.shape, 2)
    full_conv_pw = whole(gathered[1][:, r0:r1], conv_pw.shape, 1)
    full_ffn_dw_w = whole(gathered[1][:, r1:r2], ffn_dw_w.shape, 2)

    c16 = jnp.concatenate([c_all, jnp.broadcast_to(c_ctx[None, :], (N_DEV, d))], axis=0)
    b_shard = lax.dynamic_slice_in_dim(b_ada, me * cs, cs, axis=1)[:, None, :]
    m_shard = _ada_fwd(c16, w_ada, b_shard, "ada_fwd")
    m_all = _run_comm(_Gather([m_shard.reshape(nl * 16, cs)]), "gather_mod")[0]
    m_full = m_all.reshape(N_DEV, nl, 16, cs).transpose(1, 2, 0, 3).reshape(nl, 16, 6 * d)
    m_lat = lax.dynamic_index_in_dim(m_full, me, axis=1, keepdims=False)
    mod = jnp.stack([m_full[:, N_DEV], m_lat], axis=1).reshape(nl, 2, 6, 1, d)

    arriving, token = {}, m_all
    for l in range(nl):
        for nm in big_names:
            if (nm, l) != ("w_in", 0):
                arriving[(nm, l)], token = _split_start(shards[(nm, l)], True, f"gather_{nm}{l}", token)
    mod = _after(mod, token)

    cos, sin = _rope_tables()
    cst = dict(cos=cos, sin=sin, order=_chunk_order())
    layer_w = []
    for l in range(nl):
        layer_w.append(dict(
            norm1_g=norm1_g[l][None], norm2_g=norm2_g[l][None], ret_decay=ret_decay[l], ret_gn_g=ret_gn_g[l][None],
            conv_dw_w=full_conv_dw_w[l], conv_dw_b=conv_dw_b[l][None], conv_ln_g=conv_ln_g[l][None],
            conv_ln_b=conv_ln_b[l][None], conv_pw=full_conv_pw[l], rb=_rpb_rows(na_rpb[l]),
            ffn_dw_w=full_ffn_dw_w[l], ffn_dw_b=ffn_dw_b[l][None]))

    xs = jnp.concatenate([ctx[0], x[0]], axis=0)
    saved = []
    for l in range(nl):
        def arrived(nm, after, l=l):
            if (nm, l) == ("w_in", 0):
                return gathered[2]
            return _split_wait(arriving[(nm, l)], after, f"arrived_{nm}{l}")

        xs, sv = _layer_fwd(l, xs, mod[l], layer_w[l], cst, arrived)
        saved.append(sv)
    loss_tile, dxs, dfinal = _loss_head(xs, final_g[None], loss_target[0], "loss_head")
    loss = lax.psum(loss_tile[0, 0], ("x", "y", "c"))

    dmods, smalls = [None] * nl, [None] * nl
    leaving, last = {}, [loss_tile]

    def send(key, partial):
        leaving[key], token = _split_start(partial, False, "send_" + "_".join(str(k) for k in key), last[0])
        last[0] = token
        return token

    per_layer = [nm for nm in _SMALL if nm not in ("c_ctx", "b_ada", "final_g")]
    small_packs, small_arriving = [None] * nl, [None] * nl
    for l in reversed(range(nl)):
        dxs, dmods[l], smalls[l] = _layer_bwd(l, dxs, saved[l], mod[l], layer_w[l], cst, send)
        small_packs[l] = _pack([smalls[l][nm] for nm in per_layer], 512)
        if l > 0:
            small_arriving[l], last[0] = _split_start(small_packs[l], True, f"gather_small_grads{l}", last[0])
    grad_x = dxs[CTX_LEN:][None]

    arrived_grad = lambda key, after: _split_wait(leaving[key], after, "got_" + "_".join(str(k) for k in key))
    out_big = {}
    after = dxs
    for nm in ["ffn_down", "ffn_up", "w_out"]:
        out_big[nm], _ = _adamw(weights[nm], mom[nm], var[nm], [arrived_grad((nm, l), after) for l in range(nl)],
                                f"adamw_{nm}")
        after = out_big[nm][0]

    dm_mine = jnp.stack(dmods).reshape(nl * 2, 6 * d)
    dm_rows = jnp.pad(dm_mine, ((0, SUBLANES - nl * 2), (0, 0)))
    dm_all = _run_comm(_Gather([dm_rows]), "gather_dmod", after=after)[0][:, :nl * 2].reshape(N_DEV, nl, 2, 6 * d)
    dm16_full = jnp.concatenate([dm_all[:, :, 1].transpose(1, 0, 2), dm_all[:, :, 0].transpose(1, 0, 2)], axis=1)
    dm16 = lax.dynamic_slice_in_dim(dm16_full, me * cs, cs, axis=2)
    g_w_ada, dc16 = _ada_bwd(c16, dm16, w_ada, "ada_bwd")

    shared = dict(c_ctx=jnp.sum(dc16[:, N_DEV:], axis=(0, 1)),
                  b_ada=jnp.sum(jnp.stack(dmods).reshape(nl, 2, 6 * d), axis=1), final_g=dfinal[0])
    shared_all = _run_comm(_Gather([_pack(list(shared.values()), SUBLANES)]), "gather_shared_grads")[0]
    small_arriving[0], token = _split_start(small_packs[0], True, "gather_small_grads0", shared_all)

    out_big["w_ada"], _ = _adamw(w_ada, m_w_ada, v_w_ada, g_w_ada[:, None], "adamw_w_ada")
    halves = lambda a: a.reshape(2 * nl, d // 2, a.shape[2])
    res, _ = _adamw(halves(w_in), halves(m_w_in), halves(v_w_in),
                    [arrived_grad(("w_in", l, i), token) for l in range(nl) for i in range(2)], "adamw_w_in")
    out_big["w_in"] = [r.reshape(w_in.shape) for r in res]

    g_small = dict(zip(shared, _unpack(_sum_devices(shared_all, "sum_shared_grads"), [v.shape for v in shared.values()])))
    per = []
    for l in range(nl):
        got = _split_wait(small_arriving[l], res[0], f"arrived_small_grads{l}")
        per.append(_unpack(_sum_devices(got, f"sum_small_grads{l}"), [smalls[l][nm].shape for nm in per_layer]))
    g_small.update({nm: jnp.stack([per[l][i] for l in range(nl)]) for i, nm in enumerate(per_layer)})
    for nm, ax in _SMALL_SHARD_AXIS.items():
        n_sh = weights[nm].shape[ax]
        g_small[nm] = lax.dynamic_slice_in_dim(g_small[nm], me * n_sh, n_sh, axis=ax)
    shapes_own = [weights[nm].shape for nm in _SMALL]
    pk = lambda src: _pack([src[nm] for nm in _SMALL], 2 * SUBLANES)[None]
    res_small, _ = _adamw(pk(weights), pk(mom), pk(var), pk(g_small)[:, None], "adamw_small")
    out_small = [dict(zip(_SMALL, _unpack(r[0], shapes_own))) for r in res_small]

    names = ["c_ctx", "w_ada", "b_ada", "norm1_g", "w_in", "ret_decay", "ret_gn_g", "conv_dw_w", "conv_dw_b", "conv_ln_g",
             "conv_ln_b", "conv_pw", "na_rpb", "w_out", "norm2_g", "ffn_up", "ffn_dw_w", "ffn_dw_b", "ffn_down", "final_g"]
    outs = [loss, grad_x]
    for kind in range(4):
        for nm in names:
            outs.append(out_big[nm][kind] if nm in out_big else out_small[kind][nm])
    return tuple(outs)
```

```python
import numpy as np
import jax
import jax.numpy as jnp
from jax import lax
from jax.experimental import pallas as pl
from jax.experimental.pallas import tpu as pltpu

D_MODEL = 2048
SEQ = 4096
DEPTH = 2
GRID_W = 64
CTX_LEN = 256
RET_HEADS = 4
RET_DK = 128
RET_DV = 256
RET_CHUNK = 128
CONV_W = 512
CONV_K = 31
NA_HEADS = 4
NA_DH = 128
NA_ROWS = 8
NA_COLS = 16
D_FF = 5632
FFN_K = 3
ROPE_BASE = 10000.0
EPS = 1e-6
ADAM_LR = 0.001
ADAM_B1 = 0.9
ADAM_B2 = 0.999
ADAM_EPS = 1e-08
ADAM_WD = 0.01
ADAM_STEP = 10
N_DEV = 8

LANES = 128
SUBLANES = 8
VMEM_LIMIT = 56 * 1024 * 1024
ROW_CHUNK = 16

F32 = jnp.float32
BF16 = jnp.bfloat16
MESH = pl.DeviceIdType.MESH
NEG = -1e30


def _ret_qk_w():
    return RET_HEADS * RET_DK


def _ret_w():
    return RET_HEADS * RET_DV


def _na_w():
    return NA_HEADS * NA_DH


def _d_in():
    return 2 * _ret_qk_w() + 2 * _ret_w() + 2 * CONV_W + 3 * _na_w()


def _offsets():
    sizes = [_ret_qk_w(), _ret_qk_w(), _ret_w(), _ret_w(), CONV_W, CONV_W, _na_w(), _na_w(), _na_w()]
    offs = [0]
    for s in sizes[:-1]:
        offs.append(offs[-1] + s)
    return dict(zip(["q", "k", "v", "g", "a", "b", "nq", "nk", "nv"], offs))


def _t_rows():
    return CTX_LEN + SEQ


def _tm():
    return CTX_LEN


def _params(sem=None):
    kw = dict(vmem_limit_bytes=VMEM_LIMIT)
    if sem is not None:
        kw["dimension_semantics"] = sem
    return pltpu.CompilerParams(**kw)


def _tile(n, pref, align):
    best = None
    for t in range(align, min(n, pref) + 1, align):
        if n % t == 0:
            best = t
    return best if best is not None else n


def _dg(a, b, ca, cb):
    return lax.dot_general(a.astype(BF16), b.astype(BF16), (((ca,), (cb,)), ((), ())), preferred_element_type=F32)


@jax.custom_vjp
def dot_nn(a, b):
    return _dg(a, b, 1, 0)


dot_nn.defvjp(lambda a, b: (_dg(a, b, 1, 0), (a, b)),
              lambda r, g: (_dg(g, r[1], 1, 1), _dg(r[0], g, 0, 0)))


@jax.custom_vjp
def dot_nt(a, b):
    return _dg(a, b, 1, 1)


dot_nt.defvjp(lambda a, b: (_dg(a, b, 1, 1), (a, b)),
              lambda r, g: (_dg(g, r[1], 1, 0), _dg(g, r[0], 0, 0)))


@jax.custom_vjp
def dot_tn(a, b):
    return _dg(a, b, 0, 0)


dot_tn.defvjp(lambda a, b: (_dg(a, b, 0, 0), (a, b)),
              lambda r, g: (_dg(r[1], g, 1, 1), _dg(r[0], g, 1, 0)))


def _sigmoid(x):
    return 0.5 * jnp.tanh(0.5 * x) + 0.5


def _silu(x):
    return x * _sigmoid(x)


def _my_pos():
    return lax.axis_index("x"), lax.axis_index("y"), lax.axis_index("c")


def _my_index():
    x, y, c = _my_pos()
    return 4 * x + 2 * y + c


_ANY = pl.BlockSpec(memory_space=pl.ANY)


class _Gather:
    def __init__(self, arrays):
        self.arrays = list(arrays)
        n = len(self.arrays)
        self.out_shape = [jax.ShapeDtypeStruct((N_DEV,) + a.shape, a.dtype) for a in self.arrays]
        self.scratch = [pltpu.SemaphoreType.DMA((n, 7)), pltpu.SemaphoreType.DMA((n, 7)), pltpu.SemaphoreType.DMA((n,))]

    def _plan(self, xs, outs, sems):
        send_sems, recv_sems, local_sems = sems
        n = len(self.arrays)
        x, y, c = _my_pos()
        me, sibling = (x, y, c), (x, y, 1 - c)
        chips = [(1 - x, y), (x, 1 - y), (1 - x, 1 - y)]

        def slot(a, p):
            return outs[a].at[4 * p[0] + 2 * p[1] + p[2]]

        def copy(a, k, block, to, src=None):
            return pltpu.make_async_remote_copy(
                src_ref=slot(a, block) if src is None else src, dst_ref=slot(a, block),
                send_sem=send_sems.at[a, k], recv_sem=recv_sems.at[a, k], device_id=to, device_id_type=MESH)

        mine = [pltpu.make_async_copy(xs[a], slot(a, me), local_sems.at[a]) for a in range(n)]
        first = []
        for a in range(n):
            first.append(copy(a, 0, me, sibling, src=xs[a]))
            first += [copy(a, 1 + j, me, (*chip, c), src=xs[a]) for j, chip in enumerate(chips)]
        return n, c, me, sibling, chips, copy, mine, first

    def start(self, xs, outs, sems):
        _, _, _, _, _, _, mine, first = self._plan(xs, outs, sems)
        for m in mine:
            m.start()
        for cp in first:
            cp.start()

    def finish(self, xs, outs, sems):
        n, c, me, sibling, chips, copy, mine, first = self._plan(xs, outs, sems)
        passed = []
        for a in range(n):
            for j, chip in enumerate(chips):
                copy(a, 1 + j, (*chip, c), me).wait_recv()
                p = copy(a, 4 + j, (*chip, c), sibling)
                p.start()
                passed.append(p)
        for a in range(n):
            copy(a, 0, sibling, me).wait_recv()
            for j, chip in enumerate(chips):
                copy(a, 4 + j, (*chip, 1 - c), me).wait_recv()
        for cp in first + passed:
            cp.wait_send()
        for m in mine:
            m.wait()


def _run_comm(comm, name, after=None):
    n = len(comm.arrays)
    extra = [] if after is None else [after]

    def body(*refs):
        xs, outs, sems = refs[:n], refs[n + len(extra):2 * n + len(extra)], refs[2 * n + len(extra):]
        comm.start(xs, outs, sems)
        comm.finish(xs, outs, sems)

    return pl.pallas_call(body, name=name, out_shape=comm.out_shape, in_specs=[_ANY] * (n + len(extra)),
                          out_specs=[_ANY] * n, scratch_shapes=comm.scratch)(*comm.arrays, *extra)


_HBM = pl.BlockSpec(memory_space=pltpu.HBM)
_SEMS = pl.BlockSpec(memory_space=pltpu.SEMAPHORE)
_EFFECT = pltpu.SideEffectType.DATAFLOW_SIDE_EFFECTING


def _own_slot(x, gathering, name):
    shape = (N_DEV,) + x.shape if gathering else x.shape
    r, c = shape[1], shape[2]
    tr = _tile(r, 256, 2 * SUBLANES)
    me = jnp.reshape(_my_index(), (1,)).astype(jnp.int32)

    def body(me_ref, x_ref, o_ref):
        o_ref[...] = x_ref[...]

    src = (pl.BlockSpec((tr, c), lambda i, m: (i, 0)) if gathering
           else pl.BlockSpec((None, tr, c), lambda i, m: (m[0], i, 0)))
    grid_spec = pltpu.PrefetchScalarGridSpec(
        num_scalar_prefetch=1, grid=(r // tr,), in_specs=[src],
        out_specs=pl.BlockSpec((None, tr, c), lambda i, m: (m[0], i, 0)))
    return pl.pallas_call(body, name=name, grid_spec=grid_spec, out_shape=jax.ShapeDtypeStruct(shape, x.dtype),
                          compiler_params=_params(("arbitrary",)))(me, x)


def _split_plan(x_ref, land_ref, send_sems, recv_sems, gathering):
    x, y, c = _my_pos()
    me = 4 * x + 2 * y + c
    sends, recvs = [], []
    for k in range(1, N_DEV):
        px = 1 - x if (k >> 2) & 1 else x
        py = 1 - y if (k >> 1) & 1 else y
        pc = 1 - c if k & 1 else c
        peer = 4 * px + 2 * py + pc
        mine, theirs = (x_ref, x_ref) if gathering else (x_ref.at[peer], x_ref.at[me])
        sends.append(pltpu.make_async_remote_copy(
            src_ref=mine, dst_ref=land_ref.at[me], send_sem=send_sems.at[k - 1], recv_sem=recv_sems.at[k - 1],
            device_id=(px, py, pc), device_id_type=MESH))
        recvs.append(pltpu.make_async_remote_copy(
            src_ref=theirs, dst_ref=land_ref.at[peer], send_sem=send_sems.at[k - 1], recv_sem=recv_sems.at[k - 1],
            device_id=(px, py, pc), device_id_type=MESH))
    return sends, recvs


def _split_start(x, gathering, name, prev):
    land = _own_slot(x, gathering, name + "_own")

    def body(x_ref, land_ref, prev_ref, send_sems, recv_sems, x_thru, land_thru, token):
        sends, _ = _split_plan(x_ref, land_ref, send_sems, recv_sems, gathering)
        for s in sends:
            s.start()
        token[...] = jnp.zeros_like(token)

    sems = pltpu.SemaphoreType.DMA((N_DEV - 1,))
    send_sems, recv_sems, x_thru, land_thru, token = pl.pallas_call(
        body, name=name,
        out_shape=(sems, sems, pltpu.HBM(x.shape, x.dtype), pltpu.HBM(land.shape, land.dtype),
                   jax.ShapeDtypeStruct((SUBLANES, LANES), F32)),
        in_specs=(_HBM, _HBM, _ANY), out_specs=(_SEMS, _SEMS, _HBM, _HBM, pl.BlockSpec(memory_space=pltpu.VMEM)),
        input_output_aliases={0: 2, 1: 3},
        compiler_params=pltpu.CompilerParams(has_side_effects=_EFFECT),
    )(pltpu.with_memory_space_constraint(x, pltpu.HBM), pltpu.with_memory_space_constraint(land, pltpu.HBM), prev)
    return (send_sems, recv_sems, x_thru, land_thru, gathering), token


def _after(a, token):
    return a + token[0, 0].astype(a.dtype)


def _split_wait(handle, after, name):
    send_sems, recv_sems, x_thru, land_thru, gathering = handle

    def body(x_ref, land_ref, send_sems, recv_sems, after_ref, x_dead, got_ref):
        sends, recvs = _split_plan(x_ref, land_ref, send_sems, recv_sems, gathering)
        for s in sends:
            s.wait_send()
        for r in recvs:
            r.wait_recv()

    return pl.pallas_call(
        body, name=name, out_shape=(pltpu.HBM(x_thru.shape, x_thru.dtype), pltpu.HBM(land_thru.shape, land_thru.dtype)),
        in_specs=(_HBM, _HBM, _SEMS, _SEMS, _ANY), out_specs=(_HBM, _HBM), input_output_aliases={0: 0, 1: 1},
        compiler_params=pltpu.CompilerParams(has_side_effects=_EFFECT),
    )(x_thru, land_thru, send_sems, recv_sems, after)[1]


def _call(body, *, name, grid, in_specs, out_specs, out_shape, args, scratch=(), sem=None, after=None):
    if after is None:
        return list(pl.pallas_call(body, name=name, grid=grid, in_specs=list(in_specs), out_specs=list(out_specs),
                                   out_shape=list(out_shape), scratch_shapes=list(scratch),
                                   compiler_params=_params(sem))(*args))
    n_in = len(in_specs)

    def wrapped(*refs):
        body(*refs[:n_in], *refs[n_in + 1:])

    return list(pl.pallas_call(wrapped, name=name, grid=grid, in_specs=list(in_specs) + [_ANY], out_specs=list(out_specs),
                               out_shape=list(out_shape), scratch_shapes=list(scratch),
                               compiler_params=_params(sem))(*args, after))


MM_B_BLOCK_BYTES = 6 * 1024 * 1024
MM_O_BLOCK_BYTES = 13 * 1024 * 1024 // 2


def _mm(a, b, name, tb=False, out_dtype=F32, b3=False, o_cs=None, tm_max=1088, after=None):
    m, k = a.shape
    if b3:
        cs = b.shape[2]
        n, kb = (b.shape[1], N_DEV * cs) if tb else (N_DEV * cs, b.shape[1])
    else:
        n, kb = (b.shape[0], b.shape[1]) if tb else (b.shape[1], b.shape[0])
    assert k == kb, (a.shape, b.shape, tb)
    if b3 and tb:
        tm, tn = _tile(m, 544, 2 * SUBLANES), _tile(n, 256, LANES)

        def body_shards(a_ref, b_ref, o_ref):
            r = None
            for j in range(N_DEV):
                part = lax.dot_general(a_ref[:, j * cs:(j + 1) * cs], b_ref[j], (((1,), (1,)), ((), ())),
                                       preferred_element_type=F32)
                r = part if r is None else r + part
            o_ref[...] = r.astype(o_ref.dtype)

        return _call(
            body_shards, name=name, grid=(m // tm, n // tn),
            in_specs=[pl.BlockSpec((tm, k), lambda i, j: (i, 0)), pl.BlockSpec((N_DEV, tn, cs), lambda i, j: (0, j, 0))],
            out_specs=[pl.BlockSpec((tm, tn), lambda i, j: (i, j))], out_shape=[jax.ShapeDtypeStruct((m, n), out_dtype)],
            sem=("parallel", "parallel"), args=(a, b), after=after)[0]
    tm = _tile(m, tm_max, 2 * SUBLANES)
    tk = k
    if b3:
        tn = cs
    elif o_cs is not None:
        tn = o_cs if o_cs % LANES == 0 else 2 * o_cs
    else:
        tn = _tile(n, min(MM_B_BLOCK_BYTES // (2 * tk), MM_O_BLOCK_BYTES // (4 * tm)), LANES)
    cb = 1 if tb else 0
    dn = (((1,), (cb,)), ((), ()))

    def body_one(a_ref, b_ref, o_ref):
        r = lax.dot_general(a_ref[...], b_ref[...], dn, preferred_element_type=F32)
        if o_cs is None:
            o_ref[...] = r.astype(o_ref.dtype)
        else:
            for j in range(tn // o_cs):
                o_ref[j] = r[:, j * o_cs:(j + 1) * o_cs].astype(o_ref.dtype)

    a_spec = pl.BlockSpec((tm, tk), lambda i, j: (i, 0))
    if b3:
        b_spec = pl.BlockSpec((None, tk, cs), lambda i, j: (j, 0, 0))
    else:
        b_spec = pl.BlockSpec((tn, tk), lambda i, j: (j, 0)) if tb else pl.BlockSpec((tk, tn), lambda i, j: (0, j))
    if o_cs is None:
        o_spec = pl.BlockSpec((tm, tn), lambda i, j: (i, j))
        o_shape = jax.ShapeDtypeStruct((m, n), out_dtype)
    else:
        o_spec = pl.BlockSpec((tn // o_cs, tm, o_cs), lambda i, j: (j, i, 0))
        o_shape = jax.ShapeDtypeStruct((n // o_cs, m, o_cs), out_dtype)
    return _call(
        body_one, name=name, grid=(m // tm, n // tn), in_specs=[a_spec, b_spec], out_specs=[o_spec], out_shape=[o_shape],
        sem=("parallel", "parallel"), args=(a, b), after=after)[0]


DW_TM = 512


def _transpose_bf16(x, name):
    t, c = x.shape
    tt = _tm()

    def body(x_ref, o_ref):
        o_ref[...] = x_ref[...].T

    return pl.pallas_call(body, name=name, grid=(t // tt,), in_specs=[pl.BlockSpec((tt, c), lambda i: (i, 0))],
                          out_specs=pl.BlockSpec((c, tt), lambda i: (0, i)),
                          out_shape=jax.ShapeDtypeStruct((c, t), BF16), compiler_params=_params(("parallel",)))(x)


def _cast_bf16(x, name):
    r, c = x.shape
    tr = _tile(r, 512, 2 * SUBLANES)

    def body(x_ref, o_ref):
        o_ref[...] = x_ref[...].astype(BF16)

    return pl.pallas_call(body, name=name, grid=(r // tr,), in_specs=[pl.BlockSpec((tr, c), lambda i: (i, 0))],
                          out_specs=pl.BlockSpec((tr, c), lambda i: (i, 0)),
                          out_shape=jax.ShapeDtypeStruct((r, c), BF16), compiler_params=_params(("parallel",)))(x)


def _cols_from_shards(wg, name):
    _, k, cs = wg.shape
    tk = _tile(k, 256, 2 * SUBLANES)

    def body(w_ref, o_ref):
        for j in range(N_DEV):
            o_ref[:, j * cs:(j + 1) * cs] = w_ref[j]

    return pl.pallas_call(body, name=name, grid=(k // tk,),
                          in_specs=[pl.BlockSpec((N_DEV, tk, cs), lambda i: (0, i, 0))],
                          out_specs=pl.BlockSpec((tk, N_DEV * cs), lambda i: (i, 0)),
                          out_shape=jax.ShapeDtypeStruct((k, N_DEV * cs), wg.dtype),
                          compiler_params=_params(("parallel",)))(wg)


def _stream(i):
    return jnp.minimum(i, 1)


def _normmod(x, g, sh, sc):
    y = x * lax.rsqrt(jnp.mean(x * x, axis=-1, keepdims=True) + EPS)
    return (y * g) * (1.0 + sc) + sh


def _mod_spec(chunk, d):
    return pl.BlockSpec((None, None, 1, d), lambda i: (_stream(i), chunk, 0, 0))


def _normmod_fwd(x, g, mod4, which, name):
    t, d = x.shape
    tm = _tm()
    ish, isc = (0, 1) if which == 0 else (3, 4)

    def body(x_ref, g_ref, sh_ref, sc_ref, o_ref, ot_ref):
        h = _normmod(x_ref[...], g_ref[...], sh_ref[...], sc_ref[...]).astype(BF16)
        o_ref[...] = h
        ot_ref[...] = h.T

    row = pl.BlockSpec((tm, d), lambda i: (i, 0))
    return pl.pallas_call(body, name=name, grid=(t // tm,),
                          in_specs=[row, pl.BlockSpec((1, d), lambda i: (0, 0)), _mod_spec(ish, d), _mod_spec(isc, d)],
                          out_specs=[row, pl.BlockSpec((d, tm), lambda i: (0, i))],
                          out_shape=[jax.ShapeDtypeStruct((t, d), BF16), jax.ShapeDtypeStruct((d, t), BF16)],
                          compiler_params=_params(("parallel",)))(x, g, mod4, mod4)


def _normmod_bwd(x, g, mod4, which, dh, dres, name):
    t, d = x.shape
    tm = _tm()
    ish, isc = (0, 1) if which == 0 else (3, 4)

    def body(x_ref, g_ref, sh_ref, sc_ref, dh_ref, dres_ref, dx_ref, dg_ref, dsh_ref, dsc_ref):
        i = pl.program_id(0)
        _, vjp = jax.vjp(_normmod, x_ref[...], g_ref[...], sh_ref[...], sc_ref[...])
        dx, dg, dsh, dsc = vjp(dh_ref[...])
        dx_ref[...] = dres_ref[...] + dx

        @pl.when(i == 0)
        def _():
            dg_ref[...] = jnp.zeros_like(dg_ref)

        @pl.when(i <= 1)
        def _():
            dsh_ref[...] = jnp.zeros_like(dsh_ref)
            dsc_ref[...] = jnp.zeros_like(dsc_ref)

        dg_ref[...] += dg
        dsh_ref[...] += dsh
        dsc_ref[...] += dsc

    row = pl.BlockSpec((tm, d), lambda i: (i, 0))
    vec = pl.BlockSpec((1, d), lambda i: (0, 0))
    svec = pl.BlockSpec((None, 1, d), lambda i: (_stream(i), 0, 0))
    return _call(
        body, name=name, grid=(t // tm,),
        in_specs=[row, vec, _mod_spec(ish, d), _mod_spec(isc, d), row, row],
        out_specs=[row, vec, svec, svec],
        out_shape=[jax.ShapeDtypeStruct((t, d), F32), jax.ShapeDtypeStruct((1, d), F32),
                   jax.ShapeDtypeStruct((2, 1, d), F32), jax.ShapeDtypeStruct((2, 1, d), F32)],
        sem=("arbitrary",), args=(x, g, mod4, mod4, dh, dres))


def _gate_res_fwd(x, f, mod4, chunk, name):
    t, d = x.shape
    tm = _tm()

    def body(x_ref, f_ref, g_ref, o_ref):
        o_ref[...] = x_ref[...] + g_ref[...] * f_ref[...]

    row = pl.BlockSpec((tm, d), lambda i: (i, 0))
    return pl.pallas_call(body, name=name, grid=(t // tm,), in_specs=[row, row, _mod_spec(chunk, d)], out_specs=row,
                          out_shape=jax.ShapeDtypeStruct((t, d), F32), compiler_params=_params(("parallel",)))(x, f, mod4)


def _gate_res_bwd(dx, f, mod4, chunk, name):
    t, d = dx.shape
    tm = _tm()

    def body(dx_ref, f_ref, g_ref, o_ref, dg_ref):
        i = pl.program_id(0)
        dxv = dx_ref[...]
        o_ref[...] = (dxv * g_ref[...]).astype(BF16)

        @pl.when(i <= 1)
        def _():
            dg_ref[...] = jnp.zeros_like(dg_ref)

        dg_ref[...] += jnp.sum(dxv * f_ref[...], axis=0, keepdims=True)

    row = pl.BlockSpec((tm, d), lambda i: (i, 0))
    return pl.pallas_call(
        body, name=name, grid=(t // tm,), in_specs=[row, row, _mod_spec(chunk, d)],
        out_specs=[row, pl.BlockSpec((None, 1, d), lambda i: (_stream(i), 0, 0))],
        out_shape=[jax.ShapeDtypeStruct((t, d), BF16), jax.ShapeDtypeStruct((2, 1, d), F32)],
        compiler_params=_params(("arbitrary",)))(dx, f, mod4)


def _loss_head(x, final_g, target, name):
    t, d = x.shape
    tm = _tm()

    def loss_fn(xv, g, tgt):
        y = (xv * lax.rsqrt(jnp.mean(xv * xv, axis=-1, keepdims=True) + EPS)) * g
        err = y - tgt
        return 0.5 * jnp.sum(jnp.mean(err * err, axis=-1, keepdims=True))

    def body(x_ref, g_ref, t_ref, l_ref, dx_ref, dg_ref):
        i = pl.program_id(0)

        @pl.when(i == 0)
        def _():
            l_ref[...] = jnp.zeros_like(l_ref)
            dg_ref[...] = jnp.zeros_like(dg_ref)
            dx_ref[...] = jnp.zeros_like(dx_ref)

        @pl.when(i > 0)
        def _():
            l, (dx, dg) = jax.value_and_grad(loss_fn, argnums=(0, 1))(x_ref[...], g_ref[...], t_ref[...])
            l_ref[...] += jnp.full(l_ref.shape, l, F32)
            dx_ref[...] = dx
            dg_ref[...] += dg

    row = pl.BlockSpec((tm, d), lambda i: (i, 0))
    vec = pl.BlockSpec((1, d), lambda i: (0, 0))
    return pl.pallas_call(
        body, name=name, grid=(t // tm,),
        in_specs=[row, vec, pl.BlockSpec((tm, d), lambda i: (jnp.maximum(i - 1, 0), 0))],
        out_specs=[pl.BlockSpec((SUBLANES, LANES), lambda i: (0, 0)), row, vec],
        out_shape=[jax.ShapeDtypeStruct((SUBLANES, LANES), F32), jax.ShapeDtypeStruct((t, d), F32),
                   jax.ShapeDtypeStruct((1, d), F32)],
        compiler_params=_params(("arbitrary",)))(x, final_g, target)


def _swap_quarters(x):
    half, nf = RET_DK // 2, RET_DK // 4
    lane = lax.broadcasted_iota(jnp.int32, x.shape, 1)
    return jnp.where((lane % half) < nf, pltpu.roll(x, RET_DK - nf, 1), pltpu.roll(x, nf, 1))


def _rope(x, cos, sin):
    return x * cos + _swap_quarters(x) * sin


def _rope_t(y, cos, sin):
    return y * cos + _swap_quarters(y * sin)


def _ret_consts(d):
    c = RET_CHUNK
    ii = lax.broadcasted_iota(jnp.int32, (c, 1), 0).astype(F32)
    jj = lax.broadcasted_iota(jnp.int32, (1, c), 1).astype(F32)
    fwd = d == 0
    sgn = jnp.where(fwd, 1.0, -1.0).astype(F32)
    pos = jnp.where(fwd, ii, c - 1.0 - ii)
    return sgn * (ii - jj), pos


def _ret_step(lgt, state, q, k, v, diff, pos):
    c = float(RET_CHUNK)
    lg = -(jnp.maximum(-lgt, 0.0) + jnp.log1p(jnp.exp(-jnp.abs(lgt))))
    lower = diff >= 0
    decay = jnp.where(lower, jnp.exp(jnp.where(lower, diff, 0.0) * lg), 0.0)
    xi = jnp.exp((pos + 1.0) * lg)
    zeta = jnp.exp((c - 1.0 - pos) * lg)
    gch = jnp.exp(c * lg)
    inner = dot_nt(q, k) * decay
    out = dot_nn(inner, v) + dot_nn(q, state) * xi
    new_state = state * gch + dot_tn(k * zeta, v)
    return out, new_state


def _chunk_order():
    nc, nch = CTX_LEN // RET_CHUNK, _t_rows() // RET_CHUNK
    fwd = list(range(nch))
    bwd = list(range(nc - 1, -1, -1)) + list(range(nch - 1, nc - 1, -1))
    return jnp.asarray(np.array([fwd, bwd], np.int32))


def _ret_fwd(p, cos, sin, decay, order, name):
    t = p.shape[0]
    c, dk, dv, nh = RET_CHUNK, RET_DK, RET_DV, RET_HEADS
    nch = t // c
    off = _offsets()
    wqk, wv = nh * dk, nh * dv
    assert off["q"] % wqk == 0 and off["k"] % wqk == 0 and off["v"] % wv == 0
    qb, kb, vb = off["q"] // wqk, off["k"] // wqk, off["v"] // wv
    scale = RET_DK ** -0.5

    def body(ord_ref, dec_ref, q_ref, k_ref, v_ref, cos_ref, sin_ref, o_ref, st_ref, state):
        d, s = pl.program_id(0), pl.program_id(1)

        @pl.when(s == 0)
        def _():
            state[...] = jnp.zeros_like(state)

        diff, pos = _ret_consts(d)
        cosv, sinv = cos_ref[...], sin_ref[...]
        for h in range(nh):
            st = state[h]
            st_ref[h] = st
            lgt = jnp.full((1, 1), dec_ref[d, h], F32)
            q = _rope(q_ref[:, h * dk:(h + 1) * dk], cosv, sinv) * scale
            k = _rope(k_ref[:, h * dk:(h + 1) * dk], cosv, sinv)
            out, ns = _ret_step(lgt, st, q, k, v_ref[:, h * dv:(h + 1) * dv], diff, pos)
            o_ref[:, h * dv:(h + 1) * dv] = out
            state[h] = ns

    grid_spec = pltpu.PrefetchScalarGridSpec(
        num_scalar_prefetch=1, grid=(2, nch),
        in_specs=[pl.BlockSpec(memory_space=pltpu.SMEM),
                  pl.BlockSpec((c, wqk), lambda d, s, o: (o[d, s], qb)),
                  pl.BlockSpec((c, wqk), lambda d, s, o: (o[d, s], kb)),
                  pl.BlockSpec((c, wv), lambda d, s, o: (o[d, s], vb)),
                  pl.BlockSpec((c, dk), lambda d, s, o: (o[d, s], 0)),
                  pl.BlockSpec((c, dk), lambda d, s, o: (o[d, s], 0))],
        out_specs=[pl.BlockSpec((None, c, wv), lambda d, s, o: (d, o[d, s], 0)),
                   pl.BlockSpec((None, nh, None, dk, dv), lambda d, s, o: (d, 0, s, 0, 0))],
        scratch_shapes=[pltpu.VMEM((nh, dk, dv), F32)])
    return pl.pallas_call(
        body, name=name, grid_spec=grid_spec,
        out_shape=[jax.ShapeDtypeStruct((2, t, wv), F32), jax.ShapeDtypeStruct((2, nh, nch, dk, dv), F32)],
        compiler_params=_params(("arbitrary", "arbitrary")))(order, decay, p, p, p, cos, sin)


def _ret_bwd(p, cos, sin, decay, order, states, do, name):
    t = p.shape[0]
    c, dk, dv, nh = RET_CHUNK, RET_DK, RET_DV, RET_HEADS
    nch = t // c
    off = _offsets()
    wqk, wv = nh * dk, nh * dv
    qb, kb, vb = off["q"] // wqk, off["k"] // wqk, off["v"] // wv
    scale = RET_DK ** -0.5

    def body(ord_ref, dec_ref, q_ref, k_ref, v_ref, cos_ref, sin_ref, st_ref, do_ref,
             dq_ref, dk_ref, dv_ref, dd_ref, dstate):
        d, s = pl.program_id(0), pl.program_id(1)

        @pl.when(s == 0)
        def _():
            dstate[...] = jnp.zeros_like(dstate)
            dd_ref[...] = jnp.zeros_like(dd_ref)

        diff, pos = _ret_consts(d)
        cosv, sinv = cos_ref[...], sin_ref[...]
        for h in range(nh):
            qk, vv = slice(h * dk, (h + 1) * dk), slice(h * dv, (h + 1) * dv)
            lgt = jnp.full((1, 1), dec_ref[d, h], F32)
            q = _rope(q_ref[:, qk], cosv, sinv) * scale
            k = _rope(k_ref[:, qk], cosv, sinv)
            _, vjp = jax.vjp(lambda a, b, cq, ck, cv: _ret_step(a, b, cq, ck, cv, diff, pos),
                             lgt, st_ref[h], q, k, v_ref[:, vv])
            dlgt, dst, dq, dkk, dvv = vjp((do_ref[:, vv], dstate[h]))
            dstate[h] = dst
            dq_ref[:, qk] = _rope_t(dq * scale, cosv, sinv)
            dk_ref[:, qk] = _rope_t(dkk, cosv, sinv)
            dv_ref[:, vv] = dvv
            dd_ref[h] += jnp.broadcast_to(dlgt, (SUBLANES, LANES))

    rev = lambda o, d, s: o[d, nch - 1 - s]
    grid_spec = pltpu.PrefetchScalarGridSpec(
        num_scalar_prefetch=1, grid=(2, nch),
        in_specs=[pl.BlockSpec(memory_space=pltpu.SMEM),
                  pl.BlockSpec((c, wqk), lambda d, s, o: (rev(o, d, s), qb)),
                  pl.BlockSpec((c, wqk), lambda d, s, o: (rev(o, d, s), kb)),
                  pl.BlockSpec((c, wv), lambda d, s, o: (rev(o, d, s), vb)),
                  pl.BlockSpec((c, dk), lambda d, s, o: (rev(o, d, s), 0)),
                  pl.BlockSpec((c, dk), lambda d, s, o: (rev(o, d, s), 0)),
                  pl.BlockSpec((None, nh, None, dk, dv), lambda d, s, o: (d, 0, nch - 1 - s, 0, 0)),
                  pl.BlockSpec((c, wv), lambda d, s, o: (rev(o, d, s), 0))],
        out_specs=[pl.BlockSpec((None, c, wqk), lambda d, s, o: (d, rev(o, d, s), 0)),
                   pl.BlockSpec((None, c, wqk), lambda d, s, o: (d, rev(o, d, s), 0)),
                   pl.BlockSpec((None, c, wv), lambda d, s, o: (d, rev(o, d, s), 0)),
                   pl.BlockSpec((None, nh, SUBLANES, LANES), lambda d, s, o: (d, 0, 0, 0))],
        scratch_shapes=[pltpu.VMEM((nh, dk, dv), F32)])
    return pl.pallas_call(
        body, name=name, grid_spec=grid_spec,
        out_shape=[jax.ShapeDtypeStruct((2, t, wqk), F32), jax.ShapeDtypeStruct((2, t, wqk), F32),
                   jax.ShapeDtypeStruct((2, t, wv), F32), jax.ShapeDtypeStruct((2, nh, SUBLANES, LANES), F32)],
        compiler_params=_params(("arbitrary", "arbitrary")))(order, decay, p, p, p, cos, sin, states, do)


def _ggn_head(of, ob, gate, g):
    o = of + ob
    mu = jnp.mean(o, axis=-1, keepdims=True)
    var = jnp.mean(jnp.square(o - mu), axis=-1, keepdims=True)
    return ((o - mu) * lax.rsqrt(var + EPS) * g) * _silu(gate)


def _ggn_fwd(o2, p, gn_g, name):
    t = p.shape[0]
    tm, w, dv = _tm(), _ret_w(), RET_DV
    gb = _offsets()["g"] // w

    def body(o_ref, gate_ref, g_ref, out_ref):
        for h in range(RET_HEADS):
            sl = slice(h * dv, (h + 1) * dv)
            out_ref[:, sl] = _ggn_head(o_ref[0, :, sl], o_ref[1, :, sl], gate_ref[:, sl], g_ref[:, sl]).astype(BF16)

    return pl.pallas_call(
        body, name=name, grid=(t // tm,),
        in_specs=[pl.BlockSpec((2, tm, w), lambda i: (0, i, 0)), pl.BlockSpec((tm, w), lambda i: (i, gb)),
                  pl.BlockSpec((1, w), lambda i: (0, 0))],
        out_specs=pl.BlockSpec((tm, w), lambda i: (i, 0)), out_shape=jax.ShapeDtypeStruct((t, w), BF16),
        compiler_params=_params(("parallel",)))(o2, p, gn_g)


def _ggn_bwd(o2, p, gn_g, dmix, name):
    t = p.shape[0]
    tm, w, dv = _tm(), _ret_w(), RET_DV
    gb = _offsets()["g"] // w

    def body(o_ref, gate_ref, g_ref, dy_ref, do_ref, dgate_ref, dg_ref):
        i = pl.program_id(0)

        @pl.when(i == 0)
        def _():
            dg_ref[...] = jnp.zeros_like(dg_ref)

        for h in range(RET_HEADS):
            sl = slice(h * dv, (h + 1) * dv)
            _, vjp = jax.vjp(_ggn_head, o_ref[0, :, sl], o_ref[1, :, sl], gate_ref[:, sl], g_ref[:, sl])
            do, _, dgate, dg = vjp(dy_ref[:, sl])
            do_ref[:, sl] = do
            dgate_ref[:, sl] = dgate
            dg_ref[:, sl] += dg

    row = pl.BlockSpec((tm, w), lambda i: (i, 0))
    return pl.pallas_call(
        body, name=name, grid=(t // tm,),
        in_specs=[pl.BlockSpec((2, tm, w), lambda i: (0, i, 0)), pl.BlockSpec((tm, w), lambda i: (i, gb)),
                  pl.BlockSpec((1, w), lambda i: (0, 0)), row],
        out_specs=[row, row, pl.BlockSpec((1, w), lambda i: (0, 0))],
        out_shape=[jax.ShapeDtypeStruct((t, w), F32), jax.ShapeDtypeStruct((t, w), F32),
                   jax.ShapeDtypeStruct((1, w), F32)],
        compiler_params=_params(("arbitrary",)))(o2, p, gn_g, dmix)


def _halo(k):
    return SUBLANES * ((k // 2 + SUBLANES - 1) // SUBLANES)


def _halo_specs(width, colblock, h, tm):
    r = tm // h
    return [pl.BlockSpec((h, width), lambda i, *_: (jnp.maximum(i * r - 1, 0), colblock(*_))),
            pl.BlockSpec((tm, width), lambda i, *_: (i, colblock(*_))),
            pl.BlockSpec((h, width), lambda i, *_: (jnp.minimum((i + 1) * r, (_t_rows() // h) - 1), colblock(*_)))]


def _fill_ext(ext_ref, prev, cur, nxt, i, h, tm):
    nt = _t_rows() // tm
    ext_ref[0:h, :] = jnp.where(i >= 2, prev, 0.0)
    ext_ref[h:h + tm, :] = cur
    ext_ref[h + tm:h + tm + h, :] = jnp.where((i >= 1) & (i <= nt - 2), nxt, 0.0)


def _corr(ext_ref, w_ref, k, h, tm, flip):
    pad = k // 2
    acc = None
    for kk in range(k):
        o = h + (pad - kk if flip else kk - pad)
        term = w_ref[kk:kk + 1, :] * ext_ref[o:o + tm, :]
        acc = term if acc is None else acc + term
    return acc


def _conv_post(u2, ln_g, ln_b, pw):
    mu = jnp.mean(u2, axis=-1, keepdims=True)
    var = jnp.mean(jnp.square(u2 - mu), axis=-1, keepdims=True)
    y = (u2 - mu) * lax.rsqrt(var + EPS) * ln_g + ln_b
    return dot_nn(_silu(y), pw)


def _conv_fwd(p, dw_w, dw_b, ln_g, ln_b, pw, name):
    t = p.shape[0]
    tm, w, k = _tm(), CONV_W, CONV_K
    h = _halo(k)
    off = _offsets()
    ab, bb = off["a"] // w, off["b"] // w

    def body(ap, ac, an, bp, bc, bn, w_ref, b_ref, g_ref, beta_ref, pw_ref, u2_ref, out_ref, ext):
        i = pl.program_id(0)
        glu = lambda a, b: a * _sigmoid(b)
        _fill_ext(ext, glu(ap[...], bp[...]), glu(ac[...], bc[...]), glu(an[...], bn[...]), i, h, tm)
        u2 = _corr(ext, w_ref, k, h, tm, False) + b_ref[...]
        u2_ref[...] = u2
        out_ref[...] = _conv_post(u2, g_ref[...], beta_ref[...], pw_ref[...]).astype(BF16)

    vec = pl.BlockSpec((1, w), lambda i: (0, 0))
    row = pl.BlockSpec((tm, w), lambda i: (i, 0))
    return pl.pallas_call(
        body, name=name, grid=(t // tm,),
        in_specs=_halo_specs(w, lambda: ab, h, tm) + _halo_specs(w, lambda: bb, h, tm)
        + [pl.BlockSpec((k, w), lambda i: (0, 0)), vec, vec, vec, pl.BlockSpec((w, w), lambda i: (0, 0))],
        out_specs=[row, row],
        out_shape=[jax.ShapeDtypeStruct((t, w), F32), jax.ShapeDtypeStruct((t, w), BF16)],
        scratch_shapes=[pltpu.VMEM((tm + 2 * h, w), F32)],
        compiler_params=_params(("parallel",)))(p, p, p, p, p, p, dw_w, dw_b, ln_g, ln_b, pw)


def _conv_bwd1(u2, dmix, ln_g, ln_b, pw, name):
    t = u2.shape[0]
    tm, w = _tm(), CONV_W
    cb = _ret_w() // w

    def body(u2_ref, dy_ref, g_ref, beta_ref, pw_ref, du2_ref, dg_ref, db_ref, dpw_ref):
        i = pl.program_id(0)

        @pl.when(i == 0)
        def _():
            dg_ref[...] = jnp.zeros_like(dg_ref)
            db_ref[...] = jnp.zeros_like(db_ref)
            dpw_ref[...] = jnp.zeros_like(dpw_ref)

        _, vjp = jax.vjp(_conv_post, u2_ref[...], g_ref[...], beta_ref[...], pw_ref[...])
        du2, dg, db, dpw = vjp(dy_ref[...])
        du2_ref[...] = du2
        dg_ref[...] += dg
        db_ref[...] += db
        dpw_ref[...] += dpw

    vec = pl.BlockSpec((1, w), lambda i: (0, 0))
    row = pl.BlockSpec((tm, w), lambda i: (i, 0))
    mat = pl.BlockSpec((w, w), lambda i: (0, 0))
    return pl.pallas_call(
        body, name=name, grid=(t // tm,),
        in_specs=[row, pl.BlockSpec((tm, w), lambda i: (i, cb)), vec, vec, mat],
        out_specs=[row, vec, vec, mat],
        out_shape=[jax.ShapeDtypeStruct((t, w), F32), jax.ShapeDtypeStruct((1, w), F32),
                   jax.ShapeDtypeStruct((1, w), F32), jax.ShapeDtypeStruct((w, w), F32)],
        compiler_params=_params(("arbitrary",)))(u2, dmix, ln_g, ln_b, pw)


def _conv_bwd2(du2, p, dw_w, name):
    t = p.shape[0]
    tm, w, k = _tm(), CONV_W, CONV_K
    h = _halo(k)
    pad = k // 2
    off = _offsets()
    ab, bb = off["a"] // w, off["b"] // w

    def body(dp, dc, dn, ap, ac, an, bp, bc, bn, w_ref, da_ref, db_ref, dw_ref, dbias_ref, ext_d, ext_u):
        i = pl.program_id(0)

        @pl.when(i == 0)
        def _():
            dw_ref[...] = jnp.zeros_like(dw_ref)
            dbias_ref[...] = jnp.zeros_like(dbias_ref)

        glu = lambda a, b: a * _sigmoid(b)
        a, b, d = ac[...], bc[...], dc[...]
        _fill_ext(ext_d, dp[...], d, dn[...], i, h, tm)
        _fill_ext(ext_u, glu(ap[...], bp[...]), glu(a, b), glu(an[...], bn[...]), i, h, tm)
        du = _corr(ext_d, w_ref, k, h, tm, True)
        sg = _sigmoid(b)
        da_ref[...] = du * sg
        db_ref[...] = du * a * sg * (1.0 - sg)
        dbias_ref[...] += jnp.sum(d, axis=0, keepdims=True)
        for kk in range(k):
            o = h + kk - pad
            dw_ref[kk:kk + 1, :] += jnp.sum(d * ext_u[o:o + tm, :], axis=0, keepdims=True)

    vec = pl.BlockSpec((1, w), lambda i: (0, 0))
    row = pl.BlockSpec((tm, w), lambda i: (i, 0))
    kw = pl.BlockSpec((k, w), lambda i: (0, 0))
    return _call(
        body, name=name, grid=(t // tm,),
        in_specs=_halo_specs(w, lambda: 0, h, tm) + _halo_specs(w, lambda: ab, h, tm)
        + _halo_specs(w, lambda: bb, h, tm) + [kw],
        out_specs=[row, row, kw, vec],
        out_shape=[jax.ShapeDtypeStruct((t, w), F32), jax.ShapeDtypeStruct((t, w), F32),
                   jax.ShapeDtypeStruct((k, w), F32), jax.ShapeDtypeStruct((1, w), F32)],
        scratch=[pltpu.VMEM((tm + 2 * h, w), F32), pltpu.VMEM((tm + 2 * h, w), F32)],
        sem=("arbitrary",), args=(du2, du2, du2, p, p, p, p, p, p, dw_w))


def _ffn_tc():
    return _tile(D_FF, 512, LANES)


def _ffn_act_fwd(u, dw_w, dw_b, name):
    t = u.shape[0]
    tm, k, tc = _tm(), FFN_K, _ffn_tc()
    h = _halo(k)
    nj = D_FF // tc

    def body(vp, vc, vn, gp, gc, gn, wv, wg, bv, bg, out_ref, out_t_ref, ext_v, ext_g):
        i = pl.program_id(0)
        _fill_ext(ext_v, vp[...], vc[...], vn[...], i, h, tm)
        _fill_ext(ext_g, gp[...], gc[...], gn[...], i, h, tm)
        for r0 in range(0, tm, ROW_CHUNK):
            val = _corr(ext_v, wv, k, h + r0, ROW_CHUNK, False) + bv[...]
            gate = _corr(ext_g, wg, k, h + r0, ROW_CHUNK, False) + bg[...]
            out_ref[r0:r0 + ROW_CHUNK, :] = (_silu(gate) * val).astype(BF16)
        out_t_ref[...] = out_ref[...].T

    wspec = lambda s: pl.BlockSpec((k, tc), lambda i, j: (0, j + s))
    bspec = lambda s: pl.BlockSpec((1, tc), lambda i, j: (0, j + s))
    return pl.pallas_call(
        body, name=name, grid=(t // tm, nj),
        in_specs=_halo_specs(tc, lambda j: j, h, tm) + _halo_specs(tc, lambda j: j + nj, h, tm)
        + [wspec(0), wspec(nj), bspec(0), bspec(nj)],
        out_specs=[pl.BlockSpec((tm, tc), lambda i, j: (i, j)), pl.BlockSpec((tc, tm), lambda i, j: (j, i))],
        out_shape=[jax.ShapeDtypeStruct((t, D_FF), BF16), jax.ShapeDtypeStruct((D_FF, t), BF16)],
        scratch_shapes=[pltpu.VMEM((tm + 2 * h, tc), F32), pltpu.VMEM((tm + 2 * h, tc), F32)],
        compiler_params=_params(("parallel", "parallel")))(u, u, u, u, u, u, dw_w, dw_w, dw_b, dw_b)


def _ffn_act_bwd1(u, da, dw_w, dw_b, name):
    t = u.shape[0]
    tm, k, tc = _tm(), FFN_K, _ffn_tc()
    h = _halo(k)
    nj = D_FF // tc

    def body(vp, vc, vn, gp, gc, gn, wv, wg, bv, bg, da_ref, dv_ref, dg_ref, ext_v, ext_g):
        i = pl.program_id(0)
        _fill_ext(ext_v, vp[...], vc[...], vn[...], i, h, tm)
        _fill_ext(ext_g, gp[...], gc[...], gn[...], i, h, tm)
        for r0 in range(0, tm, ROW_CHUNK):
            rows = slice(r0, r0 + ROW_CHUNK)
            val = _corr(ext_v, wv, k, h + r0, ROW_CHUNK, False) + bv[...]
            gate = _corr(ext_g, wg, k, h + r0, ROW_CHUNK, False) + bg[...]
            _, vjp = jax.vjp(lambda a, b: _silu(b) * a, val, gate)
            dval, dgate = vjp(da_ref[rows, :])
            dv_ref[rows, :] = dval
            dg_ref[rows, :] = dgate

    wspec = lambda s: pl.BlockSpec((k, tc), lambda i, j: (0, j + s))
    bspec = lambda s: pl.BlockSpec((1, tc), lambda i, j: (0, j + s))
    dc = pl.pallas_call(
        body, name=name, grid=(t // tm, nj),
        in_specs=_halo_specs(tc, lambda j: j, h, tm) + _halo_specs(tc, lambda j: j + nj, h, tm)
        + [wspec(0), wspec(nj), bspec(0), bspec(nj), pl.BlockSpec((tm, tc), lambda i, j: (i, j))],
        out_specs=[pl.BlockSpec((tm, tc), lambda i, j: (i, j)), pl.BlockSpec((tm, tc), lambda i, j: (i, j))],
        out_shape=[jax.ShapeDtypeStruct((t, D_FF), F32), jax.ShapeDtypeStruct((t, D_FF), F32)],
        scratch_shapes=[pltpu.VMEM((tm + 2 * h, tc), F32), pltpu.VMEM((tm + 2 * h, tc), F32)],
        compiler_params=_params(("parallel", "parallel")))(u, u, u, u, u, u, dw_w, dw_w, dw_b, dw_b, da)
    return dc


def _dwconv_bwd(dcv, dcg, u, dw_w, name):
    t = u.shape[0]
    tm, k, tc = _tm(), FFN_K, _ffn_tc()
    h = _halo(k)
    pad = k // 2
    nj = D_FF // tc

    def body(vp, vc, vn, gp, gc, gn, up, uc, un, w_ref, du_ref, dw_ref, dbias_ref, ext_d, ext_u):
        jj, i = pl.program_id(0), pl.program_id(1)

        @pl.when(i == 0)
        def _():
            dw_ref[...] = jnp.zeros_like(dw_ref)
            dbias_ref[...] = jnp.zeros_like(dbias_ref)

        @pl.when(jj < nj)
        def _():
            _fill_ext(ext_d, vp[...], vc[...], vn[...], i, h, tm)

        @pl.when(jj >= nj)
        def _():
            _fill_ext(ext_d, gp[...], gc[...], gn[...], i, h, tm)

        _fill_ext(ext_u, up[...], uc[...], un[...], i, h, tm)
        acc_b = jnp.zeros((ROW_CHUNK, tc), F32)
        acc_w = [jnp.zeros((ROW_CHUNK, tc), F32) for _ in range(k)]
        for r0 in range(0, tm, ROW_CHUNK):
            d = ext_d[h + r0:h + r0 + ROW_CHUNK, :]
            du_ref[r0:r0 + ROW_CHUNK, :] = _corr(ext_d, w_ref, k, h + r0, ROW_CHUNK, True).astype(BF16)
            acc_b = acc_b + d
            for kk in range(k):
                o = h + r0 + kk - pad
                acc_w[kk] = acc_w[kk] + d * ext_u[o:o + ROW_CHUNK, :]
        dbias_ref[...] += jnp.sum(acc_b, axis=0, keepdims=True)
        for kk in range(k):
            dw_ref[kk:kk + 1, :] += jnp.sum(acc_w[kk], axis=0, keepdims=True)

    def hs(cb, live):
        r = tm // h
        row = lambda j, i: jnp.where(live(j), i, 0)
        return [pl.BlockSpec((h, tc), lambda j, i: (jnp.maximum(row(j, i) * r - 1, 0), cb(j))),
                pl.BlockSpec((tm, tc), lambda j, i: (row(j, i), cb(j))),
                pl.BlockSpec((h, tc), lambda j, i: (jnp.minimum((row(j, i) + 1) * r, (_t_rows() // h) - 1), cb(j)))]

    return pl.pallas_call(
        body, name=name, grid=(2 * nj, t // tm),
        in_specs=hs(lambda j: jnp.minimum(j, nj - 1), lambda j: j < nj)
        + hs(lambda j: jnp.maximum(j - nj, 0), lambda j: j >= nj)
        + hs(lambda j: j, lambda j: True) + [pl.BlockSpec((k, tc), lambda j, i: (0, j))],
        out_specs=[pl.BlockSpec((tm, tc), lambda j, i: (i, j)), pl.BlockSpec((k, tc), lambda j, i: (0, j)),
                   pl.BlockSpec((1, tc), lambda j, i: (0, j))],
        out_shape=[jax.ShapeDtypeStruct((t, 2 * D_FF), BF16), jax.ShapeDtypeStruct((k, 2 * D_FF), F32),
                   jax.ShapeDtypeStruct((1, 2 * D_FF), F32)],
        scratch_shapes=[pltpu.VMEM((tm + 2 * h, tc), F32), pltpu.VMEM((tm + 2 * h, tc), F32)],
        compiler_params=_params(("parallel", "arbitrary")))(dcv, dcv, dcv, dcg, dcg, dcg, u, u, u, dw_w)


def _na_geometry(rq):
    ncb = CTX_LEN // GRID_W
    rows_n = SEQ // GRID_W
    r = jnp.maximum(rq - ncb, 0)
    kstart = jnp.clip(r - NA_ROWS // 2, 0, rows_n - NA_ROWS)
    base = kstart - r + NA_ROWS - 1
    return rq >= ncb, kstart, base


def _na_core(q, kl, vl, kc, vc, bias, mask):
    qs = q * (NA_DH ** -0.5)
    s_l = jnp.where(mask, dot_nt(qs, kl) + bias, NEG)
    s_c = dot_nt(qs, kc)
    m = lax.stop_gradient(jnp.maximum(jnp.max(s_l, axis=1, keepdims=True), jnp.max(s_c, axis=1, keepdims=True)))
    e_l, e_c = jnp.exp(s_l - m), jnp.exp(s_c - m)
    inv = 1.0 / (jnp.sum(e_l, axis=1, keepdims=True) + jnp.sum(e_c, axis=1, keepdims=True))
    return dot_nn(e_l * inv, vl) + dot_nn(e_c * inv, vc)


def _na_mask(is_lat):
    nl = NA_ROWS * GRID_W
    q = lax.broadcasted_iota(jnp.int32, (GRID_W, nl), 0)
    w = lax.broadcasted_iota(jnp.int32, (GRID_W, nl), 1) % GRID_W
    cs = jnp.clip(q - NA_COLS // 2, 0, GRID_W - NA_COLS)
    return (w >= cs) & (w < cs + NA_COLS) & is_lat


def _na_bias(rb_ref):
    assert 2 * GRID_W == LANES
    lane = lax.broadcasted_iota(jnp.int32, (GRID_W, LANES), 1)
    tiles = []
    for kp in range(NA_ROWS // 2):
        ev = jnp.broadcast_to(rb_ref[2 * kp:2 * kp + 1, :], (GRID_W, LANES))
        od = jnp.broadcast_to(rb_ref[2 * kp + 1:2 * kp + 2, :], (GRID_W, LANES))
        ev = pltpu.roll(ev, LANES - (NA_COLS - 1), 1, stride=1, stride_axis=0)
        od = pltpu.roll(od, LANES - (NA_COLS - 1) - GRID_W, 1, stride=1, stride_axis=0)
        tiles.append(jnp.where(lane < GRID_W, ev, od))
    return jnp.concatenate(tiles, axis=1)


def _na_dbias(dbias, drb_ref):
    qi = lax.broadcasted_iota(jnp.int32, (GRID_W, GRID_W), 0)
    qj = lax.broadcasted_iota(jnp.int32, (GRID_W, GRID_W), 1)
    flip = (qi + qj == GRID_W - 1).astype(F32)
    rev = lax.dot_general(flip, dbias, (((1,), (0,)), ((), ())), precision=lax.Precision.HIGHEST,
                          preferred_element_type=F32)
    lane = lax.broadcasted_iota(jnp.int32, (GRID_W, LANES), 1)
    s_ev = LANES - (GRID_W - NA_COLS)
    for kp in range(NA_ROWS // 2):
        tile = rev[:, kp * LANES:(kp + 1) * LANES]
        ev = pltpu.roll(jnp.where(lane < GRID_W, tile, 0.0), s_ev, 1, stride=1, stride_axis=0)
        od = pltpu.roll(jnp.where(lane >= GRID_W, tile, 0.0), s_ev - GRID_W, 1, stride=1, stride_axis=0)
        drb_ref[2 * kp:2 * kp + 1, :] += jnp.sum(ev, axis=0, keepdims=True)
        drb_ref[2 * kp + 1:2 * kp + 2, :] += jnp.sum(od, axis=0, keepdims=True)


def _na_hps():
    return 2 if NA_HEADS % 2 == 0 else 1


def _na_specs(p_offsets):
    t = _t_rows()
    hps = _na_hps()
    wd = hps * NA_DH
    assert all(p_offsets[n] % wd == 0 for n in ("nq", "nk", "nv"))
    qb, kb, vb = (p_offsets[n] // wd for n in ("nq", "nk", "nv"))
    return [pl.BlockSpec((GRID_W, wd), lambda h, r: (r, qb + h)),
            pl.BlockSpec((t, wd), lambda h, r: (0, kb + h)),
            pl.BlockSpec((t, wd), lambda h, r: (0, vb + h)),
            pl.BlockSpec((hps, None, NA_ROWS, LANES), lambda h, r: (h, _na_geometry(r)[2], 0, 0))]


def _na_fwd(p, rb, name):
    t = p.shape[0]
    dh, nl = NA_DH, NA_ROWS * GRID_W

    hps = _na_hps()

    def body(q_ref, k_ref, v_ref, rb_ref, out_ref):
        rq = pl.program_id(1)
        is_lat, kstart, _ = _na_geometry(rq)
        start = pl.multiple_of(CTX_LEN + kstart * GRID_W, GRID_W)
        mask = _na_mask(is_lat)
        for hh in range(hps):
            cols = slice(hh * dh, (hh + 1) * dh)
            out = _na_core(q_ref[:, cols], k_ref[pl.ds(start, nl), cols], v_ref[pl.ds(start, nl), cols],
                           k_ref[0:CTX_LEN, cols], v_ref[0:CTX_LEN, cols], _na_bias(rb_ref.at[hh]), mask)
            out_ref[:, cols] = out.astype(BF16)

    return _call(
        body, name=name, grid=(NA_HEADS // hps, t // GRID_W), in_specs=_na_specs(_offsets()),
        out_specs=[pl.BlockSpec((GRID_W, hps * dh), lambda h, r: (r, h))],
        out_shape=[jax.ShapeDtypeStruct((t, _na_w()), BF16)],
        sem=("parallel", "arbitrary"), args=(p, p, p, rb))[0]


def _na_bwd(p, rb, dmix, name):
    t = p.shape[0]
    dh, nl = NA_DH, NA_ROWS * GRID_W

    hps = _na_hps()
    wd = hps * dh
    assert ((_ret_w() + CONV_W) // dh) % hps == 0
    ob = (_ret_w() + CONV_W) // wd

    def body(q_ref, k_ref, v_ref, rb_ref, dy_ref, dq_ref, dk_ref, dv_ref, drb_ref):
        rq = pl.program_id(1)
        is_lat, kstart, base = _na_geometry(rq)
        _, _, prev_base = _na_geometry(rq - 1)
        start = pl.multiple_of(CTX_LEN + kstart * GRID_W, GRID_W)

        @pl.when(rq == 0)
        def _():
            dk_ref[...] = jnp.zeros_like(dk_ref)
            dv_ref[...] = jnp.zeros_like(dv_ref)

        @pl.when((rq == 0) | (base != prev_base))
        def _():
            drb_ref[...] = jnp.zeros_like(drb_ref)

        mask = _na_mask(is_lat)
        for hh in range(hps):
            cols = slice(hh * dh, (hh + 1) * dh)
            _, vjp = jax.vjp(lambda *a: _na_core(*a, mask), q_ref[:, cols], k_ref[pl.ds(start, nl), cols],
                             v_ref[pl.ds(start, nl), cols], k_ref[0:CTX_LEN, cols], v_ref[0:CTX_LEN, cols],
                             _na_bias(rb_ref.at[hh]))
            dq, dkl, dvl, dkc, dvc, dbias = vjp(dy_ref[:, cols])
            dq_ref[:, cols] = dq
            dk_ref[pl.ds(start, nl), cols] += dkl
            dv_ref[pl.ds(start, nl), cols] += dvl
            dk_ref[0:CTX_LEN, cols] += dkc
            dv_ref[0:CTX_LEN, cols] += dvc
            _na_dbias(dbias, drb_ref.at[hh])

    return _call(
        body, name=name, grid=(NA_HEADS // hps, t // GRID_W),
        in_specs=_na_specs(_offsets()) + [pl.BlockSpec((GRID_W, wd), lambda h, r: (r, ob + h))],
        out_specs=[pl.BlockSpec((GRID_W, wd), lambda h, r: (r, h)), pl.BlockSpec((t, wd), lambda h, r: (0, h)),
                   pl.BlockSpec((t, wd), lambda h, r: (0, h)),
                   pl.BlockSpec((hps, None, NA_ROWS, LANES), lambda h, r: (h, _na_geometry(r)[2], 0, 0))],
        out_shape=[jax.ShapeDtypeStruct((t, _na_w()), F32), jax.ShapeDtypeStruct((t, _na_w()), F32),
                   jax.ShapeDtypeStruct((t, _na_w()), F32),
                   jax.ShapeDtypeStruct((NA_HEADS, NA_ROWS, NA_ROWS, LANES), F32)],
        sem=("parallel", "arbitrary"), args=(p, p, p, rb, dmix))


def _rpb_select():
    sel = np.zeros((2 * NA_ROWS - 1, NA_ROWS * NA_ROWS), np.float32)
    for b in range(NA_ROWS):
        for kh in range(NA_ROWS):
            sel[b + kh, b * NA_ROWS + kh] = 1.0
    return jnp.asarray(sel)


def _rpb_rows(rpb):
    pad = jnp.pad(rpb, ((0, 0), (0, 0), (0, LANES - (2 * NA_COLS - 1))))
    rows = jnp.einsum("rk,hrc->hkc", _rpb_select(), pad, precision=lax.Precision.HIGHEST)
    return rows.reshape(NA_HEADS, NA_ROWS, NA_ROWS, LANES)


def _rpb_rows_t(drb):
    flat = drb.reshape(NA_HEADS, NA_ROWS * NA_ROWS, LANES)
    out = jnp.einsum("rk,hkc->hrc", _rpb_select(), flat, precision=lax.Precision.HIGHEST)
    return out[:, :, :2 * NA_COLS - 1]


def _assemble_dp(dqr, dkr, dvr, dgate, da, db, dnq, dnk, dnv, name):
    t = dgate.shape[0]
    tm = _tm()
    off = _offsets()
    sizes = dict(q=_ret_qk_w(), k=_ret_qk_w(), v=_ret_w(), g=_ret_w(), a=CONV_W, b=CONV_W, nq=_na_w(), nk=_na_w(), nv=_na_w())

    def body(q_ref, k_ref, v_ref, g_ref, a_ref, b_ref, nq_ref, nk_ref, nv_ref, o_ref):
        def put(n, val):
            o_ref[:, off[n]:off[n] + sizes[n]] = val.astype(BF16)

        put("q", q_ref[0] + q_ref[1])
        put("k", k_ref[0] + k_ref[1])
        put("v", v_ref[0] + v_ref[1])
        put("g", g_ref[...])
        put("a", a_ref[...])
        put("b", b_ref[...])
        put("nq", nq_ref[...])
        put("nk", nk_ref[...])
        put("nv", nv_ref[...])

    two = lambda w: pl.BlockSpec((2, tm, w), lambda i: (0, i, 0))
    one = lambda w: pl.BlockSpec((tm, w), lambda i: (i, 0))
    return pl.pallas_call(
        body, name=name, grid=(t // tm,),
        in_specs=[two(sizes["q"]), two(sizes["k"]), two(sizes["v"]), one(sizes["g"]), one(CONV_W), one(CONV_W),
                  one(_na_w()), one(_na_w()), one(_na_w())],
        out_specs=one(_d_in()), out_shape=jax.ShapeDtypeStruct((t, _d_in()), BF16),
        compiler_params=_params(("parallel",)))(dqr, dkr, dvr, dgate, da, db, dnq, dnk, dnv)


def _adamw(w, m, v, gs, name):
    nl, r, c = w.shape
    stacked = not isinstance(gs, (list, tuple))
    if stacked:
        gs = [gs]
    assert stacked or len(gs) == nl
    g_n = gs[0].shape[-3]
    block_bytes = 2 * 1024 * 1024
    rows = min(block_bytes // (4 * c), block_bytes // (g_n * c * gs[0].dtype.itemsize))
    tr = _tile(r, max(2 * SUBLANES, rows // (2 * SUBLANES) * (2 * SUBLANES)), 2 * SUBLANES)
    nt = r // tr
    c1 = 1.0 - ADAM_B1 ** ADAM_STEP
    c2 = 1.0 - ADAM_B2 ** ADAM_STEP

    def body(w_ref, m_ref, v_ref, *rest):
        g_refs, (go_ref, d_ref, mo_ref, vo_ref) = rest[:len(gs)], rest[len(gs):]
        layer = pl.program_id(0)
        for ll in range(len(gs)):
            @pl.when(jnp.logical_or(stacked, layer == ll))
            def _():
                g_ref = g_refs[ll]
                g = g_ref[0].astype(F32)
                for j in range(1, g_n):
                    g = g + g_ref[j].astype(F32)
                mn = ADAM_B1 * m_ref[...] + (1.0 - ADAM_B1) * g
                vn = ADAM_B2 * v_ref[...] + (1.0 - ADAM_B2) * (g * g)
                m_hat = mn / c1
                v_hat = vn / c2
                go_ref[...] = g
                d_ref[...] = -ADAM_LR * (m_hat / (jnp.sqrt(v_hat) + ADAM_EPS) + ADAM_WD * w_ref[...])
                mo_ref[...] = mn
                vo_ref[...] = vn

    def g_spec(ll):
        if stacked:
            return pl.BlockSpec((None, g_n, tr, c), lambda l, i: (l, 0, i, 0))
        return pl.BlockSpec((g_n, tr, c), lambda l, i: (0, jnp.where(l == ll, i, jnp.where(l < ll, 0, nt - 1)), 0))

    blk = pl.BlockSpec((None, tr, c), lambda l, i: (l, i, 0))
    sds = jax.ShapeDtypeStruct((nl, r, c), F32)
    return _call(
        body, name=name, grid=(nl, nt),
        in_specs=[blk, blk, blk] + [g_spec(ll) for ll in range(len(gs))],
        out_specs=[blk, blk, blk, blk], out_shape=[sds, sds, sds, sds],
        sem=("arbitrary", "arbitrary"), args=(w, m, v, *gs))


def _sum_devices(g, name):
    _, r, c = g.shape
    tr = _tile(r, 512, SUBLANES)

    def body(g_ref, o_ref):
        acc = g_ref[0]
        for j in range(1, N_DEV):
            acc = acc + g_ref[j]
        o_ref[...] = acc

    return pl.pallas_call(body, name=name, grid=(r // tr,), in_specs=[pl.BlockSpec((N_DEV, tr, c), lambda i: (0, i, 0))],
                          out_specs=pl.BlockSpec((tr, c), lambda i: (i, 0)), out_shape=jax.ShapeDtypeStruct((r, c), F32),
                          compiler_params=_params(("parallel",)))(g)


def _ada_fwd(c16, w_ada, b_shard, name):
    nl, d, cs = w_ada.shape
    tk = _tile(d, 512, LANES)
    nk = d // tk

    def body(c_ref, w_ref, b_ref, o_ref):
        kk = pl.program_id(1)

        @pl.when(kk == 0)
        def _():
            o_ref[...] = jnp.broadcast_to(b_ref[...], o_ref.shape)

        o_ref[...] += _dg(_silu(c_ref[...]), w_ref[...], 1, 0)

    return pl.pallas_call(
        body, name=name, grid=(nl, nk),
        in_specs=[pl.BlockSpec((16, tk), lambda l, kk: (0, kk)), pl.BlockSpec((None, tk, cs), lambda l, kk: (l, kk, 0)),
                  pl.BlockSpec((None, 1, cs), lambda l, kk: (l, 0, 0))],
        out_specs=pl.BlockSpec((None, 16, cs), lambda l, kk: (l, 0, 0)),
        out_shape=jax.ShapeDtypeStruct((nl, 16, cs), F32),
        compiler_params=_params(("parallel", "arbitrary")))(c16, w_ada, b_shard)


def _ada_bwd(c16, dm16, w_ada, name):
    nl, d, cs = w_ada.shape
    td = _tile(d, 512, LANES)

    def body(c_ref, dm_ref, w_ref, gw_ref, dc_ref):
        cv = c_ref[...]
        s, vjp = jax.vjp(_silu, cv)
        gw_ref[...] = _dg(s, dm_ref[...], 0, 0)
        ds = _dg(dm_ref[...], w_ref[...], 1, 1)
        dc_ref[...] = vjp(ds)[0]

    return pl.pallas_call(
        body, name=name, grid=(nl, d // td),
        in_specs=[pl.BlockSpec((16, td), lambda l, i: (0, i)), pl.BlockSpec((None, 16, cs), lambda l, i: (l, 0, 0)),
                  pl.BlockSpec((None, td, cs), lambda l, i: (l, i, 0))],
        out_specs=[pl.BlockSpec((None, td, cs), lambda l, i: (l, i, 0)), pl.BlockSpec((None, 16, td), lambda l, i: (l, 0, i))],
        out_shape=[jax.ShapeDtypeStruct((nl, d, cs), F32), jax.ShapeDtypeStruct((nl, 16, d), F32)],
        compiler_params=_params(("parallel", "parallel")))(c16, dm16, w_ada)


def _pack_rows(shape):
    n = int(np.prod(shape))
    return SUBLANES * (-(-n // (LANES * SUBLANES)))


def _pack(arrays, row_align):
    parts, total = [], 0
    for a in arrays:
        flat = a.reshape(-1).astype(F32)
        rows = _pack_rows(a.shape)
        total += rows
        parts += [flat, jnp.zeros((rows * LANES - flat.shape[0],), F32)]
    parts.append(jnp.zeros(((-total % row_align) * LANES,), F32))
    return jnp.concatenate([p for p in parts if p.shape[0]]).reshape(-1, LANES)


def _unpack(packed, shapes):
    out, r = [], 0
    for s in shapes:
        rows = _pack_rows(s)
        out.append(packed[r:r + rows].reshape(-1)[:int(np.prod(s))].reshape(s))
        r += rows
    return out


def _rope_tables():
    half, nf = RET_DK // 2, RET_DK // 4
    pos = jnp.arange(SEQ)
    row = (pos // GRID_W).astype(F32)
    col = (pos % GRID_W).astype(F32)
    inv = ROPE_BASE ** (-jnp.arange(nf, dtype=F32) / nf)
    ar, ac = row[:, None] * inv[None, :], col[:, None] * inv[None, :]
    cos = jnp.concatenate([jnp.cos(ar), jnp.cos(ar), jnp.cos(ac), jnp.cos(ac)], axis=-1)
    sin = jnp.concatenate([-jnp.sin(ar), jnp.sin(ar), -jnp.sin(ac), jnp.sin(ac)], axis=-1)
    cos = jnp.concatenate([jnp.ones((CTX_LEN, RET_DK), F32), cos], axis=0)
    sin = jnp.concatenate([jnp.zeros((CTX_LEN, RET_DK), F32), sin], axis=0)
    return cos, sin


def _layer_fwd(l, x, mod4, w, cst, arrived):
    n = lambda s: f"l{l}_{s}"
    d = D_MODEL
    h1, h1_t = _normmod_fwd(x, w["norm1_g"], mod4, 0, n("norm1"))
    w["w_in"] = _cols_from_shards(arrived("w_in", h1), n("w_in_cols"))
    p = _mm(h1, w["w_in"], n("proj_in"))
    o2, states = _ret_fwd(p, cst["cos"], cst["sin"], w["ret_decay"], cst["order"], n("ret_fwd"))
    ret_out = _ggn_fwd(o2, p, w["ret_gn_g"], n("ret_gn"))
    u2, conv_out = _conv_fwd(p, w["conv_dw_w"], w["conv_dw_b"], w["conv_ln_g"], w["conv_ln_b"], w["conv_pw"], n("conv_fwd"))
    na_out = _na_fwd(p, w["rb"], n("na_fwd"))
    mix = jnp.concatenate([ret_out, conv_out, na_out], axis=1)
    w["w_out"] = arrived("w_out", mix).reshape(_d_mix(), d)
    g1 = _mm(mix, w["w_out"], n("proj_out"))
    x1 = _gate_res_fwd(x, g1, mod4, 2, n("res1"))
    h2, h2_t = _normmod_fwd(x1, w["norm2_g"], mod4, 1, n("norm2"))
    w["ffn_up"] = arrived("ffn_up", h2)
    u = _mm(h2, w["ffn_up"], n("ffn_up"), b3=True)
    a, a_t = _ffn_act_fwd(u, w["ffn_dw_w"], w["ffn_dw_b"], n("ffn_act"))
    w["ffn_down"] = arrived("ffn_down", a).reshape(D_FF, d)
    f = _mm(a, w["ffn_down"], n("ffn_down"))
    x2 = _gate_res_fwd(x1, f, mod4, 5, n("res2"))
    saved = dict(x=x, h1_t=h1_t, p=p, o2=o2, states=states, u2=u2, mix=mix, g1=g1, x1=x1, h2_t=h2_t, u=u, a_t=a_t, f=f)
    return x2, saved


def _layer_bwd(l, dx2, s, mod4, w, cst, send):
    n = lambda t: f"l{l}_{t}"
    d = D_MODEL
    dfg, dg2 = _gate_res_bwd(dx2, s["f"], mod4, 5, n("res2_bwd"))
    da = _mm(dfg, w["ffn_down"], n("ffn_down_dx"), tb=True)
    d_ffn_down = _mm(s["a_t"], dfg, n("ffn_down_dw"), out_dtype=BF16, tm_max=DW_TM)
    tok = send(("ffn_down", l), d_ffn_down.reshape(N_DEV, D_FF // N_DEV, d))
    dcv, dcg = _ffn_act_bwd1(s["u"], da, w["ffn_dw_w"], _after(w["ffn_dw_b"], tok), n("ffn_act_bwd"))
    du, d_ffn_dw_w, d_ffn_dw_b = _dwconv_bwd(dcv, dcg, s["u"], w["ffn_dw_w"], n("ffn_dw_bwd"))
    d_ffn_dw_b = d_ffn_dw_b[0]
    dh2 = _mm(du, w["ffn_up"], n("ffn_up_dx"), tb=True, b3=True)
    d_ffn_up = _mm(s["h2_t"], du, n("ffn_up_dw"), out_dtype=BF16, tm_max=DW_TM,
                      o_cs=2 * D_FF // N_DEV)
    tok = send(("ffn_up", l), d_ffn_up)
    (dx1, dn2, dsh2, dsc2) = _normmod_bwd(s["x1"], _after(w["norm2_g"], tok), mod4, 1, dh2, dx2, n("norm2_bwd"))
    dgg, dg1 = _gate_res_bwd(dx1, s["g1"], mod4, 2, n("res1_bwd"))
    dmix = _mm(dgg, w["w_out"], n("proj_out_dx"), tb=True)
    d_w_out = _mm(_transpose_bf16(s["mix"], n("mix_t")), dgg, n("proj_out_dw"), out_dtype=BF16, tm_max=DW_TM)
    tok = send(("w_out", l), d_w_out.reshape(N_DEV, _d_mix() // N_DEV, d))
    do, dgate, dgn = _ggn_bwd(s["o2"], s["p"], _after(w["ret_gn_g"], tok), dmix, n("ret_gn_bwd"))
    dqr, dkr, dvr, ddec = _ret_bwd(s["p"], cst["cos"], cst["sin"], w["ret_decay"], cst["order"], s["states"], do, n("ret_bwd"))
    du2, dlng, dlnb, dpw = _conv_bwd1(s["u2"], dmix, w["conv_ln_g"], w["conv_ln_b"], w["conv_pw"], n("conv_bwd1"))
    dca, dcb, ddww, ddwb = _conv_bwd2(du2, s["p"], w["conv_dw_w"], n("conv_bwd2"))
    dnq, dnk, dnv, drb = _na_bwd(s["p"], w["rb"], dmix, n("na_bwd"))
    dp = _assemble_dp(dqr, dkr, dvr, dgate, dca, dcb, dnq, dnk, dnv, n("dproj"))
    h1_t = s["h1_t"]
    half = d // 2
    for i in range(2):
        d_w_in = _mm(h1_t[i * half:(i + 1) * half], dp, n(f"proj_in_dw{i}"), out_dtype=BF16, tm_max=DW_TM,
                        o_cs=_d_in() // N_DEV)
        tok = send(("w_in", l, i), d_w_in)
    dh1 = _mm(dp, w["w_in"], n("proj_in_dx"), tb=True, after=tok)
    (dx, dn1, dsh1, dsc1) = _normmod_bwd(s["x"], _after(w["norm1_g"], tok), mod4, 0, dh1, dx1, n("norm1_bwd"))
    dmod = jnp.concatenate([dsh1, dsc1, dg1, dsh2, dsc2, dg2], axis=1)
    small = dict(norm1_g=dn1[0], ret_decay=ddec[:, :, 0, 0], ret_gn_g=dgn[0], conv_dw_w=ddww, conv_dw_b=ddwb[0],
                 conv_ln_g=dlng[0], conv_ln_b=dlnb[0], conv_pw=dpw, na_rpb=_rpb_rows_t(drb), norm2_g=dn2[0],
                 ffn_dw_w=d_ffn_dw_w, ffn_dw_b=d_ffn_dw_b)
    return dx, dmod, small


def _d_mix():
    return _ret_w() + CONV_W + _na_w()


_SMALL = ["c_ctx", "b_ada", "norm1_g", "ret_decay", "ret_gn_g", "conv_dw_w", "conv_dw_b", "conv_ln_g", "conv_ln_b",
          "conv_pw", "na_rpb", "norm2_g", "ffn_dw_w", "ffn_dw_b", "final_g"]
_SMALL_SHARD_AXIS = {"conv_dw_w": 2, "conv_pw": 1, "ffn_dw_w": 2}


def kernel(x, c, ctx, c_ctx, w_ada, b_ada, norm1_g, w_in, ret_decay, ret_gn_g, conv_dw_w, conv_dw_b, conv_ln_g, conv_ln_b, conv_pw, na_rpb, w_out, norm2_g, ffn_up, ffn_dw_w, ffn_dw_b, ffn_down, final_g, loss_target, m_c_ctx, m_w_ada, m_b_ada, m_norm1_g, m_w_in, m_ret_decay, m_ret_gn_g, m_conv_dw_w, m_conv_dw_b, m_conv_ln_g, m_conv_ln_b, m_conv_pw, m_na_rpb, m_w_out, m_norm2_g, m_ffn_up, m_ffn_dw_w, m_ffn_dw_b, m_ffn_down, m_final_g, v_c_ctx, v_w_ada, v_b_ada, v_norm1_g, v_w_in, v_ret_decay, v_ret_gn_g, v_conv_dw_w, v_conv_dw_b, v_conv_ln_g, v_conv_ln_b, v_conv_pw, v_na_rpb, v_w_out, v_norm2_g, v_ffn_up, v_ffn_dw_w, v_ffn_dw_b, v_ffn_down, v_final_g):
    d, nl = D_MODEL, DEPTH
    cs = 6 * d // N_DEV
    me = _my_index()
    weights = dict(c_ctx=c_ctx, w_ada=w_ada, b_ada=b_ada, norm1_g=norm1_g, w_in=w_in, ret_decay=ret_decay, ret_gn_g=ret_gn_g,
                   conv_dw_w=conv_dw_w, conv_dw_b=conv_dw_b, conv_ln_g=conv_ln_g, conv_ln_b=conv_ln_b, conv_pw=conv_pw,
                   na_rpb=na_rpb, w_out=w_out, norm2_g=norm2_g, ffn_up=ffn_up, ffn_dw_w=ffn_dw_w, ffn_dw_b=ffn_dw_b,
                   ffn_down=ffn_down, final_g=final_g)
    mom = dict(c_ctx=m_c_ctx, w_ada=m_w_ada, b_ada=m_b_ada, norm1_g=m_norm1_g, w_in=m_w_in, ret_decay=m_ret_decay,
               ret_gn_g=m_ret_gn_g, conv_dw_w=m_conv_dw_w, conv_dw_b=m_conv_dw_b, conv_ln_g=m_conv_ln_g,
               conv_ln_b=m_conv_ln_b, conv_pw=m_conv_pw, na_rpb=m_na_rpb, w_out=m_w_out, norm2_g=m_norm2_g,
               ffn_up=m_ffn_up, ffn_dw_w=m_ffn_dw_w, ffn_dw_b=m_ffn_dw_b, ffn_down=m_ffn_down, final_g=m_final_g)
    var = dict(c_ctx=v_c_ctx, w_ada=v_w_ada, b_ada=v_b_ada, norm1_g=v_norm1_g, w_in=v_w_in, ret_decay=v_ret_decay,
               ret_gn_g=v_ret_gn_g, conv_dw_w=v_conv_dw_w, conv_dw_b=v_conv_dw_b, conv_ln_g=v_conv_ln_g,
               conv_ln_b=v_conv_ln_b, conv_pw=v_conv_pw, na_rpb=v_na_rpb, w_out=v_w_out, norm2_g=v_norm2_g,
               ffn_up=v_ffn_up, ffn_dw_w=v_ffn_dw_w, ffn_dw_b=v_ffn_dw_b, ffn_down=v_ffn_down, final_g=v_final_g)

    big_names = ["w_in", "w_out", "ffn_up", "ffn_down"]
    shards = {(nm, l): _cast_bf16(weights[nm][l], f"cast_{nm}{l}") for l in range(nl) for nm in big_names}
    small_sharded = _pack([conv_dw_w, conv_pw, ffn_dw_w], SUBLANES)
    c_rows = jnp.pad(c, ((0, SUBLANES - 1), (0, 0)))
    gathered = _run_comm(_Gather([c_rows, small_sharded, shards[("w_in", 0)]]), "gather_first")
    c_all = gathered[0][:, 0, :]
    def whole(rows, shard_shape, axis):
        n_el = int(np.prod(shard_shape))
        parts = rows.reshape(N_DEV, -1)[:, :n_el].reshape((N_DEV,) + tuple(shard_shape))
        parts = jnp.moveaxis(parts, 0, axis)
        return parts.reshape(shard_shape[:axis] + (N_DEV * shard_shape[axis],) + shard_shape[axis + 1:])

    r0 = _pack_rows(conv_dw_w.shape)
    r1 = r0 + _pack_rows(conv_pw.shape)
    r2 = r1 + _pack_rows(ffn_dw_w.shape)
    full_conv_dw_w = whole(gathered[1][:, :r0], conv_dw_w.shape, 2)
    full_conv_pw = whole(gathered[1][:, r0:r1], conv_pw.shape, 1)
    full_ffn_dw_w = whole(gathered[1][:, r1:r2], ffn_dw_w.shape, 2)

    c16 = jnp.concatenate([c_all, jnp.broadcast_to(c_ctx[None, :], (N_DEV, d))], axis=0)
    b_shard = lax.dynamic_slice_in_dim(b_ada, me * cs, cs, axis=1)[:, None, :]
    m_shard = _ada_fwd(c16, w_ada, b_shard, "ada_fwd")
    m_all = _run_comm(_Gather([m_shard.reshape(nl * 16, cs)]), "gather_mod")[0]
    m_full = m_all.reshape(N_DEV, nl, 16, cs).transpose(1, 2, 0, 3).reshape(nl, 16, 6 * d)
    m_lat = lax.dynamic_index_in_dim(m_full, me, axis=1, keepdims=False)
    mod = jnp.stack([m_full[:, N_DEV], m_lat], axis=1).reshape(nl, 2, 6, 1, d)

    arriving, token = {}, m_all
    for l in range(nl):
        for nm in big_names:
            if (nm, l) != ("w_in", 0):
                arriving[(nm, l)], token = _split_start(shards[(nm, l)], True, f"gather_{nm}{l}", token)
    mod = _after(mod, token)

    cos, sin = _rope_tables()
    cst = dict(cos=cos, sin=sin, order=_chunk_order())
    layer_w = []
    for l in range(nl):
        layer_w.append(dict(
            norm1_g=norm1_g[l][None], norm2_g=norm2_g[l][None], ret_decay=ret_decay[l], ret_gn_g=ret_gn_g[l][None],
            conv_dw_w=full_conv_dw_w[l], conv_dw_b=conv_dw_b[l][None], conv_ln_g=conv_ln_g[l][None],
            conv_ln_b=conv_ln_b[l][None], conv_pw=full_conv_pw[l], rb=_rpb_rows(na_rpb[l]),
            ffn_dw_w=full_ffn_dw_w[l], ffn_dw_b=ffn_dw_b[l][None]))

    xs = jnp.concatenate([ctx[0], x[0]], axis=0)
    saved = []
    for l in range(nl):
        def arrived(nm, after, l=l):
            if (nm, l) == ("w_in", 0):
                return gathered[2]
            return _split_wait(arriving[(nm, l)], after, f"arrived_{nm}{l}")

        xs, sv = _layer_fwd(l, xs, mod[l], layer_w[l], cst, arrived)
        saved.append(sv)
    loss_tile, dxs, dfinal = _loss_head(xs, final_g[None], loss_target[0], "loss_head")
    loss = lax.psum(loss_tile[0, 0], ("x", "y", "c"))

    dmods, smalls = [None] * nl, [None] * nl
    leaving, last = {}, [loss_tile]

    def send(key, partial):
        leaving[key], token = _split_start(partial, False, "send_" + "_".join(str(k) for k in key), last[0])
        last[0] = token
        return token

    per_layer = [nm for nm in _SMALL if nm not in ("c_ctx", "b_ada", "final_g")]
    small_packs, small_arriving = [None] * nl, [None] * nl
    for l in reversed(range(nl)):
        dxs, dmods[l], smalls[l] = _layer_bwd(l, dxs, saved[l], mod[l], layer_w[l], cst, send)
        small_packs[l] = _pack([smalls[l][nm] for nm in per_layer], 512)
        if l > 0:
            small_arriving[l], last[0] = _split_start(small_packs[l], True, f"gather_small_grads{l}", last[0])
    grad_x = dxs[CTX_LEN:][None]

    arrived_grad = lambda key, after: _split_wait(leaving[key], after, "got_" + "_".join(str(k) for k in key))
    out_big = {}
    after = dxs
    for nm in ["ffn_down", "ffn_up", "w_out"]:
        out_big[nm] = _adamw(weights[nm], mom[nm], var[nm], [arrived_grad((nm, l), after) for l in range(nl)],
                                f"adamw_{nm}")
        after = out_big[nm][0]

    dm_mine = jnp.stack(dmods).reshape(nl * 2, 6 * d)
    dm_rows = jnp.pad(dm_mine, ((0, SUBLANES - nl * 2), (0, 0)))
    dm_all = _run_comm(_Gather([dm_rows]), "gather_dmod", after=after)[0][:, :nl * 2].reshape(N_DEV, nl, 2, 6 * d)
    dm16_full = jnp.concatenate([dm_all[:, :, 1].transpose(1, 0, 2), dm_all[:, :, 0].transpose(1, 0, 2)], axis=1)
    dm16 = lax.dynamic_slice_in_dim(dm16_full, me * cs, cs, axis=2)
    g_w_ada, dc16 = _ada_bwd(c16, dm16, w_ada, "ada_bwd")

    shared = dict(c_ctx=jnp.sum(dc16[:, N_DEV:], axis=(0, 1)),
                  b_ada=jnp.sum(jnp.stack(dmods).reshape(nl, 2, 6 * d), axis=1), final_g=dfinal[0])
    shared_all = _run_comm(_Gather([_pack(list(shared.values()), SUBLANES)]), "gather_shared_grads")[0]
    small_arriving[0], token = _split_start(small_packs[0], True, "gather_small_grads0", shared_all)

    out_big["w_ada"] = _adamw(w_ada, m_w_ada, v_w_ada, g_w_ada[:, None], "adamw_w_ada")
    halves = lambda a: a.reshape(2 * nl, d // 2, a.shape[2])
    res = _adamw(halves(w_in), halves(m_w_in), halves(v_w_in),
                    [arrived_grad(("w_in", l, i), token) for l in range(nl) for i in range(2)], "adamw_w_in")
    out_big["w_in"] = [r.reshape(w_in.shape) for r in res]

    g_small = dict(zip(shared, _unpack(_sum_devices(shared_all, "sum_shared_grads"), [v.shape for v in shared.values()])))
    per = []
    for l in range(nl):
        got = _split_wait(small_arriving[l], res[0], f"arrived_small_grads{l}")
        per.append(_unpack(_sum_devices(got, f"sum_small_grads{l}"), [smalls[l][nm].shape for nm in per_layer]))
    g_small.update({nm: jnp.stack([per[l][i] for l in range(nl)]) for i, nm in enumerate(per_layer)})
    for nm, ax in _SMALL_SHARD_AXIS.items():
        n_sh = weights[nm].shape[ax]
        g_small[nm] = lax.dynamic_slice_in_dim(g_small[nm], me * n_sh, n_sh, axis=ax)
    shapes_own = [weights[nm].shape for nm in _SMALL]
    pk = lambda src: _pack([src[nm] for nm in _SMALL], 2 * SUBLANES)[None]
    res_small = _adamw(pk(weights), pk(mom), pk(var), pk(g_small)[:, None], "adamw_small")
    out_small = [dict(zip(_SMALL, _unpack(r[0], shapes_own))) for r in res_small]

    names = ["c_ctx", "w_ada", "b_ada", "norm1_g", "w_in", "ret_decay", "ret_gn_g", "conv_dw_w", "conv_dw_b", "conv_ln_g",
             "conv_ln_b", "conv_pw", "na_rpb", "w_out", "norm2_g", "ffn_up", "ffn_dw_w", "ffn_dw_b", "ffn_down", "final_g"]
    outs = [loss, grad_x]
    for kind in range(4):
        for nm in names:
            outs.append(out_big[nm][kind] if nm in out_big else out_small[kind][nm])
    return tuple(outs)
```

```python
import numpy as np
import jax
import jax.numpy as jnp
from jax import lax
from jax.experimental import pallas as pl
from jax.experimental.pallas import tpu as pltpu

D_MODEL = 2048
SEQ = 4096
DEPTH = 2
GRID_W = 64
CTX_LEN = 256
RET_HEADS = 4
RET_DK = 128
RET_DV = 256
RET_CHUNK = 128
CONV_W = 512
CONV_K = 31
NA_HEADS = 4
NA_DH = 128
NA_ROWS = 8
NA_COLS = 16
D_FF = 5632
FFN_K = 3
ROPE_BASE = 10000.0
EPS = 1e-6
ADAM_LR = 0.001
ADAM_B1 = 0.9
ADAM_B2 = 0.999
ADAM_EPS = 1e-08
ADAM_WD = 0.01
ADAM_STEP = 10
N_DEV = 8

LANES = 128
SUBLANES = 8
VMEM_LIMIT = 56 * 1024 * 1024
ROW_CHUNK = 32

F32 = jnp.float32
BF16 = jnp.bfloat16
MESH = pl.DeviceIdType.MESH
NEG = -1e30


def _ret_qk_w():
    return RET_HEADS * RET_DK


def _ret_w():
    return RET_HEADS * RET_DV


def _na_w():
    return NA_HEADS * NA_DH


def _d_in():
    return 2 * _ret_qk_w() + 2 * _ret_w() + 2 * CONV_W + 3 * _na_w()


def _offsets():
    sizes = [_ret_qk_w(), _ret_qk_w(), _ret_w(), _ret_w(), CONV_W, CONV_W, _na_w(), _na_w(), _na_w()]
    offs = [0]
    for s in sizes[:-1]:
        offs.append(offs[-1] + s)
    return dict(zip(["q", "k", "v", "g", "a", "b", "nq", "nk", "nv"], offs))


def _t_rows():
    return CTX_LEN + SEQ


def _tm():
    return CTX_LEN


def _params(sem=None):
    kw = dict(vmem_limit_bytes=VMEM_LIMIT)
    if sem is not None:
        kw["dimension_semantics"] = sem
    return pltpu.CompilerParams(**kw)


def _tile(n, pref, align):
    best = None
    for t in range(align, min(n, pref) + 1, align):
        if n % t == 0:
            best = t
    return best if best is not None else n


def _dg(a, b, ca, cb):
    return lax.dot_general(a.astype(BF16), b.astype(BF16), (((ca,), (cb,)), ((), ())), preferred_element_type=F32)


@jax.custom_vjp
def dot_nn(a, b):
    return _dg(a, b, 1, 0)


dot_nn.defvjp(lambda a, b: (_dg(a, b, 1, 0), (a, b)),
              lambda r, g: (_dg(g, r[1], 1, 1), _dg(r[0], g, 0, 0)))


@jax.custom_vjp
def dot_nt(a, b):
    return _dg(a, b, 1, 1)


dot_nt.defvjp(lambda a, b: (_dg(a, b, 1, 1), (a, b)),
              lambda r, g: (_dg(g, r[1], 1, 0), _dg(g, r[0], 0, 0)))


@jax.custom_vjp
def dot_tn(a, b):
    return _dg(a, b, 0, 0)


dot_tn.defvjp(lambda a, b: (_dg(a, b, 0, 0), (a, b)),
              lambda r, g: (_dg(r[1], g, 1, 1), _dg(r[0], g, 1, 0)))


def _sigmoid(x):
    return 0.5 * jnp.tanh(0.5 * x) + 0.5


def _silu(x):
    return x * _sigmoid(x)


def _my_pos():
    return lax.axis_index("x"), lax.axis_index("y"), lax.axis_index("c")


def _my_index():
    x, y, c = _my_pos()
    return 4 * x + 2 * y + c


_ANY = pl.BlockSpec(memory_space=pl.ANY)


class _Gather:
    def __init__(self, arrays):
        self.arrays = list(arrays)
        n = len(self.arrays)
        self.out_shape = [jax.ShapeDtypeStruct((N_DEV,) + a.shape, a.dtype) for a in self.arrays]
        self.scratch = [pltpu.SemaphoreType.DMA((n, 7)), pltpu.SemaphoreType.DMA((n, 7)), pltpu.SemaphoreType.DMA((n,))]

    def _plan(self, xs, outs, sems):
        send_sems, recv_sems, local_sems = sems
        n = len(self.arrays)
        x, y, c = _my_pos()
        me, sibling = (x, y, c), (x, y, 1 - c)
        chips = [(1 - x, y), (x, 1 - y), (1 - x, 1 - y)]

        def slot(a, p):
            return outs[a].at[4 * p[0] + 2 * p[1] + p[2]]

        def copy(a, k, block, to, src=None):
            return pltpu.make_async_remote_copy(
                src_ref=slot(a, block) if src is None else src, dst_ref=slot(a, block),
                send_sem=send_sems.at[a, k], recv_sem=recv_sems.at[a, k], device_id=to, device_id_type=MESH)

        mine = [pltpu.make_async_copy(xs[a], slot(a, me), local_sems.at[a]) for a in range(n)]
        first = []
        for a in range(n):
            first.append(copy(a, 0, me, sibling, src=xs[a]))
            first += [copy(a, 1 + j, me, (*chip, c), src=xs[a]) for j, chip in enumerate(chips)]
        return n, c, me, sibling, chips, copy, mine, first

    def start(self, xs, outs, sems):
        _, _, _, _, _, _, mine, first = self._plan(xs, outs, sems)
        for m in mine:
            m.start()
        for cp in first:
            cp.start()

    def finish(self, xs, outs, sems):
        n, c, me, sibling, chips, copy, mine, first = self._plan(xs, outs, sems)
        passed = []
        for a in range(n):
            for j, chip in enumerate(chips):
                copy(a, 1 + j, (*chip, c), me).wait_recv()
                p = copy(a, 4 + j, (*chip, c), sibling)
                p.start()
                passed.append(p)
        for a in range(n):
            copy(a, 0, sibling, me).wait_recv()
            for j, chip in enumerate(chips):
                copy(a, 4 + j, (*chip, 1 - c), me).wait_recv()
        for cp in first + passed:
            cp.wait_send()
        for m in mine:
            m.wait()


def _run_comm(comm, name, after=None):
    n = len(comm.arrays)
    extra = [] if after is None else [after]

    def body(*refs):
        xs, outs, sems = refs[:n], refs[n + len(extra):2 * n + len(extra)], refs[2 * n + len(extra):]
        comm.start(xs, outs, sems)
        comm.finish(xs, outs, sems)

    return pl.pallas_call(body, name=name, out_shape=comm.out_shape, in_specs=[_ANY] * (n + len(extra)),
                          out_specs=[_ANY] * n, scratch_shapes=comm.scratch)(*comm.arrays, *extra)


_HBM = pl.BlockSpec(memory_space=pltpu.HBM)
_SEMS = pl.BlockSpec(memory_space=pltpu.SEMAPHORE)
_EFFECT = pltpu.SideEffectType.DATAFLOW_SIDE_EFFECTING


def _own_slot(x, gathering, name):
    shape = (N_DEV,) + x.shape if gathering else x.shape
    r, c = shape[1], shape[2]
    tr = _tile(r, 256, 2 * SUBLANES)
    me = jnp.reshape(_my_index(), (1,)).astype(jnp.int32)

    def body(me_ref, x_ref, o_ref):
        o_ref[...] = x_ref[...]

    src = (pl.BlockSpec((tr, c), lambda i, m: (i, 0)) if gathering
           else pl.BlockSpec((None, tr, c), lambda i, m: (m[0], i, 0)))
    grid_spec = pltpu.PrefetchScalarGridSpec(
        num_scalar_prefetch=1, grid=(r // tr,), in_specs=[src],
        out_specs=pl.BlockSpec((None, tr, c), lambda i, m: (m[0], i, 0)))
    return pl.pallas_call(body, name=name, grid_spec=grid_spec, out_shape=jax.ShapeDtypeStruct(shape, x.dtype),
                          compiler_params=_params(("arbitrary",)))(me, x)


def _split_plan(x_ref, land_ref, send_sems, recv_sems, gathering):
    x, y, c = _my_pos()
    me = 4 * x + 2 * y + c
    sends, recvs = [], []
    for k in range(1, N_DEV):
        px = 1 - x if (k >> 2) & 1 else x
        py = 1 - y if (k >> 1) & 1 else y
        pc = 1 - c if k & 1 else c
        peer = 4 * px + 2 * py + pc
        mine, theirs = (x_ref, x_ref) if gathering else (x_ref.at[peer], x_ref.at[me])
        sends.append(pltpu.make_async_remote_copy(
            src_ref=mine, dst_ref=land_ref.at[me], send_sem=send_sems.at[k - 1], recv_sem=recv_sems.at[k - 1],
            device_id=(px, py, pc), device_id_type=MESH))
        recvs.append(pltpu.make_async_remote_copy(
            src_ref=theirs, dst_ref=land_ref.at[peer], send_sem=send_sems.at[k - 1], recv_sem=recv_sems.at[k - 1],
            device_id=(px, py, pc), device_id_type=MESH))
    return sends, recvs


def _split_start(x, gathering, name, prev):
    land = _own_slot(x, gathering, name + "_own")

    def body(x_ref, land_ref, prev_ref, send_sems, recv_sems, x_thru, land_thru, token):
        sends, _ = _split_plan(x_ref, land_ref, send_sems, recv_sems, gathering)
        for s in sends:
            s.start()
        token[...] = jnp.zeros_like(token)

    sems = pltpu.SemaphoreType.DMA((N_DEV - 1,))
    send_sems, recv_sems, x_thru, land_thru, token = pl.pallas_call(
        body, name=name,
        out_shape=(sems, sems, pltpu.HBM(x.shape, x.dtype), pltpu.HBM(land.shape, land.dtype),
                   jax.ShapeDtypeStruct((SUBLANES, LANES), F32)),
        in_specs=(_HBM, _HBM, _ANY), out_specs=(_SEMS, _SEMS, _HBM, _HBM, pl.BlockSpec(memory_space=pltpu.VMEM)),
        input_output_aliases={0: 2, 1: 3},
        compiler_params=pltpu.CompilerParams(has_side_effects=_EFFECT),
    )(pltpu.with_memory_space_constraint(x, pltpu.HBM), pltpu.with_memory_space_constraint(land, pltpu.HBM), prev)
    return (send_sems, recv_sems, x_thru, land_thru, gathering), token


def _after(a, token):
    return a + token[0, 0].astype(a.dtype)


def _split_wait(handle, after, name):
    send_sems, recv_sems, x_thru, land_thru, gathering = handle

    def body(x_ref, land_ref, send_sems, recv_sems, after_ref, x_dead, got_ref):
        sends, recvs = _split_plan(x_ref, land_ref, send_sems, recv_sems, gathering)
        for s in sends:
            s.wait_send()
        for r in recvs:
            r.wait_recv()

    return pl.pallas_call(
        body, name=name, out_shape=(pltpu.HBM(x_thru.shape, x_thru.dtype), pltpu.HBM(land_thru.shape, land_thru.dtype)),
        in_specs=(_HBM, _HBM, _SEMS, _SEMS, _ANY), out_specs=(_HBM, _HBM), input_output_aliases={0: 0, 1: 1},
        compiler_params=pltpu.CompilerParams(has_side_effects=_EFFECT),
    )(x_thru, land_thru, send_sems, recv_sems, after)[1]


def _call(body, *, name, grid, in_specs, out_specs, out_shape, args, scratch=(), sem=None, after=None):
    if after is None:
        return list(pl.pallas_call(body, name=name, grid=grid, in_specs=list(in_specs), out_specs=list(out_specs),
                                   out_shape=list(out_shape), scratch_shapes=list(scratch),
                                   compiler_params=_params(sem))(*args))
    n_in = len(in_specs)

    def wrapped(*refs):
        body(*refs[:n_in], *refs[n_in + 1:])

    return list(pl.pallas_call(wrapped, name=name, grid=grid, in_specs=list(in_specs) + [_ANY], out_specs=list(out_specs),
                               out_shape=list(out_shape), scratch_shapes=list(scratch),
                               compiler_params=_params(sem))(*args, after))


MM_B_BLOCK_BYTES = 6 * 1024 * 1024
MM_O_BLOCK_BYTES = 13 * 1024 * 1024 // 2


def _mm(a, b, name, tb=False, out_dtype=F32, b3=False, o_cs=None, tm_max=1088, after=None):
    m, k = a.shape
    if b3:
        cs = b.shape[2]
        n, kb = (b.shape[1], N_DEV * cs) if tb else (N_DEV * cs, b.shape[1])
    else:
        n, kb = (b.shape[0], b.shape[1]) if tb else (b.shape[1], b.shape[0])
    assert k == kb, (a.shape, b.shape, tb)
    if b3 and tb:
        tm, tn = _tile(m, 544, 2 * SUBLANES), _tile(n, 256, LANES)

        def body_shards(a_ref, b_ref, o_ref):
            r = None
            for j in range(N_DEV):
                part = lax.dot_general(a_ref[:, j * cs:(j + 1) * cs], b_ref[j], (((1,), (1,)), ((), ())),
                                       preferred_element_type=F32)
                r = part if r is None else r + part
            o_ref[...] = r.astype(o_ref.dtype)

        return _call(
            body_shards, name=name, grid=(m // tm, n // tn),
            in_specs=[pl.BlockSpec((tm, k), lambda i, j: (i, 0)), pl.BlockSpec((N_DEV, tn, cs), lambda i, j: (0, j, 0))],
            out_specs=[pl.BlockSpec((tm, tn), lambda i, j: (i, j))], out_shape=[jax.ShapeDtypeStruct((m, n), out_dtype)],
            sem=("parallel", "parallel"), args=(a, b), after=after)[0]
    tm = _tile(m, tm_max, 2 * SUBLANES)
    tk = k
    if b3:
        tn = cs
    elif o_cs is not None:
        tn = o_cs if o_cs % LANES == 0 else 2 * o_cs
    else:
        tn = _tile(n, min(MM_B_BLOCK_BYTES // (2 * tk), MM_O_BLOCK_BYTES // (4 * tm)), LANES)
    cb = 1 if tb else 0
    dn = (((1,), (cb,)), ((), ()))

    def body_one(a_ref, b_ref, o_ref):
        r = lax.dot_general(a_ref[...], b_ref[...], dn, preferred_element_type=F32)
        if o_cs is None:
            o_ref[...] = r.astype(o_ref.dtype)
        else:
            for j in range(tn // o_cs):
                o_ref[j] = r[:, j * o_cs:(j + 1) * o_cs].astype(o_ref.dtype)

    a_spec = pl.BlockSpec((tm, tk), lambda i, j: (i, 0))
    if b3:
        b_spec = pl.BlockSpec((None, tk, cs), lambda i, j: (j, 0, 0))
    else:
        b_spec = pl.BlockSpec((tn, tk), lambda i, j: (j, 0)) if tb else pl.BlockSpec((tk, tn), lambda i, j: (0, j))
    if o_cs is None:
        o_spec = pl.BlockSpec((tm, tn), lambda i, j: (i, j))
        o_shape = jax.ShapeDtypeStruct((m, n), out_dtype)
    else:
        o_spec = pl.BlockSpec((tn // o_cs, tm, o_cs), lambda i, j: (j, i, 0))
        o_shape = jax.ShapeDtypeStruct((n // o_cs, m, o_cs), out_dtype)
    return _call(
        body_one, name=name, grid=(m // tm, n // tn), in_specs=[a_spec, b_spec], out_specs=[o_spec], out_shape=[o_shape],
        sem=("parallel", "parallel"), args=(a, b), after=after)[0]


DW_TM = 512


def _transpose_bf16(x, name):
    t, c = x.shape
    tt = _tm()

    def body(x_ref, o_ref):
        o_ref[...] = x_ref[...].T

    return pl.pallas_call(body, name=name, grid=(t // tt,), in_specs=[pl.BlockSpec((tt, c), lambda i: (i, 0))],
                          out_specs=pl.BlockSpec((c, tt), lambda i: (0, i)),
                          out_shape=jax.ShapeDtypeStruct((c, t), BF16), compiler_params=_params(("parallel",)))(x)


def _cast_bf16(x, name):
    r, c = x.shape
    tr = _tile(r, 512, 2 * SUBLANES)

    def body(x_ref, o_ref):
        o_ref[...] = x_ref[...].astype(BF16)

    return pl.pallas_call(body, name=name, grid=(r // tr,), in_specs=[pl.BlockSpec((tr, c), lambda i: (i, 0))],
                          out_specs=pl.BlockSpec((tr, c), lambda i: (i, 0)),
                          out_shape=jax.ShapeDtypeStruct((r, c), BF16), compiler_params=_params(("parallel",)))(x)


def _cols_from_shards(wg, name):
    _, k, cs = wg.shape
    tk = _tile(k, 256, 2 * SUBLANES)

    def body(w_ref, o_ref):
        for j in range(N_DEV):
            o_ref[:, j * cs:(j + 1) * cs] = w_ref[j]

    return pl.pallas_call(body, name=name, grid=(k // tk,),
                          in_specs=[pl.BlockSpec((N_DEV, tk, cs), lambda i: (0, i, 0))],
                          out_specs=pl.BlockSpec((tk, N_DEV * cs), lambda i: (i, 0)),
                          out_shape=jax.ShapeDtypeStruct((k, N_DEV * cs), wg.dtype),
                          compiler_params=_params(("parallel",)))(wg)


def _stream(i):
    return jnp.minimum(i, 1)


def _normmod(x, g, sh, sc):
    y = x * lax.rsqrt(jnp.mean(x * x, axis=-1, keepdims=True) + EPS)
    return (y * g) * (1.0 + sc) + sh


def _mod_spec(chunk, d):
    return pl.BlockSpec((None, None, 1, d), lambda i: (_stream(i), chunk, 0, 0))


def _normmod_fwd(x, g, mod4, which, name):
    t, d = x.shape
    tm = _tm()
    ish, isc = (0, 1) if which == 0 else (3, 4)

    def body(x_ref, g_ref, sh_ref, sc_ref, o_ref, ot_ref):
        h = _normmod(x_ref[...], g_ref[...], sh_ref[...], sc_ref[...]).astype(BF16)
        o_ref[...] = h
        ot_ref[...] = h.T

    row = pl.BlockSpec((tm, d), lambda i: (i, 0))
    return pl.pallas_call(body, name=name, grid=(t // tm,),
                          in_specs=[row, pl.BlockSpec((1, d), lambda i: (0, 0)), _mod_spec(ish, d), _mod_spec(isc, d)],
                          out_specs=[row, pl.BlockSpec((d, tm), lambda i: (0, i))],
                          out_shape=[jax.ShapeDtypeStruct((t, d), BF16), jax.ShapeDtypeStruct((d, t), BF16)],
                          compiler_params=_params(("parallel",)))(x, g, mod4, mod4)


def _normmod_bwd(x, g, mod4, which, dh, dres, name):
    t, d = x.shape
    tm = _tm()
    ish, isc = (0, 1) if which == 0 else (3, 4)

    def body(x_ref, g_ref, sh_ref, sc_ref, dh_ref, dres_ref, dx_ref, dg_ref, dsh_ref, dsc_ref):
        i = pl.program_id(0)
        _, vjp = jax.vjp(_normmod, x_ref[...], g_ref[...], sh_ref[...], sc_ref[...])
        dx, dg, dsh, dsc = vjp(dh_ref[...])
        dx_ref[...] = dres_ref[...] + dx

        @pl.when(i == 0)
        def _():
            dg_ref[...] = jnp.zeros_like(dg_ref)

        @pl.when(i <= 1)
        def _():
            dsh_ref[...] = jnp.zeros_like(dsh_ref)
            dsc_ref[...] = jnp.zeros_like(dsc_ref)

        dg_ref[...] += dg
        dsh_ref[...] += dsh
        dsc_ref[...] += dsc

    row = pl.BlockSpec((tm, d), lambda i: (i, 0))
    vec = pl.BlockSpec((1, d), lambda i: (0, 0))
    svec = pl.BlockSpec((None, 1, d), lambda i: (_stream(i), 0, 0))
    return _call(
        body, name=name, grid=(t // tm,),
        in_specs=[row, vec, _mod_spec(ish, d), _mod_spec(isc, d), row, row],
        out_specs=[row, vec, svec, svec],
        out_shape=[jax.ShapeDtypeStruct((t, d), F32), jax.ShapeDtypeStruct((1, d), F32),
                   jax.ShapeDtypeStruct((2, 1, d), F32), jax.ShapeDtypeStruct((2, 1, d), F32)],
        sem=("arbitrary",), args=(x, g, mod4, mod4, dh, dres))


def _gate_res_fwd(x, f, mod4, chunk, name):
    t, d = x.shape
    tm = _tm()

    def body(x_ref, f_ref, g_ref, o_ref):
        o_ref[...] = x_ref[...] + g_ref[...] * f_ref[...]

    row = pl.BlockSpec((tm, d), lambda i: (i, 0))
    return pl.pallas_call(body, name=name, grid=(t // tm,), in_specs=[row, row, _mod_spec(chunk, d)], out_specs=row,
                          out_shape=jax.ShapeDtypeStruct((t, d), F32), compiler_params=_params(("parallel",)))(x, f, mod4)


def _gate_res_bwd(dx, f, mod4, chunk, name):
    t, d = dx.shape
    tm = _tm()

    def body(dx_ref, f_ref, g_ref, o_ref, dg_ref):
        i = pl.program_id(0)
        dxv = dx_ref[...]
        o_ref[...] = (dxv * g_ref[...]).astype(BF16)

        @pl.when(i <= 1)
        def _():
            dg_ref[...] = jnp.zeros_like(dg_ref)

        dg_ref[...] += jnp.sum(dxv * f_ref[...], axis=0, keepdims=True)

    row = pl.BlockSpec((tm, d), lambda i: (i, 0))
    return pl.pallas_call(
        body, name=name, grid=(t // tm,), in_specs=[row, row, _mod_spec(chunk, d)],
        out_specs=[row, pl.BlockSpec((None, 1, d), lambda i: (_stream(i), 0, 0))],
        out_shape=[jax.ShapeDtypeStruct((t, d), BF16), jax.ShapeDtypeStruct((2, 1, d), F32)],
        compiler_params=_params(("arbitrary",)))(dx, f, mod4)


def _loss_head(x, final_g, target, name):
    t, d = x.shape
    tm = _tm()

    def loss_fn(xv, g, tgt):
        y = (xv * lax.rsqrt(jnp.mean(xv * xv, axis=-1, keepdims=True) + EPS)) * g
        err = y - tgt
        return 0.5 * jnp.sum(jnp.mean(err * err, axis=-1, keepdims=True))

    def body(x_ref, g_ref, t_ref, l_ref, dx_ref, dg_ref):
        i = pl.program_id(0)

        @pl.when(i == 0)
        def _():
            l_ref[...] = jnp.zeros_like(l_ref)
            dg_ref[...] = jnp.zeros_like(dg_ref)
            dx_ref[...] = jnp.zeros_like(dx_ref)

        @pl.when(i > 0)
        def _():
            l, (dx, dg) = jax.value_and_grad(loss_fn, argnums=(0, 1))(x_ref[...], g_ref[...], t_ref[...])
            l_ref[...] += jnp.full(l_ref.shape, l, F32)
            dx_ref[...] = dx
            dg_ref[...] += dg

    row = pl.BlockSpec((tm, d), lambda i: (i, 0))
    vec = pl.BlockSpec((1, d), lambda i: (0, 0))
    return pl.pallas_call(
        body, name=name, grid=(t // tm,),
        in_specs=[row, vec, pl.BlockSpec((tm, d), lambda i: (jnp.maximum(i - 1, 0), 0))],
        out_specs=[pl.BlockSpec((SUBLANES, LANES), lambda i: (0, 0)), row, vec],
        out_shape=[jax.ShapeDtypeStruct((SUBLANES, LANES), F32), jax.ShapeDtypeStruct((t, d), F32),
                   jax.ShapeDtypeStruct((1, d), F32)],
        compiler_params=_params(("arbitrary",)))(x, final_g, target)


def _swap_quarters(x):
    half, nf = RET_DK // 2, RET_DK // 4
    lane = lax.broadcasted_iota(jnp.int32, x.shape, 1)
    return jnp.where((lane % half) < nf, pltpu.roll(x, RET_DK - nf, 1), pltpu.roll(x, nf, 1))


def _rope(x, cos, sin):
    return x * cos + _swap_quarters(x) * sin


def _rope_t(y, cos, sin):
    return y * cos + _swap_quarters(y * sin)


def _ret_consts(d):
    c = RET_CHUNK
    ii = lax.broadcasted_iota(jnp.int32, (c, 1), 0).astype(F32)
    jj = lax.broadcasted_iota(jnp.int32, (1, c), 1).astype(F32)
    fwd = d == 0
    sgn = jnp.where(fwd, 1.0, -1.0).astype(F32)
    pos = jnp.where(fwd, ii, c - 1.0 - ii)
    return sgn * (ii - jj), pos


def _ret_step(lgt, state, q, k, v, diff, pos):
    c = float(RET_CHUNK)
    lg = -(jnp.maximum(-lgt, 0.0) + jnp.log1p(jnp.exp(-jnp.abs(lgt))))
    lower = diff >= 0
    decay = jnp.where(lower, jnp.exp(jnp.where(lower, diff, 0.0) * lg), 0.0)
    xi = jnp.exp((pos + 1.0) * lg)
    zeta = jnp.exp((c - 1.0 - pos) * lg)
    gch = jnp.exp(c * lg)
    inner = dot_nt(q, k) * decay
    out = dot_nn(inner, v) + dot_nn(q, state) * xi
    new_state = state * gch + dot_tn(k * zeta, v)
    return out, new_state


def _chunk_order():
    nc, nch = CTX_LEN // RET_CHUNK, _t_rows() // RET_CHUNK
    fwd = list(range(nch))
    bwd = list(range(nc - 1, -1, -1)) + list(range(nch - 1, nc - 1, -1))
    return jnp.asarray(np.array([fwd, bwd], np.int32))


def _ret_fwd(p, cos, sin, decay, order, name):
    t = p.shape[0]
    c, dk, dv, nh = RET_CHUNK, RET_DK, RET_DV, RET_HEADS
    nch = t // c
    off = _offsets()
    wqk, wv = nh * dk, nh * dv
    assert off["q"] % wqk == 0 and off["k"] % wqk == 0 and off["v"] % wv == 0
    qb, kb, vb = off["q"] // wqk, off["k"] // wqk, off["v"] // wv
    scale = RET_DK ** -0.5

    def body(ord_ref, dec_ref, q_ref, k_ref, v_ref, cos_ref, sin_ref, o_ref, st_ref, state):
        d, s = pl.program_id(0), pl.program_id(1)

        @pl.when(s == 0)
        def _():
            state[...] = jnp.zeros_like(state)

        diff, pos = _ret_consts(d)
        cosv, sinv = cos_ref[...], sin_ref[...]
        for h in range(nh):
            st = state[h]
            st_ref[h] = st
            lgt = jnp.full((1, 1), dec_ref[d, h], F32)
            q = _rope(q_ref[:, h * dk:(h + 1) * dk], cosv, sinv) * scale
            k = _rope(k_ref[:, h * dk:(h + 1) * dk], cosv, sinv)
            out, ns = _ret_step(lgt, st, q, k, v_ref[:, h * dv:(h + 1) * dv], diff, pos)
            o_ref[:, h * dv:(h + 1) * dv] = out
            state[h] = ns

    grid_spec = pltpu.PrefetchScalarGridSpec(
        num_scalar_prefetch=1, grid=(2, nch),
        in_specs=[pl.BlockSpec(memory_space=pltpu.SMEM),
                  pl.BlockSpec((c, wqk), lambda d, s, o: (o[d, s], qb)),
                  pl.BlockSpec((c, wqk), lambda d, s, o: (o[d, s], kb)),
                  pl.BlockSpec((c, wv), lambda d, s, o: (o[d, s], vb)),
                  pl.BlockSpec((c, dk), lambda d, s, o: (o[d, s], 0)),
                  pl.BlockSpec((c, dk), lambda d, s, o: (o[d, s], 0))],
        out_specs=[pl.BlockSpec((None, c, wv), lambda d, s, o: (d, o[d, s], 0)),
                   pl.BlockSpec((None, nh, None, dk, dv), lambda d, s, o: (d, 0, s, 0, 0))],
        scratch_shapes=[pltpu.VMEM((nh, dk, dv), F32)])
    return pl.pallas_call(
        body, name=name, grid_spec=grid_spec,
        out_shape=[jax.ShapeDtypeStruct((2, t, wv), F32), jax.ShapeDtypeStruct((2, nh, nch, dk, dv), F32)],
        compiler_params=_params(("arbitrary", "arbitrary")))(order, decay, p, p, p, cos, sin)


def _ret_bwd(p, cos, sin, decay, order, states, do, name):
    t = p.shape[0]
    c, dk, dv, nh = RET_CHUNK, RET_DK, RET_DV, RET_HEADS
    nch = t // c
    off = _offsets()
    wqk, wv = nh * dk, nh * dv
    qb, kb, vb = off["q"] // wqk, off["k"] // wqk, off["v"] // wv
    scale = RET_DK ** -0.5

    def body(ord_ref, dec_ref, q_ref, k_ref, v_ref, cos_ref, sin_ref, st_ref, do_ref,
             dq_ref, dk_ref, dv_ref, dd_ref, dstate):
        d, s = pl.program_id(0), pl.program_id(1)

        @pl.when(s == 0)
        def _():
            dstate[...] = jnp.zeros_like(dstate)
            dd_ref[...] = jnp.zeros_like(dd_ref)

        diff, pos = _ret_consts(d)
        cosv, sinv = cos_ref[...], sin_ref[...]
        for h in range(nh):
            qk, vv = slice(h * dk, (h + 1) * dk), slice(h * dv, (h + 1) * dv)
            lgt = jnp.full((1, 1), dec_ref[d, h], F32)
            q = _rope(q_ref[:, qk], cosv, sinv) * scale
            k = _rope(k_ref[:, qk], cosv, sinv)
            _, vjp = jax.vjp(lambda a, b, cq, ck, cv: _ret_step(a, b, cq, ck, cv, diff, pos),
                             lgt, st_ref[h], q, k, v_ref[:, vv])
            dlgt, dst, dq, dkk, dvv = vjp((do_ref[:, vv], dstate[h]))
            dstate[h] = dst
            dq_ref[:, qk] = _rope_t(dq * scale, cosv, sinv)
            dk_ref[:, qk] = _rope_t(dkk, cosv, sinv)
            dv_ref[:, vv] = dvv
            dd_ref[h] += jnp.broadcast_to(dlgt, (SUBLANES, LANES))

    rev = lambda o, d, s: o[d, nch - 1 - s]
    grid_spec = pltpu.PrefetchScalarGridSpec(
        num_scalar_prefetch=1, grid=(2, nch),
        in_specs=[pl.BlockSpec(memory_space=pltpu.SMEM),
                  pl.BlockSpec((c, wqk), lambda d, s, o: (rev(o, d, s), qb)),
                  pl.BlockSpec((c, wqk), lambda d, s, o: (rev(o, d, s), kb)),
                  pl.BlockSpec((c, wv), lambda d, s, o: (rev(o, d, s), vb)),
                  pl.BlockSpec((c, dk), lambda d, s, o: (rev(o, d, s), 0)),
                  pl.BlockSpec((c, dk), lambda d, s, o: (rev(o, d, s), 0)),
                  pl.BlockSpec((None, nh, None, dk, dv), lambda d, s, o: (d, 0, nch - 1 - s, 0, 0)),
                  pl.BlockSpec((c, wv), lambda d, s, o: (rev(o, d, s), 0))],
        out_specs=[pl.BlockSpec((None, c, wqk), lambda d, s, o: (d, rev(o, d, s), 0)),
                   pl.BlockSpec((None, c, wqk), lambda d, s, o: (d, rev(o, d, s), 0)),
                   pl.BlockSpec((None, c, wv), lambda d, s, o: (d, rev(o, d, s), 0)),
                   pl.BlockSpec((None, nh, SUBLANES, LANES), lambda d, s, o: (d, 0, 0, 0))],
        scratch_shapes=[pltpu.VMEM((nh, dk, dv), F32)])
    return pl.pallas_call(
        body, name=name, grid_spec=grid_spec,
        out_shape=[jax.ShapeDtypeStruct((2, t, wqk), F32), jax.ShapeDtypeStruct((2, t, wqk), F32),
                   jax.ShapeDtypeStruct((2, t, wv), F32), jax.ShapeDtypeStruct((2, nh, SUBLANES, LANES), F32)],
        compiler_params=_params(("arbitrary", "arbitrary")))(order, decay, p, p, p, cos, sin, states, do)


def _ggn_head(of, ob, gate, g):
    o = of + ob
    mu = jnp.mean(o, axis=-1, keepdims=True)
    var = jnp.mean(jnp.square(o - mu), axis=-1, keepdims=True)
    return ((o - mu) * lax.rsqrt(var + EPS) * g) * _silu(gate)


def _ggn_fwd(o2, p, gn_g, name):
    t = p.shape[0]
    tm, w, dv = _tm(), _ret_w(), RET_DV
    gb = _offsets()["g"] // w

    def body(o_ref, gate_ref, g_ref, out_ref):
        for h in range(RET_HEADS):
            sl = slice(h * dv, (h + 1) * dv)
            out_ref[:, sl] = _ggn_head(o_ref[0, :, sl], o_ref[1, :, sl], gate_ref[:, sl], g_ref[:, sl]).astype(BF16)

    return pl.pallas_call(
        body, name=name, grid=(t // tm,),
        in_specs=[pl.BlockSpec((2, tm, w), lambda i: (0, i, 0)), pl.BlockSpec((tm, w), lambda i: (i, gb)),
                  pl.BlockSpec((1, w), lambda i: (0, 0))],
        out_specs=pl.BlockSpec((tm, w), lambda i: (i, 0)), out_shape=jax.ShapeDtypeStruct((t, w), BF16),
        compiler_params=_params(("parallel",)))(o2, p, gn_g)


def _ggn_bwd(o2, p, gn_g, dmix, name):
    t = p.shape[0]
    tm, w, dv = _tm(), _ret_w(), RET_DV
    gb = _offsets()["g"] // w

    def body(o_ref, gate_ref, g_ref, dy_ref, do_ref, dgate_ref, dg_ref):
        i = pl.program_id(0)

        @pl.when(i == 0)
        def _():
            dg_ref[...] = jnp.zeros_like(dg_ref)

        for h in range(RET_HEADS):
            sl = slice(h * dv, (h + 1) * dv)
            _, vjp = jax.vjp(_ggn_head, o_ref[0, :, sl], o_ref[1, :, sl], gate_ref[:, sl], g_ref[:, sl])
            do, _, dgate, dg = vjp(dy_ref[:, sl])
            do_ref[:, sl] = do
            dgate_ref[:, sl] = dgate
            dg_ref[:, sl] += dg

    row = pl.BlockSpec((tm, w), lambda i: (i, 0))
    return pl.pallas_call(
        body, name=name, grid=(t // tm,),
        in_specs=[pl.BlockSpec((2, tm, w), lambda i: (0, i, 0)), pl.BlockSpec((tm, w), lambda i: (i, gb)),
                  pl.BlockSpec((1, w), lambda i: (0, 0)), row],
        out_specs=[row, row, pl.BlockSpec((1, w), lambda i: (0, 0))],
        out_shape=[jax.ShapeDtypeStruct((t, w), F32), jax.ShapeDtypeStruct((t, w), F32),
                   jax.ShapeDtypeStruct((1, w), F32)],
        compiler_params=_params(("arbitrary",)))(o2, p, gn_g, dmix)


def _halo(k):
    return SUBLANES * ((k // 2 + SUBLANES - 1) // SUBLANES)


def _halo_specs(width, colblock, h, tm):
    r = tm // h
    return [pl.BlockSpec((h, width), lambda i, *_: (jnp.maximum(i * r - 1, 0), colblock(*_))),
            pl.BlockSpec((tm, width), lambda i, *_: (i, colblock(*_))),
            pl.BlockSpec((h, width), lambda i, *_: (jnp.minimum((i + 1) * r, (_t_rows() // h) - 1), colblock(*_)))]


def _fill_ext(ext_ref, prev, cur, nxt, i, h, tm):
    nt = _t_rows() // tm
    ext_ref[0:h, :] = jnp.where(i >= 2, prev, 0.0)
    ext_ref[h:h + tm, :] = cur
    ext_ref[h + tm:h + tm + h, :] = jnp.where((i >= 1) & (i <= nt - 2), nxt, 0.0)


def _corr(ext_ref, w_ref, k, h, tm, flip):
    pad = k // 2
    acc = None
    for kk in range(k):
        o = h + (pad - kk if flip else kk - pad)
        term = w_ref[kk:kk + 1, :] * ext_ref[o:o + tm, :]
        acc = term if acc is None else acc + term
    return acc


def _conv_post(u2, ln_g, ln_b, pw):
    mu = jnp.mean(u2, axis=-1, keepdims=True)
    var = jnp.mean(jnp.square(u2 - mu), axis=-1, keepdims=True)
    y = (u2 - mu) * lax.rsqrt(var + EPS) * ln_g + ln_b
    return dot_nn(_silu(y), pw)


def _conv_fwd(p, dw_w, dw_b, ln_g, ln_b, pw, name):
    t = p.shape[0]
    tm, w, k = _tm(), CONV_W, CONV_K
    h = _halo(k)
    off = _offsets()
    ab, bb = off["a"] // w, off["b"] // w

    def body(ap, ac, an, bp, bc, bn, w_ref, b_ref, g_ref, beta_ref, pw_ref, u2_ref, out_ref, ext):
        i = pl.program_id(0)
        glu = lambda a, b: a * _sigmoid(b)
        _fill_ext(ext, glu(ap[...], bp[...]), glu(ac[...], bc[...]), glu(an[...], bn[...]), i, h, tm)
        for r0 in range(0, tm, ROW_CHUNK):
            u2_ref[r0:r0 + ROW_CHUNK, :] = _corr(ext, w_ref, k, h + r0, ROW_CHUNK, False) + b_ref[...]
        out_ref[...] = _conv_post(u2_ref[...], g_ref[...], beta_ref[...], pw_ref[...]).astype(BF16)

    vec = pl.BlockSpec((1, w), lambda i: (0, 0))
    row = pl.BlockSpec((tm, w), lambda i: (i, 0))
    return pl.pallas_call(
        body, name=name, grid=(t // tm,),
        in_specs=_halo_specs(w, lambda: ab, h, tm) + _halo_specs(w, lambda: bb, h, tm)
        + [pl.BlockSpec((k, w), lambda i: (0, 0)), vec, vec, vec, pl.BlockSpec((w, w), lambda i: (0, 0))],
        out_specs=[row, row],
        out_shape=[jax.ShapeDtypeStruct((t, w), F32), jax.ShapeDtypeStruct((t, w), BF16)],
        scratch_shapes=[pltpu.VMEM((tm + 2 * h, w), F32)],
        compiler_params=_params(("parallel",)))(p, p, p, p, p, p, dw_w, dw_b, ln_g, ln_b, pw)


def _conv_bwd1(u2, dmix, ln_g, ln_b, pw, name):
    t = u2.shape[0]
    tm, w = _tm(), CONV_W
    cb = _ret_w() // w

    def body(u2_ref, dy_ref, g_ref, beta_ref, pw_ref, du2_ref, dg_ref, db_ref, dpw_ref):
        i = pl.program_id(0)

        @pl.when(i == 0)
        def _():
            dg_ref[...] = jnp.zeros_like(dg_ref)
            db_ref[...] = jnp.zeros_like(db_ref)
            dpw_ref[...] = jnp.zeros_like(dpw_ref)

        _, vjp = jax.vjp(_conv_post, u2_ref[...], g_ref[...], beta_ref[...], pw_ref[...])
        du2, dg, db, dpw = vjp(dy_ref[...])
        du2_ref[...] = du2
        dg_ref[...] += dg
        db_ref[...] += db
        dpw_ref[...] += dpw

    vec = pl.BlockSpec((1, w), lambda i: (0, 0))
    row = pl.BlockSpec((tm, w), lambda i: (i, 0))
    mat = pl.BlockSpec((w, w), lambda i: (0, 0))
    return pl.pallas_call(
        body, name=name, grid=(t // tm,),
        in_specs=[row, pl.BlockSpec((tm, w), lambda i: (i, cb)), vec, vec, mat],
        out_specs=[row, vec, vec, mat],
        out_shape=[jax.ShapeDtypeStruct((t, w), F32), jax.ShapeDtypeStruct((1, w), F32),
                   jax.ShapeDtypeStruct((1, w), F32), jax.ShapeDtypeStruct((w, w), F32)],
        compiler_params=_params(("arbitrary",)))(u2, dmix, ln_g, ln_b, pw)


def _conv_bwd2(du2, p, dw_w, name):
    t = p.shape[0]
    tm, w, k = _tm(), CONV_W, CONV_K
    h = _halo(k)
    pad = k // 2
    off = _offsets()
    ab, bb = off["a"] // w, off["b"] // w

    def body(dp, dc, dn, ap, ac, an, bp, bc, bn, w_ref, da_ref, db_ref, dw_ref, dbias_ref, ext_d, ext_u):
        i = pl.program_id(0)

        @pl.when(i == 0)
        def _():
            dw_ref[...] = jnp.zeros_like(dw_ref)
            dbias_ref[...] = jnp.zeros_like(dbias_ref)

        glu = lambda a, b: a * _sigmoid(b)
        _fill_ext(ext_d, dp[...], dc[...], dn[...], i, h, tm)
        _fill_ext(ext_u, glu(ap[...], bp[...]), glu(ac[...], bc[...]), glu(an[...], bn[...]), i, h, tm)
        chunks = range(0, tm, ROW_CHUNK)
        acc_b = jnp.zeros((ROW_CHUNK, w), F32)
        for r0 in chunks:
            rows = slice(r0, r0 + ROW_CHUNK)
            du = _corr(ext_d, w_ref, k, h + r0, ROW_CHUNK, True)
            sg = _sigmoid(bc[rows, :])
            da_ref[rows, :] = du * sg
            db_ref[rows, :] = du * ac[rows, :] * sg * (1.0 - sg)
            acc_b = acc_b + ext_d[h + r0:h + r0 + ROW_CHUNK, :]
        dbias_ref[...] += jnp.sum(acc_b, axis=0, keepdims=True)
        for kk in range(k):
            acc = jnp.zeros((ROW_CHUNK, w), F32)
            for r0 in chunks:
                o = h + r0 + kk - pad
                acc = acc + ext_d[h + r0:h + r0 + ROW_CHUNK, :] * ext_u[o:o + ROW_CHUNK, :]
            dw_ref[kk:kk + 1, :] += jnp.sum(acc, axis=0, keepdims=True)

    vec = pl.BlockSpec((1, w), lambda i: (0, 0))
    row = pl.BlockSpec((tm, w), lambda i: (i, 0))
    kw = pl.BlockSpec((k, w), lambda i: (0, 0))
    return _call(
        body, name=name, grid=(t // tm,),
        in_specs=_halo_specs(w, lambda: 0, h, tm) + _halo_specs(w, lambda: ab, h, tm)
        + _halo_specs(w, lambda: bb, h, tm) + [kw],
        out_specs=[row, row, kw, vec],
        out_shape=[jax.ShapeDtypeStruct((t, w), F32), jax.ShapeDtypeStruct((t, w), F32),
                   jax.ShapeDtypeStruct((k, w), F32), jax.ShapeDtypeStruct((1, w), F32)],
        scratch=[pltpu.VMEM((tm + 2 * h, w), F32), pltpu.VMEM((tm + 2 * h, w), F32)],
        sem=("arbitrary",), args=(du2, du2, du2, p, p, p, p, p, p, dw_w))


def _ffn_tc():
    return _tile(D_FF, 512, LANES)


def _ffn_act_fwd(u, dw_w, dw_b, name):
    t = u.shape[0]
    tm, k, tc = _tm(), FFN_K, _ffn_tc()
    h = _halo(k)
    nj = D_FF // tc

    def body(vp, vc, vn, gp, gc, gn, wv, wg, bv, bg, out_ref, out_t_ref, ext_v, ext_g):
        i = pl.program_id(0)
        _fill_ext(ext_v, vp[...], vc[...], vn[...], i, h, tm)
        _fill_ext(ext_g, gp[...], gc[...], gn[...], i, h, tm)
        for r0 in range(0, tm, ROW_CHUNK):
            val = _corr(ext_v, wv, k, h + r0, ROW_CHUNK, False) + bv[...]
            gate = _corr(ext_g, wg, k, h + r0, ROW_CHUNK, False) + bg[...]
            out_ref[r0:r0 + ROW_CHUNK, :] = (_silu(gate) * val).astype(BF16)
        out_t_ref[...] = out_ref[...].T

    wspec = lambda s: pl.BlockSpec((k, tc), lambda i, j: (0, j + s))
    bspec = lambda s: pl.BlockSpec((1, tc), lambda i, j: (0, j + s))
    return pl.pallas_call(
        body, name=name, grid=(t // tm, nj),
        in_specs=_halo_specs(tc, lambda j: j, h, tm) + _halo_specs(tc, lambda j: j + nj, h, tm)
        + [wspec(0), wspec(nj), bspec(0), bspec(nj)],
        out_specs=[pl.BlockSpec((tm, tc), lambda i, j: (i, j)), pl.BlockSpec((tc, tm), lambda i, j: (j, i))],
        out_shape=[jax.ShapeDtypeStruct((t, D_FF), BF16), jax.ShapeDtypeStruct((D_FF, t), BF16)],
        scratch_shapes=[pltpu.VMEM((tm + 2 * h, tc), F32), pltpu.VMEM((tm + 2 * h, tc), F32)],
        compiler_params=_params(("parallel", "parallel")))(u, u, u, u, u, u, dw_w, dw_w, dw_b, dw_b)


def _ffn_act_bwd1(u, da, dw_w, dw_b, name):
    t = u.shape[0]
    tm, k, tc = _tm(), FFN_K, _ffn_tc()
    h = _halo(k)
    nj = D_FF // tc

    def body(vp, vc, vn, gp, gc, gn, wv, wg, bv, bg, da_ref, dv_ref, dg_ref, ext_v, ext_g):
        i = pl.program_id(0)
        _fill_ext(ext_v, vp[...], vc[...], vn[...], i, h, tm)
        _fill_ext(ext_g, gp[...], gc[...], gn[...], i, h, tm)
        for r0 in range(0, tm, ROW_CHUNK):
            rows = slice(r0, r0 + ROW_CHUNK)
            val = _corr(ext_v, wv, k, h + r0, ROW_CHUNK, False) + bv[...]
            gate = _corr(ext_g, wg, k, h + r0, ROW_CHUNK, False) + bg[...]
            _, vjp = jax.vjp(lambda a, b: _silu(b) * a, val, gate)
            dval, dgate = vjp(da_ref[rows, :])
            dv_ref[rows, :] = dval
            dg_ref[rows, :] = dgate

    wspec = lambda s: pl.BlockSpec((k, tc), lambda i, j: (0, j + s))
    bspec = lambda s: pl.BlockSpec((1, tc), lambda i, j: (0, j + s))
    dc = pl.pallas_call(
        body, name=name, grid=(t // tm, nj),
        in_specs=_halo_specs(tc, lambda j: j, h, tm) + _halo_specs(tc, lambda j: j + nj, h, tm)
        + [wspec(0), wspec(nj), bspec(0), bspec(nj), pl.BlockSpec((tm, tc), lambda i, j: (i, j))],
        out_specs=[pl.BlockSpec((tm, tc), lambda i, j: (i, j)), pl.BlockSpec((tm, tc), lambda i, j: (i, j))],
        out_shape=[jax.ShapeDtypeStruct((t, D_FF), F32), jax.ShapeDtypeStruct((t, D_FF), F32)],
        scratch_shapes=[pltpu.VMEM((tm + 2 * h, tc), F32), pltpu.VMEM((tm + 2 * h, tc), F32)],
        compiler_params=_params(("parallel", "parallel")))(u, u, u, u, u, u, dw_w, dw_w, dw_b, dw_b, da)
    return dc


def _dwconv_bwd(dcv, dcg, u, dw_w, name):
    t = u.shape[0]
    tm, k, tc = _tm(), FFN_K, _ffn_tc()
    h = _halo(k)
    pad = k // 2
    nj = D_FF // tc

    def body(vp, vc, vn, gp, gc, gn, up, uc, un, w_ref, du_ref, dw_ref, dbias_ref, ext_d, ext_u):
        jj, i = pl.program_id(0), pl.program_id(1)

        @pl.when(i == 0)
        def _():
            dw_ref[...] = jnp.zeros_like(dw_ref)
            dbias_ref[...] = jnp.zeros_like(dbias_ref)

        @pl.when(jj < nj)
        def _():
            _fill_ext(ext_d, vp[...], vc[...], vn[...], i, h, tm)

        @pl.when(jj >= nj)
        def _():
            _fill_ext(ext_d, gp[...], gc[...], gn[...], i, h, tm)

        _fill_ext(ext_u, up[...], uc[...], un[...], i, h, tm)
        acc_b = jnp.zeros((ROW_CHUNK, tc), F32)
        acc_w = [jnp.zeros((ROW_CHUNK, tc), F32) for _ in range(k)]
        for r0 in range(0, tm, ROW_CHUNK):
            d = ext_d[h + r0:h + r0 + ROW_CHUNK, :]
            du_ref[r0:r0 + ROW_CHUNK, :] = _corr(ext_d, w_ref, k, h + r0, ROW_CHUNK, True).astype(BF16)
            acc_b = acc_b + d
            for kk in range(k):
                o = h + r0 + kk - pad
                acc_w[kk] = acc_w[kk] + d * ext_u[o:o + ROW_CHUNK, :]
        dbias_ref[...] += jnp.sum(acc_b, axis=0, keepdims=True)
        for kk in range(k):
            dw_ref[kk:kk + 1, :] += jnp.sum(acc_w[kk], axis=0, keepdims=True)

    def hs(cb, live):
        r = tm // h
        row = lambda j, i: jnp.where(live(j), i, 0)
        return [pl.BlockSpec((h, tc), lambda j, i: (jnp.maximum(row(j, i) * r - 1, 0), cb(j))),
                pl.BlockSpec((tm, tc), lambda j, i: (row(j, i), cb(j))),
                pl.BlockSpec((h, tc), lambda j, i: (jnp.minimum((row(j, i) + 1) * r, (_t_rows() // h) - 1), cb(j)))]

    return pl.pallas_call(
        body, name=name, grid=(2 * nj, t // tm),
        in_specs=hs(lambda j: jnp.minimum(j, nj - 1), lambda j: j < nj)
        + hs(lambda j: jnp.maximum(j - nj, 0), lambda j: j >= nj)
        + hs(lambda j: j, lambda j: True) + [pl.BlockSpec((k, tc), lambda j, i: (0, j))],
        out_specs=[pl.BlockSpec((tm, tc), lambda j, i: (i, j)), pl.BlockSpec((k, tc), lambda j, i: (0, j)),
                   pl.BlockSpec((1, tc), lambda j, i: (0, j))],
        out_shape=[jax.ShapeDtypeStruct((t, 2 * D_FF), BF16), jax.ShapeDtypeStruct((k, 2 * D_FF), F32),
                   jax.ShapeDtypeStruct((1, 2 * D_FF), F32)],
        scratch_shapes=[pltpu.VMEM((tm + 2 * h, tc), F32), pltpu.VMEM((tm + 2 * h, tc), F32)],
        compiler_params=_params(("parallel", "arbitrary")))(dcv, dcv, dcv, dcg, dcg, dcg, u, u, u, dw_w)


def _na_geometry(rq):
    ncb = CTX_LEN // GRID_W
    rows_n = SEQ // GRID_W
    r = jnp.maximum(rq - ncb, 0)
    kstart = jnp.clip(r - NA_ROWS // 2, 0, rows_n - NA_ROWS)
    base = kstart - r + NA_ROWS - 1
    return rq >= ncb, kstart, base


def _na_core(q, kl, vl, kc, vc, bias, mask):
    qs = q * (NA_DH ** -0.5)
    s_l = jnp.where(mask, dot_nt(qs, kl) + bias, NEG)
    s_c = dot_nt(qs, kc)
    m = lax.stop_gradient(jnp.maximum(jnp.max(s_l, axis=1, keepdims=True), jnp.max(s_c, axis=1, keepdims=True)))
    e_l, e_c = jnp.exp(s_l - m), jnp.exp(s_c - m)
    inv = 1.0 / (jnp.sum(e_l, axis=1, keepdims=True) + jnp.sum(e_c, axis=1, keepdims=True))
    return dot_nn(e_l * inv, vl) + dot_nn(e_c * inv, vc)


def _na_mask(is_lat):
    nl = NA_ROWS * GRID_W
    q = lax.broadcasted_iota(jnp.int32, (GRID_W, nl), 0)
    w = lax.broadcasted_iota(jnp.int32, (GRID_W, nl), 1) % GRID_W
    cs = jnp.clip(q - NA_COLS // 2, 0, GRID_W - NA_COLS)
    return (w >= cs) & (w < cs + NA_COLS) & is_lat


def _na_bias(rb_ref):
    assert 2 * GRID_W == LANES
    lane = lax.broadcasted_iota(jnp.int32, (GRID_W, LANES), 1)
    tiles = []
    for kp in range(NA_ROWS // 2):
        ev = jnp.broadcast_to(rb_ref[2 * kp:2 * kp + 1, :], (GRID_W, LANES))
        od = jnp.broadcast_to(rb_ref[2 * kp + 1:2 * kp + 2, :], (GRID_W, LANES))
        ev = pltpu.roll(ev, LANES - (NA_COLS - 1), 1, stride=1, stride_axis=0)
        od = pltpu.roll(od, LANES - (NA_COLS - 1) - GRID_W, 1, stride=1, stride_axis=0)
        tiles.append(jnp.where(lane < GRID_W, ev, od))
    return jnp.concatenate(tiles, axis=1)


def _na_dbias(dbias, drb_ref):
    qi = lax.broadcasted_iota(jnp.int32, (GRID_W, GRID_W), 0)
    qj = lax.broadcasted_iota(jnp.int32, (GRID_W, GRID_W), 1)
    flip = (qi + qj == GRID_W - 1).astype(F32)
    rev = lax.dot_general(flip, dbias, (((1,), (0,)), ((), ())), precision=lax.Precision.HIGHEST,
                          preferred_element_type=F32)
    lane = lax.broadcasted_iota(jnp.int32, (GRID_W, LANES), 1)
    s_ev = LANES - (GRID_W - NA_COLS)
    for kp in range(NA_ROWS // 2):
        tile = rev[:, kp * LANES:(kp + 1) * LANES]
        ev = pltpu.roll(jnp.where(lane < GRID_W, tile, 0.0), s_ev, 1, stride=1, stride_axis=0)
        od = pltpu.roll(jnp.where(lane >= GRID_W, tile, 0.0), s_ev - GRID_W, 1, stride=1, stride_axis=0)
        drb_ref[2 * kp:2 * kp + 1, :] += jnp.sum(ev, axis=0, keepdims=True)
        drb_ref[2 * kp + 1:2 * kp + 2, :] += jnp.sum(od, axis=0, keepdims=True)


def _na_hps():
    return 2 if NA_HEADS % 2 == 0 else 1


def _na_specs(p_offsets):
    t = _t_rows()
    hps = _na_hps()
    wd = hps * NA_DH
    assert all(p_offsets[n] % wd == 0 for n in ("nq", "nk", "nv"))
    qb, kb, vb = (p_offsets[n] // wd for n in ("nq", "nk", "nv"))
    return [pl.BlockSpec((GRID_W, wd), lambda h, r: (r, qb + h)),
            pl.BlockSpec((t, wd), lambda h, r: (0, kb + h)),
            pl.BlockSpec((t, wd), lambda h, r: (0, vb + h)),
            pl.BlockSpec((hps, None, NA_ROWS, LANES), lambda h, r: (h, _na_geometry(r)[2], 0, 0))]


def _na_fwd(p, rb, name):
    t = p.shape[0]
    dh, nl = NA_DH, NA_ROWS * GRID_W

    hps = _na_hps()

    def body(q_ref, k_ref, v_ref, rb_ref, out_ref):
        rq = pl.program_id(1)
        is_lat, kstart, _ = _na_geometry(rq)
        start = pl.multiple_of(CTX_LEN + kstart * GRID_W, GRID_W)
        mask = _na_mask(is_lat)
        for hh in range(hps):
            cols = slice(hh * dh, (hh + 1) * dh)
            out = _na_core(q_ref[:, cols], k_ref[pl.ds(start, nl), cols], v_ref[pl.ds(start, nl), cols],
                           k_ref[0:CTX_LEN, cols], v_ref[0:CTX_LEN, cols], _na_bias(rb_ref.at[hh]), mask)
            out_ref[:, cols] = out.astype(BF16)

    return _call(
        body, name=name, grid=(NA_HEADS // hps, t // GRID_W), in_specs=_na_specs(_offsets()),
        out_specs=[pl.BlockSpec((GRID_W, hps * dh), lambda h, r: (r, h))],
        out_shape=[jax.ShapeDtypeStruct((t, _na_w()), BF16)],
        sem=("parallel", "arbitrary"), args=(p, p, p, rb))[0]


def _na_bwd(p, rb, dmix, name):
    t = p.shape[0]
    dh, nl = NA_DH, NA_ROWS * GRID_W

    hps = _na_hps()
    wd = hps * dh
    assert ((_ret_w() + CONV_W) // dh) % hps == 0
    ob = (_ret_w() + CONV_W) // wd

    def body(q_ref, k_ref, v_ref, rb_ref, dy_ref, dq_ref, dk_ref, dv_ref, drb_ref):
        rq = pl.program_id(1)
        is_lat, kstart, base = _na_geometry(rq)
        _, _, prev_base = _na_geometry(rq - 1)
        start = pl.multiple_of(CTX_LEN + kstart * GRID_W, GRID_W)

        @pl.when(rq == 0)
        def _():
            dk_ref[...] = jnp.zeros_like(dk_ref)
            dv_ref[...] = jnp.zeros_like(dv_ref)

        @pl.when((rq == 0) | (base != prev_base))
        def _():
            drb_ref[...] = jnp.zeros_like(drb_ref)

        mask = _na_mask(is_lat)
        for hh in range(hps):
            cols = slice(hh * dh, (hh + 1) * dh)
            _, vjp = jax.vjp(lambda *a: _na_core(*a, mask), q_ref[:, cols], k_ref[pl.ds(start, nl), cols],
                             v_ref[pl.ds(start, nl), cols], k_ref[0:CTX_LEN, cols], v_ref[0:CTX_LEN, cols],
                             _na_bias(rb_ref.at[hh]))
            dq, dkl, dvl, dkc, dvc, dbias = vjp(dy_ref[:, cols])
            dq_ref[:, cols] = dq
            dk_ref[pl.ds(start, nl), cols] += dkl
            dv_ref[pl.ds(start, nl), cols] += dvl
            dk_ref[0:CTX_LEN, cols] += dkc
            dv_ref[0:CTX_LEN, cols] += dvc
            _na_dbias(dbias, drb_ref.at[hh])

    return _call(
        body, name=name, grid=(NA_HEADS // hps, t // GRID_W),
        in_specs=_na_specs(_offsets()) + [pl.BlockSpec((GRID_W, wd), lambda h, r: (r, ob + h))],
        out_specs=[pl.BlockSpec((GRID_W, wd), lambda h, r: (r, h)), pl.BlockSpec((t, wd), lambda h, r: (0, h)),
                   pl.BlockSpec((t, wd), lambda h, r: (0, h)),
                   pl.BlockSpec((hps, None, NA_ROWS, LANES), lambda h, r: (h, _na_geometry(r)[2], 0, 0))],
        out_shape=[jax.ShapeDtypeStruct((t, _na_w()), F32), jax.ShapeDtypeStruct((t, _na_w()), F32),
                   jax.ShapeDtypeStruct((t, _na_w()), F32),
                   jax.ShapeDtypeStruct((NA_HEADS, NA_ROWS, NA_ROWS, LANES), F32)],
        sem=("parallel", "arbitrary"), args=(p, p, p, rb, dmix))


def _rpb_select():
    sel = np.zeros((2 * NA_ROWS - 1, NA_ROWS * NA_ROWS), np.float32)
    for b in range(NA_ROWS):
        for kh in range(NA_ROWS):
            sel[b + kh, b * NA_ROWS + kh] = 1.0
    return jnp.asarray(sel)


def _rpb_rows(rpb):
    pad = jnp.pad(rpb, ((0, 0), (0, 0), (0, LANES - (2 * NA_COLS - 1))))
    rows = jnp.einsum("rk,hrc->hkc", _rpb_select(), pad, precision=lax.Precision.HIGHEST)
    return rows.reshape(NA_HEADS, NA_ROWS, NA_ROWS, LANES)


def _rpb_rows_t(drb):
    flat = drb.reshape(NA_HEADS, NA_ROWS * NA_ROWS, LANES)
    out = jnp.einsum("rk,hkc->hrc", _rpb_select(), flat, precision=lax.Precision.HIGHEST)
    return out[:, :, :2 * NA_COLS - 1]


def _assemble_dp(dqr, dkr, dvr, dgate, da, db, dnq, dnk, dnv, name):
    t = dgate.shape[0]
    tm = _tm()
    off = _offsets()
    sizes = dict(q=_ret_qk_w(), k=_ret_qk_w(), v=_ret_w(), g=_ret_w(), a=CONV_W, b=CONV_W, nq=_na_w(), nk=_na_w(), nv=_na_w())

    def body(q_ref, k_ref, v_ref, g_ref, a_ref, b_ref, nq_ref, nk_ref, nv_ref, o_ref):
        def put(n, val):
            o_ref[:, off[n]:off[n] + sizes[n]] = val.astype(BF16)

        put("q", q_ref[0] + q_ref[1])
        put("k", k_ref[0] + k_ref[1])
        put("v", v_ref[0] + v_ref[1])
        put("g", g_ref[...])
        put("a", a_ref[...])
        put("b", b_ref[...])
        put("nq", nq_ref[...])
        put("nk", nk_ref[...])
        put("nv", nv_ref[...])

    two = lambda w: pl.BlockSpec((2, tm, w), lambda i: (0, i, 0))
    one = lambda w: pl.BlockSpec((tm, w), lambda i: (i, 0))
    return pl.pallas_call(
        body, name=name, grid=(t // tm,),
        in_specs=[two(sizes["q"]), two(sizes["k"]), two(sizes["v"]), one(sizes["g"]), one(CONV_W), one(CONV_W),
                  one(_na_w()), one(_na_w()), one(_na_w())],
        out_specs=one(_d_in()), out_shape=jax.ShapeDtypeStruct((t, _d_in()), BF16),
        compiler_params=_params(("parallel",)))(dqr, dkr, dvr, dgate, da, db, dnq, dnk, dnv)


def _adamw(w, m, v, gs, name):
    nl, r, c = w.shape
    stacked = not isinstance(gs, (list, tuple))
    if stacked:
        gs = [gs]
    assert stacked or len(gs) == nl
    g_n = gs[0].shape[-3]
    block_bytes = 2 * 1024 * 1024
    rows = min(block_bytes // (4 * c), block_bytes // (g_n * c * gs[0].dtype.itemsize))
    tr = _tile(r, max(2 * SUBLANES, rows // (2 * SUBLANES) * (2 * SUBLANES)), 2 * SUBLANES)
    nt = r // tr
    c1 = 1.0 - ADAM_B1 ** ADAM_STEP
    c2 = 1.0 - ADAM_B2 ** ADAM_STEP

    def body(w_ref, m_ref, v_ref, *rest):
        g_refs, (go_ref, d_ref, mo_ref, vo_ref) = rest[:len(gs)], rest[len(gs):]
        layer = pl.program_id(0)
        for ll in range(len(gs)):
            @pl.when(jnp.logical_or(stacked, layer == ll))
            def _():
                g_ref = g_refs[ll]
                g = g_ref[0].astype(F32)
                for j in range(1, g_n):
                    g = g + g_ref[j].astype(F32)
                mn = ADAM_B1 * m_ref[...] + (1.0 - ADAM_B1) * g
                vn = ADAM_B2 * v_ref[...] + (1.0 - ADAM_B2) * (g * g)
                m_hat = mn / c1
                v_hat = vn / c2
                go_ref[...] = g
                d_ref[...] = -ADAM_LR * (m_hat / (jnp.sqrt(v_hat) + ADAM_EPS) + ADAM_WD * w_ref[...])
                mo_ref[...] = mn
                vo_ref[...] = vn

    def g_spec(ll):
        if stacked:
            return pl.BlockSpec((None, g_n, tr, c), lambda l, i: (l, 0, i, 0))
        return pl.BlockSpec((g_n, tr, c), lambda l, i: (0, jnp.where(l == ll, i, jnp.where(l < ll, 0, nt - 1)), 0))

    blk = pl.BlockSpec((None, tr, c), lambda l, i: (l, i, 0))
    sds = jax.ShapeDtypeStruct((nl, r, c), F32)
    return _call(
        body, name=name, grid=(nl, nt),
        in_specs=[blk, blk, blk] + [g_spec(ll) for ll in range(len(gs))],
        out_specs=[blk, blk, blk, blk], out_shape=[sds, sds, sds, sds],
        sem=("arbitrary", "arbitrary"), args=(w, m, v, *gs))


def _sum_devices(g, name):
    _, r, c = g.shape
    tr = _tile(r, 512, SUBLANES)

    def body(g_ref, o_ref):
        acc = g_ref[0]
        for j in range(1, N_DEV):
            acc = acc + g_ref[j]
        o_ref[...] = acc

    return pl.pallas_call(body, name=name, grid=(r // tr,), in_specs=[pl.BlockSpec((N_DEV, tr, c), lambda i: (0, i, 0))],
                          out_specs=pl.BlockSpec((tr, c), lambda i: (i, 0)), out_shape=jax.ShapeDtypeStruct((r, c), F32),
                          compiler_params=_params(("parallel",)))(g)


def _ada_fwd(c16, w_ada, b_shard, name):
    nl, d, cs = w_ada.shape
    tk = _tile(d, 512, LANES)
    nk = d // tk

    def body(c_ref, w_ref, b_ref, o_ref):
        kk = pl.program_id(1)

        @pl.when(kk == 0)
        def _():
            o_ref[...] = jnp.broadcast_to(b_ref[...], o_ref.shape)

        o_ref[...] += _dg(_silu(c_ref[...]), w_ref[...], 1, 0)

    return pl.pallas_call(
        body, name=name, grid=(nl, nk),
        in_specs=[pl.BlockSpec((16, tk), lambda l, kk: (0, kk)), pl.BlockSpec((None, tk, cs), lambda l, kk: (l, kk, 0)),
                  pl.BlockSpec((None, 1, cs), lambda l, kk: (l, 0, 0))],
        out_specs=pl.BlockSpec((None, 16, cs), lambda l, kk: (l, 0, 0)),
        out_shape=jax.ShapeDtypeStruct((nl, 16, cs), F32),
        compiler_params=_params(("parallel", "arbitrary")))(c16, w_ada, b_shard)


def _ada_bwd(c16, dm16, w_ada, name):
    nl, d, cs = w_ada.shape
    td = _tile(d, 512, LANES)

    def body(c_ref, dm_ref, w_ref, gw_ref, dc_ref):
        cv = c_ref[...]
        s, vjp = jax.vjp(_silu, cv)
        gw_ref[...] = _dg(s, dm_ref[...], 0, 0)
        ds = _dg(dm_ref[...], w_ref[...], 1, 1)
        dc_ref[...] = vjp(ds)[0]

    return pl.pallas_call(
        body, name=name, grid=(nl, d // td),
        in_specs=[pl.BlockSpec((16, td), lambda l, i: (0, i)), pl.BlockSpec((None, 16, cs), lambda l, i: (l, 0, 0)),
                  pl.BlockSpec((None, td, cs), lambda l, i: (l, i, 0))],
        out_specs=[pl.BlockSpec((None, td, cs), lambda l, i: (l, i, 0)), pl.BlockSpec((None, 16, td), lambda l, i: (l, 0, i))],
        out_shape=[jax.ShapeDtypeStruct((nl, d, cs), F32), jax.ShapeDtypeStruct((nl, 16, d), F32)],
        compiler_params=_params(("parallel", "parallel")))(c16, dm16, w_ada)


def _pack_rows(shape):
    n = int(np.prod(shape))
    return SUBLANES * (-(-n // (LANES * SUBLANES)))


def _pack(arrays, row_align):
    parts, total = [], 0
    for a in arrays:
        flat = a.reshape(-1).astype(F32)
        rows = _pack_rows(a.shape)
        total += rows
        parts += [flat, jnp.zeros((rows * LANES - flat.shape[0],), F32)]
    parts.append(jnp.zeros(((-total % row_align) * LANES,), F32))
    return jnp.concatenate([p for p in parts if p.shape[0]]).reshape(-1, LANES)


def _unpack(packed, shapes):
    out, r = [], 0
    for s in shapes:
        rows = _pack_rows(s)
        out.append(packed[r:r + rows].reshape(-1)[:int(np.prod(s))].reshape(s))
        r += rows
    return out


def _rope_tables():
    half, nf = RET_DK // 2, RET_DK // 4
    pos = jnp.arange(SEQ)
    row = (pos // GRID_W).astype(F32)
    col = (pos % GRID_W).astype(F32)
    inv = ROPE_BASE ** (-jnp.arange(nf, dtype=F32) / nf)
    ar, ac = row[:, None] * inv[None, :], col[:, None] * inv[None, :]
    cos = jnp.concatenate([jnp.cos(ar), jnp.cos(ar), jnp.cos(ac), jnp.cos(ac)], axis=-1)
    sin = jnp.concatenate([-jnp.sin(ar), jnp.sin(ar), -jnp.sin(ac), jnp.sin(ac)], axis=-1)
    cos = jnp.concatenate([jnp.ones((CTX_LEN, RET_DK), F32), cos], axis=0)
    sin = jnp.concatenate([jnp.zeros((CTX_LEN, RET_DK), F32), sin], axis=0)
    return cos, sin


def _layer_fwd(l, x, mod4, w, cst, arrived):
    n = lambda s: f"l{l}_{s}"
    d = D_MODEL
    h1, h1_t = _normmod_fwd(x, w["norm1_g"], mod4, 0, n("norm1"))
    w["w_in"] = _cols_from_shards(arrived("w_in", h1), n("w_in_cols"))
    p = _mm(h1, w["w_in"], n("proj_in"))
    o2, states = _ret_fwd(p, cst["cos"], cst["sin"], w["ret_decay"], cst["order"], n("ret_fwd"))
    ret_out = _ggn_fwd(o2, p, w["ret_gn_g"], n("ret_gn"))
    u2, conv_out = _conv_fwd(p, w["conv_dw_w"], w["conv_dw_b"], w["conv_ln_g"], w["conv_ln_b"], w["conv_pw"], n("conv_fwd"))
    na_out = _na_fwd(p, w["rb"], n("na_fwd"))
    mix = jnp.concatenate([ret_out, conv_out, na_out], axis=1)
    w["w_out"] = arrived("w_out", mix).reshape(_d_mix(), d)
    g1 = _mm(mix, w["w_out"], n("proj_out"))
    x1 = _gate_res_fwd(x, g1, mod4, 2, n("res1"))
    h2, h2_t = _normmod_fwd(x1, w["norm2_g"], mod4, 1, n("norm2"))
    w["ffn_up"] = arrived("ffn_up", h2)
    u = _mm(h2, w["ffn_up"], n("ffn_up"), b3=True)
    a, a_t = _ffn_act_fwd(u, w["ffn_dw_w"], w["ffn_dw_b"], n("ffn_act"))
    w["ffn_down"] = arrived("ffn_down", a).reshape(D_FF, d)
    f = _mm(a, w["ffn_down"], n("ffn_down"))
    x2 = _gate_res_fwd(x1, f, mod4, 5, n("res2"))
    saved = dict(x=x, h1_t=h1_t, p=p, o2=o2, states=states, u2=u2, mix=mix, g1=g1, x1=x1, h2_t=h2_t, u=u, a_t=a_t, f=f)
    return x2, saved


def _layer_bwd(l, dx2, s, mod4, w, cst, send):
    n = lambda t: f"l{l}_{t}"
    d = D_MODEL
    dfg, dg2 = _gate_res_bwd(dx2, s["f"], mod4, 5, n("res2_bwd"))
    da = _mm(dfg, w["ffn_down"], n("ffn_down_dx"), tb=True)
    d_ffn_down = _mm(s["a_t"], dfg, n("ffn_down_dw"), out_dtype=BF16, tm_max=DW_TM)
    tok = send(("ffn_down", l), d_ffn_down.reshape(N_DEV, D_FF // N_DEV, d))
    dcv, dcg = _ffn_act_bwd1(s["u"], da, w["ffn_dw_w"], _after(w["ffn_dw_b"], tok), n("ffn_act_bwd"))
    du, d_ffn_dw_w, d_ffn_dw_b = _dwconv_bwd(dcv, dcg, s["u"], w["ffn_dw_w"], n("ffn_dw_bwd"))
    d_ffn_dw_b = d_ffn_dw_b[0]
    dh2 = _mm(du, w["ffn_up"], n("ffn_up_dx"), tb=True, b3=True)
    d_ffn_up = _mm(s["h2_t"], du, n("ffn_up_dw"), out_dtype=BF16, tm_max=DW_TM,
                      o_cs=2 * D_FF // N_DEV)
    tok = send(("ffn_up", l), d_ffn_up)
    (dx1, dn2, dsh2, dsc2) = _normmod_bwd(s["x1"], _after(w["norm2_g"], tok), mod4, 1, dh2, dx2, n("norm2_bwd"))
    dgg, dg1 = _gate_res_bwd(dx1, s["g1"], mod4, 2, n("res1_bwd"))
    dmix = _mm(dgg, w["w_out"], n("proj_out_dx"), tb=True)
    d_w_out = _mm(_transpose_bf16(s["mix"], n("mix_t")), dgg, n("proj_out_dw"), out_dtype=BF16, tm_max=DW_TM)
    tok = send(("w_out", l), d_w_out.reshape(N_DEV, _d_mix() // N_DEV, d))
    do, dgate, dgn = _ggn_bwd(s["o2"], s["p"], _after(w["ret_gn_g"], tok), dmix, n("ret_gn_bwd"))
    dqr, dkr, dvr, ddec = _ret_bwd(s["p"], cst["cos"], cst["sin"], w["ret_decay"], cst["order"], s["states"], do, n("ret_bwd"))
    du2, dlng, dlnb, dpw = _conv_bwd1(s["u2"], dmix, w["conv_ln_g"], w["conv_ln_b"], w["conv_pw"], n("conv_bwd1"))
    dca, dcb, ddww, ddwb = _conv_bwd2(du2, s["p"], w["conv_dw_w"], n("conv_bwd2"))
    dnq, dnk, dnv, drb = _na_bwd(s["p"], w["rb"], dmix, n("na_bwd"))
    dp = _assemble_dp(dqr, dkr, dvr, dgate, dca, dcb, dnq, dnk, dnv, n("dproj"))
    h1_t = s["h1_t"]
    half = d // 2
    for i in range(2):
        d_w_in = _mm(h1_t[i * half:(i + 1) * half], dp, n(f"proj_in_dw{i}"), out_dtype=BF16, tm_max=DW_TM,
                        o_cs=_d_in() // N_DEV)
        tok = send(("w_in", l, i), d_w_in)
    dh1 = _mm(dp, w["w_in"], n("proj_in_dx"), tb=True, after=tok)
    (dx, dn1, dsh1, dsc1) = _normmod_bwd(s["x"], _after(w["norm1_g"], tok), mod4, 0, dh1, dx1, n("norm1_bwd"))
    dmod = jnp.concatenate([dsh1, dsc1, dg1, dsh2, dsc2, dg2], axis=1)
    small = dict(norm1_g=dn1[0], ret_decay=ddec[:, :, 0, 0], ret_gn_g=dgn[0], conv_dw_w=ddww, conv_dw_b=ddwb[0],
                 conv_ln_g=dlng[0], conv_ln_b=dlnb[0], conv_pw=dpw, na_rpb=_rpb_rows_t(drb), norm2_g=dn2[0],
                 ffn_dw_w=d_ffn_dw_w, ffn_dw_b=d_ffn_dw_b)
    return dx, dmod, small


def _d_mix():
    return _ret_w() + CONV_W + _na_w()


_SMALL = ["c_ctx", "b_ada", "norm1_g", "ret_decay", "ret_gn_g", "conv_dw_w", "conv_dw_b", "conv_ln_g", "conv_ln_b",
          "conv_pw", "na_rpb", "norm2_g", "ffn_dw_w", "ffn_dw_b", "final_g"]
_SMALL_SHARD_AXIS = {"conv_dw_w": 2, "conv_pw": 1, "ffn_dw_w": 2}


def kernel(x, c, ctx, c_ctx, w_ada, b_ada, norm1_g, w_in, ret_decay, ret_gn_g, conv_dw_w, conv_dw_b, conv_ln_g, conv_ln_b, conv_pw, na_rpb, w_out, norm2_g, ffn_up, ffn_dw_w, ffn_dw_b, ffn_down, final_g, loss_target, m_c_ctx, m_w_ada, m_b_ada, m_norm1_g, m_w_in, m_ret_decay, m_ret_gn_g, m_conv_dw_w, m_conv_dw_b, m_conv_ln_g, m_conv_ln_b, m_conv_pw, m_na_rpb, m_w_out, m_norm2_g, m_ffn_up, m_ffn_dw_w, m_ffn_dw_b, m_ffn_down, m_final_g, v_c_ctx, v_w_ada, v_b_ada, v_norm1_g, v_w_in, v_ret_decay, v_ret_gn_g, v_conv_dw_w, v_conv_dw_b, v_conv_ln_g, v_conv_ln_b, v_conv_pw, v_na_rpb, v_w_out, v_norm2_g, v_ffn_up, v_ffn_dw_w, v_ffn_dw_b, v_ffn_down, v_final_g):
    d, nl = D_MODEL, DEPTH
    cs = 6 * d // N_DEV
    me = _my_index()
    weights = dict(c_ctx=c_ctx, w_ada=w_ada, b_ada=b_ada, norm1_g=norm1_g, w_in=w_in, ret_decay=ret_decay, ret_gn_g=ret_gn_g,
                   conv_dw_w=conv_dw_w, conv_dw_b=conv_dw_b, conv_ln_g=conv_ln_g, conv_ln_b=conv_ln_b, conv_pw=conv_pw,
                   na_rpb=na_rpb, w_out=w_out, norm2_g=norm2_g, ffn_up=ffn_up, ffn_dw_w=ffn_dw_w, ffn_dw_b=ffn_dw_b,
                   ffn_down=ffn_down, final_g=final_g)
    mom = dict(c_ctx=m_c_ctx, w_ada=m_w_ada, b_ada=m_b_ada, norm1_g=m_norm1_g, w_in=m_w_in, ret_decay=m_ret_decay,
               ret_gn_g=m_ret_gn_g, conv_dw_w=m_conv_dw_w, conv_dw_b=m_conv_dw_b, conv_ln_g=m_conv_ln_g,
               conv_ln_b=m_conv_ln_b, conv_pw=m_conv_pw, na_rpb=m_na_rpb, w_out=m_w_out, norm2_g=m_norm2_g,
               ffn_up=m_ffn_up, ffn_dw_w=m_ffn_dw_w, ffn_dw_b=m_ffn_dw_b, ffn_down=m_ffn_down, final_g=m_final_g)
    var = dict(c_ctx=v_c_ctx, w_ada=v_w_ada, b_ada=v_b_ada, norm1_g=v_norm1_g, w_in=v_w_in, ret_decay=v_ret_decay,
               ret_gn_g=v_ret_gn_g, conv_dw_w=v_conv_dw_w, conv_dw_b=v_conv_dw_b, conv_ln_g=v_conv_ln_g,
               conv_ln_b=v_conv_ln_b, conv_pw=v_conv_pw, na_rpb=v_na_rpb, w_out=v_w_out, norm2_g=v_norm2_g,
               ffn_up=v_ffn_up, ffn_dw_w=v_ffn_dw_w, ffn_dw_b=v_ffn_dw_b, ffn_down=v_ffn_down, final_g=v_final_g)

    big_names = ["w_in", "w_out", "ffn_up", "ffn_down"]
    shards = {(nm, l): _cast_bf16(weights[nm][l], f"cast_{nm}{l}") for l in range(nl) for nm in big_names}
    small_sharded = _pack([conv_dw_w, conv_pw, ffn_dw_w], SUBLANES)
    c_rows = jnp.pad(c, ((0, SUBLANES - 1), (0, 0)))
    gathered = _run_comm(_Gather([c_rows, small_sharded, shards[("w_in", 0)]]), "gather_first")
    c_all = gathered[0][:, 0, :]
    def whole(rows, shard_shape, axis):
        n_el = int(np.prod(shard_shape))
        parts = rows.reshape(N_DEV, -1)[:, :n_el].reshape((N_DEV,) + tuple(shard_shape))
        parts = jnp.moveaxis(parts, 0, axis)
        return parts.reshape(shard_shape[:axis] + (N_DEV * shard_shape[axis],) + shard_shape[axis + 1:])

    r0 = _pack_rows(conv_dw_w.shape)
    r1 = r0 + _pack_rows(conv_pw.shape)
    r2 = r1 + _pack_rows(ffn_dw_w.shape)
    full_conv_dw_w = whole(gathered[1][:, :r0], conv_dw_w.shape, 2)
    full_conv_pw = whole(gathered[1][:, r0:r1], conv_pw.shape, 1)
    full_ffn_dw_w = whole(gathered[1][:, r1:r2], ffn_dw_w.shape, 2)

    c16 = jnp.concatenate([c_all, jnp.broadcast_to(c_ctx[None, :], (N_DEV, d))], axis=0)
    b_shard = lax.dynamic_slice_in_dim(b_ada, me * cs, cs, axis=1)[:, None, :]
    m_shard = _ada_fwd(c16, w_ada, b_shard, "ada_fwd")
    m_all = _run_comm(_Gather([m_shard.reshape(nl * 16, cs)]), "gather_mod")[0]
    m_full = m_all.reshape(N_DEV, nl, 16, cs).transpose(1, 2, 0, 3).reshape(nl, 16, 6 * d)
    m_lat = lax.dynamic_index_in_dim(m_full, me, axis=1, keepdims=False)
    mod = jnp.stack([m_full[:, N_DEV], m_lat], axis=1).reshape(nl, 2, 6, 1, d)

    arriving, token = {}, m_all
    for l in range(nl):
        for nm in big_names:
            if (nm, l) != ("w_in", 0):
                arriving[(nm, l)], token = _split_start(shards[(nm, l)], True, f"gather_{nm}{l}", token)
    mod = _after(mod, token)

    cos, sin = _rope_tables()
    cst = dict(cos=cos, sin=sin, order=_chunk_order())
    layer_w = []
    for l in range(nl):
        layer_w.append(dict(
            norm1_g=norm1_g[l][None], norm2_g=norm2_g[l][None], ret_decay=ret_decay[l], ret_gn_g=ret_gn_g[l][None],
            conv_dw_w=full_conv_dw_w[l], conv_dw_b=conv_dw_b[l][None], conv_ln_g=conv_ln_g[l][None],
            conv_ln_b=conv_ln_b[l][None], conv_pw=full_conv_pw[l], rb=_rpb_rows(na_rpb[l]),
            ffn_dw_w=full_ffn_dw_w[l], ffn_dw_b=ffn_dw_b[l][None]))

    xs = jnp.concatenate([ctx[0], x[0]], axis=0)
    saved = []
    for l in range(nl):
        def arrived(nm, after, l=l):
            if (nm, l) == ("w_in", 0):
                return gathered[2]
            return _split_wait(arriving[(nm, l)], after, f"arrived_{nm}{l}")

        xs, sv = _layer_fwd(l, xs, mod[l], layer_w[l], cst, arrived)
        saved.append(sv)
    loss_tile, dxs, dfinal = _loss_head(xs, final_g[None], loss_target[0], "loss_head")
    loss = lax.psum(loss_tile[0, 0], ("x", "y", "c"))

    dmods, smalls = [None] * nl, [None] * nl
    leaving, last = {}, [loss_tile]

    def send(key, partial):
        leaving[key], token = _split_start(partial, False, "send_" + "_".join(str(k) for k in key), last[0])
        last[0] = token
        return token

    per_layer = [nm for nm in _SMALL if nm not in ("c_ctx", "b_ada", "final_g")]
    small_packs, small_arriving = [None] * nl, [None] * nl
    for l in reversed(range(nl)):
        dxs, dmods[l], smalls[l] = _layer_bwd(l, dxs, saved[l], mod[l], layer_w[l], cst, send)
        small_packs[l] = _pack([smalls[l][nm] for nm in per_layer], 512)
        if l > 0:
            small_arriving[l], last[0] = _split_start(small_packs[l], True, f"gather_small_grads{l}", last[0])
    grad_x = dxs[CTX_LEN:][None]

    arrived_grad = lambda key, after: _split_wait(leaving[key], after, "got_" + "_".join(str(k) for k in key))
    out_big = {}
    after = dxs
    for nm in ["ffn_down", "ffn_up", "w_out"]:
        out_big[nm] = _adamw(weights[nm], mom[nm], var[nm], [arrived_grad((nm, l), after) for l in range(nl)],
                                f"adamw_{nm}")
        after = out_big[nm][0]

    dm_mine = jnp.stack(dmods).reshape(nl * 2, 6 * d)
    dm_rows = jnp.pad(dm_mine, ((0, SUBLANES - nl * 2), (0, 0)))
    dm_all = _run_comm(_Gather([dm_rows]), "gather_dmod", after=after)[0][:, :nl * 2].reshape(N_DEV, nl, 2, 6 * d)
    dm16_full = jnp.concatenate([dm_all[:, :, 1].transpose(1, 0, 2), dm_all[:, :, 0].transpose(1, 0, 2)], axis=1)
    dm16 = lax.dynamic_slice_in_dim(dm16_full, me * cs, cs, axis=2)
    g_w_ada, dc16 = _ada_bwd(c16, dm16, w_ada, "ada_bwd")

    shared = dict(c_ctx=jnp.sum(dc16[:, N_DEV:], axis=(0, 1)),
                  b_ada=jnp.sum(jnp.stack(dmods).reshape(nl, 2, 6 * d), axis=1), final_g=dfinal[0])
    shared_all = _run_comm(_Gather([_pack(list(shared.values()), SUBLANES)]), "gather_shared_grads")[0]
    small_arriving[0], token = _split_start(small_packs[0], True, "gather_small_grads0", shared_all)

    out_big["w_ada"] = _adamw(w_ada, m_w_ada, v_w_ada, g_w_ada[:, None], "adamw_w_ada")
    halves = lambda a: a.reshape(2 * nl, d // 2, a.shape[2])
    res = _adamw(halves(w_in), halves(m_w_in), halves(v_w_in),
                    [arrived_grad(("w_in", l, i), token) for l in range(nl) for i in range(2)], "adamw_w_in")
    out_big["w_in"] = [r.reshape(w_in.shape) for r in res]

    g_small = dict(zip(shared, _unpack(_sum_devices(shared_all, "sum_shared_grads"), [v.shape for v in shared.values()])))
    per = []
    for l in range(nl):
        got = _split_wait(small_arriving[l], res[0], f"arrived_small_grads{l}")
        per.append(_unpack(_sum_devices(got, f"sum_small_grads{l}"), [smalls[l][nm].shape for nm in per_layer]))
    g_small.update({nm: jnp.stack([per[l][i] for l in range(nl)]) for i, nm in enumerate(per_layer)})
    for nm, ax in _SMALL_SHARD_AXIS.items():
        n_sh = weights[nm].shape[ax]
        g_small[nm] = lax.dynamic_slice_in_dim(g_small[nm], me * n_sh, n_sh, axis=ax)
    shapes_own = [weights[nm].shape for nm in _SMALL]
    pk = lambda src: _pack([src[nm] for nm in _SMALL], 2 * SUBLANES)[None]
    res_small = _adamw(pk(weights), pk(mom), pk(var), pk(g_small)[:, None], "adamw_small")
    out_small = [dict(zip(_SMALL, _unpack(r[0], shapes_own))) for r in res_small]

    names = ["c_ctx", "w_ada", "b_ada", "norm1_g", "w_in", "ret_decay", "ret_gn_g", "conv_dw_w", "conv_dw_b", "conv_ln_g",
             "conv_ln_b", "conv_pw", "na_rpb", "w_out", "norm2_g", "ffn_up", "ffn_dw_w", "ffn_dw_b", "ffn_down", "final_g"]
    outs = [loss, grad_x]
    for kind in range(4):
        for nm in names:
            outs.append(out_big[nm][kind] if nm in out_big else out_small[kind][nm])
    return tuple(outs)
```

```python
import numpy as np
import jax
import jax.numpy as jnp
from jax import lax
from jax.experimental import pallas as pl
from jax.experimental.pallas import tpu as pltpu

D_MODEL = 2048
SEQ = 4096
DEPTH = 2
GRID_W = 64
CTX_LEN = 256
RET_HEADS = 4
RET_DK = 128
RET_DV = 256
RET_CHUNK = 128
CONV_W = 512
CONV_K = 31
NA_HEADS = 4
NA_DH = 128
NA_ROWS = 8
NA_COLS = 16
D_FF = 5632
FFN_K = 3
ROPE_BASE = 10000.0
EPS = 1e-6
ADAM_LR = 0.001
ADAM_B1 = 0.9
ADAM_B2 = 0.999
ADAM_EPS = 1e-08
ADAM_WD = 0.01
ADAM_STEP = 10
N_DEV = 8

LANES = 128
SUBLANES = 8
VMEM_LIMIT = 56 * 1024 * 1024
ROW_CHUNK = 32

F32 = jnp.float32
BF16 = jnp.bfloat16
MESH = pl.DeviceIdType.MESH
NEG = -1e30


def _ret_qk_w():
    return RET_HEADS * RET_DK


def _ret_w():
    return RET_HEADS * RET_DV


def _na_w():
    return NA_HEADS * NA_DH


def _d_in():
    return 2 * _ret_qk_w() + 2 * _ret_w() + 2 * CONV_W + 3 * _na_w()


def _offsets():
    sizes = [_ret_qk_w(), _ret_qk_w(), _ret_w(), _ret_w(), CONV_W, CONV_W, _na_w(), _na_w(), _na_w()]
    offs = [0]
    for s in sizes[:-1]:
        offs.append(offs[-1] + s)
    return dict(zip(["q", "k", "v", "g", "a", "b", "nq", "nk", "nv"], offs))


def _t_rows():
    return CTX_LEN + SEQ


def _tm():
    return CTX_LEN


def _params(sem=None):
    kw = dict(vmem_limit_bytes=VMEM_LIMIT)
    if sem is not None:
        kw["dimension_semantics"] = sem
    return pltpu.CompilerParams(**kw)


def _tile(n, pref, align):
    best = None
    for t in range(align, min(n, pref) + 1, align):
        if n % t == 0:
            best = t
    return best if best is not None else n


def _dg(a, b, ca, cb):
    return lax.dot_general(a.astype(BF16), b.astype(BF16), (((ca,), (cb,)), ((), ())), preferred_element_type=F32)


@jax.custom_vjp
def dot_nn(a, b):
    return _dg(a, b, 1, 0)


dot_nn.defvjp(lambda a, b: (_dg(a, b, 1, 0), (a, b)),
              lambda r, g: (_dg(g, r[1], 1, 1), _dg(r[0], g, 0, 0)))


@jax.custom_vjp
def dot_nt(a, b):
    return _dg(a, b, 1, 1)


dot_nt.defvjp(lambda a, b: (_dg(a, b, 1, 1), (a, b)),
              lambda r, g: (_dg(g, r[1], 1, 0), _dg(g, r[0], 0, 0)))


@jax.custom_vjp
def dot_tn(a, b):
    return _dg(a, b, 0, 0)


dot_tn.defvjp(lambda a, b: (_dg(a, b, 0, 0), (a, b)),
              lambda r, g: (_dg(r[1], g, 1, 1), _dg(r[0], g, 1, 0)))


def _sigmoid(x):
    return 0.5 * jnp.tanh(0.5 * x) + 0.5


def _silu(x):
    return x * _sigmoid(x)


def _my_pos():
    return lax.axis_index("x"), lax.axis_index("y"), lax.axis_index("c")


def _my_index():
    x, y, c = _my_pos()
    return 4 * x + 2 * y + c


_ANY = pl.BlockSpec(memory_space=pl.ANY)


class _Gather:
    def __init__(self, arrays):
        self.arrays = list(arrays)
        n = len(self.arrays)
        self.out_shape = [jax.ShapeDtypeStruct((N_DEV,) + a.shape, a.dtype) for a in self.arrays]
        self.scratch = [pltpu.SemaphoreType.DMA((n, 7)), pltpu.SemaphoreType.DMA((n, 7)), pltpu.SemaphoreType.DMA((n,))]

    def _plan(self, xs, outs, sems):
        send_sems, recv_sems, local_sems = sems
        n = len(self.arrays)
        x, y, c = _my_pos()
        me, sibling = (x, y, c), (x, y, 1 - c)
        chips = [(1 - x, y), (x, 1 - y), (1 - x, 1 - y)]

        def slot(a, p):
            return outs[a].at[4 * p[0] + 2 * p[1] + p[2]]

        def copy(a, k, block, to, src=None):
            return pltpu.make_async_remote_copy(
                src_ref=slot(a, block) if src is None else src, dst_ref=slot(a, block),
                send_sem=send_sems.at[a, k], recv_sem=recv_sems.at[a, k], device_id=to, device_id_type=MESH)

        mine = [pltpu.make_async_copy(xs[a], slot(a, me), local_sems.at[a]) for a in range(n)]
        first = []
        for a in range(n):
            first.append(copy(a, 0, me, sibling, src=xs[a]))
            first += [copy(a, 1 + j, me, (*chip, c), src=xs[a]) for j, chip in enumerate(chips)]
        return n, c, me, sibling, chips, copy, mine, first

    def start(self, xs, outs, sems):
        _, _, _, _, _, _, mine, first = self._plan(xs, outs, sems)
        for m in mine:
            m.start()
        for cp in first:
            cp.start()

    def finish(self, xs, outs, sems):
        n, c, me, sibling, chips, copy, mine, first = self._plan(xs, outs, sems)
        passed = []
        for a in range(n):
            for j, chip in enumerate(chips):
                copy(a, 1 + j, (*chip, c), me).wait_recv()
                p = copy(a, 4 + j, (*chip, c), sibling)
                p.start()
                passed.append(p)
        for a in range(n):
            copy(a, 0, sibling, me).wait_recv()
            for j, chip in enumerate(chips):
                copy(a, 4 + j, (*chip, 1 - c), me).wait_recv()
        for cp in first + passed:
            cp.wait_send()
        for m in mine:
            m.wait()


def _run_comm(comm, name, after=None):
    n = len(comm.arrays)
    extra = [] if after is None else [after]

    def body(*refs):
        xs, outs, sems = refs[:n], refs[n + len(extra):2 * n + len(extra)], refs[2 * n + len(extra):]
        comm.start(xs, outs, sems)
        comm.finish(xs, outs, sems)

    return pl.pallas_call(body, name=name, out_shape=comm.out_shape, in_specs=[_ANY] * (n + len(extra)),
                          out_specs=[_ANY] * n, scratch_shapes=comm.scratch)(*comm.arrays, *extra)


_HBM = pl.BlockSpec(memory_space=pltpu.HBM)
_SEMS = pl.BlockSpec(memory_space=pltpu.SEMAPHORE)
_EFFECT = pltpu.SideEffectType.DATAFLOW_SIDE_EFFECTING


def _own_slot(x, gathering, name):
    shape = (N_DEV,) + x.shape if gathering else x.shape
    r, c = shape[1], shape[2]
    tr = _tile(r, 256, 2 * SUBLANES)
    me = jnp.reshape(_my_index(), (1,)).astype(jnp.int32)

    def body(me_ref, x_ref, o_ref):
        o_ref[...] = x_ref[...]

    src = (pl.BlockSpec((tr, c), lambda i, m: (i, 0)) if gathering
           else pl.BlockSpec((None, tr, c), lambda i, m: (m[0], i, 0)))
    grid_spec = pltpu.PrefetchScalarGridSpec(
        num_scalar_prefetch=1, grid=(r // tr,), in_specs=[src],
        out_specs=pl.BlockSpec((None, tr, c), lambda i, m: (m[0], i, 0)))
    return pl.pallas_call(body, name=name, grid_spec=grid_spec, out_shape=jax.ShapeDtypeStruct(shape, x.dtype),
                          compiler_params=_params(("arbitrary",)))(me, x)


def _split_plan(x_ref, land_ref, send_sems, recv_sems, gathering):
    x, y, c = _my_pos()
    me = 4 * x + 2 * y + c
    sends, recvs = [], []
    for k in range(1, N_DEV):
        px = 1 - x if (k >> 2) & 1 else x
        py = 1 - y if (k >> 1) & 1 else y
        pc = 1 - c if k & 1 else c
        peer = 4 * px + 2 * py + pc
        mine, theirs = (x_ref, x_ref) if gathering else (x_ref.at[peer], x_ref.at[me])
        sends.append(pltpu.make_async_remote_copy(
            src_ref=mine, dst_ref=land_ref.at[me], send_sem=send_sems.at[k - 1], recv_sem=recv_sems.at[k - 1],
            device_id=(px, py, pc), device_id_type=MESH))
        recvs.append(pltpu.make_async_remote_copy(
            src_ref=theirs, dst_ref=land_ref.at[peer], send_sem=send_sems.at[k - 1], recv_sem=recv_sems.at[k - 1],
            device_id=(px, py, pc), device_id_type=MESH))
    return sends, recvs


def _split_start(x, gathering, name, prev):
    land = _own_slot(x, gathering, name + "_own")

    def body(x_ref, land_ref, prev_ref, send_sems, recv_sems, x_thru, land_thru, token):
        sends, _ = _split_plan(x_ref, land_ref, send_sems, recv_sems, gathering)
        for s in sends:
            s.start()
        token[...] = jnp.zeros_like(token)

    sems = pltpu.SemaphoreType.DMA((N_DEV - 1,))
    send_sems, recv_sems, x_thru, land_thru, token = pl.pallas_call(
        body, name=name,
        out_shape=(sems, sems, pltpu.HBM(x.shape, x.dtype), pltpu.HBM(land.shape, land.dtype),
                   jax.ShapeDtypeStruct((SUBLANES, LANES), F32)),
        in_specs=(_HBM, _HBM, _ANY), out_specs=(_SEMS, _SEMS, _HBM, _HBM, pl.BlockSpec(memory_space=pltpu.VMEM)),
        input_output_aliases={0: 2, 1: 3},
        compiler_params=pltpu.CompilerParams(has_side_effects=_EFFECT),
    )(pltpu.with_memory_space_constraint(x, pltpu.HBM), pltpu.with_memory_space_constraint(land, pltpu.HBM), prev)
    return (send_sems, recv_sems, x_thru, land_thru, gathering), token


def _after(a, token):
    return a + token[0, 0].astype(a.dtype)


def _split_wait(handle, after, name):
    send_sems, recv_sems, x_thru, land_thru, gathering = handle

    def body(x_ref, land_ref, send_sems, recv_sems, after_ref, x_dead, got_ref):
        sends, recvs = _split_plan(x_ref, land_ref, send_sems, recv_sems, gathering)
        for s in sends:
            s.wait_send()
        for r in recvs:
            r.wait_recv()

    return pl.pallas_call(
        body, name=name, out_shape=(pltpu.HBM(x_thru.shape, x_thru.dtype), pltpu.HBM(land_thru.shape, land_thru.dtype)),
        in_specs=(_HBM, _HBM, _SEMS, _SEMS, _ANY), out_specs=(_HBM, _HBM), input_output_aliases={0: 0, 1: 1},
        compiler_params=pltpu.CompilerParams(has_side_effects=_EFFECT),
    )(x_thru, land_thru, send_sems, recv_sems, after)[1]


def _call(body, *, name, grid, in_specs, out_specs, out_shape, args, scratch=(), sem=None, after=None):
    if after is None:
        return list(pl.pallas_call(body, name=name, grid=grid, in_specs=list(in_specs), out_specs=list(out_specs),
                                   out_shape=list(out_shape), scratch_shapes=list(scratch),
                                   compiler_params=_params(sem))(*args))
    n_in = len(in_specs)

    def wrapped(*refs):
        body(*refs[:n_in], *refs[n_in + 1:])

    return list(pl.pallas_call(wrapped, name=name, grid=grid, in_specs=list(in_specs) + [_ANY], out_specs=list(out_specs),
                               out_shape=list(out_shape), scratch_shapes=list(scratch),
                               compiler_params=_params(sem))(*args, after))


MM_B_BLOCK_BYTES = 6 * 1024 * 1024
MM_O_BLOCK_BYTES = 13 * 1024 * 1024 // 2


def _mm(a, b, name, tb=False, out_dtype=F32, b3=False, o_cs=None, tm_max=1088, after=None):
    m, k = a.shape
    if b3:
        cs = b.shape[2]
        n, kb = (b.shape[1], N_DEV * cs) if tb else (N_DEV * cs, b.shape[1])
    else:
        n, kb = (b.shape[0], b.shape[1]) if tb else (b.shape[1], b.shape[0])
    assert k == kb, (a.shape, b.shape, tb)
    if b3 and tb:
        tm, tn = _tile(m, 544, 2 * SUBLANES), _tile(n, 256, LANES)

        def body_shards(a_ref, b_ref, o_ref):
            r = None
            for j in range(N_DEV):
                part = lax.dot_general(a_ref[:, j * cs:(j + 1) * cs], b_ref[j], (((1,), (1,)), ((), ())),
                                       preferred_element_type=F32)
                r = part if r is None else r + part
            o_ref[...] = r.astype(o_ref.dtype)

        return _call(
            body_shards, name=name, grid=(m // tm, n // tn),
            in_specs=[pl.BlockSpec((tm, k), lambda i, j: (i, 0)), pl.BlockSpec((N_DEV, tn, cs), lambda i, j: (0, j, 0))],
            out_specs=[pl.BlockSpec((tm, tn), lambda i, j: (i, j))], out_shape=[jax.ShapeDtypeStruct((m, n), out_dtype)],
            sem=("parallel", "parallel"), args=(a, b), after=after)[0]
    tm = _tile(m, tm_max, 2 * SUBLANES)
    tk = k
    if b3:
        tn = cs
    elif o_cs is not None:
        tn = o_cs if o_cs % LANES == 0 else 2 * o_cs
    else:
        tn = _tile(n, min(MM_B_BLOCK_BYTES // (2 * tk), MM_O_BLOCK_BYTES // (4 * tm)), LANES)
    cb = 1 if tb else 0
    dn = (((1,), (cb,)), ((), ()))

    def body_one(a_ref, b_ref, o_ref):
        r = lax.dot_general(a_ref[...], b_ref[...], dn, preferred_element_type=F32)
        if o_cs is None:
            o_ref[...] = r.astype(o_ref.dtype)
        else:
            for j in range(tn // o_cs):
                o_ref[j] = r[:, j * o_cs:(j + 1) * o_cs].astype(o_ref.dtype)

    a_spec = pl.BlockSpec((tm, tk), lambda i, j: (i, 0))
    if b3:
        b_spec = pl.BlockSpec((None, tk, cs), lambda i, j: (j, 0, 0))
    else:
        b_spec = pl.BlockSpec((tn, tk), lambda i, j: (j, 0)) if tb else pl.BlockSpec((tk, tn), lambda i, j: (0, j))
    if o_cs is None:
        o_spec = pl.BlockSpec((tm, tn), lambda i, j: (i, j))
        o_shape = jax.ShapeDtypeStruct((m, n), out_dtype)
    else:
        o_spec = pl.BlockSpec((tn // o_cs, tm, o_cs), lambda i, j: (j, i, 0))
        o_shape = jax.ShapeDtypeStruct((n // o_cs, m, o_cs), out_dtype)
    return _call(
        body_one, name=name, grid=(m // tm, n // tn), in_specs=[a_spec, b_spec], out_specs=[o_spec], out_shape=[o_shape],
        sem=("parallel", "parallel"), args=(a, b), after=after)[0]


DW_TM = 512


def _transpose_bf16(x, name):
    t, c = x.shape
    tt = _tm()

    def body(x_ref, o_ref):
        o_ref[...] = x_ref[...].T

    return pl.pallas_call(body, name=name, grid=(t // tt,), in_specs=[pl.BlockSpec((tt, c), lambda i: (i, 0))],
                          out_specs=pl.BlockSpec((c, tt), lambda i: (0, i)),
                          out_shape=jax.ShapeDtypeStruct((c, t), BF16), compiler_params=_params(("parallel",)))(x)


def _cast_bf16(x, name):
    r, c = x.shape
    tr = _tile(r, 512, 2 * SUBLANES)

    def body(x_ref, o_ref):
        o_ref[...] = x_ref[...].astype(BF16)

    return pl.pallas_call(body, name=name, grid=(r // tr,), in_specs=[pl.BlockSpec((tr, c), lambda i: (i, 0))],
                          out_specs=pl.BlockSpec((tr, c), lambda i: (i, 0)),
                          out_shape=jax.ShapeDtypeStruct((r, c), BF16), compiler_params=_params(("parallel",)))(x)


def _cols_from_shards(wg, name):
    _, k, cs = wg.shape
    tk = _tile(k, 256, 2 * SUBLANES)

    def body(w_ref, o_ref):
        for j in range(N_DEV):
            o_ref[:, j * cs:(j + 1) * cs] = w_ref[j]

    return pl.pallas_call(body, name=name, grid=(k // tk,),
                          in_specs=[pl.BlockSpec((N_DEV, tk, cs), lambda i: (0, i, 0))],
                          out_specs=pl.BlockSpec((tk, N_DEV * cs), lambda i: (i, 0)),
                          out_shape=jax.ShapeDtypeStruct((k, N_DEV * cs), wg.dtype),
                          compiler_params=_params(("parallel",)))(wg)


def _stream(i):
    return jnp.minimum(i, 1)


def _normmod(x, g, sh, sc):
    y = x * lax.rsqrt(jnp.mean(x * x, axis=-1, keepdims=True) + EPS)
    return (y * g) * (1.0 + sc) + sh


def _mod_spec(chunk, d):
    return pl.BlockSpec((None, None, 1, d), lambda i: (_stream(i), chunk, 0, 0))


def _normmod_fwd(x, g, mod4, which, name):
    t, d = x.shape
    tm = _tm()
    ish, isc = (0, 1) if which == 0 else (3, 4)

    def body(x_ref, g_ref, sh_ref, sc_ref, o_ref, ot_ref):
        h = _normmod(x_ref[...], g_ref[...], sh_ref[...], sc_ref[...]).astype(BF16)
        o_ref[...] = h
        ot_ref[...] = h.T

    row = pl.BlockSpec((tm, d), lambda i: (i, 0))
    return pl.pallas_call(body, name=name, grid=(t // tm,),
                          in_specs=[row, pl.BlockSpec((1, d), lambda i: (0, 0)), _mod_spec(ish, d), _mod_spec(isc, d)],
                          out_specs=[row, pl.BlockSpec((d, tm), lambda i: (0, i))],
                          out_shape=[jax.ShapeDtypeStruct((t, d), BF16), jax.ShapeDtypeStruct((d, t), BF16)],
                          compiler_params=_params(("parallel",)))(x, g, mod4, mod4)


def _normmod_bwd(x, g, mod4, which, dh, dres, name):
    t, d = x.shape
    tm = _tm()
    ish, isc = (0, 1) if which == 0 else (3, 4)

    def body(x_ref, g_ref, sh_ref, sc_ref, dh_ref, dres_ref, dx_ref, dg_ref, dsh_ref, dsc_ref):
        i = pl.program_id(0)
        _, vjp = jax.vjp(_normmod, x_ref[...], g_ref[...], sh_ref[...], sc_ref[...])
        dx, dg, dsh, dsc = vjp(dh_ref[...])
        dx_ref[...] = dres_ref[...] + dx

        @pl.when(i == 0)
        def _():
            dg_ref[...] = jnp.zeros_like(dg_ref)

        @pl.when(i <= 1)
        def _():
            dsh_ref[...] = jnp.zeros_like(dsh_ref)
            dsc_ref[...] = jnp.zeros_like(dsc_ref)

        dg_ref[...] += dg
        dsh_ref[...] += dsh
        dsc_ref[...] += dsc

    row = pl.BlockSpec((tm, d), lambda i: (i, 0))
    vec = pl.BlockSpec((1, d), lambda i: (0, 0))
    svec = pl.BlockSpec((None, 1, d), lambda i: (_stream(i), 0, 0))
    return _call(
        body, name=name, grid=(t // tm,),
        in_specs=[row, vec, _mod_spec(ish, d), _mod_spec(isc, d), row, row],
        out_specs=[row, vec, svec, svec],
        out_shape=[jax.ShapeDtypeStruct((t, d), F32), jax.ShapeDtypeStruct((1, d), F32),
                   jax.ShapeDtypeStruct((2, 1, d), F32), jax.ShapeDtypeStruct((2, 1, d), F32)],
        sem=("arbitrary",), args=(x, g, mod4, mod4, dh, dres))


def _gate_res_fwd(x, f, mod4, chunk, name):
    t, d = x.shape
    tm = _tm()

    def body(x_ref, f_ref, g_ref, o_ref):
        o_ref[...] = x_ref[...] + g_ref[...] * f_ref[...]

    row = pl.BlockSpec((tm, d), lambda i: (i, 0))
    return pl.pallas_call(body, name=name, grid=(t // tm,), in_specs=[row, row, _mod_spec(chunk, d)], out_specs=row,
                          out_shape=jax.ShapeDtypeStruct((t, d), F32), compiler_params=_params(("parallel",)))(x, f, mod4)


def _gate_res_bwd(dx, f, mod4, chunk, name):
    t, d = dx.shape
    tm = _tm()

    def body(dx_ref, f_ref, g_ref, o_ref, dg_ref):
        i = pl.program_id(0)
        dxv = dx_ref[...]
        o_ref[...] = (dxv * g_ref[...]).astype(BF16)

        @pl.when(i <= 1)
        def _():
            dg_ref[...] = jnp.zeros_like(dg_ref)

        dg_ref[...] += jnp.sum(dxv * f_ref[...], axis=0, keepdims=True)

    row = pl.BlockSpec((tm, d), lambda i: (i, 0))
    return pl.pallas_call(
        body, name=name, grid=(t // tm,), in_specs=[row, row, _mod_spec(chunk, d)],
        out_specs=[row, pl.BlockSpec((None, 1, d), lambda i: (_stream(i), 0, 0))],
        out_shape=[jax.ShapeDtypeStruct((t, d), BF16), jax.ShapeDtypeStruct((2, 1, d), F32)],
        compiler_params=_params(("arbitrary",)))(dx, f, mod4)


def _loss_head(x, final_g, target, name):
    t, d = x.shape
    tm = _tm()

    def loss_fn(xv, g, tgt):
        y = (xv * lax.rsqrt(jnp.mean(xv * xv, axis=-1, keepdims=True) + EPS)) * g
        err = y - tgt
        return 0.5 * jnp.sum(jnp.mean(err * err, axis=-1, keepdims=True))

    def body(x_ref, g_ref, t_ref, l_ref, dx_ref, dg_ref):
        i = pl.program_id(0)

        @pl.when(i == 0)
        def _():
            l_ref[...] = jnp.zeros_like(l_ref)
            dg_ref[...] = jnp.zeros_like(dg_ref)
            dx_ref[...] = jnp.zeros_like(dx_ref)

        @pl.when(i > 0)
        def _():
            l, (dx, dg) = jax.value_and_grad(loss_fn, argnums=(0, 1))(x_ref[...], g_ref[...], t_ref[...])
            l_ref[...] += jnp.full(l_ref.shape, l, F32)
            dx_ref[...] = dx
            dg_ref[...] += dg

    row = pl.BlockSpec((tm, d), lambda i: (i, 0))
    vec = pl.BlockSpec((1, d), lambda i: (0, 0))
    return pl.pallas_call(
        body, name=name, grid=(t // tm,),
        in_specs=[row, vec, pl.BlockSpec((tm, d), lambda i: (jnp.maximum(i - 1, 0), 0))],
        out_specs=[pl.BlockSpec((SUBLANES, LANES), lambda i: (0, 0)), row, vec],
        out_shape=[jax.ShapeDtypeStruct((SUBLANES, LANES), F32), jax.ShapeDtypeStruct((t, d), F32),
                   jax.ShapeDtypeStruct((1, d), F32)],
        compiler_params=_params(("arbitrary",)))(x, final_g, target)


def _swap_quarters(x):
    half, nf = RET_DK // 2, RET_DK // 4
    lane = lax.broadcasted_iota(jnp.int32, x.shape, 1)
    return jnp.where((lane % half) < nf, pltpu.roll(x, RET_DK - nf, 1), pltpu.roll(x, nf, 1))


def _rope(x, cos, sin):
    return x * cos + _swap_quarters(x) * sin


def _rope_t(y, cos, sin):
    return y * cos + _swap_quarters(y * sin)


def _ret_consts(d):
    c = RET_CHUNK
    ii = lax.broadcasted_iota(jnp.int32, (c, 1), 0).astype(F32)
    jj = lax.broadcasted_iota(jnp.int32, (1, c), 1).astype(F32)
    fwd = d == 0
    sgn = jnp.where(fwd, 1.0, -1.0).astype(F32)
    pos = jnp.where(fwd, ii, c - 1.0 - ii)
    return sgn * (ii - jj), pos


def _ret_step(lgt, state, q, k, v, diff, pos):
    c = float(RET_CHUNK)
    lg = -(jnp.maximum(-lgt, 0.0) + jnp.log1p(jnp.exp(-jnp.abs(lgt))))
    lower = diff >= 0
    decay = jnp.where(lower, jnp.exp(jnp.where(lower, diff, 0.0) * lg), 0.0)
    xi = jnp.exp((pos + 1.0) * lg)
    zeta = jnp.exp((c - 1.0 - pos) * lg)
    gch = jnp.exp(c * lg)
    inner = dot_nt(q, k) * decay
    out = dot_nn(inner, v) + dot_nn(q, state) * xi
    new_state = state * gch + dot_tn(k * zeta, v)
    return out, new_state


def _chunk_order():
    nc, nch = CTX_LEN // RET_CHUNK, _t_rows() // RET_CHUNK
    fwd = list(range(nch))
    bwd = list(range(nc - 1, -1, -1)) + list(range(nch - 1, nc - 1, -1))
    return jnp.asarray(np.array([fwd, bwd], np.int32))


def _ret_fwd(p, cos, sin, decay, order, name):
    t = p.shape[0]
    c, dk, dv, nh = RET_CHUNK, RET_DK, RET_DV, RET_HEADS
    nch = t // c
    off = _offsets()
    wqk, wv = nh * dk, nh * dv
    assert off["q"] % wqk == 0 and off["k"] % wqk == 0 and off["v"] % wv == 0
    qb, kb, vb = off["q"] // wqk, off["k"] // wqk, off["v"] // wv
    scale = RET_DK ** -0.5

    def body(ord_ref, dec_ref, q_ref, k_ref, v_ref, cos_ref, sin_ref, o_ref, st_ref, state):
        d, s = pl.program_id(0), pl.program_id(1)

        @pl.when(s == 0)
        def _():
            state[...] = jnp.zeros_like(state)

        diff, pos = _ret_consts(d)
        cosv, sinv = cos_ref[...], sin_ref[...]
        for h in range(nh):
            st = state[h]
            st_ref[h] = st
            lgt = jnp.full((1, 1), dec_ref[d, h], F32)
            q = _rope(q_ref[:, h * dk:(h + 1) * dk], cosv, sinv) * scale
            k = _rope(k_ref[:, h * dk:(h + 1) * dk], cosv, sinv)
            out, ns = _ret_step(lgt, st, q, k, v_ref[:, h * dv:(h + 1) * dv], diff, pos)
            o_ref[:, h * dv:(h + 1) * dv] = out
            state[h] = ns

    grid_spec = pltpu.PrefetchScalarGridSpec(
        num_scalar_prefetch=1, grid=(2, nch),
        in_specs=[pl.BlockSpec(memory_space=pltpu.SMEM),
                  pl.BlockSpec((c, wqk), lambda d, s, o: (o[d, s], qb)),
                  pl.BlockSpec((c, wqk), lambda d, s, o: (o[d, s], kb)),
                  pl.BlockSpec((c, wv), lambda d, s, o: (o[d, s], vb)),
                  pl.BlockSpec((c, dk), lambda d, s, o: (o[d, s], 0)),
                  pl.BlockSpec((c, dk), lambda d, s, o: (o[d, s], 0))],
        out_specs=[pl.BlockSpec((None, c, wv), lambda d, s, o: (d, o[d, s], 0)),
                   pl.BlockSpec((None, nh, None, dk, dv), lambda d, s, o: (d, 0, s, 0, 0))],
        scratch_shapes=[pltpu.VMEM((nh, dk, dv), F32)])
    return pl.pallas_call(
        body, name=name, grid_spec=grid_spec,
        out_shape=[jax.ShapeDtypeStruct((2, t, wv), F32), jax.ShapeDtypeStruct((2, nh, nch, dk, dv), F32)],
        compiler_params=_params(("arbitrary", "arbitrary")))(order, decay, p, p, p, cos, sin)


def _ret_bwd(p, cos, sin, decay, order, states, do, name):
    t = p.shape[0]
    c, dk, dv, nh = RET_CHUNK, RET_DK, RET_DV, RET_HEADS
    nch = t // c
    off = _offsets()
    wqk, wv = nh * dk, nh * dv
    qb, kb, vb = off["q"] // wqk, off["k"] // wqk, off["v"] // wv
    scale = RET_DK ** -0.5

    def body(ord_ref, dec_ref, q_ref, k_ref, v_ref, cos_ref, sin_ref, st_ref, do_ref,
             dq_ref, dk_ref, dv_ref, dd_ref, dstate):
        d, s = pl.program_id(0), pl.program_id(1)

        @pl.when(s == 0)
        def _():
            dstate[...] = jnp.zeros_like(dstate)
            dd_ref[...] = jnp.zeros_like(dd_ref)

        diff, pos = _ret_consts(d)
        cosv, sinv = cos_ref[...], sin_ref[...]
        for h in range(nh):
            qk, vv = slice(h * dk, (h + 1) * dk), slice(h * dv, (h + 1) * dv)
            lgt = jnp.full((1, 1), dec_ref[d, h], F32)
            q = _rope(q_ref[:, qk], cosv, sinv) * scale
            k = _rope(k_ref[:, qk], cosv, sinv)
            _, vjp = jax.vjp(lambda a, b, cq, ck, cv: _ret_step(a, b, cq, ck, cv, diff, pos),
                             lgt, st_ref[h], q, k, v_ref[:, vv])
            dlgt, dst, dq, dkk, dvv = vjp((do_ref[:, vv], dstate[h]))
            dstate[h] = dst
            dq_ref[:, qk] = _rope_t(dq * scale, cosv, sinv)
            dk_ref[:, qk] = _rope_t(dkk, cosv, sinv)
            dv_ref[:, vv] = dvv
            dd_ref[h] += jnp.broadcast_to(dlgt, (SUBLANES, LANES))

    rev = lambda o, d, s: o[d, nch - 1 - s]
    grid_spec = pltpu.PrefetchScalarGridSpec(
        num_scalar_prefetch=1, grid=(2, nch),
        in_specs=[pl.BlockSpec(memory_space=pltpu.SMEM),
                  pl.BlockSpec((c, wqk), lambda d, s, o: (rev(o, d, s), qb)),
                  pl.BlockSpec((c, wqk), lambda d, s, o: (rev(o, d, s), kb)),
                  pl.BlockSpec((c, wv), lambda d, s, o: (rev(o, d, s), vb)),
                  pl.BlockSpec((c, dk), lambda d, s, o: (rev(o, d, s), 0)),
                  pl.BlockSpec((c, dk), lambda d, s, o: (rev(o, d, s), 0)),
                  pl.BlockSpec((None, nh, None, dk, dv), lambda d, s, o: (d, 0, nch - 1 - s, 0, 0)),
                  pl.BlockSpec((c, wv), lambda d, s, o: (rev(o, d, s), 0))],
        out_specs=[pl.BlockSpec((None, c, wqk), lambda d, s, o: (d, rev(o, d, s), 0)),
                   pl.BlockSpec((None, c, wqk), lambda d, s, o: (d, rev(o, d, s), 0)),
                   pl.BlockSpec((None, c, wv), lambda d, s, o: (d, rev(o, d, s), 0)),
                   pl.BlockSpec((None, nh, SUBLANES, LANES), lambda d, s, o: (d, 0, 0, 0))],
        scratch_shapes=[pltpu.VMEM((nh, dk, dv), F32)])
    return pl.pallas_call(
        body, name=name, grid_spec=grid_spec,
        out_shape=[jax.ShapeDtypeStruct((2, t, wqk), F32), jax.ShapeDtypeStruct((2, t, wqk), F32),
                   jax.ShapeDtypeStruct((2, t, wv), F32), jax.ShapeDtypeStruct((2, nh, SUBLANES, LANES), F32)],
        compiler_params=_params(("arbitrary", "arbitrary")))(order, decay, p, p, p, cos, sin, states, do)


def _ggn_head(of, ob, gate, g):
    o = of + ob
    mu = jnp.mean(o, axis=-1, keepdims=True)
    var = jnp.mean(jnp.square(o - mu), axis=-1, keepdims=True)
    return ((o - mu) * lax.rsqrt(var + EPS) * g) * _silu(gate)


def _ggn_fwd(o2, p, gn_g, name):
    t = p.shape[0]
    tm, w, dv = _tm(), _ret_w(), RET_DV
    gb = _offsets()["g"] // w

    def body(o_ref, gate_ref, g_ref, out_ref):
        for h in range(RET_HEADS):
            sl = slice(h * dv, (h + 1) * dv)
            out_ref[:, sl] = _ggn_head(o_ref[0, :, sl], o_ref[1, :, sl], gate_ref[:, sl], g_ref[:, sl]).astype(BF16)

    return pl.pallas_call(
        body, name=name, grid=(t // tm,),
        in_specs=[pl.BlockSpec((2, tm, w), lambda i: (0, i, 0)), pl.BlockSpec((tm, w), lambda i: (i, gb)),
                  pl.BlockSpec((1, w), lambda i: (0, 0))],
        out_specs=pl.BlockSpec((tm, w), lambda i: (i, 0)), out_shape=jax.ShapeDtypeStruct((t, w), BF16),
        compiler_params=_params(("parallel",)))(o2, p, gn_g)


def _ggn_bwd(o2, p, gn_g, dmix, name):
    t = p.shape[0]
    tm, w, dv = _tm(), _ret_w(), RET_DV
    gb = _offsets()["g"] // w

    def body(o_ref, gate_ref, g_ref, dy_ref, do_ref, dgate_ref, dg_ref):
        i = pl.program_id(0)

        @pl.when(i == 0)
        def _():
            dg_ref[...] = jnp.zeros_like(dg_ref)

        for h in range(RET_HEADS):
            sl = slice(h * dv, (h + 1) * dv)
            _, vjp = jax.vjp(_ggn_head, o_ref[0, :, sl], o_ref[1, :, sl], gate_ref[:, sl], g_ref[:, sl])
            do, _, dgate, dg = vjp(dy_ref[:, sl])
            do_ref[:, sl] = do
            dgate_ref[:, sl] = dgate
            dg_ref[:, sl] += dg

    row = pl.BlockSpec((tm, w), lambda i: (i, 0))
    return pl.pallas_call(
        body, name=name, grid=(t // tm,),
        in_specs=[pl.BlockSpec((2, tm, w), lambda i: (0, i, 0)), pl.BlockSpec((tm, w), lambda i: (i, gb)),
                  pl.BlockSpec((1, w), lambda i: (0, 0)), row],
        out_specs=[row, row, pl.BlockSpec((1, w), lambda i: (0, 0))],
        out_shape=[jax.ShapeDtypeStruct((t, w), F32), jax.ShapeDtypeStruct((t, w), F32),
                   jax.ShapeDtypeStruct((1, w), F32)],
        compiler_params=_params(("arbitrary",)))(o2, p, gn_g, dmix)


def _halo(k):
    return SUBLANES * ((k // 2 + SUBLANES - 1) // SUBLANES)


def _halo_specs(width, colblock, h, tm):
    r = tm // h
    return [pl.BlockSpec((h, width), lambda i, *_: (jnp.maximum(i * r - 1, 0), colblock(*_))),
            pl.BlockSpec((tm, width), lambda i, *_: (i, colblock(*_))),
            pl.BlockSpec((h, width), lambda i, *_: (jnp.minimum((i + 1) * r, (_t_rows() // h) - 1), colblock(*_)))]


def _fill_ext(ext_ref, prev, cur, nxt, i, h, tm):
    nt = _t_rows() // tm
    ext_ref[0:h, :] = jnp.where(i >= 2, prev, 0.0)
    ext_ref[h:h + tm, :] = cur
    ext_ref[h + tm:h + tm + h, :] = jnp.where((i >= 1) & (i <= nt - 2), nxt, 0.0)


def _corr(ext_ref, w_ref, k, h, tm, flip, cols=slice(None)):
    pad = k // 2
    acc = None
    for kk in range(k):
        o = h + (pad - kk if flip else kk - pad)
        term = w_ref[kk:kk + 1, cols] * ext_ref[o:o + tm, cols]
        acc = term if acc is None else acc + term
    return acc


LANE_CHUNK = 512


def _chunks(tm, tc):
    return [(r0, slice(c0, min(c0 + LANE_CHUNK, tc))) for r0 in range(0, tm, ROW_CHUNK) for c0 in range(0, tc, LANE_CHUNK)]


def _conv_post(u2, ln_g, ln_b, pw):
    mu = jnp.mean(u2, axis=-1, keepdims=True)
    var = jnp.mean(jnp.square(u2 - mu), axis=-1, keepdims=True)
    y = (u2 - mu) * lax.rsqrt(var + EPS) * ln_g + ln_b
    return dot_nn(_silu(y), pw)


def _conv_fwd(p, dw_w, dw_b, ln_g, ln_b, pw, name):
    t = p.shape[0]
    tm, w, k = _tm(), CONV_W, CONV_K
    h = _halo(k)
    off = _offsets()
    ab, bb = off["a"] // w, off["b"] // w

    def body(ap, ac, an, bp, bc, bn, w_ref, b_ref, g_ref, beta_ref, pw_ref, u2_ref, out_ref, ext):
        i = pl.program_id(0)
        glu = lambda a, b: a * _sigmoid(b)
        _fill_ext(ext, glu(ap[...], bp[...]), glu(ac[...], bc[...]), glu(an[...], bn[...]), i, h, tm)
        for r0 in range(0, tm, ROW_CHUNK):
            u2_ref[r0:r0 + ROW_CHUNK, :] = _corr(ext, w_ref, k, h + r0, ROW_CHUNK, False) + b_ref[...]
        out_ref[...] = _conv_post(u2_ref[...], g_ref[...], beta_ref[...], pw_ref[...]).astype(BF16)

    vec = pl.BlockSpec((1, w), lambda i: (0, 0))
    row = pl.BlockSpec((tm, w), lambda i: (i, 0))
    return pl.pallas_call(
        body, name=name, grid=(t // tm,),
        in_specs=_halo_specs(w, lambda: ab, h, tm) + _halo_specs(w, lambda: bb, h, tm)
        + [pl.BlockSpec((k, w), lambda i: (0, 0)), vec, vec, vec, pl.BlockSpec((w, w), lambda i: (0, 0))],
        out_specs=[row, row],
        out_shape=[jax.ShapeDtypeStruct((t, w), F32), jax.ShapeDtypeStruct((t, w), BF16)],
        scratch_shapes=[pltpu.VMEM((tm + 2 * h, w), F32)],
        compiler_params=_params(("parallel",)))(p, p, p, p, p, p, dw_w, dw_b, ln_g, ln_b, pw)


def _conv_bwd1(u2, dmix, ln_g, ln_b, pw, name):
    t = u2.shape[0]
    tm, w = _tm(), CONV_W
    cb = _ret_w() // w

    def body(u2_ref, dy_ref, g_ref, beta_ref, pw_ref, du2_ref, dg_ref, db_ref, dpw_ref):
        i = pl.program_id(0)

        @pl.when(i == 0)
        def _():
            dg_ref[...] = jnp.zeros_like(dg_ref)
            db_ref[...] = jnp.zeros_like(db_ref)
            dpw_ref[...] = jnp.zeros_like(dpw_ref)

        _, vjp = jax.vjp(_conv_post, u2_ref[...], g_ref[...], beta_ref[...], pw_ref[...])
        du2, dg, db, dpw = vjp(dy_ref[...])
        du2_ref[...] = du2
        dg_ref[...] += dg
        db_ref[...] += db
        dpw_ref[...] += dpw

    vec = pl.BlockSpec((1, w), lambda i: (0, 0))
    row = pl.BlockSpec((tm, w), lambda i: (i, 0))
    mat = pl.BlockSpec((w, w), lambda i: (0, 0))
    return pl.pallas_call(
        body, name=name, grid=(t // tm,),
        in_specs=[row, pl.BlockSpec((tm, w), lambda i: (i, cb)), vec, vec, mat],
        out_specs=[row, vec, vec, mat],
        out_shape=[jax.ShapeDtypeStruct((t, w), F32), jax.ShapeDtypeStruct((1, w), F32),
                   jax.ShapeDtypeStruct((1, w), F32), jax.ShapeDtypeStruct((w, w), F32)],
        compiler_params=_params(("arbitrary",)))(u2, dmix, ln_g, ln_b, pw)


def _conv_bwd2(du2, p, dw_w, name):
    t = p.shape[0]
    tm, w, k = _tm(), CONV_W, CONV_K
    h = _halo(k)
    pad = k // 2
    off = _offsets()
    ab, bb = off["a"] // w, off["b"] // w

    def body(dp, dc, dn, ap, ac, an, bp, bc, bn, w_ref, da_ref, db_ref, dw_ref, dbias_ref, ext_d, ext_u):
        i = pl.program_id(0)

        @pl.when(i == 0)
        def _():
            dw_ref[...] = jnp.zeros_like(dw_ref)
            dbias_ref[...] = jnp.zeros_like(dbias_ref)

        glu = lambda a, b: a * _sigmoid(b)
        _fill_ext(ext_d, dp[...], dc[...], dn[...], i, h, tm)
        _fill_ext(ext_u, glu(ap[...], bp[...]), glu(ac[...], bc[...]), glu(an[...], bn[...]), i, h, tm)
        chunks = range(0, tm, ROW_CHUNK)
        acc_b = jnp.zeros((ROW_CHUNK, w), F32)
        for r0 in chunks:
            rows = slice(r0, r0 + ROW_CHUNK)
            du = _corr(ext_d, w_ref, k, h + r0, ROW_CHUNK, True)
            sg = _sigmoid(bc[rows, :])
            da_ref[rows, :] = du * sg
            db_ref[rows, :] = du * ac[rows, :] * sg * (1.0 - sg)
            acc_b = acc_b + ext_d[h + r0:h + r0 + ROW_CHUNK, :]
        dbias_ref[...] += jnp.sum(acc_b, axis=0, keepdims=True)
        for kk in range(k):
            acc = jnp.zeros((ROW_CHUNK, w), F32)
            for r0 in chunks:
                o = h + r0 + kk - pad
                acc = acc + ext_d[h + r0:h + r0 + ROW_CHUNK, :] * ext_u[o:o + ROW_CHUNK, :]
            dw_ref[kk:kk + 1, :] += jnp.sum(acc, axis=0, keepdims=True)

    vec = pl.BlockSpec((1, w), lambda i: (0, 0))
    row = pl.BlockSpec((tm, w), lambda i: (i, 0))
    kw = pl.BlockSpec((k, w), lambda i: (0, 0))
    return _call(
        body, name=name, grid=(t // tm,),
        in_specs=_halo_specs(w, lambda: 0, h, tm) + _halo_specs(w, lambda: ab, h, tm)
        + _halo_specs(w, lambda: bb, h, tm) + [kw],
        out_specs=[row, row, kw, vec],
        out_shape=[jax.ShapeDtypeStruct((t, w), F32), jax.ShapeDtypeStruct((t, w), F32),
                   jax.ShapeDtypeStruct((k, w), F32), jax.ShapeDtypeStruct((1, w), F32)],
        scratch=[pltpu.VMEM((tm + 2 * h, w), F32), pltpu.VMEM((tm + 2 * h, w), F32)],
        sem=("arbitrary",), args=(du2, du2, du2, p, p, p, p, p, p, dw_w))


def _ffn_tc():
    return _tile(D_FF, 1408, LANES)


def _ffn_act_fwd(u, dw_w, dw_b, name):
    t = u.shape[0]
    tm, k, tc = _tm(), FFN_K, _ffn_tc()
    h = _halo(k)
    nj = D_FF // tc

    def body(vp, vc, vn, gp, gc, gn, wv, wg, bv, bg, out_ref, out_t_ref, ext_v, ext_g):
        i = pl.program_id(0)
        _fill_ext(ext_v, vp[...], vc[...], vn[...], i, h, tm)
        _fill_ext(ext_g, gp[...], gc[...], gn[...], i, h, tm)
        for r0, cols in _chunks(tm, tc):
            val = _corr(ext_v, wv, k, h + r0, ROW_CHUNK, False, cols) + bv[:, cols]
            gate = _corr(ext_g, wg, k, h + r0, ROW_CHUNK, False, cols) + bg[:, cols]
            out_ref[r0:r0 + ROW_CHUNK, cols] = (_silu(gate) * val).astype(BF16)
        out_t_ref[...] = out_ref[...].T

    wspec = lambda s: pl.BlockSpec((k, tc), lambda i, j: (0, j + s))
    bspec = lambda s: pl.BlockSpec((1, tc), lambda i, j: (0, j + s))
    return pl.pallas_call(
        body, name=name, grid=(t // tm, nj),
        in_specs=_halo_specs(tc, lambda j: j, h, tm) + _halo_specs(tc, lambda j: j + nj, h, tm)
        + [wspec(0), wspec(nj), bspec(0), bspec(nj)],
        out_specs=[pl.BlockSpec((tm, tc), lambda i, j: (i, j)), pl.BlockSpec((tc, tm), lambda i, j: (j, i))],
        out_shape=[jax.ShapeDtypeStruct((t, D_FF), BF16), jax.ShapeDtypeStruct((D_FF, t), BF16)],
        scratch_shapes=[pltpu.VMEM((tm + 2 * h, tc), F32), pltpu.VMEM((tm + 2 * h, tc), F32)],
        compiler_params=_params(("parallel", "parallel")))(u, u, u, u, u, u, dw_w, dw_w, dw_b, dw_b)


def _ffn_act_bwd1(u, da, dw_w, dw_b, name):
    t = u.shape[0]
    tm, k, tc = _tm(), FFN_K, _ffn_tc()
    h = _halo(k)
    nj = D_FF // tc

    def body(vp, vc, vn, gp, gc, gn, wv, wg, bv, bg, da_ref, dv_ref, dg_ref, ext_v, ext_g):
        i = pl.program_id(0)
        _fill_ext(ext_v, vp[...], vc[...], vn[...], i, h, tm)
        _fill_ext(ext_g, gp[...], gc[...], gn[...], i, h, tm)
        for r0, cols in _chunks(tm, tc):
            rows = slice(r0, r0 + ROW_CHUNK)
            val = _corr(ext_v, wv, k, h + r0, ROW_CHUNK, False, cols) + bv[:, cols]
            gate = _corr(ext_g, wg, k, h + r0, ROW_CHUNK, False, cols) + bg[:, cols]
            _, vjp = jax.vjp(lambda a, b: _silu(b) * a, val, gate)
            dval, dgate = vjp(da_ref[rows, cols])
            dv_ref[rows, cols] = dval
            dg_ref[rows, cols] = dgate

    wspec = lambda s: pl.BlockSpec((k, tc), lambda i, j: (0, j + s))
    bspec = lambda s: pl.BlockSpec((1, tc), lambda i, j: (0, j + s))
    dc = pl.pallas_call(
        body, name=name, grid=(t // tm, nj),
        in_specs=_halo_specs(tc, lambda j: j, h, tm) + _halo_specs(tc, lambda j: j + nj, h, tm)
        + [wspec(0), wspec(nj), bspec(0), bspec(nj), pl.BlockSpec((tm, tc), lambda i, j: (i, j))],
        out_specs=[pl.BlockSpec((tm, tc), lambda i, j: (i, j)), pl.BlockSpec((tm, tc), lambda i, j: (i, j))],
        out_shape=[jax.ShapeDtypeStruct((t, D_FF), F32), jax.ShapeDtypeStruct((t, D_FF), F32)],
        scratch_shapes=[pltpu.VMEM((tm + 2 * h, tc), F32), pltpu.VMEM((tm + 2 * h, tc), F32)],
        compiler_params=_params(("parallel", "parallel")))(u, u, u, u, u, u, dw_w, dw_w, dw_b, dw_b, da)
    return dc


def _dwconv_bwd(dcv, dcg, u, dw_w, name):
    t = u.shape[0]
    tm, k, tc = _tm(), FFN_K, _ffn_tc()
    h = _halo(k)
    pad = k // 2
    nj = D_FF // tc

    def body(vp, vc, vn, gp, gc, gn, up, uc, un, w_ref, du_ref, dw_ref, dbias_ref, ext_d, ext_u):
        jj, i = pl.program_id(0), pl.program_id(1)

        @pl.when(i == 0)
        def _():
            dw_ref[...] = jnp.zeros_like(dw_ref)
            dbias_ref[...] = jnp.zeros_like(dbias_ref)

        @pl.when(jj < nj)
        def _():
            _fill_ext(ext_d, vp[...], vc[...], vn[...], i, h, tm)

        @pl.when(jj >= nj)
        def _():
            _fill_ext(ext_d, gp[...], gc[...], gn[...], i, h, tm)

        _fill_ext(ext_u, up[...], uc[...], un[...], i, h, tm)
        for c0 in range(0, tc, LANE_CHUNK):
            cols = slice(c0, min(c0 + LANE_CHUNK, tc))
            width = cols.stop - cols.start
            acc_b = jnp.zeros((ROW_CHUNK, width), F32)
            acc_w = [jnp.zeros((ROW_CHUNK, width), F32) for _ in range(k)]
            for r0 in range(0, tm, ROW_CHUNK):
                d = ext_d[h + r0:h + r0 + ROW_CHUNK, cols]
                du_ref[r0:r0 + ROW_CHUNK, cols] = _corr(ext_d, w_ref, k, h + r0, ROW_CHUNK, True, cols).astype(BF16)
                acc_b = acc_b + d
                for kk in range(k):
                    o = h + r0 + kk - pad
                    acc_w[kk] = acc_w[kk] + d * ext_u[o:o + ROW_CHUNK, cols]
            dbias_ref[:, cols] += jnp.sum(acc_b, axis=0, keepdims=True)
            for kk in range(k):
                dw_ref[kk:kk + 1, cols] += jnp.sum(acc_w[kk], axis=0, keepdims=True)

    def hs(cb, live):
        r = tm // h
        row = lambda j, i: jnp.where(live(j), i, 0)
        return [pl.BlockSpec((h, tc), lambda j, i: (jnp.maximum(row(j, i) * r - 1, 0), cb(j))),
                pl.BlockSpec((tm, tc), lambda j, i: (row(j, i), cb(j))),
                pl.BlockSpec((h, tc), lambda j, i: (jnp.minimum((row(j, i) + 1) * r, (_t_rows() // h) - 1), cb(j)))]

    return pl.pallas_call(
        body, name=name, grid=(2 * nj, t // tm),
        in_specs=hs(lambda j: jnp.minimum(j, nj - 1), lambda j: j < nj)
        + hs(lambda j: jnp.maximum(j - nj, 0), lambda j: j >= nj)
        + hs(lambda j: j, lambda j: True) + [pl.BlockSpec((k, tc), lambda j, i: (0, j))],
        out_specs=[pl.BlockSpec((tm, tc), lambda j, i: (i, j)), pl.BlockSpec((k, tc), lambda j, i: (0, j)),
                   pl.BlockSpec((1, tc), lambda j, i: (0, j))],
        out_shape=[jax.ShapeDtypeStruct((t, 2 * D_FF), BF16), jax.ShapeDtypeStruct((k, 2 * D_FF), F32),
                   jax.ShapeDtypeStruct((1, 2 * D_FF), F32)],
        scratch_shapes=[pltpu.VMEM((tm + 2 * h, tc), F32), pltpu.VMEM((tm + 2 * h, tc), F32)],
        compiler_params=_params(("parallel", "arbitrary")))(dcv, dcv, dcv, dcg, dcg, dcg, u, u, u, dw_w)


def _na_geometry(rq):
    ncb = CTX_LEN // GRID_W
    rows_n = SEQ // GRID_W
    r = jnp.maximum(rq - ncb, 0)
    kstart = jnp.clip(r - NA_ROWS // 2, 0, rows_n - NA_ROWS)
    base = kstart - r + NA_ROWS - 1
    return rq >= ncb, kstart, base


def _na_core(q, kl, vl, kc, vc, bias, mask):
    qs = q * (NA_DH ** -0.5)
    s_l = jnp.where(mask, dot_nt(qs, kl) + bias, NEG)
    s_c = dot_nt(qs, kc)
    m = lax.stop_gradient(jnp.maximum(jnp.max(s_l, axis=1, keepdims=True), jnp.max(s_c, axis=1, keepdims=True)))
    e_l, e_c = jnp.exp(s_l - m), jnp.exp(s_c - m)
    inv = 1.0 / (jnp.sum(e_l, axis=1, keepdims=True) + jnp.sum(e_c, axis=1, keepdims=True))
    return dot_nn(e_l * inv, vl) + dot_nn(e_c * inv, vc)


def _na_mask(is_lat):
    nl = NA_ROWS * GRID_W
    q = lax.broadcasted_iota(jnp.int32, (GRID_W, nl), 0)
    w = lax.broadcasted_iota(jnp.int32, (GRID_W, nl), 1) % GRID_W
    cs = jnp.clip(q - NA_COLS // 2, 0, GRID_W - NA_COLS)
    return (w >= cs) & (w < cs + NA_COLS) & is_lat


def _na_bias(rb_ref):
    assert 2 * GRID_W == LANES
    lane = lax.broadcasted_iota(jnp.int32, (GRID_W, LANES), 1)
    tiles = []
    for kp in range(NA_ROWS // 2):
        ev = jnp.broadcast_to(rb_ref[2 * kp:2 * kp + 1, :], (GRID_W, LANES))
        od = jnp.broadcast_to(rb_ref[2 * kp + 1:2 * kp + 2, :], (GRID_W, LANES))
        ev = pltpu.roll(ev, LANES - (NA_COLS - 1), 1, stride=1, stride_axis=0)
        od = pltpu.roll(od, LANES - (NA_COLS - 1) - GRID_W, 1, stride=1, stride_axis=0)
        tiles.append(jnp.where(lane < GRID_W, ev, od))
    return jnp.concatenate(tiles, axis=1)


def _na_dbias(dbias, drb_ref):
    qi = lax.broadcasted_iota(jnp.int32, (GRID_W, GRID_W), 0)
    qj = lax.broadcasted_iota(jnp.int32, (GRID_W, GRID_W), 1)
    flip = (qi + qj == GRID_W - 1).astype(F32)
    rev = lax.dot_general(flip, dbias, (((1,), (0,)), ((), ())), precision=lax.Precision.HIGHEST,
                          preferred_element_type=F32)
    lane = lax.broadcasted_iota(jnp.int32, (GRID_W, LANES), 1)
    s_ev = LANES - (GRID_W - NA_COLS)
    for kp in range(NA_ROWS // 2):
        tile = rev[:, kp * LANES:(kp + 1) * LANES]
        ev = pltpu.roll(jnp.where(lane < GRID_W, tile, 0.0), s_ev, 1, stride=1, stride_axis=0)
        od = pltpu.roll(jnp.where(lane >= GRID_W, tile, 0.0), s_ev - GRID_W, 1, stride=1, stride_axis=0)
        drb_ref[2 * kp:2 * kp + 1, :] += jnp.sum(ev, axis=0, keepdims=True)
        drb_ref[2 * kp + 1:2 * kp + 2, :] += jnp.sum(od, axis=0, keepdims=True)


def _na_hps():
    return 2 if NA_HEADS % 2 == 0 else 1


def _na_specs(p_offsets):
    t = _t_rows()
    hps = _na_hps()
    wd = hps * NA_DH
    assert all(p_offsets[n] % wd == 0 for n in ("nq", "nk", "nv"))
    qb, kb, vb = (p_offsets[n] // wd for n in ("nq", "nk", "nv"))
    return [pl.BlockSpec((GRID_W, wd), lambda h, r: (r, qb + h)),
            pl.BlockSpec((t, wd), lambda h, r: (0, kb + h)),
            pl.BlockSpec((t, wd), lambda h, r: (0, vb + h)),
            pl.BlockSpec((hps, None, NA_ROWS, LANES), lambda h, r: (h, _na_geometry(r)[2], 0, 0))]


def _na_fwd(p, rb, name):
    t = p.shape[0]
    dh, nl = NA_DH, NA_ROWS * GRID_W

    hps = _na_hps()

    def body(q_ref, k_ref, v_ref, rb_ref, out_ref):
        rq = pl.program_id(1)
        is_lat, kstart, _ = _na_geometry(rq)
        start = pl.multiple_of(CTX_LEN + kstart * GRID_W, GRID_W)
        mask = _na_mask(is_lat)
        for hh in range(hps):
            cols = slice(hh * dh, (hh + 1) * dh)
            out = _na_core(q_ref[:, cols], k_ref[pl.ds(start, nl), cols], v_ref[pl.ds(start, nl), cols],
                           k_ref[0:CTX_LEN, cols], v_ref[0:CTX_LEN, cols], _na_bias(rb_ref.at[hh]), mask)
            out_ref[:, cols] = out.astype(BF16)

    return _call(
        body, name=name, grid=(NA_HEADS // hps, t // GRID_W), in_specs=_na_specs(_offsets()),
        out_specs=[pl.BlockSpec((GRID_W, hps * dh), lambda h, r: (r, h))],
        out_shape=[jax.ShapeDtypeStruct((t, _na_w()), BF16)],
        sem=("parallel", "arbitrary"), args=(p, p, p, rb))[0]


def _na_bwd(p, rb, dmix, name):
    t = p.shape[0]
    dh, nl = NA_DH, NA_ROWS * GRID_W

    hps = _na_hps()
    wd = hps * dh
    assert ((_ret_w() + CONV_W) // dh) % hps == 0
    ob = (_ret_w() + CONV_W) // wd

    def body(q_ref, k_ref, v_ref, rb_ref, dy_ref, dq_ref, dk_ref, dv_ref, drb_ref):
        rq = pl.program_id(1)
        is_lat, kstart, base = _na_geometry(rq)
        _, _, prev_base = _na_geometry(rq - 1)
        start = pl.multiple_of(CTX_LEN + kstart * GRID_W, GRID_W)

        @pl.when(rq == 0)
        def _():
            dk_ref[...] = jnp.zeros_like(dk_ref)
            dv_ref[...] = jnp.zeros_like(dv_ref)

        @pl.when((rq == 0) | (base != prev_base))
        def _():
            drb_ref[...] = jnp.zeros_like(drb_ref)

        mask = _na_mask(is_lat)
        for hh in range(hps):
            cols = slice(hh * dh, (hh + 1) * dh)
            _, vjp = jax.vjp(lambda *a: _na_core(*a, mask), q_ref[:, cols], k_ref[pl.ds(start, nl), cols],
                             v_ref[pl.ds(start, nl), cols], k_ref[0:CTX_LEN, cols], v_ref[0:CTX_LEN, cols],
                             _na_bias(rb_ref.at[hh]))
            dq, dkl, dvl, dkc, dvc, dbias = vjp(dy_ref[:, cols])
            dq_ref[:, cols] = dq
            dk_ref[pl.ds(start, nl), cols] += dkl
            dv_ref[pl.ds(start, nl), cols] += dvl
            dk_ref[0:CTX_LEN, cols] += dkc
            dv_ref[0:CTX_LEN, cols] += dvc
            _na_dbias(dbias, drb_ref.at[hh])

    return _call(
        body, name=name, grid=(NA_HEADS // hps, t // GRID_W),
        in_specs=_na_specs(_offsets()) + [pl.BlockSpec((GRID_W, wd), lambda h, r: (r, ob + h))],
        out_specs=[pl.BlockSpec((GRID_W, wd), lambda h, r: (r, h)), pl.BlockSpec((t, wd), lambda h, r: (0, h)),
                   pl.BlockSpec((t, wd), lambda h, r: (0, h)),
                   pl.BlockSpec((hps, None, NA_ROWS, LANES), lambda h, r: (h, _na_geometry(r)[2], 0, 0))],
        out_shape=[jax.ShapeDtypeStruct((t, _na_w()), F32), jax.ShapeDtypeStruct((t, _na_w()), F32),
                   jax.ShapeDtypeStruct((t, _na_w()), F32),
                   jax.ShapeDtypeStruct((NA_HEADS, NA_ROWS, NA_ROWS, LANES), F32)],
        sem=("parallel", "arbitrary"), args=(p, p, p, rb, dmix))


def _rpb_select():
    sel = np.zeros((2 * NA_ROWS - 1, NA_ROWS * NA_ROWS), np.float32)
    for b in range(NA_ROWS):
        for kh in range(NA_ROWS):
            sel[b + kh, b * NA_ROWS + kh] = 1.0
    return jnp.asarray(sel)


def _rpb_rows(rpb):
    pad = jnp.pad(rpb, ((0, 0), (0, 0), (0, LANES - (2 * NA_COLS - 1))))
    rows = jnp.einsum("rk,hrc->hkc", _rpb_select(), pad, precision=lax.Precision.HIGHEST)
    return rows.reshape(NA_HEADS, NA_ROWS, NA_ROWS, LANES)


def _rpb_rows_t(drb):
    flat = drb.reshape(NA_HEADS, NA_ROWS * NA_ROWS, LANES)
    out = jnp.einsum("rk,hkc->hrc", _rpb_select(), flat, precision=lax.Precision.HIGHEST)
    return out[:, :, :2 * NA_COLS - 1]


def _assemble_dp(dqr, dkr, dvr, dgate, da, db, dnq, dnk, dnv, name):
    t = dgate.shape[0]
    tm = _tm()
    off = _offsets()
    sizes = dict(q=_ret_qk_w(), k=_ret_qk_w(), v=_ret_w(), g=_ret_w(), a=CONV_W, b=CONV_W, nq=_na_w(), nk=_na_w(), nv=_na_w())

    def body(q_ref, k_ref, v_ref, g_ref, a_ref, b_ref, nq_ref, nk_ref, nv_ref, o_ref):
        def put(n, val):
            o_ref[:, off[n]:off[n] + sizes[n]] = val.astype(BF16)

        put("q", q_ref[0] + q_ref[1])
        put("k", k_ref[0] + k_ref[1])
        put("v", v_ref[0] + v_ref[1])
        put("g", g_ref[...])
        put("a", a_ref[...])
        put("b", b_ref[...])
        put("nq", nq_ref[...])
        put("nk", nk_ref[...])
        put("nv", nv_ref[...])

    two = lambda w: pl.BlockSpec((2, tm, w), lambda i: (0, i, 0))
    one = lambda w: pl.BlockSpec((tm, w), lambda i: (i, 0))
    return pl.pallas_call(
        body, name=name, grid=(t // tm,),
        in_specs=[two(sizes["q"]), two(sizes["k"]), two(sizes["v"]), one(sizes["g"]), one(CONV_W), one(CONV_W),
                  one(_na_w()), one(_na_w()), one(_na_w())],
        out_specs=one(_d_in()), out_shape=jax.ShapeDtypeStruct((t, _d_in()), BF16),
        compiler_params=_params(("parallel",)))(dqr, dkr, dvr, dgate, da, db, dnq, dnk, dnv)


def _adamw(w, m, v, gs, name):
    nl, r, c = w.shape
    stacked = not isinstance(gs, (list, tuple))
    if stacked:
        gs = [gs]
    assert stacked or len(gs) == nl
    g_n = gs[0].shape[-3]
    block_bytes = 2 * 1024 * 1024
    rows = min(block_bytes // (4 * c), block_bytes // (g_n * c * gs[0].dtype.itemsize))
    tr = _tile(r, max(2 * SUBLANES, rows // (2 * SUBLANES) * (2 * SUBLANES)), 2 * SUBLANES)
    nt = r // tr
    c1 = 1.0 - ADAM_B1 ** ADAM_STEP
    c2 = 1.0 - ADAM_B2 ** ADAM_STEP

    def body(w_ref, m_ref, v_ref, *rest):
        g_refs, (go_ref, d_ref, mo_ref, vo_ref) = rest[:len(gs)], rest[len(gs):]
        layer = pl.program_id(0)
        for ll in range(len(gs)):
            @pl.when(jnp.logical_or(stacked, layer == ll))
            def _():
                g_ref = g_refs[ll]
                g = g_ref[0].astype(F32)
                for j in range(1, g_n):
                    g = g + g_ref[j].astype(F32)
                mn = ADAM_B1 * m_ref[...] + (1.0 - ADAM_B1) * g
                vn = ADAM_B2 * v_ref[...] + (1.0 - ADAM_B2) * (g * g)
                m_hat = mn / c1
                v_hat = vn / c2
                go_ref[...] = g
                d_ref[...] = -ADAM_LR * (m_hat / (jnp.sqrt(v_hat) + ADAM_EPS) + ADAM_WD * w_ref[...])
                mo_ref[...] = mn
                vo_ref[...] = vn

    def g_spec(ll):
        if stacked:
            return pl.BlockSpec((None, g_n, tr, c), lambda l, i: (l, 0, i, 0))
        return pl.BlockSpec((g_n, tr, c), lambda l, i: (0, jnp.where(l == ll, i, jnp.where(l < ll, 0, nt - 1)), 0))

    blk = pl.BlockSpec((None, tr, c), lambda l, i: (l, i, 0))
    sds = jax.ShapeDtypeStruct((nl, r, c), F32)
    return _call(
        body, name=name, grid=(nl, nt),
        in_specs=[blk, blk, blk] + [g_spec(ll) for ll in range(len(gs))],
        out_specs=[blk, blk, blk, blk], out_shape=[sds, sds, sds, sds],
        sem=("arbitrary", "arbitrary"), args=(w, m, v, *gs))


def _sum_devices(g, name):
    _, r, c = g.shape
    tr = _tile(r, 512, SUBLANES)

    def body(g_ref, o_ref):
        acc = g_ref[0]
        for j in range(1, N_DEV):
            acc = acc + g_ref[j]
        o_ref[...] = acc

    return pl.pallas_call(body, name=name, grid=(r // tr,), in_specs=[pl.BlockSpec((N_DEV, tr, c), lambda i: (0, i, 0))],
                          out_specs=pl.BlockSpec((tr, c), lambda i: (i, 0)), out_shape=jax.ShapeDtypeStruct((r, c), F32),
                          compiler_params=_params(("parallel",)))(g)


def _ada_fwd(c16, w_ada, b_shard, name):
    nl, d, cs = w_ada.shape
    tk = _tile(d, 512, LANES)
    nk = d // tk

    def body(c_ref, w_ref, b_ref, o_ref):
        kk = pl.program_id(1)

        @pl.when(kk == 0)
        def _():
            o_ref[...] = jnp.broadcast_to(b_ref[...], o_ref.shape)

        o_ref[...] += _dg(_silu(c_ref[...]), w_ref[...], 1, 0)

    return pl.pallas_call(
        body, name=name, grid=(nl, nk),
        in_specs=[pl.BlockSpec((16, tk), lambda l, kk: (0, kk)), pl.BlockSpec((None, tk, cs), lambda l, kk: (l, kk, 0)),
                  pl.BlockSpec((None, 1, cs), lambda l, kk: (l, 0, 0))],
        out_specs=pl.BlockSpec((None, 16, cs), lambda l, kk: (l, 0, 0)),
        out_shape=jax.ShapeDtypeStruct((nl, 16, cs), F32),
        compiler_params=_params(("parallel", "arbitrary")))(c16, w_ada, b_shard)


def _ada_bwd(c16, dm16, w_ada, name):
    nl, d, cs = w_ada.shape
    td = _tile(d, 512, LANES)

    def body(c_ref, dm_ref, w_ref, gw_ref, dc_ref):
        cv = c_ref[...]
        s, vjp = jax.vjp(_silu, cv)
        gw_ref[...] = _dg(s, dm_ref[...], 0, 0)
        ds = _dg(dm_ref[...], w_ref[...], 1, 1)
        dc_ref[...] = vjp(ds)[0]

    return pl.pallas_call(
        body, name=name, grid=(nl, d // td),
        in_specs=[pl.BlockSpec((16, td), lambda l, i: (0, i)), pl.BlockSpec((None, 16, cs), lambda l, i: (l, 0, 0)),
                  pl.BlockSpec((None, td, cs), lambda l, i: (l, i, 0))],
        out_specs=[pl.BlockSpec((None, td, cs), lambda l, i: (l, i, 0)), pl.BlockSpec((None, 16, td), lambda l, i: (l, 0, i))],
        out_shape=[jax.ShapeDtypeStruct((nl, d, cs), F32), jax.ShapeDtypeStruct((nl, 16, d), F32)],
        compiler_params=_params(("parallel", "parallel")))(c16, dm16, w_ada)


def _pack_rows(shape):
    n = int(np.prod(shape))
    return SUBLANES * (-(-n // (LANES * SUBLANES)))


def _pack(arrays, row_align):
    parts, total = [], 0
    for a in arrays:
        flat = a.reshape(-1).astype(F32)
        rows = _pack_rows(a.shape)
        total += rows
        parts += [flat, jnp.zeros((rows * LANES - flat.shape[0],), F32)]
    parts.append(jnp.zeros(((-total % row_align) * LANES,), F32))
    return jnp.concatenate([p for p in parts if p.shape[0]]).reshape(-1, LANES)


def _unpack(packed, shapes):
    out, r = [], 0
    for s in shapes:
        rows = _pack_rows(s)
        out.append(packed[r:r + rows].reshape(-1)[:int(np.prod(s))].reshape(s))
        r += rows
    return out


def _rope_tables():
    half, nf = RET_DK // 2, RET_DK // 4
    pos = jnp.arange(SEQ)
    row = (pos // GRID_W).astype(F32)
    col = (pos % GRID_W).astype(F32)
    inv = ROPE_BASE ** (-jnp.arange(nf, dtype=F32) / nf)
    ar, ac = row[:, None] * inv[None, :], col[:, None] * inv[None, :]
    cos = jnp.concatenate([jnp.cos(ar), jnp.cos(ar), jnp.cos(ac), jnp.cos(ac)], axis=-1)
    sin = jnp.concatenate([-jnp.sin(ar), jnp.sin(ar), -jnp.sin(ac), jnp.sin(ac)], axis=-1)
    cos = jnp.concatenate([jnp.ones((CTX_LEN, RET_DK), F32), cos], axis=0)
    sin = jnp.concatenate([jnp.zeros((CTX_LEN, RET_DK), F32), sin], axis=0)
    return cos, sin


def _layer_fwd(l, x, mod4, w, cst, arrived):
    n = lambda s: f"l{l}_{s}"
    d = D_MODEL
    h1, h1_t = _normmod_fwd(x, w["norm1_g"], mod4, 0, n("norm1"))
    w["w_in"] = _cols_from_shards(arrived("w_in", h1), n("w_in_cols"))
    p = _mm(h1, w["w_in"], n("proj_in"))
    o2, states = _ret_fwd(p, cst["cos"], cst["sin"], w["ret_decay"], cst["order"], n("ret_fwd"))
    ret_out = _ggn_fwd(o2, p, w["ret_gn_g"], n("ret_gn"))
    u2, conv_out = _conv_fwd(p, w["conv_dw_w"], w["conv_dw_b"], w["conv_ln_g"], w["conv_ln_b"], w["conv_pw"], n("conv_fwd"))
    na_out = _na_fwd(p, w["rb"], n("na_fwd"))
    mix = jnp.concatenate([ret_out, conv_out, na_out], axis=1)
    w["w_out"] = arrived("w_out", mix).reshape(_d_mix(), d)
    g1 = _mm(mix, w["w_out"], n("proj_out"))
    x1 = _gate_res_fwd(x, g1, mod4, 2, n("res1"))
    h2, h2_t = _normmod_fwd(x1, w["norm2_g"], mod4, 1, n("norm2"))
    w["ffn_up"] = arrived("ffn_up", h2)
    u = _mm(h2, w["ffn_up"], n("ffn_up"), b3=True)
    a, a_t = _ffn_act_fwd(u, w["ffn_dw_w"], w["ffn_dw_b"], n("ffn_act"))
    w["ffn_down"] = arrived("ffn_down", a).reshape(D_FF, d)
    f = _mm(a, w["ffn_down"], n("ffn_down"))
    x2 = _gate_res_fwd(x1, f, mod4, 5, n("res2"))
    saved = dict(x=x, h1_t=h1_t, p=p, o2=o2, states=states, u2=u2, mix=mix, g1=g1, x1=x1, h2_t=h2_t, u=u, a_t=a_t, f=f)
    return x2, saved


def _layer_bwd(l, dx2, s, mod4, w, cst, send):
    n = lambda t: f"l{l}_{t}"
    d = D_MODEL
    dfg, dg2 = _gate_res_bwd(dx2, s["f"], mod4, 5, n("res2_bwd"))
    da = _mm(dfg, w["ffn_down"], n("ffn_down_dx"), tb=True)
    d_ffn_down = _mm(s["a_t"], dfg, n("ffn_down_dw"), out_dtype=BF16, tm_max=DW_TM)
    tok = send(("ffn_down", l), d_ffn_down.reshape(N_DEV, D_FF // N_DEV, d))
    dcv, dcg = _ffn_act_bwd1(s["u"], da, w["ffn_dw_w"], _after(w["ffn_dw_b"], tok), n("ffn_act_bwd"))
    du, d_ffn_dw_w, d_ffn_dw_b = _dwconv_bwd(dcv, dcg, s["u"], w["ffn_dw_w"], n("ffn_dw_bwd"))
    d_ffn_dw_b = d_ffn_dw_b[0]
    dh2 = _mm(du, w["ffn_up"], n("ffn_up_dx"), tb=True, b3=True)
    d_ffn_up = _mm(s["h2_t"], du, n("ffn_up_dw"), out_dtype=BF16, tm_max=DW_TM,
                      o_cs=2 * D_FF // N_DEV)
    tok = send(("ffn_up", l), d_ffn_up)
    (dx1, dn2, dsh2, dsc2) = _normmod_bwd(s["x1"], _after(w["norm2_g"], tok), mod4, 1, dh2, dx2, n("norm2_bwd"))
    dgg, dg1 = _gate_res_bwd(dx1, s["g1"], mod4, 2, n("res1_bwd"))
    dmix = _mm(dgg, w["w_out"], n("proj_out_dx"), tb=True)
    d_w_out = _mm(_transpose_bf16(s["mix"], n("mix_t")), dgg, n("proj_out_dw"), out_dtype=BF16, tm_max=DW_TM)
    tok = send(("w_out", l), d_w_out.reshape(N_DEV, _d_mix() // N_DEV, d))
    do, dgate, dgn = _ggn_bwd(s["o2"], s["p"], _after(w["ret_gn_g"], tok), dmix, n("ret_gn_bwd"))
    dqr, dkr, dvr, ddec = _ret_bwd(s["p"], cst["cos"], cst["sin"], w["ret_decay"], cst["order"], s["states"], do, n("ret_bwd"))
    du2, dlng, dlnb, dpw = _conv_bwd1(s["u2"], dmix, w["conv_ln_g"], w["conv_ln_b"], w["conv_pw"], n("conv_bwd1"))
    dca, dcb, ddww, ddwb = _conv_bwd2(du2, s["p"], w["conv_dw_w"], n("conv_bwd2"))
    dnq, dnk, dnv, drb = _na_bwd(s["p"], w["rb"], dmix, n("na_bwd"))
    dp = _assemble_dp(dqr, dkr, dvr, dgate, dca, dcb, dnq, dnk, dnv, n("dproj"))
    h1_t = s["h1_t"]
    half = d // 2
    for i in range(2):
        d_w_in = _mm(h1_t[i * half:(i + 1) * half], dp, n(f"proj_in_dw{i}"), out_dtype=BF16, tm_max=DW_TM,
                        o_cs=_d_in() // N_DEV)
        tok = send(("w_in", l, i), d_w_in)
    dh1 = _mm(dp, w["w_in"], n("proj_in_dx"), tb=True, after=tok)
    (dx, dn1, dsh1, dsc1) = _normmod_bwd(s["x"], _after(w["norm1_g"], tok), mod4, 0, dh1, dx1, n("norm1_bwd"))
    dmod = jnp.concatenate([dsh1, dsc1, dg1, dsh2, dsc2, dg2], axis=1)
    small = dict(norm1_g=dn1[0], ret_decay=ddec[:, :, 0, 0], ret_gn_g=dgn[0], conv_dw_w=ddww, conv_dw_b=ddwb[0],
                 conv_ln_g=dlng[0], conv_ln_b=dlnb[0], conv_pw=dpw, na_rpb=_rpb_rows_t(drb), norm2_g=dn2[0],
                 ffn_dw_w=d_ffn_dw_w, ffn_dw_b=d_ffn_dw_b)
    return dx, dmod, small


def _d_mix():
    return _ret_w() + CONV_W + _na_w()


_SMALL = ["c_ctx", "b_ada", "norm1_g", "ret_decay", "ret_gn_g", "conv_dw_w", "conv_dw_b", "conv_ln_g", "conv_ln_b",
          "conv_pw", "na_rpb", "norm2_g", "ffn_dw_w", "ffn_dw_b", "final_g"]
_SMALL_SHARD_AXIS = {"conv_dw_w": 2, "conv_pw": 1, "ffn_dw_w": 2}


def kernel(x, c, ctx, c_ctx, w_ada, b_ada, norm1_g, w_in, ret_decay, ret_gn_g, conv_dw_w, conv_dw_b, conv_ln_g, conv_ln_b, conv_pw, na_rpb, w_out, norm2_g, ffn_up, ffn_dw_w, ffn_dw_b, ffn_down, final_g, loss_target, m_c_ctx, m_w_ada, m_b_ada, m_norm1_g, m_w_in, m_ret_decay, m_ret_gn_g, m_conv_dw_w, m_conv_dw_b, m_conv_ln_g, m_conv_ln_b, m_conv_pw, m_na_rpb, m_w_out, m_norm2_g, m_ffn_up, m_ffn_dw_w, m_ffn_dw_b, m_ffn_down, m_final_g, v_c_ctx, v_w_ada, v_b_ada, v_norm1_g, v_w_in, v_ret_decay, v_ret_gn_g, v_conv_dw_w, v_conv_dw_b, v_conv_ln_g, v_conv_ln_b, v_conv_pw, v_na_rpb, v_w_out, v_norm2_g, v_ffn_up, v_ffn_dw_w, v_ffn_dw_b, v_ffn_down, v_final_g):
    d, nl = D_MODEL, DEPTH
    cs = 6 * d // N_DEV
    me = _my_index()
    weights = dict(c_ctx=c_ctx, w_ada=w_ada, b_ada=b_ada, norm1_g=norm1_g, w_in=w_in, ret_decay=ret_decay, ret_gn_g=ret_gn_g,
                   conv_dw_w=conv_dw_w, conv_dw_b=conv_dw_b, conv_ln_g=conv_ln_g, conv_ln_b=conv_ln_b, conv_pw=conv_pw,
                   na_rpb=na_rpb, w_out=w_out, norm2_g=norm2_g, ffn_up=ffn_up, ffn_dw_w=ffn_dw_w, ffn_dw_b=ffn_dw_b,
                   ffn_down=ffn_down, final_g=final_g)
    mom = dict(c_ctx=m_c_ctx, w_ada=m_w_ada, b_ada=m_b_ada, norm1_g=m_norm1_g, w_in=m_w_in, ret_decay=m_ret_decay,
               ret_gn_g=m_ret_gn_g, conv_dw_w=m_conv_dw_w, conv_dw_b=m_conv_dw_b, conv_ln_g=m_conv_ln_g,
               conv_ln_b=m_conv_ln_b, conv_pw=m_conv_pw, na_rpb=m_na_rpb, w_out=m_w_out, norm2_g=m_norm2_g,
               ffn_up=m_ffn_up, ffn_dw_w=m_ffn_dw_w, ffn_dw_b=m_ffn_dw_b, ffn_down=m_ffn_down, final_g=m_final_g)
    var = dict(c_ctx=v_c_ctx, w_ada=v_w_ada, b_ada=v_b_ada, norm1_g=v_norm1_g, w_in=v_w_in, ret_decay=v_ret_decay,
               ret_gn_g=v_ret_gn_g, conv_dw_w=v_conv_dw_w, conv_dw_b=v_conv_dw_b, conv_ln_g=v_conv_ln_g,
               conv_ln_b=v_conv_ln_b, conv_pw=v_conv_pw, na_rpb=v_na_rpb, w_out=v_w_out, norm2_g=v_norm2_g,
               ffn_up=v_ffn_up, ffn_dw_w=v_ffn_dw_w, ffn_dw_b=v_ffn_dw_b, ffn_down=v_ffn_down, final_g=v_final_g)

    big_names = ["w_in", "w_out", "ffn_up", "ffn_down"]
    shards = {(nm, l): _cast_bf16(weights[nm][l], f"cast_{nm}{l}") for l in range(nl) for nm in big_names}
    small_sharded = _pack([conv_dw_w, conv_pw, ffn_dw_w], SUBLANES)
    c_rows = jnp.pad(c, ((0, SUBLANES - 1), (0, 0)))
    gathered = _run_comm(_Gather([c_rows, small_sharded, shards[("w_in", 0)]]), "gather_first")
    c_all = gathered[0][:, 0, :]
    def whole(rows, shard_shape, axis):
        n_el = int(np.prod(shard_shape))
        parts = rows.reshape(N_DEV, -1)[:, :n_el].reshape((N_DEV,) + tuple(shard_shape))
        parts = jnp.moveaxis(parts, 0, axis)
        return parts.reshape(shard_shape[:axis] + (N_DEV * shard_shape[axis],) + shard_shape[axis + 1:])

    r0 = _pack_rows(conv_dw_w.shape)
    r1 = r0 + _pack_rows(conv_pw.shape)
    r2 = r1 + _pack_rows(ffn_dw_w.shape)
    full_conv_dw_w = whole(gathered[1][:, :r0], conv_dw_w.shape, 2)
    full_conv_pw = whole(gathered[1][:, r0:r1], conv_pw.shape, 1)
    full_ffn_dw_w = whole(gathered[1][:, r1:r2], ffn_dw_w.shape, 2)

    c16 = jnp.concatenate([c_all, jnp.broadcast_to(c_ctx[None, :], (N_DEV, d))], axis=0)
    b_shard = lax.dynamic_slice_in_dim(b_ada, me * cs, cs, axis=1)[:, None, :]
    m_shard = _ada_fwd(c16, w_ada, b_shard, "ada_fwd")
    m_all = _run_comm(_Gather([m_shard.reshape(nl * 16, cs)]), "gather_mod")[0]
    m_full = m_all.reshape(N_DEV, nl, 16, cs).transpose(1, 2, 0, 3).reshape(nl, 16, 6 * d)
    m_lat = lax.dynamic_index_in_dim(m_full, me, axis=1, keepdims=False)
    mod = jnp.stack([m_full[:, N_DEV], m_lat], axis=1).reshape(nl, 2, 6, 1, d)

    arriving, token = {}, m_all
    for l in range(nl):
        for nm in big_names:
            if (nm, l) != ("w_in", 0):
                arriving[(nm, l)], token = _split_start(shards[(nm, l)], True, f"gather_{nm}{l}", token)
    mod = _after(mod, token)

    cos, sin = _rope_tables()
    cst = dict(cos=cos, sin=sin, order=_chunk_order())
    layer_w = []
    for l in range(nl):
        layer_w.append(dict(
            norm1_g=norm1_g[l][None], norm2_g=norm2_g[l][None], ret_decay=ret_decay[l], ret_gn_g=ret_gn_g[l][None],
            conv_dw_w=full_conv_dw_w[l], conv_dw_b=conv_dw_b[l][None], conv_ln_g=conv_ln_g[l][None],
            conv_ln_b=conv_ln_b[l][None], conv_pw=full_conv_pw[l], rb=_rpb_rows(na_rpb[l]),
            ffn_dw_w=full_ffn_dw_w[l], ffn_dw_b=ffn_dw_b[l][None]))

    xs = jnp.concatenate([ctx[0], x[0]], axis=0)
    saved = []
    for l in range(nl):
        def arrived(nm, after, l=l):
            if (nm, l) == ("w_in", 0):
                return gathered[2]
            return _split_wait(arriving[(nm, l)], after, f"arrived_{nm}{l}")

        xs, sv = _layer_fwd(l, xs, mod[l], layer_w[l], cst, arrived)
        saved.append(sv)
    loss_tile, dxs, dfinal = _loss_head(xs, final_g[None], loss_target[0], "loss_head")
    loss = lax.psum(loss_tile[0, 0], ("x", "y", "c"))

    dmods, smalls = [None] * nl, [None] * nl
    leaving, last = {}, [loss_tile]

    def send(key, partial):
        leaving[key], token = _split_start(partial, False, "send_" + "_".join(str(k) for k in key), last[0])
        last[0] = token
        return token

    per_layer = [nm for nm in _SMALL if nm not in ("c_ctx", "b_ada", "final_g")]
    small_packs, small_arriving = [None] * nl, [None] * nl
    for l in reversed(range(nl)):
        dxs, dmods[l], smalls[l] = _layer_bwd(l, dxs, saved[l], mod[l], layer_w[l], cst, send)
        small_packs[l] = _pack([smalls[l][nm] for nm in per_layer], 512)
        if l > 0:
            small_arriving[l], last[0] = _split_start(small_packs[l], True, f"gather_small_grads{l}", last[0])
    grad_x = dxs[CTX_LEN:][None]

    arrived_grad = lambda key, after: _split_wait(leaving[key], after, "got_" + "_".join(str(k) for k in key))
    out_big = {}
    after = dxs
    for nm in ["ffn_down", "ffn_up", "w_out"]:
        out_big[nm] = _adamw(weights[nm], mom[nm], var[nm], [arrived_grad((nm, l), after) for l in range(nl)],
                                f"adamw_{nm}")
        after = out_big[nm][0]

    dm_mine = jnp.stack(dmods).reshape(nl * 2, 6 * d)
    dm_rows = jnp.pad(dm_mine, ((0, SUBLANES - nl * 2), (0, 0)))
    dm_all = _run_comm(_Gather([dm_rows]), "gather_dmod", after=after)[0][:, :nl * 2].reshape(N_DEV, nl, 2, 6 * d)
    dm16_full = jnp.concatenate([dm_all[:, :, 1].transpose(1, 0, 2), dm_all[:, :, 0].transpose(1, 0, 2)], axis=1)
    dm16 = lax.dynamic_slice_in_dim(dm16_full, me * cs, cs, axis=2)
    g_w_ada, dc16 = _ada_bwd(c16, dm16, w_ada, "ada_bwd")

    shared = dict(c_ctx=jnp.sum(dc16[:, N_DEV:], axis=(0, 1)),
                  b_ada=jnp.sum(jnp.stack(dmods).reshape(nl, 2, 6 * d), axis=1), final_g=dfinal[0])
    shared_all = _run_comm(_Gather([_pack(list(shared.values()), SUBLANES)]), "gather_shared_grads")[0]
    small_arriving[0], token = _split_start(small_packs[0], True, "gather_small_grads0", shared_all)

    out_big["w_ada"] = _adamw(w_ada, m_w_ada, v_w_ada, g_w_ada[:, None], "adamw_w_ada")
    halves = lambda a: a.reshape(2 * nl, d // 2, a.shape[2])
    res = _adamw(halves(w_in), halves(m_w_in), halves(v_w_in),
                    [arrived_grad(("w_in", l, i), token) for l in range(nl) for i in range(2)], "adamw_w_in")
    out_big["w_in"] = [r.reshape(w_in.shape) for r in res]

    g_small = dict(zip(shared, _unpack(_sum_devices(shared_all, "sum_shared_grads"), [v.shape for v in shared.values()])))
    per = []
    for l in range(nl):
        got = _split_wait(small_arriving[l], res[0], f"arrived_small_grads{l}")
        per.append(_unpack(_sum_devices(got, f"sum_small_grads{l}"), [smalls[l][nm].shape for nm in per_layer]))
    g_small.update({nm: jnp.stack([per[l][i] for l in range(nl)]) for i, nm in enumerate(per_layer)})
    for nm, ax in _SMALL_SHARD_AXIS.items():
        n_sh = weights[nm].shape[ax]
        g_small[nm] = lax.dynamic_slice_in_dim(g_small[nm], me * n_sh, n_sh, axis=ax)
    shapes_own = [weights[nm].shape for nm in _SMALL]
    pk = lambda src: _pack([src[nm] for nm in _SMALL], 2 * SUBLANES)[None]
    res_small = _adamw(pk(weights), pk(mom), pk(var), pk(g_small)[:, None], "adamw_small")
    out_small = [dict(zip(_SMALL, _unpack(r[0], shapes_own))) for r in res_small]

    names = ["c_ctx", "w_ada", "b_ada", "norm1_g", "w_in", "ret_decay", "ret_gn_g", "conv_dw_w", "conv_dw_b", "conv_ln_g",
             "conv_ln_b", "conv_pw", "na_rpb", "w_out", "norm2_g", "ffn_up", "ffn_dw_w", "ffn_dw_b", "ffn_down", "final_g"]
    outs = [loss, grad_x]
    for kind in range(4):
        for nm in names:
            outs.append(out_big[nm][kind] if nm in out_big else out_small[kind][nm])
    return tuple(outs)
```

```python
import numpy as np
import jax
import jax.numpy as jnp
from jax import lax
from jax.experimental import pallas as pl
from jax.experimental.pallas import tpu as pltpu

D_MODEL = 2048
SEQ = 4096
DEPTH = 2
GRID_W = 64
CTX_LEN = 256
RET_HEADS = 4
RET_DK = 128
RET_DV = 256
RET_CHUNK = 128
CONV_W = 512
CONV_K = 31
NA_HEADS = 4
NA_DH = 128
NA_ROWS = 8
NA_COLS = 16
D_FF = 5632
FFN_K = 3
ROPE_BASE = 10000.0
EPS = 1e-6
ADAM_LR = 0.001
ADAM_B1 = 0.9
ADAM_B2 = 0.999
ADAM_EPS = 1e-08
ADAM_WD = 0.01
ADAM_STEP = 10
N_DEV = 8

LANES = 128
SUBLANES = 8
VMEM_LIMIT = 56 * 1024 * 1024
ROW_CHUNK = 32

F32 = jnp.float32
BF16 = jnp.bfloat16
MESH = pl.DeviceIdType.MESH
NEG = -1e30


def _ret_qk_w():
    return RET_HEADS * RET_DK


def _ret_w():
    return RET_HEADS * RET_DV


def _na_w():
    return NA_HEADS * NA_DH


def _d_in():
    return 2 * _ret_qk_w() + 2 * _ret_w() + 2 * CONV_W + 3 * _na_w()


def _offsets():
    sizes = [_ret_qk_w(), _ret_qk_w(), _ret_w(), _ret_w(), CONV_W, CONV_W, _na_w(), _na_w(), _na_w()]
    offs = [0]
    for s in sizes[:-1]:
        offs.append(offs[-1] + s)
    return dict(zip(["q", "k", "v", "g", "a", "b", "nq", "nk", "nv"], offs))


def _t_rows():
    return CTX_LEN + SEQ


def _tm():
    return CTX_LEN


def _params(sem=None):
    kw = dict(vmem_limit_bytes=VMEM_LIMIT)
    if sem is not None:
        kw["dimension_semantics"] = sem
    return pltpu.CompilerParams(**kw)


def _tile(n, pref, align):
    best = None
    for t in range(align, min(n, pref) + 1, align):
        if n % t == 0:
            best = t
    return best if best is not None else n


def _dg(a, b, ca, cb):
    return lax.dot_general(a.astype(BF16), b.astype(BF16), (((ca,), (cb,)), ((), ())), preferred_element_type=F32)


@jax.custom_vjp
def dot_nn(a, b):
    return _dg(a, b, 1, 0)


dot_nn.defvjp(lambda a, b: (_dg(a, b, 1, 0), (a, b)),
              lambda r, g: (_dg(g, r[1], 1, 1), _dg(r[0], g, 0, 0)))


@jax.custom_vjp
def dot_nt(a, b):
    return _dg(a, b, 1, 1)


dot_nt.defvjp(lambda a, b: (_dg(a, b, 1, 1), (a, b)),
              lambda r, g: (_dg(g, r[1], 1, 0), _dg(g, r[0], 0, 0)))


@jax.custom_vjp
def dot_tn(a, b):
    return _dg(a, b, 0, 0)


dot_tn.defvjp(lambda a, b: (_dg(a, b, 0, 0), (a, b)),
              lambda r, g: (_dg(r[1], g, 1, 1), _dg(r[0], g, 1, 0)))


def _sigmoid(x):
    return 0.5 * jnp.tanh(0.5 * x) + 0.5


def _silu(x):
    return x * _sigmoid(x)


def _my_pos():
    return lax.axis_index("x"), lax.axis_index("y"), lax.axis_index("c")


def _my_index():
    x, y, c = _my_pos()
    return 4 * x + 2 * y + c


_ANY = pl.BlockSpec(memory_space=pl.ANY)


class _Gather:
    def __init__(self, arrays):
        self.arrays = list(arrays)
        n = len(self.arrays)
        self.out_shape = [jax.ShapeDtypeStruct((N_DEV,) + a.shape, a.dtype) for a in self.arrays]
        self.scratch = [pltpu.SemaphoreType.DMA((n, 7)), pltpu.SemaphoreType.DMA((n, 7)), pltpu.SemaphoreType.DMA((n,))]

    def _plan(self, xs, outs, sems):
        send_sems, recv_sems, local_sems = sems
        n = len(self.arrays)
        x, y, c = _my_pos()
        me, sibling = (x, y, c), (x, y, 1 - c)
        chips = [(1 - x, y), (x, 1 - y), (1 - x, 1 - y)]

        def slot(a, p):
            return outs[a].at[4 * p[0] + 2 * p[1] + p[2]]

        def copy(a, k, block, to, src=None):
            return pltpu.make_async_remote_copy(
                src_ref=slot(a, block) if src is None else src, dst_ref=slot(a, block),
                send_sem=send_sems.at[a, k], recv_sem=recv_sems.at[a, k], device_id=to, device_id_type=MESH)

        mine = [pltpu.make_async_copy(xs[a], slot(a, me), local_sems.at[a]) for a in range(n)]
        first = []
        for a in range(n):
            first.append(copy(a, 0, me, sibling, src=xs[a]))
            first += [copy(a, 1 + j, me, (*chip, c), src=xs[a]) for j, chip in enumerate(chips)]
        return n, c, me, sibling, chips, copy, mine, first

    def start(self, xs, outs, sems):
        _, _, _, _, _, _, mine, first = self._plan(xs, outs, sems)
        for m in mine:
            m.start()
        for cp in first:
            cp.start()

    def finish(self, xs, outs, sems):
        n, c, me, sibling, chips, copy, mine, first = self._plan(xs, outs, sems)
        passed = []
        for a in range(n):
            for j, chip in enumerate(chips):
                copy(a, 1 + j, (*chip, c), me).wait_recv()
                p = copy(a, 4 + j, (*chip, c), sibling)
                p.start()
                passed.append(p)
        for a in range(n):
            copy(a, 0, sibling, me).wait_recv()
            for j, chip in enumerate(chips):
                copy(a, 4 + j, (*chip, 1 - c), me).wait_recv()
        for cp in first + passed:
            cp.wait_send()
        for m in mine:
            m.wait()


def _run_comm(comm, name, after=None):
    n = len(comm.arrays)
    extra = [] if after is None else [after]

    def body(*refs):
        xs, outs, sems = refs[:n], refs[n + len(extra):2 * n + len(extra)], refs[2 * n + len(extra):]
        comm.start(xs, outs, sems)
        comm.finish(xs, outs, sems)

    return pl.pallas_call(body, name=name, out_shape=comm.out_shape, in_specs=[_ANY] * (n + len(extra)),
                          out_specs=[_ANY] * n, scratch_shapes=comm.scratch)(*comm.arrays, *extra)


_HBM = pl.BlockSpec(memory_space=pltpu.HBM)
_SEMS = pl.BlockSpec(memory_space=pltpu.SEMAPHORE)
_EFFECT = pltpu.SideEffectType.DATAFLOW_SIDE_EFFECTING


def _own_slot(x, gathering, name):
    shape = (N_DEV,) + x.shape if gathering else x.shape
    r, c = shape[1], shape[2]
    tr = _tile(r, 256, 2 * SUBLANES)
    me = jnp.reshape(_my_index(), (1,)).astype(jnp.int32)

    def body(me_ref, x_ref, o_ref):
        o_ref[...] = x_ref[...]

    src = (pl.BlockSpec((tr, c), lambda i, m: (i, 0)) if gathering
           else pl.BlockSpec((None, tr, c), lambda i, m: (m[0], i, 0)))
    grid_spec = pltpu.PrefetchScalarGridSpec(
        num_scalar_prefetch=1, grid=(r // tr,), in_specs=[src],
        out_specs=pl.BlockSpec((None, tr, c), lambda i, m: (m[0], i, 0)))
    return pl.pallas_call(body, name=name, grid_spec=grid_spec, out_shape=jax.ShapeDtypeStruct(shape, x.dtype),
                          compiler_params=_params(("arbitrary",)))(me, x)


def _split_plan(x_ref, land_ref, send_sems, recv_sems, gathering):
    x, y, c = _my_pos()
    me = 4 * x + 2 * y + c
    sends, recvs = [], []
    for k in range(1, N_DEV):
        px = 1 - x if (k >> 2) & 1 else x
        py = 1 - y if (k >> 1) & 1 else y
        pc = 1 - c if k & 1 else c
        peer = 4 * px + 2 * py + pc
        mine, theirs = (x_ref, x_ref) if gathering else (x_ref.at[peer], x_ref.at[me])
        sends.append(pltpu.make_async_remote_copy(
            src_ref=mine, dst_ref=land_ref.at[me], send_sem=send_sems.at[k - 1], recv_sem=recv_sems.at[k - 1],
            device_id=(px, py, pc), device_id_type=MESH))
        recvs.append(pltpu.make_async_remote_copy(
            src_ref=theirs, dst_ref=land_ref.at[peer], send_sem=send_sems.at[k - 1], recv_sem=recv_sems.at[k - 1],
            device_id=(px, py, pc), device_id_type=MESH))
    return sends, recvs


def _split_start(x, gathering, name, prev):
    land = _own_slot(x, gathering, name + "_own")

    def body(x_ref, land_ref, prev_ref, send_sems, recv_sems, x_thru, land_thru, token):
        sends, _ = _split_plan(x_ref, land_ref, send_sems, recv_sems, gathering)
        for s in sends:
            s.start()
        token[...] = jnp.zeros_like(token)

    sems = pltpu.SemaphoreType.DMA((N_DEV - 1,))
    send_sems, recv_sems, x_thru, land_thru, token = pl.pallas_call(
        body, name=name,
        out_shape=(sems, sems, pltpu.HBM(x.shape, x.dtype), pltpu.HBM(land.shape, land.dtype),
                   jax.ShapeDtypeStruct((SUBLANES, LANES), F32)),
        in_specs=(_HBM, _HBM, _ANY), out_specs=(_SEMS, _SEMS, _HBM, _HBM, pl.BlockSpec(memory_space=pltpu.VMEM)),
        input_output_aliases={0: 2, 1: 3},
        compiler_params=pltpu.CompilerParams(has_side_effects=_EFFECT),
    )(pltpu.with_memory_space_constraint(x, pltpu.HBM), pltpu.with_memory_space_constraint(land, pltpu.HBM), prev)
    return (send_sems, recv_sems, x_thru, land_thru, gathering), token


def _after(a, token):
    return a + token[0, 0].astype(a.dtype)


def _split_wait(handle, after, name):
    send_sems, recv_sems, x_thru, land_thru, gathering = handle

    def body(x_ref, land_ref, send_sems, recv_sems, after_ref, x_dead, got_ref):
        sends, recvs = _split_plan(x_ref, land_ref, send_sems, recv_sems, gathering)
        for s in sends:
            s.wait_send()
        for r in recvs:
            r.wait_recv()

    return pl.pallas_call(
        body, name=name, out_shape=(pltpu.HBM(x_thru.shape, x_thru.dtype), pltpu.HBM(land_thru.shape, land_thru.dtype)),
        in_specs=(_HBM, _HBM, _SEMS, _SEMS, _ANY), out_specs=(_HBM, _HBM), input_output_aliases={0: 0, 1: 1},
        compiler_params=pltpu.CompilerParams(has_side_effects=_EFFECT),
    )(x_thru, land_thru, send_sems, recv_sems, after)[1]


def _call(body, *, name, grid, in_specs, out_specs, out_shape, args, scratch=(), sem=None, after=None):
    if after is None:
        return list(pl.pallas_call(body, name=name, grid=grid, in_specs=list(in_specs), out_specs=list(out_specs),
                                   out_shape=list(out_shape), scratch_shapes=list(scratch),
                                   compiler_params=_params(sem))(*args))
    n_in = len(in_specs)

    def wrapped(*refs):
        body(*refs[:n_in], *refs[n_in + 1:])

    return list(pl.pallas_call(wrapped, name=name, grid=grid, in_specs=list(in_specs) + [_ANY], out_specs=list(out_specs),
                               out_shape=list(out_shape), scratch_shapes=list(scratch),
                               compiler_params=_params(sem))(*args, after))


MM_B_BLOCK_BYTES = 6 * 1024 * 1024
MM_O_BLOCK_BYTES = 13 * 1024 * 1024 // 2


def _mm(a, b, name, tb=False, out_dtype=F32, b3=False, o_cs=None, tm_max=1088, after=None):
    m, k = a.shape
    if b3:
        cs = b.shape[2]
        n, kb = (b.shape[1], N_DEV * cs) if tb else (N_DEV * cs, b.shape[1])
    else:
        n, kb = (b.shape[0], b.shape[1]) if tb else (b.shape[1], b.shape[0])
    assert k == kb, (a.shape, b.shape, tb)
    if b3 and tb:
        tm, tn = _tile(m, 544, 2 * SUBLANES), _tile(n, 256, LANES)

        def body_shards(a_ref, b_ref, o_ref):
            r = None
            for j in range(N_DEV):
                part = lax.dot_general(a_ref[:, j * cs:(j + 1) * cs], b_ref[j], (((1,), (1,)), ((), ())),
                                       preferred_element_type=F32)
                r = part if r is None else r + part
            o_ref[...] = r.astype(o_ref.dtype)

        return _call(
            body_shards, name=name, grid=(m // tm, n // tn),
            in_specs=[pl.BlockSpec((tm, k), lambda i, j: (i, 0)), pl.BlockSpec((N_DEV, tn, cs), lambda i, j: (0, j, 0))],
            out_specs=[pl.BlockSpec((tm, tn), lambda i, j: (i, j))], out_shape=[jax.ShapeDtypeStruct((m, n), out_dtype)],
            sem=("parallel", "parallel"), args=(a, b), after=after)[0]
    tm = _tile(m, tm_max, 2 * SUBLANES)
    tk = k
    if b3:
        tn = cs
    elif o_cs is not None:
        tn = o_cs if o_cs % LANES == 0 else 2 * o_cs
    else:
        tn = _tile(n, min(MM_B_BLOCK_BYTES // (2 * tk), MM_O_BLOCK_BYTES // (4 * tm)), LANES)
    cb = 1 if tb else 0
    dn = (((1,), (cb,)), ((), ()))

    def body_one(a_ref, b_ref, o_ref):
        r = lax.dot_general(a_ref[...], b_ref[...], dn, preferred_element_type=F32)
        if o_cs is None:
            o_ref[...] = r.astype(o_ref.dtype)
        else:
            for j in range(tn // o_cs):
                o_ref[j] = r[:, j * o_cs:(j + 1) * o_cs].astype(o_ref.dtype)

    a_spec = pl.BlockSpec((tm, tk), lambda i, j: (i, 0))
    if b3:
        b_spec = pl.BlockSpec((None, tk, cs), lambda i, j: (j, 0, 0))
    else:
        b_spec = pl.BlockSpec((tn, tk), lambda i, j: (j, 0)) if tb else pl.BlockSpec((tk, tn), lambda i, j: (0, j))
    if o_cs is None:
        o_spec = pl.BlockSpec((tm, tn), lambda i, j: (i, j))
        o_shape = jax.ShapeDtypeStruct((m, n), out_dtype)
    else:
        o_spec = pl.BlockSpec((tn // o_cs, tm, o_cs), lambda i, j: (j, i, 0))
        o_shape = jax.ShapeDtypeStruct((n // o_cs, m, o_cs), out_dtype)
    return _call(
        body_one, name=name, grid=(m // tm, n // tn), in_specs=[a_spec, b_spec], out_specs=[o_spec], out_shape=[o_shape],
        sem=("parallel", "parallel"), args=(a, b), after=after)[0]


DW_TM = 512


def _transpose_bf16(x, name):
    t, c = x.shape
    tt = _tm()

    def body(x_ref, o_ref):
        o_ref[...] = x_ref[...].T

    return pl.pallas_call(body, name=name, grid=(t // tt,), in_specs=[pl.BlockSpec((tt, c), lambda i: (i, 0))],
                          out_specs=pl.BlockSpec((c, tt), lambda i: (0, i)),
                          out_shape=jax.ShapeDtypeStruct((c, t), BF16), compiler_params=_params(("parallel",)))(x)


def _cast_bf16(x, name):
    r, c = x.shape
    tr = _tile(r, 512, 2 * SUBLANES)

    def body(x_ref, o_ref):
        o_ref[...] = x_ref[...].astype(BF16)

    return pl.pallas_call(body, name=name, grid=(r // tr,), in_specs=[pl.BlockSpec((tr, c), lambda i: (i, 0))],
                          out_specs=pl.BlockSpec((tr, c), lambda i: (i, 0)),
                          out_shape=jax.ShapeDtypeStruct((r, c), BF16), compiler_params=_params(("parallel",)))(x)


def _cols_from_shards(wg, name):
    _, k, cs = wg.shape
    tk = _tile(k, 256, 2 * SUBLANES)

    def body(w_ref, o_ref):
        for j in range(N_DEV):
            o_ref[:, j * cs:(j + 1) * cs] = w_ref[j]

    return pl.pallas_call(body, name=name, grid=(k // tk,),
                          in_specs=[pl.BlockSpec((N_DEV, tk, cs), lambda i: (0, i, 0))],
                          out_specs=pl.BlockSpec((tk, N_DEV * cs), lambda i: (i, 0)),
                          out_shape=jax.ShapeDtypeStruct((k, N_DEV * cs), wg.dtype),
                          compiler_params=_params(("parallel",)))(wg)


def _stream(i):
    return jnp.minimum(i, 1)


def _normmod(x, g, sh, sc):
    y = x * lax.rsqrt(jnp.mean(x * x, axis=-1, keepdims=True) + EPS)
    return (y * g) * (1.0 + sc) + sh


def _mod_spec(chunk, d):
    return pl.BlockSpec((None, None, 1, d), lambda i: (_stream(i), chunk, 0, 0))


def _normmod_fwd(x, g, mod4, which, name):
    t, d = x.shape
    tm = _tm()
    ish, isc = (0, 1) if which == 0 else (3, 4)

    def body(x_ref, g_ref, sh_ref, sc_ref, o_ref, ot_ref):
        h = _normmod(x_ref[...], g_ref[...], sh_ref[...], sc_ref[...]).astype(BF16)
        o_ref[...] = h
        ot_ref[...] = h.T

    row = pl.BlockSpec((tm, d), lambda i: (i, 0))
    return pl.pallas_call(body, name=name, grid=(t // tm,),
                          in_specs=[row, pl.BlockSpec((1, d), lambda i: (0, 0)), _mod_spec(ish, d), _mod_spec(isc, d)],
                          out_specs=[row, pl.BlockSpec((d, tm), lambda i: (0, i))],
                          out_shape=[jax.ShapeDtypeStruct((t, d), BF16), jax.ShapeDtypeStruct((d, t), BF16)],
                          compiler_params=_params(("parallel",)))(x, g, mod4, mod4)


def _normmod_bwd(x, g, mod4, which, dh, dres, name):
    t, d = x.shape
    tm = _tm()
    ish, isc = (0, 1) if which == 0 else (3, 4)

    def body(x_ref, g_ref, sh_ref, sc_ref, dh_ref, dres_ref, dx_ref, dg_ref, dsh_ref, dsc_ref):
        i = pl.program_id(0)
        _, vjp = jax.vjp(_normmod, x_ref[...], g_ref[...], sh_ref[...], sc_ref[...])
        dx, dg, dsh, dsc = vjp(dh_ref[...])
        dx_ref[...] = dres_ref[...] + dx

        @pl.when(i == 0)
        def _():
            dg_ref[...] = jnp.zeros_like(dg_ref)

        @pl.when(i <= 1)
        def _():
            dsh_ref[...] = jnp.zeros_like(dsh_ref)
            dsc_ref[...] = jnp.zeros_like(dsc_ref)

        dg_ref[...] += dg
        dsh_ref[...] += dsh
        dsc_ref[...] += dsc

    row = pl.BlockSpec((tm, d), lambda i: (i, 0))
    vec = pl.BlockSpec((1, d), lambda i: (0, 0))
    svec = pl.BlockSpec((None, 1, d), lambda i: (_stream(i), 0, 0))
    return _call(
        body, name=name, grid=(t // tm,),
        in_specs=[row, vec, _mod_spec(ish, d), _mod_spec(isc, d), row, row],
        out_specs=[row, vec, svec, svec],
        out_shape=[jax.ShapeDtypeStruct((t, d), F32), jax.ShapeDtypeStruct((1, d), F32),
                   jax.ShapeDtypeStruct((2, 1, d), F32), jax.ShapeDtypeStruct((2, 1, d), F32)],
        sem=("arbitrary",), args=(x, g, mod4, mod4, dh, dres))


def _gate_res_fwd(x, f, mod4, chunk, name):
    t, d = x.shape
    tm = _tm()

    def body(x_ref, f_ref, g_ref, o_ref):
        o_ref[...] = x_ref[...] + g_ref[...] * f_ref[...]

    row = pl.BlockSpec((tm, d), lambda i: (i, 0))
    return pl.pallas_call(body, name=name, grid=(t // tm,), in_specs=[row, row, _mod_spec(chunk, d)], out_specs=row,
                          out_shape=jax.ShapeDtypeStruct((t, d), F32), compiler_params=_params(("parallel",)))(x, f, mod4)


def _gate_res_bwd(dx, f, mod4, chunk, name):
    t, d = dx.shape
    tm = _tm()

    def body(dx_ref, f_ref, g_ref, o_ref, dg_ref):
        i = pl.program_id(0)
        dxv = dx_ref[...]
        o_ref[...] = (dxv * g_ref[...]).astype(BF16)

        @pl.when(i <= 1)
        def _():
            dg_ref[...] = jnp.zeros_like(dg_ref)

        dg_ref[...] += jnp.sum(dxv * f_ref[...], axis=0, keepdims=True)

    row = pl.BlockSpec((tm, d), lambda i: (i, 0))
    return pl.pallas_call(
        body, name=name, grid=(t // tm,), in_specs=[row, row, _mod_spec(chunk, d)],
        out_specs=[row, pl.BlockSpec((None, 1, d), lambda i: (_stream(i), 0, 0))],
        out_shape=[jax.ShapeDtypeStruct((t, d), BF16), jax.ShapeDtypeStruct((2, 1, d), F32)],
        compiler_params=_params(("arbitrary",)))(dx, f, mod4)


def _loss_head(x, final_g, target, name):
    t, d = x.shape
    tm = _tm()

    def loss_fn(xv, g, tgt):
        y = (xv * lax.rsqrt(jnp.mean(xv * xv, axis=-1, keepdims=True) + EPS)) * g
        err = y - tgt
        return 0.5 * jnp.sum(jnp.mean(err * err, axis=-1, keepdims=True))

    def body(x_ref, g_ref, t_ref, l_ref, dx_ref, dg_ref):
        i = pl.program_id(0)

        @pl.when(i == 0)
        def _():
            l_ref[...] = jnp.zeros_like(l_ref)
            dg_ref[...] = jnp.zeros_like(dg_ref)
            dx_ref[...] = jnp.zeros_like(dx_ref)

        @pl.when(i > 0)
        def _():
            l, (dx, dg) = jax.value_and_grad(loss_fn, argnums=(0, 1))(x_ref[...], g_ref[...], t_ref[...])
            l_ref[...] += jnp.full(l_ref.shape, l, F32)
            dx_ref[...] = dx
            dg_ref[...] += dg

    row = pl.BlockSpec((tm, d), lambda i: (i, 0))
    vec = pl.BlockSpec((1, d), lambda i: (0, 0))
    return pl.pallas_call(
        body, name=name, grid=(t // tm,),
        in_specs=[row, vec, pl.BlockSpec((tm, d), lambda i: (jnp.maximum(i - 1, 0), 0))],
        out_specs=[pl.BlockSpec((SUBLANES, LANES), lambda i: (0, 0)), row, vec],
        out_shape=[jax.ShapeDtypeStruct((SUBLANES, LANES), F32), jax.ShapeDtypeStruct((t, d), F32),
                   jax.ShapeDtypeStruct((1, d), F32)],
        compiler_params=_params(("arbitrary",)))(x, final_g, target)


def _swap_quarters(x):
    half, nf = RET_DK // 2, RET_DK // 4
    lane = lax.broadcasted_iota(jnp.int32, x.shape, 1)
    return jnp.where((lane % half) < nf, pltpu.roll(x, RET_DK - nf, 1), pltpu.roll(x, nf, 1))


def _rope(x, cos, sin):
    return x * cos + _swap_quarters(x) * sin


def _rope_t(y, cos, sin):
    return y * cos + _swap_quarters(y * sin)


def _ret_consts(d):
    c = RET_CHUNK
    ii = lax.broadcasted_iota(jnp.int32, (c, 1), 0).astype(F32)
    jj = lax.broadcasted_iota(jnp.int32, (1, c), 1).astype(F32)
    fwd = d == 0
    sgn = jnp.where(fwd, 1.0, -1.0).astype(F32)
    pos = jnp.where(fwd, ii, c - 1.0 - ii)
    return sgn * (ii - jj), pos


def _ret_step(lgt, state, q, k, v, diff, pos):
    c = float(RET_CHUNK)
    lg = -(jnp.maximum(-lgt, 0.0) + jnp.log1p(jnp.exp(-jnp.abs(lgt))))
    lower = diff >= 0
    decay = jnp.where(lower, jnp.exp(jnp.where(lower, diff, 0.0) * lg), 0.0)
    xi = jnp.exp((pos + 1.0) * lg)
    zeta = jnp.exp((c - 1.0 - pos) * lg)
    gch = jnp.exp(c * lg)
    inner = dot_nt(q, k) * decay
    out = dot_nn(inner, v) + dot_nn(q, state) * xi
    new_state = state * gch + dot_tn(k * zeta, v)
    return out, new_state


def _chunk_order():
    nc, nch = CTX_LEN // RET_CHUNK, _t_rows() // RET_CHUNK
    fwd = list(range(nch))
    bwd = list(range(nc - 1, -1, -1)) + list(range(nch - 1, nc - 1, -1))
    return jnp.asarray(np.array([fwd, bwd], np.int32))


def _ret_fwd(p, cos, sin, decay, order, name):
    t = p.shape[0]
    c, dk, dv, nh = RET_CHUNK, RET_DK, RET_DV, RET_HEADS
    nch = t // c
    off = _offsets()
    wqk, wv = nh * dk, nh * dv
    assert off["q"] % wqk == 0 and off["k"] % wqk == 0 and off["v"] % wv == 0
    qb, kb, vb = off["q"] // wqk, off["k"] // wqk, off["v"] // wv
    scale = RET_DK ** -0.5

    def body(ord_ref, dec_ref, q_ref, k_ref, v_ref, cos_ref, sin_ref, o_ref, st_ref, state):
        d, s = pl.program_id(0), pl.program_id(1)

        @pl.when(s == 0)
        def _():
            state[...] = jnp.zeros_like(state)

        diff, pos = _ret_consts(d)
        cosv, sinv = cos_ref[...], sin_ref[...]
        for h in range(nh):
            st = state[h]
            st_ref[h] = st
            lgt = jnp.full((1, 1), dec_ref[d, h], F32)
            q = _rope(q_ref[:, h * dk:(h + 1) * dk], cosv, sinv) * scale
            k = _rope(k_ref[:, h * dk:(h + 1) * dk], cosv, sinv)
            out, ns = _ret_step(lgt, st, q, k, v_ref[:, h * dv:(h + 1) * dv], diff, pos)
            o_ref[:, h * dv:(h + 1) * dv] = out
            state[h] = ns

    grid_spec = pltpu.PrefetchScalarGridSpec(
        num_scalar_prefetch=1, grid=(2, nch),
        in_specs=[pl.BlockSpec(memory_space=pltpu.SMEM),
                  pl.BlockSpec((c, wqk), lambda d, s, o: (o[d, s], qb)),
                  pl.BlockSpec((c, wqk), lambda d, s, o: (o[d, s], kb)),
                  pl.BlockSpec((c, wv), lambda d, s, o: (o[d, s], vb)),
                  pl.BlockSpec((c, dk), lambda d, s, o: (o[d, s], 0)),
                  pl.BlockSpec((c, dk), lambda d, s, o: (o[d, s], 0))],
        out_specs=[pl.BlockSpec((None, c, wv), lambda d, s, o: (d, o[d, s], 0)),
                   pl.BlockSpec((None, nh, None, dk, dv), lambda d, s, o: (d, 0, s, 0, 0))],
        scratch_shapes=[pltpu.VMEM((nh, dk, dv), F32)])
    return pl.pallas_call(
        body, name=name, grid_spec=grid_spec,
        out_shape=[jax.ShapeDtypeStruct((2, t, wv), F32), jax.ShapeDtypeStruct((2, nh, nch, dk, dv), F32)],
        compiler_params=_params(("arbitrary", "arbitrary")))(order, decay, p, p, p, cos, sin)


def _ret_bwd(p, cos, sin, decay, order, states, do, name):
    t = p.shape[0]
    c, dk, dv, nh = RET_CHUNK, RET_DK, RET_DV, RET_HEADS
    nch = t // c
    off = _offsets()
    wqk, wv = nh * dk, nh * dv
    qb, kb, vb = off["q"] // wqk, off["k"] // wqk, off["v"] // wv
    scale = RET_DK ** -0.5

    def body(ord_ref, dec_ref, q_ref, k_ref, v_ref, cos_ref, sin_ref, st_ref, do_ref,
             dq_ref, dk_ref, dv_ref, dd_ref, dstate):
        d, s = pl.program_id(0), pl.program_id(1)

        @pl.when(s == 0)
        def _():
            dstate[...] = jnp.zeros_like(dstate)
            dd_ref[...] = jnp.zeros_like(dd_ref)

        diff, pos = _ret_consts(d)
        cosv, sinv = cos_ref[...], sin_ref[...]
        for h in range(nh):
            qk, vv = slice(h * dk, (h + 1) * dk), slice(h * dv, (h + 1) * dv)
            lgt = jnp.full((1, 1), dec_ref[d, h], F32)
            q = _rope(q_ref[:, qk], cosv, sinv) * scale
            k = _rope(k_ref[:, qk], cosv, sinv)
            _, vjp = jax.vjp(lambda a, b, cq, ck, cv: _ret_step(a, b, cq, ck, cv, diff, pos),
                             lgt, st_ref[h], q, k, v_ref[:, vv])
            dlgt, dst, dq, dkk, dvv = vjp((do_ref[:, vv], dstate[h]))
            dstate[h] = dst
            dq_ref[:, qk] = _rope_t(dq * scale, cosv, sinv)
            dk_ref[:, qk] = _rope_t(dkk, cosv, sinv)
            dv_ref[:, vv] = dvv
            dd_ref[h] += jnp.broadcast_to(dlgt, (SUBLANES, LANES))

    rev = lambda o, d, s: o[d, nch - 1 - s]
    grid_spec = pltpu.PrefetchScalarGridSpec(
        num_scalar_prefetch=1, grid=(2, nch),
        in_specs=[pl.BlockSpec(memory_space=pltpu.SMEM),
                  pl.BlockSpec((c, wqk), lambda d, s, o: (rev(o, d, s), qb)),
                  pl.BlockSpec((c, wqk), lambda d, s, o: (rev(o, d, s), kb)),
                  pl.BlockSpec((c, wv), lambda d, s, o: (rev(o, d, s), vb)),
                  pl.BlockSpec((c, dk), lambda d, s, o: (rev(o, d, s), 0)),
                  pl.BlockSpec((c, dk), lambda d, s, o: (rev(o, d, s), 0)),
                  pl.BlockSpec((None, nh, None, dk, dv), lambda d, s, o: (d, 0, nch - 1 - s, 0, 0)),
                  pl.BlockSpec((c, wv), lambda d, s, o: (rev(o, d, s), 0))],
        out_specs=[pl.BlockSpec((None, c, wqk), lambda d, s, o: (d, rev(o, d, s), 0)),
                   pl.BlockSpec((None, c, wqk), lambda d, s, o: (d, rev(o, d, s), 0)),
                   pl.BlockSpec((None, c, wv), lambda d, s, o: (d, rev(o, d, s), 0)),
                   pl.BlockSpec((None, nh, SUBLANES, LANES), lambda d, s, o: (d, 0, 0, 0))],
        scratch_shapes=[pltpu.VMEM((nh, dk, dv), F32)])
    return pl.pallas_call(
        body, name=name, grid_spec=grid_spec,
        out_shape=[jax.ShapeDtypeStruct((2, t, wqk), F32), jax.ShapeDtypeStruct((2, t, wqk), F32),
                   jax.ShapeDtypeStruct((2, t, wv), F32), jax.ShapeDtypeStruct((2, nh, SUBLANES, LANES), F32)],
        compiler_params=_params(("arbitrary", "arbitrary")))(order, decay, p, p, p, cos, sin, states, do)


def _ggn_head(of, ob, gate, g):
    o = of + ob
    mu = jnp.mean(o, axis=-1, keepdims=True)
    var = jnp.mean(jnp.square(o - mu), axis=-1, keepdims=True)
    return ((o - mu) * lax.rsqrt(var + EPS) * g) * _silu(gate)


def _ggn_fwd(o2, p, gn_g, name):
    t = p.shape[0]
    tm, w, dv = _tm(), _ret_w(), RET_DV
    gb = _offsets()["g"] // w

    def body(o_ref, gate_ref, g_ref, out_ref):
        for h in range(RET_HEADS):
            sl = slice(h * dv, (h + 1) * dv)
            out_ref[:, sl] = _ggn_head(o_ref[0, :, sl], o_ref[1, :, sl], gate_ref[:, sl], g_ref[:, sl]).astype(BF16)

    return pl.pallas_call(
        body, name=name, grid=(t // tm,),
        in_specs=[pl.BlockSpec((2, tm, w), lambda i: (0, i, 0)), pl.BlockSpec((tm, w), lambda i: (i, gb)),
                  pl.BlockSpec((1, w), lambda i: (0, 0))],
        out_specs=pl.BlockSpec((tm, w), lambda i: (i, 0)), out_shape=jax.ShapeDtypeStruct((t, w), BF16),
        compiler_params=_params(("parallel",)))(o2, p, gn_g)


def _ggn_bwd(o2, p, gn_g, dmix, name):
    t = p.shape[0]
    tm, w, dv = _tm(), _ret_w(), RET_DV
    gb = _offsets()["g"] // w

    def body(o_ref, gate_ref, g_ref, dy_ref, do_ref, dgate_ref, dg_ref):
        i = pl.program_id(0)

        @pl.when(i == 0)
        def _():
            dg_ref[...] = jnp.zeros_like(dg_ref)

        for h in range(RET_HEADS):
            sl = slice(h * dv, (h + 1) * dv)
            _, vjp = jax.vjp(_ggn_head, o_ref[0, :, sl], o_ref[1, :, sl], gate_ref[:, sl], g_ref[:, sl])
            do, _, dgate, dg = vjp(dy_ref[:, sl])
            do_ref[:, sl] = do
            dgate_ref[:, sl] = dgate
            dg_ref[:, sl] += dg

    row = pl.BlockSpec((tm, w), lambda i: (i, 0))
    return pl.pallas_call(
        body, name=name, grid=(t // tm,),
        in_specs=[pl.BlockSpec((2, tm, w), lambda i: (0, i, 0)), pl.BlockSpec((tm, w), lambda i: (i, gb)),
                  pl.BlockSpec((1, w), lambda i: (0, 0)), row],
        out_specs=[row, row, pl.BlockSpec((1, w), lambda i: (0, 0))],
        out_shape=[jax.ShapeDtypeStruct((t, w), F32), jax.ShapeDtypeStruct((t, w), F32),
                   jax.ShapeDtypeStruct((1, w), F32)],
        compiler_params=_params(("arbitrary",)))(o2, p, gn_g, dmix)


def _halo(k):
    return SUBLANES * ((k // 2 + SUBLANES - 1) // SUBLANES)


def _halo_specs(width, colblock, h, tm):
    r = tm // h
    return [pl.BlockSpec((h, width), lambda i, *_: (jnp.maximum(i * r - 1, 0), colblock(*_))),
            pl.BlockSpec((tm, width), lambda i, *_: (i, colblock(*_))),
            pl.BlockSpec((h, width), lambda i, *_: (jnp.minimum((i + 1) * r, (_t_rows() // h) - 1), colblock(*_)))]


def _fill_ext(ext_ref, prev, cur, nxt, i, h, tm):
    nt = _t_rows() // tm
    ext_ref[0:h, :] = jnp.where(i >= 2, prev, 0.0)
    ext_ref[h:h + tm, :] = cur
    ext_ref[h + tm:h + tm + h, :] = jnp.where((i >= 1) & (i <= nt - 2), nxt, 0.0)


def _corr(ext_ref, w_ref, k, h, tm, flip, cols=slice(None)):
    pad = k // 2
    acc = None
    for kk in range(k):
        o = h + (pad - kk if flip else kk - pad)
        term = w_ref[kk:kk + 1, cols] * ext_ref[o:o + tm, cols]
        acc = term if acc is None else acc + term
    return acc


LANE_CHUNK = 512


def _chunks(tm, tc):
    return [(r0, slice(c0, min(c0 + LANE_CHUNK, tc))) for r0 in range(0, tm, ROW_CHUNK) for c0 in range(0, tc, LANE_CHUNK)]


def _conv_post(u2, ln_g, ln_b, pw):
    mu = jnp.mean(u2, axis=-1, keepdims=True)
    var = jnp.mean(jnp.square(u2 - mu), axis=-1, keepdims=True)
    y = (u2 - mu) * lax.rsqrt(var + EPS) * ln_g + ln_b
    return dot_nn(_silu(y), pw)


def _conv_fwd(p, dw_w, dw_b, ln_g, ln_b, pw, name):
    t = p.shape[0]
    tm, w, k = _tm(), CONV_W, CONV_K
    h = _halo(k)
    off = _offsets()
    ab, bb = off["a"] // w, off["b"] // w

    def body(ap, ac, an, bp, bc, bn, w_ref, b_ref, g_ref, beta_ref, pw_ref, u2_ref, out_ref, ext):
        i = pl.program_id(0)
        glu = lambda a, b: a * _sigmoid(b)
        _fill_ext(ext, glu(ap[...], bp[...]), glu(ac[...], bc[...]), glu(an[...], bn[...]), i, h, tm)
        for r0 in range(0, tm, ROW_CHUNK):
            u2_ref[r0:r0 + ROW_CHUNK, :] = _corr(ext, w_ref, k, h + r0, ROW_CHUNK, False) + b_ref[...]
        out_ref[...] = _conv_post(u2_ref[...], g_ref[...], beta_ref[...], pw_ref[...]).astype(BF16)

    vec = pl.BlockSpec((1, w), lambda i: (0, 0))
    row = pl.BlockSpec((tm, w), lambda i: (i, 0))
    return pl.pallas_call(
        body, name=name, grid=(t // tm,),
        in_specs=_halo_specs(w, lambda: ab, h, tm) + _halo_specs(w, lambda: bb, h, tm)
        + [pl.BlockSpec((k, w), lambda i: (0, 0)), vec, vec, vec, pl.BlockSpec((w, w), lambda i: (0, 0))],
        out_specs=[row, row],
        out_shape=[jax.ShapeDtypeStruct((t, w), F32), jax.ShapeDtypeStruct((t, w), BF16)],
        scratch_shapes=[pltpu.VMEM((tm + 2 * h, w), F32)],
        compiler_params=_params(("parallel",)))(p, p, p, p, p, p, dw_w, dw_b, ln_g, ln_b, pw)


def _conv_bwd1(u2, dmix, ln_g, ln_b, pw, name):
    t = u2.shape[0]
    tm, w = _tm(), CONV_W
    cb = _ret_w() // w

    def body(u2_ref, dy_ref, g_ref, beta_ref, pw_ref, du2_ref, dg_ref, db_ref, dpw_ref):
        i = pl.program_id(0)

        @pl.when(i == 0)
        def _():
            dg_ref[...] = jnp.zeros_like(dg_ref)
            db_ref[...] = jnp.zeros_like(db_ref)
            dpw_ref[...] = jnp.zeros_like(dpw_ref)

        _, vjp = jax.vjp(_conv_post, u2_ref[...], g_ref[...], beta_ref[...], pw_ref[...])
        du2, dg, db, dpw = vjp(dy_ref[...])
        du2_ref[...] = du2
        dg_ref[...] += dg
        db_ref[...] += db
        dpw_ref[...] += dpw

    vec = pl.BlockSpec((1, w), lambda i: (0, 0))
    row = pl.BlockSpec((tm, w), lambda i: (i, 0))
    mat = pl.BlockSpec((w, w), lambda i: (0, 0))
    return pl.pallas_call(
        body, name=name, grid=(t // tm,),
        in_specs=[row, pl.BlockSpec((tm, w), lambda i: (i, cb)), vec, vec, mat],
        out_specs=[row, vec, vec, mat],
        out_shape=[jax.ShapeDtypeStruct((t, w), F32), jax.ShapeDtypeStruct((1, w), F32),
                   jax.ShapeDtypeStruct((1, w), F32), jax.ShapeDtypeStruct((w, w), F32)],
        compiler_params=_params(("arbitrary",)))(u2, dmix, ln_g, ln_b, pw)


def _conv_bwd2(du2, p, dw_w, name):
    t = p.shape[0]
    tm, w, k = _tm(), CONV_W, CONV_K
    h = _halo(k)
    pad = k // 2
    off = _offsets()
    ab, bb = off["a"] // w, off["b"] // w

    def body(dp, dc, dn, ap, ac, an, bp, bc, bn, w_ref, da_ref, db_ref, dw_ref, dbias_ref, ext_d, ext_u):
        i = pl.program_id(0)

        @pl.when(i == 0)
        def _():
            dw_ref[...] = jnp.zeros_like(dw_ref)
            dbias_ref[...] = jnp.zeros_like(dbias_ref)

        glu = lambda a, b: a * _sigmoid(b)
        _fill_ext(ext_d, dp[...], dc[...], dn[...], i, h, tm)
        _fill_ext(ext_u, glu(ap[...], bp[...]), glu(ac[...], bc[...]), glu(an[...], bn[...]), i, h, tm)
        chunks = range(0, tm, ROW_CHUNK)
        acc_b = jnp.zeros((ROW_CHUNK, w), F32)
        for r0 in chunks:
            rows = slice(r0, r0 + ROW_CHUNK)
            du = _corr(ext_d, w_ref, k, h + r0, ROW_CHUNK, True)
            sg = _sigmoid(bc[rows, :])
            da_ref[rows, :] = du * sg
            db_ref[rows, :] = du * ac[rows, :] * sg * (1.0 - sg)
            acc_b = acc_b + ext_d[h + r0:h + r0 + ROW_CHUNK, :]
        dbias_ref[...] += jnp.sum(acc_b, axis=0, keepdims=True)
        for kk in range(k):
            acc = jnp.zeros((ROW_CHUNK, w), F32)
            for r0 in chunks:
                o = h + r0 + kk - pad
                acc = acc + ext_d[h + r0:h + r0 + ROW_CHUNK, :] * ext_u[o:o + ROW_CHUNK, :]
            dw_ref[kk:kk + 1, :] += jnp.sum(acc, axis=0, keepdims=True)

    vec = pl.BlockSpec((1, w), lambda i: (0, 0))
    row = pl.BlockSpec((tm, w), lambda i: (i, 0))
    kw = pl.BlockSpec((k, w), lambda i: (0, 0))
    return _call(
        body, name=name, grid=(t // tm,),
        in_specs=_halo_specs(w, lambda: 0, h, tm) + _halo_specs(w, lambda: ab, h, tm)
        + _halo_specs(w, lambda: bb, h, tm) + [kw],
        out_specs=[row, row, kw, vec],
        out_shape=[jax.ShapeDtypeStruct((t, w), F32), jax.ShapeDtypeStruct((t, w), F32),
                   jax.ShapeDtypeStruct((k, w), F32), jax.ShapeDtypeStruct((1, w), F32)],
        scratch=[pltpu.VMEM((tm + 2 * h, w), F32), pltpu.VMEM((tm + 2 * h, w), F32)],
        sem=("arbitrary",), args=(du2, du2, du2, p, p, p, p, p, p, dw_w))


def _ffn_tc():
    return _tile(D_FF, 2816, LANES)


def _ffn_act_fwd(u, dw_w, dw_b, name):
    t = u.shape[0]
    tm, k, tc = _tm(), FFN_K, _ffn_tc()
    h = _halo(k)
    nj = D_FF // tc

    def body(vp, vc, vn, gp, gc, gn, wv, wg, bv, bg, out_ref, out_t_ref, ext_v, ext_g):
        i = pl.program_id(0)
        _fill_ext(ext_v, vp[...], vc[...], vn[...], i, h, tm)
        _fill_ext(ext_g, gp[...], gc[...], gn[...], i, h, tm)
        for r0, cols in _chunks(tm, tc):
            val = _corr(ext_v, wv, k, h + r0, ROW_CHUNK, False, cols) + bv[:, cols]
            gate = _corr(ext_g, wg, k, h + r0, ROW_CHUNK, False, cols) + bg[:, cols]
            out_ref[r0:r0 + ROW_CHUNK, cols] = (_silu(gate) * val).astype(BF16)
        out_t_ref[...] = out_ref[...].T

    wspec = lambda s: pl.BlockSpec((k, tc), lambda i, j: (0, j + s))
    bspec = lambda s: pl.BlockSpec((1, tc), lambda i, j: (0, j + s))
    return pl.pallas_call(
        body, name=name, grid=(t // tm, nj),
        in_specs=_halo_specs(tc, lambda j: j, h, tm) + _halo_specs(tc, lambda j: j + nj, h, tm)
        + [wspec(0), wspec(nj), bspec(0), bspec(nj)],
        out_specs=[pl.BlockSpec((tm, tc), lambda i, j: (i, j)), pl.BlockSpec((tc, tm), lambda i, j: (j, i))],
        out_shape=[jax.ShapeDtypeStruct((t, D_FF), BF16), jax.ShapeDtypeStruct((D_FF, t), BF16)],
        scratch_shapes=[pltpu.VMEM((tm + 2 * h, tc), F32), pltpu.VMEM((tm + 2 * h, tc), F32)],
        compiler_params=_params(("parallel", "parallel")))(u, u, u, u, u, u, dw_w, dw_w, dw_b, dw_b)


def _ffn_act_bwd1(u, da, dw_w, dw_b, name):
    t = u.shape[0]
    tm, k, tc = _tm(), FFN_K, _ffn_tc()
    h = _halo(k)
    nj = D_FF // tc

    def body(vp, vc, vn, gp, gc, gn, wv, wg, bv, bg, da_ref, dv_ref, dg_ref, ext_v, ext_g):
        i = pl.program_id(0)
        _fill_ext(ext_v, vp[...], vc[...], vn[...], i, h, tm)
        _fill_ext(ext_g, gp[...], gc[...], gn[...], i, h, tm)
        for r0, cols in _chunks(tm, tc):
            rows = slice(r0, r0 + ROW_CHUNK)
            val = _corr(ext_v, wv, k, h + r0, ROW_CHUNK, False, cols) + bv[:, cols]
            gate = _corr(ext_g, wg, k, h + r0, ROW_CHUNK, False, cols) + bg[:, cols]
            _, vjp = jax.vjp(lambda a, b: _silu(b) * a, val, gate)
            dval, dgate = vjp(da_ref[rows, cols])
            dv_ref[rows, cols] = dval
            dg_ref[rows, cols] = dgate

    wspec = lambda s: pl.BlockSpec((k, tc), lambda i, j: (0, j + s))
    bspec = lambda s: pl.BlockSpec((1, tc), lambda i, j: (0, j + s))
    dc = pl.pallas_call(
        body, name=name, grid=(t // tm, nj),
        in_specs=_halo_specs(tc, lambda j: j, h, tm) + _halo_specs(tc, lambda j: j + nj, h, tm)
        + [wspec(0), wspec(nj), bspec(0), bspec(nj), pl.BlockSpec((tm, tc), lambda i, j: (i, j))],
        out_specs=[pl.BlockSpec((tm, tc), lambda i, j: (i, j)), pl.BlockSpec((tm, tc), lambda i, j: (i, j))],
        out_shape=[jax.ShapeDtypeStruct((t, D_FF), F32), jax.ShapeDtypeStruct((t, D_FF), F32)],
        scratch_shapes=[pltpu.VMEM((tm + 2 * h, tc), F32), pltpu.VMEM((tm + 2 * h, tc), F32)],
        compiler_params=_params(("parallel", "parallel")))(u, u, u, u, u, u, dw_w, dw_w, dw_b, dw_b, da)
    return dc


def _dwconv_bwd(dcv, dcg, u, dw_w, name):
    t = u.shape[0]
    tm, k, tc = _tm(), FFN_K, _ffn_tc()
    h = _halo(k)
    pad = k // 2
    nj = D_FF // tc

    def body(vp, vc, vn, gp, gc, gn, up, uc, un, w_ref, du_ref, dw_ref, dbias_ref, ext_d, ext_u):
        jj, i = pl.program_id(0), pl.program_id(1)

        @pl.when(i == 0)
        def _():
            dw_ref[...] = jnp.zeros_like(dw_ref)
            dbias_ref[...] = jnp.zeros_like(dbias_ref)

        @pl.when(jj < nj)
        def _():
            _fill_ext(ext_d, vp[...], vc[...], vn[...], i, h, tm)

        @pl.when(jj >= nj)
        def _():
            _fill_ext(ext_d, gp[...], gc[...], gn[...], i, h, tm)

        _fill_ext(ext_u, up[...], uc[...], un[...], i, h, tm)
        for c0 in range(0, tc, LANE_CHUNK):
            cols = slice(c0, min(c0 + LANE_CHUNK, tc))
            width = cols.stop - cols.start
            acc_b = jnp.zeros((ROW_CHUNK, width), F32)
            acc_w = [jnp.zeros((ROW_CHUNK, width), F32) for _ in range(k)]
            for r0 in range(0, tm, ROW_CHUNK):
                d = ext_d[h + r0:h + r0 + ROW_CHUNK, cols]
                du_ref[r0:r0 + ROW_CHUNK, cols] = _corr(ext_d, w_ref, k, h + r0, ROW_CHUNK, True, cols).astype(BF16)
                acc_b = acc_b + d
                for kk in range(k):
                    o = h + r0 + kk - pad
                    acc_w[kk] = acc_w[kk] + d * ext_u[o:o + ROW_CHUNK, cols]
            dbias_ref[:, cols] += jnp.sum(acc_b, axis=0, keepdims=True)
            for kk in range(k):
                dw_ref[kk:kk + 1, cols] += jnp.sum(acc_w[kk], axis=0, keepdims=True)

    def hs(cb, live):
        r = tm // h
        row = lambda j, i: jnp.where(live(j), i, 0)
        return [pl.BlockSpec((h, tc), lambda j, i: (jnp.maximum(row(j, i) * r - 1, 0), cb(j))),
                pl.BlockSpec((tm, tc), lambda j, i: (row(j, i), cb(j))),
                pl.BlockSpec((h, tc), lambda j, i: (jnp.minimum((row(j, i) + 1) * r, (_t_rows() // h) - 1), cb(j)))]

    return pl.pallas_call(
        body, name=name, grid=(2 * nj, t // tm),
        in_specs=hs(lambda j: jnp.minimum(j, nj - 1), lambda j: j < nj)
        + hs(lambda j: jnp.maximum(j - nj, 0), lambda j: j >= nj)
        + hs(lambda j: j, lambda j: True) + [pl.BlockSpec((k, tc), lambda j, i: (0, j))],
        out_specs=[pl.BlockSpec((tm, tc), lambda j, i: (i, j)), pl.BlockSpec((k, tc), lambda j, i: (0, j)),
                   pl.BlockSpec((1, tc), lambda j, i: (0, j))],
        out_shape=[jax.ShapeDtypeStruct((t, 2 * D_FF), BF16), jax.ShapeDtypeStruct((k, 2 * D_FF), F32),
                   jax.ShapeDtypeStruct((1, 2 * D_FF), F32)],
        scratch_shapes=[pltpu.VMEM((tm + 2 * h, tc), F32), pltpu.VMEM((tm + 2 * h, tc), F32)],
        compiler_params=_params(("parallel", "arbitrary")))(dcv, dcv, dcv, dcg, dcg, dcg, u, u, u, dw_w)


def _na_geometry(rq):
    ncb = CTX_LEN // GRID_W
    rows_n = SEQ // GRID_W
    r = jnp.maximum(rq - ncb, 0)
    kstart = jnp.clip(r - NA_ROWS // 2, 0, rows_n - NA_ROWS)
    base = kstart - r + NA_ROWS - 1
    return rq >= ncb, kstart, base


def _na_core(q, kl, vl, kc, vc, bias, mask):
    qs = q * (NA_DH ** -0.5)
    s_l = jnp.where(mask, dot_nt(qs, kl) + bias, NEG)
    s_c = dot_nt(qs, kc)
    m = lax.stop_gradient(jnp.maximum(jnp.max(s_l, axis=1, keepdims=True), jnp.max(s_c, axis=1, keepdims=True)))
    e_l, e_c = jnp.exp(s_l - m), jnp.exp(s_c - m)
    inv = 1.0 / (jnp.sum(e_l, axis=1, keepdims=True) + jnp.sum(e_c, axis=1, keepdims=True))
    return dot_nn(e_l * inv, vl) + dot_nn(e_c * inv, vc)


def _na_mask(is_lat):
    nl = NA_ROWS * GRID_W
    q = lax.broadcasted_iota(jnp.int32, (GRID_W, nl), 0)
    w = lax.broadcasted_iota(jnp.int32, (GRID_W, nl), 1) % GRID_W
    cs = jnp.clip(q - NA_COLS // 2, 0, GRID_W - NA_COLS)
    return (w >= cs) & (w < cs + NA_COLS) & is_lat


def _na_bias(rb_ref):
    assert 2 * GRID_W == LANES
    lane = lax.broadcasted_iota(jnp.int32, (GRID_W, LANES), 1)
    tiles = []
    for kp in range(NA_ROWS // 2):
        ev = jnp.broadcast_to(rb_ref[2 * kp:2 * kp + 1, :], (GRID_W, LANES))
        od = jnp.broadcast_to(rb_ref[2 * kp + 1:2 * kp + 2, :], (GRID_W, LANES))
        ev = pltpu.roll(ev, LANES - (NA_COLS - 1), 1, stride=1, stride_axis=0)
        od = pltpu.roll(od, LANES - (NA_COLS - 1) - GRID_W, 1, stride=1, stride_axis=0)
        tiles.append(jnp.where(lane < GRID_W, ev, od))
    return jnp.concatenate(tiles, axis=1)


def _na_dbias(dbias, drb_ref):
    qi = lax.broadcasted_iota(jnp.int32, (GRID_W, GRID_W), 0)
    qj = lax.broadcasted_iota(jnp.int32, (GRID_W, GRID_W), 1)
    flip = (qi + qj == GRID_W - 1).astype(F32)
    rev = lax.dot_general(flip, dbias, (((1,), (0,)), ((), ())), precision=lax.Precision.HIGHEST,
                          preferred_element_type=F32)
    lane = lax.broadcasted_iota(jnp.int32, (GRID_W, LANES), 1)
    s_ev = LANES - (GRID_W - NA_COLS)
    for kp in range(NA_ROWS // 2):
        tile = rev[:, kp * LANES:(kp + 1) * LANES]
        ev = pltpu.roll(jnp.where(lane < GRID_W, tile, 0.0), s_ev, 1, stride=1, stride_axis=0)
        od = pltpu.roll(jnp.where(lane >= GRID_W, tile, 0.0), s_ev - GRID_W, 1, stride=1, stride_axis=0)
        drb_ref[2 * kp:2 * kp + 1, :] += jnp.sum(ev, axis=0, keepdims=True)
        drb_ref[2 * kp + 1:2 * kp + 2, :] += jnp.sum(od, axis=0, keepdims=True)


def _na_hps():
    return 2 if NA_HEADS % 2 == 0 else 1


def _na_specs(p_offsets):
    t = _t_rows()
    hps = _na_hps()
    wd = hps * NA_DH
    assert all(p_offsets[n] % wd == 0 for n in ("nq", "nk", "nv"))
    qb, kb, vb = (p_offsets[n] // wd for n in ("nq", "nk", "nv"))
    return [pl.BlockSpec((GRID_W, wd), lambda h, r: (r, qb + h)),
            pl.BlockSpec((t, wd), lambda h, r: (0, kb + h)),
            pl.BlockSpec((t, wd), lambda h, r: (0, vb + h)),
            pl.BlockSpec((hps, None, NA_ROWS, LANES), lambda h, r: (h, _na_geometry(r)[2], 0, 0))]


def _na_fwd(p, rb, name):
    t = p.shape[0]
    dh, nl = NA_DH, NA_ROWS * GRID_W

    hps = _na_hps()

    def body(q_ref, k_ref, v_ref, rb_ref, out_ref):
        rq = pl.program_id(1)
        is_lat, kstart, _ = _na_geometry(rq)
        start = pl.multiple_of(CTX_LEN + kstart * GRID_W, GRID_W)
        mask = _na_mask(is_lat)
        for hh in range(hps):
            cols = slice(hh * dh, (hh + 1) * dh)
            out = _na_core(q_ref[:, cols], k_ref[pl.ds(start, nl), cols], v_ref[pl.ds(start, nl), cols],
                           k_ref[0:CTX_LEN, cols], v_ref[0:CTX_LEN, cols], _na_bias(rb_ref.at[hh]), mask)
            out_ref[:, cols] = out.astype(BF16)

    return _call(
        body, name=name, grid=(NA_HEADS // hps, t // GRID_W), in_specs=_na_specs(_offsets()),
        out_specs=[pl.BlockSpec((GRID_W, hps * dh), lambda h, r: (r, h))],
        out_shape=[jax.ShapeDtypeStruct((t, _na_w()), BF16)],
        sem=("parallel", "arbitrary"), args=(p, p, p, rb))[0]


def _na_bwd(p, rb, dmix, name):
    t = p.shape[0]
    dh, nl = NA_DH, NA_ROWS * GRID_W

    hps = _na_hps()
    wd = hps * dh
    assert ((_ret_w() + CONV_W) // dh) % hps == 0
    ob = (_ret_w() + CONV_W) // wd

    def body(q_ref, k_ref, v_ref, rb_ref, dy_ref, dq_ref, dk_ref, dv_ref, drb_ref):
        rq = pl.program_id(1)
        is_lat, kstart, base = _na_geometry(rq)
        _, _, prev_base = _na_geometry(rq - 1)
        start = pl.multiple_of(CTX_LEN + kstart * GRID_W, GRID_W)

        @pl.when(rq == 0)
        def _():
            dk_ref[...] = jnp.zeros_like(dk_ref)
            dv_ref[...] = jnp.zeros_like(dv_ref)

        @pl.when((rq == 0) | (base != prev_base))
        def _():
            drb_ref[...] = jnp.zeros_like(drb_ref)

        mask = _na_mask(is_lat)
        for hh in range(hps):
            cols = slice(hh * dh, (hh + 1) * dh)
            _, vjp = jax.vjp(lambda *a: _na_core(*a, mask), q_ref[:, cols], k_ref[pl.ds(start, nl), cols],
                             v_ref[pl.ds(start, nl), cols], k_ref[0:CTX_LEN, cols], v_ref[0:CTX_LEN, cols],
                             _na_bias(rb_ref.at[hh]))
            dq, dkl, dvl, dkc, dvc, dbias = vjp(dy_ref[:, cols])
            dq_ref[:, cols] = dq
            dk_ref[pl.ds(start, nl), cols] += dkl
            dv_ref[pl.ds(start, nl), cols] += dvl
            dk_ref[0:CTX_LEN, cols] += dkc
            dv_ref[0:CTX_LEN, cols] += dvc
            _na_dbias(dbias, drb_ref.at[hh])

    return _call(
        body, name=name, grid=(NA_HEADS // hps, t // GRID_W),
        in_specs=_na_specs(_offsets()) + [pl.BlockSpec((GRID_W, wd), lambda h, r: (r, ob + h))],
        out_specs=[pl.BlockSpec((GRID_W, wd), lambda h, r: (r, h)), pl.BlockSpec((t, wd), lambda h, r: (0, h)),
                   pl.BlockSpec((t, wd), lambda h, r: (0, h)),
                   pl.BlockSpec((hps, None, NA_ROWS, LANES), lambda h, r: (h, _na_geometry(r)[2], 0, 0))],
        out_shape=[jax.ShapeDtypeStruct((t, _na_w()), F32), jax.ShapeDtypeStruct((t, _na_w()), F32),
                   jax.ShapeDtypeStruct((t, _na_w()), F32),
                   jax.ShapeDtypeStruct((NA_HEADS, NA_ROWS, NA_ROWS, LANES), F32)],
        sem=("parallel", "arbitrary"), args=(p, p, p, rb, dmix))


def _rpb_select():
    sel = np.zeros((2 * NA_ROWS - 1, NA_ROWS * NA_ROWS), np.float32)
    for b in range(NA_ROWS):
        for kh in range(NA_ROWS):
            sel[b + kh, b * NA_ROWS + kh] = 1.0
    return jnp.asarray(sel)


def _rpb_rows(rpb):
    pad = jnp.pad(rpb, ((0, 0), (0, 0), (0, LANES - (2 * NA_COLS - 1))))
    rows = jnp.einsum("rk,hrc->hkc", _rpb_select(), pad, precision=lax.Precision.HIGHEST)
    return rows.reshape(NA_HEADS, NA_ROWS, NA_ROWS, LANES)


def _rpb_rows_t(drb):
    flat = drb.reshape(NA_HEADS, NA_ROWS * NA_ROWS, LANES)
    out = jnp.einsum("rk,hkc->hrc", _rpb_select(), flat, precision=lax.Precision.HIGHEST)
    return out[:, :, :2 * NA_COLS - 1]


def _assemble_dp(dqr, dkr, dvr, dgate, da, db, dnq, dnk, dnv, name):
    t = dgate.shape[0]
    tm = _tm()
    off = _offsets()
    sizes = dict(q=_ret_qk_w(), k=_ret_qk_w(), v=_ret_w(), g=_ret_w(), a=CONV_W, b=CONV_W, nq=_na_w(), nk=_na_w(), nv=_na_w())

    def body(q_ref, k_ref, v_ref, g_ref, a_ref, b_ref, nq_ref, nk_ref, nv_ref, o_ref):
        def put(n, val):
            o_ref[:, off[n]:off[n] + sizes[n]] = val.astype(BF16)

        put("q", q_ref[0] + q_ref[1])
        put("k", k_ref[0] + k_ref[1])
        put("v", v_ref[0] + v_ref[1])
        put("g", g_ref[...])
        put("a", a_ref[...])
        put("b", b_ref[...])
        put("nq", nq_ref[...])
        put("nk", nk_ref[...])
        put("nv", nv_ref[...])

    two = lambda w: pl.BlockSpec((2, tm, w), lambda i: (0, i, 0))
    one = lambda w: pl.BlockSpec((tm, w), lambda i: (i, 0))
    return pl.pallas_call(
        body, name=name, grid=(t // tm,),
        in_specs=[two(sizes["q"]), two(sizes["k"]), two(sizes["v"]), one(sizes["g"]), one(CONV_W), one(CONV_W),
                  one(_na_w()), one(_na_w()), one(_na_w())],
        out_specs=one(_d_in()), out_shape=jax.ShapeDtypeStruct((t, _d_in()), BF16),
        compiler_params=_params(("parallel",)))(dqr, dkr, dvr, dgate, da, db, dnq, dnk, dnv)


def _adamw(w, m, v, gs, name):
    nl, r, c = w.shape
    stacked = not isinstance(gs, (list, tuple))
    if stacked:
        gs = [gs]
    assert stacked or len(gs) == nl
    g_n = gs[0].shape[-3]
    block_bytes = 2 * 1024 * 1024
    rows = min(block_bytes // (4 * c), block_bytes // (g_n * c * gs[0].dtype.itemsize))
    tr = _tile(r, max(2 * SUBLANES, rows // (2 * SUBLANES) * (2 * SUBLANES)), 2 * SUBLANES)
    nt = r // tr
    c1 = 1.0 - ADAM_B1 ** ADAM_STEP
    c2 = 1.0 - ADAM_B2 ** ADAM_STEP

    def body(w_ref, m_ref, v_ref, *rest):
        g_refs, (go_ref, d_ref, mo_ref, vo_ref) = rest[:len(gs)], rest[len(gs):]
        layer = pl.program_id(0)
        for ll in range(len(gs)):
            @pl.when(jnp.logical_or(stacked, layer == ll))
            def _():
                g_ref = g_refs[ll]
                g = g_ref[0].astype(F32)
                for j in range(1, g_n):
                    g = g + g_ref[j].astype(F32)
                mn = ADAM_B1 * m_ref[...] + (1.0 - ADAM_B1) * g
                vn = ADAM_B2 * v_ref[...] + (1.0 - ADAM_B2) * (g * g)
                m_hat = mn / c1
                v_hat = vn / c2
                go_ref[...] = g
                d_ref[...] = -ADAM_LR * (m_hat / (jnp.sqrt(v_hat) + ADAM_EPS) + ADAM_WD * w_ref[...])
                mo_ref[...] = mn
                vo_ref[...] = vn

    def g_spec(ll):
        if stacked:
            return pl.BlockSpec((None, g_n, tr, c), lambda l, i: (l, 0, i, 0))
        return pl.BlockSpec((g_n, tr, c), lambda l, i: (0, jnp.where(l == ll, i, jnp.where(l < ll, 0, nt - 1)), 0))

    blk = pl.BlockSpec((None, tr, c), lambda l, i: (l, i, 0))
    sds = jax.ShapeDtypeStruct((nl, r, c), F32)
    return _call(
        body, name=name, grid=(nl, nt),
        in_specs=[blk, blk, blk] + [g_spec(ll) for ll in range(len(gs))],
        out_specs=[blk, blk, blk, blk], out_shape=[sds, sds, sds, sds],
        sem=("arbitrary", "arbitrary"), args=(w, m, v, *gs))


def _sum_devices(g, name):
    _, r, c = g.shape
    tr = _tile(r, 512, SUBLANES)

    def body(g_ref, o_ref):
        acc = g_ref[0]
        for j in range(1, N_DEV):
            acc = acc + g_ref[j]
        o_ref[...] = acc

    return pl.pallas_call(body, name=name, grid=(r // tr,), in_specs=[pl.BlockSpec((N_DEV, tr, c), lambda i: (0, i, 0))],
                          out_specs=pl.BlockSpec((tr, c), lambda i: (i, 0)), out_shape=jax.ShapeDtypeStruct((r, c), F32),
                          compiler_params=_params(("parallel",)))(g)


def _ada_fwd(c16, w_ada, b_shard, name):
    nl, d, cs = w_ada.shape
    tk = _tile(d, 512, LANES)
    nk = d // tk

    def body(c_ref, w_ref, b_ref, o_ref):
        kk = pl.program_id(1)

        @pl.when(kk == 0)
        def _():
            o_ref[...] = jnp.broadcast_to(b_ref[...], o_ref.shape)

        o_ref[...] += _dg(_silu(c_ref[...]), w_ref[...], 1, 0)

    return pl.pallas_call(
        body, name=name, grid=(nl, nk),
        in_specs=[pl.BlockSpec((16, tk), lambda l, kk: (0, kk)), pl.BlockSpec((None, tk, cs), lambda l, kk: (l, kk, 0)),
                  pl.BlockSpec((None, 1, cs), lambda l, kk: (l, 0, 0))],
        out_specs=pl.BlockSpec((None, 16, cs), lambda l, kk: (l, 0, 0)),
        out_shape=jax.ShapeDtypeStruct((nl, 16, cs), F32),
        compiler_params=_params(("parallel", "arbitrary")))(c16, w_ada, b_shard)


def _ada_bwd(c16, dm16, w_ada, name):
    nl, d, cs = w_ada.shape
    td = _tile(d, 512, LANES)

    def body(c_ref, dm_ref, w_ref, gw_ref, dc_ref):
        cv = c_ref[...]
        s, vjp = jax.vjp(_silu, cv)
        gw_ref[...] = _dg(s, dm_ref[...], 0, 0)
        ds = _dg(dm_ref[...], w_ref[...], 1, 1)
        dc_ref[...] = vjp(ds)[0]

    return pl.pallas_call(
        body, name=name, grid=(nl, d // td),
        in_specs=[pl.BlockSpec((16, td), lambda l, i: (0, i)), pl.BlockSpec((None, 16, cs), lambda l, i: (l, 0, 0)),
                  pl.BlockSpec((None, td, cs), lambda l, i: (l, i, 0))],
        out_specs=[pl.BlockSpec((None, td, cs), lambda l, i: (l, i, 0)), pl.BlockSpec((None, 16, td), lambda l, i: (l, 0, i))],
        out_shape=[jax.ShapeDtypeStruct((nl, d, cs), F32), jax.ShapeDtypeStruct((nl, 16, d), F32)],
        compiler_params=_params(("parallel", "parallel")))(c16, dm16, w_ada)


def _pack_rows(shape):
    n = int(np.prod(shape))
    return SUBLANES * (-(-n // (LANES * SUBLANES)))


def _pack(arrays, row_align):
    parts, total = [], 0
    for a in arrays:
        flat = a.reshape(-1).astype(F32)
        rows = _pack_rows(a.shape)
        total += rows
        parts += [flat, jnp.zeros((rows * LANES - flat.shape[0],), F32)]
    parts.append(jnp.zeros(((-total % row_align) * LANES,), F32))
    return jnp.concatenate([p for p in parts if p.shape[0]]).reshape(-1, LANES)


def _unpack(packed, shapes):
    out, r = [], 0
    for s in shapes:
        rows = _pack_rows(s)
        out.append(packed[r:r + rows].reshape(-1)[:int(np.prod(s))].reshape(s))
        r += rows
    return out


def _rope_tables():
    half, nf = RET_DK // 2, RET_DK // 4
    pos = jnp.arange(SEQ)
    row = (pos // GRID_W).astype(F32)
    col = (pos % GRID_W).astype(F32)
    inv = ROPE_BASE ** (-jnp.arange(nf, dtype=F32) / nf)
    ar, ac = row[:, None] * inv[None, :], col[:, None] * inv[None, :]
    cos = jnp.concatenate([jnp.cos(ar), jnp.cos(ar), jnp.cos(ac), jnp.cos(ac)], axis=-1)
    sin = jnp.concatenate([-jnp.sin(ar), jnp.sin(ar), -jnp.sin(ac), jnp.sin(ac)], axis=-1)
    cos = jnp.concatenate([jnp.ones((CTX_LEN, RET_DK), F32), cos], axis=0)
    sin = jnp.concatenate([jnp.zeros((CTX_LEN, RET_DK), F32), sin], axis=0)
    return cos, sin


def _layer_fwd(l, x, mod4, w, cst, arrived):
    n = lambda s: f"l{l}_{s}"
    d = D_MODEL
    h1, h1_t = _normmod_fwd(x, w["norm1_g"], mod4, 0, n("norm1"))
    w["w_in"] = _cols_from_shards(arrived("w_in", h1), n("w_in_cols"))
    p = _mm(h1, w["w_in"], n("proj_in"))
    o2, states = _ret_fwd(p, cst["cos"], cst["sin"], w["ret_decay"], cst["order"], n("ret_fwd"))
    ret_out = _ggn_fwd(o2, p, w["ret_gn_g"], n("ret_gn"))
    u2, conv_out = _conv_fwd(p, w["conv_dw_w"], w["conv_dw_b"], w["conv_ln_g"], w["conv_ln_b"], w["conv_pw"], n("conv_fwd"))
    na_out = _na_fwd(p, w["rb"], n("na_fwd"))
    mix = jnp.concatenate([ret_out, conv_out, na_out], axis=1)
    w["w_out"] = arrived("w_out", mix).reshape(_d_mix(), d)
    g1 = _mm(mix, w["w_out"], n("proj_out"))
    x1 = _gate_res_fwd(x, g1, mod4, 2, n("res1"))
    h2, h2_t = _normmod_fwd(x1, w["norm2_g"], mod4, 1, n("norm2"))
    w["ffn_up"] = arrived("ffn_up", h2)
    u = _mm(h2, w["ffn_up"], n("ffn_up"), b3=True)
    a, a_t = _ffn_act_fwd(u, w["ffn_dw_w"], w["ffn_dw_b"], n("ffn_act"))
    w["ffn_down"] = arrived("ffn_down", a).reshape(D_FF, d)
    f = _mm(a, w["ffn_down"], n("ffn_down"))
    x2 = _gate_res_fwd(x1, f, mod4, 5, n("res2"))
    saved = dict(x=x, h1_t=h1_t, p=p, o2=o2, states=states, u2=u2, mix=mix, g1=g1, x1=x1, h2_t=h2_t, u=u, a_t=a_t, f=f)
    return x2, saved


def _layer_bwd(l, dx2, s, mod4, w, cst, send):
    n = lambda t: f"l{l}_{t}"
    d = D_MODEL
    dfg, dg2 = _gate_res_bwd(dx2, s["f"], mod4, 5, n("res2_bwd"))
    da = _mm(dfg, w["ffn_down"], n("ffn_down_dx"), tb=True)
    d_ffn_down = _mm(s["a_t"], dfg, n("ffn_down_dw"), out_dtype=BF16, tm_max=DW_TM)
    tok = send(("ffn_down", l), d_ffn_down.reshape(N_DEV, D_FF // N_DEV, d))
    dcv, dcg = _ffn_act_bwd1(s["u"], da, w["ffn_dw_w"], _after(w["ffn_dw_b"], tok), n("ffn_act_bwd"))
    du, d_ffn_dw_w, d_ffn_dw_b = _dwconv_bwd(dcv, dcg, s["u"], w["ffn_dw_w"], n("ffn_dw_bwd"))
    d_ffn_dw_b = d_ffn_dw_b[0]
    dh2 = _mm(du, w["ffn_up"], n("ffn_up_dx"), tb=True, b3=True)
    d_ffn_up = _mm(s["h2_t"], du, n("ffn_up_dw"), out_dtype=BF16, tm_max=DW_TM,
                      o_cs=2 * D_FF // N_DEV)
    tok = send(("ffn_up", l), d_ffn_up)
    (dx1, dn2, dsh2, dsc2) = _normmod_bwd(s["x1"], _after(w["norm2_g"], tok), mod4, 1, dh2, dx2, n("norm2_bwd"))
    dgg, dg1 = _gate_res_bwd(dx1, s["g1"], mod4, 2, n("res1_bwd"))
    dmix = _mm(dgg, w["w_out"], n("proj_out_dx"), tb=True)
    d_w_out = _mm(_transpose_bf16(s["mix"], n("mix_t")), dgg, n("proj_out_dw"), out_dtype=BF16, tm_max=DW_TM)
    tok = send(("w_out", l), d_w_out.reshape(N_DEV, _d_mix() // N_DEV, d))
    do, dgate, dgn = _ggn_bwd(s["o2"], s["p"], _after(w["ret_gn_g"], tok), dmix, n("ret_gn_bwd"))
    dqr, dkr, dvr, ddec = _ret_bwd(s["p"], cst["cos"], cst["sin"], w["ret_decay"], cst["order"], s["states"], do, n("ret_bwd"))
    du2, dlng, dlnb, dpw = _conv_bwd1(s["u2"], dmix, w["conv_ln_g"], w["conv_ln_b"], w["conv_pw"], n("conv_bwd1"))
    dca, dcb, ddww, ddwb = _conv_bwd2(du2, s["p"], w["conv_dw_w"], n("conv_bwd2"))
    dnq, dnk, dnv, drb = _na_bwd(s["p"], w["rb"], dmix, n("na_bwd"))
    dp = _assemble_dp(dqr, dkr, dvr, dgate, dca, dcb, dnq, dnk, dnv, n("dproj"))
    h1_t = s["h1_t"]
    half = d // 2
    for i in range(2):
        d_w_in = _mm(h1_t[i * half:(i + 1) * half], dp, n(f"proj_in_dw{i}"), out_dtype=BF16, tm_max=DW_TM,
                        o_cs=_d_in() // N_DEV)
        tok = send(("w_in", l, i), d_w_in)
    dh1 = _mm(dp, w["w_in"], n("proj_in_dx"), tb=True, after=tok)
    (dx, dn1, dsh1, dsc1) = _normmod_bwd(s["x"], _after(w["norm1_g"], tok), mod4, 0, dh1, dx1, n("norm1_bwd"))
    dmod = jnp.concatenate([dsh1, dsc1, dg1, dsh2, dsc2, dg2], axis=1)
    small = dict(norm1_g=dn1[0], ret_decay=ddec[:, :, 0, 0], ret_gn_g=dgn[0], conv_dw_w=ddww, conv_dw_b=ddwb[0],
                 conv_ln_g=dlng[0], conv_ln_b=dlnb[0], conv_pw=dpw, na_rpb=_rpb_rows_t(drb), norm2_g=dn2[0],
                 ffn_dw_w=d_ffn_dw_w, ffn_dw_b=d_ffn_dw_b)
    return dx, dmod, small


def _d_mix():
    return _ret_w() + CONV_W + _na_w()


_SMALL = ["c_ctx", "b_ada", "norm1_g", "ret_decay", "ret_gn_g", "conv_dw_w", "conv_dw_b", "conv_ln_g", "conv_ln_b",
          "conv_pw", "na_rpb", "norm2_g", "ffn_dw_w", "ffn_dw_b", "final_g"]
_SMALL_SHARD_AXIS = {"conv_dw_w": 2, "conv_pw": 1, "ffn_dw_w": 2}


def kernel(x, c, ctx, c_ctx, w_ada, b_ada, norm1_g, w_in, ret_decay, ret_gn_g, conv_dw_w, conv_dw_b, conv_ln_g, conv_ln_b, conv_pw, na_rpb, w_out, norm2_g, ffn_up, ffn_dw_w, ffn_dw_b, ffn_down, final_g, loss_target, m_c_ctx, m_w_ada, m_b_ada, m_norm1_g, m_w_in, m_ret_decay, m_ret_gn_g, m_conv_dw_w, m_conv_dw_b, m_conv_ln_g, m_conv_ln_b, m_conv_pw, m_na_rpb, m_w_out, m_norm2_g, m_ffn_up, m_ffn_dw_w, m_ffn_dw_b, m_ffn_down, m_final_g, v_c_ctx, v_w_ada, v_b_ada, v_norm1_g, v_w_in, v_ret_decay, v_ret_gn_g, v_conv_dw_w, v_conv_dw_b, v_conv_ln_g, v_conv_ln_b, v_conv_pw, v_na_rpb, v_w_out, v_norm2_g, v_ffn_up, v_ffn_dw_w, v_ffn_dw_b, v_ffn_down, v_final_g):
    d, nl = D_MODEL, DEPTH
    cs = 6 * d // N_DEV
    me = _my_index()
    weights = dict(c_ctx=c_ctx, w_ada=w_ada, b_ada=b_ada, norm1_g=norm1_g, w_in=w_in, ret_decay=ret_decay, ret_gn_g=ret_gn_g,
                   conv_dw_w=conv_dw_w, conv_dw_b=conv_dw_b, conv_ln_g=conv_ln_g, conv_ln_b=conv_ln_b, conv_pw=conv_pw,
                   na_rpb=na_rpb, w_out=w_out, norm2_g=norm2_g, ffn_up=ffn_up, ffn_dw_w=ffn_dw_w, ffn_dw_b=ffn_dw_b,
                   ffn_down=ffn_down, final_g=final_g)
    mom = dict(c_ctx=m_c_ctx, w_ada=m_w_ada, b_ada=m_b_ada, norm1_g=m_norm1_g, w_in=m_w_in, ret_decay=m_ret_decay,
               ret_gn_g=m_ret_gn_g, conv_dw_w=m_conv_dw_w, conv_dw_b=m_conv_dw_b, conv_ln_g=m_conv_ln_g,
               conv_ln_b=m_conv_ln_b, conv_pw=m_conv_pw, na_rpb=m_na_rpb, w_out=m_w_out, norm2_g=m_norm2_g,
               ffn_up=m_ffn_up, ffn_dw_w=m_ffn_dw_w, ffn_dw_b=m_ffn_dw_b, ffn_down=m_ffn_down, final_g=m_final_g)
    var = dict(c_ctx=v_c_ctx, w_ada=v_w_ada, b_ada=v_b_ada, norm1_g=v_norm1_g, w_in=v_w_in, ret_decay=v_ret_decay,
               ret_gn_g=v_ret_gn_g, conv_dw_w=v_conv_dw_w, conv_dw_b=v_conv_dw_b, conv_ln_g=v_conv_ln_g,
               conv_ln_b=v_conv_ln_b, conv_pw=v_conv_pw, na_rpb=v_na_rpb, w_out=v_w_out, norm2_g=v_norm2_g,
               ffn_up=v_ffn_up, ffn_dw_w=v_ffn_dw_w, ffn_dw_b=v_ffn_dw_b, ffn_down=v_ffn_down, final_g=v_final_g)

    big_names = ["w_in", "w_out", "ffn_up", "ffn_down"]
    shards = {(nm, l): _cast_bf16(weights[nm][l], f"cast_{nm}{l}") for l in range(nl) for nm in big_names}
    small_sharded = _pack([conv_dw_w, conv_pw, ffn_dw_w], SUBLANES)
    c_rows = jnp.pad(c, ((0, SUBLANES - 1), (0, 0)))
    gathered = _run_comm(_Gather([c_rows, small_sharded, shards[("w_in", 0)]]), "gather_first")
    c_all = gathered[0][:, 0, :]
    def whole(rows, shard_shape, axis):
        n_el = int(np.prod(shard_shape))
        parts = rows.reshape(N_DEV, -1)[:, :n_el].reshape((N_DEV,) + tuple(shard_shape))
        parts = jnp.moveaxis(parts, 0, axis)
        return parts.reshape(shard_shape[:axis] + (N_DEV * shard_shape[axis],) + shard_shape[axis + 1:])

    r0 = _pack_rows(conv_dw_w.shape)
    r1 = r0 + _pack_rows(conv_pw.shape)
    r2 = r1 + _pack_rows(ffn_dw_w.shape)
    full_conv_dw_w = whole(gathered[1][:, :r0], conv_dw_w.shape, 2)
    full_conv_pw = whole(gathered[1][:, r0:r1], conv_pw.shape, 1)
    full_ffn_dw_w = whole(gathered[1][:, r1:r2], ffn_dw_w.shape, 2)

    c16 = jnp.concatenate([c_all, jnp.broadcast_to(c_ctx[None, :], (N_DEV, d))], axis=0)
    b_shard = lax.dynamic_slice_in_dim(b_ada, me * cs, cs, axis=1)[:, None, :]
    m_shard = _ada_fwd(c16, w_ada, b_shard, "ada_fwd")
    m_all = _run_comm(_Gather([m_shard.reshape(nl * 16, cs)]), "gather_mod")[0]
    m_full = m_all.reshape(N_DEV, nl, 16, cs).transpose(1, 2, 0, 3).reshape(nl, 16, 6 * d)
    m_lat = lax.dynamic_index_in_dim(m_full, me, axis=1, keepdims=False)
    mod = jnp.stack([m_full[:, N_DEV], m_lat], axis=1).reshape(nl, 2, 6, 1, d)

    arriving, token = {}, m_all
    for l in range(nl):
        for nm in big_names:
            if (nm, l) != ("w_in", 0):
                arriving[(nm, l)], token = _split_start(shards[(nm, l)], True, f"gather_{nm}{l}", token)
    mod = _after(mod, token)

    cos, sin = _rope_tables()
    cst = dict(cos=cos, sin=sin, order=_chunk_order())
    layer_w = []
    for l in range(nl):
        layer_w.append(dict(
            norm1_g=norm1_g[l][None], norm2_g=norm2_g[l][None], ret_decay=ret_decay[l], ret_gn_g=ret_gn_g[l][None],
            conv_dw_w=full_conv_dw_w[l], conv_dw_b=conv_dw_b[l][None], conv_ln_g=conv_ln_g[l][None],
            conv_ln_b=conv_ln_b[l][None], conv_pw=full_conv_pw[l], rb=_rpb_rows(na_rpb[l]),
            ffn_dw_w=full_ffn_dw_w[l], ffn_dw_b=ffn_dw_b[l][None]))

    xs = jnp.concatenate([ctx[0], x[0]], axis=0)
    saved = []
    for l in range(nl):
        def arrived(nm, after, l=l):
            if (nm, l) == ("w_in", 0):
                return gathered[2]
            return _split_wait(arriving[(nm, l)], after, f"arrived_{nm}{l}")

        xs, sv = _layer_fwd(l, xs, mod[l], layer_w[l], cst, arrived)
        saved.append(sv)
    loss_tile, dxs, dfinal = _loss_head(xs, final_g[None], loss_target[0], "loss_head")
    loss = lax.psum(loss_tile[0, 0], ("x", "y", "c"))

    dmods, smalls = [None] * nl, [None] * nl
    leaving, last = {}, [loss_tile]

    def send(key, partial):
        leaving[key], token = _split_start(partial, False, "send_" + "_".join(str(k) for k in key), last[0])
        last[0] = token
        return token

    per_layer = [nm for nm in _SMALL if nm not in ("c_ctx", "b_ada", "final_g")]
    small_packs, small_arriving = [None] * nl, [None] * nl
    for l in reversed(range(nl)):
        dxs, dmods[l], smalls[l] = _layer_bwd(l, dxs, saved[l], mod[l], layer_w[l], cst, send)
        small_packs[l] = _pack([smalls[l][nm] for nm in per_layer], 512)
        if l > 0:
            small_arriving[l], last[0] = _split_start(small_packs[l], True, f"gather_small_grads{l}", last[0])
    grad_x = dxs[CTX_LEN:][None]

    arrived_grad = lambda key, after: _split_wait(leaving[key], after, "got_" + "_".join(str(k) for k in key))
    out_big = {}
    after = dxs
    for nm in ["ffn_down", "ffn_up", "w_out"]:
        out_big[nm] = _adamw(weights[nm], mom[nm], var[nm], [arrived_grad((nm, l), after) for l in range(nl)],
                                f"adamw_{nm}")
        after = out_big[nm][0]

    dm_mine = jnp.stack(dmods).reshape(nl * 2, 6 * d)
    dm_rows = jnp.pad(dm_mine, ((0, SUBLANES - nl * 2), (0, 0)))
    dm_all = _run_comm(_Gather([dm_rows]), "gather_dmod", after=after)[0][:, :nl * 2].reshape(N_DEV, nl, 2, 6 * d)
    dm16_full = jnp.concatenate([dm_all[:, :, 1].transpose(1, 0, 2), dm_all[:, :, 0].transpose(1, 0, 2)], axis=1)
    dm16 = lax.dynamic_slice_in_dim(dm16_full, me * cs, cs, axis=2)
    g_w_ada, dc16 = _ada_bwd(c16, dm16, w_ada, "ada_bwd")

    shared = dict(c_ctx=jnp.sum(dc16[:, N_DEV:], axis=(0, 1)),
                  b_ada=jnp.sum(jnp.stack(dmods).reshape(nl, 2, 6 * d), axis=1), final_g=dfinal[0])
    shared_all = _run_comm(_Gather([_pack(list(shared.values()), SUBLANES)]), "gather_shared_grads")[0]
    small_arriving[0], token = _split_start(small_packs[0], True, "gather_small_grads0", shared_all)

    out_big["w_ada"] = _adamw(w_ada, m_w_ada, v_w_ada, g_w_ada[:, None], "adamw_w_ada")
    halves = lambda a: a.reshape(2 * nl, d // 2, a.shape[2])
    res = _adamw(halves(w_in), halves(m_w_in), halves(v_w_in),
                    [arrived_grad(("w_in", l, i), token) for l in range(nl) for i in range(2)], "adamw_w_in")
    out_big["w_in"] = [r.reshape(w_in.shape) for r in res]

    g_small = dict(zip(shared, _unpack(_sum_devices(shared_all, "sum_shared_grads"), [v.shape for v in shared.values()])))
    per = []
    for l in range(nl):
        got = _split_wait(small_arriving[l], res[0], f"arrived_small_grads{l}")
        per.append(_unpack(_sum_devices(got, f"sum_small_grads{l}"), [smalls[l][nm].shape for nm in per_layer]))
    g_small.update({nm: jnp.stack([per[l][i] for l in range(nl)]) for i, nm in enumerate(per_layer)})
    for nm, ax in _SMALL_SHARD_AXIS.items():
        n_sh = weights[nm].shape[ax]
        g_small[nm] = lax.dynamic_slice_in_dim(g_small[nm], me * n_sh, n_sh, axis=ax)
    shapes_own = [weights[nm].shape for nm in _SMALL]
    pk = lambda src: _pack([src[nm] for nm in _SMALL], 2 * SUBLANES)[None]
    res_small = _adamw(pk(weights), pk(mom), pk(var), pk(g_small)[:, None], "adamw_small")
    out_small = [dict(zip(_SMALL, _unpack(r[0], shapes_own))) for r in res_small]

    names = ["c_ctx", "w_ada", "b_ada", "norm1_g", "w_in", "ret_decay", "ret_gn_g", "conv_dw_w", "conv_dw_b", "conv_ln_g",
             "conv_ln_b", "conv_pw", "na_rpb", "w_out", "norm2_g", "ffn_up", "ffn_dw_w", "ffn_dw_b", "ffn_down", "final_g"]
    outs = [loss, grad_x]
    for kind in range(4):
        for nm in names:
            outs.append(out_big[nm][kind] if nm in out_big else out_small[kind][nm])
    return tuple(outs)
```

```python
import numpy as np
import jax
import jax.numpy as jnp
from jax import lax
from jax.experimental import pallas as pl
from jax.experimental.pallas import tpu as pltpu

D_MODEL = 2048
SEQ = 4096
DEPTH = 2
GRID_W = 64
CTX_LEN = 256
RET_HEADS = 4
RET_DK = 128
RET_DV = 256
RET_CHUNK = 128
CONV_W = 512
CONV_K = 31
NA_HEADS = 4
NA_DH = 128
NA_ROWS = 8
NA_COLS = 16
D_FF = 5632
FFN_K = 3
ROPE_BASE = 10000.0
EPS = 1e-6
ADAM_LR = 0.001
ADAM_B1 = 0.9
ADAM_B2 = 0.999
ADAM_EPS = 1e-08
ADAM_WD = 0.01
ADAM_STEP = 10
N_DEV = 8

LANES = 128
SUBLANES = 8
VMEM_LIMIT = 56 * 1024 * 1024
ROW_CHUNK = 32

F32 = jnp.float32
BF16 = jnp.bfloat16
MESH = pl.DeviceIdType.MESH
NEG = -1e30


def _ret_qk_w():
    return RET_HEADS * RET_DK


def _ret_w():
    return RET_HEADS * RET_DV


def _na_w():
    return NA_HEADS * NA_DH


def _d_in():
    return 2 * _ret_qk_w() + 2 * _ret_w() + 2 * CONV_W + 3 * _na_w()


def _offsets():
    sizes = [_ret_qk_w(), _ret_qk_w(), _ret_w(), _ret_w(), CONV_W, CONV_W, _na_w(), _na_w(), _na_w()]
    offs = [0]
    for s in sizes[:-1]:
        offs.append(offs[-1] + s)
    return dict(zip(["q", "k", "v", "g", "a", "b", "nq", "nk", "nv"], offs))


def _t_rows():
    return CTX_LEN + SEQ


def _tm():
    return CTX_LEN


def _params(sem=None):
    kw = dict(vmem_limit_bytes=VMEM_LIMIT)
    if sem is not None:
        kw["dimension_semantics"] = sem
    return pltpu.CompilerParams(**kw)


def _tile(n, pref, align):
    best = None
    for t in range(align, min(n, pref) + 1, align):
        if n % t == 0:
            best = t
    return best if best is not None else n


def _dg(a, b, ca, cb):
    return lax.dot_general(a.astype(BF16), b.astype(BF16), (((ca,), (cb,)), ((), ())), preferred_element_type=F32)


@jax.custom_vjp
def dot_nn(a, b):
    return _dg(a, b, 1, 0)


dot_nn.defvjp(lambda a, b: (_dg(a, b, 1, 0), (a, b)),
              lambda r, g: (_dg(g, r[1], 1, 1), _dg(r[0], g, 0, 0)))


@jax.custom_vjp
def dot_nt(a, b):
    return _dg(a, b, 1, 1)


dot_nt.defvjp(lambda a, b: (_dg(a, b, 1, 1), (a, b)),
              lambda r, g: (_dg(g, r[1], 1, 0), _dg(g, r[0], 0, 0)))


@jax.custom_vjp
def dot_tn(a, b):
    return _dg(a, b, 0, 0)


dot_tn.defvjp(lambda a, b: (_dg(a, b, 0, 0), (a, b)),
              lambda r, g: (_dg(r[1], g, 1, 1), _dg(r[0], g, 1, 0)))


def _sigmoid(x):
    return 0.5 * jnp.tanh(0.5 * x) + 0.5


def _silu(x):
    return x * _sigmoid(x)


def _my_pos():
    return lax.axis_index("x"), lax.axis_index("y"), lax.axis_index("c")


def _my_index():
    x, y, c = _my_pos()
    return 4 * x + 2 * y + c


_ANY = pl.BlockSpec(memory_space=pl.ANY)


class _Gather:
    def __init__(self, arrays):
        self.arrays = list(arrays)
        n = len(self.arrays)
        self.out_shape = [jax.ShapeDtypeStruct((N_DEV,) + a.shape, a.dtype) for a in self.arrays]
        self.scratch = [pltpu.SemaphoreType.DMA((n, 7)), pltpu.SemaphoreType.DMA((n, 7)), pltpu.SemaphoreType.DMA((n,))]

    def _plan(self, xs, outs, sems):
        send_sems, recv_sems, local_sems = sems
        n = len(self.arrays)
        x, y, c = _my_pos()
        me, sibling = (x, y, c), (x, y, 1 - c)
        chips = [(1 - x, y), (x, 1 - y), (1 - x, 1 - y)]

        def slot(a, p):
            return outs[a].at[4 * p[0] + 2 * p[1] + p[2]]

        def copy(a, k, block, to, src=None):
            return pltpu.make_async_remote_copy(
                src_ref=slot(a, block) if src is None else src, dst_ref=slot(a, block),
                send_sem=send_sems.at[a, k], recv_sem=recv_sems.at[a, k], device_id=to, device_id_type=MESH)

        mine = [pltpu.make_async_copy(xs[a], slot(a, me), local_sems.at[a]) for a in range(n)]
        first = []
        for a in range(n):
            first.append(copy(a, 0, me, sibling, src=xs[a]))
            first += [copy(a, 1 + j, me, (*chip, c), src=xs[a]) for j, chip in enumerate(chips)]
        return n, c, me, sibling, chips, copy, mine, first

    def start(self, xs, outs, sems):
        _, _, _, _, _, _, mine, first = self._plan(xs, outs, sems)
        for m in mine:
            m.start()
        for cp in first:
            cp.start()

    def finish(self, xs, outs, sems):
        n, c, me, sibling, chips, copy, mine, first = self._plan(xs, outs, sems)
        passed = []
        for a in range(n):
            for j, chip in enumerate(chips):
                copy(a, 1 + j, (*chip, c), me).wait_recv()
                p = copy(a, 4 + j, (*chip, c), sibling)
                p.start()
                passed.append(p)
        for a in range(n):
            copy(a, 0, sibling, me).wait_recv()
            for j, chip in enumerate(chips):
                copy(a, 4 + j, (*chip, 1 - c), me).wait_recv()
        for cp in first + passed:
            cp.wait_send()
        for m in mine:
            m.wait()


def _run_comm(comm, name, after=None):
    n = len(comm.arrays)
    extra = [] if after is None else [after]

    def body(*refs):
        xs, outs, sems = refs[:n], refs[n + len(extra):2 * n + len(extra)], refs[2 * n + len(extra):]
        comm.start(xs, outs, sems)
        comm.finish(xs, outs, sems)

    return pl.pallas_call(body, name=name, out_shape=comm.out_shape, in_specs=[_ANY] * (n + len(extra)),
                          out_specs=[_ANY] * n, scratch_shapes=comm.scratch)(*comm.arrays, *extra)


_HBM = pl.BlockSpec(memory_space=pltpu.HBM)
_SEMS = pl.BlockSpec(memory_space=pltpu.SEMAPHORE)
_EFFECT = pltpu.SideEffectType.DATAFLOW_SIDE_EFFECTING


def _own_slot(x, gathering, name):
    shape = (N_DEV,) + x.shape if gathering else x.shape
    r, c = shape[1], shape[2]
    tr = _tile(r, 256, 2 * SUBLANES)
    me = jnp.reshape(_my_index(), (1,)).astype(jnp.int32)

    def body(me_ref, x_ref, o_ref):
        o_ref[...] = x_ref[...]

    src = (pl.BlockSpec((tr, c), lambda i, m: (i, 0)) if gathering
           else pl.BlockSpec((None, tr, c), lambda i, m: (m[0], i, 0)))
    grid_spec = pltpu.PrefetchScalarGridSpec(
        num_scalar_prefetch=1, grid=(r // tr,), in_specs=[src],
        out_specs=pl.BlockSpec((None, tr, c), lambda i, m: (m[0], i, 0)))
    return pl.pallas_call(body, name=name, grid_spec=grid_spec, out_shape=jax.ShapeDtypeStruct(shape, x.dtype),
                          compiler_params=_params(("arbitrary",)))(me, x)


def _split_plan(x_ref, land_ref, send_sems, recv_sems, gathering):
    x, y, c = _my_pos()
    me = 4 * x + 2 * y + c
    sends, recvs = [], []
    for k in range(1, N_DEV):
        px = 1 - x if (k >> 2) & 1 else x
        py = 1 - y if (k >> 1) & 1 else y
        pc = 1 - c if k & 1 else c
        peer = 4 * px + 2 * py + pc
        mine, theirs = (x_ref, x_ref) if gathering else (x_ref.at[peer], x_ref.at[me])
        sends.append(pltpu.make_async_remote_copy(
            src_ref=mine, dst_ref=land_ref.at[me], send_sem=send_sems.at[k - 1], recv_sem=recv_sems.at[k - 1],
            device_id=(px, py, pc), device_id_type=MESH))
        recvs.append(pltpu.make_async_remote_copy(
            src_ref=theirs, dst_ref=land_ref.at[peer], send_sem=send_sems.at[k - 1], recv_sem=recv_sems.at[k - 1],
            device_id=(px, py, pc), device_id_type=MESH))
    return sends, recvs


def _split_start(x, gathering, name, prev):
    land = _own_slot(x, gathering, name + "_own")

    def body(x_ref, land_ref, prev_ref, send_sems, recv_sems, x_thru, land_thru, token):
        sends, _ = _split_plan(x_ref, land_ref, send_sems, recv_sems, gathering)
        for s in sends:
            s.start()
        token[...] = jnp.zeros_like(token)

    sems = pltpu.SemaphoreType.DMA((N_DEV - 1,))
    send_sems, recv_sems, x_thru, land_thru, token = pl.pallas_call(
        body, name=name,
        out_shape=(sems, sems, pltpu.HBM(x.shape, x.dtype), pltpu.HBM(land.shape, land.dtype),
                   jax.ShapeDtypeStruct((SUBLANES, LANES), F32)),
        in_specs=(_HBM, _HBM, _ANY), out_specs=(_SEMS, _SEMS, _HBM, _HBM, pl.BlockSpec(memory_space=pltpu.VMEM)),
        input_output_aliases={0: 2, 1: 3},
        compiler_params=pltpu.CompilerParams(has_side_effects=_EFFECT),
    )(pltpu.with_memory_space_constraint(x, pltpu.HBM), pltpu.with_memory_space_constraint(land, pltpu.HBM), prev)
    return (send_sems, recv_sems, x_thru, land_thru, gathering), token


def _after(a, token):
    return a + token[0, 0].astype(a.dtype)


def _split_wait(handle, after, name):
    send_sems, recv_sems, x_thru, land_thru, gathering = handle

    def body(x_ref, land_ref, send_sems, recv_sems, after_ref, x_dead, got_ref):
        sends, recvs = _split_plan(x_ref, land_ref, send_sems, recv_sems, gathering)
        for s in sends:
            s.wait_send()
        for r in recvs:
            r.wait_recv()

    return pl.pallas_call(
        body, name=name, out_shape=(pltpu.HBM(x_thru.shape, x_thru.dtype), pltpu.HBM(land_thru.shape, land_thru.dtype)),
        in_specs=(_HBM, _HBM, _SEMS, _SEMS, _ANY), out_specs=(_HBM, _HBM), input_output_aliases={0: 0, 1: 1},
        compiler_params=pltpu.CompilerParams(has_side_effects=_EFFECT),
    )(x_thru, land_thru, send_sems, recv_sems, after)[1]


def _call(body, *, name, grid, in_specs, out_specs, out_shape, args, scratch=(), sem=None, after=None):
    if after is None:
        return list(pl.pallas_call(body, name=name, grid=grid, in_specs=list(in_specs), out_specs=list(out_specs),
                                   out_shape=list(out_shape), scratch_shapes=list(scratch),
                                   compiler_params=_params(sem))(*args))
    n_in = len(in_specs)

    def wrapped(*refs):
        body(*refs[:n_in], *refs[n_in + 1:])

    return list(pl.pallas_call(wrapped, name=name, grid=grid, in_specs=list(in_specs) + [_ANY], out_specs=list(out_specs),
                               out_shape=list(out_shape), scratch_shapes=list(scratch),
                               compiler_params=_params(sem))(*args, after))


MM_B_BLOCK_BYTES = 6 * 1024 * 1024
MM_O_BLOCK_BYTES = 13 * 1024 * 1024 // 2


def _mm(a, b, name, tb=False, out_dtype=F32, b3=False, o_cs=None, tm_max=1088, after=None):
    m, k = a.shape
    if b3:
        cs = b.shape[2]
        n, kb = (b.shape[1], N_DEV * cs) if tb else (N_DEV * cs, b.shape[1])
    else:
        n, kb = (b.shape[0], b.shape[1]) if tb else (b.shape[1], b.shape[0])
    assert k == kb, (a.shape, b.shape, tb)
    if b3 and tb:
        tm, tn = _tile(m, 544, 2 * SUBLANES), _tile(n, 256, LANES)

        def body_shards(a_ref, b_ref, o_ref):
            r = None
            for j in range(N_DEV):
                part = lax.dot_general(a_ref[:, j * cs:(j + 1) * cs], b_ref[j], (((1,), (1,)), ((), ())),
                                       preferred_element_type=F32)
                r = part if r is None else r + part
            o_ref[...] = r.astype(o_ref.dtype)

        return _call(
            body_shards, name=name, grid=(m // tm, n // tn),
            in_specs=[pl.BlockSpec((tm, k), lambda i, j: (i, 0)), pl.BlockSpec((N_DEV, tn, cs), lambda i, j: (0, j, 0))],
            out_specs=[pl.BlockSpec((tm, tn), lambda i, j: (i, j))], out_shape=[jax.ShapeDtypeStruct((m, n), out_dtype)],
            sem=("parallel", "parallel"), args=(a, b), after=after)[0]
    tm = _tile(m, tm_max, 2 * SUBLANES)
    tk = k
    if b3:
        tn = cs
    elif o_cs is not None:
        tn = o_cs if o_cs % LANES == 0 else 2 * o_cs
    else:
        tn = _tile(n, min(MM_B_BLOCK_BYTES // (2 * tk), MM_O_BLOCK_BYTES // (4 * tm)), LANES)
    cb = 1 if tb else 0
    dn = (((1,), (cb,)), ((), ()))

    def body_one(a_ref, b_ref, o_ref):
        r = lax.dot_general(a_ref[...], b_ref[...], dn, preferred_element_type=F32)
        if o_cs is None:
            o_ref[...] = r.astype(o_ref.dtype)
        else:
            for j in range(tn // o_cs):
                o_ref[j] = r[:, j * o_cs:(j + 1) * o_cs].astype(o_ref.dtype)

    a_spec = pl.BlockSpec((tm, tk), lambda i, j: (i, 0))
    if b3:
        b_spec = pl.BlockSpec((None, tk, cs), lambda i, j: (j, 0, 0))
    else:
        b_spec = pl.BlockSpec((tn, tk), lambda i, j: (j, 0)) if tb else pl.BlockSpec((tk, tn), lambda i, j: (0, j))
    if o_cs is None:
        o_spec = pl.BlockSpec((tm, tn), lambda i, j: (i, j))
        o_shape = jax.ShapeDtypeStruct((m, n), out_dtype)
    else:
        o_spec = pl.BlockSpec((tn // o_cs, tm, o_cs), lambda i, j: (j, i, 0))
        o_shape = jax.ShapeDtypeStruct((n // o_cs, m, o_cs), out_dtype)
    return _call(
        body_one, name=name, grid=(m // tm, n // tn), in_specs=[a_spec, b_spec], out_specs=[o_spec], out_shape=[o_shape],
        sem=("parallel", "parallel"), args=(a, b), after=after)[0]


DW_TM = 512


def _transpose_bf16(x, name):
    t, c = x.shape
    tt = _tm()

    def body(x_ref, o_ref):
        o_ref[...] = x_ref[...].T

    return pl.pallas_call(body, name=name, grid=(t // tt,), in_specs=[pl.BlockSpec((tt, c), lambda i: (i, 0))],
                          out_specs=pl.BlockSpec((c, tt), lambda i: (0, i)),
                          out_shape=jax.ShapeDtypeStruct((c, t), BF16), compiler_params=_params(("parallel",)))(x)


def _cast_bf16(x, name):
    r, c = x.shape
    tr = _tile(r, 512, 2 * SUBLANES)

    def body(x_ref, o_ref):
        o_ref[...] = x_ref[...].astype(BF16)

    return pl.pallas_call(body, name=name, grid=(r // tr,), in_specs=[pl.BlockSpec((tr, c), lambda i: (i, 0))],
                          out_specs=pl.BlockSpec((tr, c), lambda i: (i, 0)),
                          out_shape=jax.ShapeDtypeStruct((r, c), BF16), compiler_params=_params(("parallel",)))(x)


def _cols_from_shards(wg, name):
    _, k, cs = wg.shape
    tk = _tile(k, 256, 2 * SUBLANES)

    def body(w_ref, o_ref):
        for j in range(N_DEV):
            o_ref[:, j * cs:(j + 1) * cs] = w_ref[j]

    return pl.pallas_call(body, name=name, grid=(k // tk,),
                          in_specs=[pl.BlockSpec((N_DEV, tk, cs), lambda i: (0, i, 0))],
                          out_specs=pl.BlockSpec((tk, N_DEV * cs), lambda i: (i, 0)),
                          out_shape=jax.ShapeDtypeStruct((k, N_DEV * cs), wg.dtype),
                          compiler_params=_params(("parallel",)))(wg)


def _stream(i):
    return jnp.minimum(i, 1)


def _normmod(x, g, sh, sc):
    y = x * lax.rsqrt(jnp.mean(x * x, axis=-1, keepdims=True) + EPS)
    return (y * g) * (1.0 + sc) + sh


def _mod_spec(chunk, d):
    return pl.BlockSpec((None, None, 1, d), lambda i: (_stream(i), chunk, 0, 0))


def _normmod_fwd(x, g, mod4, which, name):
    t, d = x.shape
    tm = _tm()
    ish, isc = (0, 1) if which == 0 else (3, 4)

    def body(x_ref, g_ref, sh_ref, sc_ref, o_ref, ot_ref):
        h = _normmod(x_ref[...], g_ref[...], sh_ref[...], sc_ref[...]).astype(BF16)
        o_ref[...] = h
        ot_ref[...] = h.T

    row = pl.BlockSpec((tm, d), lambda i: (i, 0))
    return pl.pallas_call(body, name=name, grid=(t // tm,),
                          in_specs=[row, pl.BlockSpec((1, d), lambda i: (0, 0)), _mod_spec(ish, d), _mod_spec(isc, d)],
                          out_specs=[row, pl.BlockSpec((d, tm), lambda i: (0, i))],
                          out_shape=[jax.ShapeDtypeStruct((t, d), BF16), jax.ShapeDtypeStruct((d, t), BF16)],
                          compiler_params=_params(("parallel",)))(x, g, mod4, mod4)


def _normmod_bwd(x, g, mod4, which, dh, dres, name):
    t, d = x.shape
    tm = _tm()
    ish, isc = (0, 1) if which == 0 else (3, 4)

    def body(x_ref, g_ref, sh_ref, sc_ref, dh_ref, dres_ref, dx_ref, dg_ref, dsh_ref, dsc_ref):
        i = pl.program_id(0)
        _, vjp = jax.vjp(_normmod, x_ref[...], g_ref[...], sh_ref[...], sc_ref[...])
        dx, dg, dsh, dsc = vjp(dh_ref[...])
        dx_ref[...] = dres_ref[...] + dx

        @pl.when(i == 0)
        def _():
            dg_ref[...] = jnp.zeros_like(dg_ref)

        @pl.when(i <= 1)
        def _():
            dsh_ref[...] = jnp.zeros_like(dsh_ref)
            dsc_ref[...] = jnp.zeros_like(dsc_ref)

        dg_ref[...] += dg
        dsh_ref[...] += dsh
        dsc_ref[...] += dsc

    row = pl.BlockSpec((tm, d), lambda i: (i, 0))
    vec = pl.BlockSpec((1, d), lambda i: (0, 0))
    svec = pl.BlockSpec((None, 1, d), lambda i: (_stream(i), 0, 0))
    return _call(
        body, name=name, grid=(t // tm,),
        in_specs=[row, vec, _mod_spec(ish, d), _mod_spec(isc, d), row, row],
        out_specs=[row, vec, svec, svec],
        out_shape=[jax.ShapeDtypeStruct((t, d), F32), jax.ShapeDtypeStruct((1, d), F32),
                   jax.ShapeDtypeStruct((2, 1, d), F32), jax.ShapeDtypeStruct((2, 1, d), F32)],
        sem=("arbitrary",), args=(x, g, mod4, mod4, dh, dres))


def _gate_res_fwd(x, f, mod4, chunk, name):
    t, d = x.shape
    tm = _tm()

    def body(x_ref, f_ref, g_ref, o_ref):
        o_ref[...] = x_ref[...] + g_ref[...] * f_ref[...]

    row = pl.BlockSpec((tm, d), lambda i: (i, 0))
    return pl.pallas_call(body, name=name, grid=(t // tm,), in_specs=[row, row, _mod_spec(chunk, d)], out_specs=row,
                          out_shape=jax.ShapeDtypeStruct((t, d), F32), compiler_params=_params(("parallel",)))(x, f, mod4)


def _gate_res_bwd(dx, f, mod4, chunk, name):
    t, d = dx.shape
    tm = _tm()

    def body(dx_ref, f_ref, g_ref, o_ref, dg_ref):
        i = pl.program_id(0)
        dxv = dx_ref[...]
        o_ref[...] = (dxv * g_ref[...]).astype(BF16)

        @pl.when(i <= 1)
        def _():
            dg_ref[...] = jnp.zeros_like(dg_ref)

        dg_ref[...] += jnp.sum(dxv * f_ref[...], axis=0, keepdims=True)

    row = pl.BlockSpec((tm, d), lambda i: (i, 0))
    return pl.pallas_call(
        body, name=name, grid=(t // tm,), in_specs=[row, row, _mod_spec(chunk, d)],
        out_specs=[row, pl.BlockSpec((None, 1, d), lambda i: (_stream(i), 0, 0))],
        out_shape=[jax.ShapeDtypeStruct((t, d), BF16), jax.ShapeDtypeStruct((2, 1, d), F32)],
        compiler_params=_params(("arbitrary",)))(dx, f, mod4)


def _loss_head(x, final_g, target, name):
    t, d = x.shape
    tm = _tm()

    def loss_fn(xv, g, tgt):
        y = (xv * lax.rsqrt(jnp.mean(xv * xv, axis=-1, keepdims=True) + EPS)) * g
        err = y - tgt
        return 0.5 * jnp.sum(jnp.mean(err * err, axis=-1, keepdims=True))

    def body(x_ref, g_ref, t_ref, l_ref, dx_ref, dg_ref):
        i = pl.program_id(0)

        @pl.when(i == 0)
        def _():
            l_ref[...] = jnp.zeros_like(l_ref)
            dg_ref[...] = jnp.zeros_like(dg_ref)
            dx_ref[...] = jnp.zeros_like(dx_ref)

        @pl.when(i > 0)
        def _():
            l, (dx, dg) = jax.value_and_grad(loss_fn, argnums=(0, 1))(x_ref[...], g_ref[...], t_ref[...])
            l_ref[...] += jnp.full(l_ref.shape, l, F32)
            dx_ref[...] = dx
            dg_ref[...] += dg

    row = pl.BlockSpec((tm, d), lambda i: (i, 0))
    vec = pl.BlockSpec((1, d), lambda i: (0, 0))
    return pl.pallas_call(
        body, name=name, grid=(t // tm,),
        in_specs=[row, vec, pl.BlockSpec((tm, d), lambda i: (jnp.maximum(i - 1, 0), 0))],
        out_specs=[pl.BlockSpec((SUBLANES, LANES), lambda i: (0, 0)), row, vec],
        out_shape=[jax.ShapeDtypeStruct((SUBLANES, LANES), F32), jax.ShapeDtypeStruct((t, d), F32),
                   jax.ShapeDtypeStruct((1, d), F32)],
        compiler_params=_params(("arbitrary",)))(x, final_g, target)


def _swap_quarters(x):
    half, nf = RET_DK // 2, RET_DK // 4
    lane = lax.broadcasted_iota(jnp.int32, x.shape, 1)
    return jnp.where((lane % half) < nf, pltpu.roll(x, RET_DK - nf, 1), pltpu.roll(x, nf, 1))


def _rope(x, cos, sin):
    return x * cos + _swap_quarters(x) * sin


def _rope_t(y, cos, sin):
    return y * cos + _swap_quarters(y * sin)


def _ret_consts(d):
    c = RET_CHUNK
    ii = lax.broadcasted_iota(jnp.int32, (c, 1), 0).astype(F32)
    jj = lax.broadcasted_iota(jnp.int32, (1, c), 1).astype(F32)
    fwd = d == 0
    sgn = jnp.where(fwd, 1.0, -1.0).astype(F32)
    pos = jnp.where(fwd, ii, c - 1.0 - ii)
    return sgn * (ii - jj), pos


def _ret_step(lgt, state, q, k, v, diff, pos):
    c = float(RET_CHUNK)
    lg = -(jnp.maximum(-lgt, 0.0) + jnp.log1p(jnp.exp(-jnp.abs(lgt))))
    lower = diff >= 0
    decay = jnp.where(lower, jnp.exp(jnp.where(lower, diff, 0.0) * lg), 0.0)
    xi = jnp.exp((pos + 1.0) * lg)
    zeta = jnp.exp((c - 1.0 - pos) * lg)
    gch = jnp.exp(c * lg)
    inner = dot_nt(q, k) * decay
    out = dot_nn(inner, v) + dot_nn(q, state) * xi
    new_state = state * gch + dot_tn(k * zeta, v)
    return out, new_state


def _chunk_order():
    nc, nch = CTX_LEN // RET_CHUNK, _t_rows() // RET_CHUNK
    fwd = list(range(nch))
    bwd = list(range(nc - 1, -1, -1)) + list(range(nch - 1, nc - 1, -1))
    return jnp.asarray(np.array([fwd, bwd], np.int32))


def _ret_fwd(p, cos, sin, decay, order, name):
    t = p.shape[0]
    c, dk, dv, nh = RET_CHUNK, RET_DK, RET_DV, RET_HEADS
    nch = t // c
    off = _offsets()
    wqk, wv = nh * dk, nh * dv
    assert off["q"] % wqk == 0 and off["k"] % wqk == 0 and off["v"] % wv == 0
    qb, kb, vb = off["q"] // wqk, off["k"] // wqk, off["v"] // wv
    scale = RET_DK ** -0.5

    def body(ord_ref, dec_ref, *refs):
        ins, o_refs, st_ref, state = refs[:10], refs[10:12], refs[12], refs[13]
        s = pl.program_id(0)

        @pl.when(s == 0)
        def _():
            state[...] = jnp.zeros_like(state)

        for d in range(2):
            q_ref, k_ref, v_ref, cos_ref, sin_ref = ins[5 * d:5 * d + 5]
            diff, pos = _ret_consts(d)
            cosv, sinv = cos_ref[...], sin_ref[...]
            for h in range(nh):
                st = state[d, h]
                st_ref[d, h] = st
                lgt = jnp.full((1, 1), dec_ref[d, h], F32)
                q = _rope(q_ref[:, h * dk:(h + 1) * dk], cosv, sinv) * scale
                k = _rope(k_ref[:, h * dk:(h + 1) * dk], cosv, sinv)
                out, ns = _ret_step(lgt, st, q, k, v_ref[:, h * dv:(h + 1) * dv], diff, pos)
                o_refs[d][:, h * dv:(h + 1) * dv] = out
                state[d, h] = ns

    def dir_specs(d):
        return [pl.BlockSpec((c, wqk), lambda s, o: (o[d, s], qb)), pl.BlockSpec((c, wqk), lambda s, o: (o[d, s], kb)),
                pl.BlockSpec((c, wv), lambda s, o: (o[d, s], vb)), pl.BlockSpec((c, dk), lambda s, o: (o[d, s], 0)),
                pl.BlockSpec((c, dk), lambda s, o: (o[d, s], 0))]

    grid_spec = pltpu.PrefetchScalarGridSpec(
        num_scalar_prefetch=1, grid=(nch,),
        in_specs=[pl.BlockSpec(memory_space=pltpu.SMEM)] + dir_specs(0) + dir_specs(1),
        out_specs=[pl.BlockSpec((c, wv), lambda s, o: (o[0, s], 0)), pl.BlockSpec((c, wv), lambda s, o: (o[1, s], 0)),
                   pl.BlockSpec((2, nh, None, dk, dv), lambda s, o: (0, 0, s, 0, 0))],
        scratch_shapes=[pltpu.VMEM((2, nh, dk, dv), F32)])
    return pl.pallas_call(
        body, name=name, grid_spec=grid_spec,
        out_shape=[jax.ShapeDtypeStruct((t, wv), F32), jax.ShapeDtypeStruct((t, wv), F32),
                   jax.ShapeDtypeStruct((2, nh, nch, dk, dv), F32)],
        compiler_params=_params(("arbitrary",)))(order, decay, p, p, p, cos, sin, p, p, p, cos, sin)


def _ret_bwd(p, cos, sin, decay, order, states, do, name):
    t = p.shape[0]
    c, dk, dv, nh = RET_CHUNK, RET_DK, RET_DV, RET_HEADS
    nch = t // c
    off = _offsets()
    wqk, wv = nh * dk, nh * dv
    qb, kb, vb = off["q"] // wqk, off["k"] // wqk, off["v"] // wv
    scale = RET_DK ** -0.5

    def body(ord_ref, dec_ref, *refs):
        ins, st_ref, outs, dd_ref, dstate = refs[:12], refs[12], refs[13:19], refs[19], refs[20]
        s = pl.program_id(0)

        @pl.when(s == 0)
        def _():
            dstate[...] = jnp.zeros_like(dstate)
            dd_ref[...] = jnp.zeros_like(dd_ref)

        for d in range(2):
            q_ref, k_ref, v_ref, cos_ref, sin_ref, do_ref = ins[6 * d:6 * d + 6]
            dq_ref, dk_ref, dv_ref = outs[3 * d:3 * d + 3]
            diff, pos = _ret_consts(d)
            cosv, sinv = cos_ref[...], sin_ref[...]
            for h in range(nh):
                qk, vv = slice(h * dk, (h + 1) * dk), slice(h * dv, (h + 1) * dv)
                lgt = jnp.full((1, 1), dec_ref[d, h], F32)
                q = _rope(q_ref[:, qk], cosv, sinv) * scale
                k = _rope(k_ref[:, qk], cosv, sinv)
                _, vjp = jax.vjp(lambda a, b, cq, ck, cv: _ret_step(a, b, cq, ck, cv, diff, pos),
                                 lgt, st_ref[d, h], q, k, v_ref[:, vv])
                dlgt, dst, dq, dkk, dvv = vjp((do_ref[:, vv], dstate[d, h]))
                dstate[d, h] = dst
                dq_ref[:, qk] = _rope_t(dq * scale, cosv, sinv)
                dk_ref[:, qk] = _rope_t(dkk, cosv, sinv)
                dv_ref[:, vv] = dvv
                dd_ref[d, h] += jnp.broadcast_to(dlgt, (SUBLANES, LANES))

    def chunk(d):
        return lambda s, o: o[d, nch - 1 - s]

    def dir_specs(d):
        at = chunk(d)
        return [pl.BlockSpec((c, wqk), lambda s, o: (at(s, o), qb)), pl.BlockSpec((c, wqk), lambda s, o: (at(s, o), kb)),
                pl.BlockSpec((c, wv), lambda s, o: (at(s, o), vb)), pl.BlockSpec((c, dk), lambda s, o: (at(s, o), 0)),
                pl.BlockSpec((c, dk), lambda s, o: (at(s, o), 0)), pl.BlockSpec((c, wv), lambda s, o: (at(s, o), 0))]

    def dir_outs(d):
        at = chunk(d)
        return [pl.BlockSpec((c, wqk), lambda s, o: (at(s, o), 0)), pl.BlockSpec((c, wqk), lambda s, o: (at(s, o), 0)),
                pl.BlockSpec((c, wv), lambda s, o: (at(s, o), 0))]

    grid_spec = pltpu.PrefetchScalarGridSpec(
        num_scalar_prefetch=1, grid=(nch,),
        in_specs=[pl.BlockSpec(memory_space=pltpu.SMEM)] + dir_specs(0) + dir_specs(1)
        + [pl.BlockSpec((2, nh, None, dk, dv), lambda s, o: (0, 0, nch - 1 - s, 0, 0))],
        out_specs=dir_outs(0) + dir_outs(1) + [pl.BlockSpec((2, nh, SUBLANES, LANES), lambda s, o: (0, 0, 0, 0))],
        scratch_shapes=[pltpu.VMEM((2, nh, dk, dv), F32)])
    qk_sds, v_sds = jax.ShapeDtypeStruct((t, wqk), F32), jax.ShapeDtypeStruct((t, wv), F32)
    res = pl.pallas_call(
        body, name=name, grid_spec=grid_spec,
        out_shape=[qk_sds, qk_sds, v_sds, qk_sds, qk_sds, v_sds, jax.ShapeDtypeStruct((2, nh, SUBLANES, LANES), F32)],
        compiler_params=_params(("arbitrary",)))(order, decay, p, p, p, cos, sin, do, p, p, p, cos, sin, do, states)
    return res[:3], res[3:6], res[6]


def _ggn_head(of, ob, gate, g):
    o = of + ob
    mu = jnp.mean(o, axis=-1, keepdims=True)
    var = jnp.mean(jnp.square(o - mu), axis=-1, keepdims=True)
    return ((o - mu) * lax.rsqrt(var + EPS) * g) * _silu(gate)


def _ggn_fwd(o_f, o_b, p, gn_g, name):
    t = p.shape[0]
    tm, w, dv = _tm(), _ret_w(), RET_DV
    gb = _offsets()["g"] // w

    def body(of_ref, ob_ref, gate_ref, g_ref, out_ref):
        for h in range(RET_HEADS):
            sl = slice(h * dv, (h + 1) * dv)
            out_ref[:, sl] = _ggn_head(of_ref[:, sl], ob_ref[:, sl], gate_ref[:, sl], g_ref[:, sl]).astype(BF16)

    row = pl.BlockSpec((tm, w), lambda i: (i, 0))
    return pl.pallas_call(
        body, name=name, grid=(t // tm,),
        in_specs=[row, row, pl.BlockSpec((tm, w), lambda i: (i, gb)), pl.BlockSpec((1, w), lambda i: (0, 0))],
        out_specs=row, out_shape=jax.ShapeDtypeStruct((t, w), BF16),
        compiler_params=_params(("parallel",)))(o_f, o_b, p, gn_g)


def _ggn_bwd(o_f, o_b, p, gn_g, dmix, name):
    t = p.shape[0]
    tm, w, dv = _tm(), _ret_w(), RET_DV
    gb = _offsets()["g"] // w

    def body(of_ref, ob_ref, gate_ref, g_ref, dy_ref, do_ref, dgate_ref, dg_ref):
        i = pl.program_id(0)

        @pl.when(i == 0)
        def _():
            dg_ref[...] = jnp.zeros_like(dg_ref)

        for h in range(RET_HEADS):
            sl = slice(h * dv, (h + 1) * dv)
            _, vjp = jax.vjp(_ggn_head, of_ref[:, sl], ob_ref[:, sl], gate_ref[:, sl], g_ref[:, sl])
            do, _, dgate, dg = vjp(dy_ref[:, sl])
            do_ref[:, sl] = do
            dgate_ref[:, sl] = dgate
            dg_ref[:, sl] += dg

    row = pl.BlockSpec((tm, w), lambda i: (i, 0))
    return pl.pallas_call(
        body, name=name, grid=(t // tm,),
        in_specs=[row, row, pl.BlockSpec((tm, w), lambda i: (i, gb)), pl.BlockSpec((1, w), lambda i: (0, 0)), row],
        out_specs=[row, row, pl.BlockSpec((1, w), lambda i: (0, 0))],
        out_shape=[jax.ShapeDtypeStruct((t, w), F32), jax.ShapeDtypeStruct((t, w), F32),
                   jax.ShapeDtypeStruct((1, w), F32)],
        compiler_params=_params(("arbitrary",)))(o_f, o_b, p, gn_g, dmix)


def _halo(k):
    return SUBLANES * ((k // 2 + SUBLANES - 1) // SUBLANES)


def _halo_specs(width, colblock, h, tm):
    r = tm // h
    return [pl.BlockSpec((h, width), lambda i, *_: (jnp.maximum(i * r - 1, 0), colblock(*_))),
            pl.BlockSpec((tm, width), lambda i, *_: (i, colblock(*_))),
            pl.BlockSpec((h, width), lambda i, *_: (jnp.minimum((i + 1) * r, (_t_rows() // h) - 1), colblock(*_)))]


def _fill_ext(ext_ref, prev, cur, nxt, i, h, tm):
    nt = _t_rows() // tm
    ext_ref[0:h, :] = jnp.where(i >= 2, prev, 0.0)
    ext_ref[h:h + tm, :] = cur
    ext_ref[h + tm:h + tm + h, :] = jnp.where((i >= 1) & (i <= nt - 2), nxt, 0.0)


def _corr(ext_ref, w_ref, k, h, tm, flip, cols=slice(None)):
    pad = k // 2
    acc = None
    for kk in range(k):
        o = h + (pad - kk if flip else kk - pad)
        term = w_ref[kk:kk + 1, cols] * ext_ref[o:o + tm, cols]
        acc = term if acc is None else acc + term
    return acc


LANE_CHUNK = 512


def _chunks(tm, tc):
    return [(r0, slice(c0, min(c0 + LANE_CHUNK, tc))) for r0 in range(0, tm, ROW_CHUNK) for c0 in range(0, tc, LANE_CHUNK)]


def _conv_post(u2, ln_g, ln_b, pw):
    mu = jnp.mean(u2, axis=-1, keepdims=True)
    var = jnp.mean(jnp.square(u2 - mu), axis=-1, keepdims=True)
    y = (u2 - mu) * lax.rsqrt(var + EPS) * ln_g + ln_b
    return dot_nn(_silu(y), pw)


def _conv_fwd(p, dw_w, dw_b, ln_g, ln_b, pw, name):
    t = p.shape[0]
    tm, w, k = _tm(), CONV_W, CONV_K
    h = _halo(k)
    off = _offsets()
    ab, bb = off["a"] // w, off["b"] // w

    def body(ap, ac, an, bp, bc, bn, w_ref, b_ref, g_ref, beta_ref, pw_ref, u2_ref, out_ref, ext):
        i = pl.program_id(0)
        glu = lambda a, b: a * _sigmoid(b)
        _fill_ext(ext, glu(ap[...], bp[...]), glu(ac[...], bc[...]), glu(an[...], bn[...]), i, h, tm)
        for r0 in range(0, tm, ROW_CHUNK):
            u2_ref[r0:r0 + ROW_CHUNK, :] = _corr(ext, w_ref, k, h + r0, ROW_CHUNK, False) + b_ref[...]
        out_ref[...] = _conv_post(u2_ref[...], g_ref[...], beta_ref[...], pw_ref[...]).astype(BF16)

    vec = pl.BlockSpec((1, w), lambda i: (0, 0))
    row = pl.BlockSpec((tm, w), lambda i: (i, 0))
    return pl.pallas_call(
        body, name=name, grid=(t // tm,),
        in_specs=_halo_specs(w, lambda: ab, h, tm) + _halo_specs(w, lambda: bb, h, tm)
        + [pl.BlockSpec((k, w), lambda i: (0, 0)), vec, vec, vec, pl.BlockSpec((w, w), lambda i: (0, 0))],
        out_specs=[row, row],
        out_shape=[jax.ShapeDtypeStruct((t, w), F32), jax.ShapeDtypeStruct((t, w), BF16)],
        scratch_shapes=[pltpu.VMEM((tm + 2 * h, w), F32)],
        compiler_params=_params(("parallel",)))(p, p, p, p, p, p, dw_w, dw_b, ln_g, ln_b, pw)


def _conv_bwd1(u2, dmix, ln_g, ln_b, pw, name):
    t = u2.shape[0]
    tm, w = _tm(), CONV_W
    cb = _ret_w() // w

    def body(u2_ref, dy_ref, g_ref, beta_ref, pw_ref, du2_ref, dg_ref, db_ref, dpw_ref):
        i = pl.program_id(0)

        @pl.when(i == 0)
        def _():
            dg_ref[...] = jnp.zeros_like(dg_ref)
            db_ref[...] = jnp.zeros_like(db_ref)
            dpw_ref[...] = jnp.zeros_like(dpw_ref)

        _, vjp = jax.vjp(_conv_post, u2_ref[...], g_ref[...], beta_ref[...], pw_ref[...])
        du2, dg, db, dpw = vjp(dy_ref[...])
        du2_ref[...] = du2
        dg_ref[...] += dg
        db_ref[...] += db
        dpw_ref[...] += dpw

    vec = pl.BlockSpec((1, w), lambda i: (0, 0))
    row = pl.BlockSpec((tm, w), lambda i: (i, 0))
    mat = pl.BlockSpec((w, w), lambda i: (0, 0))
    return pl.pallas_call(
        body, name=name, grid=(t // tm,),
        in_specs=[row, pl.BlockSpec((tm, w), lambda i: (i, cb)), vec, vec, mat],
        out_specs=[row, vec, vec, mat],
        out_shape=[jax.ShapeDtypeStruct((t, w), F32), jax.ShapeDtypeStruct((1, w), F32),
                   jax.ShapeDtypeStruct((1, w), F32), jax.ShapeDtypeStruct((w, w), F32)],
        compiler_params=_params(("arbitrary",)))(u2, dmix, ln_g, ln_b, pw)


def _conv_bwd2(du2, p, dw_w, name):
    t = p.shape[0]
    tm, w, k = _tm(), CONV_W, CONV_K
    h = _halo(k)
    pad = k // 2
    off = _offsets()
    ab, bb = off["a"] // w, off["b"] // w

    def body(dp, dc, dn, ap, ac, an, bp, bc, bn, w_ref, da_ref, db_ref, dw_ref, dbias_ref, ext_d, ext_u):
        i = pl.program_id(0)

        @pl.when(i == 0)
        def _():
            dw_ref[...] = jnp.zeros_like(dw_ref)
            dbias_ref[...] = jnp.zeros_like(dbias_ref)

        glu = lambda a, b: a * _sigmoid(b)
        _fill_ext(ext_d, dp[...], dc[...], dn[...], i, h, tm)
        _fill_ext(ext_u, glu(ap[...], bp[...]), glu(ac[...], bc[...]), glu(an[...], bn[...]), i, h, tm)
        chunks = range(0, tm, ROW_CHUNK)
        acc_b = jnp.zeros((ROW_CHUNK, w), F32)
        for r0 in chunks:
            rows = slice(r0, r0 + ROW_CHUNK)
            du = _corr(ext_d, w_ref, k, h + r0, ROW_CHUNK, True)
            sg = _sigmoid(bc[rows, :])
            da_ref[rows, :] = du * sg
            db_ref[rows, :] = du * ac[rows, :] * sg * (1.0 - sg)
            acc_b = acc_b + ext_d[h + r0:h + r0 + ROW_CHUNK, :]
        dbias_ref[...] += jnp.sum(acc_b, axis=0, keepdims=True)
        for kk in range(k):
            acc = jnp.zeros((ROW_CHUNK, w), F32)
            for r0 in chunks:
                o = h + r0 + kk - pad
                acc = acc + ext_d[h + r0:h + r0 + ROW_CHUNK, :] * ext_u[o:o + ROW_CHUNK, :]
            dw_ref[kk:kk + 1, :] += jnp.sum(acc, axis=0, keepdims=True)

    vec = pl.BlockSpec((1, w), lambda i: (0, 0))
    row = pl.BlockSpec((tm, w), lambda i: (i, 0))
    kw = pl.BlockSpec((k, w), lambda i: (0, 0))
    return _call(
        body, name=name, grid=(t // tm,),
        in_specs=_halo_specs(w, lambda: 0, h, tm) + _halo_specs(w, lambda: ab, h, tm)
        + _halo_specs(w, lambda: bb, h, tm) + [kw],
        out_specs=[row, row, kw, vec],
        out_shape=[jax.ShapeDtypeStruct((t, w), F32), jax.ShapeDtypeStruct((t, w), F32),
                   jax.ShapeDtypeStruct((k, w), F32), jax.ShapeDtypeStruct((1, w), F32)],
        scratch=[pltpu.VMEM((tm + 2 * h, w), F32), pltpu.VMEM((tm + 2 * h, w), F32)],
        sem=("arbitrary",), args=(du2, du2, du2, p, p, p, p, p, p, dw_w))


def _ffn_tc():
    return _tile(D_FF, 2816, LANES)


def _ffn_act_fwd(u, dw_w, dw_b, name):
    t = u.shape[0]
    tm, k, tc = _tm(), FFN_K, _ffn_tc()
    h = _halo(k)
    nj = D_FF // tc

    def body(vp, vc, vn, gp, gc, gn, wv, wg, bv, bg, out_ref, out_t_ref, ext_v, ext_g):
        i = pl.program_id(0)
        _fill_ext(ext_v, vp[...], vc[...], vn[...], i, h, tm)
        _fill_ext(ext_g, gp[...], gc[...], gn[...], i, h, tm)
        for r0, cols in _chunks(tm, tc):
            val = _corr(ext_v, wv, k, h + r0, ROW_CHUNK, False, cols) + bv[:, cols]
            gate = _corr(ext_g, wg, k, h + r0, ROW_CHUNK, False, cols) + bg[:, cols]
            out_ref[r0:r0 + ROW_CHUNK, cols] = (_silu(gate) * val).astype(BF16)
        out_t_ref[...] = out_ref[...].T

    wspec = lambda s: pl.BlockSpec((k, tc), lambda i, j: (0, j + s))
    bspec = lambda s: pl.BlockSpec((1, tc), lambda i, j: (0, j + s))
    return pl.pallas_call(
        body, name=name, grid=(t // tm, nj),
        in_specs=_halo_specs(tc, lambda j: j, h, tm) + _halo_specs(tc, lambda j: j + nj, h, tm)
        + [wspec(0), wspec(nj), bspec(0), bspec(nj)],
        out_specs=[pl.BlockSpec((tm, tc), lambda i, j: (i, j)), pl.BlockSpec((tc, tm), lambda i, j: (j, i))],
        out_shape=[jax.ShapeDtypeStruct((t, D_FF), BF16), jax.ShapeDtypeStruct((D_FF, t), BF16)],
        scratch_shapes=[pltpu.VMEM((tm + 2 * h, tc), F32), pltpu.VMEM((tm + 2 * h, tc), F32)],
        compiler_params=_params(("parallel", "parallel")))(u, u, u, u, u, u, dw_w, dw_w, dw_b, dw_b)


def _ffn_act_bwd1(u, da, dw_w, dw_b, name):
    t = u.shape[0]
    tm, k, tc = _tm(), FFN_K, _ffn_tc()
    h = _halo(k)
    nj = D_FF // tc

    def body(vp, vc, vn, gp, gc, gn, wv, wg, bv, bg, da_ref, dv_ref, dg_ref, ext_v, ext_g):
        i = pl.program_id(0)
        _fill_ext(ext_v, vp[...], vc[...], vn[...], i, h, tm)
        _fill_ext(ext_g, gp[...], gc[...], gn[...], i, h, tm)
        for r0, cols in _chunks(tm, tc):
            rows = slice(r0, r0 + ROW_CHUNK)
            val = _corr(ext_v, wv, k, h + r0, ROW_CHUNK, False, cols) + bv[:, cols]
            gate = _corr(ext_g, wg, k, h + r0, ROW_CHUNK, False, cols) + bg[:, cols]
            _, vjp = jax.vjp(lambda a, b: _silu(b) * a, val, gate)
            dval, dgate = vjp(da_ref[rows, cols])
            dv_ref[rows, cols] = dval
            dg_ref[rows, cols] = dgate

    wspec = lambda s: pl.BlockSpec((k, tc), lambda i, j: (0, j + s))
    bspec = lambda s: pl.BlockSpec((1, tc), lambda i, j: (0, j + s))
    dc = pl.pallas_call(
        body, name=name, grid=(t // tm, nj),
        in_specs=_halo_specs(tc, lambda j: j, h, tm) + _halo_specs(tc, lambda j: j + nj, h, tm)
        + [wspec(0), wspec(nj), bspec(0), bspec(nj), pl.BlockSpec((tm, tc), lambda i, j: (i, j))],
        out_specs=[pl.BlockSpec((tm, tc), lambda i, j: (i, j)), pl.BlockSpec((tm, tc), lambda i, j: (i, j))],
        out_shape=[jax.ShapeDtypeStruct((t, D_FF), F32), jax.ShapeDtypeStruct((t, D_FF), F32)],
        scratch_shapes=[pltpu.VMEM((tm + 2 * h, tc), F32), pltpu.VMEM((tm + 2 * h, tc), F32)],
        compiler_params=_params(("parallel", "parallel")))(u, u, u, u, u, u, dw_w, dw_w, dw_b, dw_b, da)
    return dc


def _dwconv_bwd(dcv, dcg, u, dw_w, name):
    t = u.shape[0]
    tm, k, tc = _tm(), FFN_K, _ffn_tc()
    h = _halo(k)
    pad = k // 2
    nj = D_FF // tc

    def body(vp, vc, vn, gp, gc, gn, up, uc, un, w_ref, du_ref, dw_ref, dbias_ref, ext_d, ext_u):
        jj, i = pl.program_id(0), pl.program_id(1)

        @pl.when(i == 0)
        def _():
            dw_ref[...] = jnp.zeros_like(dw_ref)
            dbias_ref[...] = jnp.zeros_like(dbias_ref)

        @pl.when(jj < nj)
        def _():
            _fill_ext(ext_d, vp[...], vc[...], vn[...], i, h, tm)

        @pl.when(jj >= nj)
        def _():
            _fill_ext(ext_d, gp[...], gc[...], gn[...], i, h, tm)

        _fill_ext(ext_u, up[...], uc[...], un[...], i, h, tm)
        for c0 in range(0, tc, LANE_CHUNK):
            cols = slice(c0, min(c0 + LANE_CHUNK, tc))
            width = cols.stop - cols.start
            acc_b = jnp.zeros((ROW_CHUNK, width), F32)
            acc_w = [jnp.zeros((ROW_CHUNK, width), F32) for _ in range(k)]
            for r0 in range(0, tm, ROW_CHUNK):
                d = ext_d[h + r0:h + r0 + ROW_CHUNK, cols]
                du_ref[r0:r0 + ROW_CHUNK, cols] = _corr(ext_d, w_ref, k, h + r0, ROW_CHUNK, True, cols).astype(BF16)
                acc_b = acc_b + d
                for kk in range(k):
                    o = h + r0 + kk - pad
                    acc_w[kk] = acc_w[kk] + d * ext_u[o:o + ROW_CHUNK, cols]
            dbias_ref[:, cols] += jnp.sum(acc_b, axis=0, keepdims=True)
            for kk in range(k):
                dw_ref[kk:kk + 1, cols] += jnp.sum(acc_w[kk], axis=0, keepdims=True)

    def hs(cb, live):
        r = tm // h
        row = lambda j, i: jnp.where(live(j), i, 0)
        return [pl.BlockSpec((h, tc), lambda j, i: (jnp.maximum(row(j, i) * r - 1, 0), cb(j))),
                pl.BlockSpec((tm, tc), lambda j, i: (row(j, i), cb(j))),
                pl.BlockSpec((h, tc), lambda j, i: (jnp.minimum((row(j, i) + 1) * r, (_t_rows() // h) - 1), cb(j)))]

    return pl.pallas_call(
        body, name=name, grid=(2 * nj, t // tm),
        in_specs=hs(lambda j: jnp.minimum(j, nj - 1), lambda j: j < nj)
        + hs(lambda j: jnp.maximum(j - nj, 0), lambda j: j >= nj)
        + hs(lambda j: j, lambda j: True) + [pl.BlockSpec((k, tc), lambda j, i: (0, j))],
        out_specs=[pl.BlockSpec((tm, tc), lambda j, i: (i, j)), pl.BlockSpec((k, tc), lambda j, i: (0, j)),
                   pl.BlockSpec((1, tc), lambda j, i: (0, j))],
        out_shape=[jax.ShapeDtypeStruct((t, 2 * D_FF), BF16), jax.ShapeDtypeStruct((k, 2 * D_FF), F32),
                   jax.ShapeDtypeStruct((1, 2 * D_FF), F32)],
        scratch_shapes=[pltpu.VMEM((tm + 2 * h, tc), F32), pltpu.VMEM((tm + 2 * h, tc), F32)],
        compiler_params=_params(("parallel", "arbitrary")))(dcv, dcv, dcv, dcg, dcg, dcg, u, u, u, dw_w)


def _na_geometry(rq):
    ncb = CTX_LEN // GRID_W
    rows_n = SEQ // GRID_W
    r = jnp.maximum(rq - ncb, 0)
    kstart = jnp.clip(r - NA_ROWS // 2, 0, rows_n - NA_ROWS)
    base = kstart - r + NA_ROWS - 1
    return rq >= ncb, kstart, base


def _na_core(q, kl, vl, kc, vc, bias, mask):
    qs = q * (NA_DH ** -0.5)
    s_l = jnp.where(mask, dot_nt(qs, kl) + bias, NEG)
    s_c = dot_nt(qs, kc)
    m = lax.stop_gradient(jnp.maximum(jnp.max(s_l, axis=1, keepdims=True), jnp.max(s_c, axis=1, keepdims=True)))
    e_l, e_c = jnp.exp(s_l - m), jnp.exp(s_c - m)
    inv = 1.0 / (jnp.sum(e_l, axis=1, keepdims=True) + jnp.sum(e_c, axis=1, keepdims=True))
    return dot_nn(e_l * inv, vl) + dot_nn(e_c * inv, vc)


def _na_mask(is_lat):
    nl = NA_ROWS * GRID_W
    q = lax.broadcasted_iota(jnp.int32, (GRID_W, nl), 0)
    w = lax.broadcasted_iota(jnp.int32, (GRID_W, nl), 1) % GRID_W
    cs = jnp.clip(q - NA_COLS // 2, 0, GRID_W - NA_COLS)
    return (w >= cs) & (w < cs + NA_COLS) & is_lat


def _na_bias(rb_ref):
    assert 2 * GRID_W == LANES
    lane = lax.broadcasted_iota(jnp.int32, (GRID_W, LANES), 1)
    tiles = []
    for kp in range(NA_ROWS // 2):
        ev = jnp.broadcast_to(rb_ref[2 * kp:2 * kp + 1, :], (GRID_W, LANES))
        od = jnp.broadcast_to(rb_ref[2 * kp + 1:2 * kp + 2, :], (GRID_W, LANES))
        ev = pltpu.roll(ev, LANES - (NA_COLS - 1), 1, stride=1, stride_axis=0)
        od = pltpu.roll(od, LANES - (NA_COLS - 1) - GRID_W, 1, stride=1, stride_axis=0)
        tiles.append(jnp.where(lane < GRID_W, ev, od))
    return jnp.concatenate(tiles, axis=1)


def _na_dbias(dbias, drb_ref):
    qi = lax.broadcasted_iota(jnp.int32, (GRID_W, GRID_W), 0)
    qj = lax.broadcasted_iota(jnp.int32, (GRID_W, GRID_W), 1)
    flip = (qi + qj == GRID_W - 1).astype(F32)
    rev = lax.dot_general(flip, dbias, (((1,), (0,)), ((), ())), precision=lax.Precision.HIGHEST,
                          preferred_element_type=F32)
    lane = lax.broadcasted_iota(jnp.int32, (GRID_W, LANES), 1)
    s_ev = LANES - (GRID_W - NA_COLS)
    for kp in range(NA_ROWS // 2):
        tile = rev[:, kp * LANES:(kp + 1) * LANES]
        ev = pltpu.roll(jnp.where(lane < GRID_W, tile, 0.0), s_ev, 1, stride=1, stride_axis=0)
        od = pltpu.roll(jnp.where(lane >= GRID_W, tile, 0.0), s_ev - GRID_W, 1, stride=1, stride_axis=0)
        drb_ref[2 * kp:2 * kp + 1, :] += jnp.sum(ev, axis=0, keepdims=True)
        drb_ref[2 * kp + 1:2 * kp + 2, :] += jnp.sum(od, axis=0, keepdims=True)


def _na_hps():
    return 2 if NA_HEADS % 2 == 0 else 1


def _na_specs(p_offsets):
    t = _t_rows()
    hps = _na_hps()
    wd = hps * NA_DH
    assert all(p_offsets[n] % wd == 0 for n in ("nq", "nk", "nv"))
    qb, kb, vb = (p_offsets[n] // wd for n in ("nq", "nk", "nv"))
    return [pl.BlockSpec((GRID_W, wd), lambda h, r: (r, qb + h)),
            pl.BlockSpec((t, wd), lambda h, r: (0, kb + h)),
            pl.BlockSpec((t, wd), lambda h, r: (0, vb + h)),
            pl.BlockSpec((hps, None, NA_ROWS, LANES), lambda h, r: (h, _na_geometry(r)[2], 0, 0))]


def _na_fwd(p, rb, name):
    t = p.shape[0]
    dh, nl = NA_DH, NA_ROWS * GRID_W

    hps = _na_hps()

    def body(q_ref, k_ref, v_ref, rb_ref, out_ref):
        rq = pl.program_id(1)
        is_lat, kstart, _ = _na_geometry(rq)
        start = pl.multiple_of(CTX_LEN + kstart * GRID_W, GRID_W)
        mask = _na_mask(is_lat)
        for hh in range(hps):
            cols = slice(hh * dh, (hh + 1) * dh)
            out = _na_core(q_ref[:, cols], k_ref[pl.ds(start, nl), cols], v_ref[pl.ds(start, nl), cols],
                           k_ref[0:CTX_LEN, cols], v_ref[0:CTX_LEN, cols], _na_bias(rb_ref.at[hh]), mask)
            out_ref[:, cols] = out.astype(BF16)

    return _call(
        body, name=name, grid=(NA_HEADS // hps, t // GRID_W), in_specs=_na_specs(_offsets()),
        out_specs=[pl.BlockSpec((GRID_W, hps * dh), lambda h, r: (r, h))],
        out_shape=[jax.ShapeDtypeStruct((t, _na_w()), BF16)],
        sem=("parallel", "arbitrary"), args=(p, p, p, rb))[0]


def _na_bwd(p, rb, dmix, name):
    t = p.shape[0]
    dh, nl = NA_DH, NA_ROWS * GRID_W

    hps = _na_hps()
    wd = hps * dh
    assert ((_ret_w() + CONV_W) // dh) % hps == 0
    ob = (_ret_w() + CONV_W) // wd

    def body(q_ref, k_ref, v_ref, rb_ref, dy_ref, dq_ref, dk_ref, dv_ref, drb_ref):
        rq = pl.program_id(1)
        is_lat, kstart, base = _na_geometry(rq)
        _, _, prev_base = _na_geometry(rq - 1)
        start = pl.multiple_of(CTX_LEN + kstart * GRID_W, GRID_W)

        @pl.when(rq == 0)
        def _():
            dk_ref[...] = jnp.zeros_like(dk_ref)
            dv_ref[...] = jnp.zeros_like(dv_ref)

        @pl.when((rq == 0) | (base != prev_base))
        def _():
            drb_ref[...] = jnp.zeros_like(drb_ref)

        mask = _na_mask(is_lat)
        for hh in range(hps):
            cols = slice(hh * dh, (hh + 1) * dh)
            _, vjp = jax.vjp(lambda *a: _na_core(*a, mask), q_ref[:, cols], k_ref[pl.ds(start, nl), cols],
                             v_ref[pl.ds(start, nl), cols], k_ref[0:CTX_LEN, cols], v_ref[0:CTX_LEN, cols],
                             _na_bias(rb_ref.at[hh]))
            dq, dkl, dvl, dkc, dvc, dbias = vjp(dy_ref[:, cols])
            dq_ref[:, cols] = dq
            dk_ref[pl.ds(start, nl), cols] += dkl
            dv_ref[pl.ds(start, nl), cols] += dvl
            dk_ref[0:CTX_LEN, cols] += dkc
            dv_ref[0:CTX_LEN, cols] += dvc
            _na_dbias(dbias, drb_ref.at[hh])

    return _call(
        body, name=name, grid=(NA_HEADS // hps, t // GRID_W),
        in_specs=_na_specs(_offsets()) + [pl.BlockSpec((GRID_W, wd), lambda h, r: (r, ob + h))],
        out_specs=[pl.BlockSpec((GRID_W, wd), lambda h, r: (r, h)), pl.BlockSpec((t, wd), lambda h, r: (0, h)),
                   pl.BlockSpec((t, wd), lambda h, r: (0, h)),
                   pl.BlockSpec((hps, None, NA_ROWS, LANES), lambda h, r: (h, _na_geometry(r)[2], 0, 0))],
        out_shape=[jax.ShapeDtypeStruct((t, _na_w()), F32), jax.ShapeDtypeStruct((t, _na_w()), F32),
                   jax.ShapeDtypeStruct((t, _na_w()), F32),
                   jax.ShapeDtypeStruct((NA_HEADS, NA_ROWS, NA_ROWS, LANES), F32)],
        sem=("parallel", "arbitrary"), args=(p, p, p, rb, dmix))


def _rpb_select():
    sel = np.zeros((2 * NA_ROWS - 1, NA_ROWS * NA_ROWS), np.float32)
    for b in range(NA_ROWS):
        for kh in range(NA_ROWS):
            sel[b + kh, b * NA_ROWS + kh] = 1.0
    return jnp.asarray(sel)


def _rpb_rows(rpb):
    pad = jnp.pad(rpb, ((0, 0), (0, 0), (0, LANES - (2 * NA_COLS - 1))))
    rows = jnp.einsum("rk,hrc->hkc", _rpb_select(), pad, precision=lax.Precision.HIGHEST)
    return rows.reshape(NA_HEADS, NA_ROWS, NA_ROWS, LANES)


def _rpb_rows_t(drb):
    flat = drb.reshape(NA_HEADS, NA_ROWS * NA_ROWS, LANES)
    out = jnp.einsum("rk,hkc->hrc", _rpb_select(), flat, precision=lax.Precision.HIGHEST)
    return out[:, :, :2 * NA_COLS - 1]


def _assemble_dp(d_fwd, d_bwd, dgate, da, db, dnq, dnk, dnv, name):
    t = dgate.shape[0]
    tm = _tm()
    off = _offsets()
    sizes = dict(q=_ret_qk_w(), k=_ret_qk_w(), v=_ret_w(), g=_ret_w(), a=CONV_W, b=CONV_W, nq=_na_w(), nk=_na_w(), nv=_na_w())

    def body(qf_ref, kf_ref, vf_ref, qb_ref, kb_ref, vb_ref, g_ref, a_ref, b_ref, nq_ref, nk_ref, nv_ref, o_ref):
        def put(n, val):
            o_ref[:, off[n]:off[n] + sizes[n]] = val.astype(BF16)

        put("q", qf_ref[...] + qb_ref[...])
        put("k", kf_ref[...] + kb_ref[...])
        put("v", vf_ref[...] + vb_ref[...])
        put("g", g_ref[...])
        put("a", a_ref[...])
        put("b", b_ref[...])
        put("nq", nq_ref[...])
        put("nk", nk_ref[...])
        put("nv", nv_ref[...])

    one = lambda w: pl.BlockSpec((tm, w), lambda i: (i, 0))
    qkv = [one(sizes["q"]), one(sizes["k"]), one(sizes["v"])]
    return pl.pallas_call(
        body, name=name, grid=(t // tm,),
        in_specs=qkv + qkv + [one(sizes["g"]), one(CONV_W), one(CONV_W), one(_na_w()), one(_na_w()), one(_na_w())],
        out_specs=one(_d_in()), out_shape=jax.ShapeDtypeStruct((t, _d_in()), BF16),
        compiler_params=_params(("parallel",)))(*d_fwd, *d_bwd, dgate, da, db, dnq, dnk, dnv)


def _adamw(w, m, v, gs, name):
    nl, r, c = w.shape
    stacked = not isinstance(gs, (list, tuple))
    if stacked:
        gs = [gs]
    assert stacked or len(gs) == nl
    g_n = gs[0].shape[-3]
    block_bytes = 2 * 1024 * 1024
    rows = min(block_bytes // (4 * c), block_bytes // (g_n * c * gs[0].dtype.itemsize))
    tr = _tile(r, max(2 * SUBLANES, rows // (2 * SUBLANES) * (2 * SUBLANES)), 2 * SUBLANES)
    nt = r // tr
    c1 = 1.0 - ADAM_B1 ** ADAM_STEP
    c2 = 1.0 - ADAM_B2 ** ADAM_STEP

    def body(w_ref, m_ref, v_ref, *rest):
        g_refs, (go_ref, d_ref, mo_ref, vo_ref) = rest[:len(gs)], rest[len(gs):]
        layer = pl.program_id(0)
        for ll in range(len(gs)):
            @pl.when(jnp.logical_or(stacked, layer == ll))
            def _():
                g_ref = g_refs[ll]
                g = g_ref[0].astype(F32)
                for j in range(1, g_n):
                    g = g + g_ref[j].astype(F32)
                mn = ADAM_B1 * m_ref[...] + (1.0 - ADAM_B1) * g
                vn = ADAM_B2 * v_ref[...] + (1.0 - ADAM_B2) * (g * g)
                m_hat = mn / c1
                v_hat = vn / c2
                go_ref[...] = g
                d_ref[...] = -ADAM_LR * (m_hat / (jnp.sqrt(v_hat) + ADAM_EPS) + ADAM_WD * w_ref[...])
                mo_ref[...] = mn
                vo_ref[...] = vn

    def g_spec(ll):
        if stacked:
            return pl.BlockSpec((None, g_n, tr, c), lambda l, i: (l, 0, i, 0))
        return pl.BlockSpec((g_n, tr, c), lambda l, i: (0, jnp.where(l == ll, i, jnp.where(l < ll, 0, nt - 1)), 0))

    blk = pl.BlockSpec((None, tr, c), lambda l, i: (l, i, 0))
    sds = jax.ShapeDtypeStruct((nl, r, c), F32)
    return _call(
        body, name=name, grid=(nl, nt),
        in_specs=[blk, blk, blk] + [g_spec(ll) for ll in range(len(gs))],
        out_specs=[blk, blk, blk, blk], out_shape=[sds, sds, sds, sds],
        sem=("arbitrary", "arbitrary"), args=(w, m, v, *gs))


def _sum_devices(g, name):
    _, r, c = g.shape
    tr = _tile(r, 512, SUBLANES)

    def body(g_ref, o_ref):
        acc = g_ref[0]
        for j in range(1, N_DEV):
            acc = acc + g_ref[j]
        o_ref[...] = acc

    return pl.pallas_call(body, name=name, grid=(r // tr,), in_specs=[pl.BlockSpec((N_DEV, tr, c), lambda i: (0, i, 0))],
                          out_specs=pl.BlockSpec((tr, c), lambda i: (i, 0)), out_shape=jax.ShapeDtypeStruct((r, c), F32),
                          compiler_params=_params(("parallel",)))(g)


def _ada_fwd(c16, w_ada, b_shard, name):
    nl, d, cs = w_ada.shape
    tk = _tile(d, 512, LANES)
    nk = d // tk

    def body(c_ref, w_ref, b_ref, o_ref):
        kk = pl.program_id(1)

        @pl.when(kk == 0)
        def _():
            o_ref[...] = jnp.broadcast_to(b_ref[...], o_ref.shape)

        o_ref[...] += _dg(_silu(c_ref[...]), w_ref[...], 1, 0)

    return pl.pallas_call(
        body, name=name, grid=(nl, nk),
        in_specs=[pl.BlockSpec((16, tk), lambda l, kk: (0, kk)), pl.BlockSpec((None, tk, cs), lambda l, kk: (l, kk, 0)),
                  pl.BlockSpec((None, 1, cs), lambda l, kk: (l, 0, 0))],
        out_specs=pl.BlockSpec((None, 16, cs), lambda l, kk: (l, 0, 0)),
        out_shape=jax.ShapeDtypeStruct((nl, 16, cs), F32),
        compiler_params=_params(("parallel", "arbitrary")))(c16, w_ada, b_shard)


def _ada_bwd(c16, dm16, w_ada, name):
    nl, d, cs = w_ada.shape
    td = _tile(d, 512, LANES)

    def body(c_ref, dm_ref, w_ref, gw_ref, dc_ref):
        cv = c_ref[...]
        s, vjp = jax.vjp(_silu, cv)
        gw_ref[...] = _dg(s, dm_ref[...], 0, 0)
        ds = _dg(dm_ref[...], w_ref[...], 1, 1)
        dc_ref[...] = vjp(ds)[0]

    return pl.pallas_call(
        body, name=name, grid=(nl, d // td),
        in_specs=[pl.BlockSpec((16, td), lambda l, i: (0, i)), pl.BlockSpec((None, 16, cs), lambda l, i: (l, 0, 0)),
                  pl.BlockSpec((None, td, cs), lambda l, i: (l, i, 0))],
        out_specs=[pl.BlockSpec((None, td, cs), lambda l, i: (l, i, 0)), pl.BlockSpec((None, 16, td), lambda l, i: (l, 0, i))],
        out_shape=[jax.ShapeDtypeStruct((nl, d, cs), F32), jax.ShapeDtypeStruct((nl, 16, d), F32)],
        compiler_params=_params(("parallel", "parallel")))(c16, dm16, w_ada)


def _pack_rows(shape):
    n = int(np.prod(shape))
    return SUBLANES * (-(-n // (LANES * SUBLANES)))


def _pack(arrays, row_align):
    parts, total = [], 0
    for a in arrays:
        flat = a.reshape(-1).astype(F32)
        rows = _pack_rows(a.shape)
        total += rows
        parts += [flat, jnp.zeros((rows * LANES - flat.shape[0],), F32)]
    parts.append(jnp.zeros(((-total % row_align) * LANES,), F32))
    return jnp.concatenate([p for p in parts if p.shape[0]]).reshape(-1, LANES)


def _unpack(packed, shapes):
    out, r = [], 0
    for s in shapes:
        rows = _pack_rows(s)
        out.append(packed[r:r + rows].reshape(-1)[:int(np.prod(s))].reshape(s))
        r += rows
    return out


def _rope_tables():
    half, nf = RET_DK // 2, RET_DK // 4
    pos = jnp.arange(SEQ)
    row = (pos // GRID_W).astype(F32)
    col = (pos % GRID_W).astype(F32)
    inv = ROPE_BASE ** (-jnp.arange(nf, dtype=F32) / nf)
    ar, ac = row[:, None] * inv[None, :], col[:, None] * inv[None, :]
    cos = jnp.concatenate([jnp.cos(ar), jnp.cos(ar), jnp.cos(ac), jnp.cos(ac)], axis=-1)
    sin = jnp.concatenate([-jnp.sin(ar), jnp.sin(ar), -jnp.sin(ac), jnp.sin(ac)], axis=-1)
    cos = jnp.concatenate([jnp.ones((CTX_LEN, RET_DK), F32), cos], axis=0)
    sin = jnp.concatenate([jnp.zeros((CTX_LEN, RET_DK), F32), sin], axis=0)
    return cos, sin


def _layer_fwd(l, x, mod4, w, cst, arrived):
    n = lambda s: f"l{l}_{s}"
    d = D_MODEL
    h1, h1_t = _normmod_fwd(x, w["norm1_g"], mod4, 0, n("norm1"))
    w["w_in"] = _cols_from_shards(arrived("w_in", h1), n("w_in_cols"))
    p = _mm(h1, w["w_in"], n("proj_in"))
    o_f, o_b, states = _ret_fwd(p, cst["cos"], cst["sin"], w["ret_decay"], cst["order"], n("ret_fwd"))
    ret_out = _ggn_fwd(o_f, o_b, p, w["ret_gn_g"], n("ret_gn"))
    u2, conv_out = _conv_fwd(p, w["conv_dw_w"], w["conv_dw_b"], w["conv_ln_g"], w["conv_ln_b"], w["conv_pw"], n("conv_fwd"))
    na_out = _na_fwd(p, w["rb"], n("na_fwd"))
    mix = jnp.concatenate([ret_out, conv_out, na_out], axis=1)
    w["w_out"] = arrived("w_out", mix).reshape(_d_mix(), d)
    g1 = _mm(mix, w["w_out"], n("proj_out"))
    x1 = _gate_res_fwd(x, g1, mod4, 2, n("res1"))
    h2, h2_t = _normmod_fwd(x1, w["norm2_g"], mod4, 1, n("norm2"))
    w["ffn_up"] = arrived("ffn_up", h2)
    u = _mm(h2, w["ffn_up"], n("ffn_up"), b3=True)
    a, a_t = _ffn_act_fwd(u, w["ffn_dw_w"], w["ffn_dw_b"], n("ffn_act"))
    w["ffn_down"] = arrived("ffn_down", a).reshape(D_FF, d)
    f = _mm(a, w["ffn_down"], n("ffn_down"))
    x2 = _gate_res_fwd(x1, f, mod4, 5, n("res2"))
    saved = dict(x=x, h1_t=h1_t, p=p, o_f=o_f, o_b=o_b, states=states, u2=u2, mix=mix, g1=g1, x1=x1, h2_t=h2_t, u=u, a_t=a_t, f=f)
    return x2, saved


def _layer_bwd(l, dx2, s, mod4, w, cst, send):
    n = lambda t: f"l{l}_{t}"
    d = D_MODEL
    dfg, dg2 = _gate_res_bwd(dx2, s["f"], mod4, 5, n("res2_bwd"))
    da = _mm(dfg, w["ffn_down"], n("ffn_down_dx"), tb=True)
    d_ffn_down = _mm(s["a_t"], dfg, n("ffn_down_dw"), out_dtype=BF16, tm_max=DW_TM)
    tok = send(("ffn_down", l), d_ffn_down.reshape(N_DEV, D_FF // N_DEV, d))
    dcv, dcg = _ffn_act_bwd1(s["u"], da, w["ffn_dw_w"], _after(w["ffn_dw_b"], tok), n("ffn_act_bwd"))
    du, d_ffn_dw_w, d_ffn_dw_b = _dwconv_bwd(dcv, dcg, s["u"], w["ffn_dw_w"], n("ffn_dw_bwd"))
    d_ffn_dw_b = d_ffn_dw_b[0]
    dh2 = _mm(du, w["ffn_up"], n("ffn_up_dx"), tb=True, b3=True)
    d_ffn_up = _mm(s["h2_t"], du, n("ffn_up_dw"), out_dtype=BF16, tm_max=DW_TM,
                      o_cs=2 * D_FF // N_DEV)
    tok = send(("ffn_up", l), d_ffn_up)
    (dx1, dn2, dsh2, dsc2) = _normmod_bwd(s["x1"], _after(w["norm2_g"], tok), mod4, 1, dh2, dx2, n("norm2_bwd"))
    dgg, dg1 = _gate_res_bwd(dx1, s["g1"], mod4, 2, n("res1_bwd"))
    dmix = _mm(dgg, w["w_out"], n("proj_out_dx"), tb=True)
    d_w_out = _mm(_transpose_bf16(s["mix"], n("mix_t")), dgg, n("proj_out_dw"), out_dtype=BF16, tm_max=DW_TM)
    tok = send(("w_out", l), d_w_out.reshape(N_DEV, _d_mix() // N_DEV, d))
    do, dgate, dgn = _ggn_bwd(s["o_f"], s["o_b"], s["p"], _after(w["ret_gn_g"], tok), dmix, n("ret_gn_bwd"))
    d_fwd, d_bwd, ddec = _ret_bwd(s["p"], cst["cos"], cst["sin"], w["ret_decay"], cst["order"], s["states"], do, n("ret_bwd"))
    du2, dlng, dlnb, dpw = _conv_bwd1(s["u2"], dmix, w["conv_ln_g"], w["conv_ln_b"], w["conv_pw"], n("conv_bwd1"))
    dca, dcb, ddww, ddwb = _conv_bwd2(du2, s["p"], w["conv_dw_w"], n("conv_bwd2"))
    dnq, dnk, dnv, drb = _na_bwd(s["p"], w["rb"], dmix, n("na_bwd"))
    dp = _assemble_dp(d_fwd, d_bwd, dgate, dca, dcb, dnq, dnk, dnv, n("dproj"))
    h1_t = s["h1_t"]
    half = d // 2
    for i in range(2):
        d_w_in = _mm(h1_t[i * half:(i + 1) * half], dp, n(f"proj_in_dw{i}"), out_dtype=BF16, tm_max=DW_TM,
                        o_cs=_d_in() // N_DEV)
        tok = send(("w_in", l, i), d_w_in)
    dh1 = _mm(dp, w["w_in"], n("proj_in_dx"), tb=True, after=tok)
    (dx, dn1, dsh1, dsc1) = _normmod_bwd(s["x"], _after(w["norm1_g"], tok), mod4, 0, dh1, dx1, n("norm1_bwd"))
    dmod = jnp.concatenate([dsh1, dsc1, dg1, dsh2, dsc2, dg2], axis=1)
    small = dict(norm1_g=dn1[0], ret_decay=ddec[:, :, 0, 0], ret_gn_g=dgn[0], conv_dw_w=ddww, conv_dw_b=ddwb[0],
                 conv_ln_g=dlng[0], conv_ln_b=dlnb[0], conv_pw=dpw, na_rpb=_rpb_rows_t(drb), norm2_g=dn2[0],
                 ffn_dw_w=d_ffn_dw_w, ffn_dw_b=d_ffn_dw_b)
    return dx, dmod, small


def _d_mix():
    return _ret_w() + CONV_W + _na_w()


_SMALL = ["c_ctx", "b_ada", "norm1_g", "ret_decay", "ret_gn_g", "conv_dw_w", "conv_dw_b", "conv_ln_g", "conv_ln_b",
          "conv_pw", "na_rpb", "norm2_g", "ffn_dw_w", "ffn_dw_b", "final_g"]
_SMALL_SHARD_AXIS = {"conv_dw_w": 2, "conv_pw": 1, "ffn_dw_w": 2}


def kernel(x, c, ctx, c_ctx, w_ada, b_ada, norm1_g, w_in, ret_decay, ret_gn_g, conv_dw_w, conv_dw_b, conv_ln_g, conv_ln_b, conv_pw, na_rpb, w_out, norm2_g, ffn_up, ffn_dw_w, ffn_dw_b, ffn_down, final_g, loss_target, m_c_ctx, m_w_ada, m_b_ada, m_norm1_g, m_w_in, m_ret_decay, m_ret_gn_g, m_conv_dw_w, m_conv_dw_b, m_conv_ln_g, m_conv_ln_b, m_conv_pw, m_na_rpb, m_w_out, m_norm2_g, m_ffn_up, m_ffn_dw_w, m_ffn_dw_b, m_ffn_down, m_final_g, v_c_ctx, v_w_ada, v_b_ada, v_norm1_g, v_w_in, v_ret_decay, v_ret_gn_g, v_conv_dw_w, v_conv_dw_b, v_conv_ln_g, v_conv_ln_b, v_conv_pw, v_na_rpb, v_w_out, v_norm2_g, v_ffn_up, v_ffn_dw_w, v_ffn_dw_b, v_ffn_down, v_final_g):
    d, nl = D_MODEL, DEPTH
    cs = 6 * d // N_DEV
    me = _my_index()
    weights = dict(c_ctx=c_ctx, w_ada=w_ada, b_ada=b_ada, norm1_g=norm1_g, w_in=w_in, ret_decay=ret_decay, ret_gn_g=ret_gn_g,
                   conv_dw_w=conv_dw_w, conv_dw_b=conv_dw_b, conv_ln_g=conv_ln_g, conv_ln_b=conv_ln_b, conv_pw=conv_pw,
                   na_rpb=na_rpb, w_out=w_out, norm2_g=norm2_g, ffn_up=ffn_up, ffn_dw_w=ffn_dw_w, ffn_dw_b=ffn_dw_b,
                   ffn_down=ffn_down, final_g=final_g)
    mom = dict(c_ctx=m_c_ctx, w_ada=m_w_ada, b_ada=m_b_ada, norm1_g=m_norm1_g, w_in=m_w_in, ret_decay=m_ret_decay,
               ret_gn_g=m_ret_gn_g, conv_dw_w=m_conv_dw_w, conv_dw_b=m_conv_dw_b, conv_ln_g=m_conv_ln_g,
               conv_ln_b=m_conv_ln_b, conv_pw=m_conv_pw, na_rpb=m_na_rpb, w_out=m_w_out, norm2_g=m_norm2_g,
               ffn_up=m_ffn_up, ffn_dw_w=m_ffn_dw_w, ffn_dw_b=m_ffn_dw_b, ffn_down=m_ffn_down, final_g=m_final_g)
    var = dict(c_ctx=v_c_ctx, w_ada=v_w_ada, b_ada=v_b_ada, norm1_g=v_norm1_g, w_in=v_w_in, ret_decay=v_ret_decay,
               ret_gn_g=v_ret_gn_g, conv_dw_w=v_conv_dw_w, conv_dw_b=v_conv_dw_b, conv_ln_g=v_conv_ln_g,
               conv_ln_b=v_conv_ln_b, conv_pw=v_conv_pw, na_rpb=v_na_rpb, w_out=v_w_out, norm2_g=v_norm2_g,
               ffn_up=v_ffn_up, ffn_dw_w=v_ffn_dw_w, ffn_dw_b=v_ffn_dw_b, ffn_down=v_ffn_down, final_g=v_final_g)

    big_names = ["w_in", "w_out", "ffn_up", "ffn_down"]
    shards = {(nm, l): _cast_bf16(weights[nm][l], f"cast_{nm}{l}") for l in range(nl) for nm in big_names}
    small_sharded = _pack([conv_dw_w, conv_pw, ffn_dw_w], SUBLANES)
    c_rows = jnp.pad(c, ((0, SUBLANES - 1), (0, 0)))
    gathered = _run_comm(_Gather([c_rows, small_sharded, shards[("w_in", 0)]]), "gather_first")
    c_all = gathered[0][:, 0, :]
    def whole(rows, shard_shape, axis):
        n_el = int(np.prod(shard_shape))
        parts = rows.reshape(N_DEV, -1)[:, :n_el].reshape((N_DEV,) + tuple(shard_shape))
        parts = jnp.moveaxis(parts, 0, axis)
        return parts.reshape(shard_shape[:axis] + (N_DEV * shard_shape[axis],) + shard_shape[axis + 1:])

    r0 = _pack_rows(conv_dw_w.shape)
    r1 = r0 + _pack_rows(conv_pw.shape)
    r2 = r1 + _pack_rows(ffn_dw_w.shape)
    full_conv_dw_w = whole(gathered[1][:, :r0], conv_dw_w.shape, 2)
    full_conv_pw = whole(gathered[1][:, r0:r1], conv_pw.shape, 1)
    full_ffn_dw_w = whole(gathered[1][:, r1:r2], ffn_dw_w.shape, 2)

    c16 = jnp.concatenate([c_all, jnp.broadcast_to(c_ctx[None, :], (N_DEV, d))], axis=0)
    b_shard = lax.dynamic_slice_in_dim(b_ada, me * cs, cs, axis=1)[:, None, :]
    m_shard = _ada_fwd(c16, w_ada, b_shard, "ada_fwd")
    m_all = _run_comm(_Gather([m_shard.reshape(nl * 16, cs)]), "gather_mod")[0]
    m_full = m_all.reshape(N_DEV, nl, 16, cs).transpose(1, 2, 0, 3).reshape(nl, 16, 6 * d)
    m_lat = lax.dynamic_index_in_dim(m_full, me, axis=1, keepdims=False)
    mod = jnp.stack([m_full[:, N_DEV], m_lat], axis=1).reshape(nl, 2, 6, 1, d)

    arriving, token = {}, m_all
    for l in range(nl):
        for nm in big_names:
            if (nm, l) != ("w_in", 0):
                arriving[(nm, l)], token = _split_start(shards[(nm, l)], True, f"gather_{nm}{l}", token)
    mod = _after(mod, token)

    cos, sin = _rope_tables()
    cst = dict(cos=cos, sin=sin, order=_chunk_order())
    layer_w = []
    for l in range(nl):
        layer_w.append(dict(
            norm1_g=norm1_g[l][None], norm2_g=norm2_g[l][None], ret_decay=ret_decay[l], ret_gn_g=ret_gn_g[l][None],
            conv_dw_w=full_conv_dw_w[l], conv_dw_b=conv_dw_b[l][None], conv_ln_g=conv_ln_g[l][None],
            conv_ln_b=conv_ln_b[l][None], conv_pw=full_conv_pw[l], rb=_rpb_rows(na_rpb[l]),
            ffn_dw_w=full_ffn_dw_w[l], ffn_dw_b=ffn_dw_b[l][None]))

    xs = jnp.concatenate([ctx[0], x[0]], axis=0)
    saved = []
    for l in range(nl):
        def arrived(nm, after, l=l):
            if (nm, l) == ("w_in", 0):
                return gathered[2]
            return _split_wait(arriving[(nm, l)], after, f"arrived_{nm}{l}")

        xs, sv = _layer_fwd(l, xs, mod[l], layer_w[l], cst, arrived)
        saved.append(sv)
    loss_tile, dxs, dfinal = _loss_head(xs, final_g[None], loss_target[0], "loss_head")
    loss = lax.psum(loss_tile[0, 0], ("x", "y", "c"))

    dmods, smalls = [None] * nl, [None] * nl
    leaving, last = {}, [loss_tile]

    def send(key, partial):
        leaving[key], token = _split_start(partial, False, "send_" + "_".join(str(k) for k in key), last[0])
        last[0] = token
        return token

    per_layer = [nm for nm in _SMALL if nm not in ("c_ctx", "b_ada", "final_g")]
    small_packs, small_arriving = [None] * nl, [None] * nl
    for l in reversed(range(nl)):
        dxs, dmods[l], smalls[l] = _layer_bwd(l, dxs, saved[l], mod[l], layer_w[l], cst, send)
        small_packs[l] = _pack([smalls[l][nm] for nm in per_layer], 512)
        if l > 0:
            small_arriving[l], last[0] = _split_start(small_packs[l], True, f"gather_small_grads{l}", last[0])
    grad_x = dxs[CTX_LEN:][None]

    arrived_grad = lambda key, after: _split_wait(leaving[key], after, "got_" + "_".join(str(k) for k in key))
    out_big = {}
    after = dxs
    for nm in ["ffn_down", "ffn_up", "w_out"]:
        out_big[nm] = _adamw(weights[nm], mom[nm], var[nm], [arrived_grad((nm, l), after) for l in range(nl)],
                                f"adamw_{nm}")
        after = out_big[nm][0]

    dm_mine = jnp.stack(dmods).reshape(nl * 2, 6 * d)
    dm_rows = jnp.pad(dm_mine, ((0, SUBLANES - nl * 2), (0, 0)))
    dm_all = _run_comm(_Gather([dm_rows]), "gather_dmod", after=after)[0][:, :nl * 2].reshape(N_DEV, nl, 2, 6 * d)
    dm16_full = jnp.concatenate([dm_all[:, :, 1].transpose(1, 0, 2), dm_all[:, :, 0].transpose(1, 0, 2)], axis=1)
    dm16 = lax.dynamic_slice_in_dim(dm16_full, me * cs, cs, axis=2)
    g_w_ada, dc16 = _ada_bwd(c16, dm16, w_ada, "ada_bwd")

    shared = dict(c_ctx=jnp.sum(dc16[:, N_DEV:], axis=(0, 1)),
                  b_ada=jnp.sum(jnp.stack(dmods).reshape(nl, 2, 6 * d), axis=1), final_g=dfinal[0])
    shared_all = _run_comm(_Gather([_pack(list(shared.values()), SUBLANES)]), "gather_shared_grads")[0]
    small_arriving[0], token = _split_start(small_packs[0], True, "gather_small_grads0", shared_all)

    out_big["w_ada"] = _adamw(w_ada, m_w_ada, v_w_ada, g_w_ada[:, None], "adamw_w_ada")
    halves = lambda a: a.reshape(2 * nl, d // 2, a.shape[2])
    res = _adamw(halves(w_in), halves(m_w_in), halves(v_w_in),
                    [arrived_grad(("w_in", l, i), token) for l in range(nl) for i in range(2)], "adamw_w_in")
    out_big["w_in"] = [r.reshape(w_in.shape) for r in res]

    g_small = dict(zip(shared, _unpack(_sum_devices(shared_all, "sum_shared_grads"), [v.shape for v in shared.values()])))
    per = []
    for l in range(nl):
        got = _split_wait(small_arriving[l], res[0], f"arrived_small_grads{l}")
        per.append(_unpack(_sum_devices(got, f"sum_small_grads{l}"), [smalls[l][nm].shape for nm in per_layer]))
    g_small.update({nm: jnp.stack([per[l][i] for l in range(nl)]) for i, nm in enumerate(per_layer)})
    for nm, ax in _SMALL_SHARD_AXIS.items():
        n_sh = weights[nm].shape[ax]
        g_small[nm] = lax.dynamic_slice_in_dim(g_small[nm], me * n_sh, n_sh, axis=ax)
    shapes_own = [weights[nm].shape for nm in _SMALL]
    pk = lambda src: _pack([src[nm] for nm in _SMALL], 2 * SUBLANES)[None]
    res_small = _adamw(pk(weights), pk(mom), pk(var), pk(g_small)[:, None], "adamw_small")
    out_small = [dict(zip(_SMALL, _unpack(r[0], shapes_own))) for r in res_small]

    names = ["c_ctx", "w_ada", "b_ada", "norm1_g", "w_in", "ret_decay", "ret_gn_g", "conv_dw_w", "conv_dw_b", "conv_ln_g",
             "conv_ln_b", "conv_pw", "na_rpb", "w_out", "norm2_g", "ffn_up", "ffn_dw_w", "ffn_dw_b", "ffn_down", "final_g"]
    outs = [loss, grad_x]
    for kind in range(4):
        for nm in names:
            outs.append(out_big[nm][kind] if nm in out_big else out_small[kind][nm])
    return tuple(outs)
```

```python
import numpy as np
import jax
import jax.numpy as jnp
from jax import lax
from jax.experimental import pallas as pl
from jax.experimental.pallas import tpu as pltpu

D_MODEL = 2048
SEQ = 4096
DEPTH = 2
GRID_W = 64
CTX_LEN = 256
RET_HEADS = 4
RET_DK = 128
RET_DV = 256
RET_CHUNK = 128
CONV_W = 512
CONV_K = 31
NA_HEADS = 4
NA_DH = 128
NA_ROWS = 8
NA_COLS = 16
D_FF = 5632
FFN_K = 3
ROPE_BASE = 10000.0
EPS = 1e-6
ADAM_LR = 0.001
ADAM_B1 = 0.9
ADAM_B2 = 0.999
ADAM_EPS = 1e-08
ADAM_WD = 0.01
ADAM_STEP = 10
N_DEV = 8

LANES = 128
SUBLANES = 8
VMEM_LIMIT = 56 * 1024 * 1024
ROW_CHUNK = 32

F32 = jnp.float32
BF16 = jnp.bfloat16
MESH = pl.DeviceIdType.MESH
NEG = -1e30


def _ret_qk_w():
    return RET_HEADS * RET_DK


def _ret_w():
    return RET_HEADS * RET_DV


def _na_w():
    return NA_HEADS * NA_DH


def _d_in():
    return 2 * _ret_qk_w() + 2 * _ret_w() + 2 * CONV_W + 3 * _na_w()


def _offsets():
    sizes = [_ret_qk_w(), _ret_qk_w(), _ret_w(), _ret_w(), CONV_W, CONV_W, _na_w(), _na_w(), _na_w()]
    offs = [0]
    for s in sizes[:-1]:
        offs.append(offs[-1] + s)
    return dict(zip(["q", "k", "v", "g", "a", "b", "nq", "nk", "nv"], offs))


def _t_rows():
    return CTX_LEN + SEQ


def _tm():
    return CTX_LEN


def _params(sem=None):
    kw = dict(vmem_limit_bytes=VMEM_LIMIT)
    if sem is not None:
        kw["dimension_semantics"] = sem
    return pltpu.CompilerParams(**kw)


def _tile(n, pref, align):
    best = None
    for t in range(align, min(n, pref) + 1, align):
        if n % t == 0:
            best = t
    return best if best is not None else n


def _dg(a, b, ca, cb):
    return lax.dot_general(a.astype(BF16), b.astype(BF16), (((ca,), (cb,)), ((), ())), preferred_element_type=F32)


@jax.custom_vjp
def dot_nn(a, b):
    return _dg(a, b, 1, 0)


dot_nn.defvjp(lambda a, b: (_dg(a, b, 1, 0), (a, b)),
              lambda r, g: (_dg(g, r[1], 1, 1), _dg(r[0], g, 0, 0)))


@jax.custom_vjp
def dot_nt(a, b):
    return _dg(a, b, 1, 1)


dot_nt.defvjp(lambda a, b: (_dg(a, b, 1, 1), (a, b)),
              lambda r, g: (_dg(g, r[1], 1, 0), _dg(g, r[0], 0, 0)))


@jax.custom_vjp
def dot_tn(a, b):
    return _dg(a, b, 0, 0)


dot_tn.defvjp(lambda a, b: (_dg(a, b, 0, 0), (a, b)),
              lambda r, g: (_dg(r[1], g, 1, 1), _dg(r[0], g, 1, 0)))


def _sigmoid(x):
    return 0.5 * jnp.tanh(0.5 * x) + 0.5


def _silu(x):
    return x * _sigmoid(x)


def _my_pos():
    return lax.axis_index("x"), lax.axis_index("y"), lax.axis_index("c")


def _my_index():
    x, y, c = _my_pos()
    return 4 * x + 2 * y + c


_ANY = pl.BlockSpec(memory_space=pl.ANY)


class _Gather:
    def __init__(self, arrays):
        self.arrays = list(arrays)
        n = len(self.arrays)
        self.out_shape = [jax.ShapeDtypeStruct((N_DEV,) + a.shape, a.dtype) for a in self.arrays]
        self.scratch = [pltpu.SemaphoreType.DMA((n, 7)), pltpu.SemaphoreType.DMA((n, 7)), pltpu.SemaphoreType.DMA((n,))]

    def _plan(self, xs, outs, sems):
        send_sems, recv_sems, local_sems = sems
        n = len(self.arrays)
        x, y, c = _my_pos()
        me, sibling = (x, y, c), (x, y, 1 - c)
        chips = [(1 - x, y), (x, 1 - y), (1 - x, 1 - y)]

        def slot(a, p):
            return outs[a].at[4 * p[0] + 2 * p[1] + p[2]]

        def copy(a, k, block, to, src=None):
            return pltpu.make_async_remote_copy(
                src_ref=slot(a, block) if src is None else src, dst_ref=slot(a, block),
                send_sem=send_sems.at[a, k], recv_sem=recv_sems.at[a, k], device_id=to, device_id_type=MESH)

        mine = [pltpu.make_async_copy(xs[a], slot(a, me), local_sems.at[a]) for a in range(n)]
        first = []
        for a in range(n):
            first.append(copy(a, 0, me, sibling, src=xs[a]))
            first += [copy(a, 1 + j, me, (*chip, c), src=xs[a]) for j, chip in enumerate(chips)]
        return n, c, me, sibling, chips, copy, mine, first

    def start(self, xs, outs, sems):
        _, _, _, _, _, _, mine, first = self._plan(xs, outs, sems)
        for m in mine:
            m.start()
        for cp in first:
            cp.start()

    def finish(self, xs, outs, sems):
        n, c, me, sibling, chips, copy, mine, first = self._plan(xs, outs, sems)
        passed = []
        for a in range(n):
            for j, chip in enumerate(chips):
                copy(a, 1 + j, (*chip, c), me).wait_recv()
                p = copy(a, 4 + j, (*chip, c), sibling)
                p.start()
                passed.append(p)
        for a in range(n):
            copy(a, 0, sibling, me).wait_recv()
            for j, chip in enumerate(chips):
                copy(a, 4 + j, (*chip, 1 - c), me).wait_recv()
        for cp in first + passed:
            cp.wait_send()
        for m in mine:
            m.wait()


def _run_comm(comm, name, after=None):
    n = len(comm.arrays)
    extra = [] if after is None else [after]

    def body(*refs):
        xs, outs, sems = refs[:n], refs[n + len(extra):2 * n + len(extra)], refs[2 * n + len(extra):]
        comm.start(xs, outs, sems)
        comm.finish(xs, outs, sems)

    return pl.pallas_call(body, name=name, out_shape=comm.out_shape, in_specs=[_ANY] * (n + len(extra)),
                          out_specs=[_ANY] * n, scratch_shapes=comm.scratch)(*comm.arrays, *extra)


_HBM = pl.BlockSpec(memory_space=pltpu.HBM)
_SEMS = pl.BlockSpec(memory_space=pltpu.SEMAPHORE)
_EFFECT = pltpu.SideEffectType.DATAFLOW_SIDE_EFFECTING


def _own_slot(x, gathering, name):
    shape = (N_DEV,) + x.shape if gathering else x.shape
    r, c = shape[1], shape[2]
    tr = _tile(r, 256, 2 * SUBLANES)
    me = jnp.reshape(_my_index(), (1,)).astype(jnp.int32)

    def body(me_ref, x_ref, o_ref):
        o_ref[...] = x_ref[...]

    src = (pl.BlockSpec((tr, c), lambda i, m: (i, 0)) if gathering
           else pl.BlockSpec((None, tr, c), lambda i, m: (m[0], i, 0)))
    grid_spec = pltpu.PrefetchScalarGridSpec(
        num_scalar_prefetch=1, grid=(r // tr,), in_specs=[src],
        out_specs=pl.BlockSpec((None, tr, c), lambda i, m: (m[0], i, 0)))
    return pl.pallas_call(body, name=name, grid_spec=grid_spec, out_shape=jax.ShapeDtypeStruct(shape, x.dtype),
                          compiler_params=_params(("arbitrary",)))(me, x)


def _split_plan(x_ref, land_ref, send_sems, recv_sems, gathering):
    x, y, c = _my_pos()
    me = 4 * x + 2 * y + c
    sends, recvs = [], []
    for k in range(1, N_DEV):
        px = 1 - x if (k >> 2) & 1 else x
        py = 1 - y if (k >> 1) & 1 else y
        pc = 1 - c if k & 1 else c
        peer = 4 * px + 2 * py + pc
        mine, theirs = (x_ref, x_ref) if gathering else (x_ref.at[peer], x_ref.at[me])
        sends.append(pltpu.make_async_remote_copy(
            src_ref=mine, dst_ref=land_ref.at[me], send_sem=send_sems.at[k - 1], recv_sem=recv_sems.at[k - 1],
            device_id=(px, py, pc), device_id_type=MESH))
        recvs.append(pltpu.make_async_remote_copy(
            src_ref=theirs, dst_ref=land_ref.at[peer], send_sem=send_sems.at[k - 1], recv_sem=recv_sems.at[k - 1],
            device_id=(px, py, pc), device_id_type=MESH))
    return sends, recvs


def _split_start(x, gathering, name, prev):
    land = _own_slot(x, gathering, name + "_own")

    def body(x_ref, land_ref, prev_ref, send_sems, recv_sems, x_thru, land_thru, token):
        sends, _ = _split_plan(x_ref, land_ref, send_sems, recv_sems, gathering)
        for s in sends:
            s.start()
        token[...] = jnp.zeros_like(token)

    sems = pltpu.SemaphoreType.DMA((N_DEV - 1,))
    send_sems, recv_sems, x_thru, land_thru, token = pl.pallas_call(
        body, name=name,
        out_shape=(sems, sems, pltpu.HBM(x.shape, x.dtype), pltpu.HBM(land.shape, land.dtype),
                   jax.ShapeDtypeStruct((SUBLANES, LANES), F32)),
        in_specs=(_HBM, _HBM, _ANY), out_specs=(_SEMS, _SEMS, _HBM, _HBM, pl.BlockSpec(memory_space=pltpu.VMEM)),
        input_output_aliases={0: 2, 1: 3},
        compiler_params=pltpu.CompilerParams(has_side_effects=_EFFECT),
    )(pltpu.with_memory_space_constraint(x, pltpu.HBM), pltpu.with_memory_space_constraint(land, pltpu.HBM), prev)
    return (send_sems, recv_sems, x_thru, land_thru, gathering), token


def _after(a, token):
    return a + token[0, 0].astype(a.dtype)


def _split_wait(handle, after, name):
    send_sems, recv_sems, x_thru, land_thru, gathering = handle

    def body(x_ref, land_ref, send_sems, recv_sems, after_ref, x_dead, got_ref):
        sends, recvs = _split_plan(x_ref, land_ref, send_sems, recv_sems, gathering)
        for s in sends:
            s.wait_send()
        for r in recvs:
            r.wait_recv()

    return pl.pallas_call(
        body, name=name, out_shape=(pltpu.HBM(x_thru.shape, x_thru.dtype), pltpu.HBM(land_thru.shape, land_thru.dtype)),
        in_specs=(_HBM, _HBM, _SEMS, _SEMS, _ANY), out_specs=(_HBM, _HBM), input_output_aliases={0: 0, 1: 1},
        compiler_params=pltpu.CompilerParams(has_side_effects=_EFFECT),
    )(x_thru, land_thru, send_sems, recv_sems, after)[1]


def _call(body, *, name, grid, in_specs, out_specs, out_shape, args, scratch=(), sem=None, after=None):
    if after is None:
        return list(pl.pallas_call(body, name=name, grid=grid, in_specs=list(in_specs), out_specs=list(out_specs),
                                   out_shape=list(out_shape), scratch_shapes=list(scratch),
                                   compiler_params=_params(sem))(*args))
    n_in = len(in_specs)

    def wrapped(*refs):
        body(*refs[:n_in], *refs[n_in + 1:])

    return list(pl.pallas_call(wrapped, name=name, grid=grid, in_specs=list(in_specs) + [_ANY], out_specs=list(out_specs),
                               out_shape=list(out_shape), scratch_shapes=list(scratch),
                               compiler_params=_params(sem))(*args, after))


MM_B_BLOCK_BYTES = 6 * 1024 * 1024
MM_O_BLOCK_BYTES = 13 * 1024 * 1024 // 2


def _mm(a, b, name, tb=False, out_dtype=F32, b3=False, o_cs=None, tm_max=1088, after=None):
    m, k = a.shape
    if b3:
        cs = b.shape[2]
        n, kb = (b.shape[1], N_DEV * cs) if tb else (N_DEV * cs, b.shape[1])
    else:
        n, kb = (b.shape[0], b.shape[1]) if tb else (b.shape[1], b.shape[0])
    assert k == kb, (a.shape, b.shape, tb)
    if b3 and tb:
        tm, tn = _tile(m, 544, 2 * SUBLANES), _tile(n, 256, LANES)

        def body_shards(a_ref, b_ref, o_ref):
            r = None
            for j in range(N_DEV):
                part = lax.dot_general(a_ref[:, j * cs:(j + 1) * cs], b_ref[j], (((1,), (1,)), ((), ())),
                                       preferred_element_type=F32)
                r = part if r is None else r + part
            o_ref[...] = r.astype(o_ref.dtype)

        return _call(
            body_shards, name=name, grid=(m // tm, n // tn),
            in_specs=[pl.BlockSpec((tm, k), lambda i, j: (i, 0)), pl.BlockSpec((N_DEV, tn, cs), lambda i, j: (0, j, 0))],
            out_specs=[pl.BlockSpec((tm, tn), lambda i, j: (i, j))], out_shape=[jax.ShapeDtypeStruct((m, n), out_dtype)],
            sem=("parallel", "parallel"), args=(a, b), after=after)[0]
    tm = _tile(m, tm_max, 2 * SUBLANES)
    tk = k
    if b3:
        tn = cs
    elif o_cs is not None:
        tn = o_cs if o_cs % LANES == 0 else 2 * o_cs
    else:
        tn = _tile(n, min(MM_B_BLOCK_BYTES // (2 * tk), MM_O_BLOCK_BYTES // (4 * tm)), LANES)
    cb = 1 if tb else 0
    dn = (((1,), (cb,)), ((), ()))

    def body_one(a_ref, b_ref, o_ref):
        r = lax.dot_general(a_ref[...], b_ref[...], dn, preferred_element_type=F32)
        if o_cs is None:
            o_ref[...] = r.astype(o_ref.dtype)
        else:
            for j in range(tn // o_cs):
                o_ref[j] = r[:, j * o_cs:(j + 1) * o_cs].astype(o_ref.dtype)

    a_spec = pl.BlockSpec((tm, tk), lambda i, j: (i, 0))
    if b3:
        b_spec = pl.BlockSpec((None, tk, cs), lambda i, j: (j, 0, 0))
    else:
        b_spec = pl.BlockSpec((tn, tk), lambda i, j: (j, 0)) if tb else pl.BlockSpec((tk, tn), lambda i, j: (0, j))
    if o_cs is None:
        o_spec = pl.BlockSpec((tm, tn), lambda i, j: (i, j))
        o_shape = jax.ShapeDtypeStruct((m, n), out_dtype)
    else:
        o_spec = pl.BlockSpec((tn // o_cs, tm, o_cs), lambda i, j: (j, i, 0))
        o_shape = jax.ShapeDtypeStruct((n // o_cs, m, o_cs), out_dtype)
    return _call(
        body_one, name=name, grid=(m // tm, n // tn), in_specs=[a_spec, b_spec], out_specs=[o_spec], out_shape=[o_shape],
        sem=("parallel", "parallel"), args=(a, b), after=after)[0]


def _mm_parts(parts, b, name):
    m, n = parts[0].shape[0], b.shape[1]
    ks = [p.shape[1] for p in parts]
    k = sum(ks)
    assert b.shape[0] == k
    tm = _tile(m, 1088, 2 * SUBLANES)
    tn = _tile(n, min(MM_B_BLOCK_BYTES // (2 * k), MM_O_BLOCK_BYTES // (4 * tm)), LANES)

    def body(*refs):
        a_refs, b_ref, o_ref = refs[:-2], refs[-2], refs[-1]
        r, o = None, 0
        for a_ref, ki in zip(a_refs, ks):
            part = jnp.dot(a_ref[...], b_ref[o:o + ki, :], preferred_element_type=F32)
            r = part if r is None else r + part
            o += ki
        o_ref[...] = r

    return pl.pallas_call(
        body, name=name, grid=(m // tm, n // tn),
        in_specs=[pl.BlockSpec((tm, ki), lambda i, j: (i, 0)) for ki in ks] + [pl.BlockSpec((k, tn), lambda i, j: (0, j))],
        out_specs=pl.BlockSpec((tm, tn), lambda i, j: (i, j)), out_shape=jax.ShapeDtypeStruct((m, n), F32),
        compiler_params=_params(("parallel", "parallel")))(*parts, b)


def _transpose_parts(parts, name):
    t = parts[0].shape[0]
    cs = [p.shape[1] for p in parts]
    tt = _tm()

    def body(*refs):
        o_ref, o = refs[-1], 0
        for a_ref, ci in zip(refs[:-1], cs):
            o_ref[o:o + ci, :] = a_ref[...].T
            o += ci

    return pl.pallas_call(body, name=name, grid=(t // tt,),
                          in_specs=[pl.BlockSpec((tt, ci), lambda i: (i, 0)) for ci in cs],
                          out_specs=pl.BlockSpec((sum(cs), tt), lambda i: (0, i)),
                          out_shape=jax.ShapeDtypeStruct((sum(cs), t), BF16),
                          compiler_params=_params(("parallel",)))(*parts)


DW_TM = 512


def _cast_bf16(x, name):
    r, c = x.shape
    tr = _tile(r, 512, 2 * SUBLANES)

    def body(x_ref, o_ref):
        o_ref[...] = x_ref[...].astype(BF16)

    return pl.pallas_call(body, name=name, grid=(r // tr,), in_specs=[pl.BlockSpec((tr, c), lambda i: (i, 0))],
                          out_specs=pl.BlockSpec((tr, c), lambda i: (i, 0)),
                          out_shape=jax.ShapeDtypeStruct((r, c), BF16), compiler_params=_params(("parallel",)))(x)


def _cols_from_shards(wg, name):
    _, k, cs = wg.shape
    tk = _tile(k, 256, 2 * SUBLANES)

    def body(w_ref, o_ref):
        for j in range(N_DEV):
            o_ref[:, j * cs:(j + 1) * cs] = w_ref[j]

    return pl.pallas_call(body, name=name, grid=(k // tk,),
                          in_specs=[pl.BlockSpec((N_DEV, tk, cs), lambda i: (0, i, 0))],
                          out_specs=pl.BlockSpec((tk, N_DEV * cs), lambda i: (i, 0)),
                          out_shape=jax.ShapeDtypeStruct((k, N_DEV * cs), wg.dtype),
                          compiler_params=_params(("parallel",)))(wg)


def _stream(i):
    return jnp.minimum(i, 1)


def _normmod(x, g, sh, sc):
    y = x * lax.rsqrt(jnp.mean(x * x, axis=-1, keepdims=True) + EPS)
    return (y * g) * (1.0 + sc) + sh


def _mod_spec(chunk, d):
    return pl.BlockSpec((None, None, 1, d), lambda i: (_stream(i), chunk, 0, 0))


def _normmod_fwd(x, g, mod4, which, name):
    t, d = x.shape
    tm = _tm()
    ish, isc = (0, 1) if which == 0 else (3, 4)

    def body(x_ref, g_ref, sh_ref, sc_ref, o_ref, ot_ref):
        h = _normmod(x_ref[...], g_ref[...], sh_ref[...], sc_ref[...]).astype(BF16)
        o_ref[...] = h
        ot_ref[...] = h.T

    row = pl.BlockSpec((tm, d), lambda i: (i, 0))
    return pl.pallas_call(body, name=name, grid=(t // tm,),
                          in_specs=[row, pl.BlockSpec((1, d), lambda i: (0, 0)), _mod_spec(ish, d), _mod_spec(isc, d)],
                          out_specs=[row, pl.BlockSpec((d, tm), lambda i: (0, i))],
                          out_shape=[jax.ShapeDtypeStruct((t, d), BF16), jax.ShapeDtypeStruct((d, t), BF16)],
                          compiler_params=_params(("parallel",)))(x, g, mod4, mod4)


def _normmod_bwd(x, g, mod4, which, dh, dres, name):
    t, d = x.shape
    tm = _tm()
    ish, isc = (0, 1) if which == 0 else (3, 4)

    def body(x_ref, g_ref, sh_ref, sc_ref, dh_ref, dres_ref, dx_ref, dg_ref, dsh_ref, dsc_ref):
        i = pl.program_id(0)
        _, vjp = jax.vjp(_normmod, x_ref[...], g_ref[...], sh_ref[...], sc_ref[...])
        dx, dg, dsh, dsc = vjp(dh_ref[...])
        dx_ref[...] = dres_ref[...] + dx

        @pl.when(i == 0)
        def _():
            dg_ref[...] = jnp.zeros_like(dg_ref)

        @pl.when(i <= 1)
        def _():
            dsh_ref[...] = jnp.zeros_like(dsh_ref)
            dsc_ref[...] = jnp.zeros_like(dsc_ref)

        dg_ref[...] += dg
        dsh_ref[...] += dsh
        dsc_ref[...] += dsc

    row = pl.BlockSpec((tm, d), lambda i: (i, 0))
    vec = pl.BlockSpec((1, d), lambda i: (0, 0))
    svec = pl.BlockSpec((None, 1, d), lambda i: (_stream(i), 0, 0))
    return _call(
        body, name=name, grid=(t // tm,),
        in_specs=[row, vec, _mod_spec(ish, d), _mod_spec(isc, d), row, row],
        out_specs=[row, vec, svec, svec],
        out_shape=[jax.ShapeDtypeStruct((t, d), F32), jax.ShapeDtypeStruct((1, d), F32),
                   jax.ShapeDtypeStruct((2, 1, d), F32), jax.ShapeDtypeStruct((2, 1, d), F32)],
        sem=("arbitrary",), args=(x, g, mod4, mod4, dh, dres))


def _gate_res_fwd(x, f, mod4, chunk, name):
    t, d = x.shape
    tm = _tm()

    def body(x_ref, f_ref, g_ref, o_ref):
        o_ref[...] = x_ref[...] + g_ref[...] * f_ref[...]

    row = pl.BlockSpec((tm, d), lambda i: (i, 0))
    return pl.pallas_call(body, name=name, grid=(t // tm,), in_specs=[row, row, _mod_spec(chunk, d)], out_specs=row,
                          out_shape=jax.ShapeDtypeStruct((t, d), F32), compiler_params=_params(("parallel",)))(x, f, mod4)


def _gate_res_bwd(dx, f, mod4, chunk, name):
    t, d = dx.shape
    tm = _tm()

    def body(dx_ref, f_ref, g_ref, o_ref, dg_ref):
        i = pl.program_id(0)
        dxv = dx_ref[...]
        o_ref[...] = (dxv * g_ref[...]).astype(BF16)

        @pl.when(i <= 1)
        def _():
            dg_ref[...] = jnp.zeros_like(dg_ref)

        dg_ref[...] += jnp.sum(dxv * f_ref[...], axis=0, keepdims=True)

    row = pl.BlockSpec((tm, d), lambda i: (i, 0))
    return pl.pallas_call(
        body, name=name, grid=(t // tm,), in_specs=[row, row, _mod_spec(chunk, d)],
        out_specs=[row, pl.BlockSpec((None, 1, d), lambda i: (_stream(i), 0, 0))],
        out_shape=[jax.ShapeDtypeStruct((t, d), BF16), jax.ShapeDtypeStruct((2, 1, d), F32)],
        compiler_params=_params(("arbitrary",)))(dx, f, mod4)


def _loss_head(x, final_g, target, name):
    t, d = x.shape
    tm = _tm()

    def loss_fn(xv, g, tgt):
        y = (xv * lax.rsqrt(jnp.mean(xv * xv, axis=-1, keepdims=True) + EPS)) * g
        err = y - tgt
        return 0.5 * jnp.sum(jnp.mean(err * err, axis=-1, keepdims=True))

    def body(x_ref, g_ref, t_ref, l_ref, dx_ref, dg_ref):
        i = pl.program_id(0)

        @pl.when(i == 0)
        def _():
            l_ref[...] = jnp.zeros_like(l_ref)
            dg_ref[...] = jnp.zeros_like(dg_ref)
            dx_ref[...] = jnp.zeros_like(dx_ref)

        @pl.when(i > 0)
        def _():
            l, (dx, dg) = jax.value_and_grad(loss_fn, argnums=(0, 1))(x_ref[...], g_ref[...], t_ref[...])
            l_ref[...] += jnp.full(l_ref.shape, l, F32)
            dx_ref[...] = dx
            dg_ref[...] += dg

    row = pl.BlockSpec((tm, d), lambda i: (i, 0))
    vec = pl.BlockSpec((1, d), lambda i: (0, 0))
    return pl.pallas_call(
        body, name=name, grid=(t // tm,),
        in_specs=[row, vec, pl.BlockSpec((tm, d), lambda i: (jnp.maximum(i - 1, 0), 0))],
        out_specs=[pl.BlockSpec((SUBLANES, LANES), lambda i: (0, 0)), row, vec],
        out_shape=[jax.ShapeDtypeStruct((SUBLANES, LANES), F32), jax.ShapeDtypeStruct((t, d), F32),
                   jax.ShapeDtypeStruct((1, d), F32)],
        compiler_params=_params(("arbitrary",)))(x, final_g, target)


def _swap_quarters(x):
    half, nf = RET_DK // 2, RET_DK // 4
    lane = lax.broadcasted_iota(jnp.int32, x.shape, 1)
    return jnp.where((lane % half) < nf, pltpu.roll(x, RET_DK - nf, 1), pltpu.roll(x, nf, 1))


def _rope(x, cos, sin):
    return x * cos + _swap_quarters(x) * sin


def _rope_t(y, cos, sin):
    return y * cos + _swap_quarters(y * sin)


def _ret_consts(d):
    c = RET_CHUNK
    ii = lax.broadcasted_iota(jnp.int32, (c, 1), 0).astype(F32)
    jj = lax.broadcasted_iota(jnp.int32, (1, c), 1).astype(F32)
    fwd = d == 0
    sgn = jnp.where(fwd, 1.0, -1.0).astype(F32)
    pos = jnp.where(fwd, ii, c - 1.0 - ii)
    return sgn * (ii - jj), pos


def _ret_step(lgt, state, q, k, v, diff, pos):
    c = float(RET_CHUNK)
    lg = -(jnp.maximum(-lgt, 0.0) + jnp.log1p(jnp.exp(-jnp.abs(lgt))))
    lower = diff >= 0
    decay = jnp.where(lower, jnp.exp(jnp.where(lower, diff, 0.0) * lg), 0.0)
    xi = jnp.exp((pos + 1.0) * lg)
    zeta = jnp.exp((c - 1.0 - pos) * lg)
    gch = jnp.exp(c * lg)
    inner = dot_nt(q, k) * decay
    out = dot_nn(inner, v) + dot_nn(q, state) * xi
    new_state = state * gch + dot_tn(k * zeta, v)
    return out, new_state


def _chunk_order():
    nc, nch = CTX_LEN // RET_CHUNK, _t_rows() // RET_CHUNK
    fwd = list(range(nch))
    bwd = list(range(nc - 1, -1, -1)) + list(range(nch - 1, nc - 1, -1))
    return jnp.asarray(np.array([fwd, bwd], np.int32))


def _ret_fwd(p, cos, sin, decay, order, name):
    t = p.shape[0]
    c, dk, dv, nh = RET_CHUNK, RET_DK, RET_DV, RET_HEADS
    nch = t // c
    off = _offsets()
    wqk, wv = nh * dk, nh * dv
    assert off["q"] % wqk == 0 and off["k"] % wqk == 0 and off["v"] % wv == 0
    qb, kb, vb = off["q"] // wqk, off["k"] // wqk, off["v"] // wv
    scale = RET_DK ** -0.5

    def body(ord_ref, dec_ref, *refs):
        ins, o_refs, st_ref, state = refs[:10], refs[10:12], refs[12], refs[13]
        s = pl.program_id(0)

        @pl.when(s == 0)
        def _():
            state[...] = jnp.zeros_like(state)

        for d in range(2):
            q_ref, k_ref, v_ref, cos_ref, sin_ref = ins[5 * d:5 * d + 5]
            diff, pos = _ret_consts(d)
            cosv, sinv = cos_ref[...], sin_ref[...]
            for h in range(nh):
                st = state[d, h]
                st_ref[d, h] = st
                lgt = jnp.full((1, 1), dec_ref[d, h], F32)
                q = _rope(q_ref[:, h * dk:(h + 1) * dk], cosv, sinv) * scale
                k = _rope(k_ref[:, h * dk:(h + 1) * dk], cosv, sinv)
                out, ns = _ret_step(lgt, st, q, k, v_ref[:, h * dv:(h + 1) * dv], diff, pos)
                o_refs[d][:, h * dv:(h + 1) * dv] = out
                state[d, h] = ns

    def dir_specs(d):
        return [pl.BlockSpec((c, wqk), lambda s, o: (o[d, s], qb)), pl.BlockSpec((c, wqk), lambda s, o: (o[d, s], kb)),
                pl.BlockSpec((c, wv), lambda s, o: (o[d, s], vb)), pl.BlockSpec((c, dk), lambda s, o: (o[d, s], 0)),
                pl.BlockSpec((c, dk), lambda s, o: (o[d, s], 0))]

    grid_spec = pltpu.PrefetchScalarGridSpec(
        num_scalar_prefetch=1, grid=(nch,),
        in_specs=[pl.BlockSpec(memory_space=pltpu.SMEM)] + dir_specs(0) + dir_specs(1),
        out_specs=[pl.BlockSpec((c, wv), lambda s, o: (o[0, s], 0)), pl.BlockSpec((c, wv), lambda s, o: (o[1, s], 0)),
                   pl.BlockSpec((2, nh, None, dk, dv), lambda s, o: (0, 0, s, 0, 0))],
        scratch_shapes=[pltpu.VMEM((2, nh, dk, dv), F32)])
    return pl.pallas_call(
        body, name=name, grid_spec=grid_spec,
        out_shape=[jax.ShapeDtypeStruct((t, wv), F32), jax.ShapeDtypeStruct((t, wv), F32),
                   jax.ShapeDtypeStruct((2, nh, nch, dk, dv), F32)],
        compiler_params=_params(("arbitrary",)))(order, decay, p, p, p, cos, sin, p, p, p, cos, sin)


def _ret_bwd(p, cos, sin, decay, order, states, do, name):
    t = p.shape[0]
    c, dk, dv, nh = RET_CHUNK, RET_DK, RET_DV, RET_HEADS
    nch = t // c
    off = _offsets()
    wqk, wv = nh * dk, nh * dv
    qb, kb, vb = off["q"] // wqk, off["k"] // wqk, off["v"] // wv
    scale = RET_DK ** -0.5

    def body(ord_ref, dec_ref, *refs):
        ins, st_ref, outs, dd_ref, dstate = refs[:12], refs[12], refs[13:19], refs[19], refs[20]
        s = pl.program_id(0)

        @pl.when(s == 0)
        def _():
            dstate[...] = jnp.zeros_like(dstate)
            dd_ref[...] = jnp.zeros_like(dd_ref)

        for d in range(2):
            q_ref, k_ref, v_ref, cos_ref, sin_ref, do_ref = ins[6 * d:6 * d + 6]
            dq_ref, dk_ref, dv_ref = outs[3 * d:3 * d + 3]
            diff, pos = _ret_consts(d)
            cosv, sinv = cos_ref[...], sin_ref[...]
            for h in range(nh):
                qk, vv = slice(h * dk, (h + 1) * dk), slice(h * dv, (h + 1) * dv)
                lgt = jnp.full((1, 1), dec_ref[d, h], F32)
                q = _rope(q_ref[:, qk], cosv, sinv) * scale
                k = _rope(k_ref[:, qk], cosv, sinv)
                _, vjp = jax.vjp(lambda a, b, cq, ck, cv: _ret_step(a, b, cq, ck, cv, diff, pos),
                                 lgt, st_ref[d, h], q, k, v_ref[:, vv])
                dlgt, dst, dq, dkk, dvv = vjp((do_ref[:, vv], dstate[d, h]))
                dstate[d, h] = dst
                dq_ref[:, qk] = _rope_t(dq * scale, cosv, sinv)
                dk_ref[:, qk] = _rope_t(dkk, cosv, sinv)
                dv_ref[:, vv] = dvv
                dd_ref[d, h] += jnp.broadcast_to(dlgt, (SUBLANES, LANES))

    def chunk(d):
        return lambda s, o: o[d, nch - 1 - s]

    def dir_specs(d):
        at = chunk(d)
        return [pl.BlockSpec((c, wqk), lambda s, o: (at(s, o), qb)), pl.BlockSpec((c, wqk), lambda s, o: (at(s, o), kb)),
                pl.BlockSpec((c, wv), lambda s, o: (at(s, o), vb)), pl.BlockSpec((c, dk), lambda s, o: (at(s, o), 0)),
                pl.BlockSpec((c, dk), lambda s, o: (at(s, o), 0)), pl.BlockSpec((c, wv), lambda s, o: (at(s, o), 0))]

    def dir_outs(d):
        at = chunk(d)
        return [pl.BlockSpec((c, wqk), lambda s, o: (at(s, o), 0)), pl.BlockSpec((c, wqk), lambda s, o: (at(s, o), 0)),
                pl.BlockSpec((c, wv), lambda s, o: (at(s, o), 0))]

    grid_spec = pltpu.PrefetchScalarGridSpec(
        num_scalar_prefetch=1, grid=(nch,),
        in_specs=[pl.BlockSpec(memory_space=pltpu.SMEM)] + dir_specs(0) + dir_specs(1)
        + [pl.BlockSpec((2, nh, None, dk, dv), lambda s, o: (0, 0, nch - 1 - s, 0, 0))],
        out_specs=dir_outs(0) + dir_outs(1) + [pl.BlockSpec((2, nh, SUBLANES, LANES), lambda s, o: (0, 0, 0, 0))],
        scratch_shapes=[pltpu.VMEM((2, nh, dk, dv), F32)])
    qk_sds, v_sds = jax.ShapeDtypeStruct((t, wqk), F32), jax.ShapeDtypeStruct((t, wv), F32)
    res = pl.pallas_call(
        body, name=name, grid_spec=grid_spec,
        out_shape=[qk_sds, qk_sds, v_sds, qk_sds, qk_sds, v_sds, jax.ShapeDtypeStruct((2, nh, SUBLANES, LANES), F32)],
        compiler_params=_params(("arbitrary",)))(order, decay, p, p, p, cos, sin, do, p, p, p, cos, sin, do, states)
    return res[:3], res[3:6], res[6]


def _ggn_head(of, ob, gate, g):
    o = of + ob
    mu = jnp.mean(o, axis=-1, keepdims=True)
    var = jnp.mean(jnp.square(o - mu), axis=-1, keepdims=True)
    return ((o - mu) * lax.rsqrt(var + EPS) * g) * _silu(gate)


def _ggn_fwd(o_f, o_b, p, gn_g, name):
    t = p.shape[0]
    tm, w, dv = _tm(), _ret_w(), RET_DV
    gb = _offsets()["g"] // w

    def body(of_ref, ob_ref, gate_ref, g_ref, out_ref):
        for h in range(RET_HEADS):
            sl = slice(h * dv, (h + 1) * dv)
            out_ref[:, sl] = _ggn_head(of_ref[:, sl], ob_ref[:, sl], gate_ref[:, sl], g_ref[:, sl]).astype(BF16)

    row = pl.BlockSpec((tm, w), lambda i: (i, 0))
    return pl.pallas_call(
        body, name=name, grid=(t // tm,),
        in_specs=[row, row, pl.BlockSpec((tm, w), lambda i: (i, gb)), pl.BlockSpec((1, w), lambda i: (0, 0))],
        out_specs=row, out_shape=jax.ShapeDtypeStruct((t, w), BF16),
        compiler_params=_params(("parallel",)))(o_f, o_b, p, gn_g)


def _ggn_bwd(o_f, o_b, p, gn_g, dmix, name):
    t = p.shape[0]
    tm, w, dv = _tm(), _ret_w(), RET_DV
    gb = _offsets()["g"] // w

    def body(of_ref, ob_ref, gate_ref, g_ref, dy_ref, do_ref, dgate_ref, dg_ref):
        i = pl.program_id(0)

        @pl.when(i == 0)
        def _():
            dg_ref[...] = jnp.zeros_like(dg_ref)

        for h in range(RET_HEADS):
            sl = slice(h * dv, (h + 1) * dv)
            _, vjp = jax.vjp(_ggn_head, of_ref[:, sl], ob_ref[:, sl], gate_ref[:, sl], g_ref[:, sl])
            do, _, dgate, dg = vjp(dy_ref[:, sl])
            do_ref[:, sl] = do
            dgate_ref[:, sl] = dgate
            dg_ref[:, sl] += dg

    row = pl.BlockSpec((tm, w), lambda i: (i, 0))
    return pl.pallas_call(
        body, name=name, grid=(t // tm,),
        in_specs=[row, row, pl.BlockSpec((tm, w), lambda i: (i, gb)), pl.BlockSpec((1, w), lambda i: (0, 0)), row],
        out_specs=[row, row, pl.BlockSpec((1, w), lambda i: (0, 0))],
        out_shape=[jax.ShapeDtypeStruct((t, w), F32), jax.ShapeDtypeStruct((t, w), F32),
                   jax.ShapeDtypeStruct((1, w), F32)],
        compiler_params=_params(("arbitrary",)))(o_f, o_b, p, gn_g, dmix)


def _halo(k):
    return SUBLANES * ((k // 2 + SUBLANES - 1) // SUBLANES)


def _halo_specs(width, colblock, h, tm):
    r = tm // h
    return [pl.BlockSpec((h, width), lambda i, *_: (jnp.maximum(i * r - 1, 0), colblock(*_))),
            pl.BlockSpec((tm, width), lambda i, *_: (i, colblock(*_))),
            pl.BlockSpec((h, width), lambda i, *_: (jnp.minimum((i + 1) * r, (_t_rows() // h) - 1), colblock(*_)))]


def _fill_ext(ext_ref, prev, cur, nxt, i, h, tm):
    nt = _t_rows() // tm
    ext_ref[0:h, :] = jnp.where(i >= 2, prev, 0.0)
    ext_ref[h:h + tm, :] = cur
    ext_ref[h + tm:h + tm + h, :] = jnp.where((i >= 1) & (i <= nt - 2), nxt, 0.0)


def _corr(ext_ref, w_ref, k, h, tm, flip, cols=slice(None)):
    pad = k // 2
    acc = None
    for kk in range(k):
        o = h + (pad - kk if flip else kk - pad)
        term = w_ref[kk:kk + 1, cols] * ext_ref[o:o + tm, cols]
        acc = term if acc is None else acc + term
    return acc


LANE_CHUNK = 512


def _chunks(tm, tc):
    return [(r0, slice(c0, min(c0 + LANE_CHUNK, tc))) for r0 in range(0, tm, ROW_CHUNK) for c0 in range(0, tc, LANE_CHUNK)]


def _conv_post(u2, ln_g, ln_b, pw):
    mu = jnp.mean(u2, axis=-1, keepdims=True)
    var = jnp.mean(jnp.square(u2 - mu), axis=-1, keepdims=True)
    y = (u2 - mu) * lax.rsqrt(var + EPS) * ln_g + ln_b
    return dot_nn(_silu(y), pw)


def _conv_fwd(p, dw_w, dw_b, ln_g, ln_b, pw, name):
    t = p.shape[0]
    tm, w, k = _tm(), CONV_W, CONV_K
    h = _halo(k)
    off = _offsets()
    ab, bb = off["a"] // w, off["b"] // w

    def body(ap, ac, an, bp, bc, bn, w_ref, b_ref, g_ref, beta_ref, pw_ref, u2_ref, out_ref, ext):
        i = pl.program_id(0)
        glu = lambda a, b: a * _sigmoid(b)
        _fill_ext(ext, glu(ap[...], bp[...]), glu(ac[...], bc[...]), glu(an[...], bn[...]), i, h, tm)
        for r0 in range(0, tm, ROW_CHUNK):
            u2_ref[r0:r0 + ROW_CHUNK, :] = _corr(ext, w_ref, k, h + r0, ROW_CHUNK, False) + b_ref[...]
        out_ref[...] = _conv_post(u2_ref[...], g_ref[...], beta_ref[...], pw_ref[...]).astype(BF16)

    vec = pl.BlockSpec((1, w), lambda i: (0, 0))
    row = pl.BlockSpec((tm, w), lambda i: (i, 0))
    return pl.pallas_call(
        body, name=name, grid=(t // tm,),
        in_specs=_halo_specs(w, lambda: ab, h, tm) + _halo_specs(w, lambda: bb, h, tm)
        + [pl.BlockSpec((k, w), lambda i: (0, 0)), vec, vec, vec, pl.BlockSpec((w, w), lambda i: (0, 0))],
        out_specs=[row, row],
        out_shape=[jax.ShapeDtypeStruct((t, w), F32), jax.ShapeDtypeStruct((t, w), BF16)],
        scratch_shapes=[pltpu.VMEM((tm + 2 * h, w), F32)],
        compiler_params=_params(("parallel",)))(p, p, p, p, p, p, dw_w, dw_b, ln_g, ln_b, pw)


def _conv_bwd1(u2, dmix, ln_g, ln_b, pw, name):
    t = u2.shape[0]
    tm, w = _tm(), CONV_W
    cb = _ret_w() // w

    def body(u2_ref, dy_ref, g_ref, beta_ref, pw_ref, du2_ref, dg_ref, db_ref, dpw_ref):
        i = pl.program_id(0)

        @pl.when(i == 0)
        def _():
            dg_ref[...] = jnp.zeros_like(dg_ref)
            db_ref[...] = jnp.zeros_like(db_ref)
            dpw_ref[...] = jnp.zeros_like(dpw_ref)

        _, vjp = jax.vjp(_conv_post, u2_ref[...], g_ref[...], beta_ref[...], pw_ref[...])
        du2, dg, db, dpw = vjp(dy_ref[...])
        du2_ref[...] = du2
        dg_ref[...] += dg
        db_ref[...] += db
        dpw_ref[...] += dpw

    vec = pl.BlockSpec((1, w), lambda i: (0, 0))
    row = pl.BlockSpec((tm, w), lambda i: (i, 0))
    mat = pl.BlockSpec((w, w), lambda i: (0, 0))
    return pl.pallas_call(
        body, name=name, grid=(t // tm,),
        in_specs=[row, pl.BlockSpec((tm, w), lambda i: (i, cb)), vec, vec, mat],
        out_specs=[row, vec, vec, mat],
        out_shape=[jax.ShapeDtypeStruct((t, w), F32), jax.ShapeDtypeStruct((1, w), F32),
                   jax.ShapeDtypeStruct((1, w), F32), jax.ShapeDtypeStruct((w, w), F32)],
        compiler_params=_params(("arbitrary",)))(u2, dmix, ln_g, ln_b, pw)


def _conv_bwd2(du2, p, dw_w, name):
    t = p.shape[0]
    tm, w, k = _tm(), CONV_W, CONV_K
    h = _halo(k)
    pad = k // 2
    off = _offsets()
    ab, bb = off["a"] // w, off["b"] // w

    def body(dp, dc, dn, ap, ac, an, bp, bc, bn, w_ref, da_ref, db_ref, dw_ref, dbias_ref, ext_d, ext_u):
        i = pl.program_id(0)

        @pl.when(i == 0)
        def _():
            dw_ref[...] = jnp.zeros_like(dw_ref)
            dbias_ref[...] = jnp.zeros_like(dbias_ref)

        glu = lambda a, b: a * _sigmoid(b)
        _fill_ext(ext_d, dp[...], dc[...], dn[...], i, h, tm)
        _fill_ext(ext_u, glu(ap[...], bp[...]), glu(ac[...], bc[...]), glu(an[...], bn[...]), i, h, tm)
        chunks = range(0, tm, ROW_CHUNK)
        acc_b = jnp.zeros((ROW_CHUNK, w), F32)
        for r0 in chunks:
            rows = slice(r0, r0 + ROW_CHUNK)
            du = _corr(ext_d, w_ref, k, h + r0, ROW_CHUNK, True)
            sg = _sigmoid(bc[rows, :])
            da_ref[rows, :] = du * sg
            db_ref[rows, :] = du * ac[rows, :] * sg * (1.0 - sg)
            acc_b = acc_b + ext_d[h + r0:h + r0 + ROW_CHUNK, :]
        dbias_ref[...] += jnp.sum(acc_b, axis=0, keepdims=True)
        for kk in range(k):
            acc = jnp.zeros((ROW_CHUNK, w), F32)
            for r0 in chunks:
                o = h + r0 + kk - pad
                acc = acc + ext_d[h + r0:h + r0 + ROW_CHUNK, :] * ext_u[o:o + ROW_CHUNK, :]
            dw_ref[kk:kk + 1, :] += jnp.sum(acc, axis=0, keepdims=True)

    vec = pl.BlockSpec((1, w), lambda i: (0, 0))
    row = pl.BlockSpec((tm, w), lambda i: (i, 0))
    kw = pl.BlockSpec((k, w), lambda i: (0, 0))
    return _call(
        body, name=name, grid=(t // tm,),
        in_specs=_halo_specs(w, lambda: 0, h, tm) + _halo_specs(w, lambda: ab, h, tm)
        + _halo_specs(w, lambda: bb, h, tm) + [kw],
        out_specs=[row, row, kw, vec],
        out_shape=[jax.ShapeDtypeStruct((t, w), F32), jax.ShapeDtypeStruct((t, w), F32),
                   jax.ShapeDtypeStruct((k, w), F32), jax.ShapeDtypeStruct((1, w), F32)],
        scratch=[pltpu.VMEM((tm + 2 * h, w), F32), pltpu.VMEM((tm + 2 * h, w), F32)],
        sem=("arbitrary",), args=(du2, du2, du2, p, p, p, p, p, p, dw_w))


def _ffn_tc():
    return _tile(D_FF, 2816, LANES)


def _ffn_act_fwd(u, dw_w, dw_b, name):
    t = u.shape[0]
    tm, k, tc = _tm(), FFN_K, _ffn_tc()
    h = _halo(k)
    nj = D_FF // tc

    def body(vp, vc, vn, gp, gc, gn, wv, wg, bv, bg, out_ref, out_t_ref, ext_v, ext_g):
        i = pl.program_id(0)
        _fill_ext(ext_v, vp[...], vc[...], vn[...], i, h, tm)
        _fill_ext(ext_g, gp[...], gc[...], gn[...], i, h, tm)
        for r0, cols in _chunks(tm, tc):
            val = _corr(ext_v, wv, k, h + r0, ROW_CHUNK, False, cols) + bv[:, cols]
            gate = _corr(ext_g, wg, k, h + r0, ROW_CHUNK, False, cols) + bg[:, cols]
            out_ref[r0:r0 + ROW_CHUNK, cols] = (_silu(gate) * val).astype(BF16)
        out_t_ref[...] = out_ref[...].T

    wspec = lambda s: pl.BlockSpec((k, tc), lambda i, j: (0, j + s))
    bspec = lambda s: pl.BlockSpec((1, tc), lambda i, j: (0, j + s))
    return pl.pallas_call(
        body, name=name, grid=(t // tm, nj),
        in_specs=_halo_specs(tc, lambda j: j, h, tm) + _halo_specs(tc, lambda j: j + nj, h, tm)
        + [wspec(0), wspec(nj), bspec(0), bspec(nj)],
        out_specs=[pl.BlockSpec((tm, tc), lambda i, j: (i, j)), pl.BlockSpec((tc, tm), lambda i, j: (j, i))],
        out_shape=[jax.ShapeDtypeStruct((t, D_FF), BF16), jax.ShapeDtypeStruct((D_FF, t), BF16)],
        scratch_shapes=[pltpu.VMEM((tm + 2 * h, tc), F32), pltpu.VMEM((tm + 2 * h, tc), F32)],
        compiler_params=_params(("parallel", "parallel")))(u, u, u, u, u, u, dw_w, dw_w, dw_b, dw_b)


def _ffn_act_bwd1(u, da, dw_w, dw_b, name):
    t = u.shape[0]
    tm, k, tc = _tm(), FFN_K, _ffn_tc()
    h = _halo(k)
    nj = D_FF // tc

    def body(vp, vc, vn, gp, gc, gn, wv, wg, bv, bg, da_ref, dv_ref, dg_ref, ext_v, ext_g):
        i = pl.program_id(0)
        _fill_ext(ext_v, vp[...], vc[...], vn[...], i, h, tm)
        _fill_ext(ext_g, gp[...], gc[...], gn[...], i, h, tm)
        for r0, cols in _chunks(tm, tc):
            rows = slice(r0, r0 + ROW_CHUNK)
            val = _corr(ext_v, wv, k, h + r0, ROW_CHUNK, False, cols) + bv[:, cols]
            gate = _corr(ext_g, wg, k, h + r0, ROW_CHUNK, False, cols) + bg[:, cols]
            _, vjp = jax.vjp(lambda a, b: _silu(b) * a, val, gate)
            dval, dgate = vjp(da_ref[rows, cols])
            dv_ref[rows, cols] = dval
            dg_ref[rows, cols] = dgate

    wspec = lambda s: pl.BlockSpec((k, tc), lambda i, j: (0, j + s))
    bspec = lambda s: pl.BlockSpec((1, tc), lambda i, j: (0, j + s))
    dc = pl.pallas_call(
        body, name=name, grid=(t // tm, nj),
        in_specs=_halo_specs(tc, lambda j: j, h, tm) + _halo_specs(tc, lambda j: j + nj, h, tm)
        + [wspec(0), wspec(nj), bspec(0), bspec(nj), pl.BlockSpec((tm, tc), lambda i, j: (i, j))],
        out_specs=[pl.BlockSpec((tm, tc), lambda i, j: (i, j)), pl.BlockSpec((tm, tc), lambda i, j: (i, j))],
        out_shape=[jax.ShapeDtypeStruct((t, D_FF), F32), jax.ShapeDtypeStruct((t, D_FF), F32)],
        scratch_shapes=[pltpu.VMEM((tm + 2 * h, tc), F32), pltpu.VMEM((tm + 2 * h, tc), F32)],
        compiler_params=_params(("parallel", "parallel")))(u, u, u, u, u, u, dw_w, dw_w, dw_b, dw_b, da)
    return dc


def _dwconv_bwd(dcv, dcg, u, dw_w, name):
    t = u.shape[0]
    tm, k, tc = _tm(), FFN_K, _ffn_tc()
    h = _halo(k)
    pad = k // 2
    nj = D_FF // tc

    def body(vp, vc, vn, gp, gc, gn, up, uc, un, w_ref, du_ref, dw_ref, dbias_ref, ext_d, ext_u):
        jj, i = pl.program_id(0), pl.program_id(1)

        @pl.when(i == 0)
        def _():
            dw_ref[...] = jnp.zeros_like(dw_ref)
            dbias_ref[...] = jnp.zeros_like(dbias_ref)

        @pl.when(jj < nj)
        def _():
            _fill_ext(ext_d, vp[...], vc[...], vn[...], i, h, tm)

        @pl.when(jj >= nj)
        def _():
            _fill_ext(ext_d, gp[...], gc[...], gn[...], i, h, tm)

        _fill_ext(ext_u, up[...], uc[...], un[...], i, h, tm)
        for c0 in range(0, tc, LANE_CHUNK):
            cols = slice(c0, min(c0 + LANE_CHUNK, tc))
            width = cols.stop - cols.start
            acc_b = jnp.zeros((ROW_CHUNK, width), F32)
            acc_w = [jnp.zeros((ROW_CHUNK, width), F32) for _ in range(k)]
            for r0 in range(0, tm, ROW_CHUNK):
                d = ext_d[h + r0:h + r0 + ROW_CHUNK, cols]
                du_ref[r0:r0 + ROW_CHUNK, cols] = _corr(ext_d, w_ref, k, h + r0, ROW_CHUNK, True, cols).astype(BF16)
                acc_b = acc_b + d
                for kk in range(k):
                    o = h + r0 + kk - pad
                    acc_w[kk] = acc_w[kk] + d * ext_u[o:o + ROW_CHUNK, cols]
            dbias_ref[:, cols] += jnp.sum(acc_b, axis=0, keepdims=True)
            for kk in range(k):
                dw_ref[kk:kk + 1, cols] += jnp.sum(acc_w[kk], axis=0, keepdims=True)

    def hs(cb, live):
        r = tm // h
        row = lambda j, i: jnp.where(live(j), i, 0)
        return [pl.BlockSpec((h, tc), lambda j, i: (jnp.maximum(row(j, i) * r - 1, 0), cb(j))),
                pl.BlockSpec((tm, tc), lambda j, i: (row(j, i), cb(j))),
                pl.BlockSpec((h, tc), lambda j, i: (jnp.minimum((row(j, i) + 1) * r, (_t_rows() // h) - 1), cb(j)))]

    return pl.pallas_call(
        body, name=name, grid=(2 * nj, t // tm),
        in_specs=hs(lambda j: jnp.minimum(j, nj - 1), lambda j: j < nj)
        + hs(lambda j: jnp.maximum(j - nj, 0), lambda j: j >= nj)
        + hs(lambda j: j, lambda j: True) + [pl.BlockSpec((k, tc), lambda j, i: (0, j))],
        out_specs=[pl.BlockSpec((tm, tc), lambda j, i: (i, j)), pl.BlockSpec((k, tc), lambda j, i: (0, j)),
                   pl.BlockSpec((1, tc), lambda j, i: (0, j))],
        out_shape=[jax.ShapeDtypeStruct((t, 2 * D_FF), BF16), jax.ShapeDtypeStruct((k, 2 * D_FF), F32),
                   jax.ShapeDtypeStruct((1, 2 * D_FF), F32)],
        scratch_shapes=[pltpu.VMEM((tm + 2 * h, tc), F32), pltpu.VMEM((tm + 2 * h, tc), F32)],
        compiler_params=_params(("parallel", "arbitrary")))(dcv, dcv, dcv, dcg, dcg, dcg, u, u, u, dw_w)


def _na_geometry(rq):
    ncb = CTX_LEN // GRID_W
    rows_n = SEQ // GRID_W
    r = jnp.maximum(rq - ncb, 0)
    kstart = jnp.clip(r - NA_ROWS // 2, 0, rows_n - NA_ROWS)
    base = kstart - r + NA_ROWS - 1
    return rq >= ncb, kstart, base


def _na_core(q, kl, vl, kc, vc, bias, mask):
    qs = q * (NA_DH ** -0.5)
    s_l = jnp.where(mask, dot_nt(qs, kl) + bias, NEG)
    s_c = dot_nt(qs, kc)
    m = lax.stop_gradient(jnp.maximum(jnp.max(s_l, axis=1, keepdims=True), jnp.max(s_c, axis=1, keepdims=True)))
    e_l, e_c = jnp.exp(s_l - m), jnp.exp(s_c - m)
    inv = 1.0 / (jnp.sum(e_l, axis=1, keepdims=True) + jnp.sum(e_c, axis=1, keepdims=True))
    return dot_nn(e_l * inv, vl) + dot_nn(e_c * inv, vc)


def _na_mask(is_lat):
    nl = NA_ROWS * GRID_W
    q = lax.broadcasted_iota(jnp.int32, (GRID_W, nl), 0)
    w = lax.broadcasted_iota(jnp.int32, (GRID_W, nl), 1) % GRID_W
    cs = jnp.clip(q - NA_COLS // 2, 0, GRID_W - NA_COLS)
    return (w >= cs) & (w < cs + NA_COLS) & is_lat


def _na_bias(rb_ref):
    assert 2 * GRID_W == LANES
    lane = lax.broadcasted_iota(jnp.int32, (GRID_W, LANES), 1)
    tiles = []
    for kp in range(NA_ROWS // 2):
        ev = jnp.broadcast_to(rb_ref[2 * kp:2 * kp + 1, :], (GRID_W, LANES))
        od = jnp.broadcast_to(rb_ref[2 * kp + 1:2 * kp + 2, :], (GRID_W, LANES))
        ev = pltpu.roll(ev, LANES - (NA_COLS - 1), 1, stride=1, stride_axis=0)
        od = pltpu.roll(od, LANES - (NA_COLS - 1) - GRID_W, 1, stride=1, stride_axis=0)
        tiles.append(jnp.where(lane < GRID_W, ev, od))
    return jnp.concatenate(tiles, axis=1)


def _na_dbias(dbias, drb_ref):
    qi = lax.broadcasted_iota(jnp.int32, (GRID_W, GRID_W), 0)
    qj = lax.broadcasted_iota(jnp.int32, (GRID_W, GRID_W), 1)
    flip = (qi + qj == GRID_W - 1).astype(F32)
    rev = lax.dot_general(flip, dbias, (((1,), (0,)), ((), ())), precision=lax.Precision.HIGHEST,
                          preferred_element_type=F32)
    lane = lax.broadcasted_iota(jnp.int32, (GRID_W, LANES), 1)
    s_ev = LANES - (GRID_W - NA_COLS)
    for kp in range(NA_ROWS // 2):
        tile = rev[:, kp * LANES:(kp + 1) * LANES]
        ev = pltpu.roll(jnp.where(lane < GRID_W, tile, 0.0), s_ev, 1, stride=1, stride_axis=0)
        od = pltpu.roll(jnp.where(lane >= GRID_W, tile, 0.0), s_ev - GRID_W, 1, stride=1, stride_axis=0)
        drb_ref[2 * kp:2 * kp + 1, :] += jnp.sum(ev, axis=0, keepdims=True)
        drb_ref[2 * kp + 1:2 * kp + 2, :] += jnp.sum(od, axis=0, keepdims=True)


def _na_hps():
    return 2 if NA_HEADS % 2 == 0 else 1


def _na_specs(p_offsets):
    t = _t_rows()
    hps = _na_hps()
    wd = hps * NA_DH
    assert all(p_offsets[n] % wd == 0 for n in ("nq", "nk", "nv"))
    qb, kb, vb = (p_offsets[n] // wd for n in ("nq", "nk", "nv"))
    return [pl.BlockSpec((GRID_W, wd), lambda h, r: (r, qb + h)),
            pl.BlockSpec((t, wd), lambda h, r: (0, kb + h)),
            pl.BlockSpec((t, wd), lambda h, r: (0, vb + h)),
            pl.BlockSpec((hps, None, NA_ROWS, LANES), lambda h, r: (h, _na_geometry(r)[2], 0, 0))]


def _na_fwd(p, rb, name):
    t = p.shape[0]
    dh, nl = NA_DH, NA_ROWS * GRID_W

    hps = _na_hps()

    def body(q_ref, k_ref, v_ref, rb_ref, out_ref):
        rq = pl.program_id(1)
        is_lat, kstart, _ = _na_geometry(rq)
        start = pl.multiple_of(CTX_LEN + kstart * GRID_W, GRID_W)
        mask = _na_mask(is_lat)
        for hh in range(hps):
            cols = slice(hh * dh, (hh + 1) * dh)
            out = _na_core(q_ref[:, cols], k_ref[pl.ds(start, nl), cols], v_ref[pl.ds(start, nl), cols],
                           k_ref[0:CTX_LEN, cols], v_ref[0:CTX_LEN, cols], _na_bias(rb_ref.at[hh]), mask)
            out_ref[:, cols] = out.astype(BF16)

    return _call(
        body, name=name, grid=(NA_HEADS // hps, t // GRID_W), in_specs=_na_specs(_offsets()),
        out_specs=[pl.BlockSpec((GRID_W, hps * dh), lambda h, r: (r, h))],
        out_shape=[jax.ShapeDtypeStruct((t, _na_w()), BF16)],
        sem=("parallel", "arbitrary"), args=(p, p, p, rb))[0]


def _na_bwd(p, rb, dmix, name):
    t = p.shape[0]
    dh, nl = NA_DH, NA_ROWS * GRID_W

    hps = _na_hps()
    wd = hps * dh
    assert ((_ret_w() + CONV_W) // dh) % hps == 0
    ob = (_ret_w() + CONV_W) // wd

    def body(q_ref, k_ref, v_ref, rb_ref, dy_ref, dq_ref, dk_ref, dv_ref, drb_ref):
        rq = pl.program_id(1)
        is_lat, kstart, base = _na_geometry(rq)
        _, _, prev_base = _na_geometry(rq - 1)
        start = pl.multiple_of(CTX_LEN + kstart * GRID_W, GRID_W)

        @pl.when(rq == 0)
        def _():
            dk_ref[...] = jnp.zeros_like(dk_ref)
            dv_ref[...] = jnp.zeros_like(dv_ref)

        @pl.when((rq == 0) | (base != prev_base))
        def _():
            drb_ref[...] = jnp.zeros_like(drb_ref)

        mask = _na_mask(is_lat)
        for hh in range(hps):
            cols = slice(hh * dh, (hh + 1) * dh)
            _, vjp = jax.vjp(lambda *a: _na_core(*a, mask), q_ref[:, cols], k_ref[pl.ds(start, nl), cols],
                             v_ref[pl.ds(start, nl), cols], k_ref[0:CTX_LEN, cols], v_ref[0:CTX_LEN, cols],
                             _na_bias(rb_ref.at[hh]))
            dq, dkl, dvl, dkc, dvc, dbias = vjp(dy_ref[:, cols])
            dq_ref[:, cols] = dq
            dk_ref[pl.ds(start, nl), cols] += dkl
            dv_ref[pl.ds(start, nl), cols] += dvl
            dk_ref[0:CTX_LEN, cols] += dkc
            dv_ref[0:CTX_LEN, cols] += dvc
            _na_dbias(dbias, drb_ref.at[hh])

    return _call(
        body, name=name, grid=(NA_HEADS // hps, t // GRID_W),
        in_specs=_na_specs(_offsets()) + [pl.BlockSpec((GRID_W, wd), lambda h, r: (r, ob + h))],
        out_specs=[pl.BlockSpec((GRID_W, wd), lambda h, r: (r, h)), pl.BlockSpec((t, wd), lambda h, r: (0, h)),
                   pl.BlockSpec((t, wd), lambda h, r: (0, h)),
                   pl.BlockSpec((hps, None, NA_ROWS, LANES), lambda h, r: (h, _na_geometry(r)[2], 0, 0))],
        out_shape=[jax.ShapeDtypeStruct((t, _na_w()), F32), jax.ShapeDtypeStruct((t, _na_w()), F32),
                   jax.ShapeDtypeStruct((t, _na_w()), F32),
                   jax.ShapeDtypeStruct((NA_HEADS, NA_ROWS, NA_ROWS, LANES), F32)],
        sem=("parallel", "arbitrary"), args=(p, p, p, rb, dmix))


def _rpb_select():
    sel = np.zeros((2 * NA_ROWS - 1, NA_ROWS * NA_ROWS), np.float32)
    for b in range(NA_ROWS):
        for kh in range(NA_ROWS):
            sel[b + kh, b * NA_ROWS + kh] = 1.0
    return jnp.asarray(sel)


def _rpb_rows(rpb):
    pad = jnp.pad(rpb, ((0, 0), (0, 0), (0, LANES - (2 * NA_COLS - 1))))
    rows = jnp.einsum("rk,hrc->hkc", _rpb_select(), pad, precision=lax.Precision.HIGHEST)
    return rows.reshape(NA_HEADS, NA_ROWS, NA_ROWS, LANES)


def _rpb_rows_t(drb):
    flat = drb.reshape(NA_HEADS, NA_ROWS * NA_ROWS, LANES)
    out = jnp.einsum("rk,hkc->hrc", _rpb_select(), flat, precision=lax.Precision.HIGHEST)
    return out[:, :, :2 * NA_COLS - 1]


def _assemble_dp(d_fwd, d_bwd, dgate, da, db, dnq, dnk, dnv, name):
    t = dgate.shape[0]
    tm = _tm()
    off = _offsets()
    sizes = dict(q=_ret_qk_w(), k=_ret_qk_w(), v=_ret_w(), g=_ret_w(), a=CONV_W, b=CONV_W, nq=_na_w(), nk=_na_w(), nv=_na_w())

    def body(qf_ref, kf_ref, vf_ref, qb_ref, kb_ref, vb_ref, g_ref, a_ref, b_ref, nq_ref, nk_ref, nv_ref, o_ref):
        def put(n, val):
            o_ref[:, off[n]:off[n] + sizes[n]] = val.astype(BF16)

        put("q", qf_ref[...] + qb_ref[...])
        put("k", kf_ref[...] + kb_ref[...])
        put("v", vf_ref[...] + vb_ref[...])
        put("g", g_ref[...])
        put("a", a_ref[...])
        put("b", b_ref[...])
        put("nq", nq_ref[...])
        put("nk", nk_ref[...])
        put("nv", nv_ref[...])

    one = lambda w: pl.BlockSpec((tm, w), lambda i: (i, 0))
    qkv = [one(sizes["q"]), one(sizes["k"]), one(sizes["v"])]
    return pl.pallas_call(
        body, name=name, grid=(t // tm,),
        in_specs=qkv + qkv + [one(sizes["g"]), one(CONV_W), one(CONV_W), one(_na_w()), one(_na_w()), one(_na_w())],
        out_specs=one(_d_in()), out_shape=jax.ShapeDtypeStruct((t, _d_in()), BF16),
        compiler_params=_params(("parallel",)))(*d_fwd, *d_bwd, dgate, da, db, dnq, dnk, dnv)


def _adamw(w, m, v, gs, name):
    nl, r, c = w.shape
    stacked = not isinstance(gs, (list, tuple))
    if stacked:
        gs = [gs]
    assert stacked or len(gs) == nl
    g_n = gs[0].shape[-3]
    block_bytes = 2 * 1024 * 1024
    rows = min(block_bytes // (4 * c), block_bytes // (g_n * c * gs[0].dtype.itemsize))
    tr = _tile(r, max(2 * SUBLANES, rows // (2 * SUBLANES) * (2 * SUBLANES)), 2 * SUBLANES)
    nt = r // tr
    c1 = 1.0 - ADAM_B1 ** ADAM_STEP
    c2 = 1.0 - ADAM_B2 ** ADAM_STEP

    def body(w_ref, m_ref, v_ref, *rest):
        g_refs, (go_ref, d_ref, mo_ref, vo_ref) = rest[:len(gs)], rest[len(gs):]
        layer = pl.program_id(0)
        for ll in range(len(gs)):
            @pl.when(jnp.logical_or(stacked, layer == ll))
            def _():
                g_ref = g_refs[ll]
                g = g_ref[0].astype(F32)
                for j in range(1, g_n):
                    g = g + g_ref[j].astype(F32)
                mn = ADAM_B1 * m_ref[...] + (1.0 - ADAM_B1) * g
                vn = ADAM_B2 * v_ref[...] + (1.0 - ADAM_B2) * (g * g)
                m_hat = mn / c1
                v_hat = vn / c2
                go_ref[...] = g
                d_ref[...] = -ADAM_LR * (m_hat / (jnp.sqrt(v_hat) + ADAM_EPS) + ADAM_WD * w_ref[...])
                mo_ref[...] = mn
                vo_ref[...] = vn

    def g_spec(ll):
        if stacked:
            return pl.BlockSpec((None, g_n, tr, c), lambda l, i: (l, 0, i, 0))
        return pl.BlockSpec((g_n, tr, c), lambda l, i: (0, jnp.where(l == ll, i, jnp.where(l < ll, 0, nt - 1)), 0))

    blk = pl.BlockSpec((None, tr, c), lambda l, i: (l, i, 0))
    sds = jax.ShapeDtypeStruct((nl, r, c), F32)
    return _call(
        body, name=name, grid=(nl, nt),
        in_specs=[blk, blk, blk] + [g_spec(ll) for ll in range(len(gs))],
        out_specs=[blk, blk, blk, blk], out_shape=[sds, sds, sds, sds],
        sem=("arbitrary", "arbitrary"), args=(w, m, v, *gs))


def _sum_devices(g, name):
    _, r, c = g.shape
    tr = _tile(r, 512, SUBLANES)

    def body(g_ref, o_ref):
        acc = g_ref[0]
        for j in range(1, N_DEV):
            acc = acc + g_ref[j]
        o_ref[...] = acc

    return pl.pallas_call(body, name=name, grid=(r // tr,), in_specs=[pl.BlockSpec((N_DEV, tr, c), lambda i: (0, i, 0))],
                          out_specs=pl.BlockSpec((tr, c), lambda i: (i, 0)), out_shape=jax.ShapeDtypeStruct((r, c), F32),
                          compiler_params=_params(("parallel",)))(g)


def _ada_fwd(c16, w_ada, b_shard, name):
    nl, d, cs = w_ada.shape
    tk = _tile(d, 512, LANES)
    nk = d // tk

    def body(c_ref, w_ref, b_ref, o_ref):
        kk = pl.program_id(1)

        @pl.when(kk == 0)
        def _():
            o_ref[...] = jnp.broadcast_to(b_ref[...], o_ref.shape)

        o_ref[...] += _dg(_silu(c_ref[...]), w_ref[...], 1, 0)

    return pl.pallas_call(
        body, name=name, grid=(nl, nk),
        in_specs=[pl.BlockSpec((16, tk), lambda l, kk: (0, kk)), pl.BlockSpec((None, tk, cs), lambda l, kk: (l, kk, 0)),
                  pl.BlockSpec((None, 1, cs), lambda l, kk: (l, 0, 0))],
        out_specs=pl.BlockSpec((None, 16, cs), lambda l, kk: (l, 0, 0)),
        out_shape=jax.ShapeDtypeStruct((nl, 16, cs), F32),
        compiler_params=_params(("parallel", "arbitrary")))(c16, w_ada, b_shard)


def _ada_bwd(c16, dm16, w_ada, name):
    nl, d, cs = w_ada.shape
    td = _tile(d, 512, LANES)

    def body(c_ref, dm_ref, w_ref, gw_ref, dc_ref):
        cv = c_ref[...]
        s, vjp = jax.vjp(_silu, cv)
        gw_ref[...] = _dg(s, dm_ref[...], 0, 0)
        ds = _dg(dm_ref[...], w_ref[...], 1, 1)
        dc_ref[...] = vjp(ds)[0]

    return pl.pallas_call(
        body, name=name, grid=(nl, d // td),
        in_specs=[pl.BlockSpec((16, td), lambda l, i: (0, i)), pl.BlockSpec((None, 16, cs), lambda l, i: (l, 0, 0)),
                  pl.BlockSpec((None, td, cs), lambda l, i: (l, i, 0))],
        out_specs=[pl.BlockSpec((None, td, cs), lambda l, i: (l, i, 0)), pl.BlockSpec((None, 16, td), lambda l, i: (l, 0, i))],
        out_shape=[jax.ShapeDtypeStruct((nl, d, cs), F32), jax.ShapeDtypeStruct((nl, 16, d), F32)],
        compiler_params=_params(("parallel", "parallel")))(c16, dm16, w_ada)


def _pack_rows(shape):
    n = int(np.prod(shape))
    return SUBLANES * (-(-n // (LANES * SUBLANES)))


def _pack(arrays, row_align):
    parts, total = [], 0
    for a in arrays:
        flat = a.reshape(-1).astype(F32)
        rows = _pack_rows(a.shape)
        total += rows
        parts += [flat, jnp.zeros((rows * LANES - flat.shape[0],), F32)]
    parts.append(jnp.zeros(((-total % row_align) * LANES,), F32))
    return jnp.concatenate([p for p in parts if p.shape[0]]).reshape(-1, LANES)


def _unpack(packed, shapes):
    out, r = [], 0
    for s in shapes:
        rows = _pack_rows(s)
        out.append(packed[r:r + rows].reshape(-1)[:int(np.prod(s))].reshape(s))
        r += rows
    return out


def _rope_tables():
    half, nf = RET_DK // 2, RET_DK // 4
    pos = jnp.arange(SEQ)
    row = (pos // GRID_W).astype(F32)
    col = (pos % GRID_W).astype(F32)
    inv = ROPE_BASE ** (-jnp.arange(nf, dtype=F32) / nf)
    ar, ac = row[:, None] * inv[None, :], col[:, None] * inv[None, :]
    cos = jnp.concatenate([jnp.cos(ar), jnp.cos(ar), jnp.cos(ac), jnp.cos(ac)], axis=-1)
    sin = jnp.concatenate([-jnp.sin(ar), jnp.sin(ar), -jnp.sin(ac), jnp.sin(ac)], axis=-1)
    cos = jnp.concatenate([jnp.ones((CTX_LEN, RET_DK), F32), cos], axis=0)
    sin = jnp.concatenate([jnp.zeros((CTX_LEN, RET_DK), F32), sin], axis=0)
    return cos, sin


def _layer_fwd(l, x, mod4, w, cst, arrived):
    n = lambda s: f"l{l}_{s}"
    d = D_MODEL
    h1, h1_t = _normmod_fwd(x, w["norm1_g"], mod4, 0, n("norm1"))
    w["w_in"] = _cols_from_shards(arrived("w_in", h1), n("w_in_cols"))
    p = _mm(h1, w["w_in"], n("proj_in"))
    o_f, o_b, states = _ret_fwd(p, cst["cos"], cst["sin"], w["ret_decay"], cst["order"], n("ret_fwd"))
    ret_out = _ggn_fwd(o_f, o_b, p, w["ret_gn_g"], n("ret_gn"))
    u2, conv_out = _conv_fwd(p, w["conv_dw_w"], w["conv_dw_b"], w["conv_ln_g"], w["conv_ln_b"], w["conv_pw"], n("conv_fwd"))
    na_out = _na_fwd(p, w["rb"], n("na_fwd"))
    mix = [ret_out, conv_out, na_out]
    w["w_out"] = arrived("w_out", na_out).reshape(_d_mix(), d)
    g1 = _mm_parts(mix, w["w_out"], n("proj_out"))
    x1 = _gate_res_fwd(x, g1, mod4, 2, n("res1"))
    h2, h2_t = _normmod_fwd(x1, w["norm2_g"], mod4, 1, n("norm2"))
    w["ffn_up"] = arrived("ffn_up", h2)
    u = _mm(h2, w["ffn_up"], n("ffn_up"), b3=True)
    a, a_t = _ffn_act_fwd(u, w["ffn_dw_w"], w["ffn_dw_b"], n("ffn_act"))
    w["ffn_down"] = arrived("ffn_down", a).reshape(D_FF, d)
    f = _mm(a, w["ffn_down"], n("ffn_down"))
    x2 = _gate_res_fwd(x1, f, mod4, 5, n("res2"))
    saved = dict(x=x, h1_t=h1_t, p=p, o_f=o_f, o_b=o_b, states=states, u2=u2, mix=mix, g1=g1, x1=x1, h2_t=h2_t, u=u, a_t=a_t, f=f)
    return x2, saved


def _layer_bwd(l, dx2, s, mod4, w, cst, send):
    n = lambda t: f"l{l}_{t}"
    d = D_MODEL
    dfg, dg2 = _gate_res_bwd(dx2, s["f"], mod4, 5, n("res2_bwd"))
    da = _mm(dfg, w["ffn_down"], n("ffn_down_dx"), tb=True)
    d_ffn_down = _mm(s["a_t"], dfg, n("ffn_down_dw"), out_dtype=BF16, tm_max=DW_TM)
    tok = send(("ffn_down", l), d_ffn_down.reshape(N_DEV, D_FF // N_DEV, d))
    dcv, dcg = _ffn_act_bwd1(s["u"], da, w["ffn_dw_w"], _after(w["ffn_dw_b"], tok), n("ffn_act_bwd"))
    du, d_ffn_dw_w, d_ffn_dw_b = _dwconv_bwd(dcv, dcg, s["u"], w["ffn_dw_w"], n("ffn_dw_bwd"))
    d_ffn_dw_b = d_ffn_dw_b[0]
    dh2 = _mm(du, w["ffn_up"], n("ffn_up_dx"), tb=True, b3=True)
    d_ffn_up = _mm(s["h2_t"], du, n("ffn_up_dw"), out_dtype=BF16, tm_max=DW_TM,
                      o_cs=2 * D_FF // N_DEV)
    tok = send(("ffn_up", l), d_ffn_up)
    (dx1, dn2, dsh2, dsc2) = _normmod_bwd(s["x1"], _after(w["norm2_g"], tok), mod4, 1, dh2, dx2, n("norm2_bwd"))
    dgg, dg1 = _gate_res_bwd(dx1, s["g1"], mod4, 2, n("res1_bwd"))
    dmix = _mm(dgg, w["w_out"], n("proj_out_dx"), tb=True)
    d_w_out = _mm(_transpose_parts(s["mix"], n("mix_t")), dgg, n("proj_out_dw"), out_dtype=BF16, tm_max=DW_TM)
    tok = send(("w_out", l), d_w_out.reshape(N_DEV, _d_mix() // N_DEV, d))
    do, dgate, dgn = _ggn_bwd(s["o_f"], s["o_b"], s["p"], _after(w["ret_gn_g"], tok), dmix, n("ret_gn_bwd"))
    d_fwd, d_bwd, ddec = _ret_bwd(s["p"], cst["cos"], cst["sin"], w["ret_decay"], cst["order"], s["states"], do, n("ret_bwd"))
    du2, dlng, dlnb, dpw = _conv_bwd1(s["u2"], dmix, w["conv_ln_g"], w["conv_ln_b"], w["conv_pw"], n("conv_bwd1"))
    dca, dcb, ddww, ddwb = _conv_bwd2(du2, s["p"], w["conv_dw_w"], n("conv_bwd2"))
    dnq, dnk, dnv, drb = _na_bwd(s["p"], w["rb"], dmix, n("na_bwd"))
    dp = _assemble_dp(d_fwd, d_bwd, dgate, dca, dcb, dnq, dnk, dnv, n("dproj"))
    h1_t = s["h1_t"]
    half = d // 2
    for i in range(2):
        d_w_in = _mm(h1_t[i * half:(i + 1) * half], dp, n(f"proj_in_dw{i}"), out_dtype=BF16, tm_max=DW_TM,
                        o_cs=_d_in() // N_DEV)
        tok = send(("w_in", l, i), d_w_in)
    dh1 = _mm(dp, w["w_in"], n("proj_in_dx"), tb=True, after=tok)
    (dx, dn1, dsh1, dsc1) = _normmod_bwd(s["x"], _after(w["norm1_g"], tok), mod4, 0, dh1, dx1, n("norm1_bwd"))
    dmod = jnp.concatenate([dsh1, dsc1, dg1, dsh2, dsc2, dg2], axis=1)
    small = dict(norm1_g=dn1[0], ret_decay=ddec[:, :, 0, 0], ret_gn_g=dgn[0], conv_dw_w=ddww, conv_dw_b=ddwb[0],
                 conv_ln_g=dlng[0], conv_ln_b=dlnb[0], conv_pw=dpw, na_rpb=_rpb_rows_t(drb), norm2_g=dn2[0],
                 ffn_dw_w=d_ffn_dw_w, ffn_dw_b=d_ffn_dw_b)
    return dx, dmod, small


def _d_mix():
    return _ret_w() + CONV_W + _na_w()


_SMALL = ["c_ctx", "b_ada", "norm1_g", "ret_decay", "ret_gn_g", "conv_dw_w", "conv_dw_b", "conv_ln_g", "conv_ln_b",
          "conv_pw", "na_rpb", "norm2_g", "ffn_dw_w", "ffn_dw_b", "final_g"]
_SMALL_SHARD_AXIS = {"conv_dw_w": 2, "conv_pw": 1, "ffn_dw_w": 2}


def kernel(x, c, ctx, c_ctx, w_ada, b_ada, norm1_g, w_in, ret_decay, ret_gn_g, conv_dw_w, conv_dw_b, conv_ln_g, conv_ln_b, conv_pw, na_rpb, w_out, norm2_g, ffn_up, ffn_dw_w, ffn_dw_b, ffn_down, final_g, loss_target, m_c_ctx, m_w_ada, m_b_ada, m_norm1_g, m_w_in, m_ret_decay, m_ret_gn_g, m_conv_dw_w, m_conv_dw_b, m_conv_ln_g, m_conv_ln_b, m_conv_pw, m_na_rpb, m_w_out, m_norm2_g, m_ffn_up, m_ffn_dw_w, m_ffn_dw_b, m_ffn_down, m_final_g, v_c_ctx, v_w_ada, v_b_ada, v_norm1_g, v_w_in, v_ret_decay, v_ret_gn_g, v_conv_dw_w, v_conv_dw_b, v_conv_ln_g, v_conv_ln_b, v_conv_pw, v_na_rpb, v_w_out, v_norm2_g, v_ffn_up, v_ffn_dw_w, v_ffn_dw_b, v_ffn_down, v_final_g):
    d, nl = D_MODEL, DEPTH
    cs = 6 * d // N_DEV
    me = _my_index()
    weights = dict(c_ctx=c_ctx, w_ada=w_ada, b_ada=b_ada, norm1_g=norm1_g, w_in=w_in, ret_decay=ret_decay, ret_gn_g=ret_gn_g,
                   conv_dw_w=conv_dw_w, conv_dw_b=conv_dw_b, conv_ln_g=conv_ln_g, conv_ln_b=conv_ln_b, conv_pw=conv_pw,
                   na_rpb=na_rpb, w_out=w_out, norm2_g=norm2_g, ffn_up=ffn_up, ffn_dw_w=ffn_dw_w, ffn_dw_b=ffn_dw_b,
                   ffn_down=ffn_down, final_g=final_g)
    mom = dict(c_ctx=m_c_ctx, w_ada=m_w_ada, b_ada=m_b_ada, norm1_g=m_norm1_g, w_in=m_w_in, ret_decay=m_ret_decay,
               ret_gn_g=m_ret_gn_g, conv_dw_w=m_conv_dw_w, conv_dw_b=m_conv_dw_b, conv_ln_g=m_conv_ln_g,
               conv_ln_b=m_conv_ln_b, conv_pw=m_conv_pw, na_rpb=m_na_rpb, w_out=m_w_out, norm2_g=m_norm2_g,
               ffn_up=m_ffn_up, ffn_dw_w=m_ffn_dw_w, ffn_dw_b=m_ffn_dw_b, ffn_down=m_ffn_down, final_g=m_final_g)
    var = dict(c_ctx=v_c_ctx, w_ada=v_w_ada, b_ada=v_b_ada, norm1_g=v_norm1_g, w_in=v_w_in, ret_decay=v_ret_decay,
               ret_gn_g=v_ret_gn_g, conv_dw_w=v_conv_dw_w, conv_dw_b=v_conv_dw_b, conv_ln_g=v_conv_ln_g,
               conv_ln_b=v_conv_ln_b, conv_pw=v_conv_pw, na_rpb=v_na_rpb, w_out=v_w_out, norm2_g=v_norm2_g,
               ffn_up=v_ffn_up, ffn_dw_w=v_ffn_dw_w, ffn_dw_b=v_ffn_dw_b, ffn_down=v_ffn_down, final_g=v_final_g)

    big_names = ["w_in", "w_out", "ffn_up", "ffn_down"]
    shards = {(nm, l): _cast_bf16(weights[nm][l], f"cast_{nm}{l}") for l in range(nl) for nm in big_names}
    small_sharded = _pack([conv_dw_w, conv_pw, ffn_dw_w], SUBLANES)
    c_rows = jnp.pad(c, ((0, SUBLANES - 1), (0, 0)))
    gathered = _run_comm(_Gather([c_rows, small_sharded, shards[("w_in", 0)]]), "gather_first")
    c_all = gathered[0][:, 0, :]
    def whole(rows, shard_shape, axis):
        n_el = int(np.prod(shard_shape))
        parts = rows.reshape(N_DEV, -1)[:, :n_el].reshape((N_DEV,) + tuple(shard_shape))
        parts = jnp.moveaxis(parts, 0, axis)
        return parts.reshape(shard_shape[:axis] + (N_DEV * shard_shape[axis],) + shard_shape[axis + 1:])

    r0 = _pack_rows(conv_dw_w.shape)
    r1 = r0 + _pack_rows(conv_pw.shape)
    r2 = r1 + _pack_rows(ffn_dw_w.shape)
    full_conv_dw_w = whole(gathered[1][:, :r0], conv_dw_w.shape, 2)
    full_conv_pw = whole(gathered[1][:, r0:r1], conv_pw.shape, 1)
    full_ffn_dw_w = whole(gathered[1][:, r1:r2], ffn_dw_w.shape, 2)

    c16 = jnp.concatenate([c_all, jnp.broadcast_to(c_ctx[None, :], (N_DEV, d))], axis=0)
    b_shard = lax.dynamic_slice_in_dim(b_ada, me * cs, cs, axis=1)[:, None, :]
    m_shard = _ada_fwd(c16, w_ada, b_shard, "ada_fwd")
    m_all = _run_comm(_Gather([m_shard.reshape(nl * 16, cs)]), "gather_mod")[0]
    m_full = m_all.reshape(N_DEV, nl, 16, cs).transpose(1, 2, 0, 3).reshape(nl, 16, 6 * d)
    m_lat = lax.dynamic_index_in_dim(m_full, me, axis=1, keepdims=False)
    mod = jnp.stack([m_full[:, N_DEV], m_lat], axis=1).reshape(nl, 2, 6, 1, d)

    arriving, token = {}, m_all
    for l in range(nl):
        for nm in big_names:
            if (nm, l) != ("w_in", 0):
                arriving[(nm, l)], token = _split_start(shards[(nm, l)], True, f"gather_{nm}{l}", token)
    mod = _after(mod, token)

    cos, sin = _rope_tables()
    cst = dict(cos=cos, sin=sin, order=_chunk_order())
    layer_w = []
    for l in range(nl):
        layer_w.append(dict(
            norm1_g=norm1_g[l][None], norm2_g=norm2_g[l][None], ret_decay=ret_decay[l], ret_gn_g=ret_gn_g[l][None],
            conv_dw_w=full_conv_dw_w[l], conv_dw_b=conv_dw_b[l][None], conv_ln_g=conv_ln_g[l][None],
            conv_ln_b=conv_ln_b[l][None], conv_pw=full_conv_pw[l], rb=_rpb_rows(na_rpb[l]),
            ffn_dw_w=full_ffn_dw_w[l], ffn_dw_b=ffn_dw_b[l][None]))

    xs = jnp.concatenate([ctx[0], x[0]], axis=0)
    saved = []
    for l in range(nl):
        def arrived(nm, after, l=l):
            if (nm, l) == ("w_in", 0):
                return gathered[2]
            return _split_wait(arriving[(nm, l)], after, f"arrived_{nm}{l}")

        xs, sv = _layer_fwd(l, xs, mod[l], layer_w[l], cst, arrived)
        saved.append(sv)
    loss_tile, dxs, dfinal = _loss_head(xs, final_g[None], loss_target[0], "loss_head")
    loss = lax.psum(loss_tile[0, 0], ("x", "y", "c"))

    dmods, smalls = [None] * nl, [None] * nl
    leaving, last = {}, [loss_tile]

    def send(key, partial):
        leaving[key], token = _split_start(partial, False, "send_" + "_".join(str(k) for k in key), last[0])
        last[0] = token
        return token

    per_layer = [nm for nm in _SMALL if nm not in ("c_ctx", "b_ada", "final_g")]
    small_packs, small_arriving = [None] * nl, [None] * nl
    for l in reversed(range(nl)):
        dxs, dmods[l], smalls[l] = _layer_bwd(l, dxs, saved[l], mod[l], layer_w[l], cst, send)
        small_packs[l] = _pack([smalls[l][nm] for nm in per_layer], 512)
        if l > 0:
            small_arriving[l], last[0] = _split_start(small_packs[l], True, f"gather_small_grads{l}", last[0])
    grad_x = dxs[CTX_LEN:][None]

    arrived_grad = lambda key, after: _split_wait(leaving[key], after, "got_" + "_".join(str(k) for k in key))
    out_big = {}
    after = dxs
    for nm in ["ffn_down", "ffn_up", "w_out"]:
        out_big[nm] = _adamw(weights[nm], mom[nm], var[nm], [arrived_grad((nm, l), after) for l in range(nl)],
                                f"adamw_{nm}")
        after = out_big[nm][0]

    dm_mine = jnp.stack(dmods).reshape(nl * 2, 6 * d)
    dm_rows = jnp.pad(dm_mine, ((0, SUBLANES - nl * 2), (0, 0)))
    dm_all = _run_comm(_Gather([dm_rows]), "gather_dmod", after=after)[0][:, :nl * 2].reshape(N_DEV, nl, 2, 6 * d)
    dm16_full = jnp.concatenate([dm_all[:, :, 1].transpose(1, 0, 2), dm_all[:, :, 0].transpose(1, 0, 2)], axis=1)
    dm16 = lax.dynamic_slice_in_dim(dm16_full, me * cs, cs, axis=2)
    g_w_ada, dc16 = _ada_bwd(c16, dm16, w_ada, "ada_bwd")

    shared = dict(c_ctx=jnp.sum(dc16[:, N_DEV:], axis=(0, 1)),
                  b_ada=jnp.sum(jnp.stack(dmods).reshape(nl, 2, 6 * d), axis=1), final_g=dfinal[0])
    shared_all = _run_comm(_Gather([_pack(list(shared.values()), SUBLANES)]), "gather_shared_grads")[0]
    small_arriving[0], token = _split_start(small_packs[0], True, "gather_small_grads0", shared_all)

    out_big["w_ada"] = _adamw(w_ada, m_w_ada, v_w_ada, g_w_ada[:, None], "adamw_w_ada")
    halves = lambda a: a.reshape(2 * nl, d // 2, a.shape[2])
    res = _adamw(halves(w_in), halves(m_w_in), halves(v_w_in),
                    [arrived_grad(("w_in", l, i), token) for l in range(nl) for i in range(2)], "adamw_w_in")
    out_big["w_in"] = [r.reshape(w_in.shape) for r in res]

    g_small = dict(zip(shared, _unpack(_sum_devices(shared_all, "sum_shared_grads"), [v.shape for v in shared.values()])))
    per = []
    for l in range(nl):
        got = _split_wait(small_arriving[l], res[0], f"arrived_small_grads{l}")
        per.append(_unpack(_sum_devices(got, f"sum_small_grads{l}"), [smalls[l][nm].shape for nm in per_layer]))
    g_small.update({nm: jnp.stack([per[l][i] for l in range(nl)]) for i, nm in enumerate(per_layer)})
    for nm, ax in _SMALL_SHARD_AXIS.items():
        n_sh = weights[nm].shape[ax]
        g_small[nm] = lax.dynamic_slice_in_dim(g_small[nm], me * n_sh, n_sh, axis=ax)
    shapes_own = [weights[nm].shape for nm in _SMALL]
    pk = lambda src: _pack([src[nm] for nm in _SMALL], 2 * SUBLANES)[None]
    res_small = _adamw(pk(weights), pk(mom), pk(var), pk(g_small)[:, None], "adamw_small")
    out_small = [dict(zip(_SMALL, _unpack(r[0], shapes_own))) for r in res_small]

    names = ["c_ctx", "w_ada", "b_ada", "norm1_g", "w_in", "ret_decay", "ret_gn_g", "conv_dw_w", "conv_dw_b", "conv_ln_g",
             "conv_ln_b", "conv_pw", "na_rpb", "w_out", "norm2_g", "ffn_up", "ffn_dw_w", "ffn_dw_b", "ffn_down", "final_g"]
    outs = [loss, grad_x]
    for kind in range(4):
        for nm in names:
            outs.append(out_big[nm][kind] if nm in out_big else out_small[kind][nm])
    return tuple(outs)
```

```python
import numpy as np
import jax
import jax.numpy as jnp
from jax import lax
from jax.experimental import pallas as pl
from jax.experimental.pallas import tpu as pltpu

D_MODEL = 2048
SEQ = 4096
DEPTH = 2
GRID_W = 64
CTX_LEN = 256
RET_HEADS = 4
RET_DK = 128
RET_DV = 256
RET_CHUNK = 128
CONV_W = 512
CONV_K = 31
NA_HEADS = 4
NA_DH = 128
NA_ROWS = 8
NA_COLS = 16
D_FF = 5632
FFN_K = 3
ROPE_BASE = 10000.0
EPS = 1e-6
ADAM_LR = 0.001
ADAM_B1 = 0.9
ADAM_B2 = 0.999
ADAM_EPS = 1e-08
ADAM_WD = 0.01
ADAM_STEP = 10
N_DEV = 8

LANES = 128
SUBLANES = 8
VMEM_LIMIT = 56 * 1024 * 1024
ROW_CHUNK = 32

F32 = jnp.float32
BF16 = jnp.bfloat16
MESH = pl.DeviceIdType.MESH
NEG = -1e30


def _ret_qk_w():
    return RET_HEADS * RET_DK


def _ret_w():
    return RET_HEADS * RET_DV


def _na_w():
    return NA_HEADS * NA_DH


def _d_in():
    return 2 * _ret_qk_w() + 2 * _ret_w() + 2 * CONV_W + 3 * _na_w()


def _offsets():
    sizes = [_ret_qk_w(), _ret_qk_w(), _ret_w(), _ret_w(), CONV_W, CONV_W, _na_w(), _na_w(), _na_w()]
    offs = [0]
    for s in sizes[:-1]:
        offs.append(offs[-1] + s)
    return dict(zip(["q", "k", "v", "g", "a", "b", "nq", "nk", "nv"], offs))


def _t_rows():
    return CTX_LEN + SEQ


def _tm():
    return CTX_LEN


def _params(sem=None):
    kw = dict(vmem_limit_bytes=VMEM_LIMIT)
    if sem is not None:
        kw["dimension_semantics"] = sem
    return pltpu.CompilerParams(**kw)


def _tile(n, pref, align):
    best = None
    for t in range(align, min(n, pref) + 1, align):
        if n % t == 0:
            best = t
    return best if best is not None else n


def _dg(a, b, ca, cb):
    return lax.dot_general(a.astype(BF16), b.astype(BF16), (((ca,), (cb,)), ((), ())), preferred_element_type=F32)


@jax.custom_vjp
def dot_nn(a, b):
    return _dg(a, b, 1, 0)


dot_nn.defvjp(lambda a, b: (_dg(a, b, 1, 0), (a, b)),
              lambda r, g: (_dg(g, r[1], 1, 1), _dg(r[0], g, 0, 0)))


@jax.custom_vjp
def dot_nt(a, b):
    return _dg(a, b, 1, 1)


dot_nt.defvjp(lambda a, b: (_dg(a, b, 1, 1), (a, b)),
              lambda r, g: (_dg(g, r[1], 1, 0), _dg(g, r[0], 0, 0)))


@jax.custom_vjp
def dot_tn(a, b):
    return _dg(a, b, 0, 0)


dot_tn.defvjp(lambda a, b: (_dg(a, b, 0, 0), (a, b)),
              lambda r, g: (_dg(r[1], g, 1, 1), _dg(r[0], g, 1, 0)))


def _sigmoid(x):
    return 0.5 * jnp.tanh(0.5 * x) + 0.5


def _silu(x):
    return x * _sigmoid(x)


def _my_pos():
    return lax.axis_index("x"), lax.axis_index("y"), lax.axis_index("c")


def _my_index():
    x, y, c = _my_pos()
    return 4 * x + 2 * y + c


_ANY = pl.BlockSpec(memory_space=pl.ANY)


class _Gather:
    def __init__(self, arrays):
        self.arrays = list(arrays)
        n = len(self.arrays)
        self.out_shape = [jax.ShapeDtypeStruct((N_DEV,) + a.shape, a.dtype) for a in self.arrays]
        self.scratch = [pltpu.SemaphoreType.DMA((n, 7)), pltpu.SemaphoreType.DMA((n, 7)), pltpu.SemaphoreType.DMA((n,))]

    def _plan(self, xs, outs, sems):
        send_sems, recv_sems, local_sems = sems
        n = len(self.arrays)
        x, y, c = _my_pos()
        me, sibling = (x, y, c), (x, y, 1 - c)
        chips = [(1 - x, y), (x, 1 - y), (1 - x, 1 - y)]

        def slot(a, p):
            return outs[a].at[4 * p[0] + 2 * p[1] + p[2]]

        def copy(a, k, block, to, src=None):
            return pltpu.make_async_remote_copy(
                src_ref=slot(a, block) if src is None else src, dst_ref=slot(a, block),
                send_sem=send_sems.at[a, k], recv_sem=recv_sems.at[a, k], device_id=to, device_id_type=MESH)

        mine = [pltpu.make_async_copy(xs[a], slot(a, me), local_sems.at[a]) for a in range(n)]
        first = []
        for a in range(n):
            first.append(copy(a, 0, me, sibling, src=xs[a]))
            first += [copy(a, 1 + j, me, (*chip, c), src=xs[a]) for j, chip in enumerate(chips)]
        return n, c, me, sibling, chips, copy, mine, first

    def start(self, xs, outs, sems):
        _, _, _, _, _, _, mine, first = self._plan(xs, outs, sems)
        for m in mine:
            m.start()
        for cp in first:
            cp.start()

    def finish(self, xs, outs, sems):
        n, c, me, sibling, chips, copy, mine, first = self._plan(xs, outs, sems)
        passed = []
        for a in range(n):
            for j, chip in enumerate(chips):
                copy(a, 1 + j, (*chip, c), me).wait_recv()
                p = copy(a, 4 + j, (*chip, c), sibling)
                p.start()
                passed.append(p)
        for a in range(n):
            copy(a, 0, sibling, me).wait_recv()
            for j, chip in enumerate(chips):
                copy(a, 4 + j, (*chip, 1 - c), me).wait_recv()
        for cp in first + passed:
            cp.wait_send()
        for m in mine:
            m.wait()


def _run_comm(comm, name, after=None):
    n = len(comm.arrays)
    extra = [] if after is None else [after]

    def body(*refs):
        xs, outs, sems = refs[:n], refs[n + len(extra):2 * n + len(extra)], refs[2 * n + len(extra):]
        comm.start(xs, outs, sems)
        comm.finish(xs, outs, sems)

    return pl.pallas_call(body, name=name, out_shape=comm.out_shape, in_specs=[_ANY] * (n + len(extra)),
                          out_specs=[_ANY] * n, scratch_shapes=comm.scratch)(*comm.arrays, *extra)


_HBM = pl.BlockSpec(memory_space=pltpu.HBM)
_SEMS = pl.BlockSpec(memory_space=pltpu.SEMAPHORE)
_EFFECT = pltpu.SideEffectType.DATAFLOW_SIDE_EFFECTING


def _own_slot(x, gathering, name):
    shape = (N_DEV,) + x.shape if gathering else x.shape
    r, c = shape[1], shape[2]
    tr = _tile(r, 256, 2 * SUBLANES)
    me = jnp.reshape(_my_index(), (1,)).astype(jnp.int32)

    def body(me_ref, x_ref, o_ref):
        o_ref[...] = x_ref[...]

    src = (pl.BlockSpec((tr, c), lambda i, m: (i, 0)) if gathering
           else pl.BlockSpec((None, tr, c), lambda i, m: (m[0], i, 0)))
    grid_spec = pltpu.PrefetchScalarGridSpec(
        num_scalar_prefetch=1, grid=(r // tr,), in_specs=[src],
        out_specs=pl.BlockSpec((None, tr, c), lambda i, m: (m[0], i, 0)))
    return pl.pallas_call(body, name=name, grid_spec=grid_spec, out_shape=jax.ShapeDtypeStruct(shape, x.dtype),
                          compiler_params=_params(("arbitrary",)))(me, x)


def _split_plan(x_ref, land_ref, send_sems, recv_sems, gathering):
    x, y, c = _my_pos()
    me = 4 * x + 2 * y + c
    sends, recvs = [], []
    for k in range(1, N_DEV):
        px = 1 - x if (k >> 2) & 1 else x
        py = 1 - y if (k >> 1) & 1 else y
        pc = 1 - c if k & 1 else c
        peer = 4 * px + 2 * py + pc
        mine, theirs = (x_ref, x_ref) if gathering else (x_ref.at[peer], x_ref.at[me])
        sends.append(pltpu.make_async_remote_copy(
            src_ref=mine, dst_ref=land_ref.at[me], send_sem=send_sems.at[k - 1], recv_sem=recv_sems.at[k - 1],
            device_id=(px, py, pc), device_id_type=MESH))
        recvs.append(pltpu.make_async_remote_copy(
            src_ref=theirs, dst_ref=land_ref.at[peer], send_sem=send_sems.at[k - 1], recv_sem=recv_sems.at[k - 1],
            device_id=(px, py, pc), device_id_type=MESH))
    return sends, recvs


def _split_start(x, gathering, name, prev):
    land = _own_slot(x, gathering, name + "_own")

    def body(x_ref, land_ref, prev_ref, send_sems, recv_sems, x_thru, land_thru, token):
        sends, _ = _split_plan(x_ref, land_ref, send_sems, recv_sems, gathering)
        for s in sends:
            s.start()
        token[...] = jnp.zeros_like(token)

    sems = pltpu.SemaphoreType.DMA((N_DEV - 1,))
    send_sems, recv_sems, x_thru, land_thru, token = pl.pallas_call(
        body, name=name,
        out_shape=(sems, sems, pltpu.HBM(x.shape, x.dtype), pltpu.HBM(land.shape, land.dtype),
                   jax.ShapeDtypeStruct((SUBLANES, LANES), F32)),
        in_specs=(_HBM, _HBM, _ANY), out_specs=(_SEMS, _SEMS, _HBM, _HBM, pl.BlockSpec(memory_space=pltpu.VMEM)),
        input_output_aliases={0: 2, 1: 3},
        compiler_params=pltpu.CompilerParams(has_side_effects=_EFFECT),
    )(pltpu.with_memory_space_constraint(x, pltpu.HBM), pltpu.with_memory_space_constraint(land, pltpu.HBM), prev)
    return (send_sems, recv_sems, x_thru, land_thru, gathering), token


def _after(a, token):
    return a + token[0, 0].astype(a.dtype)


def _split_wait(handle, after, name):
    send_sems, recv_sems, x_thru, land_thru, gathering = handle

    def body(x_ref, land_ref, send_sems, recv_sems, after_ref, x_dead, got_ref):
        sends, recvs = _split_plan(x_ref, land_ref, send_sems, recv_sems, gathering)
        for s in sends:
            s.wait_send()
        for r in recvs:
            r.wait_recv()

    return pl.pallas_call(
        body, name=name, out_shape=(pltpu.HBM(x_thru.shape, x_thru.dtype), pltpu.HBM(land_thru.shape, land_thru.dtype)),
        in_specs=(_HBM, _HBM, _SEMS, _SEMS, _ANY), out_specs=(_HBM, _HBM), input_output_aliases={0: 0, 1: 1},
        compiler_params=pltpu.CompilerParams(has_side_effects=_EFFECT),
    )(x_thru, land_thru, send_sems, recv_sems, after)[1]


def _call(body, *, name, grid, in_specs, out_specs, out_shape, args, scratch=(), sem=None, after=None):
    if after is None:
        return list(pl.pallas_call(body, name=name, grid=grid, in_specs=list(in_specs), out_specs=list(out_specs),
                                   out_shape=list(out_shape), scratch_shapes=list(scratch),
                                   compiler_params=_params(sem))(*args))
    n_in = len(in_specs)

    def wrapped(*refs):
        body(*refs[:n_in], *refs[n_in + 1:])

    return list(pl.pallas_call(wrapped, name=name, grid=grid, in_specs=list(in_specs) + [_ANY], out_specs=list(out_specs),
                               out_shape=list(out_shape), scratch_shapes=list(scratch),
                               compiler_params=_params(sem))(*args, after))


MM_B_BLOCK_BYTES = 6 * 1024 * 1024
MM_O_BLOCK_BYTES = 13 * 1024 * 1024 // 2


def _mm(a, b, name, tb=False, out_dtype=F32, b3=False, o_cs=None, tm_max=1088, after=None):
    m, k = a.shape
    if b3:
        cs = b.shape[2]
        n, kb = (b.shape[1], N_DEV * cs) if tb else (N_DEV * cs, b.shape[1])
    else:
        n, kb = (b.shape[0], b.shape[1]) if tb else (b.shape[1], b.shape[0])
    assert k == kb, (a.shape, b.shape, tb)
    if b3 and tb:
        tm, tn = _tile(m, 544, 2 * SUBLANES), _tile(n, 512, LANES)

        def body_shards(a_ref, b_ref, o_ref):
            r = None
            for j in range(N_DEV):
                part = lax.dot_general(a_ref[:, j * cs:(j + 1) * cs], b_ref[j], (((1,), (1,)), ((), ())),
                                       preferred_element_type=F32)
                r = part if r is None else r + part
            o_ref[...] = r.astype(o_ref.dtype)

        return _call(
            body_shards, name=name, grid=(m // tm, n // tn),
            in_specs=[pl.BlockSpec((tm, k), lambda i, j: (i, 0)), pl.BlockSpec((N_DEV, tn, cs), lambda i, j: (0, j, 0))],
            out_specs=[pl.BlockSpec((tm, tn), lambda i, j: (i, j))], out_shape=[jax.ShapeDtypeStruct((m, n), out_dtype)],
            sem=("parallel", "parallel"), args=(a, b), after=after)[0]
    tm = _tile(m, tm_max, 2 * SUBLANES)
    tk = k
    if b3:
        tn = cs
    elif o_cs is not None:
        tn = o_cs if o_cs % LANES == 0 else 2 * o_cs
    else:
        tn = _tile(n, min(MM_B_BLOCK_BYTES // (2 * tk), MM_O_BLOCK_BYTES // (4 * tm)), LANES)
    cb = 1 if tb else 0
    dn = (((1,), (cb,)), ((), ()))

    def body_one(a_ref, b_ref, o_ref):
        r = lax.dot_general(a_ref[...], b_ref[...], dn, preferred_element_type=F32)
        if o_cs is None:
            o_ref[...] = r.astype(o_ref.dtype)
        else:
            for j in range(tn // o_cs):
                o_ref[j] = r[:, j * o_cs:(j + 1) * o_cs].astype(o_ref.dtype)

    a_spec = pl.BlockSpec((tm, tk), lambda i, j: (i, 0))
    if b3:
        b_spec = pl.BlockSpec((None, tk, cs), lambda i, j: (j, 0, 0))
    else:
        b_spec = pl.BlockSpec((tn, tk), lambda i, j: (j, 0)) if tb else pl.BlockSpec((tk, tn), lambda i, j: (0, j))
    if o_cs is None:
        o_spec = pl.BlockSpec((tm, tn), lambda i, j: (i, j))
        o_shape = jax.ShapeDtypeStruct((m, n), out_dtype)
    else:
        o_spec = pl.BlockSpec((tn // o_cs, tm, o_cs), lambda i, j: (j, i, 0))
        o_shape = jax.ShapeDtypeStruct((n // o_cs, m, o_cs), out_dtype)
    return _call(
        body_one, name=name, grid=(m // tm, n // tn), in_specs=[a_spec, b_spec], out_specs=[o_spec], out_shape=[o_shape],
        sem=("parallel", "parallel"), args=(a, b), after=after)[0]


def _mm_parts(parts, b, name):
    m, n = parts[0].shape[0], b.shape[1]
    ks = [p.shape[1] for p in parts]
    k = sum(ks)
    assert b.shape[0] == k
    tm = _tile(m, 1088, 2 * SUBLANES)
    tn = _tile(n, min(MM_B_BLOCK_BYTES // (2 * k), MM_O_BLOCK_BYTES // (4 * tm)), LANES)

    def body(*refs):
        a_refs, b_ref, o_ref = refs[:-2], refs[-2], refs[-1]
        r, o = None, 0
        for a_ref, ki in zip(a_refs, ks):
            part = jnp.dot(a_ref[...], b_ref[o:o + ki, :], preferred_element_type=F32)
            r = part if r is None else r + part
            o += ki
        o_ref[...] = r

    return pl.pallas_call(
        body, name=name, grid=(m // tm, n // tn),
        in_specs=[pl.BlockSpec((tm, ki), lambda i, j: (i, 0)) for ki in ks] + [pl.BlockSpec((k, tn), lambda i, j: (0, j))],
        out_specs=pl.BlockSpec((tm, tn), lambda i, j: (i, j)), out_shape=jax.ShapeDtypeStruct((m, n), F32),
        compiler_params=_params(("parallel", "parallel")))(*parts, b)


def _transpose_parts(parts, name):
    t = parts[0].shape[0]
    cs = [p.shape[1] for p in parts]
    tt = _tm()

    def body(*refs):
        o_ref, o = refs[-1], 0
        for a_ref, ci in zip(refs[:-1], cs):
            o_ref[o:o + ci, :] = a_ref[...].T
            o += ci

    return pl.pallas_call(body, name=name, grid=(t // tt,),
                          in_specs=[pl.BlockSpec((tt, ci), lambda i: (i, 0)) for ci in cs],
                          out_specs=pl.BlockSpec((sum(cs), tt), lambda i: (0, i)),
                          out_shape=jax.ShapeDtypeStruct((sum(cs), t), BF16),
                          compiler_params=_params(("parallel",)))(*parts)


DW_TM = 512


def _cast_bf16(x, name):
    r, c = x.shape
    tr = _tile(r, 512, 2 * SUBLANES)

    def body(x_ref, o_ref):
        o_ref[...] = x_ref[...].astype(BF16)

    return pl.pallas_call(body, name=name, grid=(r // tr,), in_specs=[pl.BlockSpec((tr, c), lambda i: (i, 0))],
                          out_specs=pl.BlockSpec((tr, c), lambda i: (i, 0)),
                          out_shape=jax.ShapeDtypeStruct((r, c), BF16), compiler_params=_params(("parallel",)))(x)


def _cols_from_shards(wg, name):
    _, k, cs = wg.shape
    tk = _tile(k, 256, 2 * SUBLANES)

    def body(w_ref, o_ref):
        for j in range(N_DEV):
            o_ref[:, j * cs:(j + 1) * cs] = w_ref[j]

    return pl.pallas_call(body, name=name, grid=(k // tk,),
                          in_specs=[pl.BlockSpec((N_DEV, tk, cs), lambda i: (0, i, 0))],
                          out_specs=pl.BlockSpec((tk, N_DEV * cs), lambda i: (i, 0)),
                          out_shape=jax.ShapeDtypeStruct((k, N_DEV * cs), wg.dtype),
                          compiler_params=_params(("parallel",)))(wg)


def _stream(i):
    return jnp.minimum(i, 1)


def _normmod(x, g, sh, sc):
    y = x * lax.rsqrt(jnp.mean(x * x, axis=-1, keepdims=True) + EPS)
    return (y * g) * (1.0 + sc) + sh


def _mod_spec(chunk, d):
    return pl.BlockSpec((None, None, 1, d), lambda i: (_stream(i), chunk, 0, 0))


def _normmod_fwd(x, g, mod4, which, name):
    t, d = x.shape
    tm = _tm()
    ish, isc = (0, 1) if which == 0 else (3, 4)

    def body(x_ref, g_ref, sh_ref, sc_ref, o_ref, ot_ref):
        h = _normmod(x_ref[...], g_ref[...], sh_ref[...], sc_ref[...]).astype(BF16)
        o_ref[...] = h
        ot_ref[...] = h.T

    row = pl.BlockSpec((tm, d), lambda i: (i, 0))
    return pl.pallas_call(body, name=name, grid=(t // tm,),
                          in_specs=[row, pl.BlockSpec((1, d), lambda i: (0, 0)), _mod_spec(ish, d), _mod_spec(isc, d)],
                          out_specs=[row, pl.BlockSpec((d, tm), lambda i: (0, i))],
                          out_shape=[jax.ShapeDtypeStruct((t, d), BF16), jax.ShapeDtypeStruct((d, t), BF16)],
                          compiler_params=_params(("parallel",)))(x, g, mod4, mod4)


def _normmod_bwd(x, g, mod4, which, dh, dres, name):
    t, d = x.shape
    tm = _tm()
    ish, isc = (0, 1) if which == 0 else (3, 4)

    def body(x_ref, g_ref, sh_ref, sc_ref, dh_ref, dres_ref, dx_ref, dg_ref, dsh_ref, dsc_ref):
        i = pl.program_id(0)
        _, vjp = jax.vjp(_normmod, x_ref[...], g_ref[...], sh_ref[...], sc_ref[...])
        dx, dg, dsh, dsc = vjp(dh_ref[...])
        dx_ref[...] = dres_ref[...] + dx

        @pl.when(i == 0)
        def _():
            dg_ref[...] = jnp.zeros_like(dg_ref)

        @pl.when(i <= 1)
        def _():
            dsh_ref[...] = jnp.zeros_like(dsh_ref)
            dsc_ref[...] = jnp.zeros_like(dsc_ref)

        dg_ref[...] += dg
        dsh_ref[...] += dsh
        dsc_ref[...] += dsc

    row = pl.BlockSpec((tm, d), lambda i: (i, 0))
    vec = pl.BlockSpec((1, d), lambda i: (0, 0))
    svec = pl.BlockSpec((None, 1, d), lambda i: (_stream(i), 0, 0))
    return _call(
        body, name=name, grid=(t // tm,),
        in_specs=[row, vec, _mod_spec(ish, d), _mod_spec(isc, d), row, row],
        out_specs=[row, vec, svec, svec],
        out_shape=[jax.ShapeDtypeStruct((t, d), F32), jax.ShapeDtypeStruct((1, d), F32),
                   jax.ShapeDtypeStruct((2, 1, d), F32), jax.ShapeDtypeStruct((2, 1, d), F32)],
        sem=("arbitrary",), args=(x, g, mod4, mod4, dh, dres))


def _gate_res_fwd(x, f, mod4, chunk, name):
    t, d = x.shape
    tm = _tm()

    def body(x_ref, f_ref, g_ref, o_ref):
        o_ref[...] = x_ref[...] + g_ref[...] * f_ref[...]

    row = pl.BlockSpec((tm, d), lambda i: (i, 0))
    return pl.pallas_call(body, name=name, grid=(t // tm,), in_specs=[row, row, _mod_spec(chunk, d)], out_specs=row,
                          out_shape=jax.ShapeDtypeStruct((t, d), F32), compiler_params=_params(("parallel",)))(x, f, mod4)


def _gate_res_bwd(dx, f, mod4, chunk, name):
    t, d = dx.shape
    tm = _tm()

    def body(dx_ref, f_ref, g_ref, o_ref, dg_ref):
        i = pl.program_id(0)
        dxv = dx_ref[...]
        o_ref[...] = (dxv * g_ref[...]).astype(BF16)

        @pl.when(i <= 1)
        def _():
            dg_ref[...] = jnp.zeros_like(dg_ref)

        dg_ref[...] += jnp.sum(dxv * f_ref[...], axis=0, keepdims=True)

    row = pl.BlockSpec((tm, d), lambda i: (i, 0))
    return pl.pallas_call(
        body, name=name, grid=(t // tm,), in_specs=[row, row, _mod_spec(chunk, d)],
        out_specs=[row, pl.BlockSpec((None, 1, d), lambda i: (_stream(i), 0, 0))],
        out_shape=[jax.ShapeDtypeStruct((t, d), BF16), jax.ShapeDtypeStruct((2, 1, d), F32)],
        compiler_params=_params(("arbitrary",)))(dx, f, mod4)


def _loss_head(x, final_g, target, name):
    t, d = x.shape
    tm = _tm()

    def loss_fn(xv, g, tgt):
        y = (xv * lax.rsqrt(jnp.mean(xv * xv, axis=-1, keepdims=True) + EPS)) * g
        err = y - tgt
        return 0.5 * jnp.sum(jnp.mean(err * err, axis=-1, keepdims=True))

    def body(x_ref, g_ref, t_ref, l_ref, dx_ref, dg_ref):
        i = pl.program_id(0)

        @pl.when(i == 0)
        def _():
            l_ref[...] = jnp.zeros_like(l_ref)
            dg_ref[...] = jnp.zeros_like(dg_ref)
            dx_ref[...] = jnp.zeros_like(dx_ref)

        @pl.when(i > 0)
        def _():
            l, (dx, dg) = jax.value_and_grad(loss_fn, argnums=(0, 1))(x_ref[...], g_ref[...], t_ref[...])
            l_ref[...] += jnp.full(l_ref.shape, l, F32)
            dx_ref[...] = dx
            dg_ref[...] += dg

    row = pl.BlockSpec((tm, d), lambda i: (i, 0))
    vec = pl.BlockSpec((1, d), lambda i: (0, 0))
    return pl.pallas_call(
        body, name=name, grid=(t // tm,),
        in_specs=[row, vec, pl.BlockSpec((tm, d), lambda i: (jnp.maximum(i - 1, 0), 0))],
        out_specs=[pl.BlockSpec((SUBLANES, LANES), lambda i: (0, 0)), row, vec],
        out_shape=[jax.ShapeDtypeStruct((SUBLANES, LANES), F32), jax.ShapeDtypeStruct((t, d), F32),
                   jax.ShapeDtypeStruct((1, d), F32)],
        compiler_params=_params(("arbitrary",)))(x, final_g, target)


def _swap_quarters(x):
    half, nf = RET_DK // 2, RET_DK // 4
    lane = lax.broadcasted_iota(jnp.int32, x.shape, 1)
    return jnp.where((lane % half) < nf, pltpu.roll(x, RET_DK - nf, 1), pltpu.roll(x, nf, 1))


def _rope(x, cos, sin):
    return x * cos + _swap_quarters(x) * sin


def _rope_t(y, cos, sin):
    return y * cos + _swap_quarters(y * sin)


def _ret_consts(d):
    c = RET_CHUNK
    ii = lax.broadcasted_iota(jnp.int32, (c, 1), 0).astype(F32)
    jj = lax.broadcasted_iota(jnp.int32, (1, c), 1).astype(F32)
    fwd = d == 0
    sgn = jnp.where(fwd, 1.0, -1.0).astype(F32)
    pos = jnp.where(fwd, ii, c - 1.0 - ii)
    return sgn * (ii - jj), pos


def _ret_step(lgt, state, q, k, v, diff, pos):
    c = float(RET_CHUNK)
    lg = -(jnp.maximum(-lgt, 0.0) + jnp.log1p(jnp.exp(-jnp.abs(lgt))))
    lower = diff >= 0
    decay = jnp.where(lower, jnp.exp(jnp.where(lower, diff, 0.0) * lg), 0.0)
    xi = jnp.exp((pos + 1.0) * lg)
    zeta = jnp.exp((c - 1.0 - pos) * lg)
    gch = jnp.exp(c * lg)
    inner = dot_nt(q, k) * decay
    out = dot_nn(inner, v) + dot_nn(q, state) * xi
    new_state = state * gch + dot_tn(k * zeta, v)
    return out, new_state


def _chunk_order():
    nc, nch = CTX_LEN // RET_CHUNK, _t_rows() // RET_CHUNK
    fwd = list(range(nch))
    bwd = list(range(nc - 1, -1, -1)) + list(range(nch - 1, nc - 1, -1))
    return jnp.asarray(np.array([fwd, bwd], np.int32))


def _ret_fwd(p, cos, sin, decay, order, name):
    t = p.shape[0]
    c, dk, dv, nh = RET_CHUNK, RET_DK, RET_DV, RET_HEADS
    nch = t // c
    off = _offsets()
    wqk, wv = nh * dk, nh * dv
    assert off["q"] % wqk == 0 and off["k"] % wqk == 0 and off["v"] % wv == 0
    qb, kb, vb = off["q"] // wqk, off["k"] // wqk, off["v"] // wv
    scale = RET_DK ** -0.5

    def body(ord_ref, dec_ref, *refs):
        ins, o_refs, st_ref, state = refs[:10], refs[10:12], refs[12], refs[13]
        s = pl.program_id(0)

        @pl.when(s == 0)
        def _():
            state[...] = jnp.zeros_like(state)

        for d in range(2):
            q_ref, k_ref, v_ref, cos_ref, sin_ref = ins[5 * d:5 * d + 5]
            diff, pos = _ret_consts(d)
            cosv, sinv = cos_ref[...], sin_ref[...]
            for h in range(nh):
                st = state[d, h]
                st_ref[d, h] = st
                lgt = jnp.full((1, 1), dec_ref[d, h], F32)
                q = _rope(q_ref[:, h * dk:(h + 1) * dk], cosv, sinv) * scale
                k = _rope(k_ref[:, h * dk:(h + 1) * dk], cosv, sinv)
                out, ns = _ret_step(lgt, st, q, k, v_ref[:, h * dv:(h + 1) * dv], diff, pos)
                o_refs[d][:, h * dv:(h + 1) * dv] = out
                state[d, h] = ns

    def dir_specs(d):
        return [pl.BlockSpec((c, wqk), lambda s, o: (o[d, s], qb)), pl.BlockSpec((c, wqk), lambda s, o: (o[d, s], kb)),
                pl.BlockSpec((c, wv), lambda s, o: (o[d, s], vb)), pl.BlockSpec((c, dk), lambda s, o: (o[d, s], 0)),
                pl.BlockSpec((c, dk), lambda s, o: (o[d, s], 0))]

    grid_spec = pltpu.PrefetchScalarGridSpec(
        num_scalar_prefetch=1, grid=(nch,),
        in_specs=[pl.BlockSpec(memory_space=pltpu.SMEM)] + dir_specs(0) + dir_specs(1),
        out_specs=[pl.BlockSpec((c, wv), lambda s, o: (o[0, s], 0)), pl.BlockSpec((c, wv), lambda s, o: (o[1, s], 0)),
                   pl.BlockSpec((2, nh, None, dk, dv), lambda s, o: (0, 0, s, 0, 0))],
        scratch_shapes=[pltpu.VMEM((2, nh, dk, dv), F32)])
    return pl.pallas_call(
        body, name=name, grid_spec=grid_spec,
        out_shape=[jax.ShapeDtypeStruct((t, wv), F32), jax.ShapeDtypeStruct((t, wv), F32),
                   jax.ShapeDtypeStruct((2, nh, nch, dk, dv), F32)],
        compiler_params=_params(("arbitrary",)))(order, decay, p, p, p, cos, sin, p, p, p, cos, sin)


def _ret_bwd(p, cos, sin, decay, order, states, do, name):
    t = p.shape[0]
    c, dk, dv, nh = RET_CHUNK, RET_DK, RET_DV, RET_HEADS
    nch = t // c
    off = _offsets()
    wqk, wv = nh * dk, nh * dv
    qb, kb, vb = off["q"] // wqk, off["k"] // wqk, off["v"] // wv
    scale = RET_DK ** -0.5

    def body(ord_ref, dec_ref, *refs):
        ins, st_ref, outs, dd_ref, dstate = refs[:12], refs[12], refs[13:19], refs[19], refs[20]
        s = pl.program_id(0)

        @pl.when(s == 0)
        def _():
            dstate[...] = jnp.zeros_like(dstate)
            dd_ref[...] = jnp.zeros_like(dd_ref)

        for d in range(2):
            q_ref, k_ref, v_ref, cos_ref, sin_ref, do_ref = ins[6 * d:6 * d + 6]
            dq_ref, dk_ref, dv_ref = outs[3 * d:3 * d + 3]
            diff, pos = _ret_consts(d)
            cosv, sinv = cos_ref[...], sin_ref[...]
            for h in range(nh):
                qk, vv = slice(h * dk, (h + 1) * dk), slice(h * dv, (h + 1) * dv)
                lgt = jnp.full((1, 1), dec_ref[d, h], F32)
                q = _rope(q_ref[:, qk], cosv, sinv) * scale
                k = _rope(k_ref[:, qk], cosv, sinv)
                _, vjp = jax.vjp(lambda a, b, cq, ck, cv: _ret_step(a, b, cq, ck, cv, diff, pos),
                                 lgt, st_ref[d, h], q, k, v_ref[:, vv])
                dlgt, dst, dq, dkk, dvv = vjp((do_ref[:, vv], dstate[d, h]))
                dstate[d, h] = dst
                dq_ref[:, qk] = _rope_t(dq * scale, cosv, sinv)
                dk_ref[:, qk] = _rope_t(dkk, cosv, sinv)
                dv_ref[:, vv] = dvv
                dd_ref[d, h] += jnp.broadcast_to(dlgt, (SUBLANES, LANES))

    def chunk(d):
        return lambda s, o: o[d, nch - 1 - s]

    def dir_specs(d):
        at = chunk(d)
        return [pl.BlockSpec((c, wqk), lambda s, o: (at(s, o), qb)), pl.BlockSpec((c, wqk), lambda s, o: (at(s, o), kb)),
                pl.BlockSpec((c, wv), lambda s, o: (at(s, o), vb)), pl.BlockSpec((c, dk), lambda s, o: (at(s, o), 0)),
                pl.BlockSpec((c, dk), lambda s, o: (at(s, o), 0)), pl.BlockSpec((c, wv), lambda s, o: (at(s, o), 0))]

    def dir_outs(d):
        at = chunk(d)
        return [pl.BlockSpec((c, wqk), lambda s, o: (at(s, o), 0)), pl.BlockSpec((c, wqk), lambda s, o: (at(s, o), 0)),
                pl.BlockSpec((c, wv), lambda s, o: (at(s, o), 0))]

    grid_spec = pltpu.PrefetchScalarGridSpec(
        num_scalar_prefetch=1, grid=(nch,),
        in_specs=[pl.BlockSpec(memory_space=pltpu.SMEM)] + dir_specs(0) + dir_specs(1)
        + [pl.BlockSpec((2, nh, None, dk, dv), lambda s, o: (0, 0, nch - 1 - s, 0, 0))],
        out_specs=dir_outs(0) + dir_outs(1) + [pl.BlockSpec((2, nh, SUBLANES, LANES), lambda s, o: (0, 0, 0, 0))],
        scratch_shapes=[pltpu.VMEM((2, nh, dk, dv), F32)])
    qk_sds, v_sds = jax.ShapeDtypeStruct((t, wqk), F32), jax.ShapeDtypeStruct((t, wv), F32)
    res = pl.pallas_call(
        body, name=name, grid_spec=grid_spec,
        out_shape=[qk_sds, qk_sds, v_sds, qk_sds, qk_sds, v_sds, jax.ShapeDtypeStruct((2, nh, SUBLANES, LANES), F32)],
        compiler_params=_params(("arbitrary",)))(order, decay, p, p, p, cos, sin, do, p, p, p, cos, sin, do, states)
    return res[:3], res[3:6], res[6]


def _ggn_head(of, ob, gate, g):
    o = of + ob
    mu = jnp.mean(o, axis=-1, keepdims=True)
    var = jnp.mean(jnp.square(o - mu), axis=-1, keepdims=True)
    return ((o - mu) * lax.rsqrt(var + EPS) * g) * _silu(gate)


def _ggn_fwd(o_f, o_b, p, gn_g, name):
    t = p.shape[0]
    tm, w, dv = _tm(), _ret_w(), RET_DV
    gb = _offsets()["g"] // w

    def body(of_ref, ob_ref, gate_ref, g_ref, out_ref):
        for h in range(RET_HEADS):
            sl = slice(h * dv, (h + 1) * dv)
            out_ref[:, sl] = _ggn_head(of_ref[:, sl], ob_ref[:, sl], gate_ref[:, sl], g_ref[:, sl]).astype(BF16)

    row = pl.BlockSpec((tm, w), lambda i: (i, 0))
    return pl.pallas_call(
        body, name=name, grid=(t // tm,),
        in_specs=[row, row, pl.BlockSpec((tm, w), lambda i: (i, gb)), pl.BlockSpec((1, w), lambda i: (0, 0))],
        out_specs=row, out_shape=jax.ShapeDtypeStruct((t, w), BF16),
        compiler_params=_params(("parallel",)))(o_f, o_b, p, gn_g)


def _ggn_bwd(o_f, o_b, p, gn_g, dmix, name):
    t = p.shape[0]
    tm, w, dv = _tm(), _ret_w(), RET_DV
    gb = _offsets()["g"] // w

    def body(of_ref, ob_ref, gate_ref, g_ref, dy_ref, do_ref, dgate_ref, dg_ref):
        i = pl.program_id(0)

        @pl.when(i == 0)
        def _():
            dg_ref[...] = jnp.zeros_like(dg_ref)

        for h in range(RET_HEADS):
            sl = slice(h * dv, (h + 1) * dv)
            _, vjp = jax.vjp(_ggn_head, of_ref[:, sl], ob_ref[:, sl], gate_ref[:, sl], g_ref[:, sl])
            do, _, dgate, dg = vjp(dy_ref[:, sl])
            do_ref[:, sl] = do
            dgate_ref[:, sl] = dgate
            dg_ref[:, sl] += dg

    row = pl.BlockSpec((tm, w), lambda i: (i, 0))
    return pl.pallas_call(
        body, name=name, grid=(t // tm,),
        in_specs=[row, row, pl.BlockSpec((tm, w), lambda i: (i, gb)), pl.BlockSpec((1, w), lambda i: (0, 0)), row],
        out_specs=[row, row, pl.BlockSpec((1, w), lambda i: (0, 0))],
        out_shape=[jax.ShapeDtypeStruct((t, w), F32), jax.ShapeDtypeStruct((t, w), F32),
                   jax.ShapeDtypeStruct((1, w), F32)],
        compiler_params=_params(("arbitrary",)))(o_f, o_b, p, gn_g, dmix)


def _halo(k):
    return SUBLANES * ((k // 2 + SUBLANES - 1) // SUBLANES)


def _halo_specs(width, colblock, h, tm):
    r = tm // h
    return [pl.BlockSpec((h, width), lambda i, *_: (jnp.maximum(i * r - 1, 0), colblock(*_))),
            pl.BlockSpec((tm, width), lambda i, *_: (i, colblock(*_))),
            pl.BlockSpec((h, width), lambda i, *_: (jnp.minimum((i + 1) * r, (_t_rows() // h) - 1), colblock(*_)))]


def _fill_ext(ext_ref, prev, cur, nxt, i, h, tm):
    nt = _t_rows() // tm
    ext_ref[0:h, :] = jnp.where(i >= 2, prev, 0.0)
    ext_ref[h:h + tm, :] = cur
    ext_ref[h + tm:h + tm + h, :] = jnp.where((i >= 1) & (i <= nt - 2), nxt, 0.0)


def _corr(ext_ref, w_ref, k, h, tm, flip, cols=slice(None)):
    pad = k // 2
    acc = None
    for kk in range(k):
        o = h + (pad - kk if flip else kk - pad)
        term = w_ref[kk:kk + 1, cols] * ext_ref[o:o + tm, cols]
        acc = term if acc is None else acc + term
    return acc


LANE_CHUNK = 512


def _chunks(tm, tc):
    return [(r0, slice(c0, min(c0 + LANE_CHUNK, tc))) for r0 in range(0, tm, ROW_CHUNK) for c0 in range(0, tc, LANE_CHUNK)]


def _conv_post(u2, ln_g, ln_b, pw):
    mu = jnp.mean(u2, axis=-1, keepdims=True)
    var = jnp.mean(jnp.square(u2 - mu), axis=-1, keepdims=True)
    y = (u2 - mu) * lax.rsqrt(var + EPS) * ln_g + ln_b
    return dot_nn(_silu(y), pw)


def _conv_fwd(p, dw_w, dw_b, ln_g, ln_b, pw, name):
    t = p.shape[0]
    tm, w, k = _tm(), CONV_W, CONV_K
    h = _halo(k)
    off = _offsets()
    ab, bb = off["a"] // w, off["b"] // w

    def body(ap, ac, an, bp, bc, bn, w_ref, b_ref, g_ref, beta_ref, pw_ref, u2_ref, out_ref, ext):
        i = pl.program_id(0)
        glu = lambda a, b: a * _sigmoid(b)
        _fill_ext(ext, glu(ap[...], bp[...]), glu(ac[...], bc[...]), glu(an[...], bn[...]), i, h, tm)
        for r0 in range(0, tm, ROW_CHUNK):
            u2_ref[r0:r0 + ROW_CHUNK, :] = _corr(ext, w_ref, k, h + r0, ROW_CHUNK, False) + b_ref[...]
        out_ref[...] = _conv_post(u2_ref[...], g_ref[...], beta_ref[...], pw_ref[...]).astype(BF16)

    vec = pl.BlockSpec((1, w), lambda i: (0, 0))
    row = pl.BlockSpec((tm, w), lambda i: (i, 0))
    return pl.pallas_call(
        body, name=name, grid=(t // tm,),
        in_specs=_halo_specs(w, lambda: ab, h, tm) + _halo_specs(w, lambda: bb, h, tm)
        + [pl.BlockSpec((k, w), lambda i: (0, 0)), vec, vec, vec, pl.BlockSpec((w, w), lambda i: (0, 0))],
        out_specs=[row, row],
        out_shape=[jax.ShapeDtypeStruct((t, w), F32), jax.ShapeDtypeStruct((t, w), BF16)],
        scratch_shapes=[pltpu.VMEM((tm + 2 * h, w), F32)],
        compiler_params=_params(("parallel",)))(p, p, p, p, p, p, dw_w, dw_b, ln_g, ln_b, pw)


def _conv_bwd1(u2, dmix, ln_g, ln_b, pw, name):
    t = u2.shape[0]
    tm, w = _tm(), CONV_W
    cb = _ret_w() // w

    def body(u2_ref, dy_ref, g_ref, beta_ref, pw_ref, du2_ref, dg_ref, db_ref, dpw_ref):
        i = pl.program_id(0)

        @pl.when(i == 0)
        def _():
            dg_ref[...] = jnp.zeros_like(dg_ref)
            db_ref[...] = jnp.zeros_like(db_ref)
            dpw_ref[...] = jnp.zeros_like(dpw_ref)

        _, vjp = jax.vjp(_conv_post, u2_ref[...], g_ref[...], beta_ref[...], pw_ref[...])
        du2, dg, db, dpw = vjp(dy_ref[...])
        du2_ref[...] = du2
        dg_ref[...] += dg
        db_ref[...] += db
        dpw_ref[...] += dpw

    vec = pl.BlockSpec((1, w), lambda i: (0, 0))
    row = pl.BlockSpec((tm, w), lambda i: (i, 0))
    mat = pl.BlockSpec((w, w), lambda i: (0, 0))
    return pl.pallas_call(
        body, name=name, grid=(t // tm,),
        in_specs=[row, pl.BlockSpec((tm, w), lambda i: (i, cb)), vec, vec, mat],
        out_specs=[row, vec, vec, mat],
        out_shape=[jax.ShapeDtypeStruct((t, w), F32), jax.ShapeDtypeStruct((1, w), F32),
                   jax.ShapeDtypeStruct((1, w), F32), jax.ShapeDtypeStruct((w, w), F32)],
        compiler_params=_params(("arbitrary",)))(u2, dmix, ln_g, ln_b, pw)


def _conv_bwd2(du2, p, dw_w, name):
    t = p.shape[0]
    tm, w, k = _tm(), CONV_W, CONV_K
    h = _halo(k)
    pad = k // 2
    off = _offsets()
    ab, bb = off["a"] // w, off["b"] // w

    def body(dp, dc, dn, ap, ac, an, bp, bc, bn, w_ref, da_ref, db_ref, dw_ref, dbias_ref, ext_d, ext_u):
        i = pl.program_id(0)

        @pl.when(i == 0)
        def _():
            dw_ref[...] = jnp.zeros_like(dw_ref)
            dbias_ref[...] = jnp.zeros_like(dbias_ref)

        glu = lambda a, b: a * _sigmoid(b)
        _fill_ext(ext_d, dp[...], dc[...], dn[...], i, h, tm)
        _fill_ext(ext_u, glu(ap[...], bp[...]), glu(ac[...], bc[...]), glu(an[...], bn[...]), i, h, tm)
        chunks = range(0, tm, ROW_CHUNK)
        acc_b = jnp.zeros((ROW_CHUNK, w), F32)
        for r0 in chunks:
            rows = slice(r0, r0 + ROW_CHUNK)
            du = _corr(ext_d, w_ref, k, h + r0, ROW_CHUNK, True)
            sg = _sigmoid(bc[rows, :])
            da_ref[rows, :] = du * sg
            db_ref[rows, :] = du * ac[rows, :] * sg * (1.0 - sg)
            acc_b = acc_b + ext_d[h + r0:h + r0 + ROW_CHUNK, :]
        dbias_ref[...] += jnp.sum(acc_b, axis=0, keepdims=True)
        for kk in range(k):
            acc = jnp.zeros((ROW_CHUNK, w), F32)
            for r0 in chunks:
                o = h + r0 + kk - pad
                acc = acc + ext_d[h + r0:h + r0 + ROW_CHUNK, :] * ext_u[o:o + ROW_CHUNK, :]
            dw_ref[kk:kk + 1, :] += jnp.sum(acc, axis=0, keepdims=True)

    vec = pl.BlockSpec((1, w), lambda i: (0, 0))
    row = pl.BlockSpec((tm, w), lambda i: (i, 0))
    kw = pl.BlockSpec((k, w), lambda i: (0, 0))
    return _call(
        body, name=name, grid=(t // tm,),
        in_specs=_halo_specs(w, lambda: 0, h, tm) + _halo_specs(w, lambda: ab, h, tm)
        + _halo_specs(w, lambda: bb, h, tm) + [kw],
        out_specs=[row, row, kw, vec],
        out_shape=[jax.ShapeDtypeStruct((t, w), F32), jax.ShapeDtypeStruct((t, w), F32),
                   jax.ShapeDtypeStruct((k, w), F32), jax.ShapeDtypeStruct((1, w), F32)],
        scratch=[pltpu.VMEM((tm + 2 * h, w), F32), pltpu.VMEM((tm + 2 * h, w), F32)],
        sem=("arbitrary",), args=(du2, du2, du2, p, p, p, p, p, p, dw_w))


def _ffn_tc():
    return _tile(D_FF, 2816, LANES)


def _ffn_act_fwd(u, dw_w, dw_b, name):
    t = u.shape[0]
    tm, k, tc = _tm(), FFN_K, _ffn_tc()
    h = _halo(k)
    nj = D_FF // tc

    def body(vp, vc, vn, gp, gc, gn, wv, wg, bv, bg, out_ref, out_t_ref, ext_v, ext_g):
        i = pl.program_id(0)
        _fill_ext(ext_v, vp[...], vc[...], vn[...], i, h, tm)
        _fill_ext(ext_g, gp[...], gc[...], gn[...], i, h, tm)
        for r0, cols in _chunks(tm, tc):
            val = _corr(ext_v, wv, k, h + r0, ROW_CHUNK, False, cols) + bv[:, cols]
            gate = _corr(ext_g, wg, k, h + r0, ROW_CHUNK, False, cols) + bg[:, cols]
            out_ref[r0:r0 + ROW_CHUNK, cols] = (_silu(gate) * val).astype(BF16)
        out_t_ref[...] = out_ref[...].T

    wspec = lambda s: pl.BlockSpec((k, tc), lambda i, j: (0, j + s))
    bspec = lambda s: pl.BlockSpec((1, tc), lambda i, j: (0, j + s))
    return pl.pallas_call(
        body, name=name, grid=(t // tm, nj),
        in_specs=_halo_specs(tc, lambda j: j, h, tm) + _halo_specs(tc, lambda j: j + nj, h, tm)
        + [wspec(0), wspec(nj), bspec(0), bspec(nj)],
        out_specs=[pl.BlockSpec((tm, tc), lambda i, j: (i, j)), pl.BlockSpec((tc, tm), lambda i, j: (j, i))],
        out_shape=[jax.ShapeDtypeStruct((t, D_FF), BF16), jax.ShapeDtypeStruct((D_FF, t), BF16)],
        scratch_shapes=[pltpu.VMEM((tm + 2 * h, tc), F32), pltpu.VMEM((tm + 2 * h, tc), F32)],
        compiler_params=_params(("parallel", "parallel")))(u, u, u, u, u, u, dw_w, dw_w, dw_b, dw_b)


def _ffn_act_bwd1(u, da, dw_w, dw_b, name):
    t = u.shape[0]
    tm, k, tc = _tm(), FFN_K, _ffn_tc()
    h = _halo(k)
    nj = D_FF // tc

    def body(vp, vc, vn, gp, gc, gn, wv, wg, bv, bg, da_ref, dv_ref, dg_ref, ext_v, ext_g):
        i = pl.program_id(0)
        _fill_ext(ext_v, vp[...], vc[...], vn[...], i, h, tm)
        _fill_ext(ext_g, gp[...], gc[...], gn[...], i, h, tm)
        for r0, cols in _chunks(tm, tc):
            rows = slice(r0, r0 + ROW_CHUNK)
            val = _corr(ext_v, wv, k, h + r0, ROW_CHUNK, False, cols) + bv[:, cols]
            gate = _corr(ext_g, wg, k, h + r0, ROW_CHUNK, False, cols) + bg[:, cols]
            _, vjp = jax.vjp(lambda a, b: _silu(b) * a, val, gate)
            dval, dgate = vjp(da_ref[rows, cols])
            dv_ref[rows, cols] = dval
            dg_ref[rows, cols] = dgate

    wspec = lambda s: pl.BlockSpec((k, tc), lambda i, j: (0, j + s))
    bspec = lambda s: pl.BlockSpec((1, tc), lambda i, j: (0, j + s))
    dc = pl.pallas_call(
        body, name=name, grid=(t // tm, nj),
        in_specs=_halo_specs(tc, lambda j: j, h, tm) + _halo_specs(tc, lambda j: j + nj, h, tm)
        + [wspec(0), wspec(nj), bspec(0), bspec(nj), pl.BlockSpec((tm, tc), lambda i, j: (i, j))],
        out_specs=[pl.BlockSpec((tm, tc), lambda i, j: (i, j)), pl.BlockSpec((tm, tc), lambda i, j: (i, j))],
        out_shape=[jax.ShapeDtypeStruct((t, D_FF), F32), jax.ShapeDtypeStruct((t, D_FF), F32)],
        scratch_shapes=[pltpu.VMEM((tm + 2 * h, tc), F32), pltpu.VMEM((tm + 2 * h, tc), F32)],
        compiler_params=_params(("parallel", "parallel")))(u, u, u, u, u, u, dw_w, dw_w, dw_b, dw_b, da)
    return dc


def _dwconv_bwd(dcv, dcg, u, dw_w, name):
    t = u.shape[0]
    tm, k, tc = _tm(), FFN_K, _ffn_tc()
    h = _halo(k)
    pad = k // 2
    nj = D_FF // tc

    def body(vp, vc, vn, gp, gc, gn, up, uc, un, w_ref, du_ref, dw_ref, dbias_ref, ext_d, ext_u):
        jj, i = pl.program_id(0), pl.program_id(1)

        @pl.when(i == 0)
        def _():
            dw_ref[...] = jnp.zeros_like(dw_ref)
            dbias_ref[...] = jnp.zeros_like(dbias_ref)

        @pl.when(jj < nj)
        def _():
            _fill_ext(ext_d, vp[...], vc[...], vn[...], i, h, tm)

        @pl.when(jj >= nj)
        def _():
            _fill_ext(ext_d, gp[...], gc[...], gn[...], i, h, tm)

        _fill_ext(ext_u, up[...], uc[...], un[...], i, h, tm)
        for c0 in range(0, tc, LANE_CHUNK):
            cols = slice(c0, min(c0 + LANE_CHUNK, tc))
            width = cols.stop - cols.start
            acc_b = jnp.zeros((ROW_CHUNK, width), F32)
            acc_w = [jnp.zeros((ROW_CHUNK, width), F32) for _ in range(k)]
            for r0 in range(0, tm, ROW_CHUNK):
                d = ext_d[h + r0:h + r0 + ROW_CHUNK, cols]
                du_ref[r0:r0 + ROW_CHUNK, cols] = _corr(ext_d, w_ref, k, h + r0, ROW_CHUNK, True, cols).astype(BF16)
                acc_b = acc_b + d
                for kk in range(k):
                    o = h + r0 + kk - pad
                    acc_w[kk] = acc_w[kk] + d * ext_u[o:o + ROW_CHUNK, cols]
            dbias_ref[:, cols] += jnp.sum(acc_b, axis=0, keepdims=True)
            for kk in range(k):
                dw_ref[kk:kk + 1, cols] += jnp.sum(acc_w[kk], axis=0, keepdims=True)

    def hs(cb, live):
        r = tm // h
        row = lambda j, i: jnp.where(live(j), i, 0)
        return [pl.BlockSpec((h, tc), lambda j, i: (jnp.maximum(row(j, i) * r - 1, 0), cb(j))),
                pl.BlockSpec((tm, tc), lambda j, i: (row(j, i), cb(j))),
                pl.BlockSpec((h, tc), lambda j, i: (jnp.minimum((row(j, i) + 1) * r, (_t_rows() // h) - 1), cb(j)))]

    return pl.pallas_call(
        body, name=name, grid=(2 * nj, t // tm),
        in_specs=hs(lambda j: jnp.minimum(j, nj - 1), lambda j: j < nj)
        + hs(lambda j: jnp.maximum(j - nj, 0), lambda j: j >= nj)
        + hs(lambda j: j, lambda j: True) + [pl.BlockSpec((k, tc), lambda j, i: (0, j))],
        out_specs=[pl.BlockSpec((tm, tc), lambda j, i: (i, j)), pl.BlockSpec((k, tc), lambda j, i: (0, j)),
                   pl.BlockSpec((1, tc), lambda j, i: (0, j))],
        out_shape=[jax.ShapeDtypeStruct((t, 2 * D_FF), BF16), jax.ShapeDtypeStruct((k, 2 * D_FF), F32),
                   jax.ShapeDtypeStruct((1, 2 * D_FF), F32)],
        scratch_shapes=[pltpu.VMEM((tm + 2 * h, tc), F32), pltpu.VMEM((tm + 2 * h, tc), F32)],
        compiler_params=_params(("parallel", "arbitrary")))(dcv, dcv, dcv, dcg, dcg, dcg, u, u, u, dw_w)


def _na_geometry(rq):
    ncb = CTX_LEN // GRID_W
    rows_n = SEQ // GRID_W
    r = jnp.maximum(rq - ncb, 0)
    kstart = jnp.clip(r - NA_ROWS // 2, 0, rows_n - NA_ROWS)
    base = kstart - r + NA_ROWS - 1
    return rq >= ncb, kstart, base


def _na_core(q, kl, vl, kc, vc, bias, mask):
    qs = q * (NA_DH ** -0.5)
    s_l = jnp.where(mask, dot_nt(qs, kl) + bias, NEG)
    s_c = dot_nt(qs, kc)
    m = lax.stop_gradient(jnp.maximum(jnp.max(s_l, axis=1, keepdims=True), jnp.max(s_c, axis=1, keepdims=True)))
    e_l, e_c = jnp.exp(s_l - m), jnp.exp(s_c - m)
    inv = 1.0 / (jnp.sum(e_l, axis=1, keepdims=True) + jnp.sum(e_c, axis=1, keepdims=True))
    return dot_nn(e_l * inv, vl) + dot_nn(e_c * inv, vc)


def _na_mask(is_lat):
    nl = NA_ROWS * GRID_W
    q = lax.broadcasted_iota(jnp.int32, (GRID_W, nl), 0)
    w = lax.broadcasted_iota(jnp.int32, (GRID_W, nl), 1) % GRID_W
    cs = jnp.clip(q - NA_COLS // 2, 0, GRID_W - NA_COLS)
    return (w >= cs) & (w < cs + NA_COLS) & is_lat


def _na_bias(rb_ref):
    assert 2 * GRID_W == LANES
    lane = lax.broadcasted_iota(jnp.int32, (GRID_W, LANES), 1)
    tiles = []
    for kp in range(NA_ROWS // 2):
        ev = jnp.broadcast_to(rb_ref[2 * kp:2 * kp + 1, :], (GRID_W, LANES))
        od = jnp.broadcast_to(rb_ref[2 * kp + 1:2 * kp + 2, :], (GRID_W, LANES))
        ev = pltpu.roll(ev, LANES - (NA_COLS - 1), 1, stride=1, stride_axis=0)
        od = pltpu.roll(od, LANES - (NA_COLS - 1) - GRID_W, 1, stride=1, stride_axis=0)
        tiles.append(jnp.where(lane < GRID_W, ev, od))
    return jnp.concatenate(tiles, axis=1)


def _na_dbias(dbias, drb_ref):
    qi = lax.broadcasted_iota(jnp.int32, (GRID_W, GRID_W), 0)
    qj = lax.broadcasted_iota(jnp.int32, (GRID_W, GRID_W), 1)
    flip = (qi + qj == GRID_W - 1).astype(F32)
    rev = lax.dot_general(flip, dbias, (((1,), (0,)), ((), ())), precision=lax.Precision.HIGHEST,
                          preferred_element_type=F32)
    lane = lax.broadcasted_iota(jnp.int32, (GRID_W, LANES), 1)
    s_ev = LANES - (GRID_W - NA_COLS)
    for kp in range(NA_ROWS // 2):
        tile = rev[:, kp * LANES:(kp + 1) * LANES]
        ev = pltpu.roll(jnp.where(lane < GRID_W, tile, 0.0), s_ev, 1, stride=1, stride_axis=0)
        od = pltpu.roll(jnp.where(lane >= GRID_W, tile, 0.0), s_ev - GRID_W, 1, stride=1, stride_axis=0)
        drb_ref[2 * kp:2 * kp + 1, :] += jnp.sum(ev, axis=0, keepdims=True)
        drb_ref[2 * kp + 1:2 * kp + 2, :] += jnp.sum(od, axis=0, keepdims=True)


def _na_hps():
    return 2 if NA_HEADS % 2 == 0 else 1


def _na_specs(p_offsets):
    t = _t_rows()
    hps = _na_hps()
    wd = hps * NA_DH
    assert all(p_offsets[n] % wd == 0 for n in ("nq", "nk", "nv"))
    qb, kb, vb = (p_offsets[n] // wd for n in ("nq", "nk", "nv"))
    return [pl.BlockSpec((GRID_W, wd), lambda h, r: (r, qb + h)),
            pl.BlockSpec((t, wd), lambda h, r: (0, kb + h)),
            pl.BlockSpec((t, wd), lambda h, r: (0, vb + h)),
            pl.BlockSpec((hps, None, NA_ROWS, LANES), lambda h, r: (h, _na_geometry(r)[2], 0, 0))]


def _na_fwd(p, rb, name):
    t = p.shape[0]
    dh, nl = NA_DH, NA_ROWS * GRID_W

    hps = _na_hps()

    def body(q_ref, k_ref, v_ref, rb_ref, out_ref):
        rq = pl.program_id(1)
        is_lat, kstart, _ = _na_geometry(rq)
        start = pl.multiple_of(CTX_LEN + kstart * GRID_W, GRID_W)
        mask = _na_mask(is_lat)
        for hh in range(hps):
            cols = slice(hh * dh, (hh + 1) * dh)
            out = _na_core(q_ref[:, cols], k_ref[pl.ds(start, nl), cols], v_ref[pl.ds(start, nl), cols],
                           k_ref[0:CTX_LEN, cols], v_ref[0:CTX_LEN, cols], _na_bias(rb_ref.at[hh]), mask)
            out_ref[:, cols] = out.astype(BF16)

    return _call(
        body, name=name, grid=(NA_HEADS // hps, t // GRID_W), in_specs=_na_specs(_offsets()),
        out_specs=[pl.BlockSpec((GRID_W, hps * dh), lambda h, r: (r, h))],
        out_shape=[jax.ShapeDtypeStruct((t, _na_w()), BF16)],
        sem=("parallel", "arbitrary"), args=(p, p, p, rb))[0]


def _na_bwd(p, rb, dmix, name):
    t = p.shape[0]
    dh, nl = NA_DH, NA_ROWS * GRID_W

    hps = _na_hps()
    wd = hps * dh
    assert ((_ret_w() + CONV_W) // dh) % hps == 0
    ob = (_ret_w() + CONV_W) // wd

    def body(q_ref, k_ref, v_ref, rb_ref, dy_ref, dq_ref, dk_ref, dv_ref, drb_ref):
        rq = pl.program_id(1)
        is_lat, kstart, base = _na_geometry(rq)
        _, _, prev_base = _na_geometry(rq - 1)
        start = pl.multiple_of(CTX_LEN + kstart * GRID_W, GRID_W)

        @pl.when(rq == 0)
        def _():
            dk_ref[...] = jnp.zeros_like(dk_ref)
            dv_ref[...] = jnp.zeros_like(dv_ref)

        @pl.when((rq == 0) | (base != prev_base))
        def _():
            drb_ref[...] = jnp.zeros_like(drb_ref)

        mask = _na_mask(is_lat)
        for hh in range(hps):
            cols = slice(hh * dh, (hh + 1) * dh)
            _, vjp = jax.vjp(lambda *a: _na_core(*a, mask), q_ref[:, cols], k_ref[pl.ds(start, nl), cols],
                             v_ref[pl.ds(start, nl), cols], k_ref[0:CTX_LEN, cols], v_ref[0:CTX_LEN, cols],
                             _na_bias(rb_ref.at[hh]))
            dq, dkl, dvl, dkc, dvc, dbias = vjp(dy_ref[:, cols])
            dq_ref[:, cols] = dq
            dk_ref[pl.ds(start, nl), cols] += dkl
            dv_ref[pl.ds(start, nl), cols] += dvl
            dk_ref[0:CTX_LEN, cols] += dkc
            dv_ref[0:CTX_LEN, cols] += dvc
            _na_dbias(dbias, drb_ref.at[hh])

    return _call(
        body, name=name, grid=(NA_HEADS // hps, t // GRID_W),
        in_specs=_na_specs(_offsets()) + [pl.BlockSpec((GRID_W, wd), lambda h, r: (r, ob + h))],
        out_specs=[pl.BlockSpec((GRID_W, wd), lambda h, r: (r, h)), pl.BlockSpec((t, wd), lambda h, r: (0, h)),
                   pl.BlockSpec((t, wd), lambda h, r: (0, h)),
                   pl.BlockSpec((hps, None, NA_ROWS, LANES), lambda h, r: (h, _na_geometry(r)[2], 0, 0))],
        out_shape=[jax.ShapeDtypeStruct((t, _na_w()), F32), jax.ShapeDtypeStruct((t, _na_w()), F32),
                   jax.ShapeDtypeStruct((t, _na_w()), F32),
                   jax.ShapeDtypeStruct((NA_HEADS, NA_ROWS, NA_ROWS, LANES), F32)],
        sem=("parallel", "arbitrary"), args=(p, p, p, rb, dmix))


def _rpb_select():
    sel = np.zeros((2 * NA_ROWS - 1, NA_ROWS * NA_ROWS), np.float32)
    for b in range(NA_ROWS):
        for kh in range(NA_ROWS):
            sel[b + kh, b * NA_ROWS + kh] = 1.0
    return jnp.asarray(sel)


def _rpb_rows(rpb):
    pad = jnp.pad(rpb, ((0, 0), (0, 0), (0, LANES - (2 * NA_COLS - 1))))
    rows = jnp.einsum("rk,hrc->hkc", _rpb_select(), pad, precision=lax.Precision.HIGHEST)
    return rows.reshape(NA_HEADS, NA_ROWS, NA_ROWS, LANES)


def _rpb_rows_t(drb):
    flat = drb.reshape(NA_HEADS, NA_ROWS * NA_ROWS, LANES)
    out = jnp.einsum("rk,hkc->hrc", _rpb_select(), flat, precision=lax.Precision.HIGHEST)
    return out[:, :, :2 * NA_COLS - 1]


def _assemble_dp(d_fwd, d_bwd, dgate, da, db, dnq, dnk, dnv, name):
    t = dgate.shape[0]
    tm = _tm()
    off = _offsets()
    sizes = dict(q=_ret_qk_w(), k=_ret_qk_w(), v=_ret_w(), g=_ret_w(), a=CONV_W, b=CONV_W, nq=_na_w(), nk=_na_w(), nv=_na_w())

    def body(qf_ref, kf_ref, vf_ref, qb_ref, kb_ref, vb_ref, g_ref, a_ref, b_ref, nq_ref, nk_ref, nv_ref, o_ref):
        def put(n, val):
            o_ref[:, off[n]:off[n] + sizes[n]] = val.astype(BF16)

        put("q", qf_ref[...] + qb_ref[...])
        put("k", kf_ref[...] + kb_ref[...])
        put("v", vf_ref[...] + vb_ref[...])
        put("g", g_ref[...])
        put("a", a_ref[...])
        put("b", b_ref[...])
        put("nq", nq_ref[...])
        put("nk", nk_ref[...])
        put("nv", nv_ref[...])

    one = lambda w: pl.BlockSpec((tm, w), lambda i: (i, 0))
    qkv = [one(sizes["q"]), one(sizes["k"]), one(sizes["v"])]
    return pl.pallas_call(
        body, name=name, grid=(t // tm,),
        in_specs=qkv + qkv + [one(sizes["g"]), one(CONV_W), one(CONV_W), one(_na_w()), one(_na_w()), one(_na_w())],
        out_specs=one(_d_in()), out_shape=jax.ShapeDtypeStruct((t, _d_in()), BF16),
        compiler_params=_params(("parallel",)))(*d_fwd, *d_bwd, dgate, da, db, dnq, dnk, dnv)


def _adamw(w, m, v, gs, name):
    nl, r, c = w.shape
    stacked = not isinstance(gs, (list, tuple))
    if stacked:
        gs = [gs]
    assert stacked or len(gs) == nl
    g_n = gs[0].shape[-3]
    block_bytes = 2 * 1024 * 1024
    rows = min(block_bytes // (4 * c), block_bytes // (g_n * c * gs[0].dtype.itemsize))
    tr = _tile(r, max(2 * SUBLANES, rows // (2 * SUBLANES) * (2 * SUBLANES)), 2 * SUBLANES)
    nt = r // tr
    c1 = 1.0 - ADAM_B1 ** ADAM_STEP
    c2 = 1.0 - ADAM_B2 ** ADAM_STEP

    def body(w_ref, m_ref, v_ref, *rest):
        g_refs, (go_ref, d_ref, mo_ref, vo_ref) = rest[:len(gs)], rest[len(gs):]
        layer = pl.program_id(0)
        for ll in range(len(gs)):
            @pl.when(jnp.logical_or(stacked, layer == ll))
            def _():
                g_ref = g_refs[ll]
                g = g_ref[0].astype(F32)
                for j in range(1, g_n):
                    g = g + g_ref[j].astype(F32)
                mn = ADAM_B1 * m_ref[...] + (1.0 - ADAM_B1) * g
                vn = ADAM_B2 * v_ref[...] + (1.0 - ADAM_B2) * (g * g)
                m_hat = mn / c1
                v_hat = vn / c2
                go_ref[...] = g
                d_ref[...] = -ADAM_LR * (m_hat / (jnp.sqrt(v_hat) + ADAM_EPS) + ADAM_WD * w_ref[...])
                mo_ref[...] = mn
                vo_ref[...] = vn

    def g_spec(ll):
        if stacked:
            return pl.BlockSpec((None, g_n, tr, c), lambda l, i: (l, 0, i, 0))
        return pl.BlockSpec((g_n, tr, c), lambda l, i: (0, jnp.where(l == ll, i, jnp.where(l < ll, 0, nt - 1)), 0))

    blk = pl.BlockSpec((None, tr, c), lambda l, i: (l, i, 0))
    sds = jax.ShapeDtypeStruct((nl, r, c), F32)
    return _call(
        body, name=name, grid=(nl, nt),
        in_specs=[blk, blk, blk] + [g_spec(ll) for ll in range(len(gs))],
        out_specs=[blk, blk, blk, blk], out_shape=[sds, sds, sds, sds],
        sem=("arbitrary", "arbitrary"), args=(w, m, v, *gs))


def _sum_devices(g, name):
    _, r, c = g.shape
    tr = _tile(r, 512, SUBLANES)

    def body(g_ref, o_ref):
        acc = g_ref[0]
        for j in range(1, N_DEV):
            acc = acc + g_ref[j]
        o_ref[...] = acc

    return pl.pallas_call(body, name=name, grid=(r // tr,), in_specs=[pl.BlockSpec((N_DEV, tr, c), lambda i: (0, i, 0))],
                          out_specs=pl.BlockSpec((tr, c), lambda i: (i, 0)), out_shape=jax.ShapeDtypeStruct((r, c), F32),
                          compiler_params=_params(("parallel",)))(g)


def _ada_fwd(c16, w_ada, b_shard, name):
    nl, d, cs = w_ada.shape
    tk = _tile(d, 512, LANES)
    nk = d // tk

    def body(c_ref, w_ref, b_ref, o_ref):
        kk = pl.program_id(1)

        @pl.when(kk == 0)
        def _():
            o_ref[...] = jnp.broadcast_to(b_ref[...], o_ref.shape)

        o_ref[...] += _dg(_silu(c_ref[...]), w_ref[...], 1, 0)

    return pl.pallas_call(
        body, name=name, grid=(nl, nk),
        in_specs=[pl.BlockSpec((16, tk), lambda l, kk: (0, kk)), pl.BlockSpec((None, tk, cs), lambda l, kk: (l, kk, 0)),
                  pl.BlockSpec((None, 1, cs), lambda l, kk: (l, 0, 0))],
        out_specs=pl.BlockSpec((None, 16, cs), lambda l, kk: (l, 0, 0)),
        out_shape=jax.ShapeDtypeStruct((nl, 16, cs), F32),
        compiler_params=_params(("parallel", "arbitrary")))(c16, w_ada, b_shard)


def _ada_bwd(c16, dm16, w_ada, name):
    nl, d, cs = w_ada.shape
    td = _tile(d, 512, LANES)

    def body(c_ref, dm_ref, w_ref, gw_ref, dc_ref):
        cv = c_ref[...]
        s, vjp = jax.vjp(_silu, cv)
        gw_ref[...] = _dg(s, dm_ref[...], 0, 0)
        ds = _dg(dm_ref[...], w_ref[...], 1, 1)
        dc_ref[...] = vjp(ds)[0]

    return pl.pallas_call(
        body, name=name, grid=(nl, d // td),
        in_specs=[pl.BlockSpec((16, td), lambda l, i: (0, i)), pl.BlockSpec((None, 16, cs), lambda l, i: (l, 0, 0)),
                  pl.BlockSpec((None, td, cs), lambda l, i: (l, i, 0))],
        out_specs=[pl.BlockSpec((None, td, cs), lambda l, i: (l, i, 0)), pl.BlockSpec((None, 16, td), lambda l, i: (l, 0, i))],
        out_shape=[jax.ShapeDtypeStruct((nl, d, cs), F32), jax.ShapeDtypeStruct((nl, 16, d), F32)],
        compiler_params=_params(("parallel", "parallel")))(c16, dm16, w_ada)


def _pack_rows(shape):
    n = int(np.prod(shape))
    return SUBLANES * (-(-n // (LANES * SUBLANES)))


def _pack(arrays, row_align):
    parts, total = [], 0
    for a in arrays:
        flat = a.reshape(-1).astype(F32)
        rows = _pack_rows(a.shape)
        total += rows
        parts += [flat, jnp.zeros((rows * LANES - flat.shape[0],), F32)]
    parts.append(jnp.zeros(((-total % row_align) * LANES,), F32))
    return jnp.concatenate([p for p in parts if p.shape[0]]).reshape(-1, LANES)


def _unpack(packed, shapes):
    out, r = [], 0
    for s in shapes:
        rows = _pack_rows(s)
        out.append(packed[r:r + rows].reshape(-1)[:int(np.prod(s))].reshape(s))
        r += rows
    return out


def _rope_tables():
    half, nf = RET_DK // 2, RET_DK // 4
    pos = jnp.arange(SEQ)
    row = (pos // GRID_W).astype(F32)
    col = (pos % GRID_W).astype(F32)
    inv = ROPE_BASE ** (-jnp.arange(nf, dtype=F32) / nf)
    ar, ac = row[:, None] * inv[None, :], col[:, None] * inv[None, :]
    cos = jnp.concatenate([jnp.cos(ar), jnp.cos(ar), jnp.cos(ac), jnp.cos(ac)], axis=-1)
    sin = jnp.concatenate([-jnp.sin(ar), jnp.sin(ar), -jnp.sin(ac), jnp.sin(ac)], axis=-1)
    cos = jnp.concatenate([jnp.ones((CTX_LEN, RET_DK), F32), cos], axis=0)
    sin = jnp.concatenate([jnp.zeros((CTX_LEN, RET_DK), F32), sin], axis=0)
    return cos, sin


def _layer_fwd(l, x, mod4, w, cst, arrived):
    n = lambda s: f"l{l}_{s}"
    d = D_MODEL
    h1, h1_t = _normmod_fwd(x, w["norm1_g"], mod4, 0, n("norm1"))
    w["w_in"] = _cols_from_shards(arrived("w_in", h1), n("w_in_cols"))
    p = _mm(h1, w["w_in"], n("proj_in"))
    o_f, o_b, states = _ret_fwd(p, cst["cos"], cst["sin"], w["ret_decay"], cst["order"], n("ret_fwd"))
    ret_out = _ggn_fwd(o_f, o_b, p, w["ret_gn_g"], n("ret_gn"))
    u2, conv_out = _conv_fwd(p, w["conv_dw_w"], w["conv_dw_b"], w["conv_ln_g"], w["conv_ln_b"], w["conv_pw"], n("conv_fwd"))
    na_out = _na_fwd(p, w["rb"], n("na_fwd"))
    mix = [ret_out, conv_out, na_out]
    w["w_out"] = arrived("w_out", na_out).reshape(_d_mix(), d)
    g1 = _mm_parts(mix, w["w_out"], n("proj_out"))
    x1 = _gate_res_fwd(x, g1, mod4, 2, n("res1"))
    h2, h2_t = _normmod_fwd(x1, w["norm2_g"], mod4, 1, n("norm2"))
    w["ffn_up"] = arrived("ffn_up", h2)
    u = _mm(h2, w["ffn_up"], n("ffn_up"), b3=True)
    a, a_t = _ffn_act_fwd(u, w["ffn_dw_w"], w["ffn_dw_b"], n("ffn_act"))
    w["ffn_down"] = arrived("ffn_down", a).reshape(D_FF, d)
    f = _mm(a, w["ffn_down"], n("ffn_down"))
    x2 = _gate_res_fwd(x1, f, mod4, 5, n("res2"))
    saved = dict(x=x, h1_t=h1_t, p=p, o_f=o_f, o_b=o_b, states=states, u2=u2, mix=mix, g1=g1, x1=x1, h2_t=h2_t, u=u, a_t=a_t, f=f)
    return x2, saved


def _layer_bwd(l, dx2, s, mod4, w, cst, send):
    n = lambda t: f"l{l}_{t}"
    d = D_MODEL
    dfg, dg2 = _gate_res_bwd(dx2, s["f"], mod4, 5, n("res2_bwd"))
    da = _mm(dfg, w["ffn_down"], n("ffn_down_dx"), tb=True)
    d_ffn_down = _mm(s["a_t"], dfg, n("ffn_down_dw"), out_dtype=BF16, tm_max=DW_TM)
    tok = send(("ffn_down", l), d_ffn_down.reshape(N_DEV, D_FF // N_DEV, d))
    dcv, dcg = _ffn_act_bwd1(s["u"], da, w["ffn_dw_w"], _after(w["ffn_dw_b"], tok), n("ffn_act_bwd"))
    du, d_ffn_dw_w, d_ffn_dw_b = _dwconv_bwd(dcv, dcg, s["u"], w["ffn_dw_w"], n("ffn_dw_bwd"))
    d_ffn_dw_b = d_ffn_dw_b[0]
    dh2 = _mm(du, w["ffn_up"], n("ffn_up_dx"), tb=True, b3=True)
    d_ffn_up = _mm(s["h2_t"], du, n("ffn_up_dw"), out_dtype=BF16, tm_max=DW_TM,
                      o_cs=2 * D_FF // N_DEV)
    tok = send(("ffn_up", l), d_ffn_up)
    (dx1, dn2, dsh2, dsc2) = _normmod_bwd(s["x1"], _after(w["norm2_g"], tok), mod4, 1, dh2, dx2, n("norm2_bwd"))
    dgg, dg1 = _gate_res_bwd(dx1, s["g1"], mod4, 2, n("res1_bwd"))
    dmix = _mm(dgg, w["w_out"], n("proj_out_dx"), tb=True)
    d_w_out = _mm(_transpose_parts(s["mix"], n("mix_t")), dgg, n("proj_out_dw"), out_dtype=BF16, tm_max=DW_TM)
    tok = send(("w_out", l), d_w_out.reshape(N_DEV, _d_mix() // N_DEV, d))
    do, dgate, dgn = _ggn_bwd(s["o_f"], s["o_b"], s["p"], _after(w["ret_gn_g"], tok), dmix, n("ret_gn_bwd"))
    d_fwd, d_bwd, ddec = _ret_bwd(s["p"], cst["cos"], cst["sin"], w["ret_decay"], cst["order"], s["states"], do, n("ret_bwd"))
    du2, dlng, dlnb, dpw = _conv_bwd1(s["u2"], dmix, w["conv_ln_g"], w["conv_ln_b"], w["conv_pw"], n("conv_bwd1"))
    dca, dcb, ddww, ddwb = _conv_bwd2(du2, s["p"], w["conv_dw_w"], n("conv_bwd2"))
    dnq, dnk, dnv, drb = _na_bwd(s["p"], w["rb"], dmix, n("na_bwd"))
    dp = _assemble_dp(d_fwd, d_bwd, dgate, dca, dcb, dnq, dnk, dnv, n("dproj"))
    h1_t = s["h1_t"]
    half = d // 2
    for i in range(2):
        d_w_in = _mm(h1_t[i * half:(i + 1) * half], dp, n(f"proj_in_dw{i}"), out_dtype=BF16, tm_max=DW_TM,
                        o_cs=_d_in() // N_DEV)
        tok = send(("w_in", l, i), d_w_in)
    dh1 = _mm(dp, w["w_in"], n("proj_in_dx"), tb=True, after=tok)
    (dx, dn1, dsh1, dsc1) = _normmod_bwd(s["x"], _after(w["norm1_g"], tok), mod4, 0, dh1, dx1, n("norm1_bwd"))
    dmod = jnp.concatenate([dsh1, dsc1, dg1, dsh2, dsc2, dg2], axis=1)
    small = dict(norm1_g=dn1[0], ret_decay=ddec[:, :, 0, 0], ret_gn_g=dgn[0], conv_dw_w=ddww, conv_dw_b=ddwb[0],
                 conv_ln_g=dlng[0], conv_ln_b=dlnb[0], conv_pw=dpw, na_rpb=_rpb_rows_t(drb), norm2_g=dn2[0],
                 ffn_dw_w=d_ffn_dw_w, ffn_dw_b=d_ffn_dw_b)
    return dx, dmod, small


def _d_mix():
    return _ret_w() + CONV_W + _na_w()


_SMALL = ["c_ctx", "b_ada", "norm1_g", "ret_decay", "ret_gn_g", "conv_dw_w", "conv_dw_b", "conv_ln_g", "conv_ln_b",
          "conv_pw", "na_rpb", "norm2_g", "ffn_dw_w", "ffn_dw_b", "final_g"]
_SMALL_SHARD_AXIS = {"conv_dw_w": 2, "conv_pw": 1, "ffn_dw_w": 2}


def kernel(x, c, ctx, c_ctx, w_ada, b_ada, norm1_g, w_in, ret_decay, ret_gn_g, conv_dw_w, conv_dw_b, conv_ln_g, conv_ln_b, conv_pw, na_rpb, w_out, norm2_g, ffn_up, ffn_dw_w, ffn_dw_b, ffn_down, final_g, loss_target, m_c_ctx, m_w_ada, m_b_ada, m_norm1_g, m_w_in, m_ret_decay, m_ret_gn_g, m_conv_dw_w, m_conv_dw_b, m_conv_ln_g, m_conv_ln_b, m_conv_pw, m_na_rpb, m_w_out, m_norm2_g, m_ffn_up, m_ffn_dw_w, m_ffn_dw_b, m_ffn_down, m_final_g, v_c_ctx, v_w_ada, v_b_ada, v_norm1_g, v_w_in, v_ret_decay, v_ret_gn_g, v_conv_dw_w, v_conv_dw_b, v_conv_ln_g, v_conv_ln_b, v_conv_pw, v_na_rpb, v_w_out, v_norm2_g, v_ffn_up, v_ffn_dw_w, v_ffn_dw_b, v_ffn_down, v_final_g):
    d, nl = D_MODEL, DEPTH
    cs = 6 * d // N_DEV
    me = _my_index()
    weights = dict(c_ctx=c_ctx, w_ada=w_ada, b_ada=b_ada, norm1_g=norm1_g, w_in=w_in, ret_decay=ret_decay, ret_gn_g=ret_gn_g,
                   conv_dw_w=conv_dw_w, conv_dw_b=conv_dw_b, conv_ln_g=conv_ln_g, conv_ln_b=conv_ln_b, conv_pw=conv_pw,
                   na_rpb=na_rpb, w_out=w_out, norm2_g=norm2_g, ffn_up=ffn_up, ffn_dw_w=ffn_dw_w, ffn_dw_b=ffn_dw_b,
                   ffn_down=ffn_down, final_g=final_g)
    mom = dict(c_ctx=m_c_ctx, w_ada=m_w_ada, b_ada=m_b_ada, norm1_g=m_norm1_g, w_in=m_w_in, ret_decay=m_ret_decay,
               ret_gn_g=m_ret_gn_g, conv_dw_w=m_conv_dw_w, conv_dw_b=m_conv_dw_b, conv_ln_g=m_conv_ln_g,
               conv_ln_b=m_conv_ln_b, conv_pw=m_conv_pw, na_rpb=m_na_rpb, w_out=m_w_out, norm2_g=m_norm2_g,
               ffn_up=m_ffn_up, ffn_dw_w=m_ffn_dw_w, ffn_dw_b=m_ffn_dw_b, ffn_down=m_ffn_down, final_g=m_final_g)
    var = dict(c_ctx=v_c_ctx, w_ada=v_w_ada, b_ada=v_b_ada, norm1_g=v_norm1_g, w_in=v_w_in, ret_decay=v_ret_decay,
               ret_gn_g=v_ret_gn_g, conv_dw_w=v_conv_dw_w, conv_dw_b=v_conv_dw_b, conv_ln_g=v_conv_ln_g,
               conv_ln_b=v_conv_ln_b, conv_pw=v_conv_pw, na_rpb=v_na_rpb, w_out=v_w_out, norm2_g=v_norm2_g,
               ffn_up=v_ffn_up, ffn_dw_w=v_ffn_dw_w, ffn_dw_b=v_ffn_dw_b, ffn_down=v_ffn_down, final_g=v_final_g)

    big_names = ["w_in", "w_out", "ffn_up", "ffn_down"]
    shards = {(nm, l): _cast_bf16(weights[nm][l], f"cast_{nm}{l}") for l in range(nl) for nm in big_names}
    small_sharded = _pack([conv_dw_w, conv_pw, ffn_dw_w], SUBLANES)
    c_rows = jnp.pad(c, ((0, SUBLANES - 1), (0, 0)))
    gathered = _run_comm(_Gather([c_rows, small_sharded, shards[("w_in", 0)]]), "gather_first")
    c_all = gathered[0][:, 0, :]
    def whole(rows, shard_shape, axis):
        n_el = int(np.prod(shard_shape))
        parts = rows.reshape(N_DEV, -1)[:, :n_el].reshape((N_DEV,) + tuple(shard_shape))
        parts = jnp.moveaxis(parts, 0, axis)
        return parts.reshape(shard_shape[:axis] + (N_DEV * shard_shape[axis],) + shard_shape[axis + 1:])

    r0 = _pack_rows(conv_dw_w.shape)
    r1 = r0 + _pack_rows(conv_pw.shape)
    r2 = r1 + _pack_rows(ffn_dw_w.shape)
    full_conv_dw_w = whole(gathered[1][:, :r0], conv_dw_w.shape, 2)
    full_conv_pw = whole(gathered[1][:, r0:r1], conv_pw.shape, 1)
    full_ffn_dw_w = whole(gathered[1][:, r1:r2], ffn_dw_w.shape, 2)

    c16 = jnp.concatenate([c_all, jnp.broadcast_to(c_ctx[None, :], (N_DEV, d))], axis=0)
    b_shard = lax.dynamic_slice_in_dim(b_ada, me * cs, cs, axis=1)[:, None, :]
    m_shard = _ada_fwd(c16, w_ada, b_shard, "ada_fwd")
    m_all = _run_comm(_Gather([m_shard.reshape(nl * 16, cs)]), "gather_mod")[0]
    m_full = m_all.reshape(N_DEV, nl, 16, cs).transpose(1, 2, 0, 3).reshape(nl, 16, 6 * d)
    m_lat = lax.dynamic_index_in_dim(m_full, me, axis=1, keepdims=False)
    mod = jnp.stack([m_full[:, N_DEV], m_lat], axis=1).reshape(nl, 2, 6, 1, d)

    arriving, token = {}, m_all
    for l in range(nl):
        for nm in big_names:
            if (nm, l) != ("w_in", 0):
                arriving[(nm, l)], token = _split_start(shards[(nm, l)], True, f"gather_{nm}{l}", token)
    mod = _after(mod, token)

    cos, sin = _rope_tables()
    cst = dict(cos=cos, sin=sin, order=_chunk_order())
    layer_w = []
    for l in range(nl):
        layer_w.append(dict(
            norm1_g=norm1_g[l][None], norm2_g=norm2_g[l][None], ret_decay=ret_decay[l], ret_gn_g=ret_gn_g[l][None],
            conv_dw_w=full_conv_dw_w[l], conv_dw_b=conv_dw_b[l][None], conv_ln_g=conv_ln_g[l][None],
            conv_ln_b=conv_ln_b[l][None], conv_pw=full_conv_pw[l], rb=_rpb_rows(na_rpb[l]),
            ffn_dw_w=full_ffn_dw_w[l], ffn_dw_b=ffn_dw_b[l][None]))

    xs = jnp.concatenate([ctx[0], x[0]], axis=0)
    saved = []
    for l in range(nl):
        def arrived(nm, after, l=l):
            if (nm, l) == ("w_in", 0):
                return gathered[2]
            return _split_wait(arriving[(nm, l)], after, f"arrived_{nm}{l}")

        xs, sv = _layer_fwd(l, xs, mod[l], layer_w[l], cst, arrived)
        saved.append(sv)
    loss_tile, dxs, dfinal = _loss_head(xs, final_g[None], loss_target[0], "loss_head")
    loss = lax.psum(loss_tile[0, 0], ("x", "y", "c"))

    dmods, smalls = [None] * nl, [None] * nl
    leaving, last = {}, [loss_tile]

    def send(key, partial):
        leaving[key], token = _split_start(partial, False, "send_" + "_".join(str(k) for k in key), last[0])
        last[0] = token
        return token

    per_layer = [nm for nm in _SMALL if nm not in ("c_ctx", "b_ada", "final_g")]
    small_packs, small_arriving = [None] * nl, [None] * nl
    for l in reversed(range(nl)):
        dxs, dmods[l], smalls[l] = _layer_bwd(l, dxs, saved[l], mod[l], layer_w[l], cst, send)
        small_packs[l] = _pack([smalls[l][nm] for nm in per_layer], 512)
        if l > 0:
            small_arriving[l], last[0] = _split_start(small_packs[l], True, f"gather_small_grads{l}", last[0])
    grad_x = dxs[CTX_LEN:][None]

    arrived_grad = lambda key, after: _split_wait(leaving[key], after, "got_" + "_".join(str(k) for k in key))
    out_big = {}
    after = dxs
    for nm in ["ffn_down", "ffn_up", "w_out"]:
        out_big[nm] = _adamw(weights[nm], mom[nm], var[nm], [arrived_grad((nm, l), after) for l in range(nl)],
                                f"adamw_{nm}")
        after = out_big[nm][0]

    dm_mine = jnp.stack(dmods).reshape(nl * 2, 6 * d)
    dm_rows = jnp.pad(dm_mine, ((0, SUBLANES - nl * 2), (0, 0)))
    dm_all = _run_comm(_Gather([dm_rows]), "gather_dmod", after=after)[0][:, :nl * 2].reshape(N_DEV, nl, 2, 6 * d)
    dm16_full = jnp.concatenate([dm_all[:, :, 1].transpose(1, 0, 2), dm_all[:, :, 0].transpose(1, 0, 2)], axis=1)
    dm16 = lax.dynamic_slice_in_dim(dm16_full, me * cs, cs, axis=2)
    g_w_ada, dc16 = _ada_bwd(c16, dm16, w_ada, "ada_bwd")

    shared = dict(c_ctx=jnp.sum(dc16[:, N_DEV:], axis=(0, 1)),
                  b_ada=jnp.sum(jnp.stack(dmods).reshape(nl, 2, 6 * d), axis=1), final_g=dfinal[0])
    shared_all = _run_comm(_Gather([_pack(list(shared.values()), SUBLANES)]), "gather_shared_grads")[0]
    small_arriving[0], token = _split_start(small_packs[0], True, "gather_small_grads0", shared_all)

    out_big["w_ada"] = _adamw(w_ada, m_w_ada, v_w_ada, g_w_ada[:, None], "adamw_w_ada")
    halves = lambda a: a.reshape(2 * nl, d // 2, a.shape[2])
    res = _adamw(halves(w_in), halves(m_w_in), halves(v_w_in),
                    [arrived_grad(("w_in", l, i), token) for l in range(nl) for i in range(2)], "adamw_w_in")
    out_big["w_in"] = [r.reshape(w_in.shape) for r in res]

    g_small = dict(zip(shared, _unpack(_sum_devices(shared_all, "sum_shared_grads"), [v.shape for v in shared.values()])))
    per = []
    for l in range(nl):
        got = _split_wait(small_arriving[l], res[0], f"arrived_small_grads{l}")
        per.append(_unpack(_sum_devices(got, f"sum_small_grads{l}"), [smalls[l][nm].shape for nm in per_layer]))
    g_small.update({nm: jnp.stack([per[l][i] for l in range(nl)]) for i, nm in enumerate(per_layer)})
    for nm, ax in _SMALL_SHARD_AXIS.items():
        n_sh = weights[nm].shape[ax]
        g_small[nm] = lax.dynamic_slice_in_dim(g_small[nm], me * n_sh, n_sh, axis=ax)
    shapes_own = [weights[nm].shape for nm in _SMALL]
    pk = lambda src: _pack([src[nm] for nm in _SMALL], 2 * SUBLANES)[None]
    res_small = _adamw(pk(weights), pk(mom), pk(var), pk(g_small)[:, None], "adamw_small")
    out_small = [dict(zip(_SMALL, _unpack(r[0], shapes_own))) for r in res_small]

    names = ["c_ctx", "w_ada", "b_ada", "norm1_g", "w_in", "ret_decay", "ret_gn_g", "conv_dw_w", "conv_dw_b", "conv_ln_g",
             "conv_ln_b", "conv_pw", "na_rpb", "w_out", "norm2_g", "ffn_up", "ffn_dw_w", "ffn_dw_b", "ffn_down", "final_g"]
    outs = [loss, grad_x]
    for kind in range(4):
        for nm in names:
            outs.append(out_big[nm][kind] if nm in out_big else out_small[kind][nm])
    return tuple(outs)
```

```python
import numpy as np
import jax
import jax.numpy as jnp
from jax import lax
from jax.experimental import pallas as pl
from jax.experimental.pallas import tpu as pltpu

D_MODEL = 2048
SEQ = 4096
DEPTH = 2
GRID_W = 64
CTX_LEN = 256
RET_HEADS = 4
RET_DK = 128
RET_DV = 256
RET_CHUNK = 128
CONV_W = 512
CONV_K = 31
NA_HEADS = 4
NA_DH = 128
NA_ROWS = 8
NA_COLS = 16
D_FF = 5632
FFN_K = 3
ROPE_BASE = 10000.0
EPS = 1e-6
ADAM_LR = 0.001
ADAM_B1 = 0.9
ADAM_B2 = 0.999
ADAM_EPS = 1e-08
ADAM_WD = 0.01
ADAM_STEP = 10
N_DEV = 8

LANES = 128
SUBLANES = 8
VMEM_LIMIT = 56 * 1024 * 1024
ROW_CHUNK = 32

F32 = jnp.float32
BF16 = jnp.bfloat16
MESH = pl.DeviceIdType.MESH
NEG = -1e30


def _ret_qk_w():
    return RET_HEADS * RET_DK


def _ret_w():
    return RET_HEADS * RET_DV


def _na_w():
    return NA_HEADS * NA_DH


def _d_in():
    return 2 * _ret_qk_w() + 2 * _ret_w() + 2 * CONV_W + 3 * _na_w()


def _offsets():
    sizes = [_ret_qk_w(), _ret_qk_w(), _ret_w(), _ret_w(), CONV_W, CONV_W, _na_w(), _na_w(), _na_w()]
    offs = [0]
    for s in sizes[:-1]:
        offs.append(offs[-1] + s)
    return dict(zip(["q", "k", "v", "g", "a", "b", "nq", "nk", "nv"], offs))


def _t_rows():
    return CTX_LEN + SEQ


def _tm():
    return CTX_LEN


def _params(sem=None):
    kw = dict(vmem_limit_bytes=VMEM_LIMIT)
    if sem is not None:
        kw["dimension_semantics"] = sem
    return pltpu.CompilerParams(**kw)


def _tile(n, pref, align):
    best = None
    for t in range(align, min(n, pref) + 1, align):
        if n % t == 0:
            best = t
    return best if best is not None else n


def _dg(a, b, ca, cb):
    return lax.dot_general(a.astype(BF16), b.astype(BF16), (((ca,), (cb,)), ((), ())), preferred_element_type=F32)


@jax.custom_vjp
def dot_nn(a, b):
    return _dg(a, b, 1, 0)


dot_nn.defvjp(lambda a, b: (_dg(a, b, 1, 0), (a, b)),
              lambda r, g: (_dg(g, r[1], 1, 1), _dg(r[0], g, 0, 0)))


@jax.custom_vjp
def dot_nt(a, b):
    return _dg(a, b, 1, 1)


dot_nt.defvjp(lambda a, b: (_dg(a, b, 1, 1), (a, b)),
              lambda r, g: (_dg(g, r[1], 1, 0), _dg(g, r[0], 0, 0)))


@jax.custom_vjp
def dot_tn(a, b):
    return _dg(a, b, 0, 0)


dot_tn.defvjp(lambda a, b: (_dg(a, b, 0, 0), (a, b)),
              lambda r, g: (_dg(r[1], g, 1, 1), _dg(r[0], g, 1, 0)))


def _sigmoid(x):
    return 0.5 * jnp.tanh(0.5 * x) + 0.5


def _silu(x):
    return x * _sigmoid(x)


def _my_pos():
    return lax.axis_index("x"), lax.axis_index("y"), lax.axis_index("c")


def _my_index():
    x, y, c = _my_pos()
    return 4 * x + 2 * y + c


_ANY = pl.BlockSpec(memory_space=pl.ANY)


class _Gather:
    def __init__(self, arrays):
        self.arrays = list(arrays)
        n = len(self.arrays)
        self.out_shape = [jax.ShapeDtypeStruct((N_DEV,) + a.shape, a.dtype) for a in self.arrays]
        self.scratch = [pltpu.SemaphoreType.DMA((n, 7)), pltpu.SemaphoreType.DMA((n, 7)), pltpu.SemaphoreType.DMA((n,))]

    def _plan(self, xs, outs, sems):
        send_sems, recv_sems, local_sems = sems
        n = len(self.arrays)
        x, y, c = _my_pos()
        me, sibling = (x, y, c), (x, y, 1 - c)
        chips = [(1 - x, y), (x, 1 - y), (1 - x, 1 - y)]

        def slot(a, p):
            return outs[a].at[4 * p[0] + 2 * p[1] + p[2]]

        def copy(a, k, block, to, src=None):
            return pltpu.make_async_remote_copy(
                src_ref=slot(a, block) if src is None else src, dst_ref=slot(a, block),
                send_sem=send_sems.at[a, k], recv_sem=recv_sems.at[a, k], device_id=to, device_id_type=MESH)

        mine = [pltpu.make_async_copy(xs[a], slot(a, me), local_sems.at[a]) for a in range(n)]
        first = []
        for a in range(n):
            first.append(copy(a, 0, me, sibling, src=xs[a]))
            first += [copy(a, 1 + j, me, (*chip, c), src=xs[a]) for j, chip in enumerate(chips)]
        return n, c, me, sibling, chips, copy, mine, first

    def start(self, xs, outs, sems):
        _, _, _, _, _, _, mine, first = self._plan(xs, outs, sems)
        for m in mine:
            m.start()
        for cp in first:
            cp.start()

    def finish(self, xs, outs, sems):
        n, c, me, sibling, chips, copy, mine, first = self._plan(xs, outs, sems)
        passed = []
        for a in range(n):
            for j, chip in enumerate(chips):
                copy(a, 1 + j, (*chip, c), me).wait_recv()
                p = copy(a, 4 + j, (*chip, c), sibling)
                p.start()
                passed.append(p)
        for a in range(n):
            copy(a, 0, sibling, me).wait_recv()
            for j, chip in enumerate(chips):
                copy(a, 4 + j, (*chip, 1 - c), me).wait_recv()
        for cp in first + passed:
            cp.wait_send()
        for m in mine:
            m.wait()


def _run_comm(comm, name, after=None):
    n = len(comm.arrays)
    extra = [] if after is None else [after]

    def body(*refs):
        xs, outs, sems = refs[:n], refs[n + len(extra):2 * n + len(extra)], refs[2 * n + len(extra):]
        comm.start(xs, outs, sems)
        comm.finish(xs, outs, sems)

    return pl.pallas_call(body, name=name, out_shape=comm.out_shape, in_specs=[_ANY] * (n + len(extra)),
                          out_specs=[_ANY] * n, scratch_shapes=comm.scratch)(*comm.arrays, *extra)


_HBM = pl.BlockSpec(memory_space=pltpu.HBM)
_SEMS = pl.BlockSpec(memory_space=pltpu.SEMAPHORE)
_EFFECT = pltpu.SideEffectType.DATAFLOW_SIDE_EFFECTING


def _own_slot(x, gathering, name):
    shape = (N_DEV,) + x.shape if gathering else x.shape
    r, c = shape[1], shape[2]
    tr = _tile(r, 256, 2 * SUBLANES)
    me = jnp.reshape(_my_index(), (1,)).astype(jnp.int32)

    def body(me_ref, x_ref, o_ref):
        o_ref[...] = x_ref[...]

    src = (pl.BlockSpec((tr, c), lambda i, m: (i, 0)) if gathering
           else pl.BlockSpec((None, tr, c), lambda i, m: (m[0], i, 0)))
    grid_spec = pltpu.PrefetchScalarGridSpec(
        num_scalar_prefetch=1, grid=(r // tr,), in_specs=[src],
        out_specs=pl.BlockSpec((None, tr, c), lambda i, m: (m[0], i, 0)))
    return pl.pallas_call(body, name=name, grid_spec=grid_spec, out_shape=jax.ShapeDtypeStruct(shape, x.dtype),
                          compiler_params=_params(("arbitrary",)))(me, x)


def _split_plan(x_ref, land_ref, send_sems, recv_sems, gathering):
    x, y, c = _my_pos()
    me = 4 * x + 2 * y + c
    sends, recvs = [], []
    for k in range(1, N_DEV):
        px = 1 - x if (k >> 2) & 1 else x
        py = 1 - y if (k >> 1) & 1 else y
        pc = 1 - c if k & 1 else c
        peer = 4 * px + 2 * py + pc
        mine, theirs = (x_ref, x_ref) if gathering else (x_ref.at[peer], x_ref.at[me])
        sends.append(pltpu.make_async_remote_copy(
            src_ref=mine, dst_ref=land_ref.at[me], send_sem=send_sems.at[k - 1], recv_sem=recv_sems.at[k - 1],
            device_id=(px, py, pc), device_id_type=MESH))
        recvs.append(pltpu.make_async_remote_copy(
            src_ref=theirs, dst_ref=land_ref.at[peer], send_sem=send_sems.at[k - 1], recv_sem=recv_sems.at[k - 1],
            device_id=(px, py, pc), device_id_type=MESH))
    return sends, recvs


def _split_start(x, gathering, name, prev):
    land = _own_slot(x, gathering, name + "_own")

    def body(x_ref, land_ref, prev_ref, send_sems, recv_sems, x_thru, land_thru, token):
        sends, _ = _split_plan(x_ref, land_ref, send_sems, recv_sems, gathering)
        for s in sends:
            s.start()
        token[...] = jnp.zeros_like(token)

    sems = pltpu.SemaphoreType.DMA((N_DEV - 1,))
    send_sems, recv_sems, x_thru, land_thru, token = pl.pallas_call(
        body, name=name,
        out_shape=(sems, sems, pltpu.HBM(x.shape, x.dtype), pltpu.HBM(land.shape, land.dtype),
                   jax.ShapeDtypeStruct((SUBLANES, LANES), F32)),
        in_specs=(_HBM, _HBM, _ANY), out_specs=(_SEMS, _SEMS, _HBM, _HBM, pl.BlockSpec(memory_space=pltpu.VMEM)),
        input_output_aliases={0: 2, 1: 3},
        compiler_params=pltpu.CompilerParams(has_side_effects=_EFFECT),
    )(pltpu.with_memory_space_constraint(x, pltpu.HBM), pltpu.with_memory_space_constraint(land, pltpu.HBM), prev)
    return (send_sems, recv_sems, x_thru, land_thru, gathering), token


def _after(a, token):
    return a + token[0, 0].astype(a.dtype)


def _split_wait(handle, after, name):
    send_sems, recv_sems, x_thru, land_thru, gathering = handle

    def body(x_ref, land_ref, send_sems, recv_sems, after_ref, x_dead, got_ref):
        sends, recvs = _split_plan(x_ref, land_ref, send_sems, recv_sems, gathering)
        for s in sends:
            s.wait_send()
        for r in recvs:
            r.wait_recv()

    return pl.pallas_call(
        body, name=name, out_shape=(pltpu.HBM(x_thru.shape, x_thru.dtype), pltpu.HBM(land_thru.shape, land_thru.dtype)),
        in_specs=(_HBM, _HBM, _SEMS, _SEMS, _ANY), out_specs=(_HBM, _HBM), input_output_aliases={0: 0, 1: 1},
        compiler_params=pltpu.CompilerParams(has_side_effects=_EFFECT),
    )(x_thru, land_thru, send_sems, recv_sems, after)[1]


def _call(body, *, name, grid, in_specs, out_specs, out_shape, args, scratch=(), sem=None, after=None):
    if after is None:
        return list(pl.pallas_call(body, name=name, grid=grid, in_specs=list(in_specs), out_specs=list(out_specs),
                                   out_shape=list(out_shape), scratch_shapes=list(scratch),
                                   compiler_params=_params(sem))(*args))
    n_in = len(in_specs)

    def wrapped(*refs):
        body(*refs[:n_in], *refs[n_in + 1:])

    return list(pl.pallas_call(wrapped, name=name, grid=grid, in_specs=list(in_specs) + [_ANY], out_specs=list(out_specs),
                               out_shape=list(out_shape), scratch_shapes=list(scratch),
                               compiler_params=_params(sem))(*args, after))


MM_B_BLOCK_BYTES = 6 * 1024 * 1024
MM_O_BLOCK_BYTES = 13 * 1024 * 1024 // 2


def _mm(a, b, name, tb=False, out_dtype=F32, b3=False, o_cs=None, tm_max=1088, after=None):
    m, k = a.shape
    if b3:
        cs = b.shape[2]
        n, kb = (b.shape[1], N_DEV * cs) if tb else (N_DEV * cs, b.shape[1])
    else:
        n, kb = (b.shape[0], b.shape[1]) if tb else (b.shape[1], b.shape[0])
    assert k == kb, (a.shape, b.shape, tb)
    if b3 and tb:
        tm, tn = _tile(m, 544, 2 * SUBLANES), _tile(n, 512, LANES)

        def body_shards(a_ref, b_ref, o_ref):
            r = None
            for j in range(N_DEV):
                part = lax.dot_general(a_ref[:, j * cs:(j + 1) * cs], b_ref[j], (((1,), (1,)), ((), ())),
                                       preferred_element_type=F32)
                r = part if r is None else r + part
            o_ref[...] = r.astype(o_ref.dtype)

        return _call(
            body_shards, name=name, grid=(m // tm, n // tn),
            in_specs=[pl.BlockSpec((tm, k), lambda i, j: (i, 0)), pl.BlockSpec((N_DEV, tn, cs), lambda i, j: (0, j, 0))],
            out_specs=[pl.BlockSpec((tm, tn), lambda i, j: (i, j))], out_shape=[jax.ShapeDtypeStruct((m, n), out_dtype)],
            sem=("parallel", "parallel"), args=(a, b), after=after)[0]
    tm = _tile(m, tm_max, 2 * SUBLANES)
    tk = k
    if b3:
        tn = cs
    elif o_cs is not None:
        tn = o_cs if o_cs % LANES == 0 else 2 * o_cs
    else:
        tn = _tile(n, min(MM_B_BLOCK_BYTES // (2 * tk), MM_O_BLOCK_BYTES // (4 * tm)), LANES)
    cb = 1 if tb else 0
    dn = (((1,), (cb,)), ((), ()))

    def body_one(a_ref, b_ref, o_ref):
        r = lax.dot_general(a_ref[...], b_ref[...], dn, preferred_element_type=F32)
        if o_cs is None:
            o_ref[...] = r.astype(o_ref.dtype)
        else:
            for j in range(tn // o_cs):
                o_ref[j] = r[:, j * o_cs:(j + 1) * o_cs].astype(o_ref.dtype)

    a_spec = pl.BlockSpec((tm, tk), lambda i, j: (i, 0))
    if b3:
        b_spec = pl.BlockSpec((None, tk, cs), lambda i, j: (j, 0, 0))
    else:
        b_spec = pl.BlockSpec((tn, tk), lambda i, j: (j, 0)) if tb else pl.BlockSpec((tk, tn), lambda i, j: (0, j))
    if o_cs is None:
        o_spec = pl.BlockSpec((tm, tn), lambda i, j: (i, j))
        o_shape = jax.ShapeDtypeStruct((m, n), out_dtype)
    else:
        o_spec = pl.BlockSpec((tn // o_cs, tm, o_cs), lambda i, j: (j, i, 0))
        o_shape = jax.ShapeDtypeStruct((n // o_cs, m, o_cs), out_dtype)
    return _call(
        body_one, name=name, grid=(m // tm, n // tn), in_specs=[a_spec, b_spec], out_specs=[o_spec], out_shape=[o_shape],
        sem=("parallel", "parallel"), args=(a, b), after=after)[0]


def _mm_parts(parts, b, name):
    m, n = parts[0].shape[0], b.shape[1]
    ks = [p.shape[1] for p in parts]
    k = sum(ks)
    assert b.shape[0] == k
    tm = _tile(m, 1088, 2 * SUBLANES)
    tn = _tile(n, min(MM_B_BLOCK_BYTES // (2 * k), MM_O_BLOCK_BYTES // (4 * tm)), LANES)

    def body(*refs):
        a_refs, b_ref, o_ref = refs[:-2], refs[-2], refs[-1]
        r, o = None, 0
        for a_ref, ki in zip(a_refs, ks):
            part = jnp.dot(a_ref[...], b_ref[o:o + ki, :], preferred_element_type=F32)
            r = part if r is None else r + part
            o += ki
        o_ref[...] = r

    return pl.pallas_call(
        body, name=name, grid=(m // tm, n // tn),
        in_specs=[pl.BlockSpec((tm, ki), lambda i, j: (i, 0)) for ki in ks] + [pl.BlockSpec((k, tn), lambda i, j: (0, j))],
        out_specs=pl.BlockSpec((tm, tn), lambda i, j: (i, j)), out_shape=jax.ShapeDtypeStruct((m, n), F32),
        compiler_params=_params(("parallel", "parallel")))(*parts, b)


def _transpose_parts(parts, name):
    t = parts[0].shape[0]
    cs = [p.shape[1] for p in parts]
    tt = _tm()

    def body(*refs):
        o_ref, o = refs[-1], 0
        for a_ref, ci in zip(refs[:-1], cs):
            o_ref[o:o + ci, :] = a_ref[...].T
            o += ci

    return pl.pallas_call(body, name=name, grid=(t // tt,),
                          in_specs=[pl.BlockSpec((tt, ci), lambda i: (i, 0)) for ci in cs],
                          out_specs=pl.BlockSpec((sum(cs), tt), lambda i: (0, i)),
                          out_shape=jax.ShapeDtypeStruct((sum(cs), t), BF16),
                          compiler_params=_params(("parallel",)))(*parts)


DW_TM = 512


def _cast_bf16(x, name):
    r, c = x.shape
    tr = _tile(r, 512, 2 * SUBLANES)

    def body(x_ref, o_ref):
        o_ref[...] = x_ref[...].astype(BF16)

    return pl.pallas_call(body, name=name, grid=(r // tr,), in_specs=[pl.BlockSpec((tr, c), lambda i: (i, 0))],
                          out_specs=pl.BlockSpec((tr, c), lambda i: (i, 0)),
                          out_shape=jax.ShapeDtypeStruct((r, c), BF16), compiler_params=_params(("parallel",)))(x)


def _cols_from_shards(wg, name):
    _, k, cs = wg.shape
    tk = _tile(k, 256, 2 * SUBLANES)

    def body(w_ref, o_ref):
        for j in range(N_DEV):
            o_ref[:, j * cs:(j + 1) * cs] = w_ref[j]

    return pl.pallas_call(body, name=name, grid=(k // tk,),
                          in_specs=[pl.BlockSpec((N_DEV, tk, cs), lambda i: (0, i, 0))],
                          out_specs=pl.BlockSpec((tk, N_DEV * cs), lambda i: (i, 0)),
                          out_shape=jax.ShapeDtypeStruct((k, N_DEV * cs), wg.dtype),
                          compiler_params=_params(("parallel",)))(wg)


def _stream(i):
    return jnp.minimum(i, 1)


def _normmod(x, g, sh, sc):
    y = x * lax.rsqrt(jnp.mean(x * x, axis=-1, keepdims=True) + EPS)
    return (y * g) * (1.0 + sc) + sh


def _mod_spec(chunk, d):
    return pl.BlockSpec((None, None, 1, d), lambda i: (_stream(i), chunk, 0, 0))


def _normmod_fwd(x, g, mod4, which, name):
    t, d = x.shape
    tm = _tm()
    ish, isc = (0, 1) if which == 0 else (3, 4)

    def body(x_ref, g_ref, sh_ref, sc_ref, o_ref, ot_ref):
        h = _normmod(x_ref[...], g_ref[...], sh_ref[...], sc_ref[...]).astype(BF16)
        o_ref[...] = h
        ot_ref[...] = h.T

    row = pl.BlockSpec((tm, d), lambda i: (i, 0))
    return pl.pallas_call(body, name=name, grid=(t // tm,),
                          in_specs=[row, pl.BlockSpec((1, d), lambda i: (0, 0)), _mod_spec(ish, d), _mod_spec(isc, d)],
                          out_specs=[row, pl.BlockSpec((d, tm), lambda i: (0, i))],
                          out_shape=[jax.ShapeDtypeStruct((t, d), BF16), jax.ShapeDtypeStruct((d, t), BF16)],
                          compiler_params=_params(("parallel",)))(x, g, mod4, mod4)


def _res_normmod_fwd(x, f, g, mod4, name):
    t, d = x.shape
    tm = _tm()

    def body(x_ref, f_ref, gate_ref, g_ref, sh_ref, sc_ref, x1_ref, o_ref, ot_ref):
        x1 = x_ref[...] + gate_ref[...] * f_ref[...]
        x1_ref[...] = x1
        h = _normmod(x1, g_ref[...], sh_ref[...], sc_ref[...]).astype(BF16)
        o_ref[...] = h
        ot_ref[...] = h.T

    row = pl.BlockSpec((tm, d), lambda i: (i, 0))
    return pl.pallas_call(
        body, name=name, grid=(t // tm,),
        in_specs=[row, row, _mod_spec(2, d), pl.BlockSpec((1, d), lambda i: (0, 0)), _mod_spec(3, d), _mod_spec(4, d)],
        out_specs=[row, row, pl.BlockSpec((d, tm), lambda i: (0, i))],
        out_shape=[jax.ShapeDtypeStruct((t, d), F32), jax.ShapeDtypeStruct((t, d), BF16), jax.ShapeDtypeStruct((d, t), BF16)],
        compiler_params=_params(("parallel",)))(x, f, mod4, g, mod4, mod4)


def _normmod_bwd(x, g, mod4, which, dh, dres, name):
    t, d = x.shape
    tm = _tm()
    ish, isc = (0, 1) if which == 0 else (3, 4)

    def body(x_ref, g_ref, sh_ref, sc_ref, dh_ref, dres_ref, dx_ref, dg_ref, dsh_ref, dsc_ref):
        i = pl.program_id(0)
        _, vjp = jax.vjp(_normmod, x_ref[...], g_ref[...], sh_ref[...], sc_ref[...])
        dx, dg, dsh, dsc = vjp(dh_ref[...])
        dx_ref[...] = dres_ref[...] + dx

        @pl.when(i == 0)
        def _():
            dg_ref[...] = jnp.zeros_like(dg_ref)

        @pl.when(i <= 1)
        def _():
            dsh_ref[...] = jnp.zeros_like(dsh_ref)
            dsc_ref[...] = jnp.zeros_like(dsc_ref)

        dg_ref[...] += dg
        dsh_ref[...] += dsh
        dsc_ref[...] += dsc

    row = pl.BlockSpec((tm, d), lambda i: (i, 0))
    vec = pl.BlockSpec((1, d), lambda i: (0, 0))
    svec = pl.BlockSpec((None, 1, d), lambda i: (_stream(i), 0, 0))
    return _call(
        body, name=name, grid=(t // tm,),
        in_specs=[row, vec, _mod_spec(ish, d), _mod_spec(isc, d), row, row],
        out_specs=[row, vec, svec, svec],
        out_shape=[jax.ShapeDtypeStruct((t, d), F32), jax.ShapeDtypeStruct((1, d), F32),
                   jax.ShapeDtypeStruct((2, 1, d), F32), jax.ShapeDtypeStruct((2, 1, d), F32)],
        sem=("arbitrary",), args=(x, g, mod4, mod4, dh, dres))


def _gate_res_fwd(x, f, mod4, chunk, name):
    t, d = x.shape
    tm = _tm()

    def body(x_ref, f_ref, g_ref, o_ref):
        o_ref[...] = x_ref[...] + g_ref[...] * f_ref[...]

    row = pl.BlockSpec((tm, d), lambda i: (i, 0))
    return pl.pallas_call(body, name=name, grid=(t // tm,), in_specs=[row, row, _mod_spec(chunk, d)], out_specs=row,
                          out_shape=jax.ShapeDtypeStruct((t, d), F32), compiler_params=_params(("parallel",)))(x, f, mod4)


def _gate_res_bwd(dx, f, mod4, chunk, name):
    t, d = dx.shape
    tm = _tm()

    def body(dx_ref, f_ref, g_ref, o_ref, dg_ref):
        i = pl.program_id(0)
        dxv = dx_ref[...]
        o_ref[...] = (dxv * g_ref[...]).astype(BF16)

        @pl.when(i <= 1)
        def _():
            dg_ref[...] = jnp.zeros_like(dg_ref)

        dg_ref[...] += jnp.sum(dxv * f_ref[...], axis=0, keepdims=True)

    row = pl.BlockSpec((tm, d), lambda i: (i, 0))
    return pl.pallas_call(
        body, name=name, grid=(t // tm,), in_specs=[row, row, _mod_spec(chunk, d)],
        out_specs=[row, pl.BlockSpec((None, 1, d), lambda i: (_stream(i), 0, 0))],
        out_shape=[jax.ShapeDtypeStruct((t, d), BF16), jax.ShapeDtypeStruct((2, 1, d), F32)],
        compiler_params=_params(("arbitrary",)))(dx, f, mod4)


def _loss_head(x, final_g, target, name):
    t, d = x.shape
    tm = _tm()

    def loss_fn(xv, g, tgt):
        y = (xv * lax.rsqrt(jnp.mean(xv * xv, axis=-1, keepdims=True) + EPS)) * g
        err = y - tgt
        return 0.5 * jnp.sum(jnp.mean(err * err, axis=-1, keepdims=True))

    def body(x_ref, g_ref, t_ref, l_ref, dx_ref, dg_ref):
        i = pl.program_id(0)

        @pl.when(i == 0)
        def _():
            l_ref[...] = jnp.zeros_like(l_ref)
            dg_ref[...] = jnp.zeros_like(dg_ref)
            dx_ref[...] = jnp.zeros_like(dx_ref)

        @pl.when(i > 0)
        def _():
            l, (dx, dg) = jax.value_and_grad(loss_fn, argnums=(0, 1))(x_ref[...], g_ref[...], t_ref[...])
            l_ref[...] += jnp.full(l_ref.shape, l, F32)
            dx_ref[...] = dx
            dg_ref[...] += dg

    row = pl.BlockSpec((tm, d), lambda i: (i, 0))
    vec = pl.BlockSpec((1, d), lambda i: (0, 0))
    return pl.pallas_call(
        body, name=name, grid=(t // tm,),
        in_specs=[row, vec, pl.BlockSpec((tm, d), lambda i: (jnp.maximum(i - 1, 0), 0))],
        out_specs=[pl.BlockSpec((SUBLANES, LANES), lambda i: (0, 0)), row, vec],
        out_shape=[jax.ShapeDtypeStruct((SUBLANES, LANES), F32), jax.ShapeDtypeStruct((t, d), F32),
                   jax.ShapeDtypeStruct((1, d), F32)],
        compiler_params=_params(("arbitrary",)))(x, final_g, target)


def _swap_quarters(x):
    half, nf = RET_DK // 2, RET_DK // 4
    lane = lax.broadcasted_iota(jnp.int32, x.shape, 1)
    return jnp.where((lane % half) < nf, pltpu.roll(x, RET_DK - nf, 1), pltpu.roll(x, nf, 1))


def _rope(x, cos, sin):
    return x * cos + _swap_quarters(x) * sin


def _rope_t(y, cos, sin):
    return y * cos + _swap_quarters(y * sin)


def _ret_consts(d):
    c = RET_CHUNK
    ii = lax.broadcasted_iota(jnp.int32, (c, 1), 0).astype(F32)
    jj = lax.broadcasted_iota(jnp.int32, (1, c), 1).astype(F32)
    fwd = d == 0
    sgn = jnp.where(fwd, 1.0, -1.0).astype(F32)
    pos = jnp.where(fwd, ii, c - 1.0 - ii)
    return sgn * (ii - jj), pos


def _ret_step(lgt, state, q, k, v, diff, pos):
    c = float(RET_CHUNK)
    lg = -(jnp.maximum(-lgt, 0.0) + jnp.log1p(jnp.exp(-jnp.abs(lgt))))
    lower = diff >= 0
    decay = jnp.where(lower, jnp.exp(jnp.where(lower, diff, 0.0) * lg), 0.0)
    xi = jnp.exp((pos + 1.0) * lg)
    zeta = jnp.exp((c - 1.0 - pos) * lg)
    gch = jnp.exp(c * lg)
    inner = dot_nt(q, k) * decay
    out = dot_nn(inner, v) + dot_nn(q, state) * xi
    new_state = state * gch + dot_tn(k * zeta, v)
    return out, new_state


def _chunk_order():
    nc, nch = CTX_LEN // RET_CHUNK, _t_rows() // RET_CHUNK
    fwd = list(range(nch))
    bwd = list(range(nc - 1, -1, -1)) + list(range(nch - 1, nc - 1, -1))
    return jnp.asarray(np.array([fwd, bwd], np.int32))


def _ret_fwd(p, cos, sin, decay, order, name):
    t = p.shape[0]
    c, dk, dv, nh = RET_CHUNK, RET_DK, RET_DV, RET_HEADS
    nch = t // c
    off = _offsets()
    wqk, wv = nh * dk, nh * dv
    assert off["q"] % wqk == 0 and off["k"] % wqk == 0 and off["v"] % wv == 0
    qb, kb, vb = off["q"] // wqk, off["k"] // wqk, off["v"] // wv
    scale = RET_DK ** -0.5

    def body(ord_ref, dec_ref, *refs):
        ins, o_refs, st_ref, state = refs[:10], refs[10:12], refs[12], refs[13]
        s = pl.program_id(0)

        @pl.when(s == 0)
        def _():
            state[...] = jnp.zeros_like(state)

        for d in range(2):
            q_ref, k_ref, v_ref, cos_ref, sin_ref = ins[5 * d:5 * d + 5]
            diff, pos = _ret_consts(d)
            cosv, sinv = cos_ref[...], sin_ref[...]
            for h in range(nh):
                st = state[d, h]
                st_ref[d, h] = st
                lgt = jnp.full((1, 1), dec_ref[d, h], F32)
                q = _rope(q_ref[:, h * dk:(h + 1) * dk], cosv, sinv) * scale
                k = _rope(k_ref[:, h * dk:(h + 1) * dk], cosv, sinv)
                out, ns = _ret_step(lgt, st, q, k, v_ref[:, h * dv:(h + 1) * dv], diff, pos)
                o_refs[d][:, h * dv:(h + 1) * dv] = out
                state[d, h] = ns

    def dir_specs(d):
        return [pl.BlockSpec((c, wqk), lambda s, o: (o[d, s], qb)), pl.BlockSpec((c, wqk), lambda s, o: (o[d, s], kb)),
                pl.BlockSpec((c, wv), lambda s, o: (o[d, s], vb)), pl.BlockSpec((c, dk), lambda s, o: (o[d, s], 0)),
                pl.BlockSpec((c, dk), lambda s, o: (o[d, s], 0))]

    grid_spec = pltpu.PrefetchScalarGridSpec(
        num_scalar_prefetch=1, grid=(nch,),
        in_specs=[pl.BlockSpec(memory_space=pltpu.SMEM)] + dir_specs(0) + dir_specs(1),
        out_specs=[pl.BlockSpec((c, wv), lambda s, o: (o[0, s], 0)), pl.BlockSpec((c, wv), lambda s, o: (o[1, s], 0)),
                   pl.BlockSpec((2, nh, None, dk, dv), lambda s, o: (0, 0, s, 0, 0))],
        scratch_shapes=[pltpu.VMEM((2, nh, dk, dv), F32)])
    return pl.pallas_call(
        body, name=name, grid_spec=grid_spec,
        out_shape=[jax.ShapeDtypeStruct((t, wv), F32), jax.ShapeDtypeStruct((t, wv), F32),
                   jax.ShapeDtypeStruct((2, nh, nch, dk, dv), F32)],
        compiler_params=_params(("arbitrary",)))(order, decay, p, p, p, cos, sin, p, p, p, cos, sin)


def _ret_bwd(p, cos, sin, decay, order, states, do, name):
    t = p.shape[0]
    c, dk, dv, nh = RET_CHUNK, RET_DK, RET_DV, RET_HEADS
    nch = t // c
    off = _offsets()
    wqk, wv = nh * dk, nh * dv
    qb, kb, vb = off["q"] // wqk, off["k"] // wqk, off["v"] // wv
    scale = RET_DK ** -0.5

    def body(ord_ref, dec_ref, *refs):
        ins, st_ref, outs, dd_ref, dstate = refs[:12], refs[12], refs[13:19], refs[19], refs[20]
        s = pl.program_id(0)

        @pl.when(s == 0)
        def _():
            dstate[...] = jnp.zeros_like(dstate)
            dd_ref[...] = jnp.zeros_like(dd_ref)

        for d in range(2):
            q_ref, k_ref, v_ref, cos_ref, sin_ref, do_ref = ins[6 * d:6 * d + 6]
            dq_ref, dk_ref, dv_ref = outs[3 * d:3 * d + 3]
            diff, pos = _ret_consts(d)
            cosv, sinv = cos_ref[...], sin_ref[...]
            for h in range(nh):
                qk, vv = slice(h * dk, (h + 1) * dk), slice(h * dv, (h + 1) * dv)
                lgt = jnp.full((1, 1), dec_ref[d, h], F32)
                q = _rope(q_ref[:, qk], cosv, sinv) * scale
                k = _rope(k_ref[:, qk], cosv, sinv)
                _, vjp = jax.vjp(lambda a, b, cq, ck, cv: _ret_step(a, b, cq, ck, cv, diff, pos),
                                 lgt, st_ref[d, h], q, k, v_ref[:, vv])
                dlgt, dst, dq, dkk, dvv = vjp((do_ref[:, vv], dstate[d, h]))
                dstate[d, h] = dst
                dq_ref[:, qk] = _rope_t(dq * scale, cosv, sinv)
                dk_ref[:, qk] = _rope_t(dkk, cosv, sinv)
                dv_ref[:, vv] = dvv
                dd_ref[d, h] += jnp.broadcast_to(dlgt, (SUBLANES, LANES))

    def chunk(d):
        return lambda s, o: o[d, nch - 1 - s]

    def dir_specs(d):
        at = chunk(d)
        return [pl.BlockSpec((c, wqk), lambda s, o: (at(s, o), qb)), pl.BlockSpec((c, wqk), lambda s, o: (at(s, o), kb)),
                pl.BlockSpec((c, wv), lambda s, o: (at(s, o), vb)), pl.BlockSpec((c, dk), lambda s, o: (at(s, o), 0)),
                pl.BlockSpec((c, dk), lambda s, o: (at(s, o), 0)), pl.BlockSpec((c, wv), lambda s, o: (at(s, o), 0))]

    def dir_outs(d):
        at = chunk(d)
        return [pl.BlockSpec((c, wqk), lambda s, o: (at(s, o), 0)), pl.BlockSpec((c, wqk), lambda s, o: (at(s, o), 0)),
                pl.BlockSpec((c, wv), lambda s, o: (at(s, o), 0))]

    grid_spec = pltpu.PrefetchScalarGridSpec(
        num_scalar_prefetch=1, grid=(nch,),
        in_specs=[pl.BlockSpec(memory_space=pltpu.SMEM)] + dir_specs(0) + dir_specs(1)
        + [pl.BlockSpec((2, nh, None, dk, dv), lambda s, o: (0, 0, nch - 1 - s, 0, 0))],
        out_specs=dir_outs(0) + dir_outs(1) + [pl.BlockSpec((2, nh, SUBLANES, LANES), lambda s, o: (0, 0, 0, 0))],
        scratch_shapes=[pltpu.VMEM((2, nh, dk, dv), F32)])
    qk_sds, v_sds = jax.ShapeDtypeStruct((t, wqk), F32), jax.ShapeDtypeStruct((t, wv), F32)
    res = pl.pallas_call(
        body, name=name, grid_spec=grid_spec,
        out_shape=[qk_sds, qk_sds, v_sds, qk_sds, qk_sds, v_sds, jax.ShapeDtypeStruct((2, nh, SUBLANES, LANES), F32)],
        compiler_params=_params(("arbitrary",)))(order, decay, p, p, p, cos, sin, do, p, p, p, cos, sin, do, states)
    return res[:3], res[3:6], res[6]


def _ggn_head(of, ob, gate, g):
    o = of + ob
    mu = jnp.mean(o, axis=-1, keepdims=True)
    var = jnp.mean(jnp.square(o - mu), axis=-1, keepdims=True)
    return ((o - mu) * lax.rsqrt(var + EPS) * g) * _silu(gate)


def _ggn_fwd(o_f, o_b, p, gn_g, name):
    t = p.shape[0]
    tm, w, dv = _tm(), _ret_w(), RET_DV
    gb = _offsets()["g"] // w

    def body(of_ref, ob_ref, gate_ref, g_ref, out_ref):
        for h in range(RET_HEADS):
            sl = slice(h * dv, (h + 1) * dv)
            out_ref[:, sl] = _ggn_head(of_ref[:, sl], ob_ref[:, sl], gate_ref[:, sl], g_ref[:, sl]).astype(BF16)

    row = pl.BlockSpec((tm, w), lambda i: (i, 0))
    return pl.pallas_call(
        body, name=name, grid=(t // tm,),
        in_specs=[row, row, pl.BlockSpec((tm, w), lambda i: (i, gb)), pl.BlockSpec((1, w), lambda i: (0, 0))],
        out_specs=row, out_shape=jax.ShapeDtypeStruct((t, w), BF16),
        compiler_params=_params(("parallel",)))(o_f, o_b, p, gn_g)


def _ggn_bwd(o_f, o_b, p, gn_g, dmix, name):
    t = p.shape[0]
    tm, w, dv = _tm(), _ret_w(), RET_DV
    gb = _offsets()["g"] // w

    def body(of_ref, ob_ref, gate_ref, g_ref, dy_ref, do_ref, dgate_ref, dg_ref):
        i = pl.program_id(0)

        @pl.when(i == 0)
        def _():
            dg_ref[...] = jnp.zeros_like(dg_ref)

        for h in range(RET_HEADS):
            sl = slice(h * dv, (h + 1) * dv)
            _, vjp = jax.vjp(_ggn_head, of_ref[:, sl], ob_ref[:, sl], gate_ref[:, sl], g_ref[:, sl])
            do, _, dgate, dg = vjp(dy_ref[:, sl])
            do_ref[:, sl] = do
            dgate_ref[:, sl] = dgate
            dg_ref[:, sl] += dg

    row = pl.BlockSpec((tm, w), lambda i: (i, 0))
    return pl.pallas_call(
        body, name=name, grid=(t // tm,),
        in_specs=[row, row, pl.BlockSpec((tm, w), lambda i: (i, gb)), pl.BlockSpec((1, w), lambda i: (0, 0)), row],
        out_specs=[row, row, pl.BlockSpec((1, w), lambda i: (0, 0))],
        out_shape=[jax.ShapeDtypeStruct((t, w), F32), jax.ShapeDtypeStruct((t, w), F32),
                   jax.ShapeDtypeStruct((1, w), F32)],
        compiler_params=_params(("arbitrary",)))(o_f, o_b, p, gn_g, dmix)


def _halo(k):
    return SUBLANES * ((k // 2 + SUBLANES - 1) // SUBLANES)


def _halo_specs(width, colblock, h, tm):
    r = tm // h
    return [pl.BlockSpec((h, width), lambda i, *_: (jnp.maximum(i * r - 1, 0), colblock(*_))),
            pl.BlockSpec((tm, width), lambda i, *_: (i, colblock(*_))),
            pl.BlockSpec((h, width), lambda i, *_: (jnp.minimum((i + 1) * r, (_t_rows() // h) - 1), colblock(*_)))]


def _fill_ext(ext_ref, prev, cur, nxt, i, h, tm):
    nt = _t_rows() // tm
    ext_ref[0:h, :] = jnp.where(i >= 2, prev, 0.0)
    ext_ref[h:h + tm, :] = cur
    ext_ref[h + tm:h + tm + h, :] = jnp.where((i >= 1) & (i <= nt - 2), nxt, 0.0)


def _corr(ext_ref, w_ref, k, h, tm, flip, cols=slice(None)):
    pad = k // 2
    acc = None
    for kk in range(k):
        o = h + (pad - kk if flip else kk - pad)
        term = w_ref[kk:kk + 1, cols] * ext_ref[o:o + tm, cols]
        acc = term if acc is None else acc + term
    return acc


LANE_CHUNK = 512


def _chunks(tm, tc):
    return [(r0, slice(c0, min(c0 + LANE_CHUNK, tc))) for r0 in range(0, tm, ROW_CHUNK) for c0 in range(0, tc, LANE_CHUNK)]


def _conv_post(u2, ln_g, ln_b, pw):
    mu = jnp.mean(u2, axis=-1, keepdims=True)
    var = jnp.mean(jnp.square(u2 - mu), axis=-1, keepdims=True)
    y = (u2 - mu) * lax.rsqrt(var + EPS) * ln_g + ln_b
    return dot_nn(_silu(y), pw)


def _conv_fwd(p, dw_w, dw_b, ln_g, ln_b, pw, name):
    t = p.shape[0]
    tm, w, k = _tm(), CONV_W, CONV_K
    h = _halo(k)
    off = _offsets()
    ab, bb = off["a"] // w, off["b"] // w

    def body(ap, ac, an, bp, bc, bn, w_ref, b_ref, g_ref, beta_ref, pw_ref, u2_ref, out_ref, ext):
        i = pl.program_id(0)
        glu = lambda a, b: a * _sigmoid(b)
        _fill_ext(ext, glu(ap[...], bp[...]), glu(ac[...], bc[...]), glu(an[...], bn[...]), i, h, tm)
        for r0 in range(0, tm, ROW_CHUNK):
            u2_ref[r0:r0 + ROW_CHUNK, :] = _corr(ext, w_ref, k, h + r0, ROW_CHUNK, False) + b_ref[...]
        out_ref[...] = _conv_post(u2_ref[...], g_ref[...], beta_ref[...], pw_ref[...]).astype(BF16)

    vec = pl.BlockSpec((1, w), lambda i: (0, 0))
    row = pl.BlockSpec((tm, w), lambda i: (i, 0))
    return pl.pallas_call(
        body, name=name, grid=(t // tm,),
        in_specs=_halo_specs(w, lambda: ab, h, tm) + _halo_specs(w, lambda: bb, h, tm)
        + [pl.BlockSpec((k, w), lambda i: (0, 0)), vec, vec, vec, pl.BlockSpec((w, w), lambda i: (0, 0))],
        out_specs=[row, row],
        out_shape=[jax.ShapeDtypeStruct((t, w), F32), jax.ShapeDtypeStruct((t, w), BF16)],
        scratch_shapes=[pltpu.VMEM((tm + 2 * h, w), F32)],
        compiler_params=_params(("parallel",)))(p, p, p, p, p, p, dw_w, dw_b, ln_g, ln_b, pw)


def _conv_bwd1(u2, dmix, ln_g, ln_b, pw, name):
    t = u2.shape[0]
    tm, w = _tm(), CONV_W
    cb = _ret_w() // w

    def body(u2_ref, dy_ref, g_ref, beta_ref, pw_ref, du2_ref, dg_ref, db_ref, dpw_ref):
        i = pl.program_id(0)

        @pl.when(i == 0)
        def _():
            dg_ref[...] = jnp.zeros_like(dg_ref)
            db_ref[...] = jnp.zeros_like(db_ref)
            dpw_ref[...] = jnp.zeros_like(dpw_ref)

        _, vjp = jax.vjp(_conv_post, u2_ref[...], g_ref[...], beta_ref[...], pw_ref[...])
        du2, dg, db, dpw = vjp(dy_ref[...])
        du2_ref[...] = du2
        dg_ref[...] += dg
        db_ref[...] += db
        dpw_ref[...] += dpw

    vec = pl.BlockSpec((1, w), lambda i: (0, 0))
    row = pl.BlockSpec((tm, w), lambda i: (i, 0))
    mat = pl.BlockSpec((w, w), lambda i: (0, 0))
    return pl.pallas_call(
        body, name=name, grid=(t // tm,),
        in_specs=[row, pl.BlockSpec((tm, w), lambda i: (i, cb)), vec, vec, mat],
        out_specs=[row, vec, vec, mat],
        out_shape=[jax.ShapeDtypeStruct((t, w), F32), jax.ShapeDtypeStruct((1, w), F32),
                   jax.ShapeDtypeStruct((1, w), F32), jax.ShapeDtypeStruct((w, w), F32)],
        compiler_params=_params(("arbitrary",)))(u2, dmix, ln_g, ln_b, pw)


def _conv_bwd2(du2, p, dw_w, name):
    t = p.shape[0]
    tm, w, k = _tm(), CONV_W, CONV_K
    h = _halo(k)
    pad = k // 2
    off = _offsets()
    ab, bb = off["a"] // w, off["b"] // w

    def body(dp, dc, dn, ap, ac, an, bp, bc, bn, w_ref, da_ref, db_ref, dw_ref, dbias_ref, ext_d, ext_u):
        i = pl.program_id(0)

        @pl.when(i == 0)
        def _():
            dw_ref[...] = jnp.zeros_like(dw_ref)
            dbias_ref[...] = jnp.zeros_like(dbias_ref)

        glu = lambda a, b: a * _sigmoid(b)
        _fill_ext(ext_d, dp[...], dc[...], dn[...], i, h, tm)
        _fill_ext(ext_u, glu(ap[...], bp[...]), glu(ac[...], bc[...]), glu(an[...], bn[...]), i, h, tm)
        chunks = range(0, tm, ROW_CHUNK)
        acc_b = jnp.zeros((ROW_CHUNK, w), F32)
        for r0 in chunks:
            rows = slice(r0, r0 + ROW_CHUNK)
            du = _corr(ext_d, w_ref, k, h + r0, ROW_CHUNK, True)
            sg = _sigmoid(bc[rows, :])
            da_ref[rows, :] = du * sg
            db_ref[rows, :] = du * ac[rows, :] * sg * (1.0 - sg)
            acc_b = acc_b + ext_d[h + r0:h + r0 + ROW_CHUNK, :]
        dbias_ref[...] += jnp.sum(acc_b, axis=0, keepdims=True)
        for kk in range(k):
            acc = jnp.zeros((ROW_CHUNK, w), F32)
            for r0 in chunks:
                o = h + r0 + kk - pad
                acc = acc + ext_d[h + r0:h + r0 + ROW_CHUNK, :] * ext_u[o:o + ROW_CHUNK, :]
            dw_ref[kk:kk + 1, :] += jnp.sum(acc, axis=0, keepdims=True)

    vec = pl.BlockSpec((1, w), lambda i: (0, 0))
    row = pl.BlockSpec((tm, w), lambda i: (i, 0))
    kw = pl.BlockSpec((k, w), lambda i: (0, 0))
    return _call(
        body, name=name, grid=(t // tm,),
        in_specs=_halo_specs(w, lambda: 0, h, tm) + _halo_specs(w, lambda: ab, h, tm)
        + _halo_specs(w, lambda: bb, h, tm) + [kw],
        out_specs=[row, row, kw, vec],
        out_shape=[jax.ShapeDtypeStruct((t, w), F32), jax.ShapeDtypeStruct((t, w), F32),
                   jax.ShapeDtypeStruct((k, w), F32), jax.ShapeDtypeStruct((1, w), F32)],
        scratch=[pltpu.VMEM((tm + 2 * h, w), F32), pltpu.VMEM((tm + 2 * h, w), F32)],
        sem=("arbitrary",), args=(du2, du2, du2, p, p, p, p, p, p, dw_w))


def _ffn_tc():
    return _tile(D_FF, 2816, LANES)


def _ffn_act_fwd(u, dw_w, dw_b, name):
    t = u.shape[0]
    tm, k, tc = _tm(), FFN_K, _ffn_tc()
    h = _halo(k)
    nj = D_FF // tc

    def body(vp, vc, vn, gp, gc, gn, wv, wg, bv, bg, out_ref, out_t_ref, ext_v, ext_g):
        i = pl.program_id(0)
        _fill_ext(ext_v, vp[...], vc[...], vn[...], i, h, tm)
        _fill_ext(ext_g, gp[...], gc[...], gn[...], i, h, tm)
        for r0, cols in _chunks(tm, tc):
            val = _corr(ext_v, wv, k, h + r0, ROW_CHUNK, False, cols) + bv[:, cols]
            gate = _corr(ext_g, wg, k, h + r0, ROW_CHUNK, False, cols) + bg[:, cols]
            out_ref[r0:r0 + ROW_CHUNK, cols] = (_silu(gate) * val).astype(BF16)
        out_t_ref[...] = out_ref[...].T

    wspec = lambda s: pl.BlockSpec((k, tc), lambda i, j: (0, j + s))
    bspec = lambda s: pl.BlockSpec((1, tc), lambda i, j: (0, j + s))
    return pl.pallas_call(
        body, name=name, grid=(t // tm, nj),
        in_specs=_halo_specs(tc, lambda j: j, h, tm) + _halo_specs(tc, lambda j: j + nj, h, tm)
        + [wspec(0), wspec(nj), bspec(0), bspec(nj)],
        out_specs=[pl.BlockSpec((tm, tc), lambda i, j: (i, j)), pl.BlockSpec((tc, tm), lambda i, j: (j, i))],
        out_shape=[jax.ShapeDtypeStruct((t, D_FF), BF16), jax.ShapeDtypeStruct((D_FF, t), BF16)],
        scratch_shapes=[pltpu.VMEM((tm + 2 * h, tc), F32), pltpu.VMEM((tm + 2 * h, tc), F32)],
        compiler_params=_params(("parallel", "parallel")))(u, u, u, u, u, u, dw_w, dw_w, dw_b, dw_b)


def _ffn_act_bwd1(u, da, dw_w, dw_b, name):
    t = u.shape[0]
    tm, k, tc = _tm(), FFN_K, _ffn_tc()
    h = _halo(k)
    nj = D_FF // tc

    def body(vp, vc, vn, gp, gc, gn, wv, wg, bv, bg, da_ref, dv_ref, dg_ref, ext_v, ext_g):
        i = pl.program_id(0)
        _fill_ext(ext_v, vp[...], vc[...], vn[...], i, h, tm)
        _fill_ext(ext_g, gp[...], gc[...], gn[...], i, h, tm)
        for r0, cols in _chunks(tm, tc):
            rows = slice(r0, r0 + ROW_CHUNK)
            val = _corr(ext_v, wv, k, h + r0, ROW_CHUNK, False, cols) + bv[:, cols]
            gate = _corr(ext_g, wg, k, h + r0, ROW_CHUNK, False, cols) + bg[:, cols]
            _, vjp = jax.vjp(lambda a, b: _silu(b) * a, val, gate)
            dval, dgate = vjp(da_ref[rows, cols])
            dv_ref[rows, cols] = dval
            dg_ref[rows, cols] = dgate

    wspec = lambda s: pl.BlockSpec((k, tc), lambda i, j: (0, j + s))
    bspec = lambda s: pl.BlockSpec((1, tc), lambda i, j: (0, j + s))
    dc = pl.pallas_call(
        body, name=name, grid=(t // tm, nj),
        in_specs=_halo_specs(tc, lambda j: j, h, tm) + _halo_specs(tc, lambda j: j + nj, h, tm)
        + [wspec(0), wspec(nj), bspec(0), bspec(nj), pl.BlockSpec((tm, tc), lambda i, j: (i, j))],
        out_specs=[pl.BlockSpec((tm, tc), lambda i, j: (i, j)), pl.BlockSpec((tm, tc), lambda i, j: (i, j))],
        out_shape=[jax.ShapeDtypeStruct((t, D_FF), F32), jax.ShapeDtypeStruct((t, D_FF), F32)],
        scratch_shapes=[pltpu.VMEM((tm + 2 * h, tc), F32), pltpu.VMEM((tm + 2 * h, tc), F32)],
        compiler_params=_params(("parallel", "parallel")))(u, u, u, u, u, u, dw_w, dw_w, dw_b, dw_b, da)
    return dc


def _dwconv_bwd(dcv, dcg, u, dw_w, name):
    t = u.shape[0]
    tm, k, tc = _tm(), FFN_K, _ffn_tc()
    h = _halo(k)
    pad = k // 2
    nj = D_FF // tc

    def body(vp, vc, vn, gp, gc, gn, up, uc, un, w_ref, du_ref, dw_ref, dbias_ref, ext_d, ext_u):
        jj, i = pl.program_id(0), pl.program_id(1)

        @pl.when(i == 0)
        def _():
            dw_ref[...] = jnp.zeros_like(dw_ref)
            dbias_ref[...] = jnp.zeros_like(dbias_ref)

        @pl.when(jj < nj)
        def _():
            _fill_ext(ext_d, vp[...], vc[...], vn[...], i, h, tm)

        @pl.when(jj >= nj)
        def _():
            _fill_ext(ext_d, gp[...], gc[...], gn[...], i, h, tm)

        _fill_ext(ext_u, up[...], uc[...], un[...], i, h, tm)
        for c0 in range(0, tc, LANE_CHUNK):
            cols = slice(c0, min(c0 + LANE_CHUNK, tc))
            width = cols.stop - cols.start
            acc_b = jnp.zeros((ROW_CHUNK, width), F32)
            acc_w = [jnp.zeros((ROW_CHUNK, width), F32) for _ in range(k)]
            for r0 in range(0, tm, ROW_CHUNK):
                d = ext_d[h + r0:h + r0 + ROW_CHUNK, cols]
                du_ref[r0:r0 + ROW_CHUNK, cols] = _corr(ext_d, w_ref, k, h + r0, ROW_CHUNK, True, cols).astype(BF16)
                acc_b = acc_b + d
                for kk in range(k):
                    o = h + r0 + kk - pad
                    acc_w[kk] = acc_w[kk] + d * ext_u[o:o + ROW_CHUNK, cols]
            dbias_ref[:, cols] += jnp.sum(acc_b, axis=0, keepdims=True)
            for kk in range(k):
                dw_ref[kk:kk + 1, cols] += jnp.sum(acc_w[kk], axis=0, keepdims=True)

    def hs(cb, live):
        r = tm // h
        row = lambda j, i: jnp.where(live(j), i, 0)
        return [pl.BlockSpec((h, tc), lambda j, i: (jnp.maximum(row(j, i) * r - 1, 0), cb(j))),
                pl.BlockSpec((tm, tc), lambda j, i: (row(j, i), cb(j))),
                pl.BlockSpec((h, tc), lambda j, i: (jnp.minimum((row(j, i) + 1) * r, (_t_rows() // h) - 1), cb(j)))]

    return pl.pallas_call(
        body, name=name, grid=(2 * nj, t // tm),
        in_specs=hs(lambda j: jnp.minimum(j, nj - 1), lambda j: j < nj)
        + hs(lambda j: jnp.maximum(j - nj, 0), lambda j: j >= nj)
        + hs(lambda j: j, lambda j: True) + [pl.BlockSpec((k, tc), lambda j, i: (0, j))],
        out_specs=[pl.BlockSpec((tm, tc), lambda j, i: (i, j)), pl.BlockSpec((k, tc), lambda j, i: (0, j)),
                   pl.BlockSpec((1, tc), lambda j, i: (0, j))],
        out_shape=[jax.ShapeDtypeStruct((t, 2 * D_FF), BF16), jax.ShapeDtypeStruct((k, 2 * D_FF), F32),
                   jax.ShapeDtypeStruct((1, 2 * D_FF), F32)],
        scratch_shapes=[pltpu.VMEM((tm + 2 * h, tc), F32), pltpu.VMEM((tm + 2 * h, tc), F32)],
        compiler_params=_params(("parallel", "arbitrary")))(dcv, dcv, dcv, dcg, dcg, dcg, u, u, u, dw_w)


def _na_geometry(rq):
    ncb = CTX_LEN // GRID_W
    rows_n = SEQ // GRID_W
    r = jnp.maximum(rq - ncb, 0)
    kstart = jnp.clip(r - NA_ROWS // 2, 0, rows_n - NA_ROWS)
    base = kstart - r + NA_ROWS - 1
    return rq >= ncb, kstart, base


def _na_core(q, kl, vl, kc, vc, bias, mask):
    qs = q * (NA_DH ** -0.5)
    s_l = jnp.where(mask, dot_nt(qs, kl) + bias, NEG)
    s_c = dot_nt(qs, kc)
    m = lax.stop_gradient(jnp.maximum(jnp.max(s_l, axis=1, keepdims=True), jnp.max(s_c, axis=1, keepdims=True)))
    e_l, e_c = jnp.exp(s_l - m), jnp.exp(s_c - m)
    inv = 1.0 / (jnp.sum(e_l, axis=1, keepdims=True) + jnp.sum(e_c, axis=1, keepdims=True))
    return dot_nn(e_l * inv, vl) + dot_nn(e_c * inv, vc)


def _na_mask(is_lat):
    nl = NA_ROWS * GRID_W
    q = lax.broadcasted_iota(jnp.int32, (GRID_W, nl), 0)
    w = lax.broadcasted_iota(jnp.int32, (GRID_W, nl), 1) % GRID_W
    cs = jnp.clip(q - NA_COLS // 2, 0, GRID_W - NA_COLS)
    return (w >= cs) & (w < cs + NA_COLS) & is_lat


def _na_bias(rb_ref):
    assert 2 * GRID_W == LANES
    lane = lax.broadcasted_iota(jnp.int32, (GRID_W, LANES), 1)
    tiles = []
    for kp in range(NA_ROWS // 2):
        ev = jnp.broadcast_to(rb_ref[2 * kp:2 * kp + 1, :], (GRID_W, LANES))
        od = jnp.broadcast_to(rb_ref[2 * kp + 1:2 * kp + 2, :], (GRID_W, LANES))
        ev = pltpu.roll(ev, LANES - (NA_COLS - 1), 1, stride=1, stride_axis=0)
        od = pltpu.roll(od, LANES - (NA_COLS - 1) - GRID_W, 1, stride=1, stride_axis=0)
        tiles.append(jnp.where(lane < GRID_W, ev, od))
    return jnp.concatenate(tiles, axis=1)


def _na_dbias(dbias, drb_ref):
    qi = lax.broadcasted_iota(jnp.int32, (GRID_W, GRID_W), 0)
    qj = lax.broadcasted_iota(jnp.int32, (GRID_W, GRID_W), 1)
    flip = (qi + qj == GRID_W - 1).astype(F32)
    rev = lax.dot_general(flip, dbias, (((1,), (0,)), ((), ())), precision=lax.Precision.HIGHEST,
                          preferred_element_type=F32)
    lane = lax.broadcasted_iota(jnp.int32, (GRID_W, LANES), 1)
    s_ev = LANES - (GRID_W - NA_COLS)
    for kp in range(NA_ROWS // 2):
        tile = rev[:, kp * LANES:(kp + 1) * LANES]
        ev = pltpu.roll(jnp.where(lane < GRID_W, tile, 0.0), s_ev, 1, stride=1, stride_axis=0)
        od = pltpu.roll(jnp.where(lane >= GRID_W, tile, 0.0), s_ev - GRID_W, 1, stride=1, stride_axis=0)
        drb_ref[2 * kp:2 * kp + 1, :] += jnp.sum(ev, axis=0, keepdims=True)
        drb_ref[2 * kp + 1:2 * kp + 2, :] += jnp.sum(od, axis=0, keepdims=True)


def _na_hps():
    return 2 if NA_HEADS % 2 == 0 else 1


def _na_specs(p_offsets):
    t = _t_rows()
    hps = _na_hps()
    wd = hps * NA_DH
    assert all(p_offsets[n] % wd == 0 for n in ("nq", "nk", "nv"))
    qb, kb, vb = (p_offsets[n] // wd for n in ("nq", "nk", "nv"))
    return [pl.BlockSpec((GRID_W, wd), lambda h, r: (r, qb + h)),
            pl.BlockSpec((t, wd), lambda h, r: (0, kb + h)),
            pl.BlockSpec((t, wd), lambda h, r: (0, vb + h)),
            pl.BlockSpec((hps, None, NA_ROWS, LANES), lambda h, r: (h, _na_geometry(r)[2], 0, 0))]


def _na_fwd(p, rb, name):
    t = p.shape[0]
    dh, nl = NA_DH, NA_ROWS * GRID_W

    hps = _na_hps()

    def body(q_ref, k_ref, v_ref, rb_ref, out_ref):
        rq = pl.program_id(1)
        is_lat, kstart, _ = _na_geometry(rq)
        start = pl.multiple_of(CTX_LEN + kstart * GRID_W, GRID_W)
        mask = _na_mask(is_lat)
        for hh in range(hps):
            cols = slice(hh * dh, (hh + 1) * dh)
            out = _na_core(q_ref[:, cols], k_ref[pl.ds(start, nl), cols], v_ref[pl.ds(start, nl), cols],
                           k_ref[0:CTX_LEN, cols], v_ref[0:CTX_LEN, cols], _na_bias(rb_ref.at[hh]), mask)
            out_ref[:, cols] = out.astype(BF16)

    return _call(
        body, name=name, grid=(NA_HEADS // hps, t // GRID_W), in_specs=_na_specs(_offsets()),
        out_specs=[pl.BlockSpec((GRID_W, hps * dh), lambda h, r: (r, h))],
        out_shape=[jax.ShapeDtypeStruct((t, _na_w()), BF16)],
        sem=("parallel", "arbitrary"), args=(p, p, p, rb))[0]


def _na_bwd(p, rb, dmix, name):
    t = p.shape[0]
    dh, nl = NA_DH, NA_ROWS * GRID_W

    hps = _na_hps()
    wd = hps * dh
    assert ((_ret_w() + CONV_W) // dh) % hps == 0
    ob = (_ret_w() + CONV_W) // wd

    def body(q_ref, k_ref, v_ref, rb_ref, dy_ref, dq_ref, dk_ref, dv_ref, drb_ref):
        rq = pl.program_id(1)
        is_lat, kstart, base = _na_geometry(rq)
        _, _, prev_base = _na_geometry(rq - 1)
        start = pl.multiple_of(CTX_LEN + kstart * GRID_W, GRID_W)

        @pl.when(rq == 0)
        def _():
            dk_ref[...] = jnp.zeros_like(dk_ref)
            dv_ref[...] = jnp.zeros_like(dv_ref)

        @pl.when((rq == 0) | (base != prev_base))
        def _():
            drb_ref[...] = jnp.zeros_like(drb_ref)

        mask = _na_mask(is_lat)
        for hh in range(hps):
            cols = slice(hh * dh, (hh + 1) * dh)
            _, vjp = jax.vjp(lambda *a: _na_core(*a, mask), q_ref[:, cols], k_ref[pl.ds(start, nl), cols],
                             v_ref[pl.ds(start, nl), cols], k_ref[0:CTX_LEN, cols], v_ref[0:CTX_LEN, cols],
                             _na_bias(rb_ref.at[hh]))
            dq, dkl, dvl, dkc, dvc, dbias = vjp(dy_ref[:, cols])
            dq_ref[:, cols] = dq
            dk_ref[pl.ds(start, nl), cols] += dkl
            dv_ref[pl.ds(start, nl), cols] += dvl
            dk_ref[0:CTX_LEN, cols] += dkc
            dv_ref[0:CTX_LEN, cols] += dvc
            _na_dbias(dbias, drb_ref.at[hh])

    return _call(
        body, name=name, grid=(NA_HEADS // hps, t // GRID_W),
        in_specs=_na_specs(_offsets()) + [pl.BlockSpec((GRID_W, wd), lambda h, r: (r, ob + h))],
        out_specs=[pl.BlockSpec((GRID_W, wd), lambda h, r: (r, h)), pl.BlockSpec((t, wd), lambda h, r: (0, h)),
                   pl.BlockSpec((t, wd), lambda h, r: (0, h)),
                   pl.BlockSpec((hps, None, NA_ROWS, LANES), lambda h, r: (h, _na_geometry(r)[2], 0, 0))],
        out_shape=[jax.ShapeDtypeStruct((t, _na_w()), F32), jax.ShapeDtypeStruct((t, _na_w()), F32),
                   jax.ShapeDtypeStruct((t, _na_w()), F32),
                   jax.ShapeDtypeStruct((NA_HEADS, NA_ROWS, NA_ROWS, LANES), F32)],
        sem=("parallel", "arbitrary"), args=(p, p, p, rb, dmix))


def _rpb_select():
    sel = np.zeros((2 * NA_ROWS - 1, NA_ROWS * NA_ROWS), np.float32)
    for b in range(NA_ROWS):
        for kh in range(NA_ROWS):
            sel[b + kh, b * NA_ROWS + kh] = 1.0
    return jnp.asarray(sel)


def _rpb_rows(rpb):
    pad = jnp.pad(rpb, ((0, 0), (0, 0), (0, LANES - (2 * NA_COLS - 1))))
    rows = jnp.einsum("rk,hrc->hkc", _rpb_select(), pad, precision=lax.Precision.HIGHEST)
    return rows.reshape(NA_HEADS, NA_ROWS, NA_ROWS, LANES)


def _rpb_rows_t(drb):
    flat = drb.reshape(NA_HEADS, NA_ROWS * NA_ROWS, LANES)
    out = jnp.einsum("rk,hkc->hrc", _rpb_select(), flat, precision=lax.Precision.HIGHEST)
    return out[:, :, :2 * NA_COLS - 1]


def _assemble_dp(d_fwd, d_bwd, dgate, da, db, dnq, dnk, dnv, name):
    t = dgate.shape[0]
    tm = _tm()
    off = _offsets()
    sizes = dict(q=_ret_qk_w(), k=_ret_qk_w(), v=_ret_w(), g=_ret_w(), a=CONV_W, b=CONV_W, nq=_na_w(), nk=_na_w(), nv=_na_w())

    def body(qf_ref, kf_ref, vf_ref, qb_ref, kb_ref, vb_ref, g_ref, a_ref, b_ref, nq_ref, nk_ref, nv_ref, o_ref):
        def put(n, val):
            o_ref[:, off[n]:off[n] + sizes[n]] = val.astype(BF16)

        put("q", qf_ref[...] + qb_ref[...])
        put("k", kf_ref[...] + kb_ref[...])
        put("v", vf_ref[...] + vb_ref[...])
        put("g", g_ref[...])
        put("a", a_ref[...])
        put("b", b_ref[...])
        put("nq", nq_ref[...])
        put("nk", nk_ref[...])
        put("nv", nv_ref[...])

    one = lambda w: pl.BlockSpec((tm, w), lambda i: (i, 0))
    qkv = [one(sizes["q"]), one(sizes["k"]), one(sizes["v"])]
    return pl.pallas_call(
        body, name=name, grid=(t // tm,),
        in_specs=qkv + qkv + [one(sizes["g"]), one(CONV_W), one(CONV_W), one(_na_w()), one(_na_w()), one(_na_w())],
        out_specs=one(_d_in()), out_shape=jax.ShapeDtypeStruct((t, _d_in()), BF16),
        compiler_params=_params(("parallel",)))(*d_fwd, *d_bwd, dgate, da, db, dnq, dnk, dnv)


def _adamw(w, m, v, gs, name):
    nl, r, c = w.shape
    stacked = not isinstance(gs, (list, tuple))
    if stacked:
        gs = [gs]
    assert stacked or len(gs) == nl
    g_n = gs[0].shape[-3]
    block_bytes = 2 * 1024 * 1024
    rows = min(block_bytes // (4 * c), block_bytes // (g_n * c * gs[0].dtype.itemsize))
    tr = _tile(r, max(2 * SUBLANES, rows // (2 * SUBLANES) * (2 * SUBLANES)), 2 * SUBLANES)
    nt = r // tr
    c1 = 1.0 - ADAM_B1 ** ADAM_STEP
    c2 = 1.0 - ADAM_B2 ** ADAM_STEP

    def body(w_ref, m_ref, v_ref, *rest):
        g_refs, (go_ref, d_ref, mo_ref, vo_ref) = rest[:len(gs)], rest[len(gs):]
        layer = pl.program_id(0)
        for ll in range(len(gs)):
            @pl.when(jnp.logical_or(stacked, layer == ll))
            def _():
                g_ref = g_refs[ll]
                g = g_ref[0].astype(F32)
                for j in range(1, g_n):
                    g = g + g_ref[j].astype(F32)
                mn = ADAM_B1 * m_ref[...] + (1.0 - ADAM_B1) * g
                vn = ADAM_B2 * v_ref[...] + (1.0 - ADAM_B2) * (g * g)
                m_hat = mn / c1
                v_hat = vn / c2
                go_ref[...] = g
                d_ref[...] = -ADAM_LR * (m_hat / (jnp.sqrt(v_hat) + ADAM_EPS) + ADAM_WD * w_ref[...])
                mo_ref[...] = mn
                vo_ref[...] = vn

    def g_spec(ll):
        if stacked:
            return pl.BlockSpec((None, g_n, tr, c), lambda l, i: (l, 0, i, 0))
        return pl.BlockSpec((g_n, tr, c), lambda l, i: (0, jnp.where(l == ll, i, jnp.where(l < ll, 0, nt - 1)), 0))

    blk = pl.BlockSpec((None, tr, c), lambda l, i: (l, i, 0))
    sds = jax.ShapeDtypeStruct((nl, r, c), F32)
    return _call(
        body, name=name, grid=(nl, nt),
        in_specs=[blk, blk, blk] + [g_spec(ll) for ll in range(len(gs))],
        out_specs=[blk, blk, blk, blk], out_shape=[sds, sds, sds, sds],
        sem=("arbitrary", "arbitrary"), args=(w, m, v, *gs))


def _sum_devices(g, name):
    _, r, c = g.shape
    tr = _tile(r, 512, SUBLANES)

    def body(g_ref, o_ref):
        acc = g_ref[0]
        for j in range(1, N_DEV):
            acc = acc + g_ref[j]
        o_ref[...] = acc

    return pl.pallas_call(body, name=name, grid=(r // tr,), in_specs=[pl.BlockSpec((N_DEV, tr, c), lambda i: (0, i, 0))],
                          out_specs=pl.BlockSpec((tr, c), lambda i: (i, 0)), out_shape=jax.ShapeDtypeStruct((r, c), F32),
                          compiler_params=_params(("parallel",)))(g)


def _ada_fwd(c16, w_ada, b_shard, name):
    nl, d, cs = w_ada.shape
    tk = _tile(d, 512, LANES)
    nk = d // tk

    def body(c_ref, w_ref, b_ref, o_ref):
        kk = pl.program_id(1)

        @pl.when(kk == 0)
        def _():
            o_ref[...] = jnp.broadcast_to(b_ref[...], o_ref.shape)

        o_ref[...] += _dg(_silu(c_ref[...]), w_ref[...], 1, 0)

    return pl.pallas_call(
        body, name=name, grid=(nl, nk),
        in_specs=[pl.BlockSpec((16, tk), lambda l, kk: (0, kk)), pl.BlockSpec((None, tk, cs), lambda l, kk: (l, kk, 0)),
                  pl.BlockSpec((None, 1, cs), lambda l, kk: (l, 0, 0))],
        out_specs=pl.BlockSpec((None, 16, cs), lambda l, kk: (l, 0, 0)),
        out_shape=jax.ShapeDtypeStruct((nl, 16, cs), F32),
        compiler_params=_params(("parallel", "arbitrary")))(c16, w_ada, b_shard)


def _ada_bwd(c16, dm16, w_ada, name):
    nl, d, cs = w_ada.shape
    td = _tile(d, 512, LANES)

    def body(c_ref, dm_ref, w_ref, gw_ref, dc_ref):
        cv = c_ref[...]
        s, vjp = jax.vjp(_silu, cv)
        gw_ref[...] = _dg(s, dm_ref[...], 0, 0)
        ds = _dg(dm_ref[...], w_ref[...], 1, 1)
        dc_ref[...] = vjp(ds)[0]

    return pl.pallas_call(
        body, name=name, grid=(nl, d // td),
        in_specs=[pl.BlockSpec((16, td), lambda l, i: (0, i)), pl.BlockSpec((None, 16, cs), lambda l, i: (l, 0, 0)),
                  pl.BlockSpec((None, td, cs), lambda l, i: (l, i, 0))],
        out_specs=[pl.BlockSpec((None, td, cs), lambda l, i: (l, i, 0)), pl.BlockSpec((None, 16, td), lambda l, i: (l, 0, i))],
        out_shape=[jax.ShapeDtypeStruct((nl, d, cs), F32), jax.ShapeDtypeStruct((nl, 16, d), F32)],
        compiler_params=_params(("parallel", "parallel")))(c16, dm16, w_ada)


def _pack_rows(shape):
    n = int(np.prod(shape))
    return SUBLANES * (-(-n // (LANES * SUBLANES)))


def _pack(arrays, row_align):
    parts, total = [], 0
    for a in arrays:
        flat = a.reshape(-1).astype(F32)
        rows = _pack_rows(a.shape)
        total += rows
        parts += [flat, jnp.zeros((rows * LANES - flat.shape[0],), F32)]
    parts.append(jnp.zeros(((-total % row_align) * LANES,), F32))
    return jnp.concatenate([p for p in parts if p.shape[0]]).reshape(-1, LANES)


def _unpack(packed, shapes):
    out, r = [], 0
    for s in shapes:
        rows = _pack_rows(s)
        out.append(packed[r:r + rows].reshape(-1)[:int(np.prod(s))].reshape(s))
        r += rows
    return out


def _rope_tables():
    half, nf = RET_DK // 2, RET_DK // 4
    pos = jnp.arange(SEQ)
    row = (pos // GRID_W).astype(F32)
    col = (pos % GRID_W).astype(F32)
    inv = ROPE_BASE ** (-jnp.arange(nf, dtype=F32) / nf)
    ar, ac = row[:, None] * inv[None, :], col[:, None] * inv[None, :]
    cos = jnp.concatenate([jnp.cos(ar), jnp.cos(ar), jnp.cos(ac), jnp.cos(ac)], axis=-1)
    sin = jnp.concatenate([-jnp.sin(ar), jnp.sin(ar), -jnp.sin(ac), jnp.sin(ac)], axis=-1)
    cos = jnp.concatenate([jnp.ones((CTX_LEN, RET_DK), F32), cos], axis=0)
    sin = jnp.concatenate([jnp.zeros((CTX_LEN, RET_DK), F32), sin], axis=0)
    return cos, sin


def _layer_fwd(l, x, mod4, w, cst, arrived):
    n = lambda s: f"l{l}_{s}"
    d = D_MODEL
    h1, h1_t = _normmod_fwd(x, w["norm1_g"], mod4, 0, n("norm1"))
    w["w_in"] = _cols_from_shards(arrived("w_in", h1), n("w_in_cols"))
    p = _mm(h1, w["w_in"], n("proj_in"))
    o_f, o_b, states = _ret_fwd(p, cst["cos"], cst["sin"], w["ret_decay"], cst["order"], n("ret_fwd"))
    ret_out = _ggn_fwd(o_f, o_b, p, w["ret_gn_g"], n("ret_gn"))
    u2, conv_out = _conv_fwd(p, w["conv_dw_w"], w["conv_dw_b"], w["conv_ln_g"], w["conv_ln_b"], w["conv_pw"], n("conv_fwd"))
    na_out = _na_fwd(p, w["rb"], n("na_fwd"))
    mix = [ret_out, conv_out, na_out]
    w["w_out"] = arrived("w_out", na_out).reshape(_d_mix(), d)
    g1 = _mm_parts(mix, w["w_out"], n("proj_out"))
    x1, h2, h2_t = _res_normmod_fwd(x, g1, w["norm2_g"], mod4, n("res1_norm2"))
    w["ffn_up"] = arrived("ffn_up", h2)
    u = _mm(h2, w["ffn_up"], n("ffn_up"), b3=True)
    a, a_t = _ffn_act_fwd(u, w["ffn_dw_w"], w["ffn_dw_b"], n("ffn_act"))
    w["ffn_down"] = arrived("ffn_down", a).reshape(D_FF, d)
    f = _mm(a, w["ffn_down"], n("ffn_down"))
    x2 = _gate_res_fwd(x1, f, mod4, 5, n("res2"))
    saved = dict(x=x, h1_t=h1_t, p=p, o_f=o_f, o_b=o_b, states=states, u2=u2, mix=mix, g1=g1, x1=x1, h2_t=h2_t, u=u, a_t=a_t, f=f)
    return x2, saved


def _layer_bwd(l, dx2, s, mod4, w, cst, send):
    n = lambda t: f"l{l}_{t}"
    d = D_MODEL
    dfg, dg2 = _gate_res_bwd(dx2, s["f"], mod4, 5, n("res2_bwd"))
    da = _mm(dfg, w["ffn_down"], n("ffn_down_dx"), tb=True)
    d_ffn_down = _mm(s["a_t"], dfg, n("ffn_down_dw"), out_dtype=BF16, tm_max=DW_TM)
    tok = send(("ffn_down", l), d_ffn_down.reshape(N_DEV, D_FF // N_DEV, d))
    dcv, dcg = _ffn_act_bwd1(s["u"], da, w["ffn_dw_w"], _after(w["ffn_dw_b"], tok), n("ffn_act_bwd"))
    du, d_ffn_dw_w, d_ffn_dw_b = _dwconv_bwd(dcv, dcg, s["u"], w["ffn_dw_w"], n("ffn_dw_bwd"))
    d_ffn_dw_b = d_ffn_dw_b[0]
    dh2 = _mm(du, w["ffn_up"], n("ffn_up_dx"), tb=True, b3=True)
    d_ffn_up = _mm(s["h2_t"], du, n("ffn_up_dw"), out_dtype=BF16, tm_max=DW_TM,
                      o_cs=2 * D_FF // N_DEV)
    tok = send(("ffn_up", l), d_ffn_up)
    (dx1, dn2, dsh2, dsc2) = _normmod_bwd(s["x1"], _after(w["norm2_g"], tok), mod4, 1, dh2, dx2, n("norm2_bwd"))
    dgg, dg1 = _gate_res_bwd(dx1, s["g1"], mod4, 2, n("res1_bwd"))
    dmix = _mm(dgg, w["w_out"], n("proj_out_dx"), tb=True)
    d_w_out = _mm(_transpose_parts(s["mix"], n("mix_t")), dgg, n("proj_out_dw"), out_dtype=BF16, tm_max=DW_TM)
    tok = send(("w_out", l), d_w_out.reshape(N_DEV, _d_mix() // N_DEV, d))
    do, dgate, dgn = _ggn_bwd(s["o_f"], s["o_b"], s["p"], _after(w["ret_gn_g"], tok), dmix, n("ret_gn_bwd"))
    d_fwd, d_bwd, ddec = _ret_bwd(s["p"], cst["cos"], cst["sin"], w["ret_decay"], cst["order"], s["states"], do, n("ret_bwd"))
    du2, dlng, dlnb, dpw = _conv_bwd1(s["u2"], dmix, w["conv_ln_g"], w["conv_ln_b"], w["conv_pw"], n("conv_bwd1"))
    dca, dcb, ddww, ddwb = _conv_bwd2(du2, s["p"], w["conv_dw_w"], n("conv_bwd2"))
    dnq, dnk, dnv, drb = _na_bwd(s["p"], w["rb"], dmix, n("na_bwd"))
    dp = _assemble_dp(d_fwd, d_bwd, dgate, dca, dcb, dnq, dnk, dnv, n("dproj"))
    h1_t = s["h1_t"]
    half = d // 2
    for i in range(2):
        d_w_in = _mm(h1_t[i * half:(i + 1) * half], dp, n(f"proj_in_dw{i}"), out_dtype=BF16, tm_max=DW_TM,
                        o_cs=_d_in() // N_DEV)
        tok = send(("w_in", l, i), d_w_in)
    dh1 = _mm(dp, w["w_in"], n("proj_in_dx"), tb=True, after=tok)
    (dx, dn1, dsh1, dsc1) = _normmod_bwd(s["x"], _after(w["norm1_g"], tok), mod4, 0, dh1, dx1, n("norm1_bwd"))
    dmod = jnp.concatenate([dsh1, dsc1, dg1, dsh2, dsc2, dg2], axis=1)
    small = dict(norm1_g=dn1[0], ret_decay=ddec[:, :, 0, 0], ret_gn_g=dgn[0], conv_dw_w=ddww, conv_dw_b=ddwb[0],
                 conv_ln_g=dlng[0], conv_ln_b=dlnb[0], conv_pw=dpw, na_rpb=_rpb_rows_t(drb), norm2_g=dn2[0],
                 ffn_dw_w=d_ffn_dw_w, ffn_dw_b=d_ffn_dw_b)
    return dx, dmod, small


def _d_mix():
    return _ret_w() + CONV_W + _na_w()


_SMALL = ["c_ctx", "b_ada", "norm1_g", "ret_decay", "ret_gn_g", "conv_dw_w", "conv_dw_b", "conv_ln_g", "conv_ln_b",
          "conv_pw", "na_rpb", "norm2_g", "ffn_dw_w", "ffn_dw_b", "final_g"]
_SMALL_SHARD_AXIS = {"conv_dw_w": 2, "conv_pw": 1, "ffn_dw_w": 2}


def kernel(x, c, ctx, c_ctx, w_ada, b_ada, norm1_g, w_in, ret_decay, ret_gn_g, conv_dw_w, conv_dw_b, conv_ln_g, conv_ln_b, conv_pw, na_rpb, w_out, norm2_g, ffn_up, ffn_dw_w, ffn_dw_b, ffn_down, final_g, loss_target, m_c_ctx, m_w_ada, m_b_ada, m_norm1_g, m_w_in, m_ret_decay, m_ret_gn_g, m_conv_dw_w, m_conv_dw_b, m_conv_ln_g, m_conv_ln_b, m_conv_pw, m_na_rpb, m_w_out, m_norm2_g, m_ffn_up, m_ffn_dw_w, m_ffn_dw_b, m_ffn_down, m_final_g, v_c_ctx, v_w_ada, v_b_ada, v_norm1_g, v_w_in, v_ret_decay, v_ret_gn_g, v_conv_dw_w, v_conv_dw_b, v_conv_ln_g, v_conv_ln_b, v_conv_pw, v_na_rpb, v_w_out, v_norm2_g, v_ffn_up, v_ffn_dw_w, v_ffn_dw_b, v_ffn_down, v_final_g):
    d, nl = D_MODEL, DEPTH
    cs = 6 * d // N_DEV
    me = _my_index()
    weights = dict(c_ctx=c_ctx, w_ada=w_ada, b_ada=b_ada, norm1_g=norm1_g, w_in=w_in, ret_decay=ret_decay, ret_gn_g=ret_gn_g,
                   conv_dw_w=conv_dw_w, conv_dw_b=conv_dw_b, conv_ln_g=conv_ln_g, conv_ln_b=conv_ln_b, conv_pw=conv_pw,
                   na_rpb=na_rpb, w_out=w_out, norm2_g=norm2_g, ffn_up=ffn_up, ffn_dw_w=ffn_dw_w, ffn_dw_b=ffn_dw_b,
                   ffn_down=ffn_down, final_g=final_g)
    mom = dict(c_ctx=m_c_ctx, w_ada=m_w_ada, b_ada=m_b_ada, norm1_g=m_norm1_g, w_in=m_w_in, ret_decay=m_ret_decay,
               ret_gn_g=m_ret_gn_g, conv_dw_w=m_conv_dw_w, conv_dw_b=m_conv_dw_b, conv_ln_g=m_conv_ln_g,
               conv_ln_b=m_conv_ln_b, conv_pw=m_conv_pw, na_rpb=m_na_rpb, w_out=m_w_out, norm2_g=m_norm2_g,
               ffn_up=m_ffn_up, ffn_dw_w=m_ffn_dw_w, ffn_dw_b=m_ffn_dw_b, ffn_down=m_ffn_down, final_g=m_final_g)
    var = dict(c_ctx=v_c_ctx, w_ada=v_w_ada, b_ada=v_b_ada, norm1_g=v_norm1_g, w_in=v_w_in, ret_decay=v_ret_decay,
               ret_gn_g=v_ret_gn_g, conv_dw_w=v_conv_dw_w, conv_dw_b=v_conv_dw_b, conv_ln_g=v_conv_ln_g,
               conv_ln_b=v_conv_ln_b, conv_pw=v_conv_pw, na_rpb=v_na_rpb, w_out=v_w_out, norm2_g=v_norm2_g,
               ffn_up=v_ffn_up, ffn_dw_w=v_ffn_dw_w, ffn_dw_b=v_ffn_dw_b, ffn_down=v_ffn_down, final_g=v_final_g)

    big_names = ["w_in", "w_out", "ffn_up", "ffn_down"]
    shards = {(nm, l): _cast_bf16(weights[nm][l], f"cast_{nm}{l}") for l in range(nl) for nm in big_names}
    small_sharded = _pack([conv_dw_w, conv_pw, ffn_dw_w], SUBLANES)
    c_rows = jnp.pad(c, ((0, SUBLANES - 1), (0, 0)))
    gathered = _run_comm(_Gather([c_rows, small_sharded, shards[("w_in", 0)]]), "gather_first")
    c_all = gathered[0][:, 0, :]
    def whole(rows, shard_shape, axis):
        n_el = int(np.prod(shard_shape))
        parts = rows.reshape(N_DEV, -1)[:, :n_el].reshape((N_DEV,) + tuple(shard_shape))
        parts = jnp.moveaxis(parts, 0, axis)
        return parts.reshape(shard_shape[:axis] + (N_DEV * shard_shape[axis],) + shard_shape[axis + 1:])

    r0 = _pack_rows(conv_dw_w.shape)
    r1 = r0 + _pack_rows(conv_pw.shape)
    r2 = r1 + _pack_rows(ffn_dw_w.shape)
    full_conv_dw_w = whole(gathered[1][:, :r0], conv_dw_w.shape, 2)
    full_conv_pw = whole(gathered[1][:, r0:r1], conv_pw.shape, 1)
    full_ffn_dw_w = whole(gathered[1][:, r1:r2], ffn_dw_w.shape, 2)

    c16 = jnp.concatenate([c_all, jnp.broadcast_to(c_ctx[None, :], (N_DEV, d))], axis=0)
    b_shard = lax.dynamic_slice_in_dim(b_ada, me * cs, cs, axis=1)[:, None, :]
    m_shard = _ada_fwd(c16, w_ada, b_shard, "ada_fwd")
    m_all = _run_comm(_Gather([m_shard.reshape(nl * 16, cs)]), "gather_mod")[0]
    m_full = m_all.reshape(N_DEV, nl, 16, cs).transpose(1, 2, 0, 3).reshape(nl, 16, 6 * d)
    m_lat = lax.dynamic_index_in_dim(m_full, me, axis=1, keepdims=False)
    mod = jnp.stack([m_full[:, N_DEV], m_lat], axis=1).reshape(nl, 2, 6, 1, d)

    arriving, token = {}, m_all
    for l in range(nl):
        for nm in big_names:
            if (nm, l) != ("w_in", 0):
                arriving[(nm, l)], token = _split_start(shards[(nm, l)], True, f"gather_{nm}{l}", token)
    mod = _after(mod, token)

    cos, sin = _rope_tables()
    cst = dict(cos=cos, sin=sin, order=_chunk_order())
    layer_w = []
    for l in range(nl):
        layer_w.append(dict(
            norm1_g=norm1_g[l][None], norm2_g=norm2_g[l][None], ret_decay=ret_decay[l], ret_gn_g=ret_gn_g[l][None],
            conv_dw_w=full_conv_dw_w[l], conv_dw_b=conv_dw_b[l][None], conv_ln_g=conv_ln_g[l][None],
            conv_ln_b=conv_ln_b[l][None], conv_pw=full_conv_pw[l], rb=_rpb_rows(na_rpb[l]),
            ffn_dw_w=full_ffn_dw_w[l], ffn_dw_b=ffn_dw_b[l][None]))

    xs = jnp.concatenate([ctx[0], x[0]], axis=0)
    saved = []
    for l in range(nl):
        def arrived(nm, after, l=l):
            if (nm, l) == ("w_in", 0):
                return gathered[2]
            return _split_wait(arriving[(nm, l)], after, f"arrived_{nm}{l}")

        xs, sv = _layer_fwd(l, xs, mod[l], layer_w[l], cst, arrived)
        saved.append(sv)
    loss_tile, dxs, dfinal = _loss_head(xs, final_g[None], loss_target[0], "loss_head")
    loss = lax.psum(loss_tile[0, 0], ("x", "y", "c"))

    dmods, smalls = [None] * nl, [None] * nl
    leaving, last = {}, [loss_tile]

    def send(key, partial):
        leaving[key], token = _split_start(partial, False, "send_" + "_".join(str(k) for k in key), last[0])
        last[0] = token
        return token

    per_layer = [nm for nm in _SMALL if nm not in ("c_ctx", "b_ada", "final_g")]
    small_packs, small_arriving = [None] * nl, [None] * nl
    for l in reversed(range(nl)):
        dxs, dmods[l], smalls[l] = _layer_bwd(l, dxs, saved[l], mod[l], layer_w[l], cst, send)
        small_packs[l] = _pack([smalls[l][nm] for nm in per_layer], 512)
        if l > 0:
            small_arriving[l], last[0] = _split_start(small_packs[l], True, f"gather_small_grads{l}", last[0])
    grad_x = dxs[CTX_LEN:][None]

    arrived_grad = lambda key, after: _split_wait(leaving[key], after, "got_" + "_".join(str(k) for k in key))
    out_big = {}
    after = dxs
    for nm in ["ffn_down", "ffn_up", "w_out"]:
        out_big[nm] = _adamw(weights[nm], mom[nm], var[nm], [arrived_grad((nm, l), after) for l in range(nl)],
                                f"adamw_{nm}")
        after = out_big[nm][0]

    dm_mine = jnp.stack(dmods).reshape(nl * 2, 6 * d)
    dm_rows = jnp.pad(dm_mine, ((0, SUBLANES - nl * 2), (0, 0)))
    dm_all = _run_comm(_Gather([dm_rows]), "gather_dmod", after=after)[0][:, :nl * 2].reshape(N_DEV, nl, 2, 6 * d)
    dm16_full = jnp.concatenate([dm_all[:, :, 1].transpose(1, 0, 2), dm_all[:, :, 0].transpose(1, 0, 2)], axis=1)
    dm16 = lax.dynamic_slice_in_dim(dm16_full, me * cs, cs, axis=2)
    g_w_ada, dc16 = _ada_bwd(c16, dm16, w_ada, "ada_bwd")

    shared = dict(c_ctx=jnp.sum(dc16[:, N_DEV:], axis=(0, 1)),
                  b_ada=jnp.sum(jnp.stack(dmods).reshape(nl, 2, 6 * d), axis=1), final_g=dfinal[0])
    shared_all = _run_comm(_Gather([_pack(list(shared.values()), SUBLANES)]), "gather_shared_grads")[0]
    small_arriving[0], token = _split_start(small_packs[0], True, "gather_small_grads0", shared_all)

    out_big["w_ada"] = _adamw(w_ada, m_w_ada, v_w_ada, g_w_ada[:, None], "adamw_w_ada")
    halves = lambda a: a.reshape(2 * nl, d // 2, a.shape[2])
    res = _adamw(halves(w_in), halves(m_w_in), halves(v_w_in),
                    [arrived_grad(("w_in", l, i), token) for l in range(nl) for i in range(2)], "adamw_w_in")
    out_big["w_in"] = [r.reshape(w_in.shape) for r in res]

    g_small = dict(zip(shared, _unpack(_sum_devices(shared_all, "sum_shared_grads"), [v.shape for v in shared.values()])))
    per = []
    for l in range(nl):
        got = _split_wait(small_arriving[l], res[0], f"arrived_small_grads{l}")
        per.append(_unpack(_sum_devices(got, f"sum_small_grads{l}"), [smalls[l][nm].shape for nm in per_layer]))
    g_small.update({nm: jnp.stack([per[l][i] for l in range(nl)]) for i, nm in enumerate(per_layer)})
    for nm, ax in _SMALL_SHARD_AXIS.items():
        n_sh = weights[nm].shape[ax]
        g_small[nm] = lax.dynamic_slice_in_dim(g_small[nm], me * n_sh, n_sh, axis=ax)
    shapes_own = [weights[nm].shape for nm in _SMALL]
    pk = lambda src: _pack([src[nm] for nm in _SMALL], 2 * SUBLANES)[None]
    res_small = _adamw(pk(weights), pk(mom), pk(var), pk(g_small)[:, None], "adamw_small")
    out_small = [dict(zip(_SMALL, _unpack(r[0], shapes_own))) for r in res_small]

    names = ["c_ctx", "w_ada", "b_ada", "norm1_g", "w_in", "ret_decay", "ret_gn_g", "conv_dw_w", "conv_dw_b", "conv_ln_g",
             "conv_ln_b", "conv_pw", "na_rpb", "w_out", "norm2_g", "ffn_up", "ffn_dw_w", "ffn_dw_b", "ffn_down", "final_g"]
    outs = [loss, grad_x]
    for kind in range(4):
        for nm in names:
            outs.append(out_big[nm][kind] if nm in out_big else out_small[kind][nm])
    return tuple(outs)
```
